```python
import jax, jax.numpy as jnp
from jax import lax
import numpy as np

D_MODEL = 1024
BATCH = 8
SEQ = 2048
DEPTH = 4
DEC_BATCH = 8
DEC_SEQ = 16
PAST_LEN = 2048

CHUNK = 64
HEAD_DIM = 64
EPS = 1e-6
A_HEADS = 8
A_WIDTH = A_HEADS * HEAD_DIM
SB_BLOCK = 128
B_HEADS = 4
B_DK = 64
B_DV = 128
B_KW = B_HEADS * B_DK
B_VW = B_HEADS * B_DV
B_GATE_RANK = 16
B_GATE_TEMP = 16.0
GLA_CHUNK = 64
C_HEADS = 16
C_WIDTH = C_HEADS * HEAD_DIM
C_LEFT_CHUNKS = 8
C_WINDOW = C_LEFT_CHUNKS * CHUNK
REL_MIN = -(CHUNK - 1)
REL_MAX = 128
N_REL = REL_MAX - REL_MIN + 1
D_FF = -(-8 * D_MODEL // (3 * 256)) * 256
N_AB = (DEPTH + 1) // 2
N_C = DEPTH // 2
AB_IN = 3 * A_WIDTH + 2 * B_KW + B_VW + B_GATE_RANK + B_VW
AB_OUT = A_WIDTH + B_VW

kernel_name = "hybrid_stickbreak_gla_chunkband_stream_step"


def _rmsnorm(x, g):
    xf = x.astype(jnp.float32)
    y = xf * lax.rsqrt(jnp.mean(xf * xf, axis=-1, keepdims=True) + EPS)
    return (y * g.astype(jnp.float32)).astype(x.dtype)


def _split_cols(z, sizes):
    out, start = [], 0
    for s in sizes:
        out.append(z[..., start:start + s])
        start += s
    return out


def _swiglu(h, wg, wu, wd):
    return (jax.nn.silu(h @ wg) * (h @ wu)) @ wd


def _sb_block(q, k, v, q_pos, k_pos):
    z = jnp.einsum('bthd,bshd->bhts', q, k).astype(jnp.float32) * (HEAD_DIM ** -0.5)
    mask = k_pos[None, :] < q_pos[:, None]
    log_beta = jax.nn.log_sigmoid(z)
    log_keep = jnp.where(mask, jax.nn.log_sigmoid(-z), 0.0)
    rev = lax.cumsum(log_keep, axis=3, reverse=True)
    after = jnp.concatenate([rev[..., 1:], jnp.zeros_like(rev[..., :1])], axis=-1)
    w = jnp.where(mask, jnp.exp(log_beta + after), 0.0)
    return jnp.einsum('bhts,bshd->bthd', w.astype(v.dtype), v)


def _sb_prompt(q, k, v):
    B, S, H, d = q.shape
    nb = S // SB_BLOCK
    qb = jnp.moveaxis(q.reshape(B, nb, SB_BLOCK, H, d), 1, 0)
    pos = jnp.arange(S, dtype=jnp.int32)
    pb = pos.reshape(nb, SB_BLOCK)
    out = lax.map(lambda a: _sb_block(a[0], k, v, a[1], pos), (qb, pb))
    return jnp.moveaxis(out, 0, 1).reshape(B, S, H, d)


def _gla(q, k, v, log_a, s0, L):
    B, T, H, dk = q.shape
    n = T // L

    def chunks(t):
        return jnp.moveaxis(t.astype(jnp.float32).reshape(B, n, L, H, t.shape[-1]), 1, 0)

    qc = chunks(q) * (dk ** -0.5)
    kc, vc, gc = chunks(k), chunks(v), chunks(log_a)
    causal = jnp.tril(jnp.ones((L, L), dtype=bool))

    def step(S, inp):
        qi, ki, vi, gi = inp
        b = jnp.cumsum(gi, axis=1)
        b_last = b[:, -1]
        qg = qi * jnp.exp(b)
        kg = ki * jnp.exp(-b)
        att = jnp.where(causal, jnp.einsum('bthd,bshd->bhts', qg, kg), 0.0)
        o = jnp.einsum('bhts,bshv->bthv', att, vi) + jnp.einsum('bthd,bhdv->bthv', qg, S)
        kd = ki * jnp.exp(b_last[:, None] - b)
        S = jnp.exp(b_last)[..., None] * S + jnp.einsum('bshd,bshv->bhdv', kd, vi)
        return S, o

    S, o = lax.scan(step, s0.astype(jnp.float32), (qc, kc, vc, gc))
    return jnp.moveaxis(o, 0, 1).reshape(B, T, H, -1), S


def _ab_project(h, w_in, w_gate, b_gate):
    B, T, _ = h.shape
    z = h @ w_in
    qa, ka, va, qb, kb, vb, g_lr, r = _split_cols(
        z, (A_WIDTH, A_WIDTH, A_WIDTH, B_KW, B_KW, B_VW, B_GATE_RANK, B_VW))
    log_a = jax.nn.log_sigmoid((g_lr @ w_gate + b_gate).astype(jnp.float32)) / B_GATE_TEMP
    rs = lambda t, H: t.reshape(B, T, H, -1)
    return (rs(qa, A_HEADS), rs(ka, A_HEADS), rs(va, A_HEADS),
            rs(qb, B_HEADS), rs(kb, B_HEADS), rs(vb, B_HEADS), rs(log_a, B_HEADS), r)


def _ab_merge(o_a, o_b, r, g_gla, w_out):
    B, T = o_a.shape[:2]
    ob = o_b.astype(jnp.float32)
    ob = ob * lax.rsqrt(jnp.mean(ob * ob, axis=-1, keepdims=True) + EPS)
    ob = ob.reshape(B, T, B_VW) * g_gla.astype(jnp.float32) * jax.nn.silu(r.astype(jnp.float32))
    cat = jnp.concatenate([o_a.reshape(B, T, A_WIDTH).astype(jnp.float32), ob], axis=-1)
    return cat @ w_out


def _band_attn(q, k, v, q_pos, k_pos, rel_table):
    s = jnp.einsum('bthd,bshd->bhts', q, k).astype(jnp.float32) * (HEAD_DIM ** -0.5)
    qc = q_pos // CHUNK
    kc = k_pos // CHUNK
    mask = ((kc[None, :] <= qc[:, None]) & (kc[None, :] >= qc[:, None] - C_LEFT_CHUNKS)
            & (k_pos[None, :] >= 0))
    rel = jnp.clip(q_pos[:, None] - k_pos[None, :], REL_MIN, REL_MAX) - REL_MIN
    s = s + rel_table[:, rel].astype(jnp.float32)[None]
    p = jax.nn.softmax(jnp.where(mask, s, -jnp.inf), axis=-1)
    return jnp.einsum('bhts,bshd->bthd', p.astype(v.dtype), v)


def _band_prompt(q, k, v, rel_table):
    B, S, H, d = q.shape
    n = S // CHUNK
    band = C_WINDOW + CHUNK
    pad = ((0, 0), (C_WINDOW, 0), (0, 0), (0, 0))
    kp, vp = jnp.pad(k, pad), jnp.pad(v, pad)
    qb = jnp.moveaxis(q.reshape(B, n, CHUNK, H, d), 1, 0)

    def one(args):
        qi, c = args
        start = c * CHUNK
        ki = lax.dynamic_slice_in_dim(kp, start, band, axis=1)
        vi = lax.dynamic_slice_in_dim(vp, start, band, axis=1)
        q_pos = start + jnp.arange(CHUNK, dtype=jnp.int32)
        k_pos = start - C_WINDOW + jnp.arange(band, dtype=jnp.int32)
        return _band_attn(qi, ki, vi, q_pos, k_pos, rel_table)

    out = lax.map(one, (qb, jnp.arange(n, dtype=jnp.int32)))
    return jnp.moveaxis(out, 0, 1).reshape(B, S, H, d)


def setup_inputs(seed: int = 0) -> dict:
    key = jax.random.key(seed)
    ks = jax.random.split(key, 21)

    def nrm(k, shape, scale):
        return jax.random.normal(k, shape, jnp.float32) * scale

    c_rows = min(C_WINDOW, PAST_LEN)
    return {
        "x_prompt": nrm(ks[0], (BATCH, SEQ, D_MODEL), 1.0),
        "x_sample": nrm(ks[1], (DEC_BATCH, DEC_SEQ, D_MODEL), 1.0),
        "cache_a_k": nrm(ks[2], (N_AB, DEC_BATCH, PAST_LEN, A_HEADS, HEAD_DIM), 1.0),
        "cache_a_v": nrm(ks[3], (N_AB, DEC_BATCH, PAST_LEN, A_HEADS, HEAD_DIM), 1.0),
        "state_b": nrm(ks[4], (N_AB, DEC_BATCH, B_HEADS, B_DK, B_DV), 0.1),
        "cache_c_k": nrm(ks[5], (N_C, DEC_BATCH, c_rows, C_HEADS, HEAD_DIM), 1.0),
        "cache_c_v": nrm(ks[6], (N_C, DEC_BATCH, c_rows, C_HEADS, HEAD_DIM), 1.0),
        "norm_mix_g": 1.0 + nrm(ks[7], (DEPTH, D_MODEL), 0.01),
        "norm_ffn_g": 1.0 + nrm(ks[8], (DEPTH, D_MODEL), 0.01),
        "w_in_ab": nrm(ks[9], (N_AB, D_MODEL, AB_IN), D_MODEL ** -0.5),
        "w_gate_b": nrm(ks[10], (N_AB, B_GATE_RANK, B_KW), B_GATE_RANK ** -0.5),
        "b_gate_b": nrm(ks[11], (N_AB, B_KW), 0.1),
        "norm_gla_g": 1.0 + nrm(ks[12], (N_AB, B_VW), 0.01),
        "w_out_ab": nrm(ks[13], (N_AB, AB_OUT, D_MODEL), AB_OUT ** -0.5),
        "w_qkv_c": nrm(ks[14], (N_C, D_MODEL, 3 * C_WIDTH), D_MODEL ** -0.5),
        "rel_bias_c": nrm(ks[15], (N_C, C_HEADS, N_REL), 0.1),
        "w_out_c": nrm(ks[16], (N_C, C_WIDTH, D_MODEL), C_WIDTH ** -0.5),
        "w_ffn_gate": nrm(ks[17], (DEPTH, D_MODEL, D_FF), D_MODEL ** -0.5),
        "w_ffn_up": nrm(ks[18], (DEPTH, D_MODEL, D_FF), D_MODEL ** -0.5),
        "w_ffn_down": nrm(ks[19], (DEPTH, D_FF, D_MODEL), D_FF ** -0.5),
        "norm_final_g": 1.0 + nrm(ks[20], (D_MODEL,), 0.01),
    }


def reference(x_prompt, x_sample, cache_a_k, cache_a_v, state_b, cache_c_k, cache_c_v,
              norm_mix_g, norm_ffn_g, w_in_ab, w_gate_b, b_gate_b, norm_gla_g, w_out_ab,
              w_qkv_c, rel_bias_c, w_out_c, w_ffn_gate, w_ffn_up, w_ffn_down, norm_final_g):
    xp, xs = x_prompt, x_sample
    Bp, Tp, _ = xp.shape
    Bs, Ts, _ = xs.shape
    P = PAST_LEN
    q_pos_s = P + jnp.arange(Ts, dtype=jnp.int32)
    a_kp, a_vp, a_ks, a_vs, b_sp, b_ss = [], [], [], [], [], []
    c_kp, c_vp, c_ks, c_vs = [], [], [], []

    for layer in range(DEPTH):
        i = layer // 2
        hp = _rmsnorm(xp, norm_mix_g[layer])
        hs = _rmsnorm(xs, norm_mix_g[layer])
        if layer % 2 == 0:
            qa, ka, va, qb, kb, vb, la, r = _ab_project(hp, w_in_ab[i], w_gate_b[i], b_gate_b[i])
            oa = _sb_prompt(qa, ka, va)
            s0 = jnp.zeros((Bp, B_HEADS, B_DK, B_DV), jnp.float32)
            ob, sbp = _gla(qb, kb, vb, la, s0, GLA_CHUNK)
            xp = xp + _ab_merge(oa, ob, r, norm_gla_g[i], w_out_ab[i]).astype(xp.dtype)

            qa2, ka2, va2, qb2, kb2, vb2, la2, r2 = _ab_project(hs, w_in_ab[i], w_gate_b[i], b_gate_b[i])
            k_all = jnp.concatenate([cache_a_k[i].astype(ka2.dtype), ka2], axis=1)
            v_all = jnp.concatenate([cache_a_v[i].astype(va2.dtype), va2], axis=1)
            oa2 = _sb_block(qa2, k_all, v_all, q_pos_s, jnp.arange(P + Ts, dtype=jnp.int32))
            ob2, sbs = _gla(qb2, kb2, vb2, la2, state_b[i], Ts)
            xs = xs + _ab_merge(oa2, ob2, r2, norm_gla_g[i], w_out_ab[i]).astype(xs.dtype)

            a_kp.append(ka); a_vp.append(va); a_ks.append(ka2); a_vs.append(va2)
            b_sp.append(sbp); b_ss.append(sbs)
        else:
            qc, kc, vc = [t.reshape(Bp, Tp, C_HEADS, HEAD_DIM)
                          for t in _split_cols(hp @ w_qkv_c[i], (C_WIDTH, C_WIDTH, C_WIDTH))]
            oc = _band_prompt(qc, kc, vc, rel_bias_c[i])
            xp = xp + (oc.reshape(Bp, Tp, C_WIDTH) @ w_out_c[i]).astype(xp.dtype)

            qc2, kc2, vc2 = [t.reshape(Bs, Ts, C_HEADS, HEAD_DIM)
                             for t in _split_cols(hs @ w_qkv_c[i], (C_WIDTH, C_WIDTH, C_WIDTH))]
            Wc = cache_c_k.shape[2]
            k_all = jnp.concatenate([cache_c_k[i].astype(kc2.dtype), kc2], axis=1)
            v_all = jnp.concatenate([cache_c_v[i].astype(vc2.dtype), vc2], axis=1)
            k_pos = P - Wc + jnp.arange(Wc + Ts, dtype=jnp.int32)
            oc2 = _band_attn(qc2, k_all, v_all, q_pos_s, k_pos, rel_bias_c[i])
            xs = xs + (oc2.reshape(Bs, Ts, C_WIDTH) @ w_out_c[i]).astype(xs.dtype)

            keep = min(C_WINDOW, Tp)
            c_kp.append(kc[:, Tp - keep:]); c_vp.append(vc[:, Tp - keep:])
            c_ks.append(kc2); c_vs.append(vc2)
        xp = xp + _swiglu(_rmsnorm(xp, norm_ffn_g[layer]), w_ffn_gate[layer], w_ffn_up[layer],
                          w_ffn_down[layer]).astype(xp.dtype)
        xs = xs + _swiglu(_rmsnorm(xs, norm_ffn_g[layer]), w_ffn_gate[layer], w_ffn_up[layer],
                          w_ffn_down[layer]).astype(xs.dtype)

    y_prompt = _rmsnorm(xp, norm_final_g)
    y_sample = _rmsnorm(xs, norm_final_g)
    a_k_prompt = jnp.stack(a_kp)
    a_v_prompt = jnp.stack(a_vp)
    a_k_sample = jnp.stack(a_ks)
    a_v_sample = jnp.stack(a_vs)
    b_state_prompt = jnp.stack(b_sp)
    b_state_sample = jnp.stack(b_ss)
    c_k_prompt = jnp.stack(c_kp)
    c_v_prompt = jnp.stack(c_vp)
    c_k_sample = jnp.stack(c_ks)
    c_v_sample = jnp.stack(c_vs)
    return (y_prompt, y_sample, a_k_prompt, a_v_prompt, a_k_sample, a_v_sample,
            b_state_prompt, b_state_sample, c_k_prompt, c_v_prompt, c_k_sample, c_v_sample)
```

```python
import functools

import jax
import jax.numpy as jnp
from jax import lax
from jax.experimental import pallas as pl
from jax.experimental.pallas import tpu as pltpu

F32 = jnp.float32
BF16 = jnp.bfloat16

EPS = 1e-6
HEAD_DIM = 64
LANES = 128
A_HEADS = 8
A_WIDTH = A_HEADS * HEAD_DIM
B_HEADS = 4
B_DK = 64
B_DV = 128
B_KW = B_HEADS * B_DK
B_VW = B_HEADS * B_DV
B_GATE_RANK = 16
B_GATE_TEMP = 16.0
GLA_CHUNK = 64
C_HEADS = 16
C_WIDTH = C_HEADS * HEAD_DIM
CHUNK = 64
C_LEFT_CHUNKS = 8
REL_MIN = -(CHUNK - 1)
REL_MAX = 128
ATT_BLOCK = 256
NEG_BIG = -1e30
VMEM_LIMIT = 56 * 1024 * 1024


def _params(*sem):
    return pltpu.CompilerParams(dimension_semantics=sem, vmem_limit_bytes=VMEM_LIMIT)


def _resident(shape):
    return pl.BlockSpec(shape, lambda *_: (0,) * len(shape), pipeline_mode=pl.Buffered(1))


def _rms(x, g):
    return x * lax.rsqrt(jnp.mean(x * x, axis=-1, keepdims=True) + EPS) * g


def _log_sigmoid_pair(z):
    l = jnp.log1p(jnp.exp(-jnp.abs(z)))
    return jnp.minimum(z, 0.0) - l, jnp.minimum(-z, 0.0) - l


def _split_bf16(x):
    hi = x.astype(BF16)
    lo = (x - hi.astype(F32)).astype(BF16)
    return hi, lo


def _dot(a, b):
    return jnp.dot(a, b, preferred_element_type=F32)


def _dot_nt(a, b):
    return lax.dot_general(a, b, (((1,), (1,)), ((), ())), preferred_element_type=F32)


def _dot_tn(a, b):
    return lax.dot_general(a, b, (((0,), (0,)), ((), ())), preferred_element_type=F32)


def _proj_ab_kernel(x_ref, g_ref, w_ref, wlr_ref, wgate_ref, bgate_ref,
                    qa_ref, ka_ref, va_ref, kab_ref, vab_ref, qb_ref, kb_ref, vb_ref, r_ref, la_ref):
    y = _rms(x_ref[...], g_ref[...]).astype(BF16)
    z = _dot(y, w_ref[...])
    c = 0
    qa_ref[...] = (z[:, c:c + A_WIDTH] * (HEAD_DIM ** -0.5)).astype(BF16); c += A_WIDTH
    ka = z[:, c:c + A_WIDTH]; c += A_WIDTH
    va = z[:, c:c + A_WIDTH]; c += A_WIDTH
    ka_ref[...] = ka
    va_ref[...] = va
    kab_ref[...] = ka.astype(BF16)
    vab_ref[...] = va.astype(BF16)
    qb_ref[...] = z[:, c:c + B_KW]; c += B_KW
    kb_ref[...] = z[:, c:c + B_KW]; c += B_KW
    vb_ref[...] = z[:, c:c + B_VW]; c += B_VW
    r_ref[...] = z[:, c:c + B_VW]
    g_lr = _dot(y, wlr_ref[...])
    gate = _dot(g_lr.astype(BF16), wgate_ref[...]) + bgate_ref[...]
    la_ref[...] = _log_sigmoid_pair(gate)[0] * (1.0 / B_GATE_TEMP)


def _proj_ab(x, g, w_main, w_lr, w_gate, b_gate, tm):
    m, d = x.shape
    row = lambda n: pl.BlockSpec((tm, n), lambda i: (i, 0))
    outs = [(A_WIDTH, BF16), (A_WIDTH, F32), (A_WIDTH, F32), (A_WIDTH, BF16), (A_WIDTH, BF16),
            (B_KW, F32), (B_KW, F32), (B_VW, F32), (B_VW, F32), (B_KW, F32)]
    return pl.pallas_call(
        _proj_ab_kernel,
        grid=(m // tm,),
        in_specs=[row(d), _resident(g.shape), _resident(w_main.shape), _resident(w_lr.shape),
                  _resident(w_gate.shape), _resident(b_gate.shape)],
        out_specs=[row(n) for n, _ in outs],
        out_shape=[jax.ShapeDtypeStruct((m, n), dt) for n, dt in outs],
        compiler_params=_params("parallel"),
        name="proj_ab",
    )(x, g, w_main, w_lr, w_gate, b_gate)


def _proj_c_kernel(x_ref, g_ref, w_ref, q_ref, k_ref, v_ref, kb_ref, vb_ref):
    y = _rms(x_ref[...], g_ref[...]).astype(BF16)
    z = _dot(y, w_ref[...])
    q_ref[...] = (z[:, :C_WIDTH] * (HEAD_DIM ** -0.5)).astype(BF16)
    k = z[:, C_WIDTH:2 * C_WIDTH]
    v = z[:, 2 * C_WIDTH:]
    k_ref[...] = k
    v_ref[...] = v
    kb_ref[...] = k.astype(BF16)
    vb_ref[...] = v.astype(BF16)


def _proj_c(x, g, w, tm):
    m, d = x.shape
    row = lambda n: pl.BlockSpec((tm, n), lambda i: (i, 0))
    dts = [BF16, F32, F32, BF16, BF16]
    return pl.pallas_call(
        _proj_c_kernel,
        grid=(m // tm,),
        in_specs=[row(d), _resident(g.shape), _resident(w.shape)],
        out_specs=[row(C_WIDTH) for _ in dts],
        out_shape=[jax.ShapeDtypeStruct((m, C_WIDTH), dt) for dt in dts],
        compiler_params=_params("parallel"),
        name="proj_c",
    )(x, g, w)


def _merge_ab_kernel(x_ref, oa_ref, ob_ref, r_ref, g_ref, w_ref, o_ref):
    ob = ob_ref[...]
    parts = []
    for h in range(B_HEADS):
        seg = ob[:, h * B_DV:(h + 1) * B_DV]
        parts.append(seg * lax.rsqrt(jnp.mean(seg * seg, axis=-1, keepdims=True) + EPS))
    r = r_ref[...]
    obn = jnp.concatenate(parts, axis=-1) * g_ref[...] * (r * jax.nn.sigmoid(r))
    o_ref[...] = (x_ref[...] + _dot(oa_ref[...], w_ref[:A_WIDTH, :])
                  + _dot(obn.astype(BF16), w_ref[A_WIDTH:, :]))


def _merge_ab(x, oa, ob, r, g_gla, w_out, tm):
    m, d = x.shape
    row = lambda n: pl.BlockSpec((tm, n), lambda i: (i, 0))
    return pl.pallas_call(
        _merge_ab_kernel,
        grid=(m // tm,),
        in_specs=[row(d), row(A_WIDTH), row(B_VW), row(B_VW), _resident(g_gla.shape),
                  _resident(w_out.shape)],
        out_specs=row(d),
        out_shape=jax.ShapeDtypeStruct((m, d), F32),
        compiler_params=_params("parallel"),
        name="merge_ab",
    )(x, oa, ob, r, g_gla, w_out)


def _out_c_kernel(x_ref, o_ref_in, w_ref, o_ref):
    o_ref[...] = x_ref[...] + _dot(o_ref_in[...], w_ref[...])


def _out_c(x, oc, w_out, tm):
    m, d = x.shape
    row = lambda n: pl.BlockSpec((tm, n), lambda i: (i, 0))
    return pl.pallas_call(
        _out_c_kernel,
        grid=(m // tm,),
        in_specs=[row(d), row(C_WIDTH), _resident(w_out.shape)],
        out_specs=row(d),
        out_shape=jax.ShapeDtypeStruct((m, d), F32),
        compiler_params=_params("parallel"),
        name="out_c",
    )(x, oc, w_out)


def _ffn_kernel(x_ref, g_ref, wg_ref, wu_ref, wd_ref, o_ref):
    x = x_ref[...]
    y = _rms(x, g_ref[...]).astype(BF16)
    h = _dot(y, wg_ref[...])
    u = _dot(y, wu_ref[...])
    a = (h * jax.nn.sigmoid(h) * u).astype(BF16)
    o_ref[...] = x + _dot(a, wd_ref[...])


def _ffn(x, g, wg, wu, wd, tm):
    m, d = x.shape
    row = pl.BlockSpec((tm, d), lambda i: (i, 0))
    return pl.pallas_call(
        _ffn_kernel,
        grid=(m // tm,),
        in_specs=[row, _resident(g.shape), _resident(wg.shape), _resident(wu.shape),
                  _resident(wd.shape)],
        out_specs=row,
        out_shape=jax.ShapeDtypeStruct((m, d), F32),
        compiler_params=_params("parallel"),
        name="ffn",
    )(x, g, wg, wu, wd)


def _final_norm_kernel(x_ref, g_ref, o_ref):
    o_ref[...] = _rms(x_ref[...], g_ref[...])


def _final_norm(x, g, tm):
    m, d = x.shape
    row = pl.BlockSpec((tm, d), lambda i: (i, 0))
    return pl.pallas_call(
        _final_norm_kernel,
        grid=(m // tm,),
        in_specs=[row, _resident(g.shape)],
        out_specs=row,
        out_shape=jax.ShapeDtypeStruct((m, d), F32),
        compiler_params=_params("parallel"),
        name="final_norm",
    )(x, g)


def _sb_kernel(q_ref, k_ref, v_ref, o_ref, acc_ref, c_ref, *, tq, tk, q_off, n_kblocks):
    qi = pl.program_id(2)
    q = q_ref[...]
    lane = lax.broadcasted_iota(jnp.int32, (tq, LANES), 1)
    lo = lane < HEAD_DIM
    q_heads = (jnp.where(lo, q, jnp.zeros_like(q)), jnp.where(lo, jnp.zeros_like(q), q))
    q_pos = q_off + qi * tq + lax.broadcasted_iota(jnp.int32, (tq, tk), 0)
    col = lax.broadcasted_iota(jnp.int32, (tq, tk), 1)
    later = (lax.broadcasted_iota(jnp.int32, (tk, tk), 0)
             > lax.broadcasted_iota(jnp.int32, (tk, tk), 1)).astype(BF16)

    acc_ref[...] = jnp.zeros_like(acc_ref)
    c_ref[...] = jnp.zeros_like(c_ref)
    nkb = jnp.minimum(n_kblocks, (q_off + (qi + 1) * tq - 2) // tk + 1)

    def body(n, carry):
        ks = pl.multiple_of((nkb - 1 - n) * tk, tk)
        k = k_ref[pl.ds(ks, tk), :]
        v = v_ref[pl.ds(ks, tk), :]
        mask = (ks + col) < q_pos
        pv = []
        for h in range(2):
            z = _dot_nt(q_heads[h], k)
            log_beta, log_keep = _log_sigmoid_pair(z)
            log_keep = jnp.where(mask, log_keep, 0.0)
            hi, lo_part = _split_bf16(log_keep)
            after = _dot(hi, later) + _dot(lo_part, later)
            c = c_ref[h]
            w = jnp.where(mask, jnp.exp(log_beta + after + c), 0.0)
            pv.append(_dot(w.astype(BF16), v))
            c_ref[h] = c + jnp.sum(log_keep, axis=-1, keepdims=True)
        acc_ref[...] += jnp.where(lo, pv[0], pv[1])
        return carry

    lax.fori_loop(0, nkb, body, 0)
    o_ref[...] = acc_ref[...].astype(o_ref.dtype)


def _sb_attention(q, k, v, q_off):
    b, tq_total, _ = q.shape
    s = k.shape[1]
    tk = ATT_BLOCK
    tq = min(ATT_BLOCK, tq_total)
    kern = functools.partial(_sb_kernel, tq=tq, tk=tk, q_off=q_off, n_kblocks=s // tk)
    kv_spec = pl.BlockSpec((None, s, LANES), lambda bi, hp, qi: (bi, 0, hp))
    q_spec = pl.BlockSpec((None, tq, LANES), lambda bi, hp, qi: (bi, qi, hp))
    return pl.pallas_call(
        kern,
        grid=(b, A_WIDTH // LANES, tq_total // tq),
        in_specs=[q_spec, kv_spec, kv_spec],
        out_specs=q_spec,
        out_shape=jax.ShapeDtypeStruct(q.shape, BF16),
        scratch_shapes=[pltpu.VMEM((tq, LANES), F32), pltpu.VMEM((2, tq, 1), F32)],
        compiler_params=_params("parallel", "parallel", "parallel"),
        name="sb_attention",
    )(q, k, v)


def _band_kernel(q_ref, k_ref, v_ref, bias_ref, o_ref, acc_ref, m_ref, l_ref,
                 *, tq, tk, q_off, k_off, s_valid):
    qi = pl.program_id(2)
    q = q_ref[...]
    lane = lax.broadcasted_iota(jnp.int32, (tq, LANES), 1)
    lo = lane < HEAD_DIM
    q_heads = (jnp.where(lo, q, jnp.zeros_like(q)), jnp.where(lo, jnp.zeros_like(q), q))
    q_start = q_off + qi * tq
    q_chunk = (q_start + lax.broadcasted_iota(jnp.int32, (tq, tk), 0)) // CHUNK
    col = lax.broadcasted_iota(jnp.int32, (tq, tk), 1)
    jq = (q_start - k_off) // tk

    acc_ref[...] = jnp.zeros_like(acc_ref)
    m_ref[...] = jnp.full_like(m_ref, NEG_BIG)
    l_ref[...] = jnp.zeros_like(l_ref)

    for dj in range(3):
        j = jq - dj

        @pl.when(j >= 0)
        def _():
            ks = pl.multiple_of(j * tk, tk)
            k = k_ref[pl.ds(ks, tk), :]
            v = v_ref[pl.ds(ks, tk), :]
            k_idx = ks + col
            k_chunk = (k_off + k_idx) // CHUNK
            mask = (k_chunk <= q_chunk) & (k_chunk >= q_chunk - C_LEFT_CHUNKS) & (k_idx < s_valid)
            pv, alphas = [], []
            for h in range(2):
                z = _dot_nt(q_heads[h], k) + bias_ref[h, dj, :tq, :]
                z = jnp.where(mask, z, NEG_BIG)
                m_old = m_ref[h]
                m_new = jnp.maximum(m_old, jnp.max(z, axis=-1, keepdims=True))
                alpha = jnp.exp(m_old - m_new)
                p = jnp.where(mask, jnp.exp(z - m_new), 0.0)
                l_ref[h] = alpha * l_ref[h] + jnp.sum(p, axis=-1, keepdims=True)
                m_ref[h] = m_new
                pv.append(_dot(p.astype(BF16), v))
                alphas.append(alpha)
            acc_ref[...] = (acc_ref[...] * jnp.where(lo, alphas[0], alphas[1])
                            + jnp.where(lo, pv[0], pv[1]))

    inv = jnp.where(lo, 1.0 / l_ref[0], 1.0 / l_ref[1])
    o_ref[...] = (acc_ref[...] * inv).astype(o_ref.dtype)


def _band_attention(q, k, v, bias, q_off, k_off, s_valid):
    b, tq_total, _ = q.shape
    s = k.shape[1]
    tk = ATT_BLOCK
    tq = min(ATT_BLOCK, tq_total)
    assert (q_off - k_off) % tk == 0 and s % tk == 0
    kern = functools.partial(_band_kernel, tq=tq, tk=tk, q_off=q_off, k_off=k_off, s_valid=s_valid)
    kv_spec = pl.BlockSpec((None, s, LANES), lambda bi, hp, qi: (bi, 0, hp))
    q_spec = pl.BlockSpec((None, tq, LANES), lambda bi, hp, qi: (bi, qi, hp))
    bias_spec = pl.BlockSpec((2, 3, ATT_BLOCK, ATT_BLOCK), lambda bi, hp, qi: (hp, 0, 0, 0))
    return pl.pallas_call(
        kern,
        grid=(b, C_WIDTH // LANES, tq_total // tq),
        in_specs=[q_spec, kv_spec, kv_spec, bias_spec],
        out_specs=q_spec,
        out_shape=jax.ShapeDtypeStruct(q.shape, BF16),
        scratch_shapes=[pltpu.VMEM((tq, LANES), F32), pltpu.VMEM((2, tq, 1), F32),
                        pltpu.VMEM((2, tq, 1), F32)],
        compiler_params=_params("parallel", "parallel", "parallel"),
        name="band_attention",
    )(q, k, v, bias)


def _band_bias(rel_table):
    t = jnp.arange(ATT_BLOCK, dtype=jnp.int32)
    d = (jnp.arange(3, dtype=jnp.int32)[:, None, None] * ATT_BLOCK + t[None, :, None] - t[None, None, :])
    return rel_table[:, jnp.clip(d, REL_MIN, REL_MAX) - REL_MIN].astype(F32)


def _gla_kernel(q_ref, k_ref, v_ref, la_ref, s0_ref, o_ref, s_out_ref, st_ref, *, n_chunks):
    L = GLA_CHUNK
    row = lax.broadcasted_iota(jnp.int32, (L, L), 0)
    colm = lax.broadcasted_iota(jnp.int32, (L, L), 1)
    tri = (colm <= row).astype(BF16)
    causal = colm <= row
    lane = lax.broadcasted_iota(jnp.int32, (L, LANES), 1)
    sub = lax.broadcasted_iota(jnp.int32, (LANES, B_DV), 0)
    zeros_state = jnp.zeros((B_DK, B_DV), F32)
    st_ref[0] = jnp.concatenate([s0_ref[0], zeros_state], axis=0)
    st_ref[1] = jnp.concatenate([zeros_state, s0_ref[1]], axis=0)

    def body(c, carry):
        rows = pl.ds(pl.multiple_of(c * L, L), L)
        q = q_ref[rows, :] * (B_DK ** -0.5)
        k = k_ref[rows, :]
        g_hi, g_lo = _split_bf16(la_ref[rows, :])
        b = _dot(tri, g_hi) + _dot(tri, g_lo)
        qg = q * jnp.exp(b)
        kg = (k * jnp.exp(-b)).astype(BF16)
        b_t = b.T
        b_last = b_t[:, L - 1:L]
        kd_t = k.T * jnp.exp(b_last - b_t)
        decay = jnp.exp(b_last)
        for h in range(2):
            mine = (lane >= h * B_DK) & (lane < (h + 1) * B_DK)
            qg_h = jnp.where(mine, qg, 0.0).astype(BF16)
            att = jnp.where(causal, _dot_nt(qg_h, kg), 0.0)
            v_h = v_ref[rows, h * B_DV:(h + 1) * B_DV].astype(BF16)
            s_h = st_ref[h]
            o_ref[rows, h * B_DV:(h + 1) * B_DV] = _dot(att.astype(BF16), v_h) + _dot(qg_h, s_h.astype(BF16))
            upd = _dot(kd_t.astype(BF16), v_h)
            mine_rows = (sub >= h * B_DK) & (sub < (h + 1) * B_DK)
            st_ref[h] = decay * s_h + jnp.where(mine_rows, upd, 0.0)
        return carry

    lax.fori_loop(0, n_chunks, body, 0)
    s_out_ref[0] = st_ref[0][:B_DK, :]
    s_out_ref[1] = st_ref[1][B_DK:, :]


def _gla(q, k, v, la, s0):
    b, t, _ = q.shape
    qk_spec = pl.BlockSpec((None, t, LANES), lambda bi, p: (bi, 0, p))
    v_spec = pl.BlockSpec((None, t, 2 * B_DV), lambda bi, p: (bi, 0, p))
    s_spec = pl.BlockSpec((None, 2, B_DK, B_DV), lambda bi, p: (bi, p, 0, 0))
    return pl.pallas_call(
        functools.partial(_gla_kernel, n_chunks=t // GLA_CHUNK),
        grid=(b, B_HEADS // 2),
        in_specs=[qk_spec, qk_spec, v_spec, qk_spec, s_spec],
        out_specs=[v_spec, s_spec],
        out_shape=[jax.ShapeDtypeStruct(v.shape, F32), jax.ShapeDtypeStruct(s0.shape, F32)],
        scratch_shapes=[pltpu.VMEM((2, LANES, B_DV), F32)],
        compiler_params=_params("parallel", "parallel"),
        name="gla",
    )(q, k, v, la, s0)


def _pad_rows(x, n):
    return jnp.pad(x, ((0, 0), (0, n - x.shape[1]), (0, 0)))


def _row_tile(m):
    for tm in (512, 256, 128, 64, 32, 16, 8):
        if m % tm == 0:
            return tm
    raise ValueError(f"token count {m} is not a multiple of 8")


def kernel(x_prompt, x_sample, cache_a_k, cache_a_v, state_b, cache_c_k, cache_c_v, norm_mix_g, norm_ffn_g, w_in_ab, w_gate_b, b_gate_b, norm_gla_g, w_out_ab, w_qkv_c, rel_bias_c, w_out_c, w_ffn_gate, w_ffn_up, w_ffn_down, norm_final_g):
    bp, tp, d = x_prompt.shape
    bs, ts, _ = x_sample.shape
    depth = norm_mix_g.shape[0]
    past = cache_a_k.shape[2]
    wc = cache_c_k.shape[2]
    assert tp % ATT_BLOCK == 0 and past % ATT_BLOCK == 0 and wc % ATT_BLOCK == 0
    assert ts <= GLA_CHUNK and ts % 8 == 0
    mp, ms = bp * tp, bs * ts
    tmp, tms = _row_tile(mp), _row_tile(ms)
    xp = x_prompt.reshape(mp, d)
    xs = x_sample.reshape(ms, d)
    row2 = lambda a: a.reshape(1, -1)

    a_kp, a_vp, a_ks, a_vs, b_sp, b_ss = [], [], [], [], [], []
    c_kp, c_vp, c_ks, c_vs = [], [], [], []

    for layer in range(depth):
        i = layer // 2
        g_mix = row2(norm_mix_g[layer])
        if layer % 2 == 0:
            w = w_in_ab[i]
            o = 3 * A_WIDTH + 2 * B_KW + B_VW
            w_main = jnp.concatenate([w[:, :o], w[:, o + B_GATE_RANK:]], axis=1).astype(BF16)
            w_lr = jnp.pad(w[:, o:o + B_GATE_RANK], ((0, 0), (0, LANES - B_GATE_RANK))).astype(BF16)
            w_gate = jnp.pad(w_gate_b[i], ((0, LANES - B_GATE_RANK), (0, 0))).astype(BF16)
            b_gate = row2(b_gate_b[i])
            g_gla = row2(norm_gla_g[i])
            w_out = w_out_ab[i].astype(BF16)

            qa, ka, va, kab, vab, qb, kb, vb, r, la = _proj_ab(xp, g_mix, w_main, w_lr, w_gate, b_gate, tmp)
            sh = lambda a: a.reshape(bp, tp, -1)
            oa = _sb_attention(sh(qa), sh(kab), sh(vab), 0)
            s0 = jnp.zeros((bp, B_HEADS, B_DK, B_DV), F32)
            ob, sbp = _gla(sh(qb), sh(kb), sh(vb), sh(la), s0)
            xp = _merge_ab(xp, oa.reshape(mp, -1), ob.reshape(mp, -1), r, g_gla, w_out, tmp)
            a_kp.append(ka.reshape(bp, tp, A_HEADS, HEAD_DIM))
            a_vp.append(va.reshape(bp, tp, A_HEADS, HEAD_DIM))
            b_sp.append(sbp)

            qa, ka, va, kab, vab, qb, kb, vb, r, la = _proj_ab(xs, g_mix, w_main, w_lr, w_gate, b_gate, tms)
            sh = lambda a: a.reshape(bs, ts, -1)
            s_all = past + ts
            s_pad = -(-s_all // ATT_BLOCK) * ATT_BLOCK
            k_all = _pad_rows(jnp.concatenate(
                [cache_a_k[i].reshape(bs, past, A_WIDTH).astype(BF16), sh(kab)], axis=1), s_pad)
            v_all = _pad_rows(jnp.concatenate(
                [cache_a_v[i].reshape(bs, past, A_WIDTH).astype(BF16), sh(vab)], axis=1), s_pad)
            oa = _sb_attention(sh(qa), k_all, v_all, past)
            pad_t = lambda a: _pad_rows(sh(a), GLA_CHUNK)
            ob, sbs = _gla(pad_t(qb), pad_t(kb), pad_t(vb), pad_t(la), state_b[i])
            xs = _merge_ab(xs, oa.reshape(ms, -1), ob[:, :ts].reshape(ms, -1), r, g_gla, w_out, tms)
            a_ks.append(ka.reshape(bs, ts, A_HEADS, HEAD_DIM))
            a_vs.append(va.reshape(bs, ts, A_HEADS, HEAD_DIM))
            b_ss.append(sbs)
        else:
            w_qkv = w_qkv_c[i].astype(BF16)
            w_out = w_out_c[i].astype(BF16)
            bias = _band_bias(rel_bias_c[i])

            q, k, v, kb16, vb16 = _proj_c(xp, g_mix, w_qkv, tmp)
            sh = lambda a: a.reshape(bp, tp, -1)
            oc = _band_attention(sh(q), sh(kb16), sh(vb16), bias, 0, 0, tp)
            xp = _out_c(xp, oc.reshape(mp, -1), w_out, tmp)
            keep = min(C_LEFT_CHUNKS * CHUNK, tp)
            c_kp.append(k.reshape(bp, tp, C_HEADS, HEAD_DIM)[:, tp - keep:])
            c_vp.append(v.reshape(bp, tp, C_HEADS, HEAD_DIM)[:, tp - keep:])

            q, k, v, kb16, vb16 = _proj_c(xs, g_mix, w_qkv, tms)
            sh = lambda a: a.reshape(bs, ts, -1)
            s_all = wc + ts
            s_pad = -(-s_all // ATT_BLOCK) * ATT_BLOCK
            k_all = _pad_rows(jnp.concatenate(
                [cache_c_k[i].reshape(bs, wc, C_WIDTH).astype(BF16), sh(kb16)], axis=1), s_pad)
            v_all = _pad_rows(jnp.concatenate(
                [cache_c_v[i].reshape(bs, wc, C_WIDTH).astype(BF16), sh(vb16)], axis=1), s_pad)
            oc = _band_attention(sh(q), k_all, v_all, bias, past, past - wc, s_all)
            xs = _out_c(xs, oc.reshape(ms, -1), w_out, tms)
            c_ks.append(k.reshape(bs, ts, C_HEADS, HEAD_DIM))
            c_vs.append(v.reshape(bs, ts, C_HEADS, HEAD_DIM))

        g_ffn = row2(norm_ffn_g[layer])
        wg = w_ffn_gate[layer].astype(BF16)
        wu = w_ffn_up[layer].astype(BF16)
        wd = w_ffn_down[layer].astype(BF16)
        xp = _ffn(xp, g_ffn, wg, wu, wd, tmp)
        xs = _ffn(xs, g_ffn, wg, wu, wd, tms)

    g_fin = row2(norm_final_g)
    y_prompt = _final_norm(xp, g_fin, tmp).reshape(bp, tp, d)
    y_sample = _final_norm(xs, g_fin, tms).reshape(bs, ts, d)
    return (y_prompt, y_sample, jnp.stack(a_kp), jnp.stack(a_vp), jnp.stack(a_ks), jnp.stack(a_vs),
            jnp.stack(b_sp), jnp.stack(b_ss), jnp.stack(c_kp), jnp.stack(c_vp),
            jnp.stack(c_ks), jnp.stack(c_vs))
```

```python
import functools

import jax
import jax.numpy as jnp
from jax import lax
from jax.experimental import pallas as pl
from jax.experimental.pallas import tpu as pltpu

F32 = jnp.float32
BF16 = jnp.bfloat16

EPS = 1e-6
HEAD_DIM = 64
LANES = 128
A_HEADS = 8
A_WIDTH = A_HEADS * HEAD_DIM
B_HEADS = 4
B_DK = 64
B_DV = 128
B_KW = B_HEADS * B_DK
B_VW = B_HEADS * B_DV
B_GATE_RANK = 16
B_GATE_TEMP = 16.0
GLA_CHUNK = 64
C_HEADS = 16
C_WIDTH = C_HEADS * HEAD_DIM
CHUNK = 64
C_LEFT_CHUNKS = 8
REL_MIN = -(CHUNK - 1)
REL_MAX = 128
ATT_BLOCK = 256
NEG_BIG = -1e30
SB_DEAD = 104.0
SB_PAIRS_PER_STEP = 2
VMEM_LIMIT = 56 * 1024 * 1024


def _params(*sem):
    return pltpu.CompilerParams(dimension_semantics=sem, vmem_limit_bytes=VMEM_LIMIT)


def _resident(shape):
    return pl.BlockSpec(shape, lambda *_: (0,) * len(shape), pipeline_mode=pl.Buffered(1))


def _rms(x, g):
    return x * lax.rsqrt(jnp.mean(x * x, axis=-1, keepdims=True) + EPS) * g


def _log_sigmoid_pair(z):
    l = jnp.log1p(jnp.exp(-jnp.abs(z)))
    return jnp.minimum(z, 0.0) - l, jnp.minimum(-z, 0.0) - l


def _split_bf16(x):
    hi = x.astype(BF16)
    lo = (x - hi.astype(F32)).astype(BF16)
    return hi, lo


def _dot(a, b):
    return jnp.dot(a, b, preferred_element_type=F32)


def _dot_nt(a, b):
    return lax.dot_general(a, b, (((1,), (1,)), ((), ())), preferred_element_type=F32)


def _dot_tn(a, b):
    return lax.dot_general(a, b, (((0,), (0,)), ((), ())), preferred_element_type=F32)


def _proj_ab_kernel(x_ref, g_ref, w_ref, wlr_ref, wgate_ref, bgate_ref,
                    qa_ref, ka_ref, va_ref, kab_ref, vab_ref, qb_ref, kb_ref, vb_ref, r_ref, la_ref):
    y = _rms(x_ref[...], g_ref[...]).astype(BF16)
    z = _dot(y, w_ref[...])
    c = 0
    qa_ref[...] = (z[:, c:c + A_WIDTH] * (HEAD_DIM ** -0.5)).astype(BF16); c += A_WIDTH
    ka = z[:, c:c + A_WIDTH]; c += A_WIDTH
    va = z[:, c:c + A_WIDTH]; c += A_WIDTH
    ka_ref[...] = ka
    va_ref[...] = va
    kab_ref[...] = ka.astype(BF16)
    vab_ref[...] = va.astype(BF16)
    qb_ref[...] = z[:, c:c + B_KW]; c += B_KW
    kb_ref[...] = z[:, c:c + B_KW]; c += B_KW
    vb_ref[...] = z[:, c:c + B_VW]; c += B_VW
    r_ref[...] = z[:, c:c + B_VW]
    g_lr = _dot(y, wlr_ref[...])
    gate = _dot(g_lr.astype(BF16), wgate_ref[...]) + bgate_ref[...]
    la_ref[...] = _log_sigmoid_pair(gate)[0] * (1.0 / B_GATE_TEMP)


def _proj_ab(x, g, w_main, w_lr, w_gate, b_gate, tm):
    m, d = x.shape
    row = lambda n: pl.BlockSpec((tm, n), lambda i: (i, 0))
    outs = [(A_WIDTH, BF16), (A_WIDTH, F32), (A_WIDTH, F32), (A_WIDTH, BF16), (A_WIDTH, BF16),
            (B_KW, F32), (B_KW, F32), (B_VW, F32), (B_VW, F32), (B_KW, F32)]
    return pl.pallas_call(
        _proj_ab_kernel,
        grid=(m // tm,),
        in_specs=[row(d), _resident(g.shape), _resident(w_main.shape), _resident(w_lr.shape),
                  _resident(w_gate.shape), _resident(b_gate.shape)],
        out_specs=[row(n) for n, _ in outs],
        out_shape=[jax.ShapeDtypeStruct((m, n), dt) for n, dt in outs],
        compiler_params=_params("parallel"),
        name="proj_ab",
    )(x, g, w_main, w_lr, w_gate, b_gate)


def _proj_c_kernel(x_ref, g_ref, w_ref, q_ref, k_ref, v_ref, kb_ref, vb_ref):
    y = _rms(x_ref[...], g_ref[...]).astype(BF16)
    z = _dot(y, w_ref[...])
    q_ref[...] = (z[:, :C_WIDTH] * (HEAD_DIM ** -0.5)).astype(BF16)
    k = z[:, C_WIDTH:2 * C_WIDTH]
    v = z[:, 2 * C_WIDTH:]
    k_ref[...] = k
    v_ref[...] = v
    kb_ref[...] = k.astype(BF16)
    vb_ref[...] = v.astype(BF16)


def _proj_c(x, g, w, tm):
    m, d = x.shape
    row = lambda n: pl.BlockSpec((tm, n), lambda i: (i, 0))
    dts = [BF16, F32, F32, BF16, BF16]
    return pl.pallas_call(
        _proj_c_kernel,
        grid=(m // tm,),
        in_specs=[row(d), _resident(g.shape), _resident(w.shape)],
        out_specs=[row(C_WIDTH) for _ in dts],
        out_shape=[jax.ShapeDtypeStruct((m, C_WIDTH), dt) for dt in dts],
        compiler_params=_params("parallel"),
        name="proj_c",
    )(x, g, w)


def _merge_ab_kernel(x_ref, oa_ref, ob_ref, r_ref, g_ref, w_ref, o_ref):
    ob = ob_ref[...]
    parts = []
    for h in range(B_HEADS):
        seg = ob[:, h * B_DV:(h + 1) * B_DV]
        parts.append(seg * lax.rsqrt(jnp.mean(seg * seg, axis=-1, keepdims=True) + EPS))
    r = r_ref[...]
    obn = jnp.concatenate(parts, axis=-1) * g_ref[...] * (r * jax.nn.sigmoid(r))
    o_ref[...] = (x_ref[...] + _dot(oa_ref[...], w_ref[:A_WIDTH, :])
                  + _dot(obn.astype(BF16), w_ref[A_WIDTH:, :]))


def _merge_ab(x, oa, ob, r, g_gla, w_out, tm):
    m, d = x.shape
    row = lambda n: pl.BlockSpec((tm, n), lambda i: (i, 0))
    return pl.pallas_call(
        _merge_ab_kernel,
        grid=(m // tm,),
        in_specs=[row(d), row(A_WIDTH), row(B_VW), row(B_VW), _resident(g_gla.shape),
                  _resident(w_out.shape)],
        out_specs=row(d),
        out_shape=jax.ShapeDtypeStruct((m, d), F32),
        compiler_params=_params("parallel"),
        name="merge_ab",
    )(x, oa, ob, r, g_gla, w_out)


def _out_c_kernel(x_ref, o_ref_in, w_ref, o_ref):
    o_ref[...] = x_ref[...] + _dot(o_ref_in[...], w_ref[...])


def _out_c(x, oc, w_out, tm):
    m, d = x.shape
    row = lambda n: pl.BlockSpec((tm, n), lambda i: (i, 0))
    return pl.pallas_call(
        _out_c_kernel,
        grid=(m // tm,),
        in_specs=[row(d), row(C_WIDTH), _resident(w_out.shape)],
        out_specs=row(d),
        out_shape=jax.ShapeDtypeStruct((m, d), F32),
        compiler_params=_params("parallel"),
        name="out_c",
    )(x, oc, w_out)


def _ffn_kernel(x_ref, g_ref, wg_ref, wu_ref, wd_ref, o_ref):
    x = x_ref[...]
    y = _rms(x, g_ref[...]).astype(BF16)
    h = _dot(y, wg_ref[...])
    u = _dot(y, wu_ref[...])
    a = (h * jax.nn.sigmoid(h) * u).astype(BF16)
    o_ref[...] = x + _dot(a, wd_ref[...])


def _ffn(x, g, wg, wu, wd, tm):
    m, d = x.shape
    row = pl.BlockSpec((tm, d), lambda i: (i, 0))
    return pl.pallas_call(
        _ffn_kernel,
        grid=(m // tm,),
        in_specs=[row, _resident(g.shape), _resident(wg.shape), _resident(wu.shape),
                  _resident(wd.shape)],
        out_specs=row,
        out_shape=jax.ShapeDtypeStruct((m, d), F32),
        compiler_params=_params("parallel"),
        name="ffn",
    )(x, g, wg, wu, wd)


def _final_norm_kernel(x_ref, g_ref, o_ref):
    o_ref[...] = _rms(x_ref[...], g_ref[...])


def _final_norm(x, g, tm):
    m, d = x.shape
    row = pl.BlockSpec((tm, d), lambda i: (i, 0))
    return pl.pallas_call(
        _final_norm_kernel,
        grid=(m // tm,),
        in_specs=[row, _resident(g.shape)],
        out_specs=row,
        out_shape=jax.ShapeDtypeStruct((m, d), F32),
        compiler_params=_params("parallel"),
        name="final_norm",
    )(x, g)


def _sb_kernel(q_ref, k_ref, v_ref, o_ref, acc_ref, c_ref, *, tq, tk, q_off, n_kblocks, n_pairs):
    qi = pl.program_id(2)
    lane = lax.broadcasted_iota(jnp.int32, (tq, LANES), 1)
    lo = lane < HEAD_DIM
    q_heads = []
    for p in range(n_pairs):
        q = q_ref[:, p * LANES:(p + 1) * LANES]
        q_heads += [jnp.where(lo, q, jnp.zeros_like(q)), jnp.where(lo, jnp.zeros_like(q), q)]
    later = (lax.broadcasted_iota(jnp.int32, (tk, tk), 0)
             > lax.broadcasted_iota(jnp.int32, (tk, tk), 1)).astype(BF16)

    c_ref[...] = jnp.zeros_like(c_ref)
    nkb = jnp.minimum(n_kblocks, (q_off + (qi + 1) * tq - 2) // tk + 1)

    def block(n, mask):
        ks = pl.multiple_of((nkb - 1 - n) * tk, tk)
        heads = range(2 * n_pairs)
        z = [_dot_nt(q_heads[h], k_ref[pl.ds(ks, tk), (h // 2) * LANES:(h // 2 + 1) * LANES])
             for h in heads]
        log_beta, drop, after = [], [], []
        for h in heads:
            d = jnp.maximum(z[h], 0.0) + jnp.log(1.0 + jnp.exp(-jnp.abs(z[h])))
            log_beta.append(z[h] - d)
            drop.append(d if mask is None else jnp.where(mask, d, 0.0))
        for h in heads:
            hi, lo_part = _split_bf16(drop[h])
            after.append(_dot(hi, later) + _dot(lo_part, later))
        pv = []
        for h in heads:
            c = c_ref[h]
            w = jnp.exp(log_beta[h] - after[h] - c)
            if mask is not None:
                w = jnp.where(mask, w, 0.0)
            c_ref[h] = c + after[h][:, 0:1] + drop[h][:, 0:1]
            pv.append(_dot(w.astype(BF16), v_ref[pl.ds(ks, tk), (h // 2) * LANES:(h // 2 + 1) * LANES]))
        out = [jnp.where(lo, pv[2 * p], pv[2 * p + 1]) for p in range(n_pairs)]
        return out[0] if n_pairs == 1 else jnp.concatenate(out, axis=-1)

    def all_dead():
        return jnp.min(c_ref[...]) > SB_DEAD

    q_pos = q_off + qi * tq + lax.broadcasted_iota(jnp.int32, (tq, tk), 0)
    k_pos = (nkb - 1) * tk + lax.broadcasted_iota(jnp.int32, (tq, tk), 1)
    acc_ref[...] = block(0, k_pos < q_pos)

    def cond(carry):
        n, dead = carry
        return (n < nkb) & jnp.logical_not(dead)

    def body(carry):
        n, _ = carry
        acc_ref[...] += block(n, None)
        return n + 1, all_dead()

    lax.while_loop(cond, body, (jnp.int32(1), all_dead()))
    o_ref[...] = acc_ref[...].astype(o_ref.dtype)


def _sb_attention(q, k, v, q_off):
    b, tq_total, _ = q.shape
    s = k.shape[1]
    tk = ATT_BLOCK
    tq = min(ATT_BLOCK, tq_total)
    assert tk % tq == 0 and q_off % tq == 0 and s % tk == 0
    n_pairs = SB_PAIRS_PER_STEP
    w = n_pairs * LANES
    kern = functools.partial(_sb_kernel, tq=tq, tk=tk, q_off=q_off, n_kblocks=s // tk, n_pairs=n_pairs)
    kv_spec = pl.BlockSpec((None, s, w), lambda bi, hp, qi: (bi, 0, hp))
    q_spec = pl.BlockSpec((None, tq, w), lambda bi, hp, qi: (bi, qi, hp))
    return pl.pallas_call(
        kern,
        grid=(b, A_WIDTH // w, tq_total // tq),
        in_specs=[q_spec, kv_spec, kv_spec],
        out_specs=q_spec,
        out_shape=jax.ShapeDtypeStruct(q.shape, BF16),
        scratch_shapes=[pltpu.VMEM((tq, w), F32), pltpu.VMEM((2 * n_pairs, tq, 1), F32)],
        compiler_params=_params("parallel", "parallel", "parallel"),
        name="sb_attention",
    )(q, k, v)


def _band_kernel(q_ref, k_ref, v_ref, bias_ref, o_ref, *, tq, tk, jq0):
    qi = pl.program_id(2)
    q = q_ref[...]
    lane = lax.broadcasted_iota(jnp.int32, (tq, LANES), 1)
    lo = lane < HEAD_DIM
    q_heads = (jnp.where(lo, q, jnp.zeros_like(q)), jnp.where(lo, jnp.zeros_like(q), q))
    lane_k = lax.broadcasted_iota(jnp.int32, (tk, LANES), 1) < HEAD_DIM
    jq = jq0 + qi

    z = [[None] * 3 for _ in range(2)]
    vals, pens = [], []
    for dj in range(3):
        j = jq - dj
        pens.append(jnp.where(j >= 0, 0.0, NEG_BIG).astype(F32))
        ks = pl.multiple_of(jnp.maximum(j, 0) * tk, tk)
        k = k_ref[pl.ds(ks, tk), :]
        v = v_ref[pl.ds(ks, tk), :]
        ones = jnp.ones_like(v)
        vals.append((jnp.where(lane_k, v, ones), jnp.where(lane_k, ones, v)))
        for h in range(2):
            z[h][dj] = _dot_nt(q_heads[h], k) + bias_ref[h, dj]

    acc = []
    for h in range(2):
        m = None
        for dj in range(3):
            mj = jnp.max(z[h][dj], axis=-1, keepdims=True) + pens[dj]
            m = mj if m is None else jnp.maximum(m, mj)
        a = None
        for dj in range(3):
            p = jnp.exp(z[h][dj] - (m - pens[dj])).astype(BF16)
            pv = _dot(p, vals[dj][h])
            a = pv if a is None else a + pv
        acc.append(a)
    den0 = pltpu.roll(acc[0], HEAD_DIM, axis=1)
    den1 = pltpu.roll(acc[1], HEAD_DIM, axis=1)
    o_ref[...] = jnp.where(lo, acc[0] / den0, acc[1] / den1).astype(o_ref.dtype)


def _band_attention(q, k, v, bias, q_off, k_off):
    b, tq_total, _ = q.shape
    s = k.shape[1]
    tk = ATT_BLOCK
    tq = min(ATT_BLOCK, tq_total)
    assert q_off % tk == 0 and k_off % tk == 0 and s % tk == 0 and (tq == tk or tq == tq_total)
    kern = functools.partial(_band_kernel, tq=tq, tk=tk, jq0=(q_off - k_off) // tk)
    kv_spec = pl.BlockSpec((None, s, LANES), lambda bi, hp, qi: (bi, 0, hp))
    q_spec = pl.BlockSpec((None, tq, LANES), lambda bi, hp, qi: (bi, qi, hp))
    bias_spec = pl.BlockSpec((2, 3, tq, tk), lambda bi, hp, qi: (hp, 0, 0, 0))
    return pl.pallas_call(
        kern,
        grid=(b, C_WIDTH // LANES, tq_total // tq),
        in_specs=[q_spec, kv_spec, kv_spec, bias_spec],
        out_specs=q_spec,
        out_shape=jax.ShapeDtypeStruct(q.shape, BF16),
        compiler_params=_params("parallel", "parallel", "parallel"),
        name="band_attention",
    )(q, k, v, bias)


def _band_bias_kernel(g_ref, o_ref, *, rows, valid0):
    tk = ATT_BLOCK
    q_chunk = lax.broadcasted_iota(jnp.int32, (rows, tk), 0) // CHUNK
    col = lax.broadcasted_iota(jnp.int32, (rows, tk), 1)
    k_chunk = col // CHUNK
    for dj in range(3):
        g = jnp.broadcast_to(g_ref[0, dj], (rows, 2 * tk))
        tile = pltpu.roll(g, 0, axis=1, stride=1, stride_axis=0)[:, :tk]
        diff = dj * (tk // CHUNK) + q_chunk - k_chunk
        seen = (diff >= 0) & (diff <= C_LEFT_CHUNKS)
        if dj == 0:
            seen = seen & (col < valid0)
        o_ref[0, dj] = jnp.where(seen, tile, NEG_BIG)


def _band_bias(rel_table, rows, valid0):
    tk = ATT_BLOCK
    c = jnp.arange(2 * tk, dtype=jnp.int32)
    u = jnp.where(c <= tk, -c, 2 * tk - c)
    idx = jnp.clip(jnp.arange(3, dtype=jnp.int32)[:, None] * tk + u[None, :], REL_MIN, REL_MAX) - REL_MIN
    g = rel_table[:, idx].astype(F32).reshape(C_HEADS, 3, 1, 2 * tk)
    return pl.pallas_call(
        functools.partial(_band_bias_kernel, rows=rows, valid0=valid0),
        grid=(C_HEADS,),
        in_specs=[pl.BlockSpec((1, 3, 1, 2 * tk), lambda h: (h, 0, 0, 0))],
        out_specs=pl.BlockSpec((1, 3, rows, tk), lambda h: (h, 0, 0, 0)),
        out_shape=jax.ShapeDtypeStruct((C_HEADS, 3, rows, tk), F32),
        compiler_params=_params("parallel"),
        name="band_bias",
    )(g)


def _gla_kernel(q_ref, k_ref, v_ref, la_ref, s0_ref, o_ref, s_out_ref, st_ref, *, n_chunks):
    L = GLA_CHUNK
    row = lax.broadcasted_iota(jnp.int32, (L, L), 0)
    colm = lax.broadcasted_iota(jnp.int32, (L, L), 1)
    tri = (colm <= row).astype(BF16)
    causal = colm <= row
    lane = lax.broadcasted_iota(jnp.int32, (L, LANES), 1)
    sub = lax.broadcasted_iota(jnp.int32, (LANES, B_DV), 0)
    zeros_state = jnp.zeros((B_DK, B_DV), F32)
    st_ref[0] = jnp.concatenate([s0_ref[0], zeros_state], axis=0)
    st_ref[1] = jnp.concatenate([zeros_state, s0_ref[1]], axis=0)

    def body(c, carry):
        rows = pl.ds(pl.multiple_of(c * L, L), L)
        q = q_ref[rows, :] * (B_DK ** -0.5)
        k = k_ref[rows, :]
        g_hi, g_lo = _split_bf16(la_ref[rows, :])
        b = _dot(tri, g_hi) + _dot(tri, g_lo)
        qg = q * jnp.exp(b)
        kg = (k * jnp.exp(-b)).astype(BF16)
        b_t = b.T
        b_last = b_t[:, L - 1:L]
        kd_t = k.T * jnp.exp(b_last - b_t)
        decay = jnp.exp(b_last)
        for h in range(2):
            mine = (lane >= h * B_DK) & (lane < (h + 1) * B_DK)
            qg_h = jnp.where(mine, qg, 0.0).astype(BF16)
            att = jnp.where(causal, _dot_nt(qg_h, kg), 0.0)
            v_h = v_ref[rows, h * B_DV:(h + 1) * B_DV].astype(BF16)
            s_h = st_ref[h]
            o_ref[rows, h * B_DV:(h + 1) * B_DV] = _dot(att.astype(BF16), v_h) + _dot(qg_h, s_h.astype(BF16))
            upd = _dot(kd_t.astype(BF16), v_h)
            mine_rows = (sub >= h * B_DK) & (sub < (h + 1) * B_DK)
            st_ref[h] = decay * s_h + jnp.where(mine_rows, upd, 0.0)
        return carry

    lax.fori_loop(0, n_chunks, body, 0)
    s_out_ref[0] = st_ref[0][:B_DK, :]
    s_out_ref[1] = st_ref[1][B_DK:, :]


def _gla(q, k, v, la, s0):
    b, t, _ = q.shape
    qk_spec = pl.BlockSpec((None, t, LANES), lambda bi, p: (bi, 0, p))
    v_spec = pl.BlockSpec((None, t, 2 * B_DV), lambda bi, p: (bi, 0, p))
    s_spec = pl.BlockSpec((None, 2, B_DK, B_DV), lambda bi, p: (bi, p, 0, 0))
    return pl.pallas_call(
        functools.partial(_gla_kernel, n_chunks=t // GLA_CHUNK),
        grid=(b, B_HEADS // 2),
        in_specs=[qk_spec, qk_spec, v_spec, qk_spec, s_spec],
        out_specs=[v_spec, s_spec],
        out_shape=[jax.ShapeDtypeStruct(v.shape, F32), jax.ShapeDtypeStruct(s0.shape, F32)],
        scratch_shapes=[pltpu.VMEM((2, LANES, B_DV), F32)],
        compiler_params=_params("parallel", "parallel"),
        name="gla",
    )(q, k, v, la, s0)


def _pad_rows(x, n):
    return jnp.pad(x, ((0, 0), (0, n - x.shape[1]), (0, 0)))


def _row_tile(m):
    for tm in (512, 256, 128, 64, 32, 16, 8):
        if m % tm == 0:
            return tm
    raise ValueError(f"token count {m} is not a multiple of 8")


def kernel(x_prompt, x_sample, cache_a_k, cache_a_v, state_b, cache_c_k, cache_c_v, norm_mix_g, norm_ffn_g, w_in_ab, w_gate_b, b_gate_b, norm_gla_g, w_out_ab, w_qkv_c, rel_bias_c, w_out_c, w_ffn_gate, w_ffn_up, w_ffn_down, norm_final_g):
    bp, tp, d = x_prompt.shape
    bs, ts, _ = x_sample.shape
    depth = norm_mix_g.shape[0]
    past = cache_a_k.shape[2]
    wc = cache_c_k.shape[2]
    assert tp % ATT_BLOCK == 0 and past % ATT_BLOCK == 0 and wc % ATT_BLOCK == 0
    assert ts <= GLA_CHUNK and ts % 8 == 0
    mp, ms = bp * tp, bs * ts
    tmp, tms = _row_tile(mp), _row_tile(ms)
    xp = x_prompt.reshape(mp, d)
    xs = x_sample.reshape(ms, d)
    row2 = lambda a: a.reshape(1, -1)

    a_kp, a_vp, a_ks, a_vs, b_sp, b_ss = [], [], [], [], [], []
    c_kp, c_vp, c_ks, c_vs = [], [], [], []

    for layer in range(depth):
        i = layer // 2
        g_mix = row2(norm_mix_g[layer])
        if layer % 2 == 0:
            w = w_in_ab[i]
            o = 3 * A_WIDTH + 2 * B_KW + B_VW
            w_main = jnp.concatenate([w[:, :o], w[:, o + B_GATE_RANK:]], axis=1).astype(BF16)
            w_lr = jnp.pad(w[:, o:o + B_GATE_RANK], ((0, 0), (0, LANES - B_GATE_RANK))).astype(BF16)
            w_gate = jnp.pad(w_gate_b[i], ((0, LANES - B_GATE_RANK), (0, 0))).astype(BF16)
            b_gate = row2(b_gate_b[i])
            g_gla = row2(norm_gla_g[i])
            w_out = w_out_ab[i].astype(BF16)

            qa, ka, va, kab, vab, qb, kb, vb, r, la = _proj_ab(xp, g_mix, w_main, w_lr, w_gate, b_gate, tmp)
            sh = lambda a: a.reshape(bp, tp, -1)
            oa = _sb_attention(sh(qa), sh(kab), sh(vab), 0)
            s0 = jnp.zeros((bp, B_HEADS, B_DK, B_DV), F32)
            ob, sbp = _gla(sh(qb), sh(kb), sh(vb), sh(la), s0)
            xp = _merge_ab(xp, oa.reshape(mp, -1), ob.reshape(mp, -1), r, g_gla, w_out, tmp)
            a_kp.append(ka.reshape(bp, tp, A_HEADS, HEAD_DIM))
            a_vp.append(va.reshape(bp, tp, A_HEADS, HEAD_DIM))
            b_sp.append(sbp)

            qa, ka, va, kab, vab, qb, kb, vb, r, la = _proj_ab(xs, g_mix, w_main, w_lr, w_gate, b_gate, tms)
            sh = lambda a: a.reshape(bs, ts, -1)
            s_all = past + ts
            s_pad = -(-s_all // ATT_BLOCK) * ATT_BLOCK
            k_all = _pad_rows(jnp.concatenate(
                [cache_a_k[i].reshape(bs, past, A_WIDTH).astype(BF16), sh(kab)], axis=1), s_pad)
            v_all = _pad_rows(jnp.concatenate(
                [cache_a_v[i].reshape(bs, past, A_WIDTH).astype(BF16), sh(vab)], axis=1), s_pad)
            oa = _sb_attention(sh(qa), k_all, v_all, past)
            pad_t = lambda a: _pad_rows(sh(a), GLA_CHUNK)
            ob, sbs = _gla(pad_t(qb), pad_t(kb), pad_t(vb), pad_t(la), state_b[i])
            xs = _merge_ab(xs, oa.reshape(ms, -1), ob[:, :ts].reshape(ms, -1), r, g_gla, w_out, tms)
            a_ks.append(ka.reshape(bs, ts, A_HEADS, HEAD_DIM))
            a_vs.append(va.reshape(bs, ts, A_HEADS, HEAD_DIM))
            b_ss.append(sbs)
        else:
            w_qkv = w_qkv_c[i].astype(BF16)
            w_out = w_out_c[i].astype(BF16)

            q, k, v, kb16, vb16 = _proj_c(xp, g_mix, w_qkv, tmp)
            sh = lambda a: a.reshape(bp, tp, -1)
            bias = _band_bias(rel_bias_c[i], ATT_BLOCK, ATT_BLOCK)
            oc = _band_attention(sh(q), sh(kb16), sh(vb16), bias, 0, 0)
            xp = _out_c(xp, oc.reshape(mp, -1), w_out, tmp)
            keep = min(C_LEFT_CHUNKS * CHUNK, tp)
            c_kp.append(k.reshape(bp, tp, C_HEADS, HEAD_DIM)[:, tp - keep:])
            c_vp.append(v.reshape(bp, tp, C_HEADS, HEAD_DIM)[:, tp - keep:])

            q, k, v, kb16, vb16 = _proj_c(xs, g_mix, w_qkv, tms)
            sh = lambda a: a.reshape(bs, ts, -1)
            s_all = wc + ts
            s_pad = -(-s_all // ATT_BLOCK) * ATT_BLOCK
            k_all = _pad_rows(jnp.concatenate(
                [cache_c_k[i].reshape(bs, wc, C_WIDTH).astype(BF16), sh(kb16)], axis=1), s_pad)
            v_all = _pad_rows(jnp.concatenate(
                [cache_c_v[i].reshape(bs, wc, C_WIDTH).astype(BF16), sh(vb16)], axis=1), s_pad)
            assert s_pad - ATT_BLOCK == wc
            bias = _band_bias(rel_bias_c[i], ts, s_all - wc)
            oc = _band_attention(sh(q), k_all, v_all, bias, past, past - wc)
            xs = _out_c(xs, oc.reshape(ms, -1), w_out, tms)
            c_ks.append(k.reshape(bs, ts, C_HEADS, HEAD_DIM))
            c_vs.append(v.reshape(bs, ts, C_HEADS, HEAD_DIM))

        g_ffn = row2(norm_ffn_g[layer])
        wg = w_ffn_gate[layer].astype(BF16)
        wu = w_ffn_up[layer].astype(BF16)
        wd = w_ffn_down[layer].astype(BF16)
        xp = _ffn(xp, g_ffn, wg, wu, wd, tmp)
        xs = _ffn(xs, g_ffn, wg, wu, wd, tms)

    g_fin = row2(norm_final_g)
    y_prompt = _final_norm(xp, g_fin, tmp).reshape(bp, tp, d)
    y_sample = _final_norm(xs, g_fin, tms).reshape(bs, ts, d)
    return (y_prompt, y_sample, jnp.stack(a_kp), jnp.stack(a_vp), jnp.stack(a_ks), jnp.stack(a_vs),
            jnp.stack(b_sp), jnp.stack(b_ss), jnp.stack(c_kp), jnp.stack(c_vp),
            jnp.stack(c_ks), jnp.stack(c_vs))
```

```python
import functools

import jax
import jax.numpy as jnp
from jax import lax
from jax.experimental import pallas as pl
from jax.experimental.pallas import tpu as pltpu

F32 = jnp.float32
BF16 = jnp.bfloat16

EPS = 1e-6
HEAD_DIM = 64
LANES = 128
A_HEADS = 8
A_WIDTH = A_HEADS * HEAD_DIM
B_HEADS = 4
B_DK = 64
B_DV = 128
B_KW = B_HEADS * B_DK
B_VW = B_HEADS * B_DV
B_GATE_RANK = 16
B_GATE_TEMP = 16.0
GLA_CHUNK = 64
C_HEADS = 16
C_WIDTH = C_HEADS * HEAD_DIM
CHUNK = 64
C_LEFT_CHUNKS = 8
REL_MIN = -(CHUNK - 1)
REL_MAX = 128
ATT_BLOCK = 256
NEG_BIG = -1e30
SB_DEAD = 104.0
SB_PAIRS_PER_STEP = 2
BAND_PAIRS_PER_STEP = 2
VMEM_LIMIT = 56 * 1024 * 1024


def _params(*sem):
    return pltpu.CompilerParams(dimension_semantics=sem, vmem_limit_bytes=VMEM_LIMIT)


def _resident(shape):
    return pl.BlockSpec(shape, lambda *_: (0,) * len(shape), pipeline_mode=pl.Buffered(1))


def _rms(x, g):
    return x * lax.rsqrt(jnp.mean(x * x, axis=-1, keepdims=True) + EPS) * g


def _log_sigmoid_pair(z):
    l = jnp.log1p(jnp.exp(-jnp.abs(z)))
    return jnp.minimum(z, 0.0) - l, jnp.minimum(-z, 0.0) - l


def _split_bf16(x):
    hi = x.astype(BF16)
    lo = (x - hi.astype(F32)).astype(BF16)
    return hi, lo


def _dot(a, b):
    return jnp.dot(a, b, preferred_element_type=F32)


def _dot_nt(a, b):
    return lax.dot_general(a, b, (((1,), (1,)), ((), ())), preferred_element_type=F32)


def _dot_tn(a, b):
    return lax.dot_general(a, b, (((0,), (0,)), ((), ())), preferred_element_type=F32)


def _pair_cols(p):
    return slice(p * LANES, (p + 1) * LANES)


def _split_heads(q_ref, n_pairs, tq):
    lo = lax.broadcasted_iota(jnp.int32, (tq, LANES), 1) < HEAD_DIM
    heads = []
    for p in range(n_pairs):
        q = q_ref[:, _pair_cols(p)]
        heads += [jnp.where(lo, q, jnp.zeros_like(q)), jnp.where(lo, jnp.zeros_like(q), q)]
    return heads, lo


def _proj_ab_kernel(x_ref, g_ref, w_ref, wlr_ref, wgate_ref, bgate_ref,
                    qa_ref, ka_ref, va_ref, kab_ref, vab_ref, qb_ref, kb_ref, vb_ref, r_ref, la_ref):
    y = _rms(x_ref[...], g_ref[...]).astype(BF16)
    z = _dot(y, w_ref[...])
    c = 0
    qa_ref[...] = (z[:, c:c + A_WIDTH] * (HEAD_DIM ** -0.5)).astype(BF16); c += A_WIDTH
    ka = z[:, c:c + A_WIDTH]; c += A_WIDTH
    va = z[:, c:c + A_WIDTH]; c += A_WIDTH
    ka_ref[...] = ka
    va_ref[...] = va
    kab_ref[...] = ka.astype(BF16)
    vab_ref[...] = va.astype(BF16)
    qb_ref[...] = z[:, c:c + B_KW]; c += B_KW
    kb_ref[...] = z[:, c:c + B_KW]; c += B_KW
    vb_ref[...] = z[:, c:c + B_VW]; c += B_VW
    r_ref[...] = z[:, c:c + B_VW]
    g_lr = _dot(y, wlr_ref[...])
    gate = _dot(g_lr.astype(BF16), wgate_ref[...]) + bgate_ref[...]
    la_ref[...] = _log_sigmoid_pair(gate)[0] * (1.0 / B_GATE_TEMP)


def _proj_ab(x, g, w_main, w_lr, w_gate, b_gate, tm):
    m, d = x.shape
    row = lambda n: pl.BlockSpec((tm, n), lambda i: (i, 0))
    outs = [(A_WIDTH, BF16), (A_WIDTH, F32), (A_WIDTH, F32), (A_WIDTH, BF16), (A_WIDTH, BF16),
            (B_KW, F32), (B_KW, F32), (B_VW, F32), (B_VW, F32), (B_KW, F32)]
    return pl.pallas_call(
        _proj_ab_kernel,
        grid=(m // tm,),
        in_specs=[row(d), _resident(g.shape), _resident(w_main.shape), _resident(w_lr.shape),
                  _resident(w_gate.shape), _resident(b_gate.shape)],
        out_specs=[row(n) for n, _ in outs],
        out_shape=[jax.ShapeDtypeStruct((m, n), dt) for n, dt in outs],
        compiler_params=_params("parallel"),
        name="proj_ab",
    )(x, g, w_main, w_lr, w_gate, b_gate)


def _proj_c_kernel(x_ref, g_ref, w_ref, q_ref, k_ref, v_ref, kb_ref, vb_ref):
    y = _rms(x_ref[...], g_ref[...]).astype(BF16)
    z = _dot(y, w_ref[...])
    q_ref[...] = (z[:, :C_WIDTH] * (HEAD_DIM ** -0.5)).astype(BF16)
    k = z[:, C_WIDTH:2 * C_WIDTH]
    v = z[:, 2 * C_WIDTH:]
    k_ref[...] = k
    v_ref[...] = v
    kb_ref[...] = k.astype(BF16)
    vb_ref[...] = v.astype(BF16)


def _proj_c(x, g, w, tm):
    m, d = x.shape
    row = lambda n: pl.BlockSpec((tm, n), lambda i: (i, 0))
    dts = [BF16, F32, F32, BF16, BF16]
    return pl.pallas_call(
        _proj_c_kernel,
        grid=(m // tm,),
        in_specs=[row(d), _resident(g.shape), _resident(w.shape)],
        out_specs=[row(C_WIDTH) for _ in dts],
        out_shape=[jax.ShapeDtypeStruct((m, C_WIDTH), dt) for dt in dts],
        compiler_params=_params("parallel"),
        name="proj_c",
    )(x, g, w)


def _merge_ab_kernel(x_ref, oa_ref, ob_ref, r_ref, g_ref, w_ref, o_ref):
    ob = ob_ref[...]
    parts = []
    for h in range(B_HEADS):
        seg = ob[:, h * B_DV:(h + 1) * B_DV]
        parts.append(seg * lax.rsqrt(jnp.mean(seg * seg, axis=-1, keepdims=True) + EPS))
    r = r_ref[...]
    obn = jnp.concatenate(parts, axis=-1) * g_ref[...] * (r * jax.nn.sigmoid(r))
    o_ref[...] = (x_ref[...] + _dot(oa_ref[...], w_ref[:A_WIDTH, :])
                  + _dot(obn.astype(BF16), w_ref[A_WIDTH:, :]))


def _merge_ab(x, oa, ob, r, g_gla, w_out, tm):
    m, d = x.shape
    row = lambda n: pl.BlockSpec((tm, n), lambda i: (i, 0))
    return pl.pallas_call(
        _merge_ab_kernel,
        grid=(m // tm,),
        in_specs=[row(d), row(A_WIDTH), row(B_VW), row(B_VW), _resident(g_gla.shape),
                  _resident(w_out.shape)],
        out_specs=row(d),
        out_shape=jax.ShapeDtypeStruct((m, d), F32),
        compiler_params=_params("parallel"),
        name="merge_ab",
    )(x, oa, ob, r, g_gla, w_out)


def _out_c_kernel(x_ref, o_ref_in, w_ref, o_ref):
    o_ref[...] = x_ref[...] + _dot(o_ref_in[...], w_ref[...])


def _out_c(x, oc, w_out, tm):
    m, d = x.shape
    row = lambda n: pl.BlockSpec((tm, n), lambda i: (i, 0))
    return pl.pallas_call(
        _out_c_kernel,
        grid=(m // tm,),
        in_specs=[row(d), row(C_WIDTH), _resident(w_out.shape)],
        out_specs=row(d),
        out_shape=jax.ShapeDtypeStruct((m, d), F32),
        compiler_params=_params("parallel"),
        name="out_c",
    )(x, oc, w_out)


def _ffn_kernel(x_ref, g_ref, wg_ref, wu_ref, wd_ref, o_ref):
    x = x_ref[...]
    y = _rms(x, g_ref[...]).astype(BF16)
    h = _dot(y, wg_ref[...])
    u = _dot(y, wu_ref[...])
    a = (h * jax.nn.sigmoid(h) * u).astype(BF16)
    o_ref[...] = x + _dot(a, wd_ref[...])


def _ffn(x, g, wg, wu, wd, tm):
    m, d = x.shape
    row = pl.BlockSpec((tm, d), lambda i: (i, 0))
    return pl.pallas_call(
        _ffn_kernel,
        grid=(m // tm,),
        in_specs=[row, _resident(g.shape), _resident(wg.shape), _resident(wu.shape),
                  _resident(wd.shape)],
        out_specs=row,
        out_shape=jax.ShapeDtypeStruct((m, d), F32),
        compiler_params=_params("parallel"),
        name="ffn",
    )(x, g, wg, wu, wd)


def _final_norm_kernel(x_ref, g_ref, o_ref):
    o_ref[...] = _rms(x_ref[...], g_ref[...])


def _final_norm(x, g, tm):
    m, d = x.shape
    row = pl.BlockSpec((tm, d), lambda i: (i, 0))
    return pl.pallas_call(
        _final_norm_kernel,
        grid=(m // tm,),
        in_specs=[row, _resident(g.shape)],
        out_specs=row,
        out_shape=jax.ShapeDtypeStruct((m, d), F32),
        compiler_params=_params("parallel"),
        name="final_norm",
    )(x, g)


def _sb_core(q_heads, lo, first_kv, first_mask, earlier_kv, n_earlier, acc_ref, c_ref, o_ref, n_pairs):
    tk = first_kv[0].shape[0]
    heads = range(2 * n_pairs)
    later = (lax.broadcasted_iota(jnp.int32, (tk, tk), 0)
             > lax.broadcasted_iota(jnp.int32, (tk, tk), 1)).astype(BF16)
    c_ref[...] = jnp.zeros_like(c_ref)

    def block(kv, mask):
        k, v = kv
        z = [_dot_nt(q_heads[h], k[:, _pair_cols(h // 2)]) for h in heads]
        log_beta, drop, after = [], [], []
        for h in heads:
            d = jnp.maximum(z[h], 0.0) + jnp.log(1.0 + jnp.exp(-jnp.abs(z[h])))
            log_beta.append(z[h] - d)
            drop.append(d if mask is None else jnp.where(mask, d, 0.0))
        for h in heads:
            hi, lo_part = _split_bf16(drop[h])
            after.append(_dot(hi, later) + _dot(lo_part, later))
        pv = []
        for h in heads:
            c = c_ref[h]
            w = jnp.exp(log_beta[h] - after[h] - c)
            if mask is not None:
                w = jnp.where(mask, w, 0.0)
            c_ref[h] = c + after[h][:, 0:1] + drop[h][:, 0:1]
            pv.append(_dot(w.astype(BF16), v[:, _pair_cols(h // 2)]))
        out = [jnp.where(lo, pv[2 * p], pv[2 * p + 1]) for p in range(n_pairs)]
        return out[0] if n_pairs == 1 else jnp.concatenate(out, axis=-1)

    def all_dead():
        return jnp.min(c_ref[...]) > SB_DEAD

    acc_ref[...] = block(first_kv, first_mask)

    def cond(carry):
        n, dead = carry
        return (n < n_earlier) & jnp.logical_not(dead)

    def body(carry):
        n, _ = carry
        acc_ref[...] += block(earlier_kv(n), None)
        return n + 1, all_dead()

    lax.while_loop(cond, body, (jnp.int32(0), all_dead()))
    o_ref[...] = acc_ref[...].astype(o_ref.dtype)


def _sb_prompt_kernel(q_ref, k_ref, v_ref, o_ref, acc_ref, c_ref, *, tb, n_pairs):
    qi = pl.program_id(2)
    q_heads, lo = _split_heads(q_ref, n_pairs, tb)

    def kv_block(j):
        rows = pl.ds(pl.multiple_of(j * tb, tb), tb)
        return k_ref[rows, :], v_ref[rows, :]

    strictly_earlier = (lax.broadcasted_iota(jnp.int32, (tb, tb), 1)
                        < lax.broadcasted_iota(jnp.int32, (tb, tb), 0))
    _sb_core(q_heads, lo, kv_block(qi), strictly_earlier, lambda n: kv_block(qi - 1 - n), qi,
             acc_ref, c_ref, o_ref, n_pairs)


def _sb_sample_kernel(q_ref, kn_ref, vn_ref, kc_ref, vc_ref, o_ref, acc_ref, c_ref, kpad_ref, vpad_ref,
                      *, ts, tk, n_cache_blocks, n_pairs):
    q_heads, lo = _split_heads(q_ref, n_pairs, ts)
    kpad_ref[...] = jnp.zeros_like(kpad_ref)
    vpad_ref[...] = jnp.zeros_like(vpad_ref)
    kpad_ref[:ts, :] = kn_ref[...]
    vpad_ref[:ts, :] = vn_ref[...]

    def cache_block(n):
        rows = pl.ds(pl.multiple_of((n_cache_blocks - 1 - n) * tk, tk), tk)
        return kc_ref[rows, :].astype(BF16), vc_ref[rows, :].astype(BF16)

    strictly_earlier = (lax.broadcasted_iota(jnp.int32, (ts, tk), 1)
                        < lax.broadcasted_iota(jnp.int32, (ts, tk), 0))
    _sb_core(q_heads, lo, (kpad_ref[...], vpad_ref[...]), strictly_earlier, cache_block, n_cache_blocks,
             acc_ref, c_ref, o_ref, n_pairs)


def _sb_attention_prompt(q, k, v):
    b, t, _ = q.shape
    tb = ATT_BLOCK
    assert t % tb == 0
    n_pairs = SB_PAIRS_PER_STEP
    w = n_pairs * LANES
    kv_spec = pl.BlockSpec((None, t, w), lambda bi, hp, qi: (bi, 0, hp))
    q_spec = pl.BlockSpec((None, tb, w), lambda bi, hp, qi: (bi, qi, hp))
    return pl.pallas_call(
        functools.partial(_sb_prompt_kernel, tb=tb, n_pairs=n_pairs),
        grid=(b, A_WIDTH // w, t // tb),
        in_specs=[q_spec, kv_spec, kv_spec],
        out_specs=q_spec,
        out_shape=jax.ShapeDtypeStruct(q.shape, BF16),
        scratch_shapes=[pltpu.VMEM((tb, w), F32), pltpu.VMEM((2 * n_pairs, tb, 1), F32)],
        compiler_params=_params("parallel", "parallel", "parallel"),
        name="sb_attention",
    )(q, k, v)


def _sb_attention_sample(q, k_new, v_new, k_cache, v_cache):
    b, ts, _ = q.shape
    past = k_cache.shape[1]
    tk = ATT_BLOCK
    assert past % tk == 0 and ts <= tk
    n_pairs = SB_PAIRS_PER_STEP
    w = n_pairs * LANES
    new_spec = pl.BlockSpec((None, ts, w), lambda bi, hp: (bi, 0, hp))
    cache_spec = pl.BlockSpec((None, past, w), lambda bi, hp: (bi, 0, hp))
    return pl.pallas_call(
        functools.partial(_sb_sample_kernel, ts=ts, tk=tk, n_cache_blocks=past // tk, n_pairs=n_pairs),
        grid=(b, A_WIDTH // w),
        in_specs=[new_spec, new_spec, new_spec, cache_spec, cache_spec],
        out_specs=new_spec,
        out_shape=jax.ShapeDtypeStruct(q.shape, BF16),
        scratch_shapes=[pltpu.VMEM((ts, w), F32), pltpu.VMEM((2 * n_pairs, ts, 1), F32),
                        pltpu.VMEM((tk, w), BF16), pltpu.VMEM((tk, w), BF16)],
        compiler_params=_params("parallel", "parallel"),
        name="sb_attention_sample",
    )(q, k_new, v_new, k_cache, v_cache)


def _band_core(q_heads, lo, kv, pens, bias_ref, o_ref, n_pairs):
    heads = range(2 * n_pairs)
    blocks = range(len(kv))
    tk = kv[0][0].shape[0]
    lane_k = lax.broadcasted_iota(jnp.int32, (tk, LANES), 1) < HEAD_DIM
    z = [[_dot_nt(q_heads[h], kv[i][0][:, _pair_cols(h // 2)]) + bias_ref[h, i] for i in blocks]
         for h in heads]
    acc = []
    for h in heads:
        m = None
        for i in blocks:
            mi = jnp.max(z[h][i], axis=-1, keepdims=True)
            if pens[i] is not None:
                mi = mi + pens[i]
            m = mi if m is None else jnp.maximum(m, mi)
        a = None
        for i in blocks:
            shift = m if pens[i] is None else m - pens[i]
            p = jnp.exp(z[h][i] - shift).astype(BF16)
            v = kv[i][1][:, _pair_cols(h // 2)]
            ones = jnp.ones_like(v)
            v = jnp.where(lane_k, v, ones) if h % 2 == 0 else jnp.where(lane_k, ones, v)
            pv = _dot(p, v)
            a = pv if a is None else a + pv
        acc.append(a)
    for p in range(n_pairs):
        a0, a1 = acc[2 * p], acc[2 * p + 1]
        o_ref[:, _pair_cols(p)] = jnp.where(lo, a0 / pltpu.roll(a0, HEAD_DIM, axis=1),
                                            a1 / pltpu.roll(a1, HEAD_DIM, axis=1)).astype(o_ref.dtype)


def _band_prompt_kernel(q_ref, k_ref, v_ref, bias_ref, o_ref, *, tq, tk, n_pairs):
    qi = pl.program_id(2)
    q_heads, lo = _split_heads(q_ref, n_pairs, tq)
    kv, pens = [], []
    for dj in range(3):
        j = qi - dj
        pens.append(None if dj == 0 else jnp.where(j >= 0, 0.0, NEG_BIG).astype(F32))
        ks = pl.multiple_of(jnp.maximum(j, 0) * tk, tk)
        kv.append((k_ref[pl.ds(ks, tk), :], v_ref[pl.ds(ks, tk), :]))
    _band_core(q_heads, lo, kv, pens, bias_ref, o_ref, n_pairs)


def _band_sample_kernel(q_ref, kn_ref, vn_ref, kc_ref, vc_ref, bias_ref, o_ref, kpad_ref, vpad_ref,
                        *, ts, tk, n_cache_blocks, n_pairs):
    q_heads, lo = _split_heads(q_ref, n_pairs, ts)
    kpad_ref[...] = jnp.zeros_like(kpad_ref)
    vpad_ref[...] = jnp.zeros_like(vpad_ref)
    kpad_ref[:ts, :] = kn_ref[...]
    vpad_ref[:ts, :] = vn_ref[...]
    kv = [(kpad_ref[...], vpad_ref[...])]
    for dj in range(1, n_cache_blocks + 1):
        rows = slice((n_cache_blocks - dj) * tk, (n_cache_blocks - dj + 1) * tk)
        kv.append((kc_ref[rows, :].astype(BF16), vc_ref[rows, :].astype(BF16)))
    _band_core(q_heads, lo, kv, [None] * len(kv), bias_ref, o_ref, n_pairs)


def _band_attention_prompt(q, k, v, bias):
    b, t, _ = q.shape
    tq = tk = ATT_BLOCK
    assert t % tk == 0
    n_pairs = BAND_PAIRS_PER_STEP
    w = n_pairs * LANES
    kern = functools.partial(_band_prompt_kernel, tq=tq, tk=tk, n_pairs=n_pairs)
    kv_spec = pl.BlockSpec((None, t, w), lambda bi, hp, qi: (bi, 0, hp))
    q_spec = pl.BlockSpec((None, tq, w), lambda bi, hp, qi: (bi, qi, hp))
    bias_spec = pl.BlockSpec((2 * n_pairs, 3, tq, tk), lambda bi, hp, qi: (hp, 0, 0, 0))
    return pl.pallas_call(
        kern,
        grid=(b, C_WIDTH // w, t // tq),
        in_specs=[q_spec, kv_spec, kv_spec, bias_spec],
        out_specs=q_spec,
        out_shape=jax.ShapeDtypeStruct(q.shape, BF16),
        compiler_params=_params("parallel", "parallel", "parallel"),
        name="band_attention",
    )(q, k, v, bias)


def _band_attention_sample(q, k_new, v_new, k_cache, v_cache, bias):
    b, ts, _ = q.shape
    wc = k_cache.shape[1]
    tk = ATT_BLOCK
    n_cache_blocks = min(wc // tk, 2)
    assert wc % (n_cache_blocks * tk) == 0 and ts <= tk
    n_pairs = BAND_PAIRS_PER_STEP
    w = n_pairs * LANES
    kern = functools.partial(_band_sample_kernel, ts=ts, tk=tk, n_cache_blocks=n_cache_blocks,
                             n_pairs=n_pairs)
    new_spec = pl.BlockSpec((None, ts, w), lambda bi, hp: (bi, 0, hp))
    cache_rows = n_cache_blocks * tk
    cache_spec = pl.BlockSpec((None, cache_rows, w), lambda bi, hp: (bi, wc // cache_rows - 1, hp))
    bias_spec = pl.BlockSpec((2 * n_pairs, 1 + n_cache_blocks, ts, tk), lambda bi, hp: (hp, 0, 0, 0))
    return pl.pallas_call(
        kern,
        grid=(b, C_WIDTH // w),
        in_specs=[new_spec, new_spec, new_spec, cache_spec, cache_spec, bias_spec],
        out_specs=new_spec,
        out_shape=jax.ShapeDtypeStruct(q.shape, BF16),
        scratch_shapes=[pltpu.VMEM((tk, w), BF16), pltpu.VMEM((tk, w), BF16)],
        compiler_params=_params("parallel", "parallel"),
        name="band_attention_sample",
    )(q, k_new, v_new, k_cache, v_cache, bias)


def _band_bias_kernel(g_ref, o_ref, *, rows, valid0):
    tk = ATT_BLOCK
    q_chunk = lax.broadcasted_iota(jnp.int32, (rows, tk), 0) // CHUNK
    col = lax.broadcasted_iota(jnp.int32, (rows, tk), 1)
    k_chunk = col // CHUNK
    for dj in range(3):
        g = jnp.broadcast_to(g_ref[0, dj], (rows, 2 * tk))
        tile = pltpu.roll(g, 0, axis=1, stride=1, stride_axis=0)[:, :tk]
        diff = dj * (tk // CHUNK) + q_chunk - k_chunk
        seen = (diff >= 0) & (diff <= C_LEFT_CHUNKS)
        if dj == 0:
            seen = seen & (col < valid0)
        o_ref[0, dj] = jnp.where(seen, tile, NEG_BIG)


def _band_bias(rel_table, rows, valid0):
    tk = ATT_BLOCK
    c = jnp.arange(2 * tk, dtype=jnp.int32)
    u = jnp.where(c <= tk, -c, 2 * tk - c)
    idx = jnp.clip(jnp.arange(3, dtype=jnp.int32)[:, None] * tk + u[None, :], REL_MIN, REL_MAX) - REL_MIN
    g = rel_table[:, idx].astype(F32).reshape(C_HEADS, 3, 1, 2 * tk)
    return pl.pallas_call(
        functools.partial(_band_bias_kernel, rows=rows, valid0=valid0),
        grid=(C_HEADS,),
        in_specs=[pl.BlockSpec((1, 3, 1, 2 * tk), lambda h: (h, 0, 0, 0))],
        out_specs=pl.BlockSpec((1, 3, rows, tk), lambda h: (h, 0, 0, 0)),
        out_shape=jax.ShapeDtypeStruct((C_HEADS, 3, rows, tk), F32),
        compiler_params=_params("parallel"),
        name="band_bias",
    )(g)


def _gla_kernel(q_ref, k_ref, v_ref, la_ref, s0_ref, o_ref, s_out_ref, st_ref, *, n_chunks):
    L = GLA_CHUNK
    row = lax.broadcasted_iota(jnp.int32, (L, L), 0)
    colm = lax.broadcasted_iota(jnp.int32, (L, L), 1)
    tri = (colm <= row).astype(BF16)
    causal = colm <= row
    lane = lax.broadcasted_iota(jnp.int32, (L, LANES), 1)
    sub = lax.broadcasted_iota(jnp.int32, (LANES, B_DV), 0)
    zeros_state = jnp.zeros((B_DK, B_DV), F32)
    st_ref[0] = jnp.concatenate([s0_ref[0], zeros_state], axis=0)
    st_ref[1] = jnp.concatenate([zeros_state, s0_ref[1]], axis=0)

    def body(c, carry):
        rows = pl.ds(pl.multiple_of(c * L, L), L)
        q = q_ref[rows, :] * (B_DK ** -0.5)
        k = k_ref[rows, :]
        g_hi, g_lo = _split_bf16(la_ref[rows, :])
        b = _dot(tri, g_hi) + _dot(tri, g_lo)
        qg = q * jnp.exp(b)
        kg = (k * jnp.exp(-b)).astype(BF16)
        b_t = b.T
        b_last = b_t[:, L - 1:L]
        kd_t = k.T * jnp.exp(b_last - b_t)
        decay = jnp.exp(b_last)
        for h in range(2):
            mine = (lane >= h * B_DK) & (lane < (h + 1) * B_DK)
            qg_h = jnp.where(mine, qg, 0.0).astype(BF16)
            att = jnp.where(causal, _dot_nt(qg_h, kg), 0.0)
            v_h = v_ref[rows, h * B_DV:(h + 1) * B_DV].astype(BF16)
            s_h = st_ref[h]
            o_ref[rows, h * B_DV:(h + 1) * B_DV] = _dot(att.astype(BF16), v_h) + _dot(qg_h, s_h.astype(BF16))
            upd = _dot(kd_t.astype(BF16), v_h)
            mine_rows = (sub >= h * B_DK) & (sub < (h + 1) * B_DK)
            st_ref[h] = decay * s_h + jnp.where(mine_rows, upd, 0.0)
        return carry

    lax.fori_loop(0, n_chunks, body, 0)
    s_out_ref[0] = st_ref[0][:B_DK, :]
    s_out_ref[1] = st_ref[1][B_DK:, :]


def _gla(q, k, v, la, s0):
    b, t, _ = q.shape
    qk_spec = pl.BlockSpec((None, t, LANES), lambda bi, p: (bi, 0, p))
    v_spec = pl.BlockSpec((None, t, 2 * B_DV), lambda bi, p: (bi, 0, p))
    s_spec = pl.BlockSpec((None, 2, B_DK, B_DV), lambda bi, p: (bi, p, 0, 0))
    return pl.pallas_call(
        functools.partial(_gla_kernel, n_chunks=t // GLA_CHUNK),
        grid=(b, B_HEADS // 2),
        in_specs=[qk_spec, qk_spec, v_spec, qk_spec, s_spec],
        out_specs=[v_spec, s_spec],
        out_shape=[jax.ShapeDtypeStruct(v.shape, F32), jax.ShapeDtypeStruct(s0.shape, F32)],
        scratch_shapes=[pltpu.VMEM((2, LANES, B_DV), F32)],
        compiler_params=_params("parallel", "parallel"),
        name="gla",
    )(q, k, v, la, s0)


def _pad_rows(x, n):
    return jnp.pad(x, ((0, 0), (0, n - x.shape[1]), (0, 0)))


def _row_tile(m):
    for tm in (512, 256, 128, 64, 32, 16, 8):
        if m % tm == 0:
            return tm
    raise ValueError(f"token count {m} is not a multiple of 8")


def kernel(x_prompt, x_sample, cache_a_k, cache_a_v, state_b, cache_c_k, cache_c_v, norm_mix_g, norm_ffn_g, w_in_ab, w_gate_b, b_gate_b, norm_gla_g, w_out_ab, w_qkv_c, rel_bias_c, w_out_c, w_ffn_gate, w_ffn_up, w_ffn_down, norm_final_g):
    bp, tp, d = x_prompt.shape
    bs, ts, _ = x_sample.shape
    depth = norm_mix_g.shape[0]
    past = cache_a_k.shape[2]
    wc = cache_c_k.shape[2]
    assert tp % ATT_BLOCK == 0 and past % ATT_BLOCK == 0 and wc % ATT_BLOCK == 0
    assert ts <= GLA_CHUNK and ts % 8 == 0
    mp, ms = bp * tp, bs * ts
    tmp, tms = _row_tile(mp), _row_tile(ms)
    xp = x_prompt.reshape(mp, d)
    xs = x_sample.reshape(ms, d)
    row2 = lambda a: a.reshape(1, -1)

    a_kp, a_vp, a_ks, a_vs, b_sp, b_ss = [], [], [], [], [], []
    c_kp, c_vp, c_ks, c_vs = [], [], [], []

    for layer in range(depth):
        i = layer // 2
        g_mix = row2(norm_mix_g[layer])
        if layer % 2 == 0:
            w = w_in_ab[i]
            o = 3 * A_WIDTH + 2 * B_KW + B_VW
            w_main = jnp.concatenate([w[:, :o], w[:, o + B_GATE_RANK:]], axis=1).astype(BF16)
            w_lr = jnp.pad(w[:, o:o + B_GATE_RANK], ((0, 0), (0, LANES - B_GATE_RANK))).astype(BF16)
            w_gate = jnp.pad(w_gate_b[i], ((0, LANES - B_GATE_RANK), (0, 0))).astype(BF16)
            b_gate = row2(b_gate_b[i])
            g_gla = row2(norm_gla_g[i])
            w_out = w_out_ab[i].astype(BF16)

            qa, ka, va, kab, vab, qb, kb, vb, r, la = _proj_ab(xp, g_mix, w_main, w_lr, w_gate, b_gate, tmp)
            sh = lambda a: a.reshape(bp, tp, -1)
            oa = _sb_attention_prompt(sh(qa), sh(kab), sh(vab))
            s0 = jnp.zeros((bp, B_HEADS, B_DK, B_DV), F32)
            ob, sbp = _gla(sh(qb), sh(kb), sh(vb), sh(la), s0)
            xp = _merge_ab(xp, oa.reshape(mp, -1), ob.reshape(mp, -1), r, g_gla, w_out, tmp)
            a_kp.append(ka.reshape(bp, tp, A_HEADS, HEAD_DIM))
            a_vp.append(va.reshape(bp, tp, A_HEADS, HEAD_DIM))
            b_sp.append(sbp)

            qa, ka, va, kab, vab, qb, kb, vb, r, la = _proj_ab(xs, g_mix, w_main, w_lr, w_gate, b_gate, tms)
            sh = lambda a: a.reshape(bs, ts, -1)
            oa = _sb_attention_sample(sh(qa), sh(kab), sh(vab), cache_a_k[i].reshape(bs, past, A_WIDTH),
                                      cache_a_v[i].reshape(bs, past, A_WIDTH))
            pad_t = lambda a: _pad_rows(sh(a), GLA_CHUNK)
            ob, sbs = _gla(pad_t(qb), pad_t(kb), pad_t(vb), pad_t(la), state_b[i])
            xs = _merge_ab(xs, oa.reshape(ms, -1), ob[:, :ts].reshape(ms, -1), r, g_gla, w_out, tms)
            a_ks.append(ka.reshape(bs, ts, A_HEADS, HEAD_DIM))
            a_vs.append(va.reshape(bs, ts, A_HEADS, HEAD_DIM))
            b_ss.append(sbs)
        else:
            w_qkv = w_qkv_c[i].astype(BF16)
            w_out = w_out_c[i].astype(BF16)

            q, k, v, kb16, vb16 = _proj_c(xp, g_mix, w_qkv, tmp)
            sh = lambda a: a.reshape(bp, tp, -1)
            bias = _band_bias(rel_bias_c[i], ATT_BLOCK, ATT_BLOCK)
            oc = _band_attention_prompt(sh(q), sh(kb16), sh(vb16), bias)
            xp = _out_c(xp, oc.reshape(mp, -1), w_out, tmp)
            keep = min(C_LEFT_CHUNKS * CHUNK, tp)
            c_kp.append(k.reshape(bp, tp, C_HEADS, HEAD_DIM)[:, tp - keep:])
            c_vp.append(v.reshape(bp, tp, C_HEADS, HEAD_DIM)[:, tp - keep:])

            q, k, v, kb16, vb16 = _proj_c(xs, g_mix, w_qkv, tms)
            sh = lambda a: a.reshape(bs, ts, -1)
            bias = _band_bias(rel_bias_c[i], ts, ts)
            oc = _band_attention_sample(sh(q), sh(kb16), sh(vb16), cache_c_k[i].reshape(bs, wc, C_WIDTH),
                                        cache_c_v[i].reshape(bs, wc, C_WIDTH), bias)
            xs = _out_c(xs, oc.reshape(ms, -1), w_out, tms)
            c_ks.append(k.reshape(bs, ts, C_HEADS, HEAD_DIM))
            c_vs.append(v.reshape(bs, ts, C_HEADS, HEAD_DIM))

        g_ffn = row2(norm_ffn_g[layer])
        wg = w_ffn_gate[layer].astype(BF16)
        wu = w_ffn_up[layer].astype(BF16)
        wd = w_ffn_down[layer].astype(BF16)
        xp = _ffn(xp, g_ffn, wg, wu, wd, tmp)
        xs = _ffn(xs, g_ffn, wg, wu, wd, tms)

    g_fin = row2(norm_final_g)
    y_prompt = _final_norm(xp, g_fin, tmp).reshape(bp, tp, d)
    y_sample = _final_norm(xs, g_fin, tms).reshape(bs, ts, d)
    return (y_prompt, y_sample, jnp.stack(a_kp), jnp.stack(a_vp), jnp.stack(a_ks), jnp.stack(a_vs),
            jnp.stack(b_sp), jnp.stack(b_ss), jnp.stack(c_kp), jnp.stack(c_vp),
            jnp.stack(c_ks), jnp.stack(c_vs))
```

```python
import functools

import jax
import jax.numpy as jnp
from jax import lax
from jax.experimental import pallas as pl
from jax.experimental.pallas import tpu as pltpu

F32 = jnp.float32
BF16 = jnp.bfloat16

EPS = 1e-6
HEAD_DIM = 64
LANES = 128
A_HEADS = 8
A_WIDTH = A_HEADS * HEAD_DIM
B_HEADS = 4
B_DK = 64
B_DV = 128
B_KW = B_HEADS * B_DK
B_VW = B_HEADS * B_DV
B_GATE_RANK = 16
B_GATE_TEMP = 16.0
GLA_CHUNK = 64
C_HEADS = 16
C_WIDTH = C_HEADS * HEAD_DIM
CHUNK = 64
C_LEFT_CHUNKS = 8
REL_MIN = -(CHUNK - 1)
REL_MAX = 128
ATT_BLOCK = 256
NEG_BIG = -1e30
SB_DEAD = 104.0
SB_PAIRS_PER_STEP = 2
BAND_PAIRS_PER_STEP = 2
VMEM_LIMIT = 56 * 1024 * 1024


def _params(*sem):
    return pltpu.CompilerParams(dimension_semantics=sem, vmem_limit_bytes=VMEM_LIMIT)


def _resident(shape):
    return pl.BlockSpec(shape, lambda *_: (0,) * len(shape), pipeline_mode=pl.Buffered(1))


def _rms(x, g):
    return x * lax.rsqrt(jnp.mean(x * x, axis=-1, keepdims=True) + EPS) * g


def _log_sigmoid_pair(z):
    l = jnp.log1p(jnp.exp(-jnp.abs(z)))
    return jnp.minimum(z, 0.0) - l, jnp.minimum(-z, 0.0) - l


def _split_bf16(x):
    hi = x.astype(BF16)
    lo = (x - hi.astype(F32)).astype(BF16)
    return hi, lo


def _dot(a, b):
    return jnp.dot(a, b, preferred_element_type=F32)


def _dot_nt(a, b):
    return lax.dot_general(a, b, (((1,), (1,)), ((), ())), preferred_element_type=F32)


def _dot_tn(a, b):
    return lax.dot_general(a, b, (((0,), (0,)), ((), ())), preferred_element_type=F32)


def _pair_cols(p):
    return slice(p * LANES, (p + 1) * LANES)


class _KV:
    def __init__(self, k, v, feature_major):
        self.k, self.v, self.feature_major = k, v, feature_major
        self.n_keys = k.shape[1] if feature_major else k.shape[0]

    def scores(self, q_h, p):
        if self.feature_major:
            return _dot(q_h, self.k[_pair_cols(p), :])
        return _dot_nt(q_h, self.k[:, _pair_cols(p)])

    def values(self, p):
        return self.v[_pair_cols(p), :] if self.feature_major else self.v[:, _pair_cols(p)]

    def weighted(self, w, v_p):
        return _dot_nt(w, v_p) if self.feature_major else _dot(w, v_p)

    def head_lanes(self):
        shape = (LANES, self.n_keys) if self.feature_major else (self.n_keys, LANES)
        return lax.broadcasted_iota(jnp.int32, shape, 0 if self.feature_major else 1) < HEAD_DIM


def _split_heads(q_ref, n_pairs, tq):
    lo = lax.broadcasted_iota(jnp.int32, (tq, LANES), 1) < HEAD_DIM
    heads = []
    for p in range(n_pairs):
        q = q_ref[:, _pair_cols(p)]
        heads += [jnp.where(lo, q, jnp.zeros_like(q)), jnp.where(lo, jnp.zeros_like(q), q)]
    return heads, lo


def _emit_kv(y, wkv_ref, k_ref, v_ref, kb_ref, vb_ref, feature_major):
    if feature_major:
        kv = _dot_nt(wkv_ref[...], y)
        width = kv.shape[0] // 2
        k, v = kv[:width, :], kv[width:, :]
    else:
        kv = _dot(y, wkv_ref[...])
        width = kv.shape[1] // 2
        k, v = kv[:, :width], kv[:, width:]
    k_ref[...] = k
    v_ref[...] = v
    kb_ref[...] = k.astype(BF16)
    vb_ref[...] = v.astype(BF16)


def _proj_ab_kernel(x_ref, g_ref, w_ref, wkv_ref, wlr_ref, wgate_ref, bgate_ref,
                    qa_ref, ka_ref, va_ref, kab_ref, vab_ref, qb_ref, kb_ref, vb_ref, r_ref, la_ref,
                    *, feature_major):
    y = _rms(x_ref[...], g_ref[...]).astype(BF16)
    _emit_kv(y, wkv_ref, ka_ref, va_ref, kab_ref, vab_ref, feature_major)
    z = _dot(y, w_ref[...])
    c = 0
    qa_ref[...] = (z[:, c:c + A_WIDTH] * (HEAD_DIM ** -0.5)).astype(BF16); c += A_WIDTH
    qb_ref[...] = z[:, c:c + B_KW]; c += B_KW
    kb_ref[...] = z[:, c:c + B_KW]; c += B_KW
    vb_ref[...] = z[:, c:c + B_VW]; c += B_VW
    r_ref[...] = z[:, c:c + B_VW]
    g_lr = _dot(y, wlr_ref[...])
    gate = _dot(g_lr.astype(BF16), wgate_ref[...]) + bgate_ref[...]
    la_ref[...] = _log_sigmoid_pair(gate)[0] * (1.0 / B_GATE_TEMP)


def _kv_out(m, width, tm, batch):
    if batch is None:
        spec = pl.BlockSpec((tm, width), lambda i: (i, 0))
        shape = (m, width)
    else:
        t = m // batch
        assert t % tm == 0
        spec = pl.BlockSpec((None, width, tm), lambda i: (i // (t // tm), 0, i % (t // tm)))
        shape = (batch, width, t)
    return [spec] * 4, [jax.ShapeDtypeStruct(shape, dt) for dt in (F32, F32, BF16, BF16)]


def _proj_ab(x, g, w_main, w_kv, w_lr, w_gate, b_gate, tm, batch=None):
    m, d = x.shape
    row = lambda n: pl.BlockSpec((tm, n), lambda i: (i, 0))
    kv_specs, kv_shapes = _kv_out(m, A_WIDTH, tm, batch)
    rest = [(B_KW, F32), (B_KW, F32), (B_VW, F32), (B_VW, F32), (B_KW, F32)]
    return pl.pallas_call(
        functools.partial(_proj_ab_kernel, feature_major=batch is not None),
        grid=(m // tm,),
        in_specs=[row(d), _resident(g.shape), _resident(w_main.shape), _resident(w_kv.shape),
                  _resident(w_lr.shape), _resident(w_gate.shape), _resident(b_gate.shape)],
        out_specs=[row(A_WIDTH)] + kv_specs + [row(n) for n, _ in rest],
        out_shape=([jax.ShapeDtypeStruct((m, A_WIDTH), BF16)] + kv_shapes
                   + [jax.ShapeDtypeStruct((m, n), dt) for n, dt in rest]),
        compiler_params=_params("parallel"),
        name="proj_ab",
    )(x, g, w_main, w_kv, w_lr, w_gate, b_gate)


def _proj_c_kernel(x_ref, g_ref, wq_ref, wkv_ref, q_ref, k_ref, v_ref, kb_ref, vb_ref, *, feature_major):
    y = _rms(x_ref[...], g_ref[...]).astype(BF16)
    _emit_kv(y, wkv_ref, k_ref, v_ref, kb_ref, vb_ref, feature_major)
    q_ref[...] = (_dot(y, wq_ref[...]) * (HEAD_DIM ** -0.5)).astype(BF16)


def _proj_c(x, g, w_q, w_kv, tm, batch=None):
    m, d = x.shape
    row = lambda n: pl.BlockSpec((tm, n), lambda i: (i, 0))
    kv_specs, kv_shapes = _kv_out(m, C_WIDTH, tm, batch)
    return pl.pallas_call(
        functools.partial(_proj_c_kernel, feature_major=batch is not None),
        grid=(m // tm,),
        in_specs=[row(d), _resident(g.shape), _resident(w_q.shape), _resident(w_kv.shape)],
        out_specs=[row(C_WIDTH)] + kv_specs,
        out_shape=[jax.ShapeDtypeStruct((m, C_WIDTH), BF16)] + kv_shapes,
        compiler_params=_params("parallel"),
        name="proj_c",
    )(x, g, w_q, w_kv)


def _merge_ab_kernel(x_ref, oa_ref, ob_ref, r_ref, g_ref, w_ref, o_ref):
    ob = ob_ref[...]
    parts = []
    for h in range(B_HEADS):
        seg = ob[:, h * B_DV:(h + 1) * B_DV]
        parts.append(seg * lax.rsqrt(jnp.mean(seg * seg, axis=-1, keepdims=True) + EPS))
    r = r_ref[...]
    obn = jnp.concatenate(parts, axis=-1) * g_ref[...] * (r * jax.nn.sigmoid(r))
    o_ref[...] = (x_ref[...] + _dot(oa_ref[...], w_ref[:A_WIDTH, :])
                  + _dot(obn.astype(BF16), w_ref[A_WIDTH:, :]))


def _merge_ab(x, oa, ob, r, g_gla, w_out, tm):
    m, d = x.shape
    row = lambda n: pl.BlockSpec((tm, n), lambda i: (i, 0))
    return pl.pallas_call(
        _merge_ab_kernel,
        grid=(m // tm,),
        in_specs=[row(d), row(A_WIDTH), row(B_VW), row(B_VW), _resident(g_gla.shape),
                  _resident(w_out.shape)],
        out_specs=row(d),
        out_shape=jax.ShapeDtypeStruct((m, d), F32),
        compiler_params=_params("parallel"),
        name="merge_ab",
    )(x, oa, ob, r, g_gla, w_out)


def _out_c_kernel(x_ref, o_ref_in, w_ref, o_ref):
    o_ref[...] = x_ref[...] + _dot(o_ref_in[...], w_ref[...])


def _out_c(x, oc, w_out, tm):
    m, d = x.shape
    row = lambda n: pl.BlockSpec((tm, n), lambda i: (i, 0))
    return pl.pallas_call(
        _out_c_kernel,
        grid=(m // tm,),
        in_specs=[row(d), row(C_WIDTH), _resident(w_out.shape)],
        out_specs=row(d),
        out_shape=jax.ShapeDtypeStruct((m, d), F32),
        compiler_params=_params("parallel"),
        name="out_c",
    )(x, oc, w_out)


def _ffn_kernel(x_ref, g_ref, wg_ref, wu_ref, wd_ref, o_ref):
    x = x_ref[...]
    y = _rms(x, g_ref[...]).astype(BF16)
    h = _dot(y, wg_ref[...])
    u = _dot(y, wu_ref[...])
    a = (h * jax.nn.sigmoid(h) * u).astype(BF16)
    o_ref[...] = x + _dot(a, wd_ref[...])


def _ffn(x, g, wg, wu, wd, tm):
    m, d = x.shape
    row = pl.BlockSpec((tm, d), lambda i: (i, 0))
    return pl.pallas_call(
        _ffn_kernel,
        grid=(m // tm,),
        in_specs=[row, _resident(g.shape), _resident(wg.shape), _resident(wu.shape),
                  _resident(wd.shape)],
        out_specs=row,
        out_shape=jax.ShapeDtypeStruct((m, d), F32),
        compiler_params=_params("parallel"),
        name="ffn",
    )(x, g, wg, wu, wd)


def _final_norm_kernel(x_ref, g_ref, o_ref):
    o_ref[...] = _rms(x_ref[...], g_ref[...])


def _final_norm(x, g, tm):
    m, d = x.shape
    row = pl.BlockSpec((tm, d), lambda i: (i, 0))
    return pl.pallas_call(
        _final_norm_kernel,
        grid=(m // tm,),
        in_specs=[row, _resident(g.shape)],
        out_specs=row,
        out_shape=jax.ShapeDtypeStruct((m, d), F32),
        compiler_params=_params("parallel"),
        name="final_norm",
    )(x, g)


def _sb_core(q_heads, lo, first_kv, first_mask, earlier_kv, n_earlier, acc_ref, c_ref, o_ref, n_pairs):
    tk = first_kv.n_keys
    heads = range(2 * n_pairs)
    later = (lax.broadcasted_iota(jnp.int32, (tk, tk), 0)
             > lax.broadcasted_iota(jnp.int32, (tk, tk), 1)).astype(BF16)
    c_ref[...] = jnp.zeros_like(c_ref)

    def block(kv, mask):
        z = [kv.scores(q_heads[h], h // 2) for h in heads]
        log_beta, drop, after = [], [], []
        for h in heads:
            d = jnp.maximum(z[h], 0.0) + jnp.log(1.0 + jnp.exp(-jnp.abs(z[h])))
            log_beta.append(z[h] - d)
            drop.append(d if mask is None else jnp.where(mask, d, 0.0))
        for h in heads:
            hi, lo_part = _split_bf16(drop[h])
            after.append(_dot(hi, later) + _dot(lo_part, later))
        pv = []
        for h in heads:
            c = c_ref[h]
            w = jnp.exp(log_beta[h] - after[h] - c)
            if mask is not None:
                w = jnp.where(mask, w, 0.0)
            c_ref[h] = c + after[h][:, 0:1] + drop[h][:, 0:1]
            pv.append(kv.weighted(w.astype(BF16), kv.values(h // 2)))
        out = [jnp.where(lo, pv[2 * p], pv[2 * p + 1]) for p in range(n_pairs)]
        return out[0] if n_pairs == 1 else jnp.concatenate(out, axis=-1)

    def all_dead():
        return jnp.min(c_ref[...]) > SB_DEAD

    acc_ref[...] = block(first_kv, first_mask)

    def cond(carry):
        n, dead = carry
        return (n < n_earlier) & jnp.logical_not(dead)

    def body(carry):
        n, _ = carry
        acc_ref[...] += block(earlier_kv(n), None)
        return n + 1, all_dead()

    lax.while_loop(cond, body, (jnp.int32(0), all_dead()))
    o_ref[...] = acc_ref[...].astype(o_ref.dtype)


def _sb_prompt_kernel(q_ref, k_ref, v_ref, o_ref, acc_ref, c_ref, *, tb, n_pairs):
    qi = pl.program_id(2)
    q_heads, lo = _split_heads(q_ref, n_pairs, tb)

    def kv_block(j):
        keys = pl.ds(pl.multiple_of(j * tb, tb), tb)
        return _KV(k_ref[:, keys], v_ref[:, keys], True)

    strictly_earlier = (lax.broadcasted_iota(jnp.int32, (tb, tb), 1)
                        < lax.broadcasted_iota(jnp.int32, (tb, tb), 0))
    _sb_core(q_heads, lo, kv_block(qi), strictly_earlier, lambda n: kv_block(qi - 1 - n), qi,
             acc_ref, c_ref, o_ref, n_pairs)


def _sb_sample_kernel(q_ref, kn_ref, vn_ref, kc_ref, vc_ref, o_ref, acc_ref, c_ref, kpad_ref, vpad_ref,
                      *, ts, tk, n_cache_blocks, n_pairs):
    q_heads, lo = _split_heads(q_ref, n_pairs, ts)
    kpad_ref[...] = jnp.zeros_like(kpad_ref)
    vpad_ref[...] = jnp.zeros_like(vpad_ref)
    kpad_ref[:ts, :] = kn_ref[...]
    vpad_ref[:ts, :] = vn_ref[...]

    def cache_block(n):
        keys = pl.ds(pl.multiple_of((n_cache_blocks - 1 - n) * tk, tk), tk)
        return _KV(kc_ref[:, keys].astype(BF16), vc_ref[:, keys].astype(BF16), True)

    strictly_earlier = (lax.broadcasted_iota(jnp.int32, (ts, tk), 1)
                        < lax.broadcasted_iota(jnp.int32, (ts, tk), 0))
    _sb_core(q_heads, lo, _KV(kpad_ref[...], vpad_ref[...], False), strictly_earlier, cache_block,
             n_cache_blocks, acc_ref, c_ref, o_ref, n_pairs)


def _sb_attention_prompt(q, k, v):
    b, t, _ = q.shape
    tb = ATT_BLOCK
    assert t % tb == 0
    n_pairs = SB_PAIRS_PER_STEP
    w = n_pairs * LANES
    kv_spec = pl.BlockSpec((None, w, t), lambda bi, hp, qi: (bi, hp, 0))
    q_spec = pl.BlockSpec((None, tb, w), lambda bi, hp, qi: (bi, qi, hp))
    return pl.pallas_call(
        functools.partial(_sb_prompt_kernel, tb=tb, n_pairs=n_pairs),
        grid=(b, A_WIDTH // w, t // tb),
        in_specs=[q_spec, kv_spec, kv_spec],
        out_specs=q_spec,
        out_shape=jax.ShapeDtypeStruct(q.shape, BF16),
        scratch_shapes=[pltpu.VMEM((tb, w), F32), pltpu.VMEM((2 * n_pairs, tb, 1), F32)],
        compiler_params=_params("parallel", "parallel", "parallel"),
        name="sb_attention",
    )(q, k, v)


def _sb_attention_sample(q, k_new, v_new, k_cache, v_cache):
    b, ts, _ = q.shape
    past = k_cache.shape[2]
    tk = ATT_BLOCK
    assert past % tk == 0 and ts <= tk
    n_pairs = SB_PAIRS_PER_STEP
    w = n_pairs * LANES
    new_spec = pl.BlockSpec((None, ts, w), lambda bi, hp: (bi, 0, hp))
    cache_spec = pl.BlockSpec((None, w, past), lambda bi, hp: (bi, hp, 0))
    return pl.pallas_call(
        functools.partial(_sb_sample_kernel, ts=ts, tk=tk, n_cache_blocks=past // tk, n_pairs=n_pairs),
        grid=(b, A_WIDTH // w),
        in_specs=[new_spec, new_spec, new_spec, cache_spec, cache_spec],
        out_specs=new_spec,
        out_shape=jax.ShapeDtypeStruct(q.shape, BF16),
        scratch_shapes=[pltpu.VMEM((ts, w), F32), pltpu.VMEM((2 * n_pairs, ts, 1), F32),
                        pltpu.VMEM((tk, w), BF16), pltpu.VMEM((tk, w), BF16)],
        compiler_params=_params("parallel", "parallel"),
        name="sb_attention_sample",
    )(q, k_new, v_new, k_cache, v_cache)


def _band_core(q_heads, lo, kv, pens, bias_ref, o_ref, n_pairs):
    heads = range(2 * n_pairs)
    blocks = range(len(kv))
    z = [[kv[i].scores(q_heads[h], h // 2) + bias_ref[h, i] for i in blocks] for h in heads]
    acc = []
    for h in heads:
        m = None
        for i in blocks:
            mi = jnp.max(z[h][i], axis=-1, keepdims=True)
            if pens[i] is not None:
                mi = mi + pens[i]
            m = mi if m is None else jnp.maximum(m, mi)
        a = None
        for i in blocks:
            shift = m if pens[i] is None else m - pens[i]
            p = jnp.exp(z[h][i] - shift).astype(BF16)
            v = kv[i].values(h // 2)
            ones = jnp.ones_like(v)
            first = kv[i].head_lanes()
            v = jnp.where(first, v, ones) if h % 2 == 0 else jnp.where(first, ones, v)
            pv = kv[i].weighted(p, v)
            a = pv if a is None else a + pv
        acc.append(a)
    for p in range(n_pairs):
        a0, a1 = acc[2 * p], acc[2 * p + 1]
        o_ref[:, _pair_cols(p)] = jnp.where(lo, a0 / pltpu.roll(a0, HEAD_DIM, axis=1),
                                            a1 / pltpu.roll(a1, HEAD_DIM, axis=1)).astype(o_ref.dtype)


def _band_prompt_kernel(q_ref, k_ref, v_ref, bias_ref, o_ref, *, tq, tk, n_pairs):
    qi = pl.program_id(2)
    q_heads, lo = _split_heads(q_ref, n_pairs, tq)
    kv, pens = [], []
    for dj in range(3):
        j = qi - dj
        pens.append(None if dj == 0 else jnp.where(j >= 0, 0.0, NEG_BIG).astype(F32))
        keys = pl.ds(pl.multiple_of(jnp.maximum(j, 0) * tk, tk), tk)
        kv.append(_KV(k_ref[:, keys], v_ref[:, keys], True))
    _band_core(q_heads, lo, kv, pens, bias_ref, o_ref, n_pairs)


def _band_sample_kernel(q_ref, kn_ref, vn_ref, kc_ref, vc_ref, bias_ref, o_ref, kpad_ref, vpad_ref,
                        *, ts, tk, n_cache_blocks, n_pairs):
    q_heads, lo = _split_heads(q_ref, n_pairs, ts)
    kpad_ref[...] = jnp.zeros_like(kpad_ref)
    vpad_ref[...] = jnp.zeros_like(vpad_ref)
    kpad_ref[:ts, :] = kn_ref[...]
    vpad_ref[:ts, :] = vn_ref[...]
    kv = [_KV(kpad_ref[...], vpad_ref[...], False)]
    for dj in range(1, n_cache_blocks + 1):
        keys = slice((n_cache_blocks - dj) * tk, (n_cache_blocks - dj + 1) * tk)
        kv.append(_KV(kc_ref[:, keys].astype(BF16), vc_ref[:, keys].astype(BF16), True))
    _band_core(q_heads, lo, kv, [None] * len(kv), bias_ref, o_ref, n_pairs)


def _band_attention_prompt(q, k, v, bias):
    b, t, _ = q.shape
    tq = tk = ATT_BLOCK
    assert t % tk == 0
    n_pairs = BAND_PAIRS_PER_STEP
    w = n_pairs * LANES
    kern = functools.partial(_band_prompt_kernel, tq=tq, tk=tk, n_pairs=n_pairs)
    kv_spec = pl.BlockSpec((None, w, t), lambda bi, hp, qi: (bi, hp, 0))
    q_spec = pl.BlockSpec((None, tq, w), lambda bi, hp, qi: (bi, qi, hp))
    bias_spec = pl.BlockSpec((2 * n_pairs, 3, tq, tk), lambda bi, hp, qi: (hp, 0, 0, 0))
    return pl.pallas_call(
        kern,
        grid=(b, C_WIDTH // w, t // tq),
        in_specs=[q_spec, kv_spec, kv_spec, bias_spec],
        out_specs=q_spec,
        out_shape=jax.ShapeDtypeStruct(q.shape, BF16),
        compiler_params=_params("parallel", "parallel", "parallel"),
        name="band_attention",
    )(q, k, v, bias)


def _band_attention_sample(q, k_new, v_new, k_cache, v_cache, bias):
    b, ts, _ = q.shape
    wc = k_cache.shape[2]
    tk = ATT_BLOCK
    n_cache_blocks = min(wc // tk, 2)
    assert wc % (n_cache_blocks * tk) == 0 and ts <= tk
    n_pairs = BAND_PAIRS_PER_STEP
    w = n_pairs * LANES
    kern = functools.partial(_band_sample_kernel, ts=ts, tk=tk, n_cache_blocks=n_cache_blocks,
                             n_pairs=n_pairs)
    new_spec = pl.BlockSpec((None, ts, w), lambda bi, hp: (bi, 0, hp))
    cache_rows = n_cache_blocks * tk
    cache_spec = pl.BlockSpec((None, w, cache_rows), lambda bi, hp: (bi, hp, wc // cache_rows - 1))
    bias_spec = pl.BlockSpec((2 * n_pairs, 1 + n_cache_blocks, ts, tk), lambda bi, hp: (hp, 0, 0, 0))
    return pl.pallas_call(
        kern,
        grid=(b, C_WIDTH // w),
        in_specs=[new_spec, new_spec, new_spec, cache_spec, cache_spec, bias_spec],
        out_specs=new_spec,
        out_shape=jax.ShapeDtypeStruct(q.shape, BF16),
        scratch_shapes=[pltpu.VMEM((tk, w), BF16), pltpu.VMEM((tk, w), BF16)],
        compiler_params=_params("parallel", "parallel"),
        name="band_attention_sample",
    )(q, k_new, v_new, k_cache, v_cache, bias)


def _band_bias_kernel(g_ref, o_ref, *, rows, valid0):
    tk = ATT_BLOCK
    q_chunk = lax.broadcasted_iota(jnp.int32, (rows, tk), 0) // CHUNK
    col = lax.broadcasted_iota(jnp.int32, (rows, tk), 1)
    k_chunk = col // CHUNK
    for dj in range(3):
        g = jnp.broadcast_to(g_ref[0, dj], (rows, 2 * tk))
        tile = pltpu.roll(g, 0, axis=1, stride=1, stride_axis=0)[:, :tk]
        diff = dj * (tk // CHUNK) + q_chunk - k_chunk
        seen = (diff >= 0) & (diff <= C_LEFT_CHUNKS)
        if dj == 0:
            seen = seen & (col < valid0)
        o_ref[0, dj] = jnp.where(seen, tile, NEG_BIG)


def _band_bias(rel_table, rows, valid0):
    tk = ATT_BLOCK
    c = jnp.arange(2 * tk, dtype=jnp.int32)
    u = jnp.where(c <= tk, -c, 2 * tk - c)
    idx = jnp.clip(jnp.arange(3, dtype=jnp.int32)[:, None] * tk + u[None, :], REL_MIN, REL_MAX) - REL_MIN
    g = rel_table[:, idx].astype(F32).reshape(C_HEADS, 3, 1, 2 * tk)
    return pl.pallas_call(
        functools.partial(_band_bias_kernel, rows=rows, valid0=valid0),
        grid=(C_HEADS,),
        in_specs=[pl.BlockSpec((1, 3, 1, 2 * tk), lambda h: (h, 0, 0, 0))],
        out_specs=pl.BlockSpec((1, 3, rows, tk), lambda h: (h, 0, 0, 0)),
        out_shape=jax.ShapeDtypeStruct((C_HEADS, 3, rows, tk), F32),
        compiler_params=_params("parallel"),
        name="band_bias",
    )(g)


def _gla_kernel(q_ref, k_ref, v_ref, la_ref, s0_ref, o_ref, s_out_ref, st_ref, *, n_chunks):
    L = GLA_CHUNK
    row = lax.broadcasted_iota(jnp.int32, (L, L), 0)
    colm = lax.broadcasted_iota(jnp.int32, (L, L), 1)
    tri = (colm <= row).astype(BF16)
    causal = colm <= row
    lane = lax.broadcasted_iota(jnp.int32, (L, LANES), 1)
    sub = lax.broadcasted_iota(jnp.int32, (LANES, B_DV), 0)
    zeros_state = jnp.zeros((B_DK, B_DV), F32)
    st_ref[0] = jnp.concatenate([s0_ref[0], zeros_state], axis=0)
    st_ref[1] = jnp.concatenate([zeros_state, s0_ref[1]], axis=0)

    def body(c, carry):
        rows = pl.ds(pl.multiple_of(c * L, L), L)
        q = q_ref[rows, :] * (B_DK ** -0.5)
        k = k_ref[rows, :]
        g_hi, g_lo = _split_bf16(la_ref[rows, :])
        b = _dot(tri, g_hi) + _dot(tri, g_lo)
        qg = q * jnp.exp(b)
        kg = (k * jnp.exp(-b)).astype(BF16)
        b_t = b.T
        b_last = b_t[:, L - 1:L]
        kd_t = k.T * jnp.exp(b_last - b_t)
        decay = jnp.exp(b_last)
        for h in range(2):
            mine = (lane >= h * B_DK) & (lane < (h + 1) * B_DK)
            qg_h = jnp.where(mine, qg, 0.0).astype(BF16)
            att = jnp.where(causal, _dot_nt(qg_h, kg), 0.0)
            v_h = v_ref[rows, h * B_DV:(h + 1) * B_DV].astype(BF16)
            s_h = st_ref[h]
            o_ref[rows, h * B_DV:(h + 1) * B_DV] = _dot(att.astype(BF16), v_h) + _dot(qg_h, s_h.astype(BF16))
            upd = _dot(kd_t.astype(BF16), v_h)
            mine_rows = (sub >= h * B_DK) & (sub < (h + 1) * B_DK)
            st_ref[h] = decay * s_h + jnp.where(mine_rows, upd, 0.0)
        return carry

    lax.fori_loop(0, n_chunks, body, 0)
    s_out_ref[0] = st_ref[0][:B_DK, :]
    s_out_ref[1] = st_ref[1][B_DK:, :]


def _gla(q, k, v, la, s0):
    b, t, _ = q.shape
    qk_spec = pl.BlockSpec((None, t, LANES), lambda bi, p: (bi, 0, p))
    v_spec = pl.BlockSpec((None, t, 2 * B_DV), lambda bi, p: (bi, 0, p))
    s_spec = pl.BlockSpec((None, 2, B_DK, B_DV), lambda bi, p: (bi, p, 0, 0))
    return pl.pallas_call(
        functools.partial(_gla_kernel, n_chunks=t // GLA_CHUNK),
        grid=(b, B_HEADS // 2),
        in_specs=[qk_spec, qk_spec, v_spec, qk_spec, s_spec],
        out_specs=[v_spec, s_spec],
        out_shape=[jax.ShapeDtypeStruct(v.shape, F32), jax.ShapeDtypeStruct(s0.shape, F32)],
        scratch_shapes=[pltpu.VMEM((2, LANES, B_DV), F32)],
        compiler_params=_params("parallel", "parallel"),
        name="gla",
    )(q, k, v, la, s0)


def _pad_rows(x, n):
    return jnp.pad(x, ((0, 0), (0, n - x.shape[1]), (0, 0)))


def _heads_last(x, heads):
    b, _, s = x.shape
    return jnp.transpose(x.reshape(b, heads, HEAD_DIM, s), (0, 3, 1, 2))


def _feature_major(cache):
    b, s, heads, hd = cache.shape
    return jnp.transpose(cache, (0, 2, 3, 1)).reshape(b, heads * hd, s)


def _row_tile(m):
    for tm in (512, 256, 128, 64, 32, 16, 8):
        if m % tm == 0:
            return tm
    raise ValueError(f"token count {m} is not a multiple of 8")


def kernel(x_prompt, x_sample, cache_a_k, cache_a_v, state_b, cache_c_k, cache_c_v, norm_mix_g, norm_ffn_g, w_in_ab, w_gate_b, b_gate_b, norm_gla_g, w_out_ab, w_qkv_c, rel_bias_c, w_out_c, w_ffn_gate, w_ffn_up, w_ffn_down, norm_final_g):
    bp, tp, d = x_prompt.shape
    bs, ts, _ = x_sample.shape
    depth = norm_mix_g.shape[0]
    past = cache_a_k.shape[2]
    wc = cache_c_k.shape[2]
    assert tp % ATT_BLOCK == 0 and past % ATT_BLOCK == 0 and wc % ATT_BLOCK == 0
    assert ts <= GLA_CHUNK and ts % 8 == 0
    mp, ms = bp * tp, bs * ts
    tmp, tms = _row_tile(tp), _row_tile(ms)
    xp = x_prompt.reshape(mp, d)
    xs = x_sample.reshape(ms, d)
    row2 = lambda a: a.reshape(1, -1)

    a_kp, a_vp, a_ks, a_vs, b_sp, b_ss = [], [], [], [], [], []
    c_kp, c_vp, c_ks, c_vs = [], [], [], []

    for layer in range(depth):
        i = layer // 2
        g_mix = row2(norm_mix_g[layer])
        if layer % 2 == 0:
            w = w_in_ab[i]
            kv0, kv1 = A_WIDTH, 3 * A_WIDTH
            o = 3 * A_WIDTH + 2 * B_KW + B_VW
            w_main = jnp.concatenate([w[:, :kv0], w[:, kv1:o], w[:, o + B_GATE_RANK:]], axis=1).astype(BF16)
            w_kv = w[:, kv0:kv1].astype(BF16)
            w_kv_t = w.T[kv0:kv1].astype(BF16)
            w_lr = jnp.pad(w[:, o:o + B_GATE_RANK], ((0, 0), (0, LANES - B_GATE_RANK))).astype(BF16)
            w_gate = jnp.pad(w_gate_b[i], ((0, LANES - B_GATE_RANK), (0, 0))).astype(BF16)
            b_gate = row2(b_gate_b[i])
            g_gla = row2(norm_gla_g[i])
            w_out = w_out_ab[i].astype(BF16)

            qa, ka, va, kab, vab, qb, kb, vb, r, la = _proj_ab(
                xp, g_mix, w_main, w_kv_t, w_lr, w_gate, b_gate, tmp, batch=bp)
            sh = lambda a: a.reshape(bp, tp, -1)
            oa = _sb_attention_prompt(sh(qa), kab, vab)
            s0 = jnp.zeros((bp, B_HEADS, B_DK, B_DV), F32)
            ob, sbp = _gla(sh(qb), sh(kb), sh(vb), sh(la), s0)
            xp = _merge_ab(xp, oa.reshape(mp, -1), ob.reshape(mp, -1), r, g_gla, w_out, tmp)
            a_kp.append(_heads_last(ka, A_HEADS))
            a_vp.append(_heads_last(va, A_HEADS))
            b_sp.append(sbp)

            qa, ka, va, kab, vab, qb, kb, vb, r, la = _proj_ab(
                xs, g_mix, w_main, w_kv, w_lr, w_gate, b_gate, tms)
            sh = lambda a: a.reshape(bs, ts, -1)
            oa = _sb_attention_sample(sh(qa), sh(kab), sh(vab), _feature_major(cache_a_k[i]),
                                      _feature_major(cache_a_v[i]))
            pad_t = lambda a: _pad_rows(sh(a), GLA_CHUNK)
            ob, sbs = _gla(pad_t(qb), pad_t(kb), pad_t(vb), pad_t(la), state_b[i])
            xs = _merge_ab(xs, oa.reshape(ms, -1), ob[:, :ts].reshape(ms, -1), r, g_gla, w_out, tms)
            a_ks.append(ka.reshape(bs, ts, A_HEADS, HEAD_DIM))
            a_vs.append(va.reshape(bs, ts, A_HEADS, HEAD_DIM))
            b_ss.append(sbs)
        else:
            w_q = w_qkv_c[i][:, :C_WIDTH].astype(BF16)
            w_kv = w_qkv_c[i][:, C_WIDTH:].astype(BF16)
            w_kv_t = w_qkv_c[i][:, C_WIDTH:].T.astype(BF16)
            w_out = w_out_c[i].astype(BF16)

            q, k, v, kb16, vb16 = _proj_c(xp, g_mix, w_q, w_kv_t, tmp, batch=bp)
            bias = _band_bias(rel_bias_c[i], ATT_BLOCK, ATT_BLOCK)
            oc = _band_attention_prompt(q.reshape(bp, tp, -1), kb16, vb16, bias)
            xp = _out_c(xp, oc.reshape(mp, -1), w_out, tmp)
            keep = min(C_LEFT_CHUNKS * CHUNK, tp)
            c_kp.append(_heads_last(k[:, :, tp - keep:], C_HEADS))
            c_vp.append(_heads_last(v[:, :, tp - keep:], C_HEADS))

            q, k, v, kb16, vb16 = _proj_c(xs, g_mix, w_q, w_kv, tms)
            sh = lambda a: a.reshape(bs, ts, -1)
            bias = _band_bias(rel_bias_c[i], ts, ts)
            oc = _band_attention_sample(sh(q), sh(kb16), sh(vb16), _feature_major(cache_c_k[i]),
                                        _feature_major(cache_c_v[i]), bias)
            xs = _out_c(xs, oc.reshape(ms, -1), w_out, tms)
            c_ks.append(k.reshape(bs, ts, C_HEADS, HEAD_DIM))
            c_vs.append(v.reshape(bs, ts, C_HEADS, HEAD_DIM))

        g_ffn = row2(norm_ffn_g[layer])
        wg = w_ffn_gate[layer].astype(BF16)
        wu = w_ffn_up[layer].astype(BF16)
        wd = w_ffn_down[layer].astype(BF16)
        xp = _ffn(xp, g_ffn, wg, wu, wd, tmp)
        xs = _ffn(xs, g_ffn, wg, wu, wd, tms)

    g_fin = row2(norm_final_g)
    y_prompt = _final_norm(xp, g_fin, tmp).reshape(bp, tp, d)
    y_sample = _final_norm(xs, g_fin, tms).reshape(bs, ts, d)
    return (y_prompt, y_sample, jnp.stack(a_kp), jnp.stack(a_vp), jnp.stack(a_ks), jnp.stack(a_vs),
            jnp.stack(b_sp), jnp.stack(b_ss), jnp.stack(c_kp), jnp.stack(c_vp),
            jnp.stack(c_ks), jnp.stack(c_vs))
```

```python
import functools

import jax
import jax.numpy as jnp
from jax import lax
from jax.experimental import pallas as pl
from jax.experimental.pallas import tpu as pltpu

F32 = jnp.float32
BF16 = jnp.bfloat16

EPS = 1e-6
HEAD_DIM = 64
LANES = 128
A_HEADS = 8
A_WIDTH = A_HEADS * HEAD_DIM
B_HEADS = 4
B_DK = 64
B_DV = 128
B_KW = B_HEADS * B_DK
B_VW = B_HEADS * B_DV
B_GATE_RANK = 16
B_GATE_TEMP = 16.0
GLA_CHUNK = 64
C_HEADS = 16
C_WIDTH = C_HEADS * HEAD_DIM
CHUNK = 64
C_LEFT_CHUNKS = 8
REL_MIN = -(CHUNK - 1)
REL_MAX = 128
ATT_BLOCK = 256
NEG_BIG = -1e30
SB_DEAD = 104.0
SB_PAIRS_PER_STEP = 2
BAND_PAIRS_PER_STEP = 2
GLA_GROUP = 4
VMEM_LIMIT = 56 * 1024 * 1024


def _params(*sem):
    return pltpu.CompilerParams(dimension_semantics=sem, vmem_limit_bytes=VMEM_LIMIT)


def _resident(shape):
    return pl.BlockSpec(shape, lambda *_: (0,) * len(shape), pipeline_mode=pl.Buffered(1))


def _rms(x, g):
    return x * lax.rsqrt(jnp.mean(x * x, axis=-1, keepdims=True) + EPS) * g


def _log_sigmoid_pair(z):
    l = jnp.log1p(jnp.exp(-jnp.abs(z)))
    return jnp.minimum(z, 0.0) - l, jnp.minimum(-z, 0.0) - l


def _split_bf16(x):
    hi = x.astype(BF16)
    lo = (x - hi.astype(F32)).astype(BF16)
    return hi, lo


def _dot(a, b):
    return jnp.dot(a, b, preferred_element_type=F32)


def _dot_nt(a, b):
    return lax.dot_general(a, b, (((1,), (1,)), ((), ())), preferred_element_type=F32)


def _dot_tn(a, b):
    return lax.dot_general(a, b, (((0,), (0,)), ((), ())), preferred_element_type=F32)


def _pair_cols(p):
    return slice(p * LANES, (p + 1) * LANES)


class _KV:
    def __init__(self, k, v, feature_major):
        self.k, self.v, self.feature_major = k, v, feature_major
        self.n_keys = k.shape[1] if feature_major else k.shape[0]

    def scores(self, q_h, p):
        if self.feature_major:
            return _dot(q_h, self.k[_pair_cols(p), :])
        return _dot_nt(q_h, self.k[:, _pair_cols(p)])

    def values(self, p):
        return self.v[_pair_cols(p), :] if self.feature_major else self.v[:, _pair_cols(p)]

    def weighted(self, w, v_p):
        return _dot_nt(w, v_p) if self.feature_major else _dot(w, v_p)

    def head_lanes(self):
        shape = (LANES, self.n_keys) if self.feature_major else (self.n_keys, LANES)
        return lax.broadcasted_iota(jnp.int32, shape, 0 if self.feature_major else 1) < HEAD_DIM


def _split_heads(q_ref, n_pairs, tq):
    lo = lax.broadcasted_iota(jnp.int32, (tq, LANES), 1) < HEAD_DIM
    heads = []
    for p in range(n_pairs):
        q = q_ref[:, _pair_cols(p)]
        heads += [jnp.where(lo, q, jnp.zeros_like(q)), jnp.where(lo, jnp.zeros_like(q), q)]
    return heads, lo


def _emit_kv(y, wkv_ref, k_ref, v_ref, kb_ref, vb_ref, feature_major):
    if feature_major:
        kv = _dot_nt(wkv_ref[...], y)
        width = kv.shape[0] // 2
        k, v = kv[:width, :], kv[width:, :]
    else:
        kv = _dot(y, wkv_ref[...])
        width = kv.shape[1] // 2
        k, v = kv[:, :width], kv[:, width:]
    k_ref[...] = k
    v_ref[...] = v
    kb_ref[...] = k.astype(BF16)
    vb_ref[...] = v.astype(BF16)


def _proj_ab_kernel(x_ref, g_ref, w_ref, wkv_ref, wlr_ref, wgate_ref, bgate_ref,
                    qa_ref, ka_ref, va_ref, kab_ref, vab_ref, qb_ref, kb_ref, vb_ref, r_ref, la_ref,
                    *, feature_major):
    y = _rms(x_ref[...], g_ref[...]).astype(BF16)
    _emit_kv(y, wkv_ref, ka_ref, va_ref, kab_ref, vab_ref, feature_major)
    z = _dot(y, w_ref[...])
    c = 0
    qa_ref[...] = (z[:, c:c + A_WIDTH] * (HEAD_DIM ** -0.5)).astype(BF16); c += A_WIDTH
    qb_ref[...] = z[:, c:c + B_KW]; c += B_KW
    kb_ref[...] = z[:, c:c + B_KW]; c += B_KW
    vb_ref[...] = z[:, c:c + B_VW]; c += B_VW
    r_ref[...] = z[:, c:c + B_VW]
    g_lr = _dot(y, wlr_ref[...])
    gate = _dot(g_lr.astype(BF16), wgate_ref[...]) + bgate_ref[...]
    la_ref[...] = _log_sigmoid_pair(gate)[0] * (1.0 / B_GATE_TEMP)


def _kv_out(m, width, tm, batch):
    if batch is None:
        spec = pl.BlockSpec((tm, width), lambda i: (i, 0))
        shape = (m, width)
    else:
        t = m // batch
        assert t % tm == 0
        spec = pl.BlockSpec((None, width, tm), lambda i: (i // (t // tm), 0, i % (t // tm)))
        shape = (batch, width, t)
    return [spec] * 4, [jax.ShapeDtypeStruct(shape, dt) for dt in (F32, F32, BF16, BF16)]


def _proj_ab(x, g, w_main, w_kv, w_lr, w_gate, b_gate, tm, batch=None):
    m, d = x.shape
    row = lambda n: pl.BlockSpec((tm, n), lambda i: (i, 0))
    kv_specs, kv_shapes = _kv_out(m, A_WIDTH, tm, batch)
    rest = [(B_KW, F32), (B_KW, F32), (B_VW, F32), (B_VW, F32), (B_KW, F32)]
    return pl.pallas_call(
        functools.partial(_proj_ab_kernel, feature_major=batch is not None),
        grid=(m // tm,),
        in_specs=[row(d), _resident(g.shape), _resident(w_main.shape), _resident(w_kv.shape),
                  _resident(w_lr.shape), _resident(w_gate.shape), _resident(b_gate.shape)],
        out_specs=[row(A_WIDTH)] + kv_specs + [row(n) for n, _ in rest],
        out_shape=([jax.ShapeDtypeStruct((m, A_WIDTH), BF16)] + kv_shapes
                   + [jax.ShapeDtypeStruct((m, n), dt) for n, dt in rest]),
        compiler_params=_params("parallel"),
        name="proj_ab",
    )(x, g, w_main, w_kv, w_lr, w_gate, b_gate)


def _proj_c_kernel(x_ref, g_ref, wq_ref, wkv_ref, q_ref, k_ref, v_ref, kb_ref, vb_ref, *, feature_major):
    y = _rms(x_ref[...], g_ref[...]).astype(BF16)
    _emit_kv(y, wkv_ref, k_ref, v_ref, kb_ref, vb_ref, feature_major)
    q_ref[...] = (_dot(y, wq_ref[...]) * (HEAD_DIM ** -0.5)).astype(BF16)


def _proj_c(x, g, w_q, w_kv, tm, batch=None):
    m, d = x.shape
    row = lambda n: pl.BlockSpec((tm, n), lambda i: (i, 0))
    kv_specs, kv_shapes = _kv_out(m, C_WIDTH, tm, batch)
    return pl.pallas_call(
        functools.partial(_proj_c_kernel, feature_major=batch is not None),
        grid=(m // tm,),
        in_specs=[row(d), _resident(g.shape), _resident(w_q.shape), _resident(w_kv.shape)],
        out_specs=[row(C_WIDTH)] + kv_specs,
        out_shape=[jax.ShapeDtypeStruct((m, C_WIDTH), BF16)] + kv_shapes,
        compiler_params=_params("parallel"),
        name="proj_c",
    )(x, g, w_q, w_kv)


def _merge_ab_kernel(x_ref, oa_ref, ob_ref, r_ref, g_ref, w_ref, o_ref):
    ob = ob_ref[...]
    parts = []
    for h in range(B_HEADS):
        seg = ob[:, h * B_DV:(h + 1) * B_DV]
        parts.append(seg * lax.rsqrt(jnp.mean(seg * seg, axis=-1, keepdims=True) + EPS))
    r = r_ref[...]
    obn = jnp.concatenate(parts, axis=-1) * g_ref[...] * (r * jax.nn.sigmoid(r))
    o_ref[...] = (x_ref[...] + _dot(oa_ref[...], w_ref[:A_WIDTH, :])
                  + _dot(obn.astype(BF16), w_ref[A_WIDTH:, :]))


def _merge_ab(x, oa, ob, r, g_gla, w_out, tm):
    m, d = x.shape
    row = lambda n: pl.BlockSpec((tm, n), lambda i: (i, 0))
    return pl.pallas_call(
        _merge_ab_kernel,
        grid=(m // tm,),
        in_specs=[row(d), row(A_WIDTH), row(B_VW), row(B_VW), _resident(g_gla.shape),
                  _resident(w_out.shape)],
        out_specs=row(d),
        out_shape=jax.ShapeDtypeStruct((m, d), F32),
        compiler_params=_params("parallel"),
        name="merge_ab",
    )(x, oa, ob, r, g_gla, w_out)


def _out_c_kernel(x_ref, o_ref_in, w_ref, o_ref):
    o_ref[...] = x_ref[...] + _dot(o_ref_in[...], w_ref[...])


def _out_c(x, oc, w_out, tm):
    m, d = x.shape
    row = lambda n: pl.BlockSpec((tm, n), lambda i: (i, 0))
    return pl.pallas_call(
        _out_c_kernel,
        grid=(m // tm,),
        in_specs=[row(d), row(C_WIDTH), _resident(w_out.shape)],
        out_specs=row(d),
        out_shape=jax.ShapeDtypeStruct((m, d), F32),
        compiler_params=_params("parallel"),
        name="out_c",
    )(x, oc, w_out)


def _ffn_kernel(x_ref, g_ref, wg_ref, wu_ref, wd_ref, o_ref):
    x = x_ref[...]
    y = _rms(x, g_ref[...]).astype(BF16)
    h = _dot(y, wg_ref[...])
    u = _dot(y, wu_ref[...])
    a = (h * jax.nn.sigmoid(h) * u).astype(BF16)
    o_ref[...] = x + _dot(a, wd_ref[...])


def _ffn(x, g, wg, wu, wd, tm):
    m, d = x.shape
    row = pl.BlockSpec((tm, d), lambda i: (i, 0))
    return pl.pallas_call(
        _ffn_kernel,
        grid=(m // tm,),
        in_specs=[row, _resident(g.shape), _resident(wg.shape), _resident(wu.shape),
                  _resident(wd.shape)],
        out_specs=row,
        out_shape=jax.ShapeDtypeStruct((m, d), F32),
        compiler_params=_params("parallel"),
        name="ffn",
    )(x, g, wg, wu, wd)


def _final_norm_kernel(x_ref, g_ref, o_ref):
    o_ref[...] = _rms(x_ref[...], g_ref[...])


def _final_norm(x, g, tm):
    m, d = x.shape
    row = pl.BlockSpec((tm, d), lambda i: (i, 0))
    return pl.pallas_call(
        _final_norm_kernel,
        grid=(m // tm,),
        in_specs=[row, _resident(g.shape)],
        out_specs=row,
        out_shape=jax.ShapeDtypeStruct((m, d), F32),
        compiler_params=_params("parallel"),
        name="final_norm",
    )(x, g)


def _sb_core(q_heads, lo, first_kv, first_mask, earlier_kv, n_earlier, acc_ref, c_ref, o_ref, n_pairs):
    tk = first_kv.n_keys
    heads = range(2 * n_pairs)
    later = (lax.broadcasted_iota(jnp.int32, (tk, tk), 0)
             > lax.broadcasted_iota(jnp.int32, (tk, tk), 1)).astype(BF16)
    c_ref[...] = jnp.zeros_like(c_ref)

    def block(kv, mask):
        z = [kv.scores(q_heads[h], h // 2) for h in heads]
        log_beta, drop, after = [], [], []
        for h in heads:
            d = jnp.maximum(z[h], 0.0) + jnp.log(1.0 + jnp.exp(-jnp.abs(z[h])))
            log_beta.append(z[h] - d)
            drop.append(d if mask is None else jnp.where(mask, d, 0.0))
        for h in heads:
            hi, lo_part = _split_bf16(drop[h])
            after.append(_dot(hi, later) + _dot(lo_part, later))
        pv = []
        for h in heads:
            c = c_ref[h]
            w = jnp.exp(log_beta[h] - after[h] - c)
            if mask is not None:
                w = jnp.where(mask, w, 0.0)
            c_ref[h] = c + after[h][:, 0:1] + drop[h][:, 0:1]
            pv.append(kv.weighted(w.astype(BF16), kv.values(h // 2)))
        out = [jnp.where(lo, pv[2 * p], pv[2 * p + 1]) for p in range(n_pairs)]
        return out[0] if n_pairs == 1 else jnp.concatenate(out, axis=-1)

    def all_dead():
        return jnp.min(c_ref[...]) > SB_DEAD

    acc_ref[...] = block(first_kv, first_mask)

    def cond(carry):
        n, dead = carry
        return (n < n_earlier) & jnp.logical_not(dead)

    def body(carry):
        n, _ = carry
        acc_ref[...] += block(earlier_kv(n), None)
        return n + 1, all_dead()

    lax.while_loop(cond, body, (jnp.int32(0), all_dead()))
    o_ref[...] = acc_ref[...].astype(o_ref.dtype)


def _sb_prompt_kernel(q_ref, k_ref, v_ref, o_ref, acc_ref, c_ref, *, tb, n_pairs):
    qi = pl.program_id(2)
    q_heads, lo = _split_heads(q_ref, n_pairs, tb)

    def kv_block(j):
        keys = pl.ds(pl.multiple_of(j * tb, tb), tb)
        return _KV(k_ref[:, keys], v_ref[:, keys], True)

    strictly_earlier = (lax.broadcasted_iota(jnp.int32, (tb, tb), 1)
                        < lax.broadcasted_iota(jnp.int32, (tb, tb), 0))
    _sb_core(q_heads, lo, kv_block(qi), strictly_earlier, lambda n: kv_block(qi - 1 - n), qi,
             acc_ref, c_ref, o_ref, n_pairs)


def _sb_sample_kernel(q_ref, kn_ref, vn_ref, kc_ref, vc_ref, o_ref, acc_ref, c_ref, kpad_ref, vpad_ref,
                      *, ts, tk, n_cache_blocks, n_pairs):
    q_heads, lo = _split_heads(q_ref, n_pairs, ts)
    kpad_ref[...] = jnp.zeros_like(kpad_ref)
    vpad_ref[...] = jnp.zeros_like(vpad_ref)
    kpad_ref[:ts, :] = kn_ref[...]
    vpad_ref[:ts, :] = vn_ref[...]

    def cache_block(n):
        keys = pl.ds(pl.multiple_of((n_cache_blocks - 1 - n) * tk, tk), tk)
        return _KV(kc_ref[:, keys].astype(BF16), vc_ref[:, keys].astype(BF16), True)

    strictly_earlier = (lax.broadcasted_iota(jnp.int32, (ts, tk), 1)
                        < lax.broadcasted_iota(jnp.int32, (ts, tk), 0))
    _sb_core(q_heads, lo, _KV(kpad_ref[...], vpad_ref[...], False), strictly_earlier, cache_block,
             n_cache_blocks, acc_ref, c_ref, o_ref, n_pairs)


def _sb_attention_prompt(q, k, v):
    b, t, _ = q.shape
    tb = ATT_BLOCK
    assert t % tb == 0
    n_pairs = SB_PAIRS_PER_STEP
    w = n_pairs * LANES
    kv_spec = pl.BlockSpec((None, w, t), lambda bi, hp, qi: (bi, hp, 0))
    q_spec = pl.BlockSpec((None, tb, w), lambda bi, hp, qi: (bi, qi, hp))
    return pl.pallas_call(
        functools.partial(_sb_prompt_kernel, tb=tb, n_pairs=n_pairs),
        grid=(b, A_WIDTH // w, t // tb),
        in_specs=[q_spec, kv_spec, kv_spec],
        out_specs=q_spec,
        out_shape=jax.ShapeDtypeStruct(q.shape, BF16),
        scratch_shapes=[pltpu.VMEM((tb, w), F32), pltpu.VMEM((2 * n_pairs, tb, 1), F32)],
        compiler_params=_params("parallel", "parallel", "parallel"),
        name="sb_attention",
    )(q, k, v)


def _sb_attention_sample(q, k_new, v_new, k_cache, v_cache):
    b, ts, _ = q.shape
    past = k_cache.shape[2]
    tk = ATT_BLOCK
    assert past % tk == 0 and ts <= tk
    n_pairs = SB_PAIRS_PER_STEP
    w = n_pairs * LANES
    new_spec = pl.BlockSpec((None, ts, w), lambda bi, hp: (bi, 0, hp))
    cache_spec = pl.BlockSpec((None, w, past), lambda bi, hp: (bi, hp, 0))
    return pl.pallas_call(
        functools.partial(_sb_sample_kernel, ts=ts, tk=tk, n_cache_blocks=past // tk, n_pairs=n_pairs),
        grid=(b, A_WIDTH // w),
        in_specs=[new_spec, new_spec, new_spec, cache_spec, cache_spec],
        out_specs=new_spec,
        out_shape=jax.ShapeDtypeStruct(q.shape, BF16),
        scratch_shapes=[pltpu.VMEM((ts, w), F32), pltpu.VMEM((2 * n_pairs, ts, 1), F32),
                        pltpu.VMEM((tk, w), BF16), pltpu.VMEM((tk, w), BF16)],
        compiler_params=_params("parallel", "parallel"),
        name="sb_attention_sample",
    )(q, k_new, v_new, k_cache, v_cache)


def _band_core(q_heads, lo, kv, pens, bias_ref, o_ref, n_pairs):
    heads = range(2 * n_pairs)
    blocks = range(len(kv))
    z = [[kv[i].scores(q_heads[h], h // 2) + bias_ref[h, i] for i in blocks] for h in heads]
    acc = []
    for h in heads:
        m = None
        for i in blocks:
            mi = jnp.max(z[h][i], axis=-1, keepdims=True)
            if pens[i] is not None:
                mi = mi + pens[i]
            m = mi if m is None else jnp.maximum(m, mi)
        a = None
        for i in blocks:
            shift = m if pens[i] is None else m - pens[i]
            p = jnp.exp(z[h][i] - shift).astype(BF16)
            v = kv[i].values(h // 2)
            ones = jnp.ones_like(v)
            first = kv[i].head_lanes()
            v = jnp.where(first, v, ones) if h % 2 == 0 else jnp.where(first, ones, v)
            pv = kv[i].weighted(p, v)
            a = pv if a is None else a + pv
        acc.append(a)
    for p in range(n_pairs):
        a0, a1 = acc[2 * p], acc[2 * p + 1]
        o_ref[:, _pair_cols(p)] = jnp.where(lo, a0 / pltpu.roll(a0, HEAD_DIM, axis=1),
                                            a1 / pltpu.roll(a1, HEAD_DIM, axis=1)).astype(o_ref.dtype)


def _band_prompt_kernel(q_ref, k_ref, v_ref, bias_ref, o_ref, *, tq, tk, n_pairs):
    qi = pl.program_id(2)
    q_heads, lo = _split_heads(q_ref, n_pairs, tq)
    kv, pens = [], []
    for dj in range(3):
        j = qi - dj
        pens.append(None if dj == 0 else jnp.where(j >= 0, 0.0, NEG_BIG).astype(F32))
        keys = pl.ds(pl.multiple_of(jnp.maximum(j, 0) * tk, tk), tk)
        kv.append(_KV(k_ref[:, keys], v_ref[:, keys], True))
    _band_core(q_heads, lo, kv, pens, bias_ref, o_ref, n_pairs)


def _band_sample_kernel(q_ref, kn_ref, vn_ref, kc_ref, vc_ref, bias_ref, o_ref, kpad_ref, vpad_ref,
                        *, ts, tk, n_cache_blocks, n_pairs):
    q_heads, lo = _split_heads(q_ref, n_pairs, ts)
    kpad_ref[...] = jnp.zeros_like(kpad_ref)
    vpad_ref[...] = jnp.zeros_like(vpad_ref)
    kpad_ref[:ts, :] = kn_ref[...]
    vpad_ref[:ts, :] = vn_ref[...]
    kv = [_KV(kpad_ref[...], vpad_ref[...], False)]
    for dj in range(1, n_cache_blocks + 1):
        keys = slice((n_cache_blocks - dj) * tk, (n_cache_blocks - dj + 1) * tk)
        kv.append(_KV(kc_ref[:, keys].astype(BF16), vc_ref[:, keys].astype(BF16), True))
    _band_core(q_heads, lo, kv, [None] * len(kv), bias_ref, o_ref, n_pairs)


def _band_attention_prompt(q, k, v, bias):
    b, t, _ = q.shape
    tq = tk = ATT_BLOCK
    assert t % tk == 0
    n_pairs = BAND_PAIRS_PER_STEP
    w = n_pairs * LANES
    kern = functools.partial(_band_prompt_kernel, tq=tq, tk=tk, n_pairs=n_pairs)
    kv_spec = pl.BlockSpec((None, w, t), lambda bi, hp, qi: (bi, hp, 0))
    q_spec = pl.BlockSpec((None, tq, w), lambda bi, hp, qi: (bi, qi, hp))
    bias_spec = pl.BlockSpec((2 * n_pairs, 3, tq, tk), lambda bi, hp, qi: (hp, 0, 0, 0))
    return pl.pallas_call(
        kern,
        grid=(b, C_WIDTH // w, t // tq),
        in_specs=[q_spec, kv_spec, kv_spec, bias_spec],
        out_specs=q_spec,
        out_shape=jax.ShapeDtypeStruct(q.shape, BF16),
        compiler_params=_params("parallel", "parallel", "parallel"),
        name="band_attention",
    )(q, k, v, bias)


def _band_attention_sample(q, k_new, v_new, k_cache, v_cache, bias):
    b, ts, _ = q.shape
    wc = k_cache.shape[2]
    tk = ATT_BLOCK
    n_cache_blocks = min(wc // tk, 2)
    assert wc % (n_cache_blocks * tk) == 0 and ts <= tk
    n_pairs = BAND_PAIRS_PER_STEP
    w = n_pairs * LANES
    kern = functools.partial(_band_sample_kernel, ts=ts, tk=tk, n_cache_blocks=n_cache_blocks,
                             n_pairs=n_pairs)
    new_spec = pl.BlockSpec((None, ts, w), lambda bi, hp: (bi, 0, hp))
    cache_rows = n_cache_blocks * tk
    cache_spec = pl.BlockSpec((None, w, cache_rows), lambda bi, hp: (bi, hp, wc // cache_rows - 1))
    bias_spec = pl.BlockSpec((2 * n_pairs, 1 + n_cache_blocks, ts, tk), lambda bi, hp: (hp, 0, 0, 0))
    return pl.pallas_call(
        kern,
        grid=(b, C_WIDTH // w),
        in_specs=[new_spec, new_spec, new_spec, cache_spec, cache_spec, bias_spec],
        out_specs=new_spec,
        out_shape=jax.ShapeDtypeStruct(q.shape, BF16),
        scratch_shapes=[pltpu.VMEM((tk, w), BF16), pltpu.VMEM((tk, w), BF16)],
        compiler_params=_params("parallel", "parallel"),
        name="band_attention_sample",
    )(q, k_new, v_new, k_cache, v_cache, bias)


def _band_bias_kernel(g_ref, o_ref, *, rows, valid0):
    tk = ATT_BLOCK
    q_chunk = lax.broadcasted_iota(jnp.int32, (rows, tk), 0) // CHUNK
    col = lax.broadcasted_iota(jnp.int32, (rows, tk), 1)
    k_chunk = col // CHUNK
    for dj in range(3):
        g = jnp.broadcast_to(g_ref[0, dj], (rows, 2 * tk))
        tile = pltpu.roll(g, 0, axis=1, stride=1, stride_axis=0)[:, :tk]
        diff = dj * (tk // CHUNK) + q_chunk - k_chunk
        seen = (diff >= 0) & (diff <= C_LEFT_CHUNKS)
        if dj == 0:
            seen = seen & (col < valid0)
        o_ref[0, dj] = jnp.where(seen, tile, NEG_BIG)


def _band_bias(rel_table, rows, valid0):
    tk = ATT_BLOCK
    c = jnp.arange(2 * tk, dtype=jnp.int32)
    u = jnp.where(c <= tk, -c, 2 * tk - c)
    idx = jnp.clip(jnp.arange(3, dtype=jnp.int32)[:, None] * tk + u[None, :], REL_MIN, REL_MAX) - REL_MIN
    g = rel_table[:, idx].astype(F32).reshape(C_HEADS, 3, 1, 2 * tk)
    return pl.pallas_call(
        functools.partial(_band_bias_kernel, rows=rows, valid0=valid0),
        grid=(C_HEADS,),
        in_specs=[pl.BlockSpec((1, 3, 1, 2 * tk), lambda h: (h, 0, 0, 0))],
        out_specs=pl.BlockSpec((1, 3, rows, tk), lambda h: (h, 0, 0, 0)),
        out_shape=jax.ShapeDtypeStruct((C_HEADS, 3, rows, tk), F32),
        compiler_params=_params("parallel"),
        name="band_bias",
    )(g)


def _gla_kernel(q_ref, k_ref, v_ref, la_ref, s0_ref, o_ref, s_out_ref, qg_ref, dec_ref, st_ref,
                *, n_chunks, group):
    L = GLA_CHUNK
    row = lax.broadcasted_iota(jnp.int32, (L, L), 0)
    colm = lax.broadcasted_iota(jnp.int32, (L, L), 1)
    tri = (colm <= row).astype(BF16)
    causal = colm <= row
    lane = lax.broadcasted_iota(jnp.int32, (L, LANES), 1)
    sub = lax.broadcasted_iota(jnp.int32, (LANES, B_DV), 0)
    mine = [(lane >= h * B_DK) & (lane < (h + 1) * B_DK) for h in range(2)]
    mine_rows = [(sub >= h * B_DK) & (sub < (h + 1) * B_DK) for h in range(2)]
    v_cols = [slice(h * B_DV, (h + 1) * B_DV) for h in range(2)]

    def chunk_rows(c):
        return pl.ds(pl.multiple_of(c * L, L), L)

    def local_pass(g, carry):
        chunks = [g * group + i for i in range(group)]
        rows = [chunk_rows(c) for c in chunks]
        b = []
        for r in rows:
            g_hi, g_lo = _split_bf16(la_ref[r, :])
            b.append(_dot(tri, g_hi) + _dot(tri, g_lo))
        qg_h, kg, kd_t = [], [], []
        for i, r in enumerate(rows):
            k = k_ref[r, :]
            qg = q_ref[r, :] * (B_DK ** -0.5) * jnp.exp(b[i])
            qg_ref[r, :] = qg.astype(BF16)
            qg_h.append([jnp.where(mine[h], qg, 0.0).astype(BF16) for h in range(2)])
            kg.append((k * jnp.exp(-b[i])).astype(BF16))
            b_t = b[i].T
            b_last = b_t[:, L - 1:L]
            kd_t.append((k.T * jnp.exp(b_last - b_t)).astype(BF16))
            dec_ref[chunks[i]] = jnp.broadcast_to(jnp.exp(b_last), (LANES, B_DV))
        att = [[jnp.where(causal, _dot_nt(qg_h[i][h], kg[i]), 0.0).astype(BF16) for h in range(2)]
               for i in range(group)]
        for i, r in enumerate(rows):
            for h in range(2):
                v_h = v_ref[r, v_cols[h]].astype(BF16)
                o_ref[r, v_cols[h]] = _dot(att[i][h], v_h)
                st_ref[chunks[i], h] = jnp.where(mine_rows[h], _dot(kd_t[i], v_h), 0.0)
        return carry

    lax.fori_loop(0, n_chunks // group, local_pass, 0)

    zeros_state = jnp.zeros((B_DK, B_DV), F32)
    s_init = (jnp.concatenate([s0_ref[0], zeros_state], axis=0),
              jnp.concatenate([zeros_state, s0_ref[1]], axis=0))

    def scan_pass(c, s):
        new = []
        for h in range(2):
            own = st_ref[c, h]
            st_ref[c, h] = s[h]
            new.append(dec_ref[c] * s[h] + own)
        return tuple(new)

    s_final = lax.fori_loop(0, n_chunks, scan_pass, s_init)
    s_out_ref[0] = s_final[0][:B_DK, :]
    s_out_ref[1] = s_final[1][B_DK:, :]

    def state_pass(g, carry):
        for i in range(group):
            c = g * group + i
            r = chunk_rows(c)
            for h in range(2):
                o_ref[r, v_cols[h]] += _dot(qg_ref[r, :], st_ref[c, h].astype(BF16))
        return carry

    lax.fori_loop(0, n_chunks // group, state_pass, 0)


def _gla(q, k, v, la, s0):
    b, t, _ = q.shape
    qk_spec = pl.BlockSpec((None, t, LANES), lambda bi, p: (bi, 0, p))
    v_spec = pl.BlockSpec((None, t, 2 * B_DV), lambda bi, p: (bi, 0, p))
    s_spec = pl.BlockSpec((None, 2, B_DK, B_DV), lambda bi, p: (bi, p, 0, 0))
    n_chunks = t // GLA_CHUNK
    group = GLA_GROUP if n_chunks % GLA_GROUP == 0 else 1
    return pl.pallas_call(
        functools.partial(_gla_kernel, n_chunks=n_chunks, group=group),
        grid=(b, B_HEADS // 2),
        in_specs=[qk_spec, qk_spec, v_spec, qk_spec, s_spec],
        out_specs=[v_spec, s_spec],
        out_shape=[jax.ShapeDtypeStruct(v.shape, F32), jax.ShapeDtypeStruct(s0.shape, F32)],
        scratch_shapes=[pltpu.VMEM((t, LANES), BF16), pltpu.VMEM((n_chunks, LANES, B_DV), F32),
                        pltpu.VMEM((n_chunks, 2, LANES, B_DV), F32)],
        compiler_params=_params("parallel", "parallel"),
        name="gla",
    )(q, k, v, la, s0)


def _pad_rows(x, n):
    return jnp.pad(x, ((0, 0), (0, n - x.shape[1]), (0, 0)))


def _heads_last(x, heads):
    b, _, s = x.shape
    return jnp.transpose(x.reshape(b, heads, HEAD_DIM, s), (0, 3, 1, 2))


def _feature_major(cache):
    b, s, heads, hd = cache.shape
    return jnp.transpose(cache, (0, 2, 3, 1)).reshape(b, heads * hd, s)


def _row_tile(m):
    for tm in (512, 256, 128, 64, 32, 16, 8):
        if m % tm == 0:
            return tm
    raise ValueError(f"token count {m} is not a multiple of 8")


def kernel(x_prompt, x_sample, cache_a_k, cache_a_v, state_b, cache_c_k, cache_c_v, norm_mix_g, norm_ffn_g, w_in_ab, w_gate_b, b_gate_b, norm_gla_g, w_out_ab, w_qkv_c, rel_bias_c, w_out_c, w_ffn_gate, w_ffn_up, w_ffn_down, norm_final_g):
    bp, tp, d = x_prompt.shape
    bs, ts, _ = x_sample.shape
    depth = norm_mix_g.shape[0]
    past = cache_a_k.shape[2]
    wc = cache_c_k.shape[2]
    assert tp % ATT_BLOCK == 0 and past % ATT_BLOCK == 0 and wc % ATT_BLOCK == 0
    assert ts <= GLA_CHUNK and ts % 8 == 0
    mp, ms = bp * tp, bs * ts
    tmp, tms = _row_tile(tp), _row_tile(ms)
    xp = x_prompt.reshape(mp, d)
    xs = x_sample.reshape(ms, d)
    row2 = lambda a: a.reshape(1, -1)

    a_kp, a_vp, a_ks, a_vs, b_sp, b_ss = [], [], [], [], [], []
    c_kp, c_vp, c_ks, c_vs = [], [], [], []

    for layer in range(depth):
        i = layer // 2
        g_mix = row2(norm_mix_g[layer])
        if layer % 2 == 0:
            w = w_in_ab[i]
            kv0, kv1 = A_WIDTH, 3 * A_WIDTH
            o = 3 * A_WIDTH + 2 * B_KW + B_VW
            w_main = jnp.concatenate([w[:, :kv0], w[:, kv1:o], w[:, o + B_GATE_RANK:]], axis=1).astype(BF16)
            w_kv = w[:, kv0:kv1].astype(BF16)
            w_kv_t = w.T[kv0:kv1].astype(BF16)
            w_lr = jnp.pad(w[:, o:o + B_GATE_RANK], ((0, 0), (0, LANES - B_GATE_RANK))).astype(BF16)
            w_gate = jnp.pad(w_gate_b[i], ((0, LANES - B_GATE_RANK), (0, 0))).astype(BF16)
            b_gate = row2(b_gate_b[i])
            g_gla = row2(norm_gla_g[i])
            w_out = w_out_ab[i].astype(BF16)

            qa, ka, va, kab, vab, qb, kb, vb, r, la = _proj_ab(
                xp, g_mix, w_main, w_kv_t, w_lr, w_gate, b_gate, tmp, batch=bp)
            sh = lambda a: a.reshape(bp, tp, -1)
            oa = _sb_attention_prompt(sh(qa), kab, vab)
            s0 = jnp.zeros((bp, B_HEADS, B_DK, B_DV), F32)
            ob, sbp = _gla(sh(qb), sh(kb), sh(vb), sh(la), s0)
            xp = _merge_ab(xp, oa.reshape(mp, -1), ob.reshape(mp, -1), r, g_gla, w_out, tmp)
            a_kp.append(_heads_last(ka, A_HEADS))
            a_vp.append(_heads_last(va, A_HEADS))
            b_sp.append(sbp)

            qa, ka, va, kab, vab, qb, kb, vb, r, la = _proj_ab(
                xs, g_mix, w_main, w_kv, w_lr, w_gate, b_gate, tms)
            sh = lambda a: a.reshape(bs, ts, -1)
            oa = _sb_attention_sample(sh(qa), sh(kab), sh(vab), _feature_major(cache_a_k[i]),
                                      _feature_major(cache_a_v[i]))
            pad_t = lambda a: _pad_rows(sh(a), GLA_CHUNK)
            ob, sbs = _gla(pad_t(qb), pad_t(kb), pad_t(vb), pad_t(la), state_b[i])
            xs = _merge_ab(xs, oa.reshape(ms, -1), ob[:, :ts].reshape(ms, -1), r, g_gla, w_out, tms)
            a_ks.append(ka.reshape(bs, ts, A_HEADS, HEAD_DIM))
            a_vs.append(va.reshape(bs, ts, A_HEADS, HEAD_DIM))
            b_ss.append(sbs)
        else:
            w_q = w_qkv_c[i][:, :C_WIDTH].astype(BF16)
            w_kv = w_qkv_c[i][:, C_WIDTH:].astype(BF16)
            w_kv_t = w_qkv_c[i][:, C_WIDTH:].T.astype(BF16)
            w_out = w_out_c[i].astype(BF16)

            q, k, v, kb16, vb16 = _proj_c(xp, g_mix, w_q, w_kv_t, tmp, batch=bp)
            bias = _band_bias(rel_bias_c[i], ATT_BLOCK, ATT_BLOCK)
            oc = _band_attention_prompt(q.reshape(bp, tp, -1), kb16, vb16, bias)
            xp = _out_c(xp, oc.reshape(mp, -1), w_out, tmp)
            keep = min(C_LEFT_CHUNKS * CHUNK, tp)
            c_kp.append(_heads_last(k[:, :, tp - keep:], C_HEADS))
            c_vp.append(_heads_last(v[:, :, tp - keep:], C_HEADS))

            q, k, v, kb16, vb16 = _proj_c(xs, g_mix, w_q, w_kv, tms)
            sh = lambda a: a.reshape(bs, ts, -1)
            bias = _band_bias(rel_bias_c[i], ts, ts)
            oc = _band_attention_sample(sh(q), sh(kb16), sh(vb16), _feature_major(cache_c_k[i]),
                                        _feature_major(cache_c_v[i]), bias)
            xs = _out_c(xs, oc.reshape(ms, -1), w_out, tms)
            c_ks.append(k.reshape(bs, ts, C_HEADS, HEAD_DIM))
            c_vs.append(v.reshape(bs, ts, C_HEADS, HEAD_DIM))

        g_ffn = row2(norm_ffn_g[layer])
        wg = w_ffn_gate[layer].astype(BF16)
        wu = w_ffn_up[layer].astype(BF16)
        wd = w_ffn_down[layer].astype(BF16)
        xp = _ffn(xp, g_ffn, wg, wu, wd, tmp)
        xs = _ffn(xs, g_ffn, wg, wu, wd, tms)

    g_fin = row2(norm_final_g)
    y_prompt = _final_norm(xp, g_fin, tmp).reshape(bp, tp, d)
    y_sample = _final_norm(xs, g_fin, tms).reshape(bs, ts, d)
    return (y_prompt, y_sample, jnp.stack(a_kp), jnp.stack(a_vp), jnp.stack(a_ks), jnp.stack(a_vs),
            jnp.stack(b_sp), jnp.stack(b_ss), jnp.stack(c_kp), jnp.stack(c_vp),
            jnp.stack(c_ks), jnp.stack(c_vs))
```

```python
import functools

import jax
import jax.numpy as jnp
from jax import lax
from jax.experimental import pallas as pl
from jax.experimental.pallas import tpu as pltpu

F32 = jnp.float32
BF16 = jnp.bfloat16

EPS = 1e-6
HEAD_DIM = 64
LANES = 128
A_HEADS = 8
A_WIDTH = A_HEADS * HEAD_DIM
B_HEADS = 4
B_DK = 64
B_DV = 128
B_KW = B_HEADS * B_DK
B_VW = B_HEADS * B_DV
B_GATE_RANK = 16
B_GATE_TEMP = 16.0
GLA_CHUNK = 64
C_HEADS = 16
C_WIDTH = C_HEADS * HEAD_DIM
CHUNK = 64
C_LEFT_CHUNKS = 8
REL_MIN = -(CHUNK - 1)
REL_MAX = 128
ATT_BLOCK = 256
NEG_BIG = -1e30
LOG2E = 1.4426950408889634
SB_DEAD = 104.0
SB_PAIRS_PER_STEP = 4
BAND_PAIRS_PER_STEP = 4
GLA_GROUP = 4
VMEM_LIMIT = 56 * 1024 * 1024


def _params(*sem):
    return pltpu.CompilerParams(dimension_semantics=sem, vmem_limit_bytes=VMEM_LIMIT)


def _resident(shape):
    return pl.BlockSpec(shape, lambda *_: (0,) * len(shape), pipeline_mode=pl.Buffered(1))


def _rms(x, g):
    return x * lax.rsqrt(jnp.mean(x * x, axis=-1, keepdims=True) + EPS) * g


def _log_sigmoid_pair(z):
    l = jnp.log1p(jnp.exp(-jnp.abs(z)))
    return jnp.minimum(z, 0.0) - l, jnp.minimum(-z, 0.0) - l


def _split_bf16(x):
    hi = x.astype(BF16)
    lo = (x - hi.astype(F32)).astype(BF16)
    return hi, lo


def _dot(a, b):
    return jnp.dot(a, b, preferred_element_type=F32)


def _dot_nt(a, b):
    return lax.dot_general(a, b, (((1,), (1,)), ((), ())), preferred_element_type=F32)


def _dot_tn(a, b):
    return lax.dot_general(a, b, (((0,), (0,)), ((), ())), preferred_element_type=F32)


def _pair_cols(p):
    return slice(p * LANES, (p + 1) * LANES)


class _KV:
    def __init__(self, k, v, feature_major):
        self.k, self.v, self.feature_major = k, v, feature_major
        self.n_keys = k.shape[1] if feature_major else k.shape[0]

    def scores(self, q_h, p):
        if self.feature_major:
            return _dot(q_h, self.k[_pair_cols(p), :])
        return _dot_nt(q_h, self.k[:, _pair_cols(p)])

    def values(self, p):
        return self.v[_pair_cols(p), :] if self.feature_major else self.v[:, _pair_cols(p)]

    def weighted(self, w, v_p):
        return _dot_nt(w, v_p) if self.feature_major else _dot(w, v_p)

    def head_lanes(self):
        shape = (LANES, self.n_keys) if self.feature_major else (self.n_keys, LANES)
        return lax.broadcasted_iota(jnp.int32, shape, 0 if self.feature_major else 1) < HEAD_DIM


def _split_heads(q_ref, n_pairs, tq):
    lo = lax.broadcasted_iota(jnp.int32, (tq, LANES), 1) < HEAD_DIM
    heads = []
    for p in range(n_pairs):
        q = q_ref[:, _pair_cols(p)]
        heads += [jnp.where(lo, q, jnp.zeros_like(q)), jnp.where(lo, jnp.zeros_like(q), q)]
    return heads, lo


def _emit_kv(y, wkv_ref, k_ref, v_ref, kb_ref, vb_ref, feature_major):
    if feature_major:
        kv = _dot_nt(wkv_ref[...], y)
        width = kv.shape[0] // 2
        k, v = kv[:width, :], kv[width:, :]
    else:
        kv = _dot(y, wkv_ref[...])
        width = kv.shape[1] // 2
        k, v = kv[:, :width], kv[:, width:]
    k_ref[...] = k
    v_ref[...] = v
    kb_ref[...] = k.astype(BF16)
    vb_ref[...] = v.astype(BF16)


def _proj_ab_kernel(x_ref, g_ref, w_ref, wkv_ref, wlr_ref, wgate_ref, bgate_ref,
                    qa_ref, ka_ref, va_ref, kab_ref, vab_ref, qb_ref, kb_ref, vb_ref, r_ref, la_ref,
                    *, feature_major):
    y = _rms(x_ref[...], g_ref[...]).astype(BF16)
    _emit_kv(y, wkv_ref, ka_ref, va_ref, kab_ref, vab_ref, feature_major)
    z = _dot(y, w_ref[...])
    c = 0
    qa_ref[...] = (z[:, c:c + A_WIDTH] * (HEAD_DIM ** -0.5)).astype(BF16); c += A_WIDTH
    qb_ref[...] = z[:, c:c + B_KW]; c += B_KW
    kb_ref[...] = z[:, c:c + B_KW]; c += B_KW
    vb_ref[...] = z[:, c:c + B_VW]; c += B_VW
    r_ref[...] = z[:, c:c + B_VW]
    g_lr = _dot(y, wlr_ref[...])
    gate = _dot(g_lr.astype(BF16), wgate_ref[...]) + bgate_ref[...]
    la_ref[...] = _log_sigmoid_pair(gate)[0] * (1.0 / B_GATE_TEMP)


def _kv_out(m, width, tm, batch):
    if batch is None:
        spec = pl.BlockSpec((tm, width), lambda i: (i, 0))
        shape = (m, width)
    else:
        t = m // batch
        assert t % tm == 0
        spec = pl.BlockSpec((None, width, tm), lambda i: (i // (t // tm), 0, i % (t // tm)))
        shape = (batch, width, t)
    return [spec] * 4, [jax.ShapeDtypeStruct(shape, dt) for dt in (F32, F32, BF16, BF16)]


def _proj_ab(x, g, w_main, w_kv, w_lr, w_gate, b_gate, tm, batch=None):
    m, d = x.shape
    row = lambda n: pl.BlockSpec((tm, n), lambda i: (i, 0))
    kv_specs, kv_shapes = _kv_out(m, A_WIDTH, tm, batch)
    rest = [(B_KW, F32), (B_KW, F32), (B_VW, F32), (B_VW, F32), (B_KW, F32)]
    return pl.pallas_call(
        functools.partial(_proj_ab_kernel, feature_major=batch is not None),
        grid=(m // tm,),
        in_specs=[row(d), _resident(g.shape), _resident(w_main.shape), _resident(w_kv.shape),
                  _resident(w_lr.shape), _resident(w_gate.shape), _resident(b_gate.shape)],
        out_specs=[row(A_WIDTH)] + kv_specs + [row(n) for n, _ in rest],
        out_shape=([jax.ShapeDtypeStruct((m, A_WIDTH), BF16)] + kv_shapes
                   + [jax.ShapeDtypeStruct((m, n), dt) for n, dt in rest]),
        compiler_params=_params("parallel"),
        name="proj_ab",
    )(x, g, w_main, w_kv, w_lr, w_gate, b_gate)


def _proj_c_kernel(x_ref, g_ref, wq_ref, wkv_ref, q_ref, k_ref, v_ref, kb_ref, vb_ref, *, feature_major):
    y = _rms(x_ref[...], g_ref[...]).astype(BF16)
    _emit_kv(y, wkv_ref, k_ref, v_ref, kb_ref, vb_ref, feature_major)
    q_ref[...] = (_dot(y, wq_ref[...]) * (HEAD_DIM ** -0.5 * LOG2E)).astype(BF16)


def _proj_c(x, g, w_q, w_kv, tm, batch=None):
    m, d = x.shape
    row = lambda n: pl.BlockSpec((tm, n), lambda i: (i, 0))
    kv_specs, kv_shapes = _kv_out(m, C_WIDTH, tm, batch)
    return pl.pallas_call(
        functools.partial(_proj_c_kernel, feature_major=batch is not None),
        grid=(m // tm,),
        in_specs=[row(d), _resident(g.shape), _resident(w_q.shape), _resident(w_kv.shape)],
        out_specs=[row(C_WIDTH)] + kv_specs,
        out_shape=[jax.ShapeDtypeStruct((m, C_WIDTH), BF16)] + kv_shapes,
        compiler_params=_params("parallel"),
        name="proj_c",
    )(x, g, w_q, w_kv)


def _layer_tail_kernel(x_ref, *refs, gla_merge, final_norm):
    if gla_merge:
        oa_ref, ob_ref, r_ref, ggla_ref, wout_ref, *refs = refs
        ob = ob_ref[...]
        parts = []
        for h in range(B_HEADS):
            seg = ob[:, h * B_DV:(h + 1) * B_DV]
            parts.append(seg * lax.rsqrt(jnp.mean(seg * seg, axis=-1, keepdims=True) + EPS))
        r = r_ref[...]
        obn = jnp.concatenate(parts, axis=-1) * ggla_ref[...] * (r * jax.nn.sigmoid(r))
        mix = _dot(oa_ref[...], wout_ref[:A_WIDTH, :]) + _dot(obn.astype(BF16), wout_ref[A_WIDTH:, :])
    else:
        oc_ref, wout_ref, *refs = refs
        mix = _dot(oc_ref[...], wout_ref[...])
    gffn_ref, wg_ref, wu_ref, wd_ref, *refs = refs
    x = x_ref[...] + mix
    y = _rms(x, gffn_ref[...]).astype(BF16)
    h = _dot(y, wg_ref[...])
    u = _dot(y, wu_ref[...])
    a = (h * jax.nn.sigmoid(h) * u).astype(BF16)
    x = x + _dot(a, wd_ref[...])
    if final_norm:
        gfin_ref, o_ref = refs
        o_ref[...] = _rms(x, gfin_ref[...])
    else:
        (o_ref,) = refs
        o_ref[...] = x


def _layer_tail(x, mixer_out, mixer_params, g_ffn, wg, wu, wd, g_fin, tm):
    m, d = x.shape
    row = lambda a: pl.BlockSpec((tm, a.shape[1]), lambda i: (i, 0))
    resident = [*mixer_params, g_ffn, wg, wu, wd] + ([] if g_fin is None else [g_fin])
    return pl.pallas_call(
        functools.partial(_layer_tail_kernel, gla_merge=len(mixer_out) == 3, final_norm=g_fin is not None),
        grid=(m // tm,),
        in_specs=[row(x)] + [row(a) for a in mixer_out] + [_resident(a.shape) for a in resident],
        out_specs=row(x),
        out_shape=jax.ShapeDtypeStruct((m, d), F32),
        compiler_params=_params("parallel"),
        name="layer_tail",
    )(x, *mixer_out, *resident)


def _sb_core(q_heads, lo, first_kv, first_mask, earlier_kv, n_earlier, acc_ref, c_ref, o_ref, n_pairs):
    tk = first_kv.n_keys
    heads = range(2 * n_pairs)
    later = (lax.broadcasted_iota(jnp.int32, (tk, tk), 0)
             > lax.broadcasted_iota(jnp.int32, (tk, tk), 1)).astype(BF16)
    c_ref[...] = jnp.zeros_like(c_ref)

    def block(kv, mask):
        z = [kv.scores(q_heads[h], h // 2) for h in heads]
        log_beta, drop, after = [], [], []
        for h in heads:
            d = jnp.maximum(z[h], 0.0) + jnp.log(1.0 + jnp.exp(-jnp.abs(z[h])))
            log_beta.append(z[h] - d)
            drop.append(d if mask is None else jnp.where(mask, d, 0.0))
        for h in heads:
            hi, lo_part = _split_bf16(drop[h])
            after.append(_dot(hi, later) + _dot(lo_part, later))
        pv = []
        for h in heads:
            c = c_ref[h]
            w = jnp.exp(log_beta[h] - after[h] - c)
            if mask is not None:
                w = jnp.where(mask, w, 0.0)
            c_ref[h] = c + after[h][:, 0:1] + drop[h][:, 0:1]
            pv.append(kv.weighted(w.astype(BF16), kv.values(h // 2)))
        out = [jnp.where(lo, pv[2 * p], pv[2 * p + 1]) for p in range(n_pairs)]
        return out[0] if n_pairs == 1 else jnp.concatenate(out, axis=-1)

    def all_dead():
        return jnp.min(c_ref[...]) > SB_DEAD

    acc_ref[...] = block(first_kv, first_mask)

    def cond(carry):
        n, dead = carry
        return (n < n_earlier) & jnp.logical_not(dead)

    def body(carry):
        n, _ = carry
        acc_ref[...] += block(earlier_kv(n), None)
        return n + 1, all_dead()

    lax.while_loop(cond, body, (jnp.int32(0), all_dead()))
    o_ref[...] = acc_ref[...].astype(o_ref.dtype)


def _sb_prompt_kernel(q_ref, k_ref, v_ref, o_ref, acc_ref, c_ref, *, tb, n_pairs):
    qi = pl.program_id(2)
    q_heads, lo = _split_heads(q_ref, n_pairs, tb)

    def kv_block(j):
        keys = pl.ds(pl.multiple_of(j * tb, tb), tb)
        return _KV(k_ref[:, keys], v_ref[:, keys], True)

    strictly_earlier = (lax.broadcasted_iota(jnp.int32, (tb, tb), 1)
                        < lax.broadcasted_iota(jnp.int32, (tb, tb), 0))
    _sb_core(q_heads, lo, kv_block(qi), strictly_earlier, lambda n: kv_block(qi - 1 - n), qi,
             acc_ref, c_ref, o_ref, n_pairs)


def _sb_sample_kernel(q_ref, kn_ref, vn_ref, kc_ref, vc_ref, o_ref, acc_ref, c_ref, kpad_ref, vpad_ref,
                      *, ts, tk, n_cache_blocks, n_pairs):
    q_heads, lo = _split_heads(q_ref, n_pairs, ts)
    kpad_ref[...] = jnp.zeros_like(kpad_ref)
    vpad_ref[...] = jnp.zeros_like(vpad_ref)
    kpad_ref[:ts, :] = kn_ref[...]
    vpad_ref[:ts, :] = vn_ref[...]

    def cache_block(n):
        keys = pl.ds(pl.multiple_of((n_cache_blocks - 1 - n) * tk, tk), tk)
        return _KV(kc_ref[:, keys].astype(BF16), vc_ref[:, keys].astype(BF16), True)

    strictly_earlier = (lax.broadcasted_iota(jnp.int32, (ts, tk), 1)
                        < lax.broadcasted_iota(jnp.int32, (ts, tk), 0))
    _sb_core(q_heads, lo, _KV(kpad_ref[...], vpad_ref[...], False), strictly_earlier, cache_block,
             n_cache_blocks, acc_ref, c_ref, o_ref, n_pairs)


def _sb_attention_prompt(q, k, v):
    b, t, _ = q.shape
    tb = ATT_BLOCK
    assert t % tb == 0
    n_pairs = SB_PAIRS_PER_STEP
    w = n_pairs * LANES
    kv_spec = pl.BlockSpec((None, w, t), lambda bi, hp, qi: (bi, hp, 0))
    q_spec = pl.BlockSpec((None, tb, w), lambda bi, hp, qi: (bi, qi, hp))
    return pl.pallas_call(
        functools.partial(_sb_prompt_kernel, tb=tb, n_pairs=n_pairs),
        grid=(b, A_WIDTH // w, t // tb),
        in_specs=[q_spec, kv_spec, kv_spec],
        out_specs=q_spec,
        out_shape=jax.ShapeDtypeStruct(q.shape, BF16),
        scratch_shapes=[pltpu.VMEM((tb, w), F32), pltpu.VMEM((2 * n_pairs, tb, 1), F32)],
        compiler_params=_params("parallel", "parallel", "parallel"),
        name="sb_attention",
    )(q, k, v)


def _sb_attention_sample(q, k_new, v_new, k_cache, v_cache):
    b, ts, _ = q.shape
    past = k_cache.shape[2]
    tk = ATT_BLOCK
    assert past % tk == 0 and ts <= tk
    n_pairs = SB_PAIRS_PER_STEP
    w = n_pairs * LANES
    new_spec = pl.BlockSpec((None, ts, w), lambda bi, hp: (bi, 0, hp))
    cache_spec = pl.BlockSpec((None, w, past), lambda bi, hp: (bi, hp, 0))
    return pl.pallas_call(
        functools.partial(_sb_sample_kernel, ts=ts, tk=tk, n_cache_blocks=past // tk, n_pairs=n_pairs),
        grid=(b, A_WIDTH // w),
        in_specs=[new_spec, new_spec, new_spec, cache_spec, cache_spec],
        out_specs=new_spec,
        out_shape=jax.ShapeDtypeStruct(q.shape, BF16),
        scratch_shapes=[pltpu.VMEM((ts, w), F32), pltpu.VMEM((2 * n_pairs, ts, 1), F32),
                        pltpu.VMEM((tk, w), BF16), pltpu.VMEM((tk, w), BF16)],
        compiler_params=_params("parallel", "parallel"),
        name="sb_attention_sample",
    )(q, k_new, v_new, k_cache, v_cache)


def _band_core(q_heads, lo, kv, pens, bias_ref, o_ref, n_pairs):
    heads = range(2 * n_pairs)
    blocks = range(len(kv))
    z = [[kv[i].scores(q_heads[h], h // 2) + bias_ref[h, i] for i in blocks] for h in heads]
    acc = []
    for h in heads:
        m = None
        for i in blocks:
            mi = jnp.max(z[h][i], axis=-1, keepdims=True)
            if pens[i] is not None:
                mi = mi + pens[i]
            m = mi if m is None else jnp.maximum(m, mi)
        a = None
        for i in blocks:
            shift = m if pens[i] is None else m - pens[i]
            p = jnp.exp2(z[h][i] - shift).astype(BF16)
            v = kv[i].values(h // 2)
            ones = jnp.ones_like(v)
            first = kv[i].head_lanes()
            v = jnp.where(first, v, ones) if h % 2 == 0 else jnp.where(first, ones, v)
            pv = kv[i].weighted(p, v)
            a = pv if a is None else a + pv
        acc.append(a)
    for p in range(n_pairs):
        a0, a1 = acc[2 * p], acc[2 * p + 1]
        o_ref[:, _pair_cols(p)] = jnp.where(lo, a0 / pltpu.roll(a0, HEAD_DIM, axis=1),
                                            a1 / pltpu.roll(a1, HEAD_DIM, axis=1)).astype(o_ref.dtype)


def _band_prompt_kernel(q_ref, k_ref, v_ref, bias_ref, o_ref, *, tq, tk, n_pairs):
    qi = pl.program_id(2)
    q_heads, lo = _split_heads(q_ref, n_pairs, tq)
    kv, pens = [], []
    for dj in range(3):
        j = qi - dj
        pens.append(None if dj == 0 else jnp.where(j >= 0, 0.0, NEG_BIG).astype(F32))
        keys = pl.ds(pl.multiple_of(jnp.maximum(j, 0) * tk, tk), tk)
        kv.append(_KV(k_ref[:, keys], v_ref[:, keys], True))
    _band_core(q_heads, lo, kv, pens, bias_ref, o_ref, n_pairs)


def _band_sample_kernel(q_ref, kn_ref, vn_ref, kc_ref, vc_ref, bias_ref, o_ref, kpad_ref, vpad_ref,
                        *, ts, tk, n_cache_blocks, n_pairs):
    q_heads, lo = _split_heads(q_ref, n_pairs, ts)
    kpad_ref[...] = jnp.zeros_like(kpad_ref)
    vpad_ref[...] = jnp.zeros_like(vpad_ref)
    kpad_ref[:ts, :] = kn_ref[...]
    vpad_ref[:ts, :] = vn_ref[...]
    kv = [_KV(kpad_ref[...], vpad_ref[...], False)]
    for dj in range(1, n_cache_blocks + 1):
        keys = slice((n_cache_blocks - dj) * tk, (n_cache_blocks - dj + 1) * tk)
        kv.append(_KV(kc_ref[:, keys].astype(BF16), vc_ref[:, keys].astype(BF16), True))
    _band_core(q_heads, lo, kv, [None] * len(kv), bias_ref, o_ref, n_pairs)


def _band_attention_prompt(q, k, v, bias):
    b, t, _ = q.shape
    tq = tk = ATT_BLOCK
    assert t % tk == 0
    n_pairs = BAND_PAIRS_PER_STEP
    w = n_pairs * LANES
    kern = functools.partial(_band_prompt_kernel, tq=tq, tk=tk, n_pairs=n_pairs)
    kv_spec = pl.BlockSpec((None, w, t), lambda bi, hp, qi: (bi, hp, 0))
    q_spec = pl.BlockSpec((None, tq, w), lambda bi, hp, qi: (bi, qi, hp))
    bias_spec = pl.BlockSpec((2 * n_pairs, 3, tq, tk), lambda bi, hp, qi: (hp, 0, 0, 0))
    return pl.pallas_call(
        kern,
        grid=(b, C_WIDTH // w, t // tq),
        in_specs=[q_spec, kv_spec, kv_spec, bias_spec],
        out_specs=q_spec,
        out_shape=jax.ShapeDtypeStruct(q.shape, BF16),
        compiler_params=_params("parallel", "parallel", "parallel"),
        name="band_attention",
    )(q, k, v, bias)


def _band_attention_sample(q, k_new, v_new, k_cache, v_cache, bias):
    b, ts, _ = q.shape
    wc = k_cache.shape[2]
    tk = ATT_BLOCK
    n_cache_blocks = min(wc // tk, 2)
    assert wc % (n_cache_blocks * tk) == 0 and ts <= tk
    n_pairs = BAND_PAIRS_PER_STEP
    w = n_pairs * LANES
    kern = functools.partial(_band_sample_kernel, ts=ts, tk=tk, n_cache_blocks=n_cache_blocks,
                             n_pairs=n_pairs)
    new_spec = pl.BlockSpec((None, ts, w), lambda bi, hp: (bi, 0, hp))
    cache_rows = n_cache_blocks * tk
    cache_spec = pl.BlockSpec((None, w, cache_rows), lambda bi, hp: (bi, hp, wc // cache_rows - 1))
    bias_spec = pl.BlockSpec((2 * n_pairs, 1 + n_cache_blocks, ts, tk), lambda bi, hp: (hp, 0, 0, 0))
    return pl.pallas_call(
        kern,
        grid=(b, C_WIDTH // w),
        in_specs=[new_spec, new_spec, new_spec, cache_spec, cache_spec, bias_spec],
        out_specs=new_spec,
        out_shape=jax.ShapeDtypeStruct(q.shape, BF16),
        scratch_shapes=[pltpu.VMEM((tk, w), BF16), pltpu.VMEM((tk, w), BF16)],
        compiler_params=_params("parallel", "parallel"),
        name="band_attention_sample",
    )(q, k_new, v_new, k_cache, v_cache, bias)


def _band_bias_kernel(g_ref, o_ref, *, rows, valid0):
    tk = ATT_BLOCK
    q_chunk = lax.broadcasted_iota(jnp.int32, (rows, tk), 0) // CHUNK
    col = lax.broadcasted_iota(jnp.int32, (rows, tk), 1)
    k_chunk = col // CHUNK
    for dj in range(3):
        g = jnp.broadcast_to(g_ref[0, dj], (rows, 2 * tk))
        tile = pltpu.roll(g, 0, axis=1, stride=1, stride_axis=0)[:, :tk]
        diff = dj * (tk // CHUNK) + q_chunk - k_chunk
        seen = (diff >= 0) & (diff <= C_LEFT_CHUNKS)
        if dj == 0:
            seen = seen & (col < valid0)
        o_ref[0, dj] = jnp.where(seen, tile * LOG2E, NEG_BIG)


def _band_bias(rel_table, rows, valid0):
    tk = ATT_BLOCK
    c = jnp.arange(2 * tk, dtype=jnp.int32)
    u = jnp.where(c <= tk, -c, 2 * tk - c)
    idx = jnp.clip(jnp.arange(3, dtype=jnp.int32)[:, None] * tk + u[None, :], REL_MIN, REL_MAX) - REL_MIN
    g = rel_table[:, idx].astype(F32).reshape(C_HEADS, 3, 1, 2 * tk)
    return pl.pallas_call(
        functools.partial(_band_bias_kernel, rows=rows, valid0=valid0),
        grid=(C_HEADS,),
        in_specs=[pl.BlockSpec((1, 3, 1, 2 * tk), lambda h: (h, 0, 0, 0))],
        out_specs=pl.BlockSpec((1, 3, rows, tk), lambda h: (h, 0, 0, 0)),
        out_shape=jax.ShapeDtypeStruct((C_HEADS, 3, rows, tk), F32),
        compiler_params=_params("parallel"),
        name="band_bias",
    )(g)


def _gla_kernel(q_ref, k_ref, v_ref, la_ref, s0_ref, o_ref, s_out_ref, qg_ref, dec_ref, st_ref,
                *, n_chunks, group):
    L = GLA_CHUNK
    row = lax.broadcasted_iota(jnp.int32, (L, L), 0)
    colm = lax.broadcasted_iota(jnp.int32, (L, L), 1)
    tri = (colm <= row).astype(BF16)
    causal = colm <= row
    lane = lax.broadcasted_iota(jnp.int32, (L, LANES), 1)
    sub = lax.broadcasted_iota(jnp.int32, (LANES, B_DV), 0)
    mine = [(lane >= h * B_DK) & (lane < (h + 1) * B_DK) for h in range(2)]
    mine_rows = [(sub >= h * B_DK) & (sub < (h + 1) * B_DK) for h in range(2)]
    v_cols = [slice(h * B_DV, (h + 1) * B_DV) for h in range(2)]

    def chunk_rows(c):
        return pl.ds(pl.multiple_of(c * L, L), L)

    def local_pass(g, carry):
        chunks = [g * group + i for i in range(group)]
        rows = [chunk_rows(c) for c in chunks]
        b = []
        for r in rows:
            g_hi, g_lo = _split_bf16(la_ref[r, :])
            b.append(_dot(tri, g_hi) + _dot(tri, g_lo))
        qg_h, kg, kd_t = [], [], []
        for i, r in enumerate(rows):
            k = k_ref[r, :]
            qg = q_ref[r, :] * (B_DK ** -0.5) * jnp.exp(b[i])
            qg_ref[r, :] = qg.astype(BF16)
            qg_h.append([jnp.where(mine[h], qg, 0.0).astype(BF16) for h in range(2)])
            kg.append((k * jnp.exp(-b[i])).astype(BF16))
            b_t = b[i].T
            b_last = b_t[:, L - 1:L]
            kd_t.append((k.T * jnp.exp(b_last - b_t)).astype(BF16))
            dec_ref[chunks[i]] = jnp.broadcast_to(jnp.exp(b_last), (LANES, B_DV))
        att = [[jnp.where(causal, _dot_nt(qg_h[i][h], kg[i]), 0.0).astype(BF16) for h in range(2)]
               for i in range(group)]
        for i, r in enumerate(rows):
            for h in range(2):
                v_h = v_ref[r, v_cols[h]].astype(BF16)
                o_ref[r, v_cols[h]] = _dot(att[i][h], v_h)
                st_ref[chunks[i], h] = jnp.where(mine_rows[h], _dot(kd_t[i], v_h), 0.0)
        return carry

    lax.fori_loop(0, n_chunks // group, local_pass, 0)

    zeros_state = jnp.zeros((B_DK, B_DV), F32)
    s_init = (jnp.concatenate([s0_ref[0], zeros_state], axis=0),
              jnp.concatenate([zeros_state, s0_ref[1]], axis=0))

    def scan_pass(c, s):
        new = []
        for h in range(2):
            own = st_ref[c, h]
            st_ref[c, h] = s[h]
            new.append(dec_ref[c] * s[h] + own)
        return tuple(new)

    s_final = lax.fori_loop(0, n_chunks, scan_pass, s_init)
    s_out_ref[0] = s_final[0][:B_DK, :]
    s_out_ref[1] = s_final[1][B_DK:, :]

    def state_pass(g, carry):
        for i in range(group):
            c = g * group + i
            r = chunk_rows(c)
            for h in range(2):
                o_ref[r, v_cols[h]] += _dot(qg_ref[r, :], st_ref[c, h].astype(BF16))
        return carry

    lax.fori_loop(0, n_chunks // group, state_pass, 0)


def _gla(q, k, v, la, s0):
    b, t, _ = q.shape
    qk_spec = pl.BlockSpec((None, t, LANES), lambda bi, p: (bi, 0, p))
    v_spec = pl.BlockSpec((None, t, 2 * B_DV), lambda bi, p: (bi, 0, p))
    s_spec = pl.BlockSpec((None, 2, B_DK, B_DV), lambda bi, p: (bi, p, 0, 0))
    n_chunks = t // GLA_CHUNK
    group = GLA_GROUP if n_chunks % GLA_GROUP == 0 else 1
    return pl.pallas_call(
        functools.partial(_gla_kernel, n_chunks=n_chunks, group=group),
        grid=(b, B_HEADS // 2),
        in_specs=[qk_spec, qk_spec, v_spec, qk_spec, s_spec],
        out_specs=[v_spec, s_spec],
        out_shape=[jax.ShapeDtypeStruct(v.shape, F32), jax.ShapeDtypeStruct(s0.shape, F32)],
        scratch_shapes=[pltpu.VMEM((t, LANES), BF16), pltpu.VMEM((n_chunks, LANES, B_DV), F32),
                        pltpu.VMEM((n_chunks, 2, LANES, B_DV), F32)],
        compiler_params=_params("parallel", "parallel"),
        name="gla",
    )(q, k, v, la, s0)


def _pad_rows(x, n):
    return jnp.pad(x, ((0, 0), (0, n - x.shape[1]), (0, 0)))


def _heads_last(x, heads):
    b, _, s = x.shape
    return jnp.transpose(x.reshape(b, heads, HEAD_DIM, s), (0, 3, 1, 2))


def _feature_major(cache):
    b, s, heads, hd = cache.shape
    return jnp.transpose(cache, (0, 2, 3, 1)).reshape(b, heads * hd, s)


def _row_tile(m):
    for tm in (512, 256, 128, 64, 32, 16, 8):
        if m % tm == 0:
            return tm
    raise ValueError(f"token count {m} is not a multiple of 8")


def kernel(x_prompt, x_sample, cache_a_k, cache_a_v, state_b, cache_c_k, cache_c_v, norm_mix_g, norm_ffn_g, w_in_ab, w_gate_b, b_gate_b, norm_gla_g, w_out_ab, w_qkv_c, rel_bias_c, w_out_c, w_ffn_gate, w_ffn_up, w_ffn_down, norm_final_g):
    bp, tp, d = x_prompt.shape
    bs, ts, _ = x_sample.shape
    depth = norm_mix_g.shape[0]
    past = cache_a_k.shape[2]
    wc = cache_c_k.shape[2]
    assert tp % ATT_BLOCK == 0 and past % ATT_BLOCK == 0 and wc % ATT_BLOCK == 0
    assert ts <= GLA_CHUNK and ts % 8 == 0
    mp, ms = bp * tp, bs * ts
    tmp, tms = _row_tile(tp), _row_tile(ms)
    xp = x_prompt.reshape(mp, d)
    xs = x_sample.reshape(ms, d)
    row2 = lambda a: a.reshape(1, -1)

    a_kp, a_vp, a_ks, a_vs, b_sp, b_ss = [], [], [], [], [], []
    c_kp, c_vp, c_ks, c_vs = [], [], [], []

    for layer in range(depth):
        i = layer // 2
        g_mix = row2(norm_mix_g[layer])
        ffn = (row2(norm_ffn_g[layer]), w_ffn_gate[layer].astype(BF16), w_ffn_up[layer].astype(BF16),
               w_ffn_down[layer].astype(BF16), row2(norm_final_g) if layer == depth - 1 else None)
        if layer % 2 == 0:
            w = w_in_ab[i]
            kv0, kv1 = A_WIDTH, 3 * A_WIDTH
            o = 3 * A_WIDTH + 2 * B_KW + B_VW
            w_main = jnp.concatenate([w[:, :kv0], w[:, kv1:o], w[:, o + B_GATE_RANK:]], axis=1).astype(BF16)
            w_kv = w[:, kv0:kv1].astype(BF16)
            w_kv_t = w.T[kv0:kv1].astype(BF16)
            w_lr = jnp.pad(w[:, o:o + B_GATE_RANK], ((0, 0), (0, LANES - B_GATE_RANK))).astype(BF16)
            w_gate = jnp.pad(w_gate_b[i], ((0, LANES - B_GATE_RANK), (0, 0))).astype(BF16)
            b_gate = row2(b_gate_b[i])
            g_gla = row2(norm_gla_g[i])
            w_out = w_out_ab[i].astype(BF16)

            qa, ka, va, kab, vab, qb, kb, vb, r, la = _proj_ab(
                xp, g_mix, w_main, w_kv_t, w_lr, w_gate, b_gate, tmp, batch=bp)
            sh = lambda a: a.reshape(bp, tp, -1)
            oa = _sb_attention_prompt(sh(qa), kab, vab)
            s0 = jnp.zeros((bp, B_HEADS, B_DK, B_DV), F32)
            ob, sbp = _gla(sh(qb), sh(kb), sh(vb), sh(la), s0)
            xp = _layer_tail(xp, (oa.reshape(mp, -1), ob.reshape(mp, -1), r), (g_gla, w_out), *ffn, tmp)
            a_kp.append(_heads_last(ka, A_HEADS))
            a_vp.append(_heads_last(va, A_HEADS))
            b_sp.append(sbp)

            qa, ka, va, kab, vab, qb, kb, vb, r, la = _proj_ab(
                xs, g_mix, w_main, w_kv, w_lr, w_gate, b_gate, tms)
            sh = lambda a: a.reshape(bs, ts, -1)
            oa = _sb_attention_sample(sh(qa), sh(kab), sh(vab), _feature_major(cache_a_k[i]),
                                      _feature_major(cache_a_v[i]))
            pad_t = lambda a: _pad_rows(sh(a), GLA_CHUNK)
            ob, sbs = _gla(pad_t(qb), pad_t(kb), pad_t(vb), pad_t(la), state_b[i])
            xs = _layer_tail(xs, (oa.reshape(ms, -1), ob[:, :ts].reshape(ms, -1), r), (g_gla, w_out), *ffn, tms)
            a_ks.append(ka.reshape(bs, ts, A_HEADS, HEAD_DIM))
            a_vs.append(va.reshape(bs, ts, A_HEADS, HEAD_DIM))
            b_ss.append(sbs)
        else:
            w_q = w_qkv_c[i][:, :C_WIDTH].astype(BF16)
            w_kv = w_qkv_c[i][:, C_WIDTH:].astype(BF16)
            w_kv_t = w_qkv_c[i][:, C_WIDTH:].T.astype(BF16)
            w_out = w_out_c[i].astype(BF16)

            q, k, v, kb16, vb16 = _proj_c(xp, g_mix, w_q, w_kv_t, tmp, batch=bp)
            bias = _band_bias(rel_bias_c[i], ATT_BLOCK, ATT_BLOCK)
            oc = _band_attention_prompt(q.reshape(bp, tp, -1), kb16, vb16, bias)
            xp = _layer_tail(xp, (oc.reshape(mp, -1),), (w_out,), *ffn, tmp)
            keep = min(C_LEFT_CHUNKS * CHUNK, tp)
            c_kp.append(_heads_last(k[:, :, tp - keep:], C_HEADS))
            c_vp.append(_heads_last(v[:, :, tp - keep:], C_HEADS))

            q, k, v, kb16, vb16 = _proj_c(xs, g_mix, w_q, w_kv, tms)
            sh = lambda a: a.reshape(bs, ts, -1)
            bias = _band_bias(rel_bias_c[i], ts, ts)
            oc = _band_attention_sample(sh(q), sh(kb16), sh(vb16), _feature_major(cache_c_k[i]),
                                        _feature_major(cache_c_v[i]), bias)
            xs = _layer_tail(xs, (oc.reshape(ms, -1),), (w_out,), *ffn, tms)
            c_ks.append(k.reshape(bs, ts, C_HEADS, HEAD_DIM))
            c_vs.append(v.reshape(bs, ts, C_HEADS, HEAD_DIM))

    y_prompt = xp.reshape(bp, tp, d)
    y_sample = xs.reshape(bs, ts, d)
    return (y_prompt, y_sample, jnp.stack(a_kp), jnp.stack(a_vp), jnp.stack(a_ks), jnp.stack(a_vs),
            jnp.stack(b_sp), jnp.stack(b_ss), jnp.stack(c_kp), jnp.stack(c_vp),
            jnp.stack(c_ks), jnp.stack(c_vs))
```

```python
import functools

import jax
import jax.numpy as jnp
from jax import lax
from jax.experimental import pallas as pl
from jax.experimental.pallas import tpu as pltpu

F32 = jnp.float32
BF16 = jnp.bfloat16

EPS = 1e-6
HEAD_DIM = 64
LANES = 128
A_HEADS = 8
A_WIDTH = A_HEADS * HEAD_DIM
B_HEADS = 4
B_DK = 64
B_DV = 128
B_KW = B_HEADS * B_DK
B_VW = B_HEADS * B_DV
B_GATE_RANK = 16
B_GATE_TEMP = 16.0
GLA_CHUNK = 64
C_HEADS = 16
C_WIDTH = C_HEADS * HEAD_DIM
CHUNK = 64
C_LEFT_CHUNKS = 8
REL_MIN = -(CHUNK - 1)
REL_MAX = 128
ATT_BLOCK = 256
NEG_BIG = -1e30
LOG2E = 1.4426950408889634
SB_DEAD = 104.0
SB_PAIRS_PER_STEP = 4
BAND_PAIRS_PER_STEP = 4
GLA_GROUP = 4
VMEM_LIMIT = 56 * 1024 * 1024


def _params(*sem):
    return pltpu.CompilerParams(dimension_semantics=sem, vmem_limit_bytes=VMEM_LIMIT)


class _Slab:
    def __init__(self, stacked, index):
        self.stacked, self.index, self.shape = stacked, index, stacked.shape[1:]


def _operand(a):
    return a.stacked if isinstance(a, _Slab) else a


def _resident(a):
    if isinstance(a, _Slab):
        index = (a.index,) + (0,) * len(a.shape)
        return pl.BlockSpec((None, *a.shape), lambda *_: index, pipeline_mode=pl.Buffered(1))
    return pl.BlockSpec(a.shape, lambda *_: (0,) * a.ndim, pipeline_mode=pl.Buffered(1))


def _rms(x, g):
    return x * lax.rsqrt(jnp.mean(x * x, axis=-1, keepdims=True) + EPS) * g


def _log_sigmoid_pair(z):
    l = jnp.log1p(jnp.exp(-jnp.abs(z)))
    return jnp.minimum(z, 0.0) - l, jnp.minimum(-z, 0.0) - l


def _split_bf16(x):
    hi = x.astype(BF16)
    lo = (x - hi.astype(F32)).astype(BF16)
    return hi, lo


def _dot(a, b):
    return jnp.dot(a, b, preferred_element_type=F32)


def _dot_nt(a, b):
    return lax.dot_general(a, b, (((1,), (1,)), ((), ())), preferred_element_type=F32)


def _dot_tn(a, b):
    return lax.dot_general(a, b, (((0,), (0,)), ((), ())), preferred_element_type=F32)


def _pair_cols(p):
    return slice(p * LANES, (p + 1) * LANES)


class _KV:
    def __init__(self, k, v, feature_major):
        self.k, self.v, self.feature_major = k, v, feature_major
        self.n_keys = k.shape[1] if feature_major else k.shape[0]

    def scores(self, q_h, p):
        if self.feature_major:
            return _dot(q_h, self.k[_pair_cols(p), :])
        return _dot_nt(q_h, self.k[:, _pair_cols(p)])

    def values(self, p):
        return self.v[_pair_cols(p), :] if self.feature_major else self.v[:, _pair_cols(p)]

    def weighted(self, w, v_p):
        return _dot_nt(w, v_p) if self.feature_major else _dot(w, v_p)

    def head_lanes(self):
        shape = (LANES, self.n_keys) if self.feature_major else (self.n_keys, LANES)
        return lax.broadcasted_iota(jnp.int32, shape, 0 if self.feature_major else 1) < HEAD_DIM


def _split_heads(q_ref, n_pairs, tq):
    lo = lax.broadcasted_iota(jnp.int32, (tq, LANES), 1) < HEAD_DIM
    heads = []
    for p in range(n_pairs):
        q = q_ref[:, _pair_cols(p)]
        heads += [jnp.where(lo, q, jnp.zeros_like(q)), jnp.where(lo, jnp.zeros_like(q), q)]
    return heads, lo


def _emit_kv(y, wkv_ref, k_ref, v_ref, kb_ref, vb_ref, feature_major):
    if feature_major:
        kv = _dot_nt(wkv_ref[...], y)
        width = kv.shape[0] // 2
        k, v = kv[:width, :], kv[width:, :]
    else:
        kv = _dot(y, wkv_ref[...])
        width = kv.shape[1] // 2
        k, v = kv[:, :width], kv[:, width:]
    k_ref[...] = k
    v_ref[...] = v
    kb_ref[...] = k.astype(BF16)
    vb_ref[...] = v.astype(BF16)


def _proj_ab_kernel(x_ref, g_ref, w_ref, wkv_ref, wlr_ref, wgate_ref, bgate_ref,
                    qa_ref, ka_ref, va_ref, kab_ref, vab_ref, qb_ref, kb_ref, vb_ref, r_ref, la_ref,
                    *, feature_major):
    y = _rms(x_ref[...], g_ref[...]).astype(BF16)
    _emit_kv(y, wkv_ref, ka_ref, va_ref, kab_ref, vab_ref, feature_major)
    z = _dot(y, w_ref[...])
    c = 0
    qa_ref[...] = (z[:, c:c + A_WIDTH] * (HEAD_DIM ** -0.5)).astype(BF16); c += A_WIDTH
    qb_ref[...] = z[:, c:c + B_KW]; c += B_KW
    kb_ref[...] = z[:, c:c + B_KW]; c += B_KW
    vb_ref[...] = z[:, c:c + B_VW]; c += B_VW
    r_ref[...] = z[:, c:c + B_VW]
    g_lr = _dot(y, wlr_ref[...])
    gate = _dot(g_lr.astype(BF16), wgate_ref[...]) + bgate_ref[...]
    la_ref[...] = _log_sigmoid_pair(gate)[0] * (1.0 / B_GATE_TEMP)


def _kv_out(m, width, tm, batch):
    if batch is None:
        spec = pl.BlockSpec((tm, width), lambda i: (i, 0))
        shape = (m, width)
    else:
        t = m // batch
        assert t % tm == 0
        spec = pl.BlockSpec((None, width, tm), lambda i: (i // (t // tm), 0, i % (t // tm)))
        shape = (batch, width, t)
    return [spec] * 4, [jax.ShapeDtypeStruct(shape, dt) for dt in (F32, F32, BF16, BF16)]


def _proj_ab(x, g, w_main, w_kv, w_lr, w_gate, b_gate, tm, batch=None):
    m, d = x.shape
    row = lambda n: pl.BlockSpec((tm, n), lambda i: (i, 0))
    params = (g, w_main, w_kv, w_lr, w_gate, b_gate)
    kv_specs, kv_shapes = _kv_out(m, A_WIDTH, tm, batch)
    rest =[(B_KW, F32), (B_KW, F32), (B_VW, F32), (B_VW, F32), (B_KW, F32)]
    return pl.pallas_call(
        functools.partial(_proj_ab_kernel, feature_major=batch is not None),
        grid=(m // tm,),
        in_specs=[row(d)] + [_resident(a) for a in params],
        out_specs=[row(A_WIDTH)] + kv_specs + [row(n) for n, _ in rest],
        out_shape=([jax.ShapeDtypeStruct((m, A_WIDTH), BF16)] + kv_shapes
                   + [jax.ShapeDtypeStruct((m, n), dt) for n, dt in rest]),
        compiler_params=_params("parallel"),
        name="proj_ab",
    )(x, *[_operand(a) for a in params])


def _proj_c_kernel(x_ref, g_ref, wq_ref, wkv_ref, q_ref, k_ref, v_ref, kb_ref, vb_ref, *, feature_major):
    y = _rms(x_ref[...], g_ref[...]).astype(BF16)
    _emit_kv(y, wkv_ref, k_ref, v_ref, kb_ref, vb_ref, feature_major)
    q_ref[...] = (_dot(y, wq_ref[...]) * (HEAD_DIM ** -0.5 * LOG2E)).astype(BF16)


def _proj_c(x, g, w_q, w_kv, tm, batch=None):
    m, d = x.shape
    row = lambda n: pl.BlockSpec((tm, n), lambda i: (i, 0))
    kv_specs, kv_shapes = _kv_out(m, C_WIDTH, tm, batch)
    return pl.pallas_call(
        functools.partial(_proj_c_kernel, feature_major=batch is not None),
        grid=(m // tm,),
        in_specs=[row(d)] + [_resident(a) for a in (g, w_q, w_kv)],
        out_specs=[row(C_WIDTH)] + kv_specs,
        out_shape=[jax.ShapeDtypeStruct((m, C_WIDTH), BF16)] + kv_shapes,
        compiler_params=_params("parallel"),
        name="proj_c",
    )(x, *[_operand(a) for a in (g, w_q, w_kv)])


def _layer_tail_kernel(x_ref, *refs, gla_merge, final_norm):
    if gla_merge:
        oa_ref, ob_ref, r_ref, ggla_ref, wout_ref, *refs = refs
        ob = ob_ref[...]
        parts = []
        for h in range(B_HEADS):
            seg = ob[:, h * B_DV:(h + 1) * B_DV]
            parts.append(seg * lax.rsqrt(jnp.mean(seg * seg, axis=-1, keepdims=True) + EPS))
        r = r_ref[...]
        obn = jnp.concatenate(parts, axis=-1) * ggla_ref[...] * (r * jax.nn.sigmoid(r))
        mix = _dot(oa_ref[...], wout_ref[:A_WIDTH, :]) + _dot(obn.astype(BF16), wout_ref[A_WIDTH:, :])
    else:
        oc_ref, wout_ref, *refs = refs
        mix = _dot(oc_ref[...], wout_ref[...])
    gffn_ref, wg_ref, wu_ref, wd_ref, *refs = refs
    x = x_ref[...] + mix
    y = _rms(x, gffn_ref[...]).astype(BF16)
    h = _dot(y, wg_ref[...])
    u = _dot(y, wu_ref[...])
    a = (h * jax.nn.sigmoid(h) * u).astype(BF16)
    x = x + _dot(a, wd_ref[...])
    if final_norm:
        gfin_ref, o_ref = refs
        o_ref[...] = _rms(x, gfin_ref[...])
    else:
        (o_ref,) = refs
        o_ref[...] = x


def _layer_tail(x, mixer_out, mixer_params, g_ffn, wg, wu, wd, g_fin, tm):
    m, d = x.shape
    row = lambda a: pl.BlockSpec((tm, a.shape[1]), lambda i: (i, 0))
    resident = [*mixer_params, g_ffn, wg, wu, wd] + ([] if g_fin is None else [g_fin])
    return pl.pallas_call(
        functools.partial(_layer_tail_kernel, gla_merge=len(mixer_out) == 3, final_norm=g_fin is not None),
        grid=(m // tm,),
        in_specs=[row(x)] + [row(a) for a in mixer_out] + [_resident(a) for a in resident],
        out_specs=row(x),
        out_shape=jax.ShapeDtypeStruct((m, d), F32),
        compiler_params=_params("parallel"),
        name="layer_tail",
    )(x, *mixer_out, *[_operand(a) for a in resident])


def _sb_core(q_heads, lo, first_kv, first_mask, earlier_kv, n_earlier, acc_ref, c_ref, o_ref, n_pairs):
    tk = first_kv.n_keys
    heads = range(2 * n_pairs)
    later = (lax.broadcasted_iota(jnp.int32, (tk, tk), 0)
             > lax.broadcasted_iota(jnp.int32, (tk, tk), 1)).astype(BF16)
    c_ref[...] = jnp.zeros_like(c_ref)

    def block(kv, mask):
        z = [kv.scores(q_heads[h], h // 2) for h in heads]
        log_beta, drop, after = [], [], []
        for h in heads:
            d = jnp.maximum(z[h], 0.0) + jnp.log(1.0 + jnp.exp(-jnp.abs(z[h])))
            log_beta.append(z[h] - d)
            drop.append(d if mask is None else jnp.where(mask, d, 0.0))
        for h in heads:
            hi, lo_part = _split_bf16(drop[h])
            after.append(_dot(hi, later) + _dot(lo_part, later))
        pv = []
        for h in heads:
            c = c_ref[h]
            w = jnp.exp(log_beta[h] - after[h] - c)
            if mask is not None:
                w = jnp.where(mask, w, 0.0)
            c_ref[h] = c + after[h][:, 0:1] + drop[h][:, 0:1]
            pv.append(kv.weighted(w.astype(BF16), kv.values(h // 2)))
        out = [jnp.where(lo, pv[2 * p], pv[2 * p + 1]) for p in range(n_pairs)]
        return out[0] if n_pairs == 1 else jnp.concatenate(out, axis=-1)

    def all_dead():
        return jnp.min(c_ref[...]) > SB_DEAD

    acc_ref[...] = block(first_kv, first_mask)

    def cond(carry):
        n, dead = carry
        return (n < n_earlier) & jnp.logical_not(dead)

    def body(carry):
        n, _ = carry
        acc_ref[...] += block(earlier_kv(n), None)
        return n + 1, all_dead()

    lax.while_loop(cond, body, (jnp.int32(0), all_dead()))
    o_ref[...] = acc_ref[...].astype(o_ref.dtype)


def _sb_prompt_kernel(q_ref, k_ref, v_ref, o_ref, acc_ref, c_ref, *, tb, n_pairs):
    qi = pl.program_id(2)
    q_heads, lo = _split_heads(q_ref, n_pairs, tb)

    def kv_block(j):
        keys = pl.ds(pl.multiple_of(j * tb, tb), tb)
        return _KV(k_ref[:, keys], v_ref[:, keys], True)

    strictly_earlier = (lax.broadcasted_iota(jnp.int32, (tb, tb), 1)
                        < lax.broadcasted_iota(jnp.int32, (tb, tb), 0))
    _sb_core(q_heads, lo, kv_block(qi), strictly_earlier, lambda n: kv_block(qi - 1 - n), qi,
             acc_ref, c_ref, o_ref, n_pairs)


def _sb_sample_kernel(q_ref, kn_ref, vn_ref, kc_ref, vc_ref, o_ref, acc_ref, c_ref, kpad_ref, vpad_ref,
                      *, ts, tk, n_cache_blocks, n_pairs):
    q_heads, lo = _split_heads(q_ref, n_pairs, ts)
    kpad_ref[...] = jnp.zeros_like(kpad_ref)
    vpad_ref[...] = jnp.zeros_like(vpad_ref)
    kpad_ref[:ts, :] = kn_ref[...]
    vpad_ref[:ts, :] = vn_ref[...]

    def cache_block(n):
        keys = pl.ds(pl.multiple_of((n_cache_blocks - 1 - n) * tk, tk), tk)
        return _KV(kc_ref[:, keys].astype(BF16), vc_ref[:, keys].astype(BF16), True)

    strictly_earlier = (lax.broadcasted_iota(jnp.int32, (ts, tk), 1)
                        < lax.broadcasted_iota(jnp.int32, (ts, tk), 0))
    _sb_core(q_heads, lo, _KV(kpad_ref[...], vpad_ref[...], False), strictly_earlier, cache_block,
             n_cache_blocks, acc_ref, c_ref, o_ref, n_pairs)


def _sb_attention_prompt(q, k, v):
    b, t, _ = q.shape
    tb = ATT_BLOCK
    assert t % tb == 0
    n_pairs = SB_PAIRS_PER_STEP
    w = n_pairs * LANES
    kv_spec = pl.BlockSpec((None, w, t), lambda bi, hp, qi: (bi, hp, 0))
    q_spec = pl.BlockSpec((None, tb, w), lambda bi, hp, qi: (bi, qi, hp))
    return pl.pallas_call(
        functools.partial(_sb_prompt_kernel, tb=tb, n_pairs=n_pairs),
        grid=(b, A_WIDTH // w, t // tb),
        in_specs=[q_spec, kv_spec, kv_spec],
        out_specs=q_spec,
        out_shape=jax.ShapeDtypeStruct(q.shape, BF16),
        scratch_shapes=[pltpu.VMEM((tb, w), F32), pltpu.VMEM((2 * n_pairs, tb, 1), F32)],
        compiler_params=_params("parallel", "parallel", "parallel"),
        name="sb_attention",
    )(q, k, v)


def _sb_attention_sample(q, k_new, v_new, k_cache, v_cache, layer):
    b, ts, _ = q.shape
    past = k_cache.shape[3]
    tk = ATT_BLOCK
    assert past % tk == 0 and ts <= tk
    n_pairs = SB_PAIRS_PER_STEP
    w = n_pairs * LANES
    new_spec = pl.BlockSpec((None, ts, w), lambda bi, hp: (bi, 0, hp))
    cache_spec = pl.BlockSpec((None, None, w, past), lambda bi, hp: (layer, bi, hp, 0))
    return pl.pallas_call(
        functools.partial(_sb_sample_kernel, ts=ts, tk=tk, n_cache_blocks=past // tk, n_pairs=n_pairs),
        grid=(b, A_WIDTH // w),
        in_specs=[new_spec, new_spec, new_spec, cache_spec, cache_spec],
        out_specs=new_spec,
        out_shape=jax.ShapeDtypeStruct(q.shape, BF16),
        scratch_shapes=[pltpu.VMEM((ts, w), F32), pltpu.VMEM((2 * n_pairs, ts, 1), F32),
                        pltpu.VMEM((tk, w), BF16), pltpu.VMEM((tk, w), BF16)],
        compiler_params=_params("parallel", "parallel"),
        name="sb_attention_sample",
    )(q, k_new, v_new, k_cache, v_cache)


def _band_core(q_heads, lo, kv, pens, bias_ref, o_ref, n_pairs):
    heads = range(2 * n_pairs)
    blocks = range(len(kv))
    z = [[kv[i].scores(q_heads[h], h // 2) + bias_ref[h, i] for i in blocks] for h in heads]
    acc = []
    for h in heads:
        m = None
        for i in blocks:
            mi = jnp.max(z[h][i], axis=-1, keepdims=True)
            if pens[i] is not None:
                mi = mi + pens[i]
            m = mi if m is None else jnp.maximum(m, mi)
        a = None
        for i in blocks:
            shift = m if pens[i] is None else m - pens[i]
            p = jnp.exp2(z[h][i] - shift).astype(BF16)
            v = kv[i].values(h // 2)
            ones = jnp.ones_like(v)
            first = kv[i].head_lanes()
            v = jnp.where(first, v, ones) if h % 2 == 0 else jnp.where(first, ones, v)
            pv = kv[i].weighted(p, v)
            a = pv if a is None else a + pv
        acc.append(a)
    for p in range(n_pairs):
        a0, a1 = acc[2 * p], acc[2 * p + 1]
        o_ref[:, _pair_cols(p)] = jnp.where(lo, a0 / pltpu.roll(a0, HEAD_DIM, axis=1),
                                            a1 / pltpu.roll(a1, HEAD_DIM, axis=1)).astype(o_ref.dtype)


def _band_prompt_kernel(q_ref, k_ref, v_ref, bias_ref, o_ref, *, tq, tk, n_pairs):
    qi = pl.program_id(2)
    q_heads, lo = _split_heads(q_ref, n_pairs, tq)
    kv, pens = [], []
    for dj in range(3):
        j = qi - dj
        pens.append(None if dj == 0 else jnp.where(j >= 0, 0.0, NEG_BIG).astype(F32))
        keys = pl.ds(pl.multiple_of(jnp.maximum(j, 0) * tk, tk), tk)
        kv.append(_KV(k_ref[:, keys], v_ref[:, keys], True))
    _band_core(q_heads, lo, kv, pens, bias_ref, o_ref, n_pairs)


def _band_sample_kernel(q_ref, kn_ref, vn_ref, kc_ref, vc_ref, bias_ref, o_ref, kpad_ref, vpad_ref,
                        *, ts, tk, n_cache_blocks, n_pairs):
    q_heads, lo = _split_heads(q_ref, n_pairs, ts)
    kpad_ref[...] = jnp.zeros_like(kpad_ref)
    vpad_ref[...] = jnp.zeros_like(vpad_ref)
    kpad_ref[:ts, :] = kn_ref[...]
    vpad_ref[:ts, :] = vn_ref[...]
    kv = [_KV(kpad_ref[...], vpad_ref[...], False)]
    for dj in range(1, n_cache_blocks + 1):
        keys = slice((n_cache_blocks - dj) * tk, (n_cache_blocks - dj + 1) * tk)
        kv.append(_KV(kc_ref[:, keys].astype(BF16), vc_ref[:, keys].astype(BF16), True))
    _band_core(q_heads, lo, kv, [None] * len(kv), bias_ref, o_ref, n_pairs)


def _band_attention_prompt(q, k, v, bias):
    b, t, _ = q.shape
    tq = tk = ATT_BLOCK
    assert t % tk == 0
    n_pairs = BAND_PAIRS_PER_STEP
    w = n_pairs * LANES
    kern = functools.partial(_band_prompt_kernel, tq=tq, tk=tk, n_pairs=n_pairs)
    kv_spec = pl.BlockSpec((None, w, t), lambda bi, hp, qi: (bi, hp, 0))
    q_spec = pl.BlockSpec((None, tq, w), lambda bi, hp, qi: (bi, qi, hp))
    bias_spec = pl.BlockSpec((2 * n_pairs, 3, tq, tk), lambda bi, hp, qi: (hp, 0, 0, 0))
    return pl.pallas_call(
        kern,
        grid=(b, C_WIDTH // w, t // tq),
        in_specs=[q_spec, kv_spec, kv_spec, bias_spec],
        out_specs=q_spec,
        out_shape=jax.ShapeDtypeStruct(q.shape, BF16),
        compiler_params=_params("parallel", "parallel", "parallel"),
        name="band_attention",
    )(q, k, v, bias)


def _band_attention_sample(q, k_new, v_new, k_cache, v_cache, layer, bias):
    b, ts, _ = q.shape
    wc = k_cache.shape[3]
    tk = ATT_BLOCK
    n_cache_blocks = min(wc // tk, 2)
    assert wc % (n_cache_blocks * tk) == 0 and ts <= tk
    n_pairs = BAND_PAIRS_PER_STEP
    w = n_pairs * LANES
    kern = functools.partial(_band_sample_kernel, ts=ts, tk=tk, n_cache_blocks=n_cache_blocks,
                             n_pairs=n_pairs)
    new_spec = pl.BlockSpec((None, ts, w), lambda bi, hp: (bi, 0, hp))
    cache_rows = n_cache_blocks * tk
    cache_spec = pl.BlockSpec((None, None, w, cache_rows),
                              lambda bi, hp: (layer, bi, hp, wc // cache_rows - 1))
    bias_spec = pl.BlockSpec((2 * n_pairs, 1 + n_cache_blocks, ts, tk), lambda bi, hp: (hp, 0, 0, 0))
    return pl.pallas_call(
        kern,
        grid=(b, C_WIDTH // w),
        in_specs=[new_spec, new_spec, new_spec, cache_spec, cache_spec, bias_spec],
        out_specs=new_spec,
        out_shape=jax.ShapeDtypeStruct(q.shape, BF16),
        scratch_shapes=[pltpu.VMEM((tk, w), BF16), pltpu.VMEM((tk, w), BF16)],
        compiler_params=_params("parallel", "parallel"),
        name="band_attention_sample",
    )(q, k_new, v_new, k_cache, v_cache, bias)


def _band_bias_kernel(g_ref, o_ref, *, rows, valid0):
    tk = ATT_BLOCK
    q_chunk = lax.broadcasted_iota(jnp.int32, (rows, tk), 0) // CHUNK
    col = lax.broadcasted_iota(jnp.int32, (rows, tk), 1)
    k_chunk = col // CHUNK
    for dj in range(3):
        g = jnp.broadcast_to(g_ref[0, dj], (rows, 2 * tk))
        tile = pltpu.roll(g, 0, axis=1, stride=1, stride_axis=0)[:, :tk]
        diff = dj * (tk // CHUNK) + q_chunk - k_chunk
        seen = (diff >= 0) & (diff <= C_LEFT_CHUNKS)
        if dj == 0:
            seen = seen & (col < valid0)
        o_ref[0, dj] = jnp.where(seen, tile * LOG2E, NEG_BIG)


def _band_bias(rel_table, rows, valid0):
    tk = ATT_BLOCK
    c = jnp.arange(2 * tk, dtype=jnp.int32)
    u = jnp.where(c <= tk, -c, 2 * tk - c)
    idx = jnp.clip(jnp.arange(3, dtype=jnp.int32)[:, None] * tk + u[None, :], REL_MIN, REL_MAX) - REL_MIN
    g = rel_table[:, idx].astype(F32).reshape(C_HEADS, 3, 1, 2 * tk)
    return pl.pallas_call(
        functools.partial(_band_bias_kernel, rows=rows, valid0=valid0),
        grid=(C_HEADS,),
        in_specs=[pl.BlockSpec((1, 3, 1, 2 * tk), lambda h: (h, 0, 0, 0))],
        out_specs=pl.BlockSpec((1, 3, rows, tk), lambda h: (h, 0, 0, 0)),
        out_shape=jax.ShapeDtypeStruct((C_HEADS, 3, rows, tk), F32),
        compiler_params=_params("parallel"),
        name="band_bias",
    )(g)


def _gla_kernel(q_ref, k_ref, v_ref, la_ref, s0_ref, o_ref, s_out_ref, qg_ref, dec_ref, st_ref,
                *, n_chunks, group):
    L = GLA_CHUNK
    row = lax.broadcasted_iota(jnp.int32, (L, L), 0)
    colm = lax.broadcasted_iota(jnp.int32, (L, L), 1)
    tri = (colm <= row).astype(BF16)
    causal = colm <= row
    lane = lax.broadcasted_iota(jnp.int32, (L, LANES), 1)
    sub = lax.broadcasted_iota(jnp.int32, (LANES, B_DV), 0)
    mine = [(lane >= h * B_DK) & (lane < (h + 1) * B_DK) for h in range(2)]
    mine_rows = [(sub >= h * B_DK) & (sub < (h + 1) * B_DK) for h in range(2)]
    v_cols = [slice(h * B_DV, (h + 1) * B_DV) for h in range(2)]

    def chunk_rows(c):
        return pl.ds(pl.multiple_of(c * L, L), L)

    def local_pass(g, carry):
        chunks = [g * group + i for i in range(group)]
        rows = [chunk_rows(c) for c in chunks]
        b = []
        for r in rows:
            g_hi, g_lo = _split_bf16(la_ref[r, :])
            b.append(_dot(tri, g_hi) + _dot(tri, g_lo))
        qg_h, kg, kd_t = [], [], []
        for i, r in enumerate(rows):
            k = k_ref[r, :]
            qg = q_ref[r, :] * (B_DK ** -0.5) * jnp.exp(b[i])
            qg_ref[r, :] = qg.astype(BF16)
            qg_h.append([jnp.where(mine[h], qg, 0.0).astype(BF16) for h in range(2)])
            kg.append((k * jnp.exp(-b[i])).astype(BF16))
            b_t = b[i].T
            b_last = b_t[:, L - 1:L]
            kd_t.append((k.T * jnp.exp(b_last - b_t)).astype(BF16))
            dec_ref[chunks[i]] = jnp.broadcast_to(jnp.exp(b_last), (LANES, B_DV))
        att = [[jnp.where(causal, _dot_nt(qg_h[i][h], kg[i]), 0.0).astype(BF16) for h in range(2)]
               for i in range(group)]
        for i, r in enumerate(rows):
            for h in range(2):
                v_h = v_ref[r, v_cols[h]].astype(BF16)
                o_ref[r, v_cols[h]] = _dot(att[i][h], v_h)
                st_ref[chunks[i], h] = jnp.where(mine_rows[h], _dot(kd_t[i], v_h), 0.0)
        return carry

    lax.fori_loop(0, n_chunks // group, local_pass, 0)

    zeros_state = jnp.zeros((B_DK, B_DV), F32)
    s_init = (jnp.concatenate([s0_ref[0], zeros_state], axis=0),
              jnp.concatenate([zeros_state, s0_ref[1]], axis=0))

    def scan_pass(c, s):
        new = []
        for h in range(2):
            own = st_ref[c, h]
            st_ref[c, h] = s[h]
            new.append(dec_ref[c] * s[h] + own)
        return tuple(new)

    s_final = lax.fori_loop(0, n_chunks, scan_pass, s_init)
    s_out_ref[0] = s_final[0][:B_DK, :]
    s_out_ref[1] = s_final[1][B_DK:, :]

    def state_pass(g, carry):
        for i in range(group):
            c = g * group + i
            r = chunk_rows(c)
            for h in range(2):
                o_ref[r, v_cols[h]] += _dot(qg_ref[r, :], st_ref[c, h].astype(BF16))
        return carry

    lax.fori_loop(0, n_chunks // group, state_pass, 0)


def _gla(q, k, v, la, s0):
    b, t, _ = q.shape
    qk_spec = pl.BlockSpec((None, t, LANES), lambda bi, p: (bi, 0, p))
    v_spec = pl.BlockSpec((None, t, 2 * B_DV), lambda bi, p: (bi, 0, p))
    s_spec = pl.BlockSpec((None, 2, B_DK, B_DV), lambda bi, p: (bi, p, 0, 0))
    n_chunks = t // GLA_CHUNK
    group = GLA_GROUP if n_chunks % GLA_GROUP == 0 else 1
    return pl.pallas_call(
        functools.partial(_gla_kernel, n_chunks=n_chunks, group=group),
        grid=(b, B_HEADS // 2),
        in_specs=[qk_spec, qk_spec, v_spec, qk_spec, s_spec],
        out_specs=[v_spec, s_spec],
        out_shape=[jax.ShapeDtypeStruct(v.shape, F32), jax.ShapeDtypeStruct(s0.shape, F32)],
        scratch_shapes=[pltpu.VMEM((t, LANES), BF16), pltpu.VMEM((n_chunks, LANES, B_DV), F32),
                        pltpu.VMEM((n_chunks, 2, LANES, B_DV), F32)],
        compiler_params=_params("parallel", "parallel"),
        name="gla",
    )(q, k, v, la, s0)


def _pad_rows(x, n):
    return jnp.pad(x, ((0, 0), (0, n - x.shape[1]), (0, 0)))


def _heads_last(x, heads):
    b, _, s = x.shape
    return jnp.transpose(x.reshape(b, heads, HEAD_DIM, s), (0, 3, 1, 2))


def _feature_major(cache):
    n, b, s, heads, hd = cache.shape
    return jnp.transpose(cache, (0, 1, 3, 4, 2)).reshape(n, b, heads * hd, s)


def _row_tile(m):
    for tm in (512, 256, 128, 64, 32, 16, 8):
        if m % tm == 0:
            return tm
    raise ValueError(f"token count {m} is not a multiple of 8")


def kernel(x_prompt, x_sample, cache_a_k, cache_a_v, state_b, cache_c_k, cache_c_v, norm_mix_g, norm_ffn_g, w_in_ab, w_gate_b, b_gate_b, norm_gla_g, w_out_ab, w_qkv_c, rel_bias_c, w_out_c, w_ffn_gate, w_ffn_up, w_ffn_down, norm_final_g):
    bp, tp, d = x_prompt.shape
    bs, ts, _ = x_sample.shape
    depth = norm_mix_g.shape[0]
    past = cache_a_k.shape[2]
    wc = cache_c_k.shape[2]
    assert tp % ATT_BLOCK == 0 and past % ATT_BLOCK == 0 and wc % ATT_BLOCK == 0
    assert ts <= GLA_CHUNK and ts % 8 == 0
    mp, ms = bp * tp, bs * ts
    tmp, tms = _row_tile(tp), _row_tile(ms)
    xp = x_prompt.reshape(mp, d)
    xs = x_sample.reshape(ms, d)
    row2 = lambda a: a.reshape(1, -1)

    a_kp, a_vp, a_ks, a_vs, b_sp, b_ss = [], [], [], [], [], []
    c_kp, c_vp, c_ks, c_vs = [], [], [], []

    wg_all, wu_all, wd_all = (w.astype(BF16) for w in (w_ffn_gate, w_ffn_up, w_ffn_down))
    kv0, kv1 = A_WIDTH, 3 * A_WIDTH
    o = 3 * A_WIDTH + 2 * B_KW + B_VW
    w_main_all = jnp.concatenate([w_in_ab[:, :, :kv0], w_in_ab[:, :, kv1:o], w_in_ab[:, :, o + B_GATE_RANK:]],
                                 axis=2).astype(BF16)
    w_kv_ab_all = w_in_ab[:, :, kv0:kv1].astype(BF16)
    w_kv_ab_t_all = jnp.swapaxes(w_in_ab, 1, 2)[:, kv0:kv1].astype(BF16)
    w_lr_all = jnp.pad(w_in_ab[:, :, o:o + B_GATE_RANK],
                       ((0, 0), (0, 0), (0, LANES - B_GATE_RANK))).astype(BF16)
    w_gate_all = jnp.pad(w_gate_b, ((0, 0), (0, LANES - B_GATE_RANK), (0, 0))).astype(BF16)
    w_out_ab_all = w_out_ab.astype(BF16)
    w_q_all = w_qkv_c[:, :, :C_WIDTH].astype(BF16)
    w_kv_c_all = w_qkv_c[:, :, C_WIDTH:].astype(BF16)
    w_kv_c_t_all = jnp.swapaxes(w_qkv_c[:, :, C_WIDTH:], 1, 2).astype(BF16)
    w_out_c_all = w_out_c.astype(BF16)
    cache_a_k_fm, cache_a_v_fm, cache_c_k_fm, cache_c_v_fm = (
        _feature_major(c) for c in (cache_a_k, cache_a_v, cache_c_k, cache_c_v))

    for layer in range(depth):
        i = layer // 2
        g_mix = row2(norm_mix_g[layer])
        ffn = (row2(norm_ffn_g[layer]), _Slab(wg_all, layer), _Slab(wu_all, layer), _Slab(wd_all, layer),
               row2(norm_final_g) if layer == depth - 1 else None)
        if layer % 2 == 0:
            w_main, w_kv, w_kv_t, w_lr, w_gate, w_out = (
                _Slab(w, i) for w in (w_main_all, w_kv_ab_all, w_kv_ab_t_all, w_lr_all, w_gate_all,
                                      w_out_ab_all))
            b_gate = row2(b_gate_b[i])
            g_gla = row2(norm_gla_g[i])

            qa, ka, va, kab, vab, qb, kb, vb, r, la = _proj_ab(
                xp, g_mix, w_main, w_kv_t, w_lr, w_gate, b_gate, tmp, batch=bp)
            sh = lambda a: a.reshape(bp, tp, -1)
            oa = _sb_attention_prompt(sh(qa), kab, vab)
            s0 = jnp.zeros((bp, B_HEADS, B_DK, B_DV), F32)
            ob, sbp = _gla(sh(qb), sh(kb), sh(vb), sh(la), s0)
            xp = _layer_tail(xp, (oa.reshape(mp, -1), ob.reshape(mp, -1), r), (g_gla, w_out), *ffn, tmp)
            a_kp.append(_heads_last(ka, A_HEADS))
            a_vp.append(_heads_last(va, A_HEADS))
            b_sp.append(sbp)

            qa, ka, va, kab, vab, qb, kb, vb, r, la = _proj_ab(
                xs, g_mix, w_main, w_kv, w_lr, w_gate, b_gate, tms)
            sh = lambda a: a.reshape(bs, ts, -1)
            oa = _sb_attention_sample(sh(qa), sh(kab), sh(vab), cache_a_k_fm, cache_a_v_fm, i)
            pad_t = lambda a: _pad_rows(sh(a), GLA_CHUNK)
            ob, sbs = _gla(pad_t(qb), pad_t(kb), pad_t(vb), pad_t(la), state_b[i])
            xs = _layer_tail(xs, (oa.reshape(ms, -1), ob[:, :ts].reshape(ms, -1), r), (g_gla, w_out), *ffn, tms)
            a_ks.append(ka.reshape(bs, ts, A_HEADS, HEAD_DIM))
            a_vs.append(va.reshape(bs, ts, A_HEADS, HEAD_DIM))
            b_ss.append(sbs)
        else:
            w_q, w_kv, w_kv_t, w_out = (
                _Slab(w, i) for w in (w_q_all, w_kv_c_all, w_kv_c_t_all, w_out_c_all))

            q, k, v, kb16, vb16 = _proj_c(xp, g_mix, w_q, w_kv_t, tmp, batch=bp)
            bias = _band_bias(rel_bias_c[i], ATT_BLOCK, ATT_BLOCK)
            oc = _band_attention_prompt(q.reshape(bp, tp, -1), kb16, vb16, bias)
            xp = _layer_tail(xp, (oc.reshape(mp, -1),), (w_out,), *ffn, tmp)
            keep = min(C_LEFT_CHUNKS * CHUNK, tp)
            c_kp.append(_heads_last(k[:, :, tp - keep:], C_HEADS))
            c_vp.append(_heads_last(v[:, :, tp - keep:], C_HEADS))

            q, k, v, kb16, vb16 = _proj_c(xs, g_mix, w_q, w_kv, tms)
            sh = lambda a: a.reshape(bs, ts, -1)
            bias = _band_bias(rel_bias_c[i], ts, ts)
            oc = _band_attention_sample(sh(q), sh(kb16), sh(vb16), cache_c_k_fm, cache_c_v_fm, i, bias)
            xs = _layer_tail(xs, (oc.reshape(ms, -1),), (w_out,), *ffn, tms)
            c_ks.append(k.reshape(bs, ts, C_HEADS, HEAD_DIM))
            c_vs.append(v.reshape(bs, ts, C_HEADS, HEAD_DIM))

    y_prompt = xp.reshape(bp, tp, d)
    y_sample = xs.reshape(bs, ts, d)
    return (y_prompt, y_sample, jnp.stack(a_kp), jnp.stack(a_vp), jnp.stack(a_ks), jnp.stack(a_vs),
            jnp.stack(b_sp), jnp.stack(b_ss), jnp.stack(c_kp), jnp.stack(c_vp),
            jnp.stack(c_ks), jnp.stack(c_vs))
```

```python
import functools

import jax
import jax.numpy as jnp
from jax import lax
from jax.experimental import pallas as pl
from jax.experimental.pallas import tpu as pltpu

F32 = jnp.float32
BF16 = jnp.bfloat16

EPS = 1e-6
HEAD_DIM = 64
LANES = 128
A_HEADS = 8
A_WIDTH = A_HEADS * HEAD_DIM
B_HEADS = 4
B_DK = 64
B_DV = 128
B_KW = B_HEADS * B_DK
B_VW = B_HEADS * B_DV
B_GATE_RANK = 16
B_GATE_TEMP = 16.0
GLA_CHUNK = 64
C_HEADS = 16
C_WIDTH = C_HEADS * HEAD_DIM
CHUNK = 64
C_LEFT_CHUNKS = 8
REL_MIN = -(CHUNK - 1)
REL_MAX = 128
ATT_BLOCK = 256
NEG_BIG = -1e30
LOG2E = 1.4426950408889634
SB_DEAD = 104.0
SB_PAIRS_PER_STEP = 4
BAND_PAIRS_PER_STEP = 4
GLA_GROUP = 4
VMEM_LIMIT = 56 * 1024 * 1024


def _params(*sem):
    return pltpu.CompilerParams(dimension_semantics=sem, vmem_limit_bytes=VMEM_LIMIT)


class _Slab:
    def __init__(self, stacked, index):
        self.stacked, self.index, self.shape = stacked, index, stacked.shape[1:]


def _operand(a):
    return a.stacked if isinstance(a, _Slab) else a


def _resident(a):
    if isinstance(a, _Slab):
        index = (a.index,) + (0,) * len(a.shape)
        return pl.BlockSpec((None, *a.shape), lambda *_: index, pipeline_mode=pl.Buffered(1))
    return pl.BlockSpec(a.shape, lambda *_: (0,) * a.ndim, pipeline_mode=pl.Buffered(1))


def _rms(x, g):
    return x * lax.rsqrt(jnp.mean(x * x, axis=-1, keepdims=True) + EPS) * g


def _log_sigmoid_pair(z):
    l = jnp.log1p(jnp.exp(-jnp.abs(z)))
    return jnp.minimum(z, 0.0) - l, jnp.minimum(-z, 0.0) - l


def _split_bf16(x):
    hi = x.astype(BF16)
    lo = (x - hi.astype(F32)).astype(BF16)
    return hi, lo


def _dot(a, b):
    return jnp.dot(a, b, preferred_element_type=F32)


def _dot_nt(a, b):
    return lax.dot_general(a, b, (((1,), (1,)), ((), ())), preferred_element_type=F32)


def _dot_tn(a, b):
    return lax.dot_general(a, b, (((0,), (0,)), ((), ())), preferred_element_type=F32)


def _pair_cols(p):
    return slice(p * LANES, (p + 1) * LANES)


class _KV:
    def __init__(self, k, v, feature_major):
        self.k, self.v, self.feature_major = k, v, feature_major
        self.n_keys = k.shape[1] if feature_major else k.shape[0]

    def scores(self, q_h, p):
        if self.feature_major:
            return _dot(q_h, self.k[_pair_cols(p), :])
        return _dot_nt(q_h, self.k[:, _pair_cols(p)])

    def values(self, p):
        return self.v[_pair_cols(p), :] if self.feature_major else self.v[:, _pair_cols(p)]

    def weighted(self, w, v_p):
        return _dot_nt(w, v_p) if self.feature_major else _dot(w, v_p)

    def head_lanes(self):
        shape = (LANES, self.n_keys) if self.feature_major else (self.n_keys, LANES)
        return lax.broadcasted_iota(jnp.int32, shape, 0 if self.feature_major else 1) < HEAD_DIM


def _split_heads(q_ref, n_pairs, tq):
    lo = lax.broadcasted_iota(jnp.int32, (tq, LANES), 1) < HEAD_DIM
    heads = []
    for p in range(n_pairs):
        q = q_ref[:, _pair_cols(p)]
        heads += [jnp.where(lo, q, jnp.zeros_like(q)), jnp.where(lo, jnp.zeros_like(q), q)]
    return heads, lo


def _emit_kv(y, wkv_ref, k_ref, v_ref, kb_ref, vb_ref, feature_major, kept_tail=None):
    if feature_major:
        kv = _dot_nt(wkv_ref[...], y)
        width = kv.shape[0] // 2
        k, v = kv[:width, :], kv[width:, :]
    else:
        kv = _dot(y, wkv_ref[...])
        width = kv.shape[1] // 2
        k, v = kv[:, :width], kv[:, width:]
    kb_ref[...] = k.astype(BF16)
    vb_ref[...] = v.astype(BF16)
    if kept_tail is None:
        k_ref[...] = k
        v_ref[...] = v
    else:
        n_tiles, n_kept = kept_tail
        tile = pl.program_id(0) % n_tiles
        tm = k.shape[1]

        @pl.when(tile >= n_tiles - n_kept)
        def _():
            cols = pl.ds(pl.multiple_of((tile - (n_tiles - n_kept)) * tm, tm), tm)
            k_ref[:, cols] = k
            v_ref[:, cols] = v


def _proj_ab_kernel(x_ref, g_ref, w_ref, wkv_ref, wlr_ref, wgate_ref, bgate_ref, *refs,
                    feature_major, n_aliased):
    qa_ref, ka_ref, va_ref, kab_ref, vab_ref, qb_ref, kb_ref, vb_ref, r_ref, la_ref = refs[n_aliased:]
    y = _rms(x_ref[...], g_ref[...]).astype(BF16)
    _emit_kv(y, wkv_ref, ka_ref, va_ref, kab_ref, vab_ref, feature_major)
    z = _dot(y, w_ref[...])
    c = 0
    qa_ref[...] = (z[:, c:c + A_WIDTH] * (HEAD_DIM ** -0.5)).astype(BF16); c += A_WIDTH
    qb_ref[...] = z[:, c:c + B_KW]; c += B_KW
    kb_ref[...] = z[:, c:c + B_KW]; c += B_KW
    vb_ref[...] = z[:, c:c + B_VW]; c += B_VW
    r_ref[...] = z[:, c:c + B_VW]
    g_lr = _dot(y, wlr_ref[...])
    gate = _dot(g_lr.astype(BF16), wgate_ref[...]) + bgate_ref[...]
    la_ref[...] = _log_sigmoid_pair(gate)[0] * (1.0 / B_GATE_TEMP)


class _KVStack:
    def __init__(self, layer, n_layers, keep, previous=None):
        self.layer, self.n_layers, self.keep, self.previous = layer, n_layers, keep, previous


def _kv_out(m, width, tm, batch, stack):
    if batch is None:
        spec = pl.BlockSpec((tm, width), lambda i: (i, 0))
        return [spec] * 4, [jax.ShapeDtypeStruct((m, width), dt) for dt in (F32, F32, BF16, BF16)], None
    t = m // batch
    assert t % tm == 0 and stack.keep % tm == 0 and stack.keep <= t
    n_tiles, layer = t // tm, stack.layer
    copy_spec = pl.BlockSpec((None, width, tm), lambda i: (i // n_tiles, 0, i % n_tiles))
    copy_shape = jax.ShapeDtypeStruct((batch, width, t), BF16)
    if stack.keep == t:
        f32_spec = pl.BlockSpec((None, None, width, tm), lambda i: (layer, i // n_tiles, 0, i % n_tiles))
        kept_tail = None
    else:
        f32_spec = pl.BlockSpec((None, None, width, stack.keep), lambda i: (layer, i // n_tiles, 0, 0))
        kept_tail = (n_tiles, stack.keep // tm)
    f32_shape = jax.ShapeDtypeStruct((stack.n_layers, batch, width, stack.keep), F32)
    return [f32_spec, f32_spec, copy_spec, copy_spec], [f32_shape, f32_shape, copy_shape, copy_shape], kept_tail


def _aliased_stack(stack, n_inputs, first_output):
    if stack is None or stack.previous is None:
        return [], [], {}
    prev = list(stack.previous)
    specs = [pl.BlockSpec(memory_space=pl.ANY)] * len(prev)
    return prev, specs, {n_inputs + j: first_output + j for j in range(len(prev))}


def _proj_ab(x, g, w_main, w_kv, w_lr, w_gate, b_gate, tm, batch=None, stack=None):
    m, d = x.shape
    row = lambda n: pl.BlockSpec((tm, n), lambda i: (i, 0))
    params = (g, w_main, w_kv, w_lr, w_gate, b_gate)
    kv_specs, kv_shapes, kept_tail = _kv_out(m, A_WIDTH, tm, batch, stack)
    assert kept_tail is None
    prev, prev_specs, aliases = _aliased_stack(stack, 1 + len(params), 1)
    rest = [(B_KW, F32), (B_KW, F32), (B_VW, F32), (B_VW, F32), (B_KW, F32)]
    return pl.pallas_call(
        functools.partial(_proj_ab_kernel, feature_major=batch is not None, n_aliased=len(prev)),
        grid=(m // tm,),
        in_specs=[row(d)] + [_resident(a) for a in params] + prev_specs,
        out_specs=[row(A_WIDTH)] + kv_specs + [row(n) for n, _ in rest],
        out_shape=([jax.ShapeDtypeStruct((m, A_WIDTH), BF16)] + kv_shapes
                   + [jax.ShapeDtypeStruct((m, n), dt) for n, dt in rest]),
        input_output_aliases=aliases,
        compiler_params=_params("parallel"),
        name="proj_ab",
    )(x, *[_operand(a) for a in params], *prev)


def _proj_c_kernel(x_ref, g_ref, wq_ref, wkv_ref, *refs, feature_major, n_aliased, kept_tail):
    q_ref, k_ref, v_ref, kb_ref, vb_ref = refs[n_aliased:]
    y = _rms(x_ref[...], g_ref[...]).astype(BF16)
    _emit_kv(y, wkv_ref, k_ref, v_ref, kb_ref, vb_ref, feature_major, kept_tail)
    q_ref[...] = (_dot(y, wq_ref[...]) * (HEAD_DIM ** -0.5 * LOG2E)).astype(BF16)


def _proj_c(x, g, w_q, w_kv, tm, batch=None, stack=None):
    m, d = x.shape
    row = lambda n: pl.BlockSpec((tm, n), lambda i: (i, 0))
    params = (g, w_q, w_kv)
    kv_specs, kv_shapes, kept_tail = _kv_out(m, C_WIDTH, tm, batch, stack)
    prev, prev_specs, aliases = _aliased_stack(stack, 1 + len(params), 1)
    return pl.pallas_call(
        functools.partial(_proj_c_kernel, feature_major=batch is not None, n_aliased=len(prev),
                          kept_tail=kept_tail),
        grid=(m // tm,),
        in_specs=[row(d)] + [_resident(a) for a in params] + prev_specs,
        out_specs=[row(C_WIDTH)] + kv_specs,
        out_shape=[jax.ShapeDtypeStruct((m, C_WIDTH), BF16)] + kv_shapes,
        input_output_aliases=aliases,
        compiler_params=_params("arbitrary" if kept_tail else "parallel"),
        name="proj_c",
    )(x, *[_operand(a) for a in params], *prev)


def _layer_tail_kernel(x_ref, *refs, gla_merge, final_norm):
    if gla_merge:
        oa_ref, ob_ref, r_ref, ggla_ref, wout_ref, *refs = refs
        ob = ob_ref[...]
        parts = []
        for h in range(B_HEADS):
            seg = ob[:, h * B_DV:(h + 1) * B_DV]
            parts.append(seg * lax.rsqrt(jnp.mean(seg * seg, axis=-1, keepdims=True) + EPS))
        r = r_ref[...]
        obn = jnp.concatenate(parts, axis=-1) * ggla_ref[...] * (r * jax.nn.sigmoid(r))
        mix = _dot(oa_ref[...], wout_ref[:A_WIDTH, :]) + _dot(obn.astype(BF16), wout_ref[A_WIDTH:, :])
    else:
        oc_ref, wout_ref, *refs = refs
        mix = _dot(oc_ref[...], wout_ref[...])
    gffn_ref, wg_ref, wu_ref, wd_ref, *refs = refs
    x = x_ref[...] + mix
    y = _rms(x, gffn_ref[...]).astype(BF16)
    h = _dot(y, wg_ref[...])
    u = _dot(y, wu_ref[...])
    a = (h * jax.nn.sigmoid(h) * u).astype(BF16)
    x = x + _dot(a, wd_ref[...])
    if final_norm:
        gfin_ref, o_ref = refs
        o_ref[...] = _rms(x, gfin_ref[...])
    else:
        (o_ref,) = refs
        o_ref[...] = x


def _layer_tail(x, mixer_out, mixer_params, g_ffn, wg, wu, wd, g_fin, tm):
    m, d = x.shape
    row = lambda a: pl.BlockSpec((tm, a.shape[1]), lambda i: (i, 0))
    resident = [*mixer_params, g_ffn, wg, wu, wd] + ([] if g_fin is None else [g_fin])
    return pl.pallas_call(
        functools.partial(_layer_tail_kernel, gla_merge=len(mixer_out) == 3, final_norm=g_fin is not None),
        grid=(m // tm,),
        in_specs=[row(x)] + [row(a) for a in mixer_out] + [_resident(a) for a in resident],
        out_specs=row(x),
        out_shape=jax.ShapeDtypeStruct((m, d), F32),
        compiler_params=_params("parallel"),
        name="layer_tail",
    )(x, *mixer_out, *[_operand(a) for a in resident])


def _sb_core(q_heads, lo, first_kv, first_mask, earlier_kv, n_earlier, acc_ref, c_ref, o_ref, n_pairs):
    tk = first_kv.n_keys
    heads = range(2 * n_pairs)
    later = (lax.broadcasted_iota(jnp.int32, (tk, tk), 0)
             > lax.broadcasted_iota(jnp.int32, (tk, tk), 1)).astype(BF16)
    c_ref[...] = jnp.zeros_like(c_ref)

    def block(kv, mask):
        z = [kv.scores(q_heads[h], h // 2) for h in heads]
        log_beta, drop, after = [], [], []
        for h in heads:
            d = jnp.maximum(z[h], 0.0) + jnp.log(1.0 + jnp.exp(-jnp.abs(z[h])))
            log_beta.append(z[h] - d)
            drop.append(d if mask is None else jnp.where(mask, d, 0.0))
        for h in heads:
            hi, lo_part = _split_bf16(drop[h])
            after.append(_dot(hi, later) + _dot(lo_part, later))
        pv = []
        for h in heads:
            c = c_ref[h]
            w = jnp.exp(log_beta[h] - after[h] - c)
            if mask is not None:
                w = jnp.where(mask, w, 0.0)
            c_ref[h] = c + after[h][:, 0:1] + drop[h][:, 0:1]
            pv.append(kv.weighted(w.astype(BF16), kv.values(h // 2)))
        out = [jnp.where(lo, pv[2 * p], pv[2 * p + 1]) for p in range(n_pairs)]
        return out[0] if n_pairs == 1 else jnp.concatenate(out, axis=-1)

    def all_dead():
        return jnp.min(c_ref[...]) > SB_DEAD

    acc_ref[...] = block(first_kv, first_mask)

    def cond(carry):
        n, dead = carry
        return (n < n_earlier) & jnp.logical_not(dead)

    def body(carry):
        n, _ = carry
        acc_ref[...] += block(earlier_kv(n), None)
        return n + 1, all_dead()

    lax.while_loop(cond, body, (jnp.int32(0), all_dead()))
    o_ref[...] = acc_ref[...].astype(o_ref.dtype)


def _sb_prompt_kernel(q_ref, k_ref, v_ref, o_ref, acc_ref, c_ref, *, tb, n_pairs):
    qi = pl.program_id(2)
    q_heads, lo = _split_heads(q_ref, n_pairs, tb)

    def kv_block(j):
        keys = pl.ds(pl.multiple_of(j * tb, tb), tb)
        return _KV(k_ref[:, keys], v_ref[:, keys], True)

    strictly_earlier = (lax.broadcasted_iota(jnp.int32, (tb, tb), 1)
                        < lax.broadcasted_iota(jnp.int32, (tb, tb), 0))
    _sb_core(q_heads, lo, kv_block(qi), strictly_earlier, lambda n: kv_block(qi - 1 - n), qi,
             acc_ref, c_ref, o_ref, n_pairs)


def _sb_sample_kernel(q_ref, kn_ref, vn_ref, kc_ref, vc_ref, o_ref, acc_ref, c_ref, kpad_ref, vpad_ref,
                      *, ts, tk, n_cache_blocks, n_pairs):
    q_heads, lo = _split_heads(q_ref, n_pairs, ts)
    kpad_ref[...] = jnp.zeros_like(kpad_ref)
    vpad_ref[...] = jnp.zeros_like(vpad_ref)
    kpad_ref[:ts, :] = kn_ref[...]
    vpad_ref[:ts, :] = vn_ref[...]

    def cache_block(n):
        keys = pl.ds(pl.multiple_of((n_cache_blocks - 1 - n) * tk, tk), tk)
        return _KV(kc_ref[:, keys].astype(BF16), vc_ref[:, keys].astype(BF16), True)

    strictly_earlier = (lax.broadcasted_iota(jnp.int32, (ts, tk), 1)
                        < lax.broadcasted_iota(jnp.int32, (ts, tk), 0))
    _sb_core(q_heads, lo, _KV(kpad_ref[...], vpad_ref[...], False), strictly_earlier, cache_block,
             n_cache_blocks, acc_ref, c_ref, o_ref, n_pairs)


def _sb_attention_prompt(q, k, v):
    b, t, _ = q.shape
    tb = ATT_BLOCK
    assert t % tb == 0
    n_pairs = SB_PAIRS_PER_STEP
    w = n_pairs * LANES
    kv_spec = pl.BlockSpec((None, w, t), lambda bi, hp, qi: (bi, hp, 0))
    q_spec = pl.BlockSpec((None, tb, w), lambda bi, hp, qi: (bi, qi, hp))
    return pl.pallas_call(
        functools.partial(_sb_prompt_kernel, tb=tb, n_pairs=n_pairs),
        grid=(b, A_WIDTH // w, t // tb),
        in_specs=[q_spec, kv_spec, kv_spec],
        out_specs=q_spec,
        out_shape=jax.ShapeDtypeStruct(q.shape, BF16),
        scratch_shapes=[pltpu.VMEM((tb, w), F32), pltpu.VMEM((2 * n_pairs, tb, 1), F32)],
        compiler_params=_params("parallel", "parallel", "parallel"),
        name="sb_attention",
    )(q, k, v)


def _sb_attention_sample(q, k_new, v_new, k_cache, v_cache, layer):
    b, ts, _ = q.shape
    past = k_cache.shape[3]
    tk = ATT_BLOCK
    assert past % tk == 0 and ts <= tk
    n_pairs = SB_PAIRS_PER_STEP
    w = n_pairs * LANES
    new_spec = pl.BlockSpec((None, ts, w), lambda bi, hp: (bi, 0, hp))
    cache_spec = pl.BlockSpec((None, None, w, past), lambda bi, hp: (layer, bi, hp, 0))
    return pl.pallas_call(
        functools.partial(_sb_sample_kernel, ts=ts, tk=tk, n_cache_blocks=past // tk, n_pairs=n_pairs),
        grid=(b, A_WIDTH // w),
        in_specs=[new_spec, new_spec, new_spec, cache_spec, cache_spec],
        out_specs=new_spec,
        out_shape=jax.ShapeDtypeStruct(q.shape, BF16),
        scratch_shapes=[pltpu.VMEM((ts, w), F32), pltpu.VMEM((2 * n_pairs, ts, 1), F32),
                        pltpu.VMEM((tk, w), BF16), pltpu.VMEM((tk, w), BF16)],
        compiler_params=_params("parallel", "parallel"),
        name="sb_attention_sample",
    )(q, k_new, v_new, k_cache, v_cache)


def _band_core(q_heads, lo, kv, pens, bias_ref, o_ref, n_pairs):
    heads = range(2 * n_pairs)
    blocks = range(len(kv))
    z = [[kv[i].scores(q_heads[h], h // 2) + bias_ref[h, i] for i in blocks] for h in heads]
    acc = []
    for h in heads:
        m = None
        for i in blocks:
            mi = jnp.max(z[h][i], axis=-1, keepdims=True)
            if pens[i] is not None:
                mi = mi + pens[i]
            m = mi if m is None else jnp.maximum(m, mi)
        a = None
        for i in blocks:
            shift = m if pens[i] is None else m - pens[i]
            p = jnp.exp2(z[h][i] - shift).astype(BF16)
            v = kv[i].values(h // 2)
            ones = jnp.ones_like(v)
            first = kv[i].head_lanes()
            v = jnp.where(first, v, ones) if h % 2 == 0 else jnp.where(first, ones, v)
            pv = kv[i].weighted(p, v)
            a = pv if a is None else a + pv
        acc.append(a)
    for p in range(n_pairs):
        a0, a1 = acc[2 * p], acc[2 * p + 1]
        o_ref[:, _pair_cols(p)] = jnp.where(lo, a0 / pltpu.roll(a0, HEAD_DIM, axis=1),
                                            a1 / pltpu.roll(a1, HEAD_DIM, axis=1)).astype(o_ref.dtype)


def _band_prompt_kernel(q_ref, k_ref, v_ref, bias_ref, o_ref, *, tq, tk, n_pairs):
    qi = pl.program_id(2)
    q_heads, lo = _split_heads(q_ref, n_pairs, tq)
    kv, pens = [], []
    for dj in range(3):
        j = qi - dj
        pens.append(None if dj == 0 else jnp.where(j >= 0, 0.0, NEG_BIG).astype(F32))
        keys = pl.ds(pl.multiple_of(jnp.maximum(j, 0) * tk, tk), tk)
        kv.append(_KV(k_ref[:, keys], v_ref[:, keys], True))
    _band_core(q_heads, lo, kv, pens, bias_ref, o_ref, n_pairs)


def _band_sample_kernel(q_ref, kn_ref, vn_ref, kc_ref, vc_ref, bias_ref, o_ref, kpad_ref, vpad_ref,
                        *, ts, tk, n_cache_blocks, n_pairs):
    q_heads, lo = _split_heads(q_ref, n_pairs, ts)
    kpad_ref[...] = jnp.zeros_like(kpad_ref)
    vpad_ref[...] = jnp.zeros_like(vpad_ref)
    kpad_ref[:ts, :] = kn_ref[...]
    vpad_ref[:ts, :] = vn_ref[...]
    kv = [_KV(kpad_ref[...], vpad_ref[...], False)]
    for dj in range(1, n_cache_blocks + 1):
        keys = slice((n_cache_blocks - dj) * tk, (n_cache_blocks - dj + 1) * tk)
        kv.append(_KV(kc_ref[:, keys].astype(BF16), vc_ref[:, keys].astype(BF16), True))
    _band_core(q_heads, lo, kv, [None] * len(kv), bias_ref, o_ref, n_pairs)


def _band_attention_prompt(q, k, v, bias):
    b, t, _ = q.shape
    tq = tk = ATT_BLOCK
    assert t % tk == 0
    n_pairs = BAND_PAIRS_PER_STEP
    w = n_pairs * LANES
    kern = functools.partial(_band_prompt_kernel, tq=tq, tk=tk, n_pairs=n_pairs)
    kv_spec = pl.BlockSpec((None, w, t), lambda bi, hp, qi: (bi, hp, 0))
    q_spec = pl.BlockSpec((None, tq, w), lambda bi, hp, qi: (bi, qi, hp))
    bias_spec = pl.BlockSpec((2 * n_pairs, 3, tq, tk), lambda bi, hp, qi: (hp, 0, 0, 0))
    return pl.pallas_call(
        kern,
        grid=(b, C_WIDTH // w, t // tq),
        in_specs=[q_spec, kv_spec, kv_spec, bias_spec],
        out_specs=q_spec,
        out_shape=jax.ShapeDtypeStruct(q.shape, BF16),
        compiler_params=_params("parallel", "parallel", "parallel"),
        name="band_attention",
    )(q, k, v, bias)


def _band_attention_sample(q, k_new, v_new, k_cache, v_cache, layer, bias):
    b, ts, _ = q.shape
    wc = k_cache.shape[3]
    tk = ATT_BLOCK
    n_cache_blocks = min(wc // tk, 2)
    assert wc % (n_cache_blocks * tk) == 0 and ts <= tk
    n_pairs = BAND_PAIRS_PER_STEP
    w = n_pairs * LANES
    kern = functools.partial(_band_sample_kernel, ts=ts, tk=tk, n_cache_blocks=n_cache_blocks,
                             n_pairs=n_pairs)
    new_spec = pl.BlockSpec((None, ts, w), lambda bi, hp: (bi, 0, hp))
    cache_rows = n_cache_blocks * tk
    cache_spec = pl.BlockSpec((None, None, w, cache_rows),
                              lambda bi, hp: (layer, bi, hp, wc // cache_rows - 1))
    bias_spec = pl.BlockSpec((2 * n_pairs, 1 + n_cache_blocks, ts, tk), lambda bi, hp: (hp, 0, 0, 0))
    return pl.pallas_call(
        kern,
        grid=(b, C_WIDTH // w),
        in_specs=[new_spec, new_spec, new_spec, cache_spec, cache_spec, bias_spec],
        out_specs=new_spec,
        out_shape=jax.ShapeDtypeStruct(q.shape, BF16),
        scratch_shapes=[pltpu.VMEM((tk, w), BF16), pltpu.VMEM((tk, w), BF16)],
        compiler_params=_params("parallel", "parallel"),
        name="band_attention_sample",
    )(q, k_new, v_new, k_cache, v_cache, bias)


def _band_bias_kernel(g_ref, o_ref, *, rows, valid0):
    tk = ATT_BLOCK
    q_chunk = lax.broadcasted_iota(jnp.int32, (rows, tk), 0) // CHUNK
    col = lax.broadcasted_iota(jnp.int32, (rows, tk), 1)
    k_chunk = col // CHUNK
    for dj in range(3):
        g = jnp.broadcast_to(g_ref[0, dj], (rows, 2 * tk))
        tile = pltpu.roll(g, 0, axis=1, stride=1, stride_axis=0)[:, :tk]
        diff = dj * (tk // CHUNK) + q_chunk - k_chunk
        seen = (diff >= 0) & (diff <= C_LEFT_CHUNKS)
        if dj == 0:
            seen = seen & (col < valid0)
        o_ref[0, dj] = jnp.where(seen, tile * LOG2E, NEG_BIG)


def _band_bias(rel_table, rows, valid0):
    tk = ATT_BLOCK
    c = jnp.arange(2 * tk, dtype=jnp.int32)
    u = jnp.where(c <= tk, -c, 2 * tk - c)
    idx = jnp.clip(jnp.arange(3, dtype=jnp.int32)[:, None] * tk + u[None, :], REL_MIN, REL_MAX) - REL_MIN
    g = rel_table[:, idx].astype(F32).reshape(C_HEADS, 3, 1, 2 * tk)
    return pl.pallas_call(
        functools.partial(_band_bias_kernel, rows=rows, valid0=valid0),
        grid=(C_HEADS,),
        in_specs=[pl.BlockSpec((1, 3, 1, 2 * tk), lambda h: (h, 0, 0, 0))],
        out_specs=pl.BlockSpec((1, 3, rows, tk), lambda h: (h, 0, 0, 0)),
        out_shape=jax.ShapeDtypeStruct((C_HEADS, 3, rows, tk), F32),
        compiler_params=_params("parallel"),
        name="band_bias",
    )(g)


def _gla_kernel(q_ref, k_ref, v_ref, la_ref, s0_ref, o_ref, s_out_ref, qg_ref, dec_ref, st_ref,
                *, n_chunks, group):
    L = GLA_CHUNK
    row = lax.broadcasted_iota(jnp.int32, (L, L), 0)
    colm = lax.broadcasted_iota(jnp.int32, (L, L), 1)
    tri = (colm <= row).astype(BF16)
    causal = colm <= row
    lane = lax.broadcasted_iota(jnp.int32, (L, LANES), 1)
    sub = lax.broadcasted_iota(jnp.int32, (LANES, B_DV), 0)
    mine = [(lane >= h * B_DK) & (lane < (h + 1) * B_DK) for h in range(2)]
    mine_rows = [(sub >= h * B_DK) & (sub < (h + 1) * B_DK) for h in range(2)]
    v_cols = [slice(h * B_DV, (h + 1) * B_DV) for h in range(2)]

    def chunk_rows(c):
        return pl.ds(pl.multiple_of(c * L, L), L)

    def local_pass(g, carry):
        chunks = [g * group + i for i in range(group)]
        rows = [chunk_rows(c) for c in chunks]
        b = []
        for r in rows:
            g_hi, g_lo = _split_bf16(la_ref[r, :])
            b.append(_dot(tri, g_hi) + _dot(tri, g_lo))
        qg_h, kg, kd_t = [], [], []
        for i, r in enumerate(rows):
            k = k_ref[r, :]
            qg = q_ref[r, :] * (B_DK ** -0.5) * jnp.exp(b[i])
            qg_ref[r, :] = qg.astype(BF16)
            qg_h.append([jnp.where(mine[h], qg, 0.0).astype(BF16) for h in range(2)])
            kg.append((k * jnp.exp(-b[i])).astype(BF16))
            b_t = b[i].T
            b_last = b_t[:, L - 1:L]
            kd_t.append((k.T * jnp.exp(b_last - b_t)).astype(BF16))
            dec_ref[chunks[i]] = jnp.broadcast_to(jnp.exp(b_last), (LANES, B_DV))
        att = [[jnp.where(causal, _dot_nt(qg_h[i][h], kg[i]), 0.0).astype(BF16) for h in range(2)]
               for i in range(group)]
        for i, r in enumerate(rows):
            for h in range(2):
                v_h = v_ref[r, v_cols[h]].astype(BF16)
                o_ref[r, v_cols[h]] = _dot(att[i][h], v_h)
                st_ref[chunks[i], h] = jnp.where(mine_rows[h], _dot(kd_t[i], v_h), 0.0)
        return carry

    lax.fori_loop(0, n_chunks // group, local_pass, 0)

    zeros_state = jnp.zeros((B_DK, B_DV), F32)
    s_init = (jnp.concatenate([s0_ref[0], zeros_state], axis=0),
              jnp.concatenate([zeros_state, s0_ref[1]], axis=0))

    def scan_pass(c, s):
        new = []
        for h in range(2):
            own = st_ref[c, h]
            st_ref[c, h] = s[h]
            new.append(dec_ref[c] * s[h] + own)
        return tuple(new)

    s_final = lax.fori_loop(0, n_chunks, scan_pass, s_init)
    s_out_ref[0] = s_final[0][:B_DK, :]
    s_out_ref[1] = s_final[1][B_DK:, :]

    def state_pass(g, carry):
        for i in range(group):
            c = g * group + i
            r = chunk_rows(c)
            for h in range(2):
                o_ref[r, v_cols[h]] += _dot(qg_ref[r, :], st_ref[c, h].astype(BF16))
        return carry

    lax.fori_loop(0, n_chunks // group, state_pass, 0)


def _gla(q, k, v, la, s0):
    b, t, _ = q.shape
    qk_spec = pl.BlockSpec((None, t, LANES), lambda bi, p: (bi, 0, p))
    v_spec = pl.BlockSpec((None, t, 2 * B_DV), lambda bi, p: (bi, 0, p))
    s_spec = pl.BlockSpec((None, 2, B_DK, B_DV), lambda bi, p: (bi, p, 0, 0))
    n_chunks = t // GLA_CHUNK
    group = GLA_GROUP if n_chunks % GLA_GROUP == 0 else 1
    return pl.pallas_call(
        functools.partial(_gla_kernel, n_chunks=n_chunks, group=group),
        grid=(b, B_HEADS // 2),
        in_specs=[qk_spec, qk_spec, v_spec, qk_spec, s_spec],
        out_specs=[v_spec, s_spec],
        out_shape=[jax.ShapeDtypeStruct(v.shape, F32), jax.ShapeDtypeStruct(s0.shape, F32)],
        scratch_shapes=[pltpu.VMEM((t, LANES), BF16), pltpu.VMEM((n_chunks, LANES, B_DV), F32),
                        pltpu.VMEM((n_chunks, 2, LANES, B_DV), F32)],
        compiler_params=_params("parallel", "parallel"),
        name="gla",
    )(q, k, v, la, s0)


def _pad_rows(x, n):
    return jnp.pad(x, ((0, 0), (0, n - x.shape[1]), (0, 0)))


def _heads_last(x, heads):
    n, b, _, s = x.shape
    return jnp.transpose(x.reshape(n, b, heads, HEAD_DIM, s), (0, 1, 4, 2, 3))


def _feature_major(cache):
    n, b, s, heads, hd = cache.shape
    return jnp.transpose(cache, (0, 1, 3, 4, 2)).reshape(n, b, heads * hd, s)


def _row_tile(m):
    for tm in (512, 256, 128, 64, 32, 16, 8):
        if m % tm == 0:
            return tm
    raise ValueError(f"token count {m} is not a multiple of 8")


def kernel(x_prompt, x_sample, cache_a_k, cache_a_v, state_b, cache_c_k, cache_c_v, norm_mix_g, norm_ffn_g, w_in_ab, w_gate_b, b_gate_b, norm_gla_g, w_out_ab, w_qkv_c, rel_bias_c, w_out_c, w_ffn_gate, w_ffn_up, w_ffn_down, norm_final_g):
    bp, tp, d = x_prompt.shape
    bs, ts, _ = x_sample.shape
    depth = norm_mix_g.shape[0]
    past = cache_a_k.shape[2]
    wc = cache_c_k.shape[2]
    assert tp % ATT_BLOCK == 0 and past % ATT_BLOCK == 0 and wc % ATT_BLOCK == 0
    assert ts <= GLA_CHUNK and ts % 8 == 0
    mp, ms = bp * tp, bs * ts
    tmp, tms = _row_tile(tp), _row_tile(ms)
    xp = x_prompt.reshape(mp, d)
    xs = x_sample.reshape(ms, d)
    row2 = lambda a: a.reshape(1, -1)

    a_ks, a_vs, b_sp, b_ss, c_ks, c_vs = [], [], [], [], [], []
    a_kv_prompt = c_kv_prompt = None
    n_ab, n_c = (depth + 1) // 2, depth // 2
    keep = min(C_LEFT_CHUNKS * CHUNK, tp)

    wg_all, wu_all, wd_all = (w.astype(BF16) for w in (w_ffn_gate, w_ffn_up, w_ffn_down))
    kv0, kv1 = A_WIDTH, 3 * A_WIDTH
    o = 3 * A_WIDTH + 2 * B_KW + B_VW
    w_main_all = jnp.concatenate([w_in_ab[:, :, :kv0], w_in_ab[:, :, kv1:o], w_in_ab[:, :, o + B_GATE_RANK:]],
                                 axis=2).astype(BF16)
    w_kv_ab_all = w_in_ab[:, :, kv0:kv1].astype(BF16)
    w_kv_ab_t_all = jnp.swapaxes(w_in_ab, 1, 2)[:, kv0:kv1].astype(BF16)
    w_lr_all = jnp.pad(w_in_ab[:, :, o:o + B_GATE_RANK],
                       ((0, 0), (0, 0), (0, LANES - B_GATE_RANK))).astype(BF16)
    w_gate_all = jnp.pad(w_gate_b, ((0, 0), (0, LANES - B_GATE_RANK), (0, 0))).astype(BF16)
    w_out_ab_all = w_out_ab.astype(BF16)
    w_q_all = w_qkv_c[:, :, :C_WIDTH].astype(BF16)
    w_kv_c_all = w_qkv_c[:, :, C_WIDTH:].astype(BF16)
    w_kv_c_t_all = jnp.swapaxes(w_qkv_c[:, :, C_WIDTH:], 1, 2).astype(BF16)
    w_out_c_all = w_out_c.astype(BF16)
    cache_a_k_fm, cache_a_v_fm, cache_c_k_fm, cache_c_v_fm = (
        _feature_major(c) for c in (cache_a_k, cache_a_v, cache_c_k, cache_c_v))

    for layer in range(depth):
        i = layer // 2
        g_mix = row2(norm_mix_g[layer])
        ffn = (row2(norm_ffn_g[layer]), _Slab(wg_all, layer), _Slab(wu_all, layer), _Slab(wd_all, layer),
               row2(norm_final_g) if layer == depth - 1 else None)
        if layer % 2 == 0:
            w_main, w_kv, w_kv_t, w_lr, w_gate, w_out = (
                _Slab(w, i) for w in (w_main_all, w_kv_ab_all, w_kv_ab_t_all, w_lr_all, w_gate_all,
                                      w_out_ab_all))
            b_gate = row2(b_gate_b[i])
            g_gla = row2(norm_gla_g[i])

            qa, ka, va, kab, vab, qb, kb, vb, r, la = _proj_ab(
                xp, g_mix, w_main, w_kv_t, w_lr, w_gate, b_gate, tmp, batch=bp,
                stack=_KVStack(i, n_ab, tp, a_kv_prompt))
            a_kv_prompt = (ka, va)
            sh = lambda a: a.reshape(bp, tp, -1)
            oa = _sb_attention_prompt(sh(qa), kab, vab)
            s0 = jnp.zeros((bp, B_HEADS, B_DK, B_DV), F32)
            ob, sbp = _gla(sh(qb), sh(kb), sh(vb), sh(la), s0)
            xp = _layer_tail(xp, (oa.reshape(mp, -1), ob.reshape(mp, -1), r), (g_gla, w_out), *ffn, tmp)
            b_sp.append(sbp)

            qa, ka, va, kab, vab, qb, kb, vb, r, la = _proj_ab(
                xs, g_mix, w_main, w_kv, w_lr, w_gate, b_gate, tms)
            sh = lambda a: a.reshape(bs, ts, -1)
            oa = _sb_attention_sample(sh(qa), sh(kab), sh(vab), cache_a_k_fm, cache_a_v_fm, i)
            pad_t = lambda a: _pad_rows(sh(a), GLA_CHUNK)
            ob, sbs = _gla(pad_t(qb), pad_t(kb), pad_t(vb), pad_t(la), state_b[i])
            xs = _layer_tail(xs, (oa.reshape(ms, -1), ob[:, :ts].reshape(ms, -1), r), (g_gla, w_out), *ffn, tms)
            a_ks.append(ka.reshape(bs, ts, A_HEADS, HEAD_DIM))
            a_vs.append(va.reshape(bs, ts, A_HEADS, HEAD_DIM))
            b_ss.append(sbs)
        else:
            w_q, w_kv, w_kv_t, w_out = (
                _Slab(w, i) for w in (w_q_all, w_kv_c_all, w_kv_c_t_all, w_out_c_all))

            q, k, v, kb16, vb16 = _proj_c(xp, g_mix, w_q, w_kv_t, tmp, batch=bp,
                                          stack=_KVStack(i, n_c, keep, c_kv_prompt))
            c_kv_prompt = (k, v)
            bias = _band_bias(rel_bias_c[i], ATT_BLOCK, ATT_BLOCK)
            oc = _band_attention_prompt(q.reshape(bp, tp, -1), kb16, vb16, bias)
            xp = _layer_tail(xp, (oc.reshape(mp, -1),), (w_out,), *ffn, tmp)

            q, k, v, kb16, vb16 = _proj_c(xs, g_mix, w_q, w_kv, tms)
            sh = lambda a: a.reshape(bs, ts, -1)
            bias = _band_bias(rel_bias_c[i], ts, ts)
            oc = _band_attention_sample(sh(q), sh(kb16), sh(vb16), cache_c_k_fm, cache_c_v_fm, i, bias)
            xs = _layer_tail(xs, (oc.reshape(ms, -1),), (w_out,), *ffn, tms)
            c_ks.append(k.reshape(bs, ts, C_HEADS, HEAD_DIM))
            c_vs.append(v.reshape(bs, ts, C_HEADS, HEAD_DIM))

    y_prompt = xp.reshape(bp, tp, d)
    y_sample = xs.reshape(bs, ts, d)
    a_kp, a_vp = (_heads_last(a, A_HEADS) for a in a_kv_prompt)
    c_kp, c_vp = (_heads_last(a, C_HEADS) for a in c_kv_prompt)
    return (y_prompt, y_sample, a_kp, a_vp, jnp.stack(a_ks), jnp.stack(a_vs),
            jnp.stack(b_sp), jnp.stack(b_ss), c_kp, c_vp, jnp.stack(c_ks), jnp.stack(c_vs))
```

```python
import functools

import jax
import jax.numpy as jnp
from jax import lax
from jax.experimental import pallas as pl
from jax.experimental.pallas import tpu as pltpu

F32 = jnp.float32
BF16 = jnp.bfloat16

EPS = 1e-6
HEAD_DIM = 64
LANES = 128
A_HEADS = 8
A_WIDTH = A_HEADS * HEAD_DIM
B_HEADS = 4
B_DK = 64
B_DV = 128
B_KW = B_HEADS * B_DK
B_VW = B_HEADS * B_DV
B_GATE_RANK = 16
B_GATE_TEMP = 16.0
GLA_CHUNK = 64
C_HEADS = 16
C_WIDTH = C_HEADS * HEAD_DIM
CHUNK = 64
C_LEFT_CHUNKS = 8
REL_MIN = -(CHUNK - 1)
REL_MAX = 128
ATT_BLOCK = 256
NEG_BIG = -1e30
LOG2E = 1.4426950408889634
SB_DEAD = 104.0
SB_PAIRS_PER_STEP = 4
BAND_PAIRS_PER_STEP = 4
GLA_GROUP = 16
VMEM_LIMIT = 56 * 1024 * 1024


def _params(*sem):
    return pltpu.CompilerParams(dimension_semantics=sem, vmem_limit_bytes=VMEM_LIMIT)


class _Slab:
    def __init__(self, stacked, index):
        self.stacked, self.index, self.shape = stacked, index, stacked.shape[1:]


def _operand(a):
    return a.stacked if isinstance(a, _Slab) else a


def _resident(a):
    if isinstance(a, _Slab):
        index = (a.index,) + (0,) * len(a.shape)
        return pl.BlockSpec((None, *a.shape), lambda *_: index, pipeline_mode=pl.Buffered(1))
    return pl.BlockSpec(a.shape, lambda *_: (0,) * a.ndim, pipeline_mode=pl.Buffered(1))


def _rms(x, g):
    return x * lax.rsqrt(jnp.mean(x * x, axis=-1, keepdims=True) + EPS) * g


def _log_sigmoid_pair(z):
    l = jnp.log1p(jnp.exp(-jnp.abs(z)))
    return jnp.minimum(z, 0.0) - l, jnp.minimum(-z, 0.0) - l


def _split_bf16(x):
    hi = x.astype(BF16)
    lo = (x - hi.astype(F32)).astype(BF16)
    return hi, lo


def _dot(a, b):
    return jnp.dot(a, b, preferred_element_type=F32)


def _dot_nt(a, b):
    return lax.dot_general(a, b, (((1,), (1,)), ((), ())), preferred_element_type=F32)


def _dot_tn(a, b):
    return lax.dot_general(a, b, (((0,), (0,)), ((), ())), preferred_element_type=F32)


def _pair_cols(p):
    return slice(p * LANES, (p + 1) * LANES)


class _KV:
    def __init__(self, k, v, feature_major):
        self.k, self.v, self.feature_major = k, v, feature_major
        self.n_keys = k.shape[1] if feature_major else k.shape[0]

    def scores(self, q_h, p):
        if self.feature_major:
            return _dot(q_h, self.k[_pair_cols(p), :])
        return _dot_nt(q_h, self.k[:, _pair_cols(p)])

    def values(self, p):
        return self.v[_pair_cols(p), :] if self.feature_major else self.v[:, _pair_cols(p)]

    def weighted(self, w, v_p):
        return _dot_nt(w, v_p) if self.feature_major else _dot(w, v_p)

    def head_lanes(self):
        shape = (LANES, self.n_keys) if self.feature_major else (self.n_keys, LANES)
        return lax.broadcasted_iota(jnp.int32, shape, 0 if self.feature_major else 1) < HEAD_DIM


def _split_heads(q_ref, n_pairs, tq):
    lo = lax.broadcasted_iota(jnp.int32, (tq, LANES), 1) < HEAD_DIM
    heads = []
    for p in range(n_pairs):
        q = q_ref[:, _pair_cols(p)]
        heads += [jnp.where(lo, q, jnp.zeros_like(q)), jnp.where(lo, jnp.zeros_like(q), q)]
    return heads, lo


def _emit_kv(y, wkv_ref, k_ref, v_ref, kb_ref, vb_ref, feature_major, kept_tail=None):
    if feature_major:
        kv = _dot_nt(wkv_ref[...], y)
        width = kv.shape[0] // 2
        k, v = kv[:width, :], kv[width:, :]
    else:
        kv = _dot(y, wkv_ref[...])
        width = kv.shape[1] // 2
        k, v = kv[:, :width], kv[:, width:]
    kb_ref[...] = k.astype(BF16)
    vb_ref[...] = v.astype(BF16)
    if kept_tail is None:
        k_ref[...] = k
        v_ref[...] = v
    else:
        n_tiles, n_kept = kept_tail
        tile = pl.program_id(0) % n_tiles
        tm = k.shape[1]

        @pl.when(tile >= n_tiles - n_kept)
        def _():
            cols = pl.ds(pl.multiple_of((tile - (n_tiles - n_kept)) * tm, tm), tm)
            k_ref[:, cols] = k
            v_ref[:, cols] = v


def _proj_ab_kernel(x_ref, g_ref, w_ref, wkv_ref, wlr_ref, wgate_ref, bgate_ref, *refs,
                    feature_major, n_aliased):
    qa_ref, ka_ref, va_ref, kab_ref, vab_ref, qb_ref, kb_ref, vb_ref, r_ref, la_ref = refs[n_aliased:]
    y = _rms(x_ref[...], g_ref[...]).astype(BF16)
    _emit_kv(y, wkv_ref, ka_ref, va_ref, kab_ref, vab_ref, feature_major)
    z = _dot(y, w_ref[...])
    c = 0
    qa_ref[...] = (z[:, c:c + A_WIDTH] * (HEAD_DIM ** -0.5)).astype(BF16); c += A_WIDTH
    qb_ref[...] = z[:, c:c + B_KW]; c += B_KW
    kb_ref[...] = z[:, c:c + B_KW]; c += B_KW
    vb_ref[...] = z[:, c:c + B_VW]; c += B_VW
    r_ref[...] = z[:, c:c + B_VW]
    g_lr = _dot(y, wlr_ref[...])
    gate = _dot(g_lr.astype(BF16), wgate_ref[...]) + bgate_ref[...]
    la_ref[...] = _log_sigmoid_pair(gate)[0] * (1.0 / B_GATE_TEMP)


class _KVStack:
    def __init__(self, layer, n_layers, keep, previous=None):
        self.layer, self.n_layers, self.keep, self.previous = layer, n_layers, keep, previous


def _kv_out(m, width, tm, batch, stack):
    if batch is None:
        spec = pl.BlockSpec((tm, width), lambda i: (i, 0))
        return [spec] * 4, [jax.ShapeDtypeStruct((m, width), dt) for dt in (F32, F32, BF16, BF16)], None
    t = m // batch
    assert t % tm == 0 and stack.keep % tm == 0 and stack.keep <= t
    n_tiles, layer = t // tm, stack.layer
    copy_spec = pl.BlockSpec((None, width, tm), lambda i: (i // n_tiles, 0, i % n_tiles))
    copy_shape = jax.ShapeDtypeStruct((batch, width, t), BF16)
    if stack.keep == t:
        f32_spec = pl.BlockSpec((None, None, width, tm), lambda i: (layer, i // n_tiles, 0, i % n_tiles))
        kept_tail = None
    else:
        f32_spec = pl.BlockSpec((None, None, width, stack.keep), lambda i: (layer, i // n_tiles, 0, 0))
        kept_tail = (n_tiles, stack.keep // tm)
    f32_shape = jax.ShapeDtypeStruct((stack.n_layers, batch, width, stack.keep), F32)
    return [f32_spec, f32_spec, copy_spec, copy_spec], [f32_shape, f32_shape, copy_shape, copy_shape], kept_tail


def _aliased_stack(stack, n_inputs, first_output):
    if stack is None or stack.previous is None:
        return [], [], {}
    prev = list(stack.previous)
    specs = [pl.BlockSpec(memory_space=pl.ANY)] * len(prev)
    return prev, specs, {n_inputs + j: first_output + j for j in range(len(prev))}


def _proj_ab(x, g, w_main, w_kv, w_lr, w_gate, b_gate, tm, batch=None, stack=None):
    m, d = x.shape
    row = lambda n: pl.BlockSpec((tm, n), lambda i: (i, 0))
    params = (g, w_main, w_kv, w_lr, w_gate, b_gate)
    kv_specs, kv_shapes, kept_tail = _kv_out(m, A_WIDTH, tm, batch, stack)
    assert kept_tail is None
    prev, prev_specs, aliases = _aliased_stack(stack, 1 + len(params), 1)
    rest = [(B_KW, F32), (B_KW, F32), (B_VW, F32), (B_VW, F32), (B_KW, F32)]
    return pl.pallas_call(
        functools.partial(_proj_ab_kernel, feature_major=batch is not None, n_aliased=len(prev)),
        grid=(m // tm,),
        in_specs=[row(d)] + [_resident(a) for a in params] + prev_specs,
        out_specs=[row(A_WIDTH)] + kv_specs + [row(n) for n, _ in rest],
        out_shape=([jax.ShapeDtypeStruct((m, A_WIDTH), BF16)] + kv_shapes
                   + [jax.ShapeDtypeStruct((m, n), dt) for n, dt in rest]),
        input_output_aliases=aliases,
        compiler_params=_params("parallel"),
        name="proj_ab",
    )(x, *[_operand(a) for a in params], *prev)


def _proj_c_kernel(x_ref, g_ref, wq_ref, wkv_ref, *refs, feature_major, n_aliased, kept_tail):
    q_ref, k_ref, v_ref, kb_ref, vb_ref = refs[n_aliased:]
    y = _rms(x_ref[...], g_ref[...]).astype(BF16)
    _emit_kv(y, wkv_ref, k_ref, v_ref, kb_ref, vb_ref, feature_major, kept_tail)
    q_ref[...] = (_dot(y, wq_ref[...]) * (HEAD_DIM ** -0.5 * LOG2E)).astype(BF16)


def _proj_c(x, g, w_q, w_kv, tm, batch=None, stack=None):
    m, d = x.shape
    row = lambda n: pl.BlockSpec((tm, n), lambda i: (i, 0))
    params = (g, w_q, w_kv)
    kv_specs, kv_shapes, kept_tail = _kv_out(m, C_WIDTH, tm, batch, stack)
    prev, prev_specs, aliases = _aliased_stack(stack, 1 + len(params), 1)
    return pl.pallas_call(
        functools.partial(_proj_c_kernel, feature_major=batch is not None, n_aliased=len(prev),
                          kept_tail=kept_tail),
        grid=(m // tm,),
        in_specs=[row(d)] + [_resident(a) for a in params] + prev_specs,
        out_specs=[row(C_WIDTH)] + kv_specs,
        out_shape=[jax.ShapeDtypeStruct((m, C_WIDTH), BF16)] + kv_shapes,
        input_output_aliases=aliases,
        compiler_params=_params("arbitrary" if kept_tail else "parallel"),
        name="proj_c",
    )(x, *[_operand(a) for a in params], *prev)


def _layer_tail_kernel(x_ref, *refs, gla_merge, final_norm):
    if gla_merge:
        oa_ref, ob_ref, r_ref, ggla_ref, wout_ref, *refs = refs
        ob = ob_ref[...]
        parts = []
        for h in range(B_HEADS):
            seg = ob[:, h * B_DV:(h + 1) * B_DV]
            parts.append(seg * lax.rsqrt(jnp.mean(seg * seg, axis=-1, keepdims=True) + EPS))
        r = r_ref[...]
        obn = jnp.concatenate(parts, axis=-1) * ggla_ref[...] * (r * jax.nn.sigmoid(r))
        mix = _dot(oa_ref[...], wout_ref[:A_WIDTH, :]) + _dot(obn.astype(BF16), wout_ref[A_WIDTH:, :])
    else:
        oc_ref, wout_ref, *refs = refs
        mix = _dot(oc_ref[...], wout_ref[...])
    gffn_ref, wg_ref, wu_ref, wd_ref, *refs = refs
    x = x_ref[...] + mix
    y = _rms(x, gffn_ref[...]).astype(BF16)
    h = _dot(y, wg_ref[...])
    u = _dot(y, wu_ref[...])
    a = (h * jax.nn.sigmoid(h) * u).astype(BF16)
    x = x + _dot(a, wd_ref[...])
    if final_norm:
        gfin_ref, o_ref = refs
        o_ref[...] = _rms(x, gfin_ref[...])
    else:
        (o_ref,) = refs
        o_ref[...] = x


def _layer_tail(x, mixer_out, mixer_params, g_ffn, wg, wu, wd, g_fin, tm):
    m, d = x.shape
    row = lambda a: pl.BlockSpec((tm, a.shape[1]), lambda i: (i, 0))
    resident = [*mixer_params, g_ffn, wg, wu, wd] + ([] if g_fin is None else [g_fin])
    return pl.pallas_call(
        functools.partial(_layer_tail_kernel, gla_merge=len(mixer_out) == 3, final_norm=g_fin is not None),
        grid=(m // tm,),
        in_specs=[row(x)] + [row(a) for a in mixer_out] + [_resident(a) for a in resident],
        out_specs=row(x),
        out_shape=jax.ShapeDtypeStruct((m, d), F32),
        compiler_params=_params("parallel"),
        name="layer_tail",
    )(x, *mixer_out, *[_operand(a) for a in resident])


def _sb_core(q_heads, lo, first_kv, first_mask, earlier_kv, n_earlier, acc_ref, c_ref, o_ref, n_pairs):
    tk = first_kv.n_keys
    heads = range(2 * n_pairs)
    later = (lax.broadcasted_iota(jnp.int32, (tk, tk), 0)
             > lax.broadcasted_iota(jnp.int32, (tk, tk), 1)).astype(BF16)
    c_ref[...] = jnp.zeros_like(c_ref)

    def block(kv, mask):
        z = [kv.scores(q_heads[h], h // 2) for h in heads]
        log_beta, drop, after = [], [], []
        for h in heads:
            d = jnp.maximum(z[h], 0.0) + jnp.log(1.0 + jnp.exp(-jnp.abs(z[h])))
            log_beta.append(z[h] - d)
            drop.append(d if mask is None else jnp.where(mask, d, 0.0))
        for h in heads:
            hi, lo_part = _split_bf16(drop[h])
            after.append(_dot(hi, later) + _dot(lo_part, later))
        pv = []
        for h in heads:
            c = c_ref[h]
            w = jnp.exp(log_beta[h] - after[h] - c)
            if mask is not None:
                w = jnp.where(mask, w, 0.0)
            c_ref[h] = c + after[h][:, 0:1] + drop[h][:, 0:1]
            pv.append(kv.weighted(w.astype(BF16), kv.values(h // 2)))
        out = [jnp.where(lo, pv[2 * p], pv[2 * p + 1]) for p in range(n_pairs)]
        return out[0] if n_pairs == 1 else jnp.concatenate(out, axis=-1)

    def all_dead():
        return jnp.min(c_ref[...]) > SB_DEAD

    acc_ref[...] = block(first_kv, first_mask)

    def cond(carry):
        n, dead = carry
        return (n < n_earlier) & jnp.logical_not(dead)

    def body(carry):
        n, _ = carry
        acc_ref[...] += block(earlier_kv(n), None)
        return n + 1, all_dead()

    lax.while_loop(cond, body, (jnp.int32(0), all_dead()))
    o_ref[...] = acc_ref[...].astype(o_ref.dtype)


def _sb_prompt_kernel(q_ref, k_ref, v_ref, o_ref, acc_ref, c_ref, *, tb, n_pairs):
    qi = pl.program_id(2)
    q_heads, lo = _split_heads(q_ref, n_pairs, tb)

    def kv_block(j):
        keys = pl.ds(pl.multiple_of(j * tb, tb), tb)
        return _KV(k_ref[:, keys], v_ref[:, keys], True)

    strictly_earlier = (lax.broadcasted_iota(jnp.int32, (tb, tb), 1)
                        < lax.broadcasted_iota(jnp.int32, (tb, tb), 0))
    _sb_core(q_heads, lo, kv_block(qi), strictly_earlier, lambda n: kv_block(qi - 1 - n), qi,
             acc_ref, c_ref, o_ref, n_pairs)


def _sb_sample_kernel(q_ref, kn_ref, vn_ref, kc_ref, vc_ref, o_ref, acc_ref, c_ref, kpad_ref, vpad_ref,
                      *, ts, tk, n_cache_blocks, n_pairs):
    q_heads, lo = _split_heads(q_ref, n_pairs, ts)
    kpad_ref[...] = jnp.zeros_like(kpad_ref)
    vpad_ref[...] = jnp.zeros_like(vpad_ref)
    kpad_ref[:ts, :] = kn_ref[...]
    vpad_ref[:ts, :] = vn_ref[...]

    def cache_block(n):
        keys = pl.ds(pl.multiple_of((n_cache_blocks - 1 - n) * tk, tk), tk)
        return _KV(kc_ref[:, keys].astype(BF16), vc_ref[:, keys].astype(BF16), True)

    strictly_earlier = (lax.broadcasted_iota(jnp.int32, (ts, tk), 1)
                        < lax.broadcasted_iota(jnp.int32, (ts, tk), 0))
    _sb_core(q_heads, lo, _KV(kpad_ref[...], vpad_ref[...], False), strictly_earlier, cache_block,
             n_cache_blocks, acc_ref, c_ref, o_ref, n_pairs)


def _sb_attention_prompt(q, k, v):
    b, t, _ = q.shape
    tb = ATT_BLOCK
    assert t % tb == 0
    n_pairs = SB_PAIRS_PER_STEP
    w = n_pairs * LANES
    kv_spec = pl.BlockSpec((None, w, t), lambda bi, hp, qi: (bi, hp, 0))
    q_spec = pl.BlockSpec((None, tb, w), lambda bi, hp, qi: (bi, qi, hp))
    return pl.pallas_call(
        functools.partial(_sb_prompt_kernel, tb=tb, n_pairs=n_pairs),
        grid=(b, A_WIDTH // w, t // tb),
        in_specs=[q_spec, kv_spec, kv_spec],
        out_specs=q_spec,
        out_shape=jax.ShapeDtypeStruct(q.shape, BF16),
        scratch_shapes=[pltpu.VMEM((tb, w), F32), pltpu.VMEM((2 * n_pairs, tb, 1), F32)],
        compiler_params=_params("parallel", "parallel", "parallel"),
        name="sb_attention",
    )(q, k, v)


def _sb_attention_sample(q, k_new, v_new, k_cache, v_cache, layer):
    b, ts, _ = q.shape
    past = k_cache.shape[3]
    tk = ATT_BLOCK
    assert past % tk == 0 and ts <= tk
    n_pairs = SB_PAIRS_PER_STEP
    w = n_pairs * LANES
    new_spec = pl.BlockSpec((None, ts, w), lambda bi, hp: (bi, 0, hp))
    cache_spec = pl.BlockSpec((None, None, w, past), lambda bi, hp: (layer, bi, hp, 0))
    return pl.pallas_call(
        functools.partial(_sb_sample_kernel, ts=ts, tk=tk, n_cache_blocks=past // tk, n_pairs=n_pairs),
        grid=(b, A_WIDTH // w),
        in_specs=[new_spec, new_spec, new_spec, cache_spec, cache_spec],
        out_specs=new_spec,
        out_shape=jax.ShapeDtypeStruct(q.shape, BF16),
        scratch_shapes=[pltpu.VMEM((ts, w), F32), pltpu.VMEM((2 * n_pairs, ts, 1), F32),
                        pltpu.VMEM((tk, w), BF16), pltpu.VMEM((tk, w), BF16)],
        compiler_params=_params("parallel", "parallel"),
        name="sb_attention_sample",
    )(q, k_new, v_new, k_cache, v_cache)


def _band_core(q_heads, lo, kv, pens, bias, o_ref, n_pairs):
    heads = range(2 * n_pairs)
    blocks = range(len(kv))
    z = [[kv[i].scores(q_heads[h], h // 2) + bias(h, i) for i in blocks] for h in heads]
    acc = []
    for h in heads:
        m = None
        for i in blocks:
            mi = jnp.max(z[h][i], axis=-1, keepdims=True)
            if pens[i] is not None:
                mi = mi + pens[i]
            m = mi if m is None else jnp.maximum(m, mi)
        a = None
        for i in blocks:
            shift = m if pens[i] is None else m - pens[i]
            p = jnp.exp2(z[h][i] - shift).astype(BF16)
            v = kv[i].values(h // 2)
            ones = jnp.ones_like(v)
            first = kv[i].head_lanes()
            v = jnp.where(first, v, ones) if h % 2 == 0 else jnp.where(first, ones, v)
            pv = kv[i].weighted(p, v)
            a = pv if a is None else a + pv
        acc.append(a)
    for p in range(n_pairs):
        a0, a1 = acc[2 * p], acc[2 * p + 1]
        o_ref[:, _pair_cols(p)] = jnp.where(lo, a0 / pltpu.roll(a0, HEAD_DIM, axis=1),
                                            a1 / pltpu.roll(a1, HEAD_DIM, axis=1)).astype(o_ref.dtype)


def _band_prompt_kernel(q_ref, k_ref, v_ref, bias_ref, o_ref, *, tq, tk, n_pairs):
    qi = pl.program_id(2)
    q_heads, lo = _split_heads(q_ref, n_pairs, tq)
    kv, pens = [], []
    for dj in range(3):
        j = qi - dj
        pens.append(None if dj == 0 else jnp.where(j >= 0, 0.0, NEG_BIG).astype(F32))
        keys = pl.ds(pl.multiple_of(jnp.maximum(j, 0) * tk, tk), tk)
        kv.append(_KV(k_ref[:, keys], v_ref[:, keys], True))
    _band_core(q_heads, lo, kv, pens, lambda h, i: bias_ref[h, i], o_ref, n_pairs)


def _band_sample_kernel(q_ref, kn_ref, vn_ref, kc_ref, vc_ref, bias_ref, o_ref, kpad_ref, vpad_ref,
                        *, ts, tk, n_cache_blocks, n_pairs):
    q_heads, lo = _split_heads(q_ref, n_pairs, ts)
    kpad_ref[...] = jnp.zeros_like(kpad_ref)
    vpad_ref[...] = jnp.zeros_like(vpad_ref)
    kpad_ref[:ts, :] = kn_ref[...]
    vpad_ref[:ts, :] = vn_ref[...]
    kv = [_KV(kpad_ref[...], vpad_ref[...], False)]
    for dj in range(1, n_cache_blocks + 1):
        keys = slice((n_cache_blocks - dj) * tk, (n_cache_blocks - dj + 1) * tk)
        kv.append(_KV(kc_ref[:, keys].astype(BF16), vc_ref[:, keys].astype(BF16), True))
    is_new_key = lax.broadcasted_iota(jnp.int32, (ts, tk), 1) < ts

    def bias(h, i):
        return jnp.where(is_new_key, bias_ref[h, 0], NEG_BIG) if i == 0 else bias_ref[h, i]

    _band_core(q_heads, lo, kv, [None] * len(kv), bias, o_ref, n_pairs)


def _band_attention_prompt(q, k, v, bias):
    b, t, _ = q.shape
    tq = tk = ATT_BLOCK
    assert t % tk == 0
    n_pairs = BAND_PAIRS_PER_STEP
    w = n_pairs * LANES
    kern = functools.partial(_band_prompt_kernel, tq=tq, tk=tk, n_pairs=n_pairs)
    kv_spec = pl.BlockSpec((None, w, t), lambda bi, hp, qi: (bi, hp, 0))
    q_spec = pl.BlockSpec((None, tq, w), lambda bi, hp, qi: (bi, qi, hp))
    bias_spec = pl.BlockSpec((2 * n_pairs, 3, tq, tk), lambda bi, hp, qi: (hp, 0, 0, 0))
    return pl.pallas_call(
        kern,
        grid=(b, C_WIDTH // w, t // tq),
        in_specs=[q_spec, kv_spec, kv_spec, bias_spec],
        out_specs=q_spec,
        out_shape=jax.ShapeDtypeStruct(q.shape, BF16),
        compiler_params=_params("parallel", "parallel", "parallel"),
        name="band_attention",
    )(q, k, v, bias)


def _band_attention_sample(q, k_new, v_new, k_cache, v_cache, layer, bias):
    b, ts, _ = q.shape
    wc = k_cache.shape[3]
    tk = ATT_BLOCK
    n_cache_blocks = min(wc // tk, 2)
    assert wc % (n_cache_blocks * tk) == 0 and ts <= tk
    n_pairs = BAND_PAIRS_PER_STEP
    w = n_pairs * LANES
    kern = functools.partial(_band_sample_kernel, ts=ts, tk=tk, n_cache_blocks=n_cache_blocks,
                             n_pairs=n_pairs)
    new_spec = pl.BlockSpec((None, ts, w), lambda bi, hp: (bi, 0, hp))
    cache_rows = n_cache_blocks * tk
    cache_spec = pl.BlockSpec((None, None, w, cache_rows),
                              lambda bi, hp: (layer, bi, hp, wc // cache_rows - 1))
    bias_spec = pl.BlockSpec((2 * n_pairs, 1 + n_cache_blocks, ts, tk), lambda bi, hp: (hp, 0, 0, 0))
    return pl.pallas_call(
        kern,
        grid=(b, C_WIDTH // w),
        in_specs=[new_spec, new_spec, new_spec, cache_spec, cache_spec, bias_spec],
        out_specs=new_spec,
        out_shape=jax.ShapeDtypeStruct(q.shape, BF16),
        scratch_shapes=[pltpu.VMEM((tk, w), BF16), pltpu.VMEM((tk, w), BF16)],
        compiler_params=_params("parallel", "parallel"),
        name="band_attention_sample",
    )(q, k_new, v_new, k_cache, v_cache, bias)


def _band_bias_kernel(g_ref, o_ref):
    rows = tk = ATT_BLOCK
    q_chunk = lax.broadcasted_iota(jnp.int32, (rows, tk), 0) // CHUNK
    k_chunk = lax.broadcasted_iota(jnp.int32, (rows, tk), 1) // CHUNK
    for dj in range(3):
        g = jnp.broadcast_to(g_ref[0, dj], (rows, 2 * tk))
        tile = pltpu.roll(g, 0, axis=1, stride=1, stride_axis=0)[:, :tk]
        diff = dj * (tk // CHUNK) + q_chunk - k_chunk
        seen = (diff >= 0) & (diff <= C_LEFT_CHUNKS)
        o_ref[0, dj] = jnp.where(seen, tile * LOG2E, NEG_BIG)


def _band_bias(rel_table):
    rows = tk = ATT_BLOCK
    c = jnp.arange(2 * tk, dtype=jnp.int32)
    u = jnp.where(c <= tk, -c, 2 * tk - c)
    idx = jnp.clip(jnp.arange(3, dtype=jnp.int32)[:, None] * tk + u[None, :], REL_MIN, REL_MAX) - REL_MIN
    g = rel_table[:, idx].astype(F32).reshape(C_HEADS, 3, 1, 2 * tk)
    return pl.pallas_call(
        _band_bias_kernel,
        grid=(C_HEADS,),
        in_specs=[pl.BlockSpec((1, 3, 1, 2 * tk), lambda h: (h, 0, 0, 0))],
        out_specs=pl.BlockSpec((1, 3, rows, tk), lambda h: (h, 0, 0, 0)),
        out_shape=jax.ShapeDtypeStruct((C_HEADS, 3, rows, tk), F32),
        compiler_params=_params("parallel"),
        name="band_bias",
    )(g)


def _gla_kernel(q_ref, k_ref, v_ref, la_ref, s0_ref, o_ref, s_out_ref, qg_ref, dec_ref, st_ref,
                *, n_chunks, group):
    L = GLA_CHUNK
    row = lax.broadcasted_iota(jnp.int32, (L, L), 0)
    colm = lax.broadcasted_iota(jnp.int32, (L, L), 1)
    tri = (colm <= row).astype(BF16)
    causal = colm <= row
    lane = lax.broadcasted_iota(jnp.int32, (L, LANES), 1)
    sub = lax.broadcasted_iota(jnp.int32, (LANES, B_DV), 0)
    mine = [(lane >= h * B_DK) & (lane < (h + 1) * B_DK) for h in range(2)]
    mine_rows = [(sub >= h * B_DK) & (sub < (h + 1) * B_DK) for h in range(2)]
    v_cols = [slice(h * B_DV, (h + 1) * B_DV) for h in range(2)]

    def chunk_rows(c):
        return pl.ds(pl.multiple_of(c * L, L), L)

    def local_pass(g, carry):
        chunks = [g * group + i for i in range(group)]
        rows = [chunk_rows(c) for c in chunks]
        b = []
        for r in rows:
            g_hi, g_lo = _split_bf16(la_ref[r, :])
            b.append(_dot(tri, g_hi) + _dot(tri, g_lo))
        qg_h, kg, kd_t = [], [], []
        for i, r in enumerate(rows):
            k = k_ref[r, :]
            qg = q_ref[r, :] * (B_DK ** -0.5) * jnp.exp(b[i])
            qg_ref[r, :] = qg.astype(BF16)
            qg_h.append([jnp.where(mine[h], qg, 0.0).astype(BF16) for h in range(2)])
            kg.append((k * jnp.exp(-b[i])).astype(BF16))
            b_t = b[i].T
            b_last = b_t[:, L - 1:L]
            kd_t.append((k.T * jnp.exp(b_last - b_t)).astype(BF16))
            dec_ref[chunks[i]] = jnp.broadcast_to(jnp.exp(b_last), (LANES, B_DV))
        att = [[jnp.where(causal, _dot_nt(qg_h[i][h], kg[i]), 0.0).astype(BF16) for h in range(2)]
               for i in range(group)]
        for i, r in enumerate(rows):
            for h in range(2):
                v_h = v_ref[r, v_cols[h]].astype(BF16)
                o_ref[r, v_cols[h]] = _dot(att[i][h], v_h)
                st_ref[chunks[i], h] = jnp.where(mine_rows[h], _dot(kd_t[i], v_h), 0.0)
        return carry

    lax.fori_loop(0, n_chunks // group, local_pass, 0)

    zeros_state = jnp.zeros((B_DK, B_DV), F32)
    s_init = (jnp.concatenate([s0_ref[0], zeros_state], axis=0),
              jnp.concatenate([zeros_state, s0_ref[1]], axis=0))

    def scan_pass(c, s):
        new = []
        for h in range(2):
            own = st_ref[c, h]
            st_ref[c, h] = s[h]
            new.append(dec_ref[c] * s[h] + own)
        return tuple(new)

    s_final = lax.fori_loop(0, n_chunks, scan_pass, s_init)
    s_out_ref[0] = s_final[0][:B_DK, :]
    s_out_ref[1] = s_final[1][B_DK:, :]

    def state_pass(g, carry):
        for i in range(group):
            c = g * group + i
            r = chunk_rows(c)
            for h in range(2):
                o_ref[r, v_cols[h]] += _dot(qg_ref[r, :], st_ref[c, h].astype(BF16))
        return carry

    lax.fori_loop(0, n_chunks // group, state_pass, 0)


def _gla(q, k, v, la, s0):
    b, t, _ = q.shape
    qk_spec = pl.BlockSpec((None, t, LANES), lambda bi, p: (bi, 0, p))
    v_spec = pl.BlockSpec((None, t, 2 * B_DV), lambda bi, p: (bi, 0, p))
    s_spec = pl.BlockSpec((None, 2, B_DK, B_DV), lambda bi, p: (bi, p, 0, 0))
    n_chunks = t // GLA_CHUNK
    group = GLA_GROUP if n_chunks % GLA_GROUP == 0 else 1
    return pl.pallas_call(
        functools.partial(_gla_kernel, n_chunks=n_chunks, group=group),
        grid=(b, B_HEADS // 2),
        in_specs=[qk_spec, qk_spec, v_spec, qk_spec, s_spec],
        out_specs=[v_spec, s_spec],
        out_shape=[jax.ShapeDtypeStruct(v.shape, F32), jax.ShapeDtypeStruct(s0.shape, F32)],
        scratch_shapes=[pltpu.VMEM((t, LANES), BF16), pltpu.VMEM((n_chunks, LANES, B_DV), F32),
                        pltpu.VMEM((n_chunks, 2, LANES, B_DV), F32)],
        compiler_params=_params("parallel", "parallel"),
        name="gla",
    )(q, k, v, la, s0)


def _pad_rows(x, n):
    return jnp.pad(x, ((0, 0), (0, n - x.shape[1]), (0, 0)))


def _heads_last(x, heads):
    n, b, _, s = x.shape
    return jnp.transpose(x.reshape(n, b, heads, HEAD_DIM, s), (0, 1, 4, 2, 3))


def _feature_major(cache):
    n, b, s, heads, hd = cache.shape
    return jnp.transpose(cache, (0, 1, 3, 4, 2)).reshape(n, b, heads * hd, s)


def _row_tile(m):
    for tm in (512, 256, 128, 64, 32, 16, 8):
        if m % tm == 0:
            return tm
    raise ValueError(f"token count {m} is not a multiple of 8")


def kernel(x_prompt, x_sample, cache_a_k, cache_a_v, state_b, cache_c_k, cache_c_v, norm_mix_g, norm_ffn_g, w_in_ab, w_gate_b, b_gate_b, norm_gla_g, w_out_ab, w_qkv_c, rel_bias_c, w_out_c, w_ffn_gate, w_ffn_up, w_ffn_down, norm_final_g):
    bp, tp, d = x_prompt.shape
    bs, ts, _ = x_sample.shape
    depth = norm_mix_g.shape[0]
    past = cache_a_k.shape[2]
    wc = cache_c_k.shape[2]
    assert tp % ATT_BLOCK == 0 and past % ATT_BLOCK == 0 and wc % ATT_BLOCK == 0
    assert ts <= GLA_CHUNK and ts % 8 == 0
    mp, ms = bp * tp, bs * ts
    tmp, tms = _row_tile(tp), _row_tile(ms)
    xp = x_prompt.reshape(mp, d)
    xs = x_sample.reshape(ms, d)
    row2 = lambda a: a.reshape(1, -1)

    a_ks, a_vs, b_sp, b_ss, c_ks, c_vs = [], [], [], [], [], []
    a_kv_prompt = c_kv_prompt = None
    n_ab, n_c = (depth + 1) // 2, depth // 2
    keep = min(C_LEFT_CHUNKS * CHUNK, tp)

    wg_all, wu_all, wd_all = (w.astype(BF16) for w in (w_ffn_gate, w_ffn_up, w_ffn_down))
    kv0, kv1 = A_WIDTH, 3 * A_WIDTH
    o = 3 * A_WIDTH + 2 * B_KW + B_VW
    w_main_all = jnp.concatenate([w_in_ab[:, :, :kv0], w_in_ab[:, :, kv1:o], w_in_ab[:, :, o + B_GATE_RANK:]],
                                 axis=2).astype(BF16)
    w_kv_ab_all = w_in_ab[:, :, kv0:kv1].astype(BF16)
    w_kv_ab_t_all = jnp.swapaxes(w_in_ab, 1, 2)[:, kv0:kv1].astype(BF16)
    w_lr_all = jnp.pad(w_in_ab[:, :, o:o + B_GATE_RANK],
                       ((0, 0), (0, 0), (0, LANES - B_GATE_RANK))).astype(BF16)
    w_gate_all = jnp.pad(w_gate_b, ((0, 0), (0, LANES - B_GATE_RANK), (0, 0))).astype(BF16)
    w_out_ab_all = w_out_ab.astype(BF16)
    w_q_all = w_qkv_c[:, :, :C_WIDTH].astype(BF16)
    w_kv_c_all = w_qkv_c[:, :, C_WIDTH:].astype(BF16)
    w_kv_c_t_all = jnp.swapaxes(w_qkv_c[:, :, C_WIDTH:], 1, 2).astype(BF16)
    w_out_c_all = w_out_c.astype(BF16)
    cache_a_k_fm, cache_a_v_fm, cache_c_k_fm, cache_c_v_fm = (
        _feature_major(c) for c in (cache_a_k, cache_a_v, cache_c_k, cache_c_v))

    for layer in range(depth):
        i = layer // 2
        g_mix = row2(norm_mix_g[layer])
        ffn = (row2(norm_ffn_g[layer]), _Slab(wg_all, layer), _Slab(wu_all, layer), _Slab(wd_all, layer),
               row2(norm_final_g) if layer == depth - 1 else None)
        if layer % 2 == 0:
            w_main, w_kv, w_kv_t, w_lr, w_gate, w_out = (
                _Slab(w, i) for w in (w_main_all, w_kv_ab_all, w_kv_ab_t_all, w_lr_all, w_gate_all,
                                      w_out_ab_all))
            b_gate = row2(b_gate_b[i])
            g_gla = row2(norm_gla_g[i])

            qa, ka, va, kab, vab, qb, kb, vb, r, la = _proj_ab(
                xp, g_mix, w_main, w_kv_t, w_lr, w_gate, b_gate, tmp, batch=bp,
                stack=_KVStack(i, n_ab, tp, a_kv_prompt))
            a_kv_prompt = (ka, va)
            sh = lambda a: a.reshape(bp, tp, -1)
            oa = _sb_attention_prompt(sh(qa), kab, vab)
            s0 = jnp.zeros((bp, B_HEADS, B_DK, B_DV), F32)
            ob, sbp = _gla(sh(qb), sh(kb), sh(vb), sh(la), s0)
            xp = _layer_tail(xp, (oa.reshape(mp, -1), ob.reshape(mp, -1), r), (g_gla, w_out), *ffn, tmp)
            b_sp.append(sbp)

            qa, ka, va, kab, vab, qb, kb, vb, r, la = _proj_ab(
                xs, g_mix, w_main, w_kv, w_lr, w_gate, b_gate, tms)
            sh = lambda a: a.reshape(bs, ts, -1)
            oa = _sb_attention_sample(sh(qa), sh(kab), sh(vab), cache_a_k_fm, cache_a_v_fm, i)
            pad_t = lambda a: _pad_rows(sh(a), GLA_CHUNK)
            ob, sbs = _gla(pad_t(qb), pad_t(kb), pad_t(vb), pad_t(la), state_b[i])
            xs = _layer_tail(xs, (oa.reshape(ms, -1), ob[:, :ts].reshape(ms, -1), r), (g_gla, w_out), *ffn, tms)
            a_ks.append(ka.reshape(bs, ts, A_HEADS, HEAD_DIM))
            a_vs.append(va.reshape(bs, ts, A_HEADS, HEAD_DIM))
            b_ss.append(sbs)
        else:
            w_q, w_kv, w_kv_t, w_out = (
                _Slab(w, i) for w in (w_q_all, w_kv_c_all, w_kv_c_t_all, w_out_c_all))

            q, k, v, kb16, vb16 = _proj_c(xp, g_mix, w_q, w_kv_t, tmp, batch=bp,
                                          stack=_KVStack(i, n_c, keep, c_kv_prompt))
            c_kv_prompt = (k, v)
            bias = _band_bias(rel_bias_c[i])
            oc = _band_attention_prompt(q.reshape(bp, tp, -1), kb16, vb16, bias)
            xp = _layer_tail(xp, (oc.reshape(mp, -1),), (w_out,), *ffn, tmp)

            q, k, v, kb16, vb16 = _proj_c(xs, g_mix, w_q, w_kv, tms)
            sh = lambda a: a.reshape(bs, ts, -1)
            oc = _band_attention_sample(sh(q), sh(kb16), sh(vb16), cache_c_k_fm, cache_c_v_fm, i, bias)
            xs = _layer_tail(xs, (oc.reshape(ms, -1),), (w_out,), *ffn, tms)
            c_ks.append(k.reshape(bs, ts, C_HEADS, HEAD_DIM))
            c_vs.append(v.reshape(bs, ts, C_HEADS, HEAD_DIM))

    y_prompt = xp.reshape(bp, tp, d)
    y_sample = xs.reshape(bs, ts, d)
    a_kp, a_vp = (_heads_last(a, A_HEADS) for a in a_kv_prompt)
    c_kp, c_vp = (_heads_last(a, C_HEADS) for a in c_kv_prompt)
    return (y_prompt, y_sample, a_kp, a_vp, jnp.stack(a_ks), jnp.stack(a_vs),
            jnp.stack(b_sp), jnp.stack(b_ss), c_kp, c_vp, jnp.stack(c_ks), jnp.stack(c_vs))
```

```python
import functools

import jax
import jax.numpy as jnp
from jax import lax
from jax.experimental import pallas as pl
from jax.experimental.pallas import tpu as pltpu

F32 = jnp.float32
BF16 = jnp.bfloat16

EPS = 1e-6
HEAD_DIM = 64
LANES = 128
A_HEADS = 8
A_WIDTH = A_HEADS * HEAD_DIM
B_HEADS = 4
B_DK = 64
B_DV = 128
B_KW = B_HEADS * B_DK
B_VW = B_HEADS * B_DV
B_GATE_RANK = 16
B_GATE_TEMP = 16.0
GLA_CHUNK = 64
C_HEADS = 16
C_WIDTH = C_HEADS * HEAD_DIM
CHUNK = 64
C_LEFT_CHUNKS = 8
REL_MIN = -(CHUNK - 1)
REL_MAX = 128
ATT_BLOCK = 256
NEG_BIG = -1e30
LOG2E = 1.4426950408889634
SB_DEAD = 104.0
SB_PAIRS_PER_STEP = 4
BAND_PAIRS_PER_STEP = 4
GLA_MAX_TILE = 512
VMEM_LIMIT = 56 * 1024 * 1024


def _params(*sem):
    return pltpu.CompilerParams(dimension_semantics=sem, vmem_limit_bytes=VMEM_LIMIT)


class _Slab:
    def __init__(self, stacked, index):
        self.stacked, self.index, self.shape = stacked, index, stacked.shape[1:]


def _operand(a):
    return a.stacked if isinstance(a, _Slab) else a


def _resident(a):
    if isinstance(a, _Slab):
        index = (a.index,) + (0,) * len(a.shape)
        return pl.BlockSpec((None, *a.shape), lambda *_: index, pipeline_mode=pl.Buffered(1))
    return pl.BlockSpec(a.shape, lambda *_: (0,) * a.ndim, pipeline_mode=pl.Buffered(1))


def _rms(x, g):
    return x * lax.rsqrt(jnp.mean(x * x, axis=-1, keepdims=True) + EPS) * g


def _log_sigmoid_pair(z):
    l = jnp.log1p(jnp.exp(-jnp.abs(z)))
    return jnp.minimum(z, 0.0) - l, jnp.minimum(-z, 0.0) - l


def _split_bf16(x):
    hi = x.astype(BF16)
    lo = (x - hi.astype(F32)).astype(BF16)
    return hi, lo


def _dot(a, b):
    return jnp.dot(a, b, preferred_element_type=F32)


def _dot_nt(a, b):
    return lax.dot_general(a, b, (((1,), (1,)), ((), ())), preferred_element_type=F32)


def _dot_tn(a, b):
    return lax.dot_general(a, b, (((0,), (0,)), ((), ())), preferred_element_type=F32)


def _pair_cols(p):
    return slice(p * LANES, (p + 1) * LANES)


class _KV:
    def __init__(self, k, v, feature_major):
        self.k, self.v, self.feature_major = k, v, feature_major
        self.n_keys = k.shape[1] if feature_major else k.shape[0]

    def scores(self, q_h, p):
        if self.feature_major:
            return _dot(q_h, self.k[_pair_cols(p), :])
        return _dot_nt(q_h, self.k[:, _pair_cols(p)])

    def values(self, p):
        return self.v[_pair_cols(p), :] if self.feature_major else self.v[:, _pair_cols(p)]

    def weighted(self, w, v_p):
        return _dot_nt(w, v_p) if self.feature_major else _dot(w, v_p)

    def head_lanes(self):
        shape = (LANES, self.n_keys) if self.feature_major else (self.n_keys, LANES)
        return lax.broadcasted_iota(jnp.int32, shape, 0 if self.feature_major else 1) < HEAD_DIM


def _split_heads(q_ref, n_pairs, tq):
    lo = lax.broadcasted_iota(jnp.int32, (tq, LANES), 1) < HEAD_DIM
    heads = []
    for p in range(n_pairs):
        q = q_ref[:, _pair_cols(p)]
        heads += [jnp.where(lo, q, jnp.zeros_like(q)), jnp.where(lo, jnp.zeros_like(q), q)]
    return heads, lo


def _emit_kv(y, wkv_ref, k_ref, v_ref, kb_ref, vb_ref, feature_major, kept_tail=None):
    if feature_major:
        kv = _dot_nt(wkv_ref[...], y)
        width = kv.shape[0] // 2
        k, v = kv[:width, :], kv[width:, :]
    else:
        kv = _dot(y, wkv_ref[...])
        width = kv.shape[1] // 2
        k, v = kv[:, :width], kv[:, width:]
    kb_ref[...] = k.astype(BF16)
    vb_ref[...] = v.astype(BF16)
    if kept_tail is None:
        k_ref[...] = k
        v_ref[...] = v
    else:
        n_tiles, n_kept = kept_tail
        tile = pl.program_id(0) % n_tiles
        tm = k.shape[1]

        @pl.when(tile >= n_tiles - n_kept)
        def _():
            cols = pl.ds(pl.multiple_of((tile - (n_tiles - n_kept)) * tm, tm), tm)
            k_ref[:, cols] = k
            v_ref[:, cols] = v


def _proj_ab_kernel(x_ref, g_ref, w_ref, wkv_ref, wlr_ref, wgate_ref, bgate_ref, *refs,
                    feature_major, n_aliased, tiles_per_seq):
    refs = refs[n_aliased:]
    qa_ref, ka_ref, va_ref, kab_ref, vab_ref = refs[:5]
    y = _rms(x_ref[...], g_ref[...]).astype(BF16)
    g_lr = _dot(y, wlr_ref[...])
    gate = _dot(g_lr.astype(BF16), wgate_ref[...]) + bgate_ref[...]
    la = _log_sigmoid_pair(gate)[0] * (1.0 / B_GATE_TEMP)
    z = _dot(y, w_ref[...])
    c = 0
    qb = z[:, c:c + B_KW]; c += B_KW
    kb = z[:, c:c + B_KW]; c += B_KW
    vb = z[:, c:c + B_VW]; c += B_VW
    r = z[:, c:c + B_VW]; c += B_VW
    qa_ref[...] = (z[:, c:c + A_WIDTH] * (HEAD_DIM ** -0.5)).astype(BF16)
    if tiles_per_seq is None:
        qb_ref, kb_ref, vb_ref, r_ref, la_ref = refs[5:]
        qb_ref[...], kb_ref[...], vb_ref[...], r_ref[...], la_ref[...] = qb, kb, vb, r, la
        _emit_kv(y, wkv_ref, ka_ref, va_ref, kab_ref, vab_ref, feature_major)
        return
    r_ref, ob_ref, s_out_ref, carry_ref = refs[5:]
    r_ref[...] = r
    _emit_kv(y, wkv_ref, ka_ref, va_ref, kab_ref, vab_ref, feature_major)

    @pl.when(pl.program_id(0) % tiles_per_seq == 0)
    def _():
        carry_ref[...] = jnp.zeros_like(carry_ref)

    for p in range(B_HEADS // 2):
        cols, v_cols = _pair_cols(p), slice(2 * p * B_DV, 2 * (p + 1) * B_DV)
        s = _gla_pair_tile(qb[:, cols], kb[:, cols], vb[:, v_cols], la[:, cols],
                           [carry_ref[2 * p], carry_ref[2 * p + 1]], _gla_rows_emit(ob_ref, p))
        carry_ref[2 * p], carry_ref[2 * p + 1] = s
        s_out_ref[2 * p] = s[0][:B_DK, :]
        s_out_ref[2 * p + 1] = s[1][B_DK:, :]


class _KVStack:
    def __init__(self, layer, n_layers, keep, previous=None):
        self.layer, self.n_layers, self.keep, self.previous = layer, n_layers, keep, previous


def _kv_out(m, width, tm, batch, stack):
    if batch is None:
        spec = pl.BlockSpec((tm, width), lambda i: (i, 0))
        return [spec] * 4, [jax.ShapeDtypeStruct((m, width), dt) for dt in (F32, F32, BF16, BF16)], None
    t = m // batch
    assert t % tm == 0 and stack.keep % tm == 0 and stack.keep <= t
    n_tiles, layer = t // tm, stack.layer
    copy_spec = pl.BlockSpec((None, width, tm), lambda i: (i // n_tiles, 0, i % n_tiles))
    copy_shape = jax.ShapeDtypeStruct((batch, width, t), BF16)
    if stack.keep == t:
        f32_spec = pl.BlockSpec((None, None, width, tm), lambda i: (layer, i // n_tiles, 0, i % n_tiles))
        kept_tail = None
    else:
        f32_spec = pl.BlockSpec((None, None, width, stack.keep), lambda i: (layer, i // n_tiles, 0, 0))
        kept_tail = (n_tiles, stack.keep // tm)
    f32_shape = jax.ShapeDtypeStruct((stack.n_layers, batch, width, stack.keep), F32)
    return [f32_spec, f32_spec, copy_spec, copy_spec], [f32_shape, f32_shape, copy_shape, copy_shape], kept_tail


def _aliased_stack(stack, n_inputs, first_output):
    if stack is None or stack.previous is None:
        return [], [], {}
    prev = list(stack.previous)
    specs = [pl.BlockSpec(memory_space=pl.ANY)] * len(prev)
    return prev, specs, {n_inputs + j: first_output + j for j in range(len(prev))}


def _proj_ab(x, g, w_main, w_kv, w_lr, w_gate, b_gate, tm, batch=None, stack=None):
    m, d = x.shape
    row = lambda n: pl.BlockSpec((tm, n), lambda i: (i, 0))
    params = (g, w_main, w_kv, w_lr, w_gate, b_gate)
    kv_specs, kv_shapes, kept_tail = _kv_out(m, A_WIDTH, tm, batch, stack)
    assert kept_tail is None
    prev, prev_specs, aliases = _aliased_stack(stack, 1 + len(params), 1)
    if batch is None:
        rest = [(B_KW, F32), (B_KW, F32), (B_VW, F32), (B_VW, F32), (B_KW, F32)]
        rest_specs = [row(n) for n, _ in rest]
        rest_shapes = [jax.ShapeDtypeStruct((m, n), dt) for n, dt in rest]
        tiles_per_seq, scratch = None, []
    else:
        tiles_per_seq = m // batch // tm
        assert tm % GLA_CHUNK == 0 and tm <= GLA_MAX_TILE
        state = (B_HEADS, B_DK, B_DV)
        rest_specs = [row(B_VW), row(B_VW), pl.BlockSpec((None, *state), lambda i: (i // tiles_per_seq, 0, 0, 0))]
        rest_shapes = [jax.ShapeDtypeStruct((m, B_VW), F32), jax.ShapeDtypeStruct((m, B_VW), F32),
                       jax.ShapeDtypeStruct((batch, *state), F32)]
        scratch = [pltpu.VMEM((B_HEADS, LANES, B_DV), F32)]
    return pl.pallas_call(
        functools.partial(_proj_ab_kernel, feature_major=batch is not None, n_aliased=len(prev),
                          tiles_per_seq=tiles_per_seq),
        grid=(m // tm,),
        in_specs=[row(d)] + [_resident(a) for a in params] + prev_specs,
        out_specs=[row(A_WIDTH)] + kv_specs + rest_specs,
        out_shape=[jax.ShapeDtypeStruct((m, A_WIDTH), BF16)] + kv_shapes + rest_shapes,
        scratch_shapes=scratch,
        input_output_aliases=aliases,
        compiler_params=_params("parallel" if batch is None else "arbitrary"),
        name="proj_ab",
    )(x, *[_operand(a) for a in params], *prev)


def _proj_c_kernel(x_ref, g_ref, wq_ref, wkv_ref, *refs, feature_major, n_aliased, kept_tail):
    q_ref, k_ref, v_ref, kb_ref, vb_ref = refs[n_aliased:]
    y = _rms(x_ref[...], g_ref[...]).astype(BF16)
    _emit_kv(y, wkv_ref, k_ref, v_ref, kb_ref, vb_ref, feature_major, kept_tail)
    q_ref[...] = (_dot(y, wq_ref[...]) * (HEAD_DIM ** -0.5 * LOG2E)).astype(BF16)


def _proj_c(x, g, w_q, w_kv, tm, batch=None, stack=None):
    m, d = x.shape
    row = lambda n: pl.BlockSpec((tm, n), lambda i: (i, 0))
    params = (g, w_q, w_kv)
    kv_specs, kv_shapes, kept_tail = _kv_out(m, C_WIDTH, tm, batch, stack)
    prev, prev_specs, aliases = _aliased_stack(stack, 1 + len(params), 1)
    return pl.pallas_call(
        functools.partial(_proj_c_kernel, feature_major=batch is not None, n_aliased=len(prev),
                          kept_tail=kept_tail),
        grid=(m // tm,),
        in_specs=[row(d)] + [_resident(a) for a in params] + prev_specs,
        out_specs=[row(C_WIDTH)] + kv_specs,
        out_shape=[jax.ShapeDtypeStruct((m, C_WIDTH), BF16)] + kv_shapes,
        input_output_aliases=aliases,
        compiler_params=_params("arbitrary" if kept_tail else "parallel"),
        name="proj_c",
    )(x, *[_operand(a) for a in params], *prev)


def _layer_tail_kernel(x_ref, *refs, gla_merge, final_norm):
    if gla_merge:
        oa_ref, ob_ref, r_ref, ggla_ref, wout_ref, *refs = refs
        ob = ob_ref[...]
        parts = []
        for h in range(B_HEADS):
            seg = ob[:, h * B_DV:(h + 1) * B_DV]
            parts.append(seg * lax.rsqrt(jnp.mean(seg * seg, axis=-1, keepdims=True) + EPS))
        r = r_ref[...]
        obn = jnp.concatenate(parts, axis=-1) * ggla_ref[...] * (r * jax.nn.sigmoid(r))
        mix = _dot(oa_ref[...], wout_ref[:A_WIDTH, :]) + _dot(obn.astype(BF16), wout_ref[A_WIDTH:, :])
    else:
        oc_ref, wout_ref, *refs = refs
        mix = _dot(oc_ref[...], wout_ref[...])
    gffn_ref, wg_ref, wu_ref, wd_ref, *refs = refs
    x = x_ref[...] + mix
    y = _rms(x, gffn_ref[...]).astype(BF16)
    h = _dot(y, wg_ref[...])
    u = _dot(y, wu_ref[...])
    a = (h * jax.nn.sigmoid(h) * u).astype(BF16)
    x = x + _dot(a, wd_ref[...])
    if final_norm:
        gfin_ref, o_ref = refs
        o_ref[...] = _rms(x, gfin_ref[...])
    else:
        (o_ref,) = refs
        o_ref[...] = x


def _layer_tail(x, mixer_out, mixer_params, g_ffn, wg, wu, wd, g_fin, tm):
    m, d = x.shape
    row = lambda a: pl.BlockSpec((tm, a.shape[1]), lambda i: (i, 0))
    resident = [*mixer_params, g_ffn, wg, wu, wd] + ([] if g_fin is None else [g_fin])
    return pl.pallas_call(
        functools.partial(_layer_tail_kernel, gla_merge=len(mixer_out) == 3, final_norm=g_fin is not None),
        grid=(m // tm,),
        in_specs=[row(x)] + [row(a) for a in mixer_out] + [_resident(a) for a in resident],
        out_specs=row(x),
        out_shape=jax.ShapeDtypeStruct((m, d), F32),
        compiler_params=_params("parallel"),
        name="layer_tail",
    )(x, *mixer_out, *[_operand(a) for a in resident])


def _sb_core(q_heads, lo, first_kv, first_mask, earlier_kv, n_earlier, acc_ref, c_ref, o_ref, n_pairs):
    tk = first_kv.n_keys
    heads = range(2 * n_pairs)
    later = (lax.broadcasted_iota(jnp.int32, (tk, tk), 0)
             > lax.broadcasted_iota(jnp.int32, (tk, tk), 1)).astype(BF16)
    c_ref[...] = jnp.zeros_like(c_ref)

    def block(kv, mask):
        z = [kv.scores(q_heads[h], h // 2) for h in heads]
        log_beta, drop, after = [], [], []
        for h in heads:
            d = jnp.maximum(z[h], 0.0) + jnp.log(1.0 + jnp.exp(-jnp.abs(z[h])))
            log_beta.append(z[h] - d)
            drop.append(d if mask is None else jnp.where(mask, d, 0.0))
        for h in heads:
            hi, lo_part = _split_bf16(drop[h])
            after.append(_dot(hi, later) + _dot(lo_part, later))
        pv = []
        for h in heads:
            c = c_ref[h]
            w = jnp.exp(log_beta[h] - after[h] - c)
            if mask is not None:
                w = jnp.where(mask, w, 0.0)
            c_ref[h] = c + after[h][:, 0:1] + drop[h][:, 0:1]
            pv.append(kv.weighted(w.astype(BF16), kv.values(h // 2)))
        out = [jnp.where(lo, pv[2 * p], pv[2 * p + 1]) for p in range(n_pairs)]
        return out[0] if n_pairs == 1 else jnp.concatenate(out, axis=-1)

    def all_dead():
        return jnp.min(c_ref[...]) > SB_DEAD

    acc_ref[...] = block(first_kv, first_mask)

    def cond(carry):
        n, dead = carry
        return (n < n_earlier) & jnp.logical_not(dead)

    def body(carry):
        n, _ = carry
        acc_ref[...] += block(earlier_kv(n), None)
        return n + 1, all_dead()

    lax.while_loop(cond, body, (jnp.int32(0), all_dead()))
    o_ref[...] = acc_ref[...].astype(o_ref.dtype)


def _sb_prompt_kernel(q_ref, k_ref, v_ref, o_ref, acc_ref, c_ref, *, tb, n_pairs):
    qi = pl.program_id(2)
    q_heads, lo = _split_heads(q_ref, n_pairs, tb)

    def kv_block(j):
        keys = pl.ds(pl.multiple_of(j * tb, tb), tb)
        return _KV(k_ref[:, keys], v_ref[:, keys], True)

    strictly_earlier = (lax.broadcasted_iota(jnp.int32, (tb, tb), 1)
                        < lax.broadcasted_iota(jnp.int32, (tb, tb), 0))
    _sb_core(q_heads, lo, kv_block(qi), strictly_earlier, lambda n: kv_block(qi - 1 - n), qi,
             acc_ref, c_ref, o_ref, n_pairs)


def _sb_sample_kernel(q_ref, kn_ref, vn_ref, kc_ref, vc_ref, o_ref, acc_ref, c_ref, kpad_ref, vpad_ref,
                      *, ts, tk, n_cache_blocks, n_pairs):
    q_heads, lo = _split_heads(q_ref, n_pairs, ts)
    kpad_ref[...] = jnp.zeros_like(kpad_ref)
    vpad_ref[...] = jnp.zeros_like(vpad_ref)
    kpad_ref[:ts, :] = kn_ref[...]
    vpad_ref[:ts, :] = vn_ref[...]

    def cache_block(n):
        keys = pl.ds(pl.multiple_of((n_cache_blocks - 1 - n) * tk, tk), tk)
        return _KV(kc_ref[:, keys].astype(BF16), vc_ref[:, keys].astype(BF16), True)

    strictly_earlier = (lax.broadcasted_iota(jnp.int32, (ts, tk), 1)
                        < lax.broadcasted_iota(jnp.int32, (ts, tk), 0))
    _sb_core(q_heads, lo, _KV(kpad_ref[...], vpad_ref[...], False), strictly_earlier, cache_block,
             n_cache_blocks, acc_ref, c_ref, o_ref, n_pairs)


def _sb_attention_prompt(q, k, v):
    b, t, _ = q.shape
    tb = ATT_BLOCK
    assert t % tb == 0
    n_pairs = SB_PAIRS_PER_STEP
    w = n_pairs * LANES
    kv_spec = pl.BlockSpec((None, w, t), lambda bi, hp, qi: (bi, hp, 0))
    q_spec = pl.BlockSpec((None, tb, w), lambda bi, hp, qi: (bi, qi, hp))
    return pl.pallas_call(
        functools.partial(_sb_prompt_kernel, tb=tb, n_pairs=n_pairs),
        grid=(b, A_WIDTH // w, t // tb),
        in_specs=[q_spec, kv_spec, kv_spec],
        out_specs=q_spec,
        out_shape=jax.ShapeDtypeStruct(q.shape, BF16),
        scratch_shapes=[pltpu.VMEM((tb, w), F32), pltpu.VMEM((2 * n_pairs, tb, 1), F32)],
        compiler_params=_params("parallel", "parallel", "parallel"),
        name="sb_attention",
    )(q, k, v)


def _sb_attention_sample(q, k_new, v_new, k_cache, v_cache, layer):
    b, ts, _ = q.shape
    past = k_cache.shape[3]
    tk = ATT_BLOCK
    assert past % tk == 0 and ts <= tk
    n_pairs = SB_PAIRS_PER_STEP
    w = n_pairs * LANES
    new_spec = pl.BlockSpec((None, ts, w), lambda bi, hp: (bi, 0, hp))
    cache_spec = pl.BlockSpec((None, None, w, past), lambda bi, hp: (layer, bi, hp, 0))
    return pl.pallas_call(
        functools.partial(_sb_sample_kernel, ts=ts, tk=tk, n_cache_blocks=past // tk, n_pairs=n_pairs),
        grid=(b, A_WIDTH // w),
        in_specs=[new_spec, new_spec, new_spec, cache_spec, cache_spec],
        out_specs=new_spec,
        out_shape=jax.ShapeDtypeStruct(q.shape, BF16),
        scratch_shapes=[pltpu.VMEM((ts, w), F32), pltpu.VMEM((2 * n_pairs, ts, 1), F32),
                        pltpu.VMEM((tk, w), BF16), pltpu.VMEM((tk, w), BF16)],
        compiler_params=_params("parallel", "parallel"),
        name="sb_attention_sample",
    )(q, k_new, v_new, k_cache, v_cache)


def _band_core(q_heads, lo, kv, pens, bias, o_ref, n_pairs):
    heads = range(2 * n_pairs)
    blocks = range(len(kv))
    z = [[kv[i].scores(q_heads[h], h // 2) + bias(h, i) for i in blocks] for h in heads]
    acc = []
    for h in heads:
        m = None
        for i in blocks:
            mi = jnp.max(z[h][i], axis=-1, keepdims=True)
            if pens[i] is not None:
                mi = mi + pens[i]
            m = mi if m is None else jnp.maximum(m, mi)
        a = None
        for i in blocks:
            shift = m if pens[i] is None else m - pens[i]
            p = jnp.exp2(z[h][i] - shift).astype(BF16)
            v = kv[i].values(h // 2)
            ones = jnp.ones_like(v)
            first = kv[i].head_lanes()
            v = jnp.where(first, v, ones) if h % 2 == 0 else jnp.where(first, ones, v)
            pv = kv[i].weighted(p, v)
            a = pv if a is None else a + pv
        acc.append(a)
    for p in range(n_pairs):
        a0, a1 = acc[2 * p], acc[2 * p + 1]
        o_ref[:, _pair_cols(p)] = jnp.where(lo, a0 / pltpu.roll(a0, HEAD_DIM, axis=1),
                                            a1 / pltpu.roll(a1, HEAD_DIM, axis=1)).astype(o_ref.dtype)


def _band_prompt_kernel(q_ref, k_ref, v_ref, bias_ref, o_ref, *, tq, tk, n_pairs):
    qi = pl.program_id(2)
    q_heads, lo = _split_heads(q_ref, n_pairs, tq)
    kv, pens = [], []
    for dj in range(3):
        j = qi - dj
        pens.append(None if dj == 0 else jnp.where(j >= 0, 0.0, NEG_BIG).astype(F32))
        keys = pl.ds(pl.multiple_of(jnp.maximum(j, 0) * tk, tk), tk)
        kv.append(_KV(k_ref[:, keys], v_ref[:, keys], True))
    _band_core(q_heads, lo, kv, pens, lambda h, i: bias_ref[h, i], o_ref, n_pairs)


def _band_sample_kernel(q_ref, kn_ref, vn_ref, kc_ref, vc_ref, bias_ref, o_ref, kpad_ref, vpad_ref,
                        *, ts, tk, n_cache_blocks, n_pairs):
    q_heads, lo = _split_heads(q_ref, n_pairs, ts)
    kpad_ref[...] = jnp.zeros_like(kpad_ref)
    vpad_ref[...] = jnp.zeros_like(vpad_ref)
    kpad_ref[:ts, :] = kn_ref[...]
    vpad_ref[:ts, :] = vn_ref[...]
    kv = [_KV(kpad_ref[...], vpad_ref[...], False)]
    for dj in range(1, n_cache_blocks + 1):
        keys = slice((n_cache_blocks - dj) * tk, (n_cache_blocks - dj + 1) * tk)
        kv.append(_KV(kc_ref[:, keys].astype(BF16), vc_ref[:, keys].astype(BF16), True))
    is_new_key = lax.broadcasted_iota(jnp.int32, (ts, tk), 1) < ts

    def bias(h, i):
        return jnp.where(is_new_key, bias_ref[h, 0], NEG_BIG) if i == 0 else bias_ref[h, i]

    _band_core(q_heads, lo, kv, [None] * len(kv), bias, o_ref, n_pairs)


def _band_attention_prompt(q, k, v, bias):
    b, t, _ = q.shape
    tq = tk = ATT_BLOCK
    assert t % tk == 0
    n_pairs = BAND_PAIRS_PER_STEP
    w = n_pairs * LANES
    kern = functools.partial(_band_prompt_kernel, tq=tq, tk=tk, n_pairs=n_pairs)
    kv_spec = pl.BlockSpec((None, w, t), lambda bi, hp, qi: (bi, hp, 0))
    q_spec = pl.BlockSpec((None, tq, w), lambda bi, hp, qi: (bi, qi, hp))
    bias_spec = pl.BlockSpec((2 * n_pairs, 3, tq, tk), lambda bi, hp, qi: (hp, 0, 0, 0))
    return pl.pallas_call(
        kern,
        grid=(b, C_WIDTH // w, t // tq),
        in_specs=[q_spec, kv_spec, kv_spec, bias_spec],
        out_specs=q_spec,
        out_shape=jax.ShapeDtypeStruct(q.shape, BF16),
        compiler_params=_params("parallel", "parallel", "parallel"),
        name="band_attention",
    )(q, k, v, bias)


def _band_attention_sample(q, k_new, v_new, k_cache, v_cache, layer, bias):
    b, ts, _ = q.shape
    wc = k_cache.shape[3]
    tk = ATT_BLOCK
    n_cache_blocks = min(wc // tk, 2)
    assert wc % (n_cache_blocks * tk) == 0 and ts <= tk
    n_pairs = BAND_PAIRS_PER_STEP
    w = n_pairs * LANES
    kern = functools.partial(_band_sample_kernel, ts=ts, tk=tk, n_cache_blocks=n_cache_blocks,
                             n_pairs=n_pairs)
    new_spec = pl.BlockSpec((None, ts, w), lambda bi, hp: (bi, 0, hp))
    cache_rows = n_cache_blocks * tk
    cache_spec = pl.BlockSpec((None, None, w, cache_rows),
                              lambda bi, hp: (layer, bi, hp, wc // cache_rows - 1))
    bias_spec = pl.BlockSpec((2 * n_pairs, 1 + n_cache_blocks, ts, tk), lambda bi, hp: (hp, 0, 0, 0))
    return pl.pallas_call(
        kern,
        grid=(b, C_WIDTH // w),
        in_specs=[new_spec, new_spec, new_spec, cache_spec, cache_spec, bias_spec],
        out_specs=new_spec,
        out_shape=jax.ShapeDtypeStruct(q.shape, BF16),
        scratch_shapes=[pltpu.VMEM((tk, w), BF16), pltpu.VMEM((tk, w), BF16)],
        compiler_params=_params("parallel", "parallel"),
        name="band_attention_sample",
    )(q, k_new, v_new, k_cache, v_cache, bias)


def _band_bias_kernel(g_ref, o_ref):
    rows = tk = ATT_BLOCK
    q_chunk = lax.broadcasted_iota(jnp.int32, (rows, tk), 0) // CHUNK
    k_chunk = lax.broadcasted_iota(jnp.int32, (rows, tk), 1) // CHUNK
    for dj in range(3):
        g = jnp.broadcast_to(g_ref[0, dj], (rows, 2 * tk))
        tile = pltpu.roll(g, 0, axis=1, stride=1, stride_axis=0)[:, :tk]
        diff = dj * (tk // CHUNK) + q_chunk - k_chunk
        seen = (diff >= 0) & (diff <= C_LEFT_CHUNKS)
        o_ref[0, dj] = jnp.where(seen, tile * LOG2E, NEG_BIG)


def _band_bias(rel_table):
    rows = tk = ATT_BLOCK
    c = jnp.arange(2 * tk, dtype=jnp.int32)
    u = jnp.where(c <= tk, -c, 2 * tk - c)
    idx = jnp.clip(jnp.arange(3, dtype=jnp.int32)[:, None] * tk + u[None, :], REL_MIN, REL_MAX) - REL_MIN
    g = rel_table[:, idx].astype(F32).reshape(C_HEADS, 3, 1, 2 * tk)
    return pl.pallas_call(
        _band_bias_kernel,
        grid=(C_HEADS,),
        in_specs=[pl.BlockSpec((1, 3, 1, 2 * tk), lambda h: (h, 0, 0, 0))],
        out_specs=pl.BlockSpec((1, 3, rows, tk), lambda h: (h, 0, 0, 0)),
        out_shape=jax.ShapeDtypeStruct((C_HEADS, 3, rows, tk), F32),
        compiler_params=_params("parallel"),
        name="band_bias",
    )(g)


def _gla_pair_tile(q, k, v, la, states, emit):
    L = GLA_CHUNK
    chunks = range(q.shape[0] // L)
    row = lax.broadcasted_iota(jnp.int32, (L, L), 0)
    colm = lax.broadcasted_iota(jnp.int32, (L, L), 1)
    tri = (colm <= row).astype(BF16)
    causal = colm <= row
    lane = lax.broadcasted_iota(jnp.int32, (L, LANES), 1)
    sub = lax.broadcasted_iota(jnp.int32, (LANES, B_DV), 0)
    mine = [(lane >= h * B_DK) & (lane < (h + 1) * B_DK) for h in range(2)]
    mine_rows = [(sub >= h * B_DK) & (sub < (h + 1) * B_DK) for h in range(2)]
    rows = [slice(c * L, (c + 1) * L) for c in chunks]

    b = []
    for r in rows:
        g_hi, g_lo = _split_bf16(la[r, :])
        b.append(_dot(tri, g_hi) + _dot(tri, g_lo))
    qg, qg_h, kg, kd_t, decay = [], [], [], [], []
    for c, r in zip(chunks, rows):
        qg_c = q[r, :] * (B_DK ** -0.5) * jnp.exp(b[c])
        qg.append(qg_c.astype(BF16))
        qg_h.append([jnp.where(mine[h], qg_c, 0.0).astype(BF16) for h in range(2)])
        kg.append((k[r, :] * jnp.exp(-b[c])).astype(BF16))
        b_t = b[c].T
        b_last = b_t[:, L - 1:L]
        kd_t.append((k[r, :].T * jnp.exp(b_last - b_t)).astype(BF16))
        decay.append(jnp.exp(b_last))
    att = [[jnp.where(causal, _dot_nt(qg_h[c][h], kg[c]), 0.0).astype(BF16) for h in range(2)]
           for c in chunks]
    o_intra, own = [], []
    for c, r in zip(chunks, rows):
        v_h = [v[r, h * B_DV:(h + 1) * B_DV].astype(BF16) for h in range(2)]
        o_intra.append([_dot(att[c][h], v_h[h]) for h in range(2)])
        own.append([jnp.where(mine_rows[h], _dot(kd_t[c], v_h[h]), 0.0) for h in range(2)])

    states = list(states)
    start = []
    for c in chunks:
        start.append([s.astype(BF16) for s in states])
        states = [decay[c] * states[h] + own[c][h] for h in range(2)]

    for c in chunks:
        for h in range(2):
            emit(c, h, o_intra[c][h] + _dot(qg[c], start[c][h]))
    return states


def _pair_states(s0_ref, p):
    zeros_state = jnp.zeros((B_DK, B_DV), F32)
    return [jnp.concatenate([s0_ref[2 * p], zeros_state], axis=0),
            jnp.concatenate([zeros_state, s0_ref[2 * p + 1]], axis=0)]


def _gla_rows_emit(o_ref, p):
    def emit(c, h, o):
        o_ref[c * GLA_CHUNK:(c + 1) * GLA_CHUNK, (2 * p + h) * B_DV:(2 * p + h + 1) * B_DV] = o
    return emit


def _gla_kernel(q_ref, k_ref, v_ref, la_ref, s0_ref, o_ref, s_out_ref):
    for p in range(B_HEADS // 2):
        cols, v_cols = _pair_cols(p), slice(2 * p * B_DV, 2 * (p + 1) * B_DV)
        s = _gla_pair_tile(q_ref[:, cols], k_ref[:, cols], v_ref[:, v_cols], la_ref[:, cols],
                           _pair_states(s0_ref, p), _gla_rows_emit(o_ref, p))
        s_out_ref[2 * p] = s[0][:B_DK, :]
        s_out_ref[2 * p + 1] = s[1][B_DK:, :]


def _gla(q, k, v, la, s0):
    b, t, _ = q.shape
    assert t % GLA_CHUNK == 0 and t <= GLA_MAX_TILE
    qk_spec = pl.BlockSpec((None, t, B_KW), lambda bi: (bi, 0, 0))
    v_spec = pl.BlockSpec((None, t, B_VW), lambda bi: (bi, 0, 0))
    s_spec = pl.BlockSpec((None, B_HEADS, B_DK, B_DV), lambda bi: (bi, 0, 0, 0))
    return pl.pallas_call(
        _gla_kernel,
        grid=(b,),
        in_specs=[qk_spec, qk_spec, v_spec, qk_spec, s_spec],
        out_specs=[v_spec, s_spec],
        out_shape=[jax.ShapeDtypeStruct(v.shape, F32), jax.ShapeDtypeStruct(s0.shape, F32)],
        compiler_params=_params("parallel"),
        name="gla",
    )(q, k, v, la, s0)


def _pad_rows(x, n):
    return jnp.pad(x, ((0, 0), (0, n - x.shape[1]), (0, 0)))


def _heads_last(x, heads):
    n, b, _, s = x.shape
    return jnp.transpose(x.reshape(n, b, heads, HEAD_DIM, s), (0, 1, 4, 2, 3))


def _feature_major(cache):
    n, b, s, heads, hd = cache.shape
    return jnp.transpose(cache, (0, 1, 3, 4, 2)).reshape(n, b, heads * hd, s)


def _row_tile(m):
    for tm in (512, 256, 128, 64, 32, 16, 8):
        if m % tm == 0:
            return tm
    raise ValueError(f"token count {m} is not a multiple of 8")


def kernel(x_prompt, x_sample, cache_a_k, cache_a_v, state_b, cache_c_k, cache_c_v, norm_mix_g, norm_ffn_g, w_in_ab, w_gate_b, b_gate_b, norm_gla_g, w_out_ab, w_qkv_c, rel_bias_c, w_out_c, w_ffn_gate, w_ffn_up, w_ffn_down, norm_final_g):
    bp, tp, d = x_prompt.shape
    bs, ts, _ = x_sample.shape
    depth = norm_mix_g.shape[0]
    past = cache_a_k.shape[2]
    wc = cache_c_k.shape[2]
    assert tp % ATT_BLOCK == 0 and past % ATT_BLOCK == 0 and wc % ATT_BLOCK == 0
    assert ts <= GLA_CHUNK and ts % 8 == 0
    mp, ms = bp * tp, bs * ts
    tmp, tms = _row_tile(tp), _row_tile(ms)
    xp = x_prompt.reshape(mp, d)
    xs = x_sample.reshape(ms, d)
    row2 = lambda a: a.reshape(1, -1)

    a_ks, a_vs, b_sp, b_ss, c_ks, c_vs = [], [], [], [], [], []
    a_kv_prompt = c_kv_prompt = None
    n_ab, n_c = (depth + 1) // 2, depth // 2
    keep = min(C_LEFT_CHUNKS * CHUNK, tp)

    wg_all, wu_all, wd_all = (w.astype(BF16) for w in (w_ffn_gate, w_ffn_up, w_ffn_down))
    kv0, kv1 = A_WIDTH, 3 * A_WIDTH
    o = 3 * A_WIDTH + 2 * B_KW + B_VW
    w_main_all = jnp.concatenate([w_in_ab[:, :, kv1:o], w_in_ab[:, :, o + B_GATE_RANK:], w_in_ab[:, :, :kv0]],
                                 axis=2).astype(BF16)
    w_kv_ab_all = w_in_ab[:, :, kv0:kv1].astype(BF16)
    w_kv_ab_t_all = jnp.swapaxes(w_in_ab, 1, 2)[:, kv0:kv1].astype(BF16)
    w_lr_all = jnp.pad(w_in_ab[:, :, o:o + B_GATE_RANK],
                       ((0, 0), (0, 0), (0, LANES - B_GATE_RANK))).astype(BF16)
    w_gate_all = jnp.pad(w_gate_b, ((0, 0), (0, LANES - B_GATE_RANK), (0, 0))).astype(BF16)
    w_out_ab_all = w_out_ab.astype(BF16)
    w_q_all = w_qkv_c[:, :, :C_WIDTH].astype(BF16)
    w_kv_c_all = w_qkv_c[:, :, C_WIDTH:].astype(BF16)
    w_kv_c_t_all = jnp.swapaxes(w_qkv_c[:, :, C_WIDTH:], 1, 2).astype(BF16)
    w_out_c_all = w_out_c.astype(BF16)
    cache_a_k_fm, cache_a_v_fm, cache_c_k_fm, cache_c_v_fm = (
        _feature_major(c) for c in (cache_a_k, cache_a_v, cache_c_k, cache_c_v))

    for layer in range(depth):
        i = layer // 2
        g_mix = row2(norm_mix_g[layer])
        ffn = (row2(norm_ffn_g[layer]), _Slab(wg_all, layer), _Slab(wu_all, layer), _Slab(wd_all, layer),
               row2(norm_final_g) if layer == depth - 1 else None)
        if layer % 2 == 0:
            w_main, w_kv, w_kv_t, w_lr, w_gate, w_out = (
                _Slab(w, i) for w in (w_main_all, w_kv_ab_all, w_kv_ab_t_all, w_lr_all, w_gate_all,
                                      w_out_ab_all))
            b_gate = row2(b_gate_b[i])
            g_gla = row2(norm_gla_g[i])

            qa, ka, va, kab, vab, r, ob, sbp = _proj_ab(
                xp, g_mix, w_main, w_kv_t, w_lr, w_gate, b_gate, tmp, batch=bp,
                stack=_KVStack(i, n_ab, tp, a_kv_prompt))
            a_kv_prompt = (ka, va)
            oa = _sb_attention_prompt(qa.reshape(bp, tp, -1), kab, vab)
            xp = _layer_tail(xp, (oa.reshape(mp, -1), ob, r), (g_gla, w_out), *ffn, tmp)
            b_sp.append(sbp)

            qa, ka, va, kab, vab, qb, kb, vb, r, la = _proj_ab(
                xs, g_mix, w_main, w_kv, w_lr, w_gate, b_gate, tms)
            sh = lambda a: a.reshape(bs, ts, -1)
            oa = _sb_attention_sample(sh(qa), sh(kab), sh(vab), cache_a_k_fm, cache_a_v_fm, i)
            pad_t = lambda a: _pad_rows(sh(a), GLA_CHUNK)
            ob, sbs = _gla(pad_t(qb), pad_t(kb), pad_t(vb), pad_t(la), state_b[i])
            xs = _layer_tail(xs, (oa.reshape(ms, -1), ob[:, :ts].reshape(ms, -1), r), (g_gla, w_out), *ffn, tms)
            a_ks.append(ka.reshape(bs, ts, A_HEADS, HEAD_DIM))
            a_vs.append(va.reshape(bs, ts, A_HEADS, HEAD_DIM))
            b_ss.append(sbs)
        else:
            w_q, w_kv, w_kv_t, w_out = (
                _Slab(w, i) for w in (w_q_all, w_kv_c_all, w_kv_c_t_all, w_out_c_all))

            q, k, v, kb16, vb16 = _proj_c(xp, g_mix, w_q, w_kv_t, tmp, batch=bp,
                                          stack=_KVStack(i, n_c, keep, c_kv_prompt))
            c_kv_prompt = (k, v)
            bias = _band_bias(rel_bias_c[i])
            oc = _band_attention_prompt(q.reshape(bp, tp, -1), kb16, vb16, bias)
            xp = _layer_tail(xp, (oc.reshape(mp, -1),), (w_out,), *ffn, tmp)

            q, k, v, kb16, vb16 = _proj_c(xs, g_mix, w_q, w_kv, tms)
            sh = lambda a: a.reshape(bs, ts, -1)
            oc = _band_attention_sample(sh(q), sh(kb16), sh(vb16), cache_c_k_fm, cache_c_v_fm, i, bias)
            xs = _layer_tail(xs, (oc.reshape(ms, -1),), (w_out,), *ffn, tms)
            c_ks.append(k.reshape(bs, ts, C_HEADS, HEAD_DIM))
            c_vs.append(v.reshape(bs, ts, C_HEADS, HEAD_DIM))

    y_prompt = xp.reshape(bp, tp, d)
    y_sample = xs.reshape(bs, ts, d)
    a_kp, a_vp = (_heads_last(a, A_HEADS) for a in a_kv_prompt)
    c_kp, c_vp = (_heads_last(a, C_HEADS) for a in c_kv_prompt)
    return (y_prompt, y_sample, a_kp, a_vp, jnp.stack(a_ks), jnp.stack(a_vs),
            jnp.stack(b_sp), jnp.stack(b_ss), c_kp, c_vp, jnp.stack(c_ks), jnp.stack(c_vs))
```

```python
import functools

import jax
import jax.numpy as jnp
from jax import lax
from jax.experimental import pallas as pl
from jax.experimental.pallas import tpu as pltpu

F32 = jnp.float32
BF16 = jnp.bfloat16

EPS = 1e-6
HEAD_DIM = 64
LANES = 128
A_HEADS = 8
A_WIDTH = A_HEADS * HEAD_DIM
B_HEADS = 4
B_DK = 64
B_DV = 128
B_KW = B_HEADS * B_DK
B_VW = B_HEADS * B_DV
B_GATE_RANK = 16
B_GATE_TEMP = 16.0
GLA_CHUNK = 64
C_HEADS = 16
C_WIDTH = C_HEADS * HEAD_DIM
CHUNK = 64
C_LEFT_CHUNKS = 8
REL_MIN = -(CHUNK - 1)
REL_MAX = 128
ATT_BLOCK = 256
NEG_BIG = -1e30
LOG2E = 1.4426950408889634
SB_DEAD = 104.0
SB_PAIRS_PER_STEP = 4
BAND_PAIRS_PER_STEP = 4
GLA_MAX_TILE = 512
VMEM_LIMIT = 56 * 1024 * 1024


def _params(*sem):
    return pltpu.CompilerParams(dimension_semantics=sem, vmem_limit_bytes=VMEM_LIMIT)


class _Slab:
    def __init__(self, stacked, index):
        self.stacked, self.index, self.shape = stacked, index, stacked.shape[1:]


def _operand(a):
    return a.stacked if isinstance(a, _Slab) else a


def _resident(a):
    if isinstance(a, _Slab):
        index = (a.index,) + (0,) * len(a.shape)
        return pl.BlockSpec((None, *a.shape), lambda *_: index, pipeline_mode=pl.Buffered(1))
    return pl.BlockSpec(a.shape, lambda *_: (0,) * a.ndim, pipeline_mode=pl.Buffered(1))


def _rms(x, g):
    return x * lax.rsqrt(jnp.mean(x * x, axis=-1, keepdims=True) + EPS) * g


def _log_sigmoid_pair(z):
    l = jnp.log1p(jnp.exp(-jnp.abs(z)))
    return jnp.minimum(z, 0.0) - l, jnp.minimum(-z, 0.0) - l


def _split_bf16(x):
    hi = x.astype(BF16)
    lo = (x - hi.astype(F32)).astype(BF16)
    return hi, lo


def _dot(a, b):
    return jnp.dot(a, b, preferred_element_type=F32)


def _dot_nt(a, b):
    return lax.dot_general(a, b, (((1,), (1,)), ((), ())), preferred_element_type=F32)


def _dot_tn(a, b):
    return lax.dot_general(a, b, (((0,), (0,)), ((), ())), preferred_element_type=F32)


def _pair_cols(p):
    return slice(p * LANES, (p + 1) * LANES)


class _KV:
    def __init__(self, k, v, feature_major):
        self.k, self.v, self.feature_major = k, v, feature_major
        self.n_keys = k.shape[1] if feature_major else k.shape[0]

    def scores(self, q_h, p):
        if self.feature_major:
            return _dot(q_h, self.k[_pair_cols(p), :])
        return _dot_nt(q_h, self.k[:, _pair_cols(p)])

    def values(self, p):
        return self.v[_pair_cols(p), :] if self.feature_major else self.v[:, _pair_cols(p)]

    def weighted(self, w, v_p):
        return _dot_nt(w, v_p) if self.feature_major else _dot(w, v_p)

    def head_lanes(self):
        shape = (LANES, self.n_keys) if self.feature_major else (self.n_keys, LANES)
        return lax.broadcasted_iota(jnp.int32, shape, 0 if self.feature_major else 1) < HEAD_DIM


def _split_heads(q_ref, n_pairs, tq):
    lo = lax.broadcasted_iota(jnp.int32, (tq, LANES), 1) < HEAD_DIM
    heads = []
    for p in range(n_pairs):
        q = q_ref[:, _pair_cols(p)]
        heads += [jnp.where(lo, q, jnp.zeros_like(q)), jnp.where(lo, jnp.zeros_like(q), q)]
    return heads, lo


def _emit_kv(y, wkv_ref, k_ref, v_ref, kb_ref, vb_ref, feature_major, kept_tail=None):
    if feature_major:
        kv = _dot_nt(wkv_ref[...], y)
        width = kv.shape[0] // 2
        k, v = kv[:width, :], kv[width:, :]
    else:
        kv = _dot(y, wkv_ref[...])
        width = kv.shape[1] // 2
        k, v = kv[:, :width], kv[:, width:]
    kb_ref[...] = k.astype(BF16)
    vb_ref[...] = v.astype(BF16)
    if kept_tail is None:
        k_ref[...] = k
        v_ref[...] = v
    else:
        n_tiles, n_kept = kept_tail
        tile = pl.program_id(0) % n_tiles
        tm = k.shape[1]

        @pl.when(tile >= n_tiles - n_kept)
        def _():
            cols = pl.ds(pl.multiple_of((tile - (n_tiles - n_kept)) * tm, tm), tm)
            k_ref[:, cols] = k
            v_ref[:, cols] = v


def _proj_ab_kernel(x_ref, g_ref, w_ref, wkv_ref, wlr_ref, wgate_ref, bgate_ref, *refs,
                    feature_major, n_aliased):
    qa_ref, ka_ref, va_ref, kab_ref, vab_ref, qb_ref, kb_ref, vb_ref, r_ref, la_ref = refs[n_aliased:]
    y = _rms(x_ref[...], g_ref[...]).astype(BF16)
    _emit_kv(y, wkv_ref, ka_ref, va_ref, kab_ref, vab_ref, feature_major)
    z = _dot(y, w_ref[...])
    c = 0
    qa_ref[...] = (z[:, c:c + A_WIDTH] * (HEAD_DIM ** -0.5)).astype(BF16); c += A_WIDTH
    qb_ref[...] = z[:, c:c + B_KW]; c += B_KW
    kb_ref[...] = z[:, c:c + B_KW]; c += B_KW
    vb_ref[...] = z[:, c:c + B_VW]; c += B_VW
    r_ref[...] = z[:, c:c + B_VW]
    g_lr = _dot(y, wlr_ref[...])
    gate = _dot(g_lr.astype(BF16), wgate_ref[...]) + bgate_ref[...]
    la_ref[...] = _log_sigmoid_pair(gate)[0] * (1.0 / B_GATE_TEMP)


class _KVStack:
    def __init__(self, layer, n_layers, keep, previous=None):
        self.layer, self.n_layers, self.keep, self.previous = layer, n_layers, keep, previous


def _kv_out(m, width, tm, batch, stack):
    if batch is None:
        spec = pl.BlockSpec((tm, width), lambda i: (i, 0))
        return [spec] * 4, [jax.ShapeDtypeStruct((m, width), dt) for dt in (F32, F32, BF16, BF16)], None
    t = m // batch
    assert t % tm == 0 and stack.keep % tm == 0 and stack.keep <= t
    n_tiles, layer = t // tm, stack.layer
    copy_spec = pl.BlockSpec((None, width, tm), lambda i: (i // n_tiles, 0, i % n_tiles))
    copy_shape = jax.ShapeDtypeStruct((batch, width, t), BF16)
    if stack.keep == t:
        f32_spec = pl.BlockSpec((None, None, width, tm), lambda i: (layer, i // n_tiles, 0, i % n_tiles))
        kept_tail = None
    else:
        f32_spec = pl.BlockSpec((None, None, width, stack.keep), lambda i: (layer, i // n_tiles, 0, 0))
        kept_tail = (n_tiles, stack.keep // tm)
    f32_shape = jax.ShapeDtypeStruct((stack.n_layers, batch, width, stack.keep), F32)
    return [f32_spec, f32_spec, copy_spec, copy_spec], [f32_shape, f32_shape, copy_shape, copy_shape], kept_tail


def _aliased_stack(stack, n_inputs, first_output):
    if stack is None or stack.previous is None:
        return [], [], {}
    prev = list(stack.previous)
    specs = [pl.BlockSpec(memory_space=pl.ANY)] * len(prev)
    return prev, specs, {n_inputs + j: first_output + j for j in range(len(prev))}


def _proj_ab(x, g, w_main, w_kv, w_lr, w_gate, b_gate, tm, batch=None, stack=None):
    m, d = x.shape
    row = lambda n: pl.BlockSpec((tm, n), lambda i: (i, 0))
    params = (g, w_main, w_kv, w_lr, w_gate, b_gate)
    kv_specs, kv_shapes, kept_tail = _kv_out(m, A_WIDTH, tm, batch, stack)
    assert kept_tail is None
    prev, prev_specs, aliases = _aliased_stack(stack, 1 + len(params), 1)
    rest = [(B_KW, F32), (B_KW, F32), (B_VW, F32), (B_VW, F32), (B_KW, F32)]
    return pl.pallas_call(
        functools.partial(_proj_ab_kernel, feature_major=batch is not None, n_aliased=len(prev)),
        grid=(m // tm,),
        in_specs=[row(d)] + [_resident(a) for a in params] + prev_specs,
        out_specs=[row(A_WIDTH)] + kv_specs + [row(n) for n, _ in rest],
        out_shape=([jax.ShapeDtypeStruct((m, A_WIDTH), BF16)] + kv_shapes
                   + [jax.ShapeDtypeStruct((m, n), dt) for n, dt in rest]),
        input_output_aliases=aliases,
        compiler_params=_params("parallel"),
        name="proj_ab",
    )(x, *[_operand(a) for a in params], *prev)


def _proj_c_kernel(x_ref, g_ref, wq_ref, wkv_ref, *refs, feature_major, n_aliased, kept_tail):
    q_ref, k_ref, v_ref, kb_ref, vb_ref = refs[n_aliased:]
    y = _rms(x_ref[...], g_ref[...]).astype(BF16)
    _emit_kv(y, wkv_ref, k_ref, v_ref, kb_ref, vb_ref, feature_major, kept_tail)
    q_ref[...] = (_dot(y, wq_ref[...]) * (HEAD_DIM ** -0.5 * LOG2E)).astype(BF16)


def _proj_c(x, g, w_q, w_kv, tm, batch=None, stack=None):
    m, d = x.shape
    row = lambda n: pl.BlockSpec((tm, n), lambda i: (i, 0))
    params = (g, w_q, w_kv)
    kv_specs, kv_shapes, kept_tail = _kv_out(m, C_WIDTH, tm, batch, stack)
    prev, prev_specs, aliases = _aliased_stack(stack, 1 + len(params), 1)
    return pl.pallas_call(
        functools.partial(_proj_c_kernel, feature_major=batch is not None, n_aliased=len(prev),
                          kept_tail=kept_tail),
        grid=(m // tm,),
        in_specs=[row(d)] + [_resident(a) for a in params] + prev_specs,
        out_specs=[row(C_WIDTH)] + kv_specs,
        out_shape=[jax.ShapeDtypeStruct((m, C_WIDTH), BF16)] + kv_shapes,
        input_output_aliases=aliases,
        compiler_params=_params("arbitrary" if kept_tail else "parallel"),
        name="proj_c",
    )(x, *[_operand(a) for a in params], *prev)


def _layer_tail_kernel(x_ref, *refs, gla_merge, final_norm):
    if gla_merge:
        oa_ref, ob_ref, r_ref, ggla_ref, wout_ref, *refs = refs
        ob = ob_ref[...]
        parts = []
        for h in range(B_HEADS):
            seg = ob[:, h * B_DV:(h + 1) * B_DV]
            parts.append(seg * lax.rsqrt(jnp.mean(seg * seg, axis=-1, keepdims=True) + EPS))
        r = r_ref[...]
        obn = jnp.concatenate(parts, axis=-1) * ggla_ref[...] * (r * jax.nn.sigmoid(r))
        mix = _dot(oa_ref[...], wout_ref[:A_WIDTH, :]) + _dot(obn.astype(BF16), wout_ref[A_WIDTH:, :])
    else:
        oc_ref, wout_ref, *refs = refs
        mix = _dot(oc_ref[...], wout_ref[...])
    gffn_ref, wg_ref, wu_ref, wd_ref, *refs = refs
    x = x_ref[...] + mix
    y = _rms(x, gffn_ref[...]).astype(BF16)
    h = _dot(y, wg_ref[...])
    u = _dot(y, wu_ref[...])
    a = (h * jax.nn.sigmoid(h) * u).astype(BF16)
    x = x + _dot(a, wd_ref[...])
    if final_norm:
        gfin_ref, o_ref = refs
        o_ref[...] = _rms(x, gfin_ref[...])
    else:
        (o_ref,) = refs
        o_ref[...] = x


def _layer_tail(x, mixer_out, mixer_params, g_ffn, wg, wu, wd, g_fin, tm):
    m, d = x.shape
    row = lambda a: pl.BlockSpec((tm, a.shape[1]), lambda i: (i, 0))
    resident = [*mixer_params, g_ffn, wg, wu, wd] + ([] if g_fin is None else [g_fin])
    return pl.pallas_call(
        functools.partial(_layer_tail_kernel, gla_merge=len(mixer_out) == 3, final_norm=g_fin is not None),
        grid=(m // tm,),
        in_specs=[row(x)] + [row(a) for a in mixer_out] + [_resident(a) for a in resident],
        out_specs=row(x),
        out_shape=jax.ShapeDtypeStruct((m, d), F32),
        compiler_params=_params("parallel"),
        name="layer_tail",
    )(x, *mixer_out, *[_operand(a) for a in resident])


def _interleave(stage_generators):
    results = [None] * len(stage_generators)
    live = list(range(len(stage_generators)))
    while live:
        for i in list(live):
            try:
                next(stage_generators[i])
            except StopIteration as done:
                results[i] = done.value
                live.remove(i)
    return results


def _sb_core(q_heads, lo, first_kv, first_mask, earlier_kv, n_earlier, acc_ref, c_ref, o_ref, n_pairs,
             companions=(), on_companions=None):
    tk = first_kv.n_keys
    heads = range(2 * n_pairs)
    later = (lax.broadcasted_iota(jnp.int32, (tk, tk), 0)
             > lax.broadcasted_iota(jnp.int32, (tk, tk), 1)).astype(BF16)
    c_ref[...] = jnp.zeros_like(c_ref)

    def block(kv, mask):
        z = [kv.scores(q_heads[h], h // 2) for h in heads]
        yield
        log_beta, drop, after = [], [], []
        for h in heads:
            d = jnp.maximum(z[h], 0.0) + jnp.log(1.0 + jnp.exp(-jnp.abs(z[h])))
            log_beta.append(z[h] - d)
            drop.append(d if mask is None else jnp.where(mask, d, 0.0))
        yield
        for h in heads:
            hi, lo_part = _split_bf16(drop[h])
            after.append(_dot(hi, later) + _dot(lo_part, later))
        yield
        pv = []
        for h in heads:
            c = c_ref[h]
            w = jnp.exp(log_beta[h] - after[h] - c)
            if mask is not None:
                w = jnp.where(mask, w, 0.0)
            c_ref[h] = c + after[h][:, 0:1] + drop[h][:, 0:1]
            pv.append(kv.weighted(w.astype(BF16), kv.values(h // 2)))
        yield
        out = [jnp.where(lo, pv[2 * p], pv[2 * p + 1]) for p in range(n_pairs)]
        return out[0] if n_pairs == 1 else jnp.concatenate(out, axis=-1)

    def all_dead():
        return jnp.min(c_ref[...]) > SB_DEAD

    first, *companion_results = _interleave([block(first_kv, first_mask), *companions])
    acc_ref[...] = first
    if companions:
        on_companions(companion_results)

    def cond(carry):
        n, dead = carry
        return (n < n_earlier) & jnp.logical_not(dead)

    def body(carry):
        n, _ = carry
        acc_ref[...] += _interleave([block(earlier_kv(n), None)])[0]
        return n + 1, all_dead()

    lax.while_loop(cond, body, (jnp.int32(0), all_dead()))
    o_ref[...] = acc_ref[...].astype(o_ref.dtype)


def _mixer_ab_prompt_kernel(q_ref, k_ref, v_ref, qb_ref, kb_ref, vb_ref, la_ref, o_ref, ob_ref, s_out_ref,
                            acc_ref, c_ref, carry_ref, *, tb, n_pairs):
    qi = pl.program_id(1)

    @pl.when(qi == 0)
    def _():
        carry_ref[...] = jnp.zeros_like(carry_ref)

    pairs = range(B_HEADS // 2)
    gla = [_gla_pair_tile(qb_ref[:, _pair_cols(p)], kb_ref[:, _pair_cols(p)], vb_ref[:, _pair_v_cols(p)],
                          la_ref[:, _pair_cols(p)], [carry_ref[2 * p], carry_ref[2 * p + 1]],
                          _gla_rows_emit(ob_ref, p)) for p in pairs]
    q_heads, lo = _split_heads(q_ref, n_pairs, tb)

    def kv_block(j):
        keys = pl.ds(pl.multiple_of(j * tb, tb), tb)
        return _KV(k_ref[:, keys], v_ref[:, keys], True)

    strictly_earlier = (lax.broadcasted_iota(jnp.int32, (tb, tb), 1)
                        < lax.broadcasted_iota(jnp.int32, (tb, tb), 0))
    def keep_states(states):
        for p in pairs:
            carry_ref[2 * p], carry_ref[2 * p + 1] = states[p]
            s_out_ref[2 * p] = states[p][0][:B_DK, :]
            s_out_ref[2 * p + 1] = states[p][1][B_DK:, :]

    _sb_core(q_heads, lo, kv_block(qi), strictly_earlier, lambda n: kv_block(qi - 1 - n), qi,
             acc_ref, c_ref, o_ref, n_pairs, companions=gla, on_companions=keep_states)


def _sb_sample_kernel(q_ref, kn_ref, vn_ref, kc_ref, vc_ref, o_ref, acc_ref, c_ref, kpad_ref, vpad_ref,
                      *, ts, tk, n_cache_blocks, n_pairs):
    q_heads, lo = _split_heads(q_ref, n_pairs, ts)
    kpad_ref[...] = jnp.zeros_like(kpad_ref)
    vpad_ref[...] = jnp.zeros_like(vpad_ref)
    kpad_ref[:ts, :] = kn_ref[...]
    vpad_ref[:ts, :] = vn_ref[...]

    def cache_block(n):
        keys = pl.ds(pl.multiple_of((n_cache_blocks - 1 - n) * tk, tk), tk)
        return _KV(kc_ref[:, keys].astype(BF16), vc_ref[:, keys].astype(BF16), True)

    strictly_earlier = (lax.broadcasted_iota(jnp.int32, (ts, tk), 1)
                        < lax.broadcasted_iota(jnp.int32, (ts, tk), 0))
    _sb_core(q_heads, lo, _KV(kpad_ref[...], vpad_ref[...], False), strictly_earlier, cache_block,
             n_cache_blocks, acc_ref, c_ref, o_ref, n_pairs)


def _mixer_ab_prompt(q, k, v, qb, kb, vb, la):
    b, t, _ = q.shape
    tb = ATT_BLOCK
    assert t % tb == 0 and tb % GLA_CHUNK == 0 and tb <= GLA_MAX_TILE
    n_pairs = A_WIDTH // LANES
    rows = lambda width: pl.BlockSpec((None, tb, width), lambda bi, qi: (bi, qi, 0))
    kv_spec = pl.BlockSpec((None, A_WIDTH, t), lambda bi, qi: (bi, 0, 0))
    state = (B_HEADS, B_DK, B_DV)
    return pl.pallas_call(
        functools.partial(_mixer_ab_prompt_kernel, tb=tb, n_pairs=n_pairs),
        grid=(b, t // tb),
        in_specs=[rows(A_WIDTH), kv_spec, kv_spec, rows(B_KW), rows(B_KW), rows(B_VW), rows(B_KW)],
        out_specs=[rows(A_WIDTH), rows(B_VW), pl.BlockSpec((None, *state), lambda bi, qi: (bi, 0, 0, 0))],
        out_shape=[jax.ShapeDtypeStruct(q.shape, BF16), jax.ShapeDtypeStruct(vb.shape, F32),
                   jax.ShapeDtypeStruct((b, *state), F32)],
        scratch_shapes=[pltpu.VMEM((tb, A_WIDTH), F32), pltpu.VMEM((2 * n_pairs, tb, 1), F32),
                        pltpu.VMEM((B_HEADS, LANES, B_DV), F32)],
        compiler_params=_params("parallel", "arbitrary"),
        name="mixer_ab",
    )(q, k, v, qb, kb, vb, la)


def _sb_attention_sample(q, k_new, v_new, k_cache, v_cache, layer):
    b, ts, _ = q.shape
    past = k_cache.shape[3]
    tk = ATT_BLOCK
    assert past % tk == 0 and ts <= tk
    n_pairs = SB_PAIRS_PER_STEP
    w = n_pairs * LANES
    new_spec = pl.BlockSpec((None, ts, w), lambda bi, hp: (bi, 0, hp))
    cache_spec = pl.BlockSpec((None, None, w, past), lambda bi, hp: (layer, bi, hp, 0))
    return pl.pallas_call(
        functools.partial(_sb_sample_kernel, ts=ts, tk=tk, n_cache_blocks=past // tk, n_pairs=n_pairs),
        grid=(b, A_WIDTH // w),
        in_specs=[new_spec, new_spec, new_spec, cache_spec, cache_spec],
        out_specs=new_spec,
        out_shape=jax.ShapeDtypeStruct(q.shape, BF16),
        scratch_shapes=[pltpu.VMEM((ts, w), F32), pltpu.VMEM((2 * n_pairs, ts, 1), F32),
                        pltpu.VMEM((tk, w), BF16), pltpu.VMEM((tk, w), BF16)],
        compiler_params=_params("parallel", "parallel"),
        name="sb_attention_sample",
    )(q, k_new, v_new, k_cache, v_cache)


def _band_core(q_heads, lo, kv, pens, bias, o_ref, n_pairs):
    heads = range(2 * n_pairs)
    blocks = range(len(kv))
    z = [[kv[i].scores(q_heads[h], h // 2) + bias(h, i) for i in blocks] for h in heads]
    acc = []
    for h in heads:
        m = None
        for i in blocks:
            mi = jnp.max(z[h][i], axis=-1, keepdims=True)
            if pens[i] is not None:
                mi = mi + pens[i]
            m = mi if m is None else jnp.maximum(m, mi)
        a = None
        for i in blocks:
            shift = m if pens[i] is None else m - pens[i]
            p = jnp.exp2(z[h][i] - shift).astype(BF16)
            v = kv[i].values(h // 2)
            ones = jnp.ones_like(v)
            first = kv[i].head_lanes()
            v = jnp.where(first, v, ones) if h % 2 == 0 else jnp.where(first, ones, v)
            pv = kv[i].weighted(p, v)
            a = pv if a is None else a + pv
        acc.append(a)
    for p in range(n_pairs):
        a0, a1 = acc[2 * p], acc[2 * p + 1]
        o_ref[:, _pair_cols(p)] = jnp.where(lo, a0 / pltpu.roll(a0, HEAD_DIM, axis=1),
                                            a1 / pltpu.roll(a1, HEAD_DIM, axis=1)).astype(o_ref.dtype)


def _band_prompt_kernel(q_ref, k_ref, v_ref, bias_ref, o_ref, *, tq, tk, n_pairs):
    qi = pl.program_id(2)
    q_heads, lo = _split_heads(q_ref, n_pairs, tq)
    kv, pens = [], []
    for dj in range(3):
        j = qi - dj
        pens.append(None if dj == 0 else jnp.where(j >= 0, 0.0, NEG_BIG).astype(F32))
        keys = pl.ds(pl.multiple_of(jnp.maximum(j, 0) * tk, tk), tk)
        kv.append(_KV(k_ref[:, keys], v_ref[:, keys], True))
    _band_core(q_heads, lo, kv, pens, lambda h, i: bias_ref[h, i], o_ref, n_pairs)


def _band_sample_kernel(q_ref, kn_ref, vn_ref, kc_ref, vc_ref, bias_ref, o_ref, kpad_ref, vpad_ref,
                        *, ts, tk, n_cache_blocks, n_pairs):
    q_heads, lo = _split_heads(q_ref, n_pairs, ts)
    kpad_ref[...] = jnp.zeros_like(kpad_ref)
    vpad_ref[...] = jnp.zeros_like(vpad_ref)
    kpad_ref[:ts, :] = kn_ref[...]
    vpad_ref[:ts, :] = vn_ref[...]
    kv = [_KV(kpad_ref[...], vpad_ref[...], False)]
    for dj in range(1, n_cache_blocks + 1):
        keys = slice((n_cache_blocks - dj) * tk, (n_cache_blocks - dj + 1) * tk)
        kv.append(_KV(kc_ref[:, keys].astype(BF16), vc_ref[:, keys].astype(BF16), True))
    is_new_key = lax.broadcasted_iota(jnp.int32, (ts, tk), 1) < ts

    def bias(h, i):
        return jnp.where(is_new_key, bias_ref[h, 0], NEG_BIG) if i == 0 else bias_ref[h, i]

    _band_core(q_heads, lo, kv, [None] * len(kv), bias, o_ref, n_pairs)


def _band_attention_prompt(q, k, v, bias):
    b, t, _ = q.shape
    tq = tk = ATT_BLOCK
    assert t % tk == 0
    n_pairs = BAND_PAIRS_PER_STEP
    w = n_pairs * LANES
    kern = functools.partial(_band_prompt_kernel, tq=tq, tk=tk, n_pairs=n_pairs)
    kv_spec = pl.BlockSpec((None, w, t), lambda bi, hp, qi: (bi, hp, 0))
    q_spec = pl.BlockSpec((None, tq, w), lambda bi, hp, qi: (bi, qi, hp))
    bias_spec = pl.BlockSpec((2 * n_pairs, 3, tq, tk), lambda bi, hp, qi: (hp, 0, 0, 0))
    return pl.pallas_call(
        kern,
        grid=(b, C_WIDTH // w, t // tq),
        in_specs=[q_spec, kv_spec, kv_spec, bias_spec],
        out_specs=q_spec,
        out_shape=jax.ShapeDtypeStruct(q.shape, BF16),
        compiler_params=_params("parallel", "parallel", "parallel"),
        name="band_attention",
    )(q, k, v, bias)


def _band_attention_sample(q, k_new, v_new, k_cache, v_cache, layer, bias):
    b, ts, _ = q.shape
    wc = k_cache.shape[3]
    tk = ATT_BLOCK
    n_cache_blocks = min(wc // tk, 2)
    assert wc % (n_cache_blocks * tk) == 0 and ts <= tk
    n_pairs = BAND_PAIRS_PER_STEP
    w = n_pairs * LANES
    kern = functools.partial(_band_sample_kernel, ts=ts, tk=tk, n_cache_blocks=n_cache_blocks,
                             n_pairs=n_pairs)
    new_spec = pl.BlockSpec((None, ts, w), lambda bi, hp: (bi, 0, hp))
    cache_rows = n_cache_blocks * tk
    cache_spec = pl.BlockSpec((None, None, w, cache_rows),
                              lambda bi, hp: (layer, bi, hp, wc // cache_rows - 1))
    bias_spec = pl.BlockSpec((2 * n_pairs, 1 + n_cache_blocks, ts, tk), lambda bi, hp: (hp, 0, 0, 0))
    return pl.pallas_call(
        kern,
        grid=(b, C_WIDTH // w),
        in_specs=[new_spec, new_spec, new_spec, cache_spec, cache_spec, bias_spec],
        out_specs=new_spec,
        out_shape=jax.ShapeDtypeStruct(q.shape, BF16),
        scratch_shapes=[pltpu.VMEM((tk, w), BF16), pltpu.VMEM((tk, w), BF16)],
        compiler_params=_params("parallel", "parallel"),
        name="band_attention_sample",
    )(q, k_new, v_new, k_cache, v_cache, bias)


def _band_bias_kernel(g_ref, o_ref):
    rows = tk = ATT_BLOCK
    q_chunk = lax.broadcasted_iota(jnp.int32, (rows, tk), 0) // CHUNK
    k_chunk = lax.broadcasted_iota(jnp.int32, (rows, tk), 1) // CHUNK
    for dj in range(3):
        g = jnp.broadcast_to(g_ref[0, dj], (rows, 2 * tk))
        tile = pltpu.roll(g, 0, axis=1, stride=1, stride_axis=0)[:, :tk]
        diff = dj * (tk // CHUNK) + q_chunk - k_chunk
        seen = (diff >= 0) & (diff <= C_LEFT_CHUNKS)
        o_ref[0, dj] = jnp.where(seen, tile * LOG2E, NEG_BIG)


def _band_bias(rel_table):
    rows = tk = ATT_BLOCK
    c = jnp.arange(2 * tk, dtype=jnp.int32)
    u = jnp.where(c <= tk, -c, 2 * tk - c)
    idx = jnp.clip(jnp.arange(3, dtype=jnp.int32)[:, None] * tk + u[None, :], REL_MIN, REL_MAX) - REL_MIN
    g = rel_table[:, idx].astype(F32).reshape(C_HEADS, 3, 1, 2 * tk)
    return pl.pallas_call(
        _band_bias_kernel,
        grid=(C_HEADS,),
        in_specs=[pl.BlockSpec((1, 3, 1, 2 * tk), lambda h: (h, 0, 0, 0))],
        out_specs=pl.BlockSpec((1, 3, rows, tk), lambda h: (h, 0, 0, 0)),
        out_shape=jax.ShapeDtypeStruct((C_HEADS, 3, rows, tk), F32),
        compiler_params=_params("parallel"),
        name="band_bias",
    )(g)


def _gla_pair_tile(q, k, v, la, states, emit):
    L = GLA_CHUNK
    chunks = range(q.shape[0] // L)
    row = lax.broadcasted_iota(jnp.int32, (L, L), 0)
    colm = lax.broadcasted_iota(jnp.int32, (L, L), 1)
    tri = (colm <= row).astype(BF16)
    causal = colm <= row
    lane = lax.broadcasted_iota(jnp.int32, (L, LANES), 1)
    sub = lax.broadcasted_iota(jnp.int32, (LANES, B_DV), 0)
    mine = [(lane >= h * B_DK) & (lane < (h + 1) * B_DK) for h in range(2)]
    mine_rows = [(sub >= h * B_DK) & (sub < (h + 1) * B_DK) for h in range(2)]
    rows = [slice(c * L, (c + 1) * L) for c in chunks]

    b = []
    for r in rows:
        g_hi, g_lo = _split_bf16(la[r, :])
        b.append(_dot(tri, g_hi) + _dot(tri, g_lo))
    yield
    qg, qg_h, kg, kd_t, decay = [], [], [], [], []
    for c, r in zip(chunks, rows):
        qg_c = q[r, :] * (B_DK ** -0.5) * jnp.exp(b[c])
        qg.append(qg_c.astype(BF16))
        qg_h.append([jnp.where(mine[h], qg_c, 0.0).astype(BF16) for h in range(2)])
        kg.append((k[r, :] * jnp.exp(-b[c])).astype(BF16))
        b_t = b[c].T
        b_last = b_t[:, L - 1:L]
        kd_t.append((k[r, :].T * jnp.exp(b_last - b_t)).astype(BF16))
        decay.append(jnp.exp(b_last))
    yield
    att =[[jnp.where(causal, _dot_nt(qg_h[c][h], kg[c]), 0.0).astype(BF16) for h in range(2)]
           for c in chunks]
    yield
    o_intra, own = [], []
    for c, r in zip(chunks, rows):
        v_h = [v[r, h * B_DV:(h + 1) * B_DV].astype(BF16) for h in range(2)]
        o_intra.append([_dot(att[c][h], v_h[h]) for h in range(2)])
        own.append([jnp.where(mine_rows[h], _dot(kd_t[c], v_h[h]), 0.0) for h in range(2)])

    yield
    states = list(states)
    start = []
    for c in chunks:
        start.append([s.astype(BF16) for s in states])
        states = [decay[c] * states[h] + own[c][h] for h in range(2)]
    yield
    for c in chunks:
        for h in range(2):
            emit(c, h, o_intra[c][h] + _dot(qg[c], start[c][h]))
    return states


def _pair_states(s0_ref, p):
    zeros_state = jnp.zeros((B_DK, B_DV), F32)
    return [jnp.concatenate([s0_ref[2 * p], zeros_state], axis=0),
            jnp.concatenate([zeros_state, s0_ref[2 * p + 1]], axis=0)]


def _pair_v_cols(p):
    return slice(2 * p * B_DV, 2 * (p + 1) * B_DV)


def _gla_rows_emit(o_ref, p):
    def emit(c, h, o):
        o_ref[c * GLA_CHUNK:(c + 1) * GLA_CHUNK, (2 * p + h) * B_DV:(2 * p + h + 1) * B_DV] = o
    return emit


def _gla_kernel(q_ref, k_ref, v_ref, la_ref, s0_ref, o_ref, s_out_ref):
    pairs = range(B_HEADS // 2)
    states = _interleave([
        _gla_pair_tile(q_ref[:, _pair_cols(p)], k_ref[:, _pair_cols(p)], v_ref[:, _pair_v_cols(p)],
                       la_ref[:, _pair_cols(p)], _pair_states(s0_ref, p), _gla_rows_emit(o_ref, p))
        for p in pairs])
    for p in pairs:
        s_out_ref[2 * p] = states[p][0][:B_DK, :]
        s_out_ref[2 * p + 1] = states[p][1][B_DK:, :]


def _gla(q, k, v, la, s0):
    b, t, _ = q.shape
    assert t % GLA_CHUNK == 0 and t <= GLA_MAX_TILE
    qk_spec = pl.BlockSpec((None, t, B_KW), lambda bi: (bi, 0, 0))
    v_spec = pl.BlockSpec((None, t, B_VW), lambda bi: (bi, 0, 0))
    s_spec = pl.BlockSpec((None, B_HEADS, B_DK, B_DV), lambda bi: (bi, 0, 0, 0))
    return pl.pallas_call(
        _gla_kernel,
        grid=(b,),
        in_specs=[qk_spec, qk_spec, v_spec, qk_spec, s_spec],
        out_specs=[v_spec, s_spec],
        out_shape=[jax.ShapeDtypeStruct(v.shape, F32), jax.ShapeDtypeStruct(s0.shape, F32)],
        compiler_params=_params("parallel"),
        name="gla",
    )(q, k, v, la, s0)


def _pad_rows(x, n):
    return jnp.pad(x, ((0, 0), (0, n - x.shape[1]), (0, 0)))


def _heads_last(x, heads):
    n, b, _, s = x.shape
    return jnp.transpose(x.reshape(n, b, heads, HEAD_DIM, s), (0, 1, 4, 2, 3))


def _feature_major(cache):
    n, b, s, heads, hd = cache.shape
    return jnp.transpose(cache, (0, 1, 3, 4, 2)).reshape(n, b, heads * hd, s)


def _row_tile(m):
    for tm in (512, 256, 128, 64, 32, 16, 8):
        if m % tm == 0:
            return tm
    raise ValueError(f"token count {m} is not a multiple of 8")


def kernel(x_prompt, x_sample, cache_a_k, cache_a_v, state_b, cache_c_k, cache_c_v, norm_mix_g, norm_ffn_g, w_in_ab, w_gate_b, b_gate_b, norm_gla_g, w_out_ab, w_qkv_c, rel_bias_c, w_out_c, w_ffn_gate, w_ffn_up, w_ffn_down, norm_final_g):
    bp, tp, d = x_prompt.shape
    bs, ts, _ = x_sample.shape
    depth = norm_mix_g.shape[0]
    past = cache_a_k.shape[2]
    wc = cache_c_k.shape[2]
    assert tp % ATT_BLOCK == 0 and past % ATT_BLOCK == 0 and wc % ATT_BLOCK == 0
    assert ts <= GLA_CHUNK and ts % 8 == 0
    mp, ms = bp * tp, bs * ts
    tmp, tms = _row_tile(tp), _row_tile(ms)
    xp = x_prompt.reshape(mp, d)
    xs = x_sample.reshape(ms, d)
    row2 = lambda a: a.reshape(1, -1)

    a_ks, a_vs, b_sp, b_ss, c_ks, c_vs = [], [], [], [], [], []
    a_kv_prompt = c_kv_prompt = None
    n_ab, n_c = (depth + 1) // 2, depth // 2
    keep = min(C_LEFT_CHUNKS * CHUNK, tp)

    wg_all, wu_all, wd_all = (w.astype(BF16) for w in (w_ffn_gate, w_ffn_up, w_ffn_down))
    kv0, kv1 = A_WIDTH, 3 * A_WIDTH
    o = 3 * A_WIDTH + 2 * B_KW + B_VW
    w_main_all = jnp.concatenate([w_in_ab[:, :, :kv0], w_in_ab[:, :, kv1:o], w_in_ab[:, :, o + B_GATE_RANK:]],
                                 axis=2).astype(BF16)
    w_kv_ab_all = w_in_ab[:, :, kv0:kv1].astype(BF16)
    w_kv_ab_t_all = jnp.swapaxes(w_in_ab, 1, 2)[:, kv0:kv1].astype(BF16)
    w_lr_all = jnp.pad(w_in_ab[:, :, o:o + B_GATE_RANK],
                       ((0, 0), (0, 0), (0, LANES - B_GATE_RANK))).astype(BF16)
    w_gate_all = jnp.pad(w_gate_b, ((0, 0), (0, LANES - B_GATE_RANK), (0, 0))).astype(BF16)
    w_out_ab_all = w_out_ab.astype(BF16)
    w_q_all = w_qkv_c[:, :, :C_WIDTH].astype(BF16)
    w_kv_c_all = w_qkv_c[:, :, C_WIDTH:].astype(BF16)
    w_kv_c_t_all = jnp.swapaxes(w_qkv_c[:, :, C_WIDTH:], 1, 2).astype(BF16)
    w_out_c_all = w_out_c.astype(BF16)
    cache_a_k_fm, cache_a_v_fm, cache_c_k_fm, cache_c_v_fm = (
        _feature_major(c) for c in (cache_a_k, cache_a_v, cache_c_k, cache_c_v))

    for layer in range(depth):
        i = layer // 2
        g_mix = row2(norm_mix_g[layer])
        ffn = (row2(norm_ffn_g[layer]), _Slab(wg_all, layer), _Slab(wu_all, layer), _Slab(wd_all, layer),
               row2(norm_final_g) if layer == depth - 1 else None)
        if layer % 2 == 0:
            w_main, w_kv, w_kv_t, w_lr, w_gate, w_out = (
                _Slab(w, i) for w in (w_main_all, w_kv_ab_all, w_kv_ab_t_all, w_lr_all, w_gate_all,
                                      w_out_ab_all))
            b_gate = row2(b_gate_b[i])
            g_gla = row2(norm_gla_g[i])

            qa, ka, va, kab, vab, qb, kb, vb, r, la = _proj_ab(
                xp, g_mix, w_main, w_kv_t, w_lr, w_gate, b_gate, tmp, batch=bp,
                stack=_KVStack(i, n_ab, tp, a_kv_prompt))
            a_kv_prompt = (ka, va)
            sh = lambda a: a.reshape(bp, tp, -1)
            oa, ob, sbp = _mixer_ab_prompt(sh(qa), kab, vab, sh(qb), sh(kb), sh(vb), sh(la))
            xp = _layer_tail(xp, (oa.reshape(mp, -1), ob.reshape(mp, -1), r), (g_gla, w_out), *ffn, tmp)
            b_sp.append(sbp)

            qa, ka, va, kab, vab, qb, kb, vb, r, la = _proj_ab(
                xs, g_mix, w_main, w_kv, w_lr, w_gate, b_gate, tms)
            sh = lambda a: a.reshape(bs, ts, -1)
            oa = _sb_attention_sample(sh(qa), sh(kab), sh(vab), cache_a_k_fm, cache_a_v_fm, i)
            pad_t = lambda a: _pad_rows(sh(a), GLA_CHUNK)
            ob, sbs = _gla(pad_t(qb), pad_t(kb), pad_t(vb), pad_t(la), state_b[i])
            xs = _layer_tail(xs, (oa.reshape(ms, -1), ob[:, :ts].reshape(ms, -1), r), (g_gla, w_out), *ffn, tms)
            a_ks.append(ka.reshape(bs, ts, A_HEADS, HEAD_DIM))
            a_vs.append(va.reshape(bs, ts, A_HEADS, HEAD_DIM))
            b_ss.append(sbs)
        else:
            w_q, w_kv, w_kv_t, w_out = (
                _Slab(w, i) for w in (w_q_all, w_kv_c_all, w_kv_c_t_all, w_out_c_all))

            q, k, v, kb16, vb16 = _proj_c(xp, g_mix, w_q, w_kv_t, tmp, batch=bp,
                                          stack=_KVStack(i, n_c, keep, c_kv_prompt))
            c_kv_prompt = (k, v)
            bias = _band_bias(rel_bias_c[i])
            oc = _band_attention_prompt(q.reshape(bp, tp, -1), kb16, vb16, bias)
            xp = _layer_tail(xp, (oc.reshape(mp, -1),), (w_out,), *ffn, tmp)

            q, k, v, kb16, vb16 = _proj_c(xs, g_mix, w_q, w_kv, tms)
            sh = lambda a: a.reshape(bs, ts, -1)
            oc = _band_attention_sample(sh(q), sh(kb16), sh(vb16), cache_c_k_fm, cache_c_v_fm, i, bias)
            xs = _layer_tail(xs, (oc.reshape(ms, -1),), (w_out,), *ffn, tms)
            c_ks.append(k.reshape(bs, ts, C_HEADS, HEAD_DIM))
            c_vs.append(v.reshape(bs, ts, C_HEADS, HEAD_DIM))

    y_prompt = xp.reshape(bp, tp, d)
    y_sample = xs.reshape(bs, ts, d)
    a_kp, a_vp = (_heads_last(a, A_HEADS) for a in a_kv_prompt)
    c_kp, c_vp = (_heads_last(a, C_HEADS) for a in c_kv_prompt)
    return (y_prompt, y_sample, a_kp, a_vp, jnp.stack(a_ks), jnp.stack(a_vs),
            jnp.stack(b_sp), jnp.stack(b_ss), c_kp, c_vp, jnp.stack(c_ks), jnp.stack(c_vs))
```

```python
import functools
import math

import jax
import jax.numpy as jnp
from jax import lax
from jax.experimental import pallas as pl
from jax.experimental.pallas import tpu as pltpu

F32 = jnp.float32
BF16 = jnp.bfloat16

EPS = 1e-6
HEAD_DIM = 64
LANES = 128
A_HEADS = 8
A_WIDTH = A_HEADS * HEAD_DIM
B_HEADS = 4
B_DK = 64
B_DV = 128
B_KW = B_HEADS * B_DK
B_VW = B_HEADS * B_DV
B_GATE_RANK = 16
B_GATE_TEMP = 16.0
GLA_CHUNK = 64
C_HEADS = 16
C_WIDTH = C_HEADS * HEAD_DIM
CHUNK = 64
C_LEFT_CHUNKS = 8
REL_MIN = -(CHUNK - 1)
REL_MAX = 128
ATT_BLOCK = 256
NEG_BIG = -1e30
LOG2E = 1.4426950408889634
SB_DEAD = 104.0
SB_PAIRS_PER_STEP = 4
BAND_PAIRS_PER_STEP = 4
ATT_BLOCKS_PER_STEP = 4
GLA_MAX_TILE = 512
VMEM_LIMIT = 56 * 1024 * 1024


def _params(*sem):
    return pltpu.CompilerParams(dimension_semantics=sem, vmem_limit_bytes=VMEM_LIMIT)


class _Slab:
    def __init__(self, stacked, index):
        self.stacked, self.index, self.shape = stacked, index, stacked.shape[1:]


def _operand(a):
    return a.stacked if isinstance(a, _Slab) else a


def _resident(a):
    if isinstance(a, _Slab):
        index = (a.index,) + (0,) * len(a.shape)
        return pl.BlockSpec((None, *a.shape), lambda *_: index, pipeline_mode=pl.Buffered(1))
    return pl.BlockSpec(a.shape, lambda *_: (0,) * a.ndim, pipeline_mode=pl.Buffered(1))


def _rms(x, g):
    return x * lax.rsqrt(jnp.mean(x * x, axis=-1, keepdims=True) + EPS) * g


def _log_sigmoid_pair(z):
    l = jnp.log1p(jnp.exp(-jnp.abs(z)))
    return jnp.minimum(z, 0.0) - l, jnp.minimum(-z, 0.0) - l


def _split_bf16(x):
    hi = x.astype(BF16)
    lo = (x - hi.astype(F32)).astype(BF16)
    return hi, lo


def _dot(a, b):
    return jnp.dot(a, b, preferred_element_type=F32)


def _dot_nt(a, b):
    return lax.dot_general(a, b, (((1,), (1,)), ((), ())), preferred_element_type=F32)


def _dot_tn(a, b):
    return lax.dot_general(a, b, (((0,), (0,)), ((), ())), preferred_element_type=F32)


def _pair_cols(p):
    return slice(p * LANES, (p + 1) * LANES)


class _KV:
    def __init__(self, k, v, feature_major):
        self.k, self.v, self.feature_major = k, v, feature_major
        self.n_keys = k.shape[1] if feature_major else k.shape[0]

    def scores(self, q_h, p):
        if self.feature_major:
            return _dot(q_h, self.k[_pair_cols(p), :])
        return _dot_nt(q_h, self.k[:, _pair_cols(p)])

    def values(self, p):
        return self.v[_pair_cols(p), :] if self.feature_major else self.v[:, _pair_cols(p)]

    def weighted(self, w, v_p):
        return _dot_nt(w, v_p) if self.feature_major else _dot(w, v_p)

    def head_lanes(self):
        shape = (LANES, self.n_keys) if self.feature_major else (self.n_keys, LANES)
        return lax.broadcasted_iota(jnp.int32, shape, 0 if self.feature_major else 1) < HEAD_DIM


def _split_heads(q_ref, n_pairs, tq):
    lo = lax.broadcasted_iota(jnp.int32, (tq, LANES), 1) < HEAD_DIM
    heads = []
    for p in range(n_pairs):
        q = q_ref[:, _pair_cols(p)]
        heads += [jnp.where(lo, q, jnp.zeros_like(q)), jnp.where(lo, jnp.zeros_like(q), q)]
    return heads, lo


def _emit_kv(y, wkv_ref, k_ref, v_ref, kb_ref, vb_ref, feature_major, kept_tail=None):
    if feature_major:
        kv = _dot_nt(wkv_ref[...], y)
        width = kv.shape[0] // 2
        k, v = kv[:width, :], kv[width:, :]
    else:
        kv = _dot(y, wkv_ref[...])
        width = kv.shape[1] // 2
        k, v = kv[:, :width], kv[:, width:]
    kb_ref[...] = k.astype(BF16)
    vb_ref[...] = v.astype(BF16)
    if kept_tail is None:
        k_ref[...] = k
        v_ref[...] = v
    else:
        n_tiles, n_kept = kept_tail
        tile = pl.program_id(0) % n_tiles
        tm = k.shape[1]

        @pl.when(tile >= n_tiles - n_kept)
        def _():
            cols = pl.ds(pl.multiple_of((tile - (n_tiles - n_kept)) * tm, tm), tm)
            k_ref[:, cols] = k
            v_ref[:, cols] = v


def _proj_ab_kernel(x_ref, g_ref, w_ref, wkv_ref, wlr_ref, wgate_ref, bgate_ref, *refs,
                    feature_major, n_aliased):
    qa_ref, ka_ref, va_ref, kab_ref, vab_ref, qb_ref, kb_ref, vb_ref, r_ref, la_ref = refs[n_aliased:]
    y = _rms(x_ref[...], g_ref[...]).astype(BF16)
    _emit_kv(y, wkv_ref, ka_ref, va_ref, kab_ref, vab_ref, feature_major)
    z = _dot(y, w_ref[...])
    c = 0
    qa_ref[...] = (z[:, c:c + A_WIDTH] * (HEAD_DIM ** -0.5)).astype(BF16); c += A_WIDTH
    qb_ref[...] = z[:, c:c + B_KW]; c += B_KW
    kb_ref[...] = z[:, c:c + B_KW]; c += B_KW
    vb_ref[...] = z[:, c:c + B_VW]; c += B_VW
    r_ref[...] = z[:, c:c + B_VW]
    g_lr = _dot(y, wlr_ref[...])
    gate = _dot(g_lr.astype(BF16), wgate_ref[...]) + bgate_ref[...]
    la_ref[...] = _log_sigmoid_pair(gate)[0] * (1.0 / B_GATE_TEMP)


class _KVStack:
    def __init__(self, layer, n_layers, keep, previous=None):
        self.layer, self.n_layers, self.keep, self.previous = layer, n_layers, keep, previous


def _kv_out(m, width, tm, batch, stack):
    if batch is None:
        spec = pl.BlockSpec((tm, width), lambda i: (i, 0))
        return [spec] * 4, [jax.ShapeDtypeStruct((m, width), dt) for dt in (F32, F32, BF16, BF16)], None
    t = m // batch
    assert t % tm == 0 and stack.keep % tm == 0 and stack.keep <= t
    n_tiles, layer = t // tm, stack.layer
    copy_spec = pl.BlockSpec((None, width, tm), lambda i: (i // n_tiles, 0, i % n_tiles))
    copy_shape = jax.ShapeDtypeStruct((batch, width, t), BF16)
    if stack.keep == t:
        f32_spec = pl.BlockSpec((None, None, width, tm), lambda i: (layer, i // n_tiles, 0, i % n_tiles))
        kept_tail = None
    else:
        f32_spec = pl.BlockSpec((None, None, width, stack.keep), lambda i: (layer, i // n_tiles, 0, 0))
        kept_tail = (n_tiles, stack.keep // tm)
    f32_shape = jax.ShapeDtypeStruct((stack.n_layers, batch, width, stack.keep), F32)
    return [f32_spec, f32_spec, copy_spec, copy_spec], [f32_shape, f32_shape, copy_shape, copy_shape], kept_tail


def _aliased_stack(stack, n_inputs, first_output):
    if stack is None or stack.previous is None:
        return [], [], {}
    prev = list(stack.previous)
    specs = [pl.BlockSpec(memory_space=pl.ANY)] * len(prev)
    return prev, specs, {n_inputs + j: first_output + j for j in range(len(prev))}


def _proj_ab(x, g, w_main, w_kv, w_lr, w_gate, b_gate, tm, batch=None, stack=None):
    m, d = x.shape
    row = lambda n: pl.BlockSpec((tm, n), lambda i: (i, 0))
    params = (g, w_main, w_kv, w_lr, w_gate, b_gate)
    kv_specs, kv_shapes, kept_tail = _kv_out(m, A_WIDTH, tm, batch, stack)
    assert kept_tail is None
    prev, prev_specs, aliases = _aliased_stack(stack, 1 + len(params), 1)
    rest = [(B_KW, F32), (B_KW, F32), (B_VW, F32), (B_VW, F32), (B_KW, F32)]
    return pl.pallas_call(
        functools.partial(_proj_ab_kernel, feature_major=batch is not None, n_aliased=len(prev)),
        grid=(m // tm,),
        in_specs=[row(d)] + [_resident(a) for a in params] + prev_specs,
        out_specs=[row(A_WIDTH)] + kv_specs + [row(n) for n, _ in rest],
        out_shape=([jax.ShapeDtypeStruct((m, A_WIDTH), BF16)] + kv_shapes
                   + [jax.ShapeDtypeStruct((m, n), dt) for n, dt in rest]),
        input_output_aliases=aliases,
        compiler_params=_params("parallel"),
        name="proj_ab",
    )(x, *[_operand(a) for a in params], *prev)


def _proj_c_kernel(x_ref, g_ref, wq_ref, wkv_ref, *refs, feature_major, n_aliased, kept_tail):
    q_ref, k_ref, v_ref, kb_ref, vb_ref = refs[n_aliased:]
    y = _rms(x_ref[...], g_ref[...]).astype(BF16)
    _emit_kv(y, wkv_ref, k_ref, v_ref, kb_ref, vb_ref, feature_major, kept_tail)
    q_ref[...] = (_dot(y, wq_ref[...]) * (HEAD_DIM ** -0.5 * LOG2E)).astype(BF16)


def _proj_c(x, g, w_q, w_kv, tm, batch=None, stack=None):
    m, d = x.shape
    row = lambda n: pl.BlockSpec((tm, n), lambda i: (i, 0))
    params = (g, w_q, w_kv)
    kv_specs, kv_shapes, kept_tail = _kv_out(m, C_WIDTH, tm, batch, stack)
    prev, prev_specs, aliases = _aliased_stack(stack, 1 + len(params), 1)
    return pl.pallas_call(
        functools.partial(_proj_c_kernel, feature_major=batch is not None, n_aliased=len(prev),
                          kept_tail=kept_tail),
        grid=(m // tm,),
        in_specs=[row(d)] + [_resident(a) for a in params] + prev_specs,
        out_specs=[row(C_WIDTH)] + kv_specs,
        out_shape=[jax.ShapeDtypeStruct((m, C_WIDTH), BF16)] + kv_shapes,
        input_output_aliases=aliases,
        compiler_params=_params("arbitrary" if kept_tail else "parallel"),
        name="proj_c",
    )(x, *[_operand(a) for a in params], *prev)


def _layer_tail_kernel(x_ref, *refs, gla_merge, final_norm):
    if gla_merge:
        oa_ref, ob_ref, r_ref, ggla_ref, wout_ref, *refs = refs
        ob = ob_ref[...]
        parts = []
        for h in range(B_HEADS):
            seg = ob[:, h * B_DV:(h + 1) * B_DV]
            parts.append(seg * lax.rsqrt(jnp.mean(seg * seg, axis=-1, keepdims=True) + EPS))
        r = r_ref[...]
        obn = jnp.concatenate(parts, axis=-1) * ggla_ref[...] * (r * jax.nn.sigmoid(r))
        mix = _dot(oa_ref[...], wout_ref[:A_WIDTH, :]) + _dot(obn.astype(BF16), wout_ref[A_WIDTH:, :])
    else:
        oc_ref, wout_ref, *refs = refs
        mix = _dot(oc_ref[...], wout_ref[...])
    gffn_ref, wg_ref, wu_ref, wd_ref, *refs = refs
    x = x_ref[...] + mix
    y = _rms(x, gffn_ref[...]).astype(BF16)
    h = _dot(y, wg_ref[...])
    u = _dot(y, wu_ref[...])
    a = (h * jax.nn.sigmoid(h) * u).astype(BF16)
    x = x + _dot(a, wd_ref[...])
    if final_norm:
        gfin_ref, o_ref = refs
        o_ref[...] = _rms(x, gfin_ref[...])
    else:
        (o_ref,) = refs
        o_ref[...] = x


def _layer_tail(x, mixer_out, mixer_params, g_ffn, wg, wu, wd, g_fin, tm):
    m, d = x.shape
    row = lambda a: pl.BlockSpec((tm, a.shape[1]), lambda i: (i, 0))
    resident = [*mixer_params, g_ffn, wg, wu, wd] + ([] if g_fin is None else [g_fin])
    return pl.pallas_call(
        functools.partial(_layer_tail_kernel, gla_merge=len(mixer_out) == 3, final_norm=g_fin is not None),
        grid=(m // tm,),
        in_specs=[row(x)] + [row(a) for a in mixer_out] + [_resident(a) for a in resident],
        out_specs=row(x),
        out_shape=jax.ShapeDtypeStruct((m, d), F32),
        compiler_params=_params("parallel"),
        name="layer_tail",
    )(x, *mixer_out, *[_operand(a) for a in resident])


def _interleave(stage_generators):
    results = [None] * len(stage_generators)
    live = list(range(len(stage_generators)))
    while live:
        for i in list(live):
            try:
                next(stage_generators[i])
            except StopIteration as done:
                results[i] = done.value
                live.remove(i)
    return results


def _sb_core(q_heads, lo, first_kv, first_mask, earlier_kv, n_earlier, acc_ref, c_ref, o_ref, n_pairs,
             companions=(), on_companions=None):
    tk = first_kv.n_keys
    heads = range(2 * n_pairs)
    later = (lax.broadcasted_iota(jnp.int32, (tk, tk), 0)
             > lax.broadcasted_iota(jnp.int32, (tk, tk), 1)).astype(BF16)
    c_ref[...] = jnp.zeros_like(c_ref)

    def block(kv, mask):
        z = [kv.scores(q_heads[h], h // 2) for h in heads]
        yield
        log_beta, drop, after = [], [], []
        for h in heads:
            d = jnp.maximum(z[h], 0.0) + jnp.log(1.0 + jnp.exp(-jnp.abs(z[h])))
            log_beta.append(z[h] - d)
            drop.append(d if mask is None else jnp.where(mask, d, 0.0))
        yield
        for h in heads:
            hi, lo_part = _split_bf16(drop[h])
            after.append(_dot(hi, later) + _dot(lo_part, later))
        yield
        pv = []
        for h in heads:
            c = c_ref[h]
            w = jnp.exp(log_beta[h] - after[h] - c)
            if mask is not None:
                w = jnp.where(mask, w, 0.0)
            c_ref[h] = c + after[h][:, 0:1] + drop[h][:, 0:1]
            pv.append(kv.weighted(w.astype(BF16), kv.values(h // 2)))
        yield
        out = [jnp.where(lo, pv[2 * p], pv[2 * p + 1]) for p in range(n_pairs)]
        return out[0] if n_pairs == 1 else jnp.concatenate(out, axis=-1)

    def all_dead():
        return jnp.min(c_ref[...]) > SB_DEAD

    first, *companion_results = _interleave([block(first_kv, first_mask), *companions])
    acc_ref[...] = first
    if companions:
        on_companions(companion_results)

    def cond(carry):
        n, dead = carry
        return (n < n_earlier) & jnp.logical_not(dead)

    def body(carry):
        n, _ = carry
        acc_ref[...] += _interleave([block(earlier_kv(n), None)])[0]
        return n + 1, all_dead()

    lax.while_loop(cond, body, (jnp.int32(0), all_dead()))
    o_ref[...] = acc_ref[...].astype(o_ref.dtype)


def _mixer_ab_prompt_kernel(q_ref, k_ref, v_ref, qb_ref, kb_ref, vb_ref, la_ref, o_ref, ob_ref, s_out_ref,
                            acc_ref, c_ref, carry_ref, *, tb, n_pairs, blocks_per_step):
    pairs = range(B_HEADS // 2)
    strictly_earlier = (lax.broadcasted_iota(jnp.int32, (tb, tb), 1)
                        < lax.broadcasted_iota(jnp.int32, (tb, tb), 0))

    def kv_block(j):
        keys = pl.ds(pl.multiple_of(j * tb, tb), tb)
        return _KV(k_ref[:, keys], v_ref[:, keys], True)

    def keep_states(states):
        for p in pairs:
            carry_ref[2 * p], carry_ref[2 * p + 1] = states[p]
            s_out_ref[2 * p] = states[p][0][:B_DK, :]
            s_out_ref[2 * p + 1] = states[p][1][B_DK:, :]

    @pl.when(pl.program_id(1) == 0)
    def _():
        carry_ref[...] = jnp.zeros_like(carry_ref)

    def one_block(s, carry):
        qi = pl.program_id(1) * blocks_per_step + s
        rows = pl.ds(pl.multiple_of(s * tb, tb), tb)
        gla = [_gla_pair_tile(qb_ref[rows, _pair_cols(p)], kb_ref[rows, _pair_cols(p)],
                              vb_ref[rows, _pair_v_cols(p)], la_ref[rows, _pair_cols(p)],
                              [carry_ref[2 * p], carry_ref[2 * p + 1]],
                              _gla_rows_emit(ob_ref.at[rows, :], p)) for p in pairs]
        q_heads, lo = _split_heads(q_ref.at[rows, :], n_pairs, tb)
        _sb_core(q_heads, lo, kv_block(qi), strictly_earlier, lambda n: kv_block(qi - 1 - n), qi,
                 acc_ref, c_ref, o_ref.at[rows, :], n_pairs, companions=gla, on_companions=keep_states)
        return carry

    lax.fori_loop(0, blocks_per_step, one_block, 0)


def _sb_sample_kernel(q_ref, kn_ref, vn_ref, kc_ref, vc_ref, o_ref, acc_ref, c_ref, kpad_ref, vpad_ref,
                      *, ts, tk, n_cache_blocks, n_pairs):
    q_heads, lo = _split_heads(q_ref, n_pairs, ts)
    kpad_ref[...] = jnp.zeros_like(kpad_ref)
    vpad_ref[...] = jnp.zeros_like(vpad_ref)
    kpad_ref[:ts, :] = kn_ref[...]
    vpad_ref[:ts, :] = vn_ref[...]

    def cache_block(n):
        keys = pl.ds(pl.multiple_of((n_cache_blocks - 1 - n) * tk, tk), tk)
        return _KV(kc_ref[:, keys].astype(BF16), vc_ref[:, keys].astype(BF16), True)

    strictly_earlier = (lax.broadcasted_iota(jnp.int32, (ts, tk), 1)
                        < lax.broadcasted_iota(jnp.int32, (ts, tk), 0))
    _sb_core(q_heads, lo, _KV(kpad_ref[...], vpad_ref[...], False), strictly_earlier, cache_block,
             n_cache_blocks, acc_ref, c_ref, o_ref, n_pairs)


def _mixer_ab_prompt(q, k, v, qb, kb, vb, la):
    b, t, _ = q.shape
    tb = ATT_BLOCK
    assert t % tb == 0 and tb % GLA_CHUNK == 0 and tb <= GLA_MAX_TILE
    n_pairs = A_WIDTH // LANES
    per_step = math.gcd(t // tb, ATT_BLOCKS_PER_STEP)
    rows = lambda width: pl.BlockSpec((None, per_step * tb, width), lambda bi, qi: (bi, qi, 0))
    kv_spec = pl.BlockSpec((None, A_WIDTH, t), lambda bi, qi: (bi, 0, 0))
    state = (B_HEADS, B_DK, B_DV)
    return pl.pallas_call(
        functools.partial(_mixer_ab_prompt_kernel, tb=tb, n_pairs=n_pairs, blocks_per_step=per_step),
        grid=(b, t // (per_step * tb)),
        in_specs=[rows(A_WIDTH), kv_spec, kv_spec, rows(B_KW), rows(B_KW), rows(B_VW), rows(B_KW)],
        out_specs=[rows(A_WIDTH), rows(B_VW), pl.BlockSpec((None, *state), lambda bi, qi: (bi, 0, 0, 0))],
        out_shape=[jax.ShapeDtypeStruct(q.shape, BF16), jax.ShapeDtypeStruct(vb.shape, F32),
                   jax.ShapeDtypeStruct((b, *state), F32)],
        scratch_shapes=[pltpu.VMEM((tb, A_WIDTH), F32), pltpu.VMEM((2 * n_pairs, tb, 1), F32),
                        pltpu.VMEM((B_HEADS, LANES, B_DV), F32)],
        compiler_params=_params("parallel", "arbitrary"),
        name="mixer_ab",
    )(q, k, v, qb, kb, vb, la)


def _sb_attention_sample(q, k_new, v_new, k_cache, v_cache, layer):
    b, ts, _ = q.shape
    past = k_cache.shape[3]
    tk = ATT_BLOCK
    assert past % tk == 0 and ts <= tk
    n_pairs = SB_PAIRS_PER_STEP
    w = n_pairs * LANES
    new_spec = pl.BlockSpec((None, ts, w), lambda bi, hp: (bi, 0, hp))
    cache_spec = pl.BlockSpec((None, None, w, past), lambda bi, hp: (layer, bi, hp, 0))
    return pl.pallas_call(
        functools.partial(_sb_sample_kernel, ts=ts, tk=tk, n_cache_blocks=past // tk, n_pairs=n_pairs),
        grid=(b, A_WIDTH // w),
        in_specs=[new_spec, new_spec, new_spec, cache_spec, cache_spec],
        out_specs=new_spec,
        out_shape=jax.ShapeDtypeStruct(q.shape, BF16),
        scratch_shapes=[pltpu.VMEM((ts, w), F32), pltpu.VMEM((2 * n_pairs, ts, 1), F32),
                        pltpu.VMEM((tk, w), BF16), pltpu.VMEM((tk, w), BF16)],
        compiler_params=_params("parallel", "parallel"),
        name="sb_attention_sample",
    )(q, k_new, v_new, k_cache, v_cache)


def _band_core(q_heads, lo, kv, pens, bias, o_ref, n_pairs):
    heads = range(2 * n_pairs)
    blocks = range(len(kv))
    z = [[kv[i].scores(q_heads[h], h // 2) + bias(h, i) for i in blocks] for h in heads]
    acc = []
    for h in heads:
        m = None
        for i in blocks:
            mi = jnp.max(z[h][i], axis=-1, keepdims=True)
            if pens[i] is not None:
                mi = mi + pens[i]
            m = mi if m is None else jnp.maximum(m, mi)
        a = None
        for i in blocks:
            shift = m if pens[i] is None else m - pens[i]
            p = jnp.exp2(z[h][i] - shift).astype(BF16)
            v = kv[i].values(h // 2)
            ones = jnp.ones_like(v)
            first = kv[i].head_lanes()
            v = jnp.where(first, v, ones) if h % 2 == 0 else jnp.where(first, ones, v)
            pv = kv[i].weighted(p, v)
            a = pv if a is None else a + pv
        acc.append(a)
    for p in range(n_pairs):
        a0, a1 = acc[2 * p], acc[2 * p + 1]
        o_ref[:, _pair_cols(p)] = jnp.where(lo, a0 / pltpu.roll(a0, HEAD_DIM, axis=1),
                                            a1 / pltpu.roll(a1, HEAD_DIM, axis=1)).astype(o_ref.dtype)


def _band_prompt_kernel(q_ref, k_ref, v_ref, bias_ref, o_ref, *, tq, tk, n_pairs, blocks_per_step):
    def one_block(s, carry):
        qi = pl.program_id(2) * blocks_per_step + s
        rows = pl.ds(pl.multiple_of(s * tq, tq), tq)
        q_heads, lo = _split_heads(q_ref.at[rows, :], n_pairs, tq)
        kv, pens = [], []
        for dj in range(3):
            j = qi - dj
            pens.append(None if dj == 0 else jnp.where(j >= 0, 0.0, NEG_BIG).astype(F32))
            keys = pl.ds(pl.multiple_of(jnp.maximum(j, 0) * tk, tk), tk)
            kv.append(_KV(k_ref[:, keys], v_ref[:, keys], True))
        _band_core(q_heads, lo, kv, pens, lambda h, i: bias_ref[h, i], o_ref.at[rows, :], n_pairs)
        return carry

    lax.fori_loop(0, blocks_per_step, one_block, 0)


def _band_sample_kernel(q_ref, kn_ref, vn_ref, kc_ref, vc_ref, bias_ref, o_ref, kpad_ref, vpad_ref,
                        *, ts, tk, n_cache_blocks, n_pairs):
    q_heads, lo = _split_heads(q_ref, n_pairs, ts)
    kpad_ref[...] = jnp.zeros_like(kpad_ref)
    vpad_ref[...] = jnp.zeros_like(vpad_ref)
    kpad_ref[:ts, :] = kn_ref[...]
    vpad_ref[:ts, :] = vn_ref[...]
    kv = [_KV(kpad_ref[...], vpad_ref[...], False)]
    for dj in range(1, n_cache_blocks + 1):
        keys = slice((n_cache_blocks - dj) * tk, (n_cache_blocks - dj + 1) * tk)
        kv.append(_KV(kc_ref[:, keys].astype(BF16), vc_ref[:, keys].astype(BF16), True))
    is_new_key = lax.broadcasted_iota(jnp.int32, (ts, tk), 1) < ts

    def bias(h, i):
        return jnp.where(is_new_key, bias_ref[h, 0], NEG_BIG) if i == 0 else bias_ref[h, i]

    _band_core(q_heads, lo, kv, [None] * len(kv), bias, o_ref, n_pairs)


def _band_attention_prompt(q, k, v, bias):
    b, t, _ = q.shape
    tq = tk = ATT_BLOCK
    assert t % tk == 0
    n_pairs = BAND_PAIRS_PER_STEP
    w = n_pairs * LANES
    per_step = math.gcd(t // tq, ATT_BLOCKS_PER_STEP)
    kern = functools.partial(_band_prompt_kernel, tq=tq, tk=tk, n_pairs=n_pairs, blocks_per_step=per_step)
    kv_spec = pl.BlockSpec((None, w, t), lambda bi, hp, qi: (bi, hp, 0))
    q_spec = pl.BlockSpec((None, per_step * tq, w), lambda bi, hp, qi: (bi, qi, hp))
    bias_spec = pl.BlockSpec((2 * n_pairs, 3, tq, tk), lambda bi, hp, qi: (hp, 0, 0, 0))
    return pl.pallas_call(
        kern,
        grid=(b, C_WIDTH // w, t // (per_step * tq)),
        in_specs=[q_spec, kv_spec, kv_spec, bias_spec],
        out_specs=q_spec,
        out_shape=jax.ShapeDtypeStruct(q.shape, BF16),
        compiler_params=_params("parallel", "parallel", "parallel"),
        name="band_attention",
    )(q, k, v, bias)


def _band_attention_sample(q, k_new, v_new, k_cache, v_cache, layer, bias):
    b, ts, _ = q.shape
    wc = k_cache.shape[3]
    tk = ATT_BLOCK
    n_cache_blocks = min(wc // tk, 2)
    assert wc % (n_cache_blocks * tk) == 0 and ts <= tk
    n_pairs = BAND_PAIRS_PER_STEP
    w = n_pairs * LANES
    kern = functools.partial(_band_sample_kernel, ts=ts, tk=tk, n_cache_blocks=n_cache_blocks,
                             n_pairs=n_pairs)
    new_spec = pl.BlockSpec((None, ts, w), lambda bi, hp: (bi, 0, hp))
    cache_rows = n_cache_blocks * tk
    cache_spec = pl.BlockSpec((None, None, w, cache_rows),
                              lambda bi, hp: (layer, bi, hp, wc // cache_rows - 1))
    bias_spec = pl.BlockSpec((2 * n_pairs, 1 + n_cache_blocks, ts, tk), lambda bi, hp: (hp, 0, 0, 0))
    return pl.pallas_call(
        kern,
        grid=(b, C_WIDTH // w),
        in_specs=[new_spec, new_spec, new_spec, cache_spec, cache_spec, bias_spec],
        out_specs=new_spec,
        out_shape=jax.ShapeDtypeStruct(q.shape, BF16),
        scratch_shapes=[pltpu.VMEM((tk, w), BF16), pltpu.VMEM((tk, w), BF16)],
        compiler_params=_params("parallel", "parallel"),
        name="band_attention_sample",
    )(q, k_new, v_new, k_cache, v_cache, bias)


def _band_bias_kernel(g_ref, o_ref):
    rows = tk = ATT_BLOCK
    q_chunk = lax.broadcasted_iota(jnp.int32, (rows, tk), 0) // CHUNK
    k_chunk = lax.broadcasted_iota(jnp.int32, (rows, tk), 1) // CHUNK
    for dj in range(3):
        g = jnp.broadcast_to(g_ref[0, dj], (rows, 2 * tk))
        tile = pltpu.roll(g, 0, axis=1, stride=1, stride_axis=0)[:, :tk]
        diff = dj * (tk // CHUNK) + q_chunk - k_chunk
        seen = (diff >= 0) & (diff <= C_LEFT_CHUNKS)
        o_ref[0, dj] = jnp.where(seen, tile * LOG2E, NEG_BIG)


def _band_bias(rel_table):
    rows = tk = ATT_BLOCK
    c = jnp.arange(2 * tk, dtype=jnp.int32)
    u = jnp.where(c <= tk, -c, 2 * tk - c)
    idx = jnp.clip(jnp.arange(3, dtype=jnp.int32)[:, None] * tk + u[None, :], REL_MIN, REL_MAX) - REL_MIN
    g = rel_table[:, idx].astype(F32).reshape(C_HEADS, 3, 1, 2 * tk)
    return pl.pallas_call(
        _band_bias_kernel,
        grid=(C_HEADS,),
        in_specs=[pl.BlockSpec((1, 3, 1, 2 * tk), lambda h: (h, 0, 0, 0))],
        out_specs=pl.BlockSpec((1, 3, rows, tk), lambda h: (h, 0, 0, 0)),
        out_shape=jax.ShapeDtypeStruct((C_HEADS, 3, rows, tk), F32),
        compiler_params=_params("parallel"),
        name="band_bias",
    )(g)


def _gla_pair_tile(q, k, v, la, states, emit):
    L = GLA_CHUNK
    chunks = range(q.shape[0] // L)
    row = lax.broadcasted_iota(jnp.int32, (L, L), 0)
    colm = lax.broadcasted_iota(jnp.int32, (L, L), 1)
    tri = (colm <= row).astype(BF16)
    causal = colm <= row
    lane = lax.broadcasted_iota(jnp.int32, (L, LANES), 1)
    sub = lax.broadcasted_iota(jnp.int32, (LANES, B_DV), 0)
    mine = [(lane >= h * B_DK) & (lane < (h + 1) * B_DK) for h in range(2)]
    mine_rows = [(sub >= h * B_DK) & (sub < (h + 1) * B_DK) for h in range(2)]
    rows = [slice(c * L, (c + 1) * L) for c in chunks]

    b = []
    for r in rows:
        g_hi, g_lo = _split_bf16(la[r, :])
        b.append(_dot(tri, g_hi) + _dot(tri, g_lo))
    yield
    qg, qg_h, kg, kd_t, decay = [], [], [], [], []
    for c, r in zip(chunks, rows):
        qg_c = q[r, :] * (B_DK ** -0.5) * jnp.exp(b[c])
        qg.append(qg_c.astype(BF16))
        qg_h.append([jnp.where(mine[h], qg_c, 0.0).astype(BF16) for h in range(2)])
        kg.append((k[r, :] * jnp.exp(-b[c])).astype(BF16))
        b_t = b[c].T
        b_last = b_t[:, L - 1:L]
        kd_t.append((k[r, :].T * jnp.exp(b_last - b_t)).astype(BF16))
        decay.append(jnp.exp(b_last))
    yield
    att =[[jnp.where(causal, _dot_nt(qg_h[c][h], kg[c]), 0.0).astype(BF16) for h in range(2)]
           for c in chunks]
    yield
    o_intra, own = [], []
    for c, r in zip(chunks, rows):
        v_h = [v[r, h * B_DV:(h + 1) * B_DV].astype(BF16) for h in range(2)]
        o_intra.append([_dot(att[c][h], v_h[h]) for h in range(2)])
        own.append([jnp.where(mine_rows[h], _dot(kd_t[c], v_h[h]), 0.0) for h in range(2)])

    yield
    states = list(states)
    start = []
    for c in chunks:
        start.append([s.astype(BF16) for s in states])
        states = [decay[c] * states[h] + own[c][h] for h in range(2)]
    yield
    for c in chunks:
        for h in range(2):
            emit(c, h, o_intra[c][h] + _dot(qg[c], start[c][h]))
    return states


def _pair_states(s0_ref, p):
    zeros_state = jnp.zeros((B_DK, B_DV), F32)
    return [jnp.concatenate([s0_ref[2 * p], zeros_state], axis=0),
            jnp.concatenate([zeros_state, s0_ref[2 * p + 1]], axis=0)]


def _pair_v_cols(p):
    return slice(2 * p * B_DV, 2 * (p + 1) * B_DV)


def _gla_rows_emit(o_ref, p):
    def emit(c, h, o):
        o_ref[c * GLA_CHUNK:(c + 1) * GLA_CHUNK, (2 * p + h) * B_DV:(2 * p + h + 1) * B_DV] = o
    return emit


def _gla_kernel(q_ref, k_ref, v_ref, la_ref, s0_ref, o_ref, s_out_ref):
    pairs = range(B_HEADS // 2)
    states = _interleave([
        _gla_pair_tile(q_ref[:, _pair_cols(p)], k_ref[:, _pair_cols(p)], v_ref[:, _pair_v_cols(p)],
                       la_ref[:, _pair_cols(p)], _pair_states(s0_ref, p), _gla_rows_emit(o_ref, p))
        for p in pairs])
    for p in pairs:
        s_out_ref[2 * p] = states[p][0][:B_DK, :]
        s_out_ref[2 * p + 1] = states[p][1][B_DK:, :]


def _gla(q, k, v, la, s0):
    b, t, _ = q.shape
    assert t % GLA_CHUNK == 0 and t <= GLA_MAX_TILE
    qk_spec = pl.BlockSpec((None, t, B_KW), lambda bi: (bi, 0, 0))
    v_spec = pl.BlockSpec((None, t, B_VW), lambda bi: (bi, 0, 0))
    s_spec = pl.BlockSpec((None, B_HEADS, B_DK, B_DV), lambda bi: (bi, 0, 0, 0))
    return pl.pallas_call(
        _gla_kernel,
        grid=(b,),
        in_specs=[qk_spec, qk_spec, v_spec, qk_spec, s_spec],
        out_specs=[v_spec, s_spec],
        out_shape=[jax.ShapeDtypeStruct(v.shape, F32), jax.ShapeDtypeStruct(s0.shape, F32)],
        compiler_params=_params("parallel"),
        name="gla",
    )(q, k, v, la, s0)


def _pad_rows(x, n):
    return jnp.pad(x, ((0, 0), (0, n - x.shape[1]), (0, 0)))


def _heads_last(x, heads):
    n, b, _, s = x.shape
    return jnp.transpose(x.reshape(n, b, heads, HEAD_DIM, s), (0, 1, 4, 2, 3))


def _feature_major(cache):
    n, b, s, heads, hd = cache.shape
    return jnp.transpose(cache, (0, 1, 3, 4, 2)).reshape(n, b, heads * hd, s)


def _row_tile(m):
    for tm in (512, 256, 128, 64, 32, 16, 8):
        if m % tm == 0:
            return tm
    raise ValueError(f"token count {m} is not a multiple of 8")


def kernel(x_prompt, x_sample, cache_a_k, cache_a_v, state_b, cache_c_k, cache_c_v, norm_mix_g, norm_ffn_g, w_in_ab, w_gate_b, b_gate_b, norm_gla_g, w_out_ab, w_qkv_c, rel_bias_c, w_out_c, w_ffn_gate, w_ffn_up, w_ffn_down, norm_final_g):
    bp, tp, d = x_prompt.shape
    bs, ts, _ = x_sample.shape
    depth = norm_mix_g.shape[0]
    past = cache_a_k.shape[2]
    wc = cache_c_k.shape[2]
    assert tp % ATT_BLOCK == 0 and past % ATT_BLOCK == 0 and wc % ATT_BLOCK == 0
    assert ts <= GLA_CHUNK and ts % 8 == 0
    mp, ms = bp * tp, bs * ts
    tmp, tms = _row_tile(tp), _row_tile(ms)
    xp = x_prompt.reshape(mp, d)
    xs = x_sample.reshape(ms, d)
    row2 = lambda a: a.reshape(1, -1)

    a_ks, a_vs, b_sp, b_ss, c_ks, c_vs = [], [], [], [], [], []
    a_kv_prompt = c_kv_prompt = None
    n_ab, n_c = (depth + 1) // 2, depth // 2
    keep = min(C_LEFT_CHUNKS * CHUNK, tp)

    wg_all, wu_all, wd_all = (w.astype(BF16) for w in (w_ffn_gate, w_ffn_up, w_ffn_down))
    kv0, kv1 = A_WIDTH, 3 * A_WIDTH
    o = 3 * A_WIDTH + 2 * B_KW + B_VW
    w_main_all = jnp.concatenate([w_in_ab[:, :, :kv0], w_in_ab[:, :, kv1:o], w_in_ab[:, :, o + B_GATE_RANK:]],
                                 axis=2).astype(BF16)
    w_kv_ab_all = w_in_ab[:, :, kv0:kv1].astype(BF16)
    w_kv_ab_t_all = jnp.swapaxes(w_in_ab, 1, 2)[:, kv0:kv1].astype(BF16)
    w_lr_all = jnp.pad(w_in_ab[:, :, o:o + B_GATE_RANK],
                       ((0, 0), (0, 0), (0, LANES - B_GATE_RANK))).astype(BF16)
    w_gate_all = jnp.pad(w_gate_b, ((0, 0), (0, LANES - B_GATE_RANK), (0, 0))).astype(BF16)
    w_out_ab_all = w_out_ab.astype(BF16)
    w_q_all = w_qkv_c[:, :, :C_WIDTH].astype(BF16)
    w_kv_c_all = w_qkv_c[:, :, C_WIDTH:].astype(BF16)
    w_kv_c_t_all = jnp.swapaxes(w_qkv_c[:, :, C_WIDTH:], 1, 2).astype(BF16)
    w_out_c_all = w_out_c.astype(BF16)
    cache_a_k_fm, cache_a_v_fm, cache_c_k_fm, cache_c_v_fm = (
        _feature_major(c) for c in (cache_a_k, cache_a_v, cache_c_k, cache_c_v))

    for layer in range(depth):
        i = layer // 2
        g_mix = row2(norm_mix_g[layer])
        ffn = (row2(norm_ffn_g[layer]), _Slab(wg_all, layer), _Slab(wu_all, layer), _Slab(wd_all, layer),
               row2(norm_final_g) if layer == depth - 1 else None)
        if layer % 2 == 0:
            w_main, w_kv, w_kv_t, w_lr, w_gate, w_out = (
                _Slab(w, i) for w in (w_main_all, w_kv_ab_all, w_kv_ab_t_all, w_lr_all, w_gate_all,
                                      w_out_ab_all))
            b_gate = row2(b_gate_b[i])
            g_gla = row2(norm_gla_g[i])

            qa, ka, va, kab, vab, qb, kb, vb, r, la = _proj_ab(
                xp, g_mix, w_main, w_kv_t, w_lr, w_gate, b_gate, tmp, batch=bp,
                stack=_KVStack(i, n_ab, tp, a_kv_prompt))
            a_kv_prompt = (ka, va)
            sh = lambda a: a.reshape(bp, tp, -1)
            oa, ob, sbp = _mixer_ab_prompt(sh(qa), kab, vab, sh(qb), sh(kb), sh(vb), sh(la))
            xp = _layer_tail(xp, (oa.reshape(mp, -1), ob.reshape(mp, -1), r), (g_gla, w_out), *ffn, tmp)
            b_sp.append(sbp)

            qa, ka, va, kab, vab, qb, kb, vb, r, la = _proj_ab(
                xs, g_mix, w_main, w_kv, w_lr, w_gate, b_gate, tms)
            sh = lambda a: a.reshape(bs, ts, -1)
            oa = _sb_attention_sample(sh(qa), sh(kab), sh(vab), cache_a_k_fm, cache_a_v_fm, i)
            pad_t = lambda a: _pad_rows(sh(a), GLA_CHUNK)
            ob, sbs = _gla(pad_t(qb), pad_t(kb), pad_t(vb), pad_t(la), state_b[i])
            xs = _layer_tail(xs, (oa.reshape(ms, -1), ob[:, :ts].reshape(ms, -1), r), (g_gla, w_out), *ffn, tms)
            a_ks.append(ka.reshape(bs, ts, A_HEADS, HEAD_DIM))
            a_vs.append(va.reshape(bs, ts, A_HEADS, HEAD_DIM))
            b_ss.append(sbs)
        else:
            w_q, w_kv, w_kv_t, w_out = (
                _Slab(w, i) for w in (w_q_all, w_kv_c_all, w_kv_c_t_all, w_out_c_all))

            q, k, v, kb16, vb16 = _proj_c(xp, g_mix, w_q, w_kv_t, tmp, batch=bp,
                                          stack=_KVStack(i, n_c, keep, c_kv_prompt))
            c_kv_prompt = (k, v)
            bias = _band_bias(rel_bias_c[i])
            oc = _band_attention_prompt(q.reshape(bp, tp, -1), kb16, vb16, bias)
            xp = _layer_tail(xp, (oc.reshape(mp, -1),), (w_out,), *ffn, tmp)

            q, k, v, kb16, vb16 = _proj_c(xs, g_mix, w_q, w_kv, tms)
            sh = lambda a: a.reshape(bs, ts, -1)
            oc = _band_attention_sample(sh(q), sh(kb16), sh(vb16), cache_c_k_fm, cache_c_v_fm, i, bias)
            xs = _layer_tail(xs, (oc.reshape(ms, -1),), (w_out,), *ffn, tms)
            c_ks.append(k.reshape(bs, ts, C_HEADS, HEAD_DIM))
            c_vs.append(v.reshape(bs, ts, C_HEADS, HEAD_DIM))

    y_prompt = xp.reshape(bp, tp, d)
    y_sample = xs.reshape(bs, ts, d)
    a_kp, a_vp = (_heads_last(a, A_HEADS) for a in a_kv_prompt)
    c_kp, c_vp = (_heads_last(a, C_HEADS) for a in c_kv_prompt)
    return (y_prompt, y_sample, a_kp, a_vp, jnp.stack(a_ks), jnp.stack(a_vs),
            jnp.stack(b_sp), jnp.stack(b_ss), c_kp, c_vp, jnp.stack(c_ks), jnp.stack(c_vs))
```

```python
import functools
import math

import jax
import jax.numpy as jnp
from jax import lax
from jax.experimental import pallas as pl
from jax.experimental.pallas import tpu as pltpu

F32 = jnp.float32
BF16 = jnp.bfloat16

EPS = 1e-6
HEAD_DIM = 64
LANES = 128
A_HEADS = 8
A_WIDTH = A_HEADS * HEAD_DIM
B_HEADS = 4
B_DK = 64
B_DV = 128
B_KW = B_HEADS * B_DK
B_VW = B_HEADS * B_DV
B_GATE_RANK = 16
B_GATE_TEMP = 16.0
GLA_CHUNK = 64
C_HEADS = 16
C_WIDTH = C_HEADS * HEAD_DIM
CHUNK = 64
C_LEFT_CHUNKS = 8
REL_MIN = -(CHUNK - 1)
REL_MAX = 128
ATT_BLOCK = 256
NEG_BIG = -1e30
LOG2E = 1.4426950408889634
SB_DEAD = 104.0
SB_PAIRS_PER_STEP = 4
BAND_PAIRS_PER_STEP = 4
ATT_BLOCKS_PER_STEP = 4
GLA_MAX_TILE = 512
VMEM_LIMIT = 56 * 1024 * 1024


def _params(*sem):
    return pltpu.CompilerParams(dimension_semantics=sem, vmem_limit_bytes=VMEM_LIMIT)


class _Slab:
    def __init__(self, stacked, index):
        self.stacked, self.index, self.shape = stacked, index, stacked.shape[1:]


def _operand(a):
    return a.stacked if isinstance(a, _Slab) else a


def _resident(a):
    if isinstance(a, _Slab):
        index = (a.index,) + (0,) * len(a.shape)
        return pl.BlockSpec((None, *a.shape), lambda *_: index, pipeline_mode=pl.Buffered(1))
    return pl.BlockSpec(a.shape, lambda *_: (0,) * a.ndim, pipeline_mode=pl.Buffered(1))


def _rms(x, g):
    return x * lax.rsqrt(jnp.mean(x * x, axis=-1, keepdims=True) + EPS) * g


def _log_sigmoid_pair(z):
    l = jnp.log1p(jnp.exp(-jnp.abs(z)))
    return jnp.minimum(z, 0.0) - l, jnp.minimum(-z, 0.0) - l


def _split_bf16(x):
    hi = x.astype(BF16)
    lo = (x - hi.astype(F32)).astype(BF16)
    return hi, lo


def _dot(a, b):
    return jnp.dot(a, b, preferred_element_type=F32)


def _dot_nt(a, b):
    return lax.dot_general(a, b, (((1,), (1,)), ((), ())), preferred_element_type=F32)


def _dot_tn(a, b):
    return lax.dot_general(a, b, (((0,), (0,)), ((), ())), preferred_element_type=F32)


def _pair_cols(p):
    return slice(p * LANES, (p + 1) * LANES)


class _KV:
    def __init__(self, k, v, feature_major):
        self.k, self.v, self.feature_major = k, v, feature_major
        self.n_keys = k.shape[1] if feature_major else k.shape[0]

    def scores(self, q_h, p):
        if self.feature_major:
            return _dot(q_h, self.k[_pair_cols(p), :])
        return _dot_nt(q_h, self.k[:, _pair_cols(p)])

    def values(self, p):
        return self.v[_pair_cols(p), :] if self.feature_major else self.v[:, _pair_cols(p)]

    def weighted(self, w, v_p):
        return _dot_nt(w, v_p) if self.feature_major else _dot(w, v_p)

    def head_lanes(self):
        shape = (LANES, self.n_keys) if self.feature_major else (self.n_keys, LANES)
        return lax.broadcasted_iota(jnp.int32, shape, 0 if self.feature_major else 1) < HEAD_DIM


def _split_heads(q_ref, n_pairs, tq):
    lo = lax.broadcasted_iota(jnp.int32, (tq, LANES), 1) < HEAD_DIM
    heads = []
    for p in range(n_pairs):
        q = q_ref[:, _pair_cols(p)]
        heads += [jnp.where(lo, q, jnp.zeros_like(q)), jnp.where(lo, jnp.zeros_like(q), q)]
    return heads, lo


def _emit_kv(y, wkv_ref, k_ref, v_ref, kb_ref, vb_ref, feature_major, kept_tail=None):
    if feature_major:
        kv = _dot_nt(wkv_ref[...], y)
        width = kv.shape[0] // 2
        k, v = kv[:width, :], kv[width:, :]
    else:
        kv = _dot(y, wkv_ref[...])
        width = kv.shape[1] // 2
        k, v = kv[:, :width], kv[:, width:]
    kb_ref[...] = k.astype(BF16)
    vb_ref[...] = v.astype(BF16)
    if kept_tail is None:
        k_ref[...] = k
        v_ref[...] = v
    else:
        n_tiles, n_kept = kept_tail
        tile = pl.program_id(0) % n_tiles
        tm = k.shape[1]

        @pl.when(tile >= n_tiles - n_kept)
        def _():
            cols = pl.ds(pl.multiple_of((tile - (n_tiles - n_kept)) * tm, tm), tm)
            k_ref[:, cols] = k
            v_ref[:, cols] = v


class _Riders:
    def __init__(self, stacks, layer, steps):
        self.stacks, self.layer = list(stacks), layer
        self.chunks = next(c for c in (16, 8, 4, 2, 1)
                           if c <= steps and all(w.shape[1] % (16 * c) == 0 for w in self.stacks))

    def specs(self):
        layer, last = self.layer, self.chunks - 1
        ins = [pl.BlockSpec((None, w.shape[1] // self.chunks, w.shape[2]),
                            lambda i: (layer, jnp.minimum(i, last), 0)) for w in self.stacks]
        outs = [pl.BlockSpec((w.shape[1] // self.chunks, w.shape[2]),
                             lambda i: (jnp.minimum(i, last), 0)) for w in self.stacks]
        shapes = [jax.ShapeDtypeStruct(w.shape[1:], BF16) for w in self.stacks]
        return ins, outs, shapes


def _split_rider_refs(refs, n_aliased, n_riders):
    rider_in = refs[n_aliased:n_aliased + n_riders]
    outs = refs[n_aliased + n_riders:len(refs) - n_riders]
    return outs, list(zip(rider_in, refs[len(refs) - n_riders:]))


def _cast_riders(pairs, chunks):
    if pairs:
        @pl.when(pl.program_id(0) < chunks)
        def _():
            for src, dst in pairs:
                dst[...] = src[...].astype(BF16)


def _proj_ab_kernel(x_ref, g_ref, w_ref, wkv_ref, wlr_ref, wgate_ref, bgate_ref, *refs,
                    feature_major, n_aliased, rider_chunks, n_riders):
    outs, riders = _split_rider_refs(refs, n_aliased, n_riders)
    qa_ref, ka_ref, va_ref, kab_ref, vab_ref, qb_ref, kb_ref, vb_ref, r_ref, la_ref = outs
    _cast_riders(riders, rider_chunks)
    y = _rms(x_ref[...], g_ref[...]).astype(BF16)
    _emit_kv(y, wkv_ref, ka_ref, va_ref, kab_ref, vab_ref, feature_major)
    z = _dot(y, w_ref[...])
    c = 0
    qa_ref[...] = (z[:, c:c + A_WIDTH] * (HEAD_DIM ** -0.5)).astype(BF16); c += A_WIDTH
    qb_ref[...] = z[:, c:c + B_KW]; c += B_KW
    kb_ref[...] = z[:, c:c + B_KW]; c += B_KW
    vb_ref[...] = z[:, c:c + B_VW]; c += B_VW
    r_ref[...] = z[:, c:c + B_VW]
    g_lr = _dot(y, wlr_ref[...])
    gate = _dot(g_lr.astype(BF16), wgate_ref[...]) + bgate_ref[...]
    la_ref[...] = _log_sigmoid_pair(gate)[0] * (1.0 / B_GATE_TEMP)


class _KVStack:
    def __init__(self, layer, n_layers, keep, previous=None):
        self.layer, self.n_layers, self.keep, self.previous = layer, n_layers, keep, previous


def _kv_out(m, width, tm, batch, stack):
    if batch is None:
        spec = pl.BlockSpec((tm, width), lambda i: (i, 0))
        return [spec] * 4, [jax.ShapeDtypeStruct((m, width), dt) for dt in (F32, F32, BF16, BF16)], None
    t = m // batch
    assert t % tm == 0 and stack.keep % tm == 0 and stack.keep <= t
    n_tiles, layer = t // tm, stack.layer
    copy_spec = pl.BlockSpec((None, width, tm), lambda i: (i // n_tiles, 0, i % n_tiles))
    copy_shape = jax.ShapeDtypeStruct((batch, width, t), BF16)
    if stack.keep == t:
        f32_spec = pl.BlockSpec((None, None, width, tm), lambda i: (layer, i // n_tiles, 0, i % n_tiles))
        kept_tail = None
    else:
        f32_spec = pl.BlockSpec((None, None, width, stack.keep), lambda i: (layer, i // n_tiles, 0, 0))
        kept_tail = (n_tiles, stack.keep // tm)
    f32_shape = jax.ShapeDtypeStruct((stack.n_layers, batch, width, stack.keep), F32)
    return [f32_spec, f32_spec, copy_spec, copy_spec], [f32_shape, f32_shape, copy_shape, copy_shape], kept_tail


def _aliased_stack(stack, n_inputs, first_output):
    if stack is None or stack.previous is None:
        return [], [], {}
    prev = list(stack.previous)
    specs = [pl.BlockSpec(memory_space=pl.ANY)] * len(prev)
    return prev, specs, {n_inputs + j: first_output + j for j in range(len(prev))}


def _rider_plumbing(riders):
    if riders is None:
        return [], [], [], [], 0
    ins, outs, shapes = riders.specs()
    return riders.stacks, ins, outs, shapes, riders.chunks


def _proj_ab(x, g, w_main, w_kv, w_lr, w_gate, b_gate, tm, batch=None, stack=None, riders=None):
    m, d = x.shape
    row = lambda n: pl.BlockSpec((tm, n), lambda i: (i, 0))
    params = (g, w_main, w_kv, w_lr, w_gate, b_gate)
    kv_specs, kv_shapes, kept_tail = _kv_out(m, A_WIDTH, tm, batch, stack)
    assert kept_tail is None
    prev, prev_specs, aliases = _aliased_stack(stack, 1 + len(params), 1)
    ride_ops, ride_in, ride_out, ride_shapes, ride_chunks = _rider_plumbing(riders)
    rest = [(B_KW, F32), (B_KW, F32), (B_VW, F32), (B_VW, F32), (B_KW, F32)]
    return pl.pallas_call(
        functools.partial(_proj_ab_kernel, feature_major=batch is not None, n_aliased=len(prev),
                          rider_chunks=ride_chunks, n_riders=len(ride_ops)),
        grid=(m // tm,),
        in_specs=[row(d)] + [_resident(a) for a in params] + prev_specs + ride_in,
        out_specs=[row(A_WIDTH)] + kv_specs + [row(n) for n, _ in rest] + ride_out,
        out_shape=([jax.ShapeDtypeStruct((m, A_WIDTH), BF16)] + kv_shapes
                   + [jax.ShapeDtypeStruct((m, n), dt) for n, dt in rest] + ride_shapes),
        input_output_aliases=aliases,
        compiler_params=_params("arbitrary" if ride_ops else "parallel"),
        name="proj_ab",
    )(x, *[_operand(a) for a in params], *prev, *ride_ops)


def _proj_c_kernel(x_ref, g_ref, wq_ref, wkv_ref, *refs, feature_major, n_aliased, kept_tail,
                   rider_chunks, n_riders):
    outs, riders = _split_rider_refs(refs, n_aliased, n_riders)
    q_ref, k_ref, v_ref, kb_ref, vb_ref = outs
    _cast_riders(riders, rider_chunks)
    y = _rms(x_ref[...], g_ref[...]).astype(BF16)
    _emit_kv(y, wkv_ref, k_ref, v_ref, kb_ref, vb_ref, feature_major, kept_tail)
    q_ref[...] = (_dot(y, wq_ref[...]) * (HEAD_DIM ** -0.5 * LOG2E)).astype(BF16)


def _proj_c(x, g, w_q, w_kv, tm, batch=None, stack=None, riders=None):
    m, d = x.shape
    row = lambda n: pl.BlockSpec((tm, n), lambda i: (i, 0))
    params = (g, w_q, w_kv)
    kv_specs, kv_shapes, kept_tail = _kv_out(m, C_WIDTH, tm, batch, stack)
    prev, prev_specs, aliases = _aliased_stack(stack, 1 + len(params), 1)
    ride_ops, ride_in, ride_out, ride_shapes, ride_chunks = _rider_plumbing(riders)
    return pl.pallas_call(
        functools.partial(_proj_c_kernel, feature_major=batch is not None, n_aliased=len(prev),
                          kept_tail=kept_tail, rider_chunks=ride_chunks, n_riders=len(ride_ops)),
        grid=(m // tm,),
        in_specs=[row(d)] + [_resident(a) for a in params] + prev_specs + ride_in,
        out_specs=[row(C_WIDTH)] + kv_specs + ride_out,
        out_shape=[jax.ShapeDtypeStruct((m, C_WIDTH), BF16)] + kv_shapes + ride_shapes,
        input_output_aliases=aliases,
        compiler_params=_params("arbitrary" if (kept_tail or ride_ops) else "parallel"),
        name="proj_c",
    )(x, *[_operand(a) for a in params], *prev, *ride_ops)


def _layer_tail_kernel(x_ref, *refs, gla_merge, final_norm):
    if gla_merge:
        oa_ref, ob_ref, r_ref, ggla_ref, wout_ref, *refs = refs
        ob = ob_ref[...]
        parts = []
        for h in range(B_HEADS):
            seg = ob[:, h * B_DV:(h + 1) * B_DV]
            parts.append(seg * lax.rsqrt(jnp.mean(seg * seg, axis=-1, keepdims=True) + EPS))
        r = r_ref[...]
        obn = jnp.concatenate(parts, axis=-1) * ggla_ref[...] * (r * jax.nn.sigmoid(r))
        mix = _dot(oa_ref[...], wout_ref[:A_WIDTH, :]) + _dot(obn.astype(BF16), wout_ref[A_WIDTH:, :])
    else:
        oc_ref, wout_ref, *refs = refs
        mix = _dot(oc_ref[...], wout_ref[...])
    gffn_ref, wg_ref, wu_ref, wd_ref, *refs = refs
    x = x_ref[...] + mix
    y = _rms(x, gffn_ref[...]).astype(BF16)
    h = _dot(y, wg_ref[...])
    u = _dot(y, wu_ref[...])
    a = (h * jax.nn.sigmoid(h) * u).astype(BF16)
    x = x + _dot(a, wd_ref[...])
    if final_norm:
        gfin_ref, o_ref = refs
        o_ref[...] = _rms(x, gfin_ref[...])
    else:
        (o_ref,) = refs
        o_ref[...] = x


def _layer_tail(x, mixer_out, mixer_params, g_ffn, wg, wu, wd, g_fin, tm):
    m, d = x.shape
    row = lambda a: pl.BlockSpec((tm, a.shape[1]), lambda i: (i, 0))
    resident = [*mixer_params, g_ffn, wg, wu, wd] + ([] if g_fin is None else [g_fin])
    return pl.pallas_call(
        functools.partial(_layer_tail_kernel, gla_merge=len(mixer_out) == 3, final_norm=g_fin is not None),
        grid=(m // tm,),
        in_specs=[row(x)] + [row(a) for a in mixer_out] + [_resident(a) for a in resident],
        out_specs=row(x),
        out_shape=jax.ShapeDtypeStruct((m, d), F32),
        compiler_params=_params("parallel"),
        name="layer_tail",
    )(x, *mixer_out, *[_operand(a) for a in resident])


def _interleave(stage_generators):
    results = [None] * len(stage_generators)
    live = list(range(len(stage_generators)))
    while live:
        for i in list(live):
            try:
                next(stage_generators[i])
            except StopIteration as done:
                results[i] = done.value
                live.remove(i)
    return results


def _sb_core(q_heads, lo, first_kv, first_mask, earlier_kv, n_earlier, acc_ref, c_ref, o_ref, n_pairs,
             companions=(), on_companions=None):
    tk = first_kv.n_keys
    heads = range(2 * n_pairs)
    later = (lax.broadcasted_iota(jnp.int32, (tk, tk), 0)
             > lax.broadcasted_iota(jnp.int32, (tk, tk), 1)).astype(BF16)
    c_ref[...] = jnp.zeros_like(c_ref)

    def block(kv, mask):
        z = [kv.scores(q_heads[h], h // 2) for h in heads]
        yield
        log_beta, drop, after = [], [], []
        for h in heads:
            d = jnp.maximum(z[h], 0.0) + jnp.log(1.0 + jnp.exp(-jnp.abs(z[h])))
            log_beta.append(z[h] - d)
            drop.append(d if mask is None else jnp.where(mask, d, 0.0))
        yield
        for h in heads:
            hi, lo_part = _split_bf16(drop[h])
            after.append(_dot(hi, later) + _dot(lo_part, later))
        yield
        pv = []
        for h in heads:
            c = c_ref[h]
            w = jnp.exp(log_beta[h] - after[h] - c)
            if mask is not None:
                w = jnp.where(mask, w, 0.0)
            c_ref[h] = c + after[h][:, 0:1] + drop[h][:, 0:1]
            pv.append(kv.weighted(w.astype(BF16), kv.values(h // 2)))
        yield
        out = [jnp.where(lo, pv[2 * p], pv[2 * p + 1]) for p in range(n_pairs)]
        return out[0] if n_pairs == 1 else jnp.concatenate(out, axis=-1)

    def all_dead():
        return jnp.min(c_ref[...]) > SB_DEAD

    first, *companion_results = _interleave([block(first_kv, first_mask), *companions])
    acc_ref[...] = first
    if companions:
        on_companions(companion_results)

    def cond(carry):
        n, dead = carry
        return (n < n_earlier) & jnp.logical_not(dead)

    def body(carry):
        n, _ = carry
        acc_ref[...] += _interleave([block(earlier_kv(n), None)])[0]
        return n + 1, all_dead()

    lax.while_loop(cond, body, (jnp.int32(0), all_dead()))
    o_ref[...] = acc_ref[...].astype(o_ref.dtype)


def _mixer_ab_prompt_kernel(q_ref, k_ref, v_ref, qb_ref, kb_ref, vb_ref, la_ref, o_ref, ob_ref, s_out_ref,
                            acc_ref, c_ref, carry_ref, *, tb, n_pairs, blocks_per_step):
    pairs = range(B_HEADS // 2)
    strictly_earlier = (lax.broadcasted_iota(jnp.int32, (tb, tb), 1)
                        < lax.broadcasted_iota(jnp.int32, (tb, tb), 0))

    def kv_block(j):
        keys = pl.ds(pl.multiple_of(j * tb, tb), tb)
        return _KV(k_ref[:, keys], v_ref[:, keys], True)

    def keep_states(states):
        for p in pairs:
            carry_ref[2 * p], carry_ref[2 * p + 1] = states[p]
            s_out_ref[2 * p] = states[p][0][:B_DK, :]
            s_out_ref[2 * p + 1] = states[p][1][B_DK:, :]

    @pl.when(pl.program_id(1) == 0)
    def _():
        carry_ref[...] = jnp.zeros_like(carry_ref)

    def one_block(s, carry):
        qi = pl.program_id(1) * blocks_per_step + s
        rows = pl.ds(pl.multiple_of(s * tb, tb), tb)
        gla = [_gla_pair_tile(qb_ref[rows, _pair_cols(p)], kb_ref[rows, _pair_cols(p)],
                              vb_ref[rows, _pair_v_cols(p)], la_ref[rows, _pair_cols(p)],
                              [carry_ref[2 * p], carry_ref[2 * p + 1]],
                              _gla_rows_emit(ob_ref.at[rows, :], p)) for p in pairs]
        q_heads, lo = _split_heads(q_ref.at[rows, :], n_pairs, tb)
        _sb_core(q_heads, lo, kv_block(qi), strictly_earlier, lambda n: kv_block(qi - 1 - n), qi,
                 acc_ref, c_ref, o_ref.at[rows, :], n_pairs, companions=gla, on_companions=keep_states)
        return carry

    lax.fori_loop(0, blocks_per_step, one_block, 0)


def _sb_sample_kernel(q_ref, kn_ref, vn_ref, kc_ref, vc_ref, o_ref, acc_ref, c_ref, kpad_ref, vpad_ref,
                      *, ts, tk, n_cache_blocks, n_pairs):
    q_heads, lo = _split_heads(q_ref, n_pairs, ts)
    kpad_ref[...] = jnp.zeros_like(kpad_ref)
    vpad_ref[...] = jnp.zeros_like(vpad_ref)
    kpad_ref[:ts, :] = kn_ref[...]
    vpad_ref[:ts, :] = vn_ref[...]

    def cache_block(n):
        keys = pl.ds(pl.multiple_of((n_cache_blocks - 1 - n) * tk, tk), tk)
        return _KV(kc_ref[:, keys].astype(BF16), vc_ref[:, keys].astype(BF16), True)

    strictly_earlier = (lax.broadcasted_iota(jnp.int32, (ts, tk), 1)
                        < lax.broadcasted_iota(jnp.int32, (ts, tk), 0))
    _sb_core(q_heads, lo, _KV(kpad_ref[...], vpad_ref[...], False), strictly_earlier, cache_block,
             n_cache_blocks, acc_ref, c_ref, o_ref, n_pairs)


def _mixer_ab_prompt(q, k, v, qb, kb, vb, la):
    b, t, _ = q.shape
    tb = ATT_BLOCK
    assert t % tb == 0 and tb % GLA_CHUNK == 0 and tb <= GLA_MAX_TILE
    n_pairs = A_WIDTH // LANES
    per_step = math.gcd(t // tb, ATT_BLOCKS_PER_STEP)
    rows = lambda width: pl.BlockSpec((None, per_step * tb, width), lambda bi, qi: (bi, qi, 0))
    kv_spec = pl.BlockSpec((None, A_WIDTH, t), lambda bi, qi: (bi, 0, 0))
    state = (B_HEADS, B_DK, B_DV)
    return pl.pallas_call(
        functools.partial(_mixer_ab_prompt_kernel, tb=tb, n_pairs=n_pairs, blocks_per_step=per_step),
        grid=(b, t // (per_step * tb)),
        in_specs=[rows(A_WIDTH), kv_spec, kv_spec, rows(B_KW), rows(B_KW), rows(B_VW), rows(B_KW)],
        out_specs=[rows(A_WIDTH), rows(B_VW), pl.BlockSpec((None, *state), lambda bi, qi: (bi, 0, 0, 0))],
        out_shape=[jax.ShapeDtypeStruct(q.shape, BF16), jax.ShapeDtypeStruct(vb.shape, F32),
                   jax.ShapeDtypeStruct((b, *state), F32)],
        scratch_shapes=[pltpu.VMEM((tb, A_WIDTH), F32), pltpu.VMEM((2 * n_pairs, tb, 1), F32),
                        pltpu.VMEM((B_HEADS, LANES, B_DV), F32)],
        compiler_params=_params("parallel", "arbitrary"),
        name="mixer_ab",
    )(q, k, v, qb, kb, vb, la)


def _sb_attention_sample(q, k_new, v_new, k_cache, v_cache, layer):
    b, ts, _ = q.shape
    past = k_cache.shape[3]
    tk = ATT_BLOCK
    assert past % tk == 0 and ts <= tk
    n_pairs = SB_PAIRS_PER_STEP
    w = n_pairs * LANES
    new_spec = pl.BlockSpec((None, ts, w), lambda bi, hp: (bi, 0, hp))
    cache_spec = pl.BlockSpec((None, None, w, past), lambda bi, hp: (layer, bi, hp, 0))
    return pl.pallas_call(
        functools.partial(_sb_sample_kernel, ts=ts, tk=tk, n_cache_blocks=past // tk, n_pairs=n_pairs),
        grid=(b, A_WIDTH // w),
        in_specs=[new_spec, new_spec, new_spec, cache_spec, cache_spec],
        out_specs=new_spec,
        out_shape=jax.ShapeDtypeStruct(q.shape, BF16),
        scratch_shapes=[pltpu.VMEM((ts, w), F32), pltpu.VMEM((2 * n_pairs, ts, 1), F32),
                        pltpu.VMEM((tk, w), BF16), pltpu.VMEM((tk, w), BF16)],
        compiler_params=_params("parallel", "parallel"),
        name="sb_attention_sample",
    )(q, k_new, v_new, k_cache, v_cache)


def _band_core(q_heads, lo, kv, pens, bias, o_ref, n_pairs):
    heads = range(2 * n_pairs)
    blocks = range(len(kv))
    z = [[kv[i].scores(q_heads[h], h // 2) + bias(h, i) for i in blocks] for h in heads]
    acc = []
    for h in heads:
        m = None
        for i in blocks:
            mi = jnp.max(z[h][i], axis=-1, keepdims=True)
            if pens[i] is not None:
                mi = mi + pens[i]
            m = mi if m is None else jnp.maximum(m, mi)
        a = None
        for i in blocks:
            shift = m if pens[i] is None else m - pens[i]
            p = jnp.exp2(z[h][i] - shift).astype(BF16)
            v = kv[i].values(h // 2)
            ones = jnp.ones_like(v)
            first = kv[i].head_lanes()
            v = jnp.where(first, v, ones) if h % 2 == 0 else jnp.where(first, ones, v)
            pv = kv[i].weighted(p, v)
            a = pv if a is None else a + pv
        acc.append(a)
    for p in range(n_pairs):
        a0, a1 = acc[2 * p], acc[2 * p + 1]
        o_ref[:, _pair_cols(p)] = jnp.where(lo, a0 / pltpu.roll(a0, HEAD_DIM, axis=1),
                                            a1 / pltpu.roll(a1, HEAD_DIM, axis=1)).astype(o_ref.dtype)


def _band_prompt_kernel(q_ref, k_ref, v_ref, bias_ref, o_ref, *, tq, tk, n_pairs, blocks_per_step):
    def one_block(s, carry):
        qi = pl.program_id(2) * blocks_per_step + s
        rows = pl.ds(pl.multiple_of(s * tq, tq), tq)
        q_heads, lo = _split_heads(q_ref.at[rows, :], n_pairs, tq)
        kv, pens = [], []
        for dj in range(3):
            j = qi - dj
            pens.append(None if dj == 0 else jnp.where(j >= 0, 0.0, NEG_BIG).astype(F32))
            keys = pl.ds(pl.multiple_of(jnp.maximum(j, 0) * tk, tk), tk)
            kv.append(_KV(k_ref[:, keys], v_ref[:, keys], True))
        _band_core(q_heads, lo, kv, pens, lambda h, i: bias_ref[h, i], o_ref.at[rows, :], n_pairs)
        return carry

    lax.fori_loop(0, blocks_per_step, one_block, 0)


def _band_sample_kernel(q_ref, kn_ref, vn_ref, kc_ref, vc_ref, bias_ref, o_ref, kpad_ref, vpad_ref,
                        *, ts, tk, n_cache_blocks, n_pairs):
    q_heads, lo = _split_heads(q_ref, n_pairs, ts)
    kpad_ref[...] = jnp.zeros_like(kpad_ref)
    vpad_ref[...] = jnp.zeros_like(vpad_ref)
    kpad_ref[:ts, :] = kn_ref[...]
    vpad_ref[:ts, :] = vn_ref[...]
    kv = [_KV(kpad_ref[...], vpad_ref[...], False)]
    for dj in range(1, n_cache_blocks + 1):
        keys = slice((n_cache_blocks - dj) * tk, (n_cache_blocks - dj + 1) * tk)
        kv.append(_KV(kc_ref[:, keys].astype(BF16), vc_ref[:, keys].astype(BF16), True))
    is_new_key = lax.broadcasted_iota(jnp.int32, (ts, tk), 1) < ts

    def bias(h, i):
        return jnp.where(is_new_key, bias_ref[h, 0], NEG_BIG) if i == 0 else bias_ref[h, i]

    _band_core(q_heads, lo, kv, [None] * len(kv), bias, o_ref, n_pairs)


def _band_attention_prompt(q, k, v, bias):
    b, t, _ = q.shape
    tq = tk = ATT_BLOCK
    assert t % tk == 0
    n_pairs = BAND_PAIRS_PER_STEP
    w = n_pairs * LANES
    per_step = math.gcd(t // tq, ATT_BLOCKS_PER_STEP)
    kern = functools.partial(_band_prompt_kernel, tq=tq, tk=tk, n_pairs=n_pairs, blocks_per_step=per_step)
    kv_spec = pl.BlockSpec((None, w, t), lambda bi, hp, qi: (bi, hp, 0))
    q_spec = pl.BlockSpec((None, per_step * tq, w), lambda bi, hp, qi: (bi, qi, hp))
    bias_spec = pl.BlockSpec((2 * n_pairs, 3, tq, tk), lambda bi, hp, qi: (hp, 0, 0, 0))
    return pl.pallas_call(
        kern,
        grid=(b, C_WIDTH // w, t // (per_step * tq)),
        in_specs=[q_spec, kv_spec, kv_spec, bias_spec],
        out_specs=q_spec,
        out_shape=jax.ShapeDtypeStruct(q.shape, BF16),
        compiler_params=_params("parallel", "parallel", "parallel"),
        name="band_attention",
    )(q, k, v, bias)


def _band_attention_sample(q, k_new, v_new, k_cache, v_cache, layer, bias):
    b, ts, _ = q.shape
    wc = k_cache.shape[3]
    tk = ATT_BLOCK
    n_cache_blocks = min(wc // tk, 2)
    assert wc % (n_cache_blocks * tk) == 0 and ts <= tk
    n_pairs = BAND_PAIRS_PER_STEP
    w = n_pairs * LANES
    kern = functools.partial(_band_sample_kernel, ts=ts, tk=tk, n_cache_blocks=n_cache_blocks,
                             n_pairs=n_pairs)
    new_spec = pl.BlockSpec((None, ts, w), lambda bi, hp: (bi, 0, hp))
    cache_rows = n_cache_blocks * tk
    cache_spec = pl.BlockSpec((None, None, w, cache_rows),
                              lambda bi, hp: (layer, bi, hp, wc // cache_rows - 1))
    bias_spec = pl.BlockSpec((2 * n_pairs, 1 + n_cache_blocks, ts, tk), lambda bi, hp: (hp, 0, 0, 0))
    return pl.pallas_call(
        kern,
        grid=(b, C_WIDTH // w),
        in_specs=[new_spec, new_spec, new_spec, cache_spec, cache_spec, bias_spec],
        out_specs=new_spec,
        out_shape=jax.ShapeDtypeStruct(q.shape, BF16),
        scratch_shapes=[pltpu.VMEM((tk, w), BF16), pltpu.VMEM((tk, w), BF16)],
        compiler_params=_params("parallel", "parallel"),
        name="band_attention_sample",
    )(q, k_new, v_new, k_cache, v_cache, bias)


def _band_bias_kernel(g_ref, o_ref):
    rows = tk = ATT_BLOCK
    q_chunk = lax.broadcasted_iota(jnp.int32, (rows, tk), 0) // CHUNK
    k_chunk = lax.broadcasted_iota(jnp.int32, (rows, tk), 1) // CHUNK
    for dj in range(3):
        g = jnp.broadcast_to(g_ref[0, dj], (rows, 2 * tk))
        tile = pltpu.roll(g, 0, axis=1, stride=1, stride_axis=0)[:, :tk]
        diff = dj * (tk // CHUNK) + q_chunk - k_chunk
        seen = (diff >= 0) & (diff <= C_LEFT_CHUNKS)
        o_ref[0, dj] = jnp.where(seen, tile * LOG2E, NEG_BIG)


def _band_bias(rel_table):
    rows = tk = ATT_BLOCK
    c = jnp.arange(2 * tk, dtype=jnp.int32)
    u = jnp.where(c <= tk, -c, 2 * tk - c)
    idx = jnp.clip(jnp.arange(3, dtype=jnp.int32)[:, None] * tk + u[None, :], REL_MIN, REL_MAX) - REL_MIN
    g = rel_table[:, idx].astype(F32).reshape(C_HEADS, 3, 1, 2 * tk)
    return pl.pallas_call(
        _band_bias_kernel,
        grid=(C_HEADS,),
        in_specs=[pl.BlockSpec((1, 3, 1, 2 * tk), lambda h: (h, 0, 0, 0))],
        out_specs=pl.BlockSpec((1, 3, rows, tk), lambda h: (h, 0, 0, 0)),
        out_shape=jax.ShapeDtypeStruct((C_HEADS, 3, rows, tk), F32),
        compiler_params=_params("parallel"),
        name="band_bias",
    )(g)


def _gla_pair_tile(q, k, v, la, states, emit):
    L = GLA_CHUNK
    chunks = range(q.shape[0] // L)
    row = lax.broadcasted_iota(jnp.int32, (L, L), 0)
    colm = lax.broadcasted_iota(jnp.int32, (L, L), 1)
    tri = (colm <= row).astype(BF16)
    causal = colm <= row
    lane = lax.broadcasted_iota(jnp.int32, (L, LANES), 1)
    sub = lax.broadcasted_iota(jnp.int32, (LANES, B_DV), 0)
    mine = [(lane >= h * B_DK) & (lane < (h + 1) * B_DK) for h in range(2)]
    mine_rows = [(sub >= h * B_DK) & (sub < (h + 1) * B_DK) for h in range(2)]
    rows = [slice(c * L, (c + 1) * L) for c in chunks]

    b = []
    for r in rows:
        g_hi, g_lo = _split_bf16(la[r, :])
        b.append(_dot(tri, g_hi) + _dot(tri, g_lo))
    yield
    qg, qg_h, kg, kd_t, decay = [], [], [], [], []
    for c, r in zip(chunks, rows):
        qg_c = q[r, :] * (B_DK ** -0.5) * jnp.exp(b[c])
        qg.append(qg_c.astype(BF16))
        qg_h.append([jnp.where(mine[h], qg_c, 0.0).astype(BF16) for h in range(2)])
        kg.append((k[r, :] * jnp.exp(-b[c])).astype(BF16))
        b_t = b[c].T
        b_last = b_t[:, L - 1:L]
        kd_t.append((k[r, :].T * jnp.exp(b_last - b_t)).astype(BF16))
        decay.append(jnp.exp(b_last))
    yield
    att =[[jnp.where(causal, _dot_nt(qg_h[c][h], kg[c]), 0.0).astype(BF16) for h in range(2)]
           for c in chunks]
    yield
    o_intra, own = [], []
    for c, r in zip(chunks, rows):
        v_h = [v[r, h * B_DV:(h + 1) * B_DV].astype(BF16) for h in range(2)]
        o_intra.append([_dot(att[c][h], v_h[h]) for h in range(2)])
        own.append([jnp.where(mine_rows[h], _dot(kd_t[c], v_h[h]), 0.0) for h in range(2)])

    yield
    states = list(states)
    start = []
    for c in chunks:
        start.append([s.astype(BF16) for s in states])
        states = [decay[c] * states[h] + own[c][h] for h in range(2)]
    yield
    for c in chunks:
        for h in range(2):
            emit(c, h, o_intra[c][h] + _dot(qg[c], start[c][h]))
    return states


def _pair_states(s0_ref, p):
    zeros_state = jnp.zeros((B_DK, B_DV), F32)
    return [jnp.concatenate([s0_ref[2 * p], zeros_state], axis=0),
            jnp.concatenate([zeros_state, s0_ref[2 * p + 1]], axis=0)]


def _pair_v_cols(p):
    return slice(2 * p * B_DV, 2 * (p + 1) * B_DV)


def _gla_rows_emit(o_ref, p):
    def emit(c, h, o):
        o_ref[c * GLA_CHUNK:(c + 1) * GLA_CHUNK, (2 * p + h) * B_DV:(2 * p + h + 1) * B_DV] = o
    return emit


def _gla_kernel(q_ref, k_ref, v_ref, la_ref, s0_ref, o_ref, s_out_ref):
    pairs = range(B_HEADS // 2)
    states = _interleave([
        _gla_pair_tile(q_ref[:, _pair_cols(p)], k_ref[:, _pair_cols(p)], v_ref[:, _pair_v_cols(p)],
                       la_ref[:, _pair_cols(p)], _pair_states(s0_ref, p), _gla_rows_emit(o_ref, p))
        for p in pairs])
    for p in pairs:
        s_out_ref[2 * p] = states[p][0][:B_DK, :]
        s_out_ref[2 * p + 1] = states[p][1][B_DK:, :]


def _gla(q, k, v, la, s0):
    b, t, _ = q.shape
    assert t % GLA_CHUNK == 0 and t <= GLA_MAX_TILE
    qk_spec = pl.BlockSpec((None, t, B_KW), lambda bi: (bi, 0, 0))
    v_spec = pl.BlockSpec((None, t, B_VW), lambda bi: (bi, 0, 0))
    s_spec = pl.BlockSpec((None, B_HEADS, B_DK, B_DV), lambda bi: (bi, 0, 0, 0))
    return pl.pallas_call(
        _gla_kernel,
        grid=(b,),
        in_specs=[qk_spec, qk_spec, v_spec, qk_spec, s_spec],
        out_specs=[v_spec, s_spec],
        out_shape=[jax.ShapeDtypeStruct(v.shape, F32), jax.ShapeDtypeStruct(s0.shape, F32)],
        compiler_params=_params("parallel"),
        name="gla",
    )(q, k, v, la, s0)


def _pad_rows(x, n):
    return jnp.pad(x, ((0, 0), (0, n - x.shape[1]), (0, 0)))


def _heads_last(x, heads):
    n, b, _, s = x.shape
    return jnp.transpose(x.reshape(n, b, heads, HEAD_DIM, s), (0, 1, 4, 2, 3))


def _feature_major(cache):
    n, b, s, heads, hd = cache.shape
    return jnp.transpose(cache, (0, 1, 3, 4, 2)).reshape(n, b, heads * hd, s)


def _row_tile(m):
    for tm in (512, 256, 128, 64, 32, 16, 8):
        if m % tm == 0:
            return tm
    raise ValueError(f"token count {m} is not a multiple of 8")


def kernel(x_prompt, x_sample, cache_a_k, cache_a_v, state_b, cache_c_k, cache_c_v, norm_mix_g, norm_ffn_g, w_in_ab, w_gate_b, b_gate_b, norm_gla_g, w_out_ab, w_qkv_c, rel_bias_c, w_out_c, w_ffn_gate, w_ffn_up, w_ffn_down, norm_final_g):
    bp, tp, d = x_prompt.shape
    bs, ts, _ = x_sample.shape
    depth = norm_mix_g.shape[0]
    past = cache_a_k.shape[2]
    wc = cache_c_k.shape[2]
    assert tp % ATT_BLOCK == 0 and past % ATT_BLOCK == 0 and wc % ATT_BLOCK == 0
    assert ts <= GLA_CHUNK and ts % 8 == 0
    mp, ms = bp * tp, bs * ts
    tmp, tms = _row_tile(tp), _row_tile(ms)
    xp = x_prompt.reshape(mp, d)
    xs = x_sample.reshape(ms, d)
    row2 = lambda a: a.reshape(1, -1)

    a_ks, a_vs, b_sp, b_ss, c_ks, c_vs = [], [], [], [], [], []
    a_kv_prompt = c_kv_prompt = None
    n_ab, n_c = (depth + 1) // 2, depth // 2
    keep = min(C_LEFT_CHUNKS * CHUNK, tp)

    kv0, kv1 = A_WIDTH, 3 * A_WIDTH
    o = 3 * A_WIDTH + 2 * B_KW + B_VW
    w_main_all = jnp.concatenate([w_in_ab[:, :, :kv0], w_in_ab[:, :, kv1:o], w_in_ab[:, :, o + B_GATE_RANK:]],
                                 axis=2).astype(BF16)
    w_kv_ab_all = w_in_ab[:, :, kv0:kv1].astype(BF16)
    w_kv_ab_t_all = jnp.swapaxes(w_in_ab, 1, 2)[:, kv0:kv1].astype(BF16)
    w_lr_all = jnp.pad(w_in_ab[:, :, o:o + B_GATE_RANK],
                       ((0, 0), (0, 0), (0, LANES - B_GATE_RANK))).astype(BF16)
    w_gate_all = jnp.pad(w_gate_b, ((0, 0), (0, LANES - B_GATE_RANK), (0, 0))).astype(BF16)
    w_out_ab_all = w_out_ab.astype(BF16)
    w_q_all = w_qkv_c[:, :, :C_WIDTH].astype(BF16)
    w_kv_c_all = w_qkv_c[:, :, C_WIDTH:].astype(BF16)
    w_kv_c_t_all = jnp.swapaxes(w_qkv_c[:, :, C_WIDTH:], 1, 2).astype(BF16)
    w_out_c_all = w_out_c.astype(BF16)
    cache_a_k_fm, cache_a_v_fm, cache_c_k_fm, cache_c_v_fm = (
        _feature_major(c) for c in (cache_a_k, cache_a_v, cache_c_k, cache_c_v))

    for layer in range(depth):
        i = layer // 2
        g_mix = row2(norm_mix_g[layer])
        riders = _Riders((w_ffn_gate, w_ffn_up, w_ffn_down), layer, mp // tmp)
        ffn_ends = (row2(norm_ffn_g[layer]), row2(norm_final_g) if layer == depth - 1 else None)
        if layer % 2 == 0:
            w_main, w_kv, w_kv_t, w_lr, w_gate, w_out = (
                _Slab(w, i) for w in (w_main_all, w_kv_ab_all, w_kv_ab_t_all, w_lr_all, w_gate_all,
                                      w_out_ab_all))
            b_gate = row2(b_gate_b[i])
            g_gla = row2(norm_gla_g[i])

            qa, ka, va, kab, vab, qb, kb, vb, r, la, *ffn_w = _proj_ab(
                xp, g_mix, w_main, w_kv_t, w_lr, w_gate, b_gate, tmp, batch=bp,
                stack=_KVStack(i, n_ab, tp, a_kv_prompt), riders=riders)
            ffn = (ffn_ends[0], *ffn_w, ffn_ends[1])
            a_kv_prompt = (ka, va)
            sh = lambda a: a.reshape(bp, tp, -1)
            oa, ob, sbp = _mixer_ab_prompt(sh(qa), kab, vab, sh(qb), sh(kb), sh(vb), sh(la))
            xp = _layer_tail(xp, (oa.reshape(mp, -1), ob.reshape(mp, -1), r), (g_gla, w_out), *ffn, tmp)
            b_sp.append(sbp)

            qa, ka, va, kab, vab, qb, kb, vb, r, la = _proj_ab(
                xs, g_mix, w_main, w_kv, w_lr, w_gate, b_gate, tms)
            sh = lambda a: a.reshape(bs, ts, -1)
            oa = _sb_attention_sample(sh(qa), sh(kab), sh(vab), cache_a_k_fm, cache_a_v_fm, i)
            pad_t = lambda a: _pad_rows(sh(a), GLA_CHUNK)
            ob, sbs = _gla(pad_t(qb), pad_t(kb), pad_t(vb), pad_t(la), state_b[i])
            xs = _layer_tail(xs, (oa.reshape(ms, -1), ob[:, :ts].reshape(ms, -1), r), (g_gla, w_out), *ffn, tms)
            a_ks.append(ka.reshape(bs, ts, A_HEADS, HEAD_DIM))
            a_vs.append(va.reshape(bs, ts, A_HEADS, HEAD_DIM))
            b_ss.append(sbs)
        else:
            w_q, w_kv, w_kv_t, w_out = (
                _Slab(w, i) for w in (w_q_all, w_kv_c_all, w_kv_c_t_all, w_out_c_all))

            q, k, v, kb16, vb16, *ffn_w = _proj_c(xp, g_mix, w_q, w_kv_t, tmp, batch=bp,
                                                  stack=_KVStack(i, n_c, keep, c_kv_prompt), riders=riders)
            ffn = (ffn_ends[0], *ffn_w, ffn_ends[1])
            c_kv_prompt = (k, v)
            bias = _band_bias(rel_bias_c[i])
            oc = _band_attention_prompt(q.reshape(bp, tp, -1), kb16, vb16, bias)
            xp = _layer_tail(xp, (oc.reshape(mp, -1),), (w_out,), *ffn, tmp)

            q, k, v, kb16, vb16 = _proj_c(xs, g_mix, w_q, w_kv, tms)
            sh = lambda a: a.reshape(bs, ts, -1)
            oc = _band_attention_sample(sh(q), sh(kb16), sh(vb16), cache_c_k_fm, cache_c_v_fm, i, bias)
            xs = _layer_tail(xs, (oc.reshape(ms, -1),), (w_out,), *ffn, tms)
            c_ks.append(k.reshape(bs, ts, C_HEADS, HEAD_DIM))
            c_vs.append(v.reshape(bs, ts, C_HEADS, HEAD_DIM))

    y_prompt = xp.reshape(bp, tp, d)
    y_sample = xs.reshape(bs, ts, d)
    a_kp, a_vp = (_heads_last(a, A_HEADS) for a in a_kv_prompt)
    c_kp, c_vp = (_heads_last(a, C_HEADS) for a in c_kv_prompt)
    return (y_prompt, y_sample, a_kp, a_vp, jnp.stack(a_ks), jnp.stack(a_vs),
            jnp.stack(b_sp), jnp.stack(b_ss), c_kp, c_vp, jnp.stack(c_ks), jnp.stack(c_vs))
```

```python
import functools
import math

import jax
import jax.numpy as jnp
from jax import lax
from jax.experimental import pallas as pl
from jax.experimental.pallas import tpu as pltpu

F32 = jnp.float32
BF16 = jnp.bfloat16

EPS = 1e-6
HEAD_DIM = 64
LANES = 128
A_HEADS = 8
A_WIDTH = A_HEADS * HEAD_DIM
B_HEADS = 4
B_DK = 64
B_DV = 128
B_KW = B_HEADS * B_DK
B_VW = B_HEADS * B_DV
B_GATE_RANK = 16
B_GATE_TEMP = 16.0
GLA_CHUNK = 64
C_HEADS = 16
C_WIDTH = C_HEADS * HEAD_DIM
CHUNK = 64
C_LEFT_CHUNKS = 8
REL_MIN = -(CHUNK - 1)
REL_MAX = 128
ATT_BLOCK = 256
NEG_BIG = -1e30
LOG2E = 1.4426950408889634
SB_DEAD = 104.0
SB_PAIRS_PER_STEP = 4
BAND_PAIRS_PER_STEP = 4
ATT_BLOCKS_PER_STEP = 4
GLA_MAX_TILE = 512
VMEM_LIMIT = 56 * 1024 * 1024


def _params(*sem):
    return pltpu.CompilerParams(dimension_semantics=sem, vmem_limit_bytes=VMEM_LIMIT)


class _Slab:
    def __init__(self, stacked, index):
        self.stacked, self.index, self.shape = stacked, index, stacked.shape[1:]


def _operand(a):
    return a.stacked if isinstance(a, _Slab) else a


def _resident(a):
    if isinstance(a, _Slab):
        index = (a.index,) + (0,) * len(a.shape)
        return pl.BlockSpec((None, *a.shape), lambda *_: index, pipeline_mode=pl.Buffered(1))
    return pl.BlockSpec(a.shape, lambda *_: (0,) * a.ndim, pipeline_mode=pl.Buffered(1))


def _rms(x, g):
    return x * lax.rsqrt(jnp.mean(x * x, axis=-1, keepdims=True) + EPS) * g


def _log_sigmoid_pair(z):
    l = jnp.log1p(jnp.exp(-jnp.abs(z)))
    return jnp.minimum(z, 0.0) - l, jnp.minimum(-z, 0.0) - l


def _split_bf16(x):
    hi = x.astype(BF16)
    lo = (x - hi.astype(F32)).astype(BF16)
    return hi, lo


def _dot(a, b):
    return jnp.dot(a, b, preferred_element_type=F32)


def _dot_nt(a, b):
    return lax.dot_general(a, b, (((1,), (1,)), ((), ())), preferred_element_type=F32)


def _dot_tn(a, b):
    return lax.dot_general(a, b, (((0,), (0,)), ((), ())), preferred_element_type=F32)


def _pair_cols(p):
    return slice(p * LANES, (p + 1) * LANES)


class _KV:
    def __init__(self, k, v, feature_major):
        self.k, self.v, self.feature_major = k, v, feature_major
        self.n_keys = k.shape[1] if feature_major else k.shape[0]

    def scores(self, q_h, p):
        if self.feature_major:
            return _dot(q_h, self.k[_pair_cols(p), :])
        return _dot_nt(q_h, self.k[:, _pair_cols(p)])

    def values(self, p):
        return self.v[_pair_cols(p), :] if self.feature_major else self.v[:, _pair_cols(p)]

    def weighted(self, w, v_p):
        return _dot_nt(w, v_p) if self.feature_major else _dot(w, v_p)

    def head_lanes(self):
        shape = (LANES, self.n_keys) if self.feature_major else (self.n_keys, LANES)
        return lax.broadcasted_iota(jnp.int32, shape, 0 if self.feature_major else 1) < HEAD_DIM


def _split_heads(q_ref, n_pairs, tq):
    lo = lax.broadcasted_iota(jnp.int32, (tq, LANES), 1) < HEAD_DIM
    heads = []
    for p in range(n_pairs):
        q = q_ref[:, _pair_cols(p)]
        heads += [jnp.where(lo, q, jnp.zeros_like(q)), jnp.where(lo, jnp.zeros_like(q), q)]
    return heads, lo


def _emit_kv(y, wkv_ref, k_ref, v_ref, kb_ref, vb_ref, feature_major, kept_tail=None):
    if feature_major:
        kv = _dot_nt(wkv_ref[...], y)
        width = kv.shape[0] // 2
        k, v = kv[:width, :], kv[width:, :]
    else:
        kv = _dot(y, wkv_ref[...])
        width = kv.shape[1] // 2
        k, v = kv[:, :width], kv[:, width:]
    kb_ref[...] = k.astype(BF16)
    vb_ref[...] = v.astype(BF16)
    if kept_tail is None:
        k_ref[...] = k
        v_ref[...] = v
    else:
        n_tiles, n_kept = kept_tail
        tile = pl.program_id(0) % n_tiles
        tm = k.shape[1]
        cols = pl.ds(pl.multiple_of(jnp.maximum(tile - (n_tiles - n_kept), 0) * tm, tm), tm)
        k_ref[:, cols] = k
        v_ref[:, cols] = v


class _Riders:
    def __init__(self, stacks, layer, steps):
        self.stacks, self.layer = list(stacks), layer
        self.chunks = next(c for c in (16, 8, 4, 2, 1)
                           if c <= steps and all(w.shape[1] % (16 * c) == 0 for w in self.stacks))

    def specs(self):
        layer, last = self.layer, self.chunks - 1
        ins = [pl.BlockSpec((None, w.shape[1] // self.chunks, w.shape[2]),
                            lambda i: (layer, jnp.minimum(i, last), 0)) for w in self.stacks]
        outs = [pl.BlockSpec((w.shape[1] // self.chunks, w.shape[2]),
                             lambda i: (jnp.minimum(i, last), 0)) for w in self.stacks]
        shapes = [jax.ShapeDtypeStruct(w.shape[1:], BF16) for w in self.stacks]
        return ins, outs, shapes


def _split_rider_refs(refs, n_aliased, n_riders):
    rider_in = refs[n_aliased:n_aliased + n_riders]
    outs = refs[n_aliased + n_riders:len(refs) - n_riders]
    return outs, list(zip(rider_in, refs[len(refs) - n_riders:]))


def _cast_riders(pairs):
    for src, dst in pairs:
        dst[...] = src[...].astype(BF16)


def _proj_ab_kernel(x_ref, g_ref, w_ref, wkv_ref, wlr_ref, wgate_ref, bgate_ref, *refs,
                    feature_major, n_aliased, n_riders):
    outs, riders = _split_rider_refs(refs, n_aliased, n_riders)
    qa_ref, ka_ref, va_ref, kab_ref, vab_ref, qb_ref, kb_ref, vb_ref, r_ref, la_ref = outs
    _cast_riders(riders)
    y = _rms(x_ref[...], g_ref[...]).astype(BF16)
    _emit_kv(y, wkv_ref, ka_ref, va_ref, kab_ref, vab_ref, feature_major)
    z = _dot(y, w_ref[...])
    c = 0
    qa_ref[...] = (z[:, c:c + A_WIDTH] * (HEAD_DIM ** -0.5)).astype(BF16); c += A_WIDTH
    qb_ref[...] = z[:, c:c + B_KW]; c += B_KW
    kb_ref[...] = z[:, c:c + B_KW]; c += B_KW
    vb_ref[...] = z[:, c:c + B_VW]; c += B_VW
    r_ref[...] = z[:, c:c + B_VW]
    g_lr = _dot(y, wlr_ref[...])
    gate = _dot(g_lr.astype(BF16), wgate_ref[...]) + bgate_ref[...]
    la_ref[...] = _log_sigmoid_pair(gate)[0] * (1.0 / B_GATE_TEMP)


class _KVStack:
    def __init__(self, layer, n_layers, keep, previous=None):
        self.layer, self.n_layers, self.keep, self.previous = layer, n_layers, keep, previous


def _kv_out(m, width, tm, batch, stack):
    if batch is None:
        spec = pl.BlockSpec((tm, width), lambda i: (i, 0))
        return [spec] * 4, [jax.ShapeDtypeStruct((m, width), dt) for dt in (F32, F32, BF16, BF16)], None
    t = m // batch
    assert t % tm == 0 and stack.keep % tm == 0 and stack.keep <= t
    n_tiles, layer = t // tm, stack.layer
    copy_spec = pl.BlockSpec((None, width, tm), lambda i: (i // n_tiles, 0, i % n_tiles))
    copy_shape = jax.ShapeDtypeStruct((batch, width, t), BF16)
    if stack.keep == t:
        f32_spec = pl.BlockSpec((None, None, width, tm), lambda i: (layer, i // n_tiles, 0, i % n_tiles))
        kept_tail = None
    else:
        f32_spec = pl.BlockSpec((None, None, width, stack.keep), lambda i: (layer, i // n_tiles, 0, 0))
        kept_tail = (n_tiles, stack.keep // tm)
    f32_shape = jax.ShapeDtypeStruct((stack.n_layers, batch, width, stack.keep), F32)
    return [f32_spec, f32_spec, copy_spec, copy_spec], [f32_shape, f32_shape, copy_shape, copy_shape], kept_tail


def _aliased_stack(stack, n_inputs, first_output):
    if stack is None or stack.previous is None:
        return [], [], {}
    prev = list(stack.previous)
    specs = [pl.BlockSpec(memory_space=pl.ANY)] * len(prev)
    return prev, specs, {n_inputs + j: first_output + j for j in range(len(prev))}


def _rider_plumbing(riders):
    if riders is None:
        return [], [], [], []
    return (riders.stacks, *riders.specs())


def _proj_ab(x, g, w_main, w_kv, w_lr, w_gate, b_gate, tm, batch=None, stack=None, riders=None):
    m, d = x.shape
    row = lambda n: pl.BlockSpec((tm, n), lambda i: (i, 0))
    params = (g, w_main, w_kv, w_lr, w_gate, b_gate)
    kv_specs, kv_shapes, kept_tail = _kv_out(m, A_WIDTH, tm, batch, stack)
    assert kept_tail is None
    prev, prev_specs, aliases = _aliased_stack(stack, 1 + len(params), 1)
    ride_ops, ride_in, ride_out, ride_shapes = _rider_plumbing(riders)
    rest = [(B_KW, F32), (B_KW, F32), (B_VW, F32), (B_VW, F32), (B_KW, F32)]
    return pl.pallas_call(
        functools.partial(_proj_ab_kernel, feature_major=batch is not None, n_aliased=len(prev),
                          n_riders=len(ride_ops)),
        grid=(m // tm,),
        in_specs=[row(d)] + [_resident(a) for a in params] + prev_specs + ride_in,
        out_specs=[row(A_WIDTH)] + kv_specs + [row(n) for n, _ in rest] + ride_out,
        out_shape=([jax.ShapeDtypeStruct((m, A_WIDTH), BF16)] + kv_shapes
                   + [jax.ShapeDtypeStruct((m, n), dt) for n, dt in rest] + ride_shapes),
        input_output_aliases=aliases,
        compiler_params=_params("arbitrary" if ride_ops else "parallel"),
        name="proj_ab",
    )(x, *[_operand(a) for a in params], *prev, *ride_ops)


def _proj_c_kernel(x_ref, g_ref, wq_ref, wkv_ref, *refs, feature_major, n_aliased, kept_tail,
                   n_riders):
    outs, riders = _split_rider_refs(refs, n_aliased, n_riders)
    q_ref, k_ref, v_ref, kb_ref, vb_ref = outs
    _cast_riders(riders)
    y = _rms(x_ref[...], g_ref[...]).astype(BF16)
    _emit_kv(y, wkv_ref, k_ref, v_ref, kb_ref, vb_ref, feature_major, kept_tail)
    q_ref[...] = (_dot(y, wq_ref[...]) * (HEAD_DIM ** -0.5 * LOG2E)).astype(BF16)


def _proj_c(x, g, w_q, w_kv, tm, batch=None, stack=None, riders=None):
    m, d = x.shape
    row = lambda n: pl.BlockSpec((tm, n), lambda i: (i, 0))
    params = (g, w_q, w_kv)
    kv_specs, kv_shapes, kept_tail = _kv_out(m, C_WIDTH, tm, batch, stack)
    prev, prev_specs, aliases = _aliased_stack(stack, 1 + len(params), 1)
    ride_ops, ride_in, ride_out, ride_shapes = _rider_plumbing(riders)
    return pl.pallas_call(
        functools.partial(_proj_c_kernel, feature_major=batch is not None, n_aliased=len(prev),
                          kept_tail=kept_tail, n_riders=len(ride_ops)),
        grid=(m // tm,),
        in_specs=[row(d)] + [_resident(a) for a in params] + prev_specs + ride_in,
        out_specs=[row(C_WIDTH)] + kv_specs + ride_out,
        out_shape=[jax.ShapeDtypeStruct((m, C_WIDTH), BF16)] + kv_shapes + ride_shapes,
        input_output_aliases=aliases,
        compiler_params=_params("arbitrary" if (kept_tail or ride_ops) else "parallel"),
        name="proj_c",
    )(x, *[_operand(a) for a in params], *prev, *ride_ops)


def _layer_tail_rows(x_ref, mixer_refs, param_refs, o_ref, gla_merge, final_norm):
    if gla_merge:
        oa_ref, ob_ref, r_ref = mixer_refs
        ggla_ref, wout_ref, *param_refs = param_refs
        ob = ob_ref[...]
        parts = []
        for h in range(B_HEADS):
            seg = ob[:, h * B_DV:(h + 1) * B_DV]
            parts.append(seg * lax.rsqrt(jnp.mean(seg * seg, axis=-1, keepdims=True) + EPS))
        r = r_ref[...]
        obn = jnp.concatenate(parts, axis=-1) * ggla_ref[...] * (r * jax.nn.sigmoid(r))
        mix = _dot(oa_ref[...], wout_ref[:A_WIDTH, :]) + _dot(obn.astype(BF16), wout_ref[A_WIDTH:, :])
    else:
        (oc_ref,) = mixer_refs
        wout_ref, *param_refs = param_refs
        mix = _dot(oc_ref[...], wout_ref[...])
    gffn_ref, wg_ref, wu_ref, wd_ref, *param_refs = param_refs
    x = x_ref[...] + mix
    y = _rms(x, gffn_ref[...]).astype(BF16)
    h = _dot(y, wg_ref[...])
    u = _dot(y, wu_ref[...])
    a = (h * jax.nn.sigmoid(h) * u).astype(BF16)
    x = x + _dot(a, wd_ref[...])
    o_ref[...] = _rms(x, param_refs[0][...]) if final_norm else x


def _layer_tail_kernel(*refs, n_mixer, n_params, gla_merge, final_norm, main_steps):
    main, extra = refs[:1 + n_mixer], refs[1 + n_mixer:2 + 2 * n_mixer]
    params = refs[2 + 2 * n_mixer:2 + 2 * n_mixer + n_params]
    o_ref, o_extra_ref = refs[-2:]

    @pl.when(pl.program_id(0) < main_steps)
    def _():
        _layer_tail_rows(main[0], main[1:], params, o_ref, gla_merge, final_norm)

    @pl.when(pl.program_id(0) == main_steps)
    def _():
        _layer_tail_rows(extra[0], extra[1:], params, o_extra_ref, gla_merge, final_norm)


def _layer_tail(x, mixer_out, x_extra, mixer_out_extra, mixer_params, g_ffn, wg, wu, wd, g_fin, tm):
    m, d = x.shape
    main_steps = m // tm
    row = lambda a: pl.BlockSpec((tm, a.shape[1]), lambda i: (jnp.minimum(i, main_steps - 1), 0))
    whole = lambda a: pl.BlockSpec(a.shape, lambda i: (0, 0))
    params = [*mixer_params, g_ffn, wg, wu, wd] + ([] if g_fin is None else [g_fin])
    return pl.pallas_call(
        functools.partial(_layer_tail_kernel, n_mixer=len(mixer_out), n_params=len(params),
                          gla_merge=len(mixer_out) == 3, final_norm=g_fin is not None, main_steps=main_steps),
        grid=(main_steps + 1,),
        in_specs=([row(x)] + [row(a) for a in mixer_out] + [whole(x_extra)]
                  + [whole(a) for a in mixer_out_extra] + [_resident(a) for a in params]),
        out_specs=[row(x), whole(x_extra)],
        out_shape=[jax.ShapeDtypeStruct((m, d), F32), jax.ShapeDtypeStruct(x_extra.shape, F32)],
        compiler_params=_params("arbitrary"),
        name="layer_tail",
    )(x, *mixer_out, x_extra, *mixer_out_extra, *[_operand(a) for a in params])


def _interleave(stage_generators):
    results = [None] * len(stage_generators)
    live = list(range(len(stage_generators)))
    while live:
        for i in list(live):
            try:
                next(stage_generators[i])
            except StopIteration as done:
                results[i] = done.value
                live.remove(i)
    return results


def _sb_core(q_heads, lo, first_kv, first_mask, earlier_kv, n_earlier, acc_ref, c_ref, o_ref, n_pairs,
             companions=(), on_companions=None):
    tk = first_kv.n_keys
    heads = range(2 * n_pairs)
    later = (lax.broadcasted_iota(jnp.int32, (tk, tk), 0)
             > lax.broadcasted_iota(jnp.int32, (tk, tk), 1)).astype(BF16)
    c_ref[...] = jnp.zeros_like(c_ref)

    def block(kv, mask):
        z = [kv.scores(q_heads[h], h // 2) for h in heads]
        yield
        log_beta, drop, after = [], [], []
        for h in heads:
            d = jnp.maximum(z[h], 0.0) + jnp.log(1.0 + jnp.exp(-jnp.abs(z[h])))
            log_beta.append(z[h] - d)
            drop.append(d if mask is None else jnp.where(mask, d, 0.0))
        yield
        for h in heads:
            hi, lo_part = _split_bf16(drop[h])
            after.append(_dot(hi, later) + _dot(lo_part, later))
        yield
        pv = []
        for h in heads:
            c = c_ref[h]
            w = jnp.exp(log_beta[h] - after[h] - c)
            if mask is not None:
                w = jnp.where(mask, w, 0.0)
            c_ref[h] = c + after[h][:, 0:1] + drop[h][:, 0:1]
            pv.append(kv.weighted(w.astype(BF16), kv.values(h // 2)))
        yield
        out = [jnp.where(lo, pv[2 * p], pv[2 * p + 1]) for p in range(n_pairs)]
        return out[0] if n_pairs == 1 else jnp.concatenate(out, axis=-1)

    def all_dead():
        return jnp.min(c_ref[...]) > SB_DEAD

    first, *companion_results = _interleave([block(first_kv, first_mask), *companions])
    acc_ref[...] = first
    if companions:
        on_companions(companion_results)

    def cond(carry):
        n, dead = carry
        return (n < n_earlier) & jnp.logical_not(dead)

    def body(carry):
        n, _ = carry
        acc_ref[...] += _interleave([block(earlier_kv(n), None)])[0]
        return n + 1, all_dead()

    lax.while_loop(cond, body, (jnp.int32(0), all_dead()))
    o_ref[...] = acc_ref[...].astype(o_ref.dtype)


def _mixer_ab_prompt_kernel(q_ref, k_ref, v_ref, qb_ref, kb_ref, vb_ref, la_ref, o_ref, ob_ref, s_out_ref,
                            acc_ref, c_ref, carry_ref, *, tb, n_pairs, blocks_per_step):
    pairs = range(B_HEADS // 2)
    strictly_earlier = (lax.broadcasted_iota(jnp.int32, (tb, tb), 1)
                        < lax.broadcasted_iota(jnp.int32, (tb, tb), 0))

    def kv_block(j):
        keys = pl.ds(pl.multiple_of(j * tb, tb), tb)
        return _KV(k_ref[:, keys], v_ref[:, keys], True)

    def keep_states(states):
        for p in pairs:
            carry_ref[2 * p], carry_ref[2 * p + 1] = states[p]
            s_out_ref[2 * p] = states[p][0][:B_DK, :]
            s_out_ref[2 * p + 1] = states[p][1][B_DK:, :]

    @pl.when(pl.program_id(1) == 0)
    def _():
        carry_ref[...] = jnp.zeros_like(carry_ref)

    def one_block(s, carry):
        qi = pl.program_id(1) * blocks_per_step + s
        rows = pl.ds(pl.multiple_of(s * tb, tb), tb)
        gla = [_gla_pair_tile(qb_ref[rows, _pair_cols(p)], kb_ref[rows, _pair_cols(p)],
                              vb_ref[rows, _pair_v_cols(p)], la_ref[rows, _pair_cols(p)],
                              [carry_ref[2 * p], carry_ref[2 * p + 1]],
                              _gla_rows_emit(ob_ref.at[rows, :], p)) for p in pairs]
        q_heads, lo = _split_heads(q_ref.at[rows, :], n_pairs, tb)
        _sb_core(q_heads, lo, kv_block(qi), strictly_earlier, lambda n: kv_block(qi - 1 - n), qi,
                 acc_ref, c_ref, o_ref.at[rows, :], n_pairs, companions=gla, on_companions=keep_states)
        return carry

    lax.fori_loop(0, blocks_per_step, one_block, 0)


def _sb_sample_kernel(q_ref, kn_ref, vn_ref, kc_ref, vc_ref, o_ref, acc_ref, c_ref, kpad_ref, vpad_ref,
                      *, ts, tk, n_cache_blocks, n_pairs):
    q_heads, lo = _split_heads(q_ref, n_pairs, ts)
    kpad_ref[...] = jnp.zeros_like(kpad_ref)
    vpad_ref[...] = jnp.zeros_like(vpad_ref)
    kpad_ref[:ts, :] = kn_ref[...]
    vpad_ref[:ts, :] = vn_ref[...]

    def cache_block(n):
        keys = pl.ds(pl.multiple_of((n_cache_blocks - 1 - n) * tk, tk), tk)
        return _KV(kc_ref[:, keys].astype(BF16), vc_ref[:, keys].astype(BF16), True)

    strictly_earlier = (lax.broadcasted_iota(jnp.int32, (ts, tk), 1)
                        < lax.broadcasted_iota(jnp.int32, (ts, tk), 0))
    _sb_core(q_heads, lo, _KV(kpad_ref[...], vpad_ref[...], False), strictly_earlier, cache_block,
             n_cache_blocks, acc_ref, c_ref, o_ref, n_pairs)


def _mixer_ab_prompt(q, k, v, qb, kb, vb, la):
    b, t, _ = q.shape
    tb = ATT_BLOCK
    assert t % tb == 0 and tb % GLA_CHUNK == 0 and tb <= GLA_MAX_TILE
    n_pairs = A_WIDTH // LANES
    per_step = math.gcd(t // tb, ATT_BLOCKS_PER_STEP)
    rows = lambda width: pl.BlockSpec((None, per_step * tb, width), lambda bi, qi: (bi, qi, 0))
    kv_spec = pl.BlockSpec((None, A_WIDTH, t), lambda bi, qi: (bi, 0, 0))
    state = (B_HEADS, B_DK, B_DV)
    return pl.pallas_call(
        functools.partial(_mixer_ab_prompt_kernel, tb=tb, n_pairs=n_pairs, blocks_per_step=per_step),
        grid=(b, t // (per_step * tb)),
        in_specs=[rows(A_WIDTH), kv_spec, kv_spec, rows(B_KW), rows(B_KW), rows(B_VW), rows(B_KW)],
        out_specs=[rows(A_WIDTH), rows(B_VW), pl.BlockSpec((None, *state), lambda bi, qi: (bi, 0, 0, 0))],
        out_shape=[jax.ShapeDtypeStruct(q.shape, BF16), jax.ShapeDtypeStruct(vb.shape, F32),
                   jax.ShapeDtypeStruct((b, *state), F32)],
        scratch_shapes=[pltpu.VMEM((tb, A_WIDTH), F32), pltpu.VMEM((2 * n_pairs, tb, 1), F32),
                        pltpu.VMEM((B_HEADS, LANES, B_DV), F32)],
        compiler_params=_params("parallel", "arbitrary"),
        name="mixer_ab",
    )(q, k, v, qb, kb, vb, la)


def _sb_attention_sample(q, k_new, v_new, k_cache, v_cache, layer):
    b, ts, _ = q.shape
    past = k_cache.shape[3]
    tk = ATT_BLOCK
    assert past % tk == 0 and ts <= tk
    n_pairs = SB_PAIRS_PER_STEP
    w = n_pairs * LANES
    new_spec = pl.BlockSpec((None, ts, w), lambda bi, hp: (bi, 0, hp))
    cache_spec = pl.BlockSpec((None, None, w, past), lambda bi, hp: (layer, bi, hp, 0))
    return pl.pallas_call(
        functools.partial(_sb_sample_kernel, ts=ts, tk=tk, n_cache_blocks=past // tk, n_pairs=n_pairs),
        grid=(b, A_WIDTH // w),
        in_specs=[new_spec, new_spec, new_spec, cache_spec, cache_spec],
        out_specs=new_spec,
        out_shape=jax.ShapeDtypeStruct(q.shape, BF16),
        scratch_shapes=[pltpu.VMEM((ts, w), F32), pltpu.VMEM((2 * n_pairs, ts, 1), F32),
                        pltpu.VMEM((tk, w), BF16), pltpu.VMEM((tk, w), BF16)],
        compiler_params=_params("parallel", "parallel"),
        name="sb_attention_sample",
    )(q, k_new, v_new, k_cache, v_cache)


def _band_core(q_heads, lo, kv, pens, bias, o_ref, n_pairs):
    heads = range(2 * n_pairs)
    blocks = range(len(kv))
    z = [[kv[i].scores(q_heads[h], h // 2) + bias(h, i) for i in blocks] for h in heads]
    acc = []
    for h in heads:
        m = None
        for i in blocks:
            mi = jnp.max(z[h][i], axis=-1, keepdims=True)
            if pens[i] is not None:
                mi = mi + pens[i]
            m = mi if m is None else jnp.maximum(m, mi)
        a = None
        for i in blocks:
            shift = m if pens[i] is None else m - pens[i]
            p = jnp.exp2(z[h][i] - shift).astype(BF16)
            v = kv[i].values(h // 2)
            ones = jnp.ones_like(v)
            first = kv[i].head_lanes()
            v = jnp.where(first, v, ones) if h % 2 == 0 else jnp.where(first, ones, v)
            pv = kv[i].weighted(p, v)
            a = pv if a is None else a + pv
        acc.append(a)
    for p in range(n_pairs):
        a0, a1 = acc[2 * p], acc[2 * p + 1]
        o_ref[:, _pair_cols(p)] = jnp.where(lo, a0 / pltpu.roll(a0, HEAD_DIM, axis=1),
                                            a1 / pltpu.roll(a1, HEAD_DIM, axis=1)).astype(o_ref.dtype)


def _band_prompt_kernel(q_ref, k_ref, v_ref, bias_ref, o_ref, *, tq, tk, n_pairs, blocks_per_step):
    def one_block(s, carry):
        qi = pl.program_id(2) * blocks_per_step + s
        rows = pl.ds(pl.multiple_of(s * tq, tq), tq)
        q_heads, lo = _split_heads(q_ref.at[rows, :], n_pairs, tq)
        kv, pens = [], []
        for dj in range(3):
            j = qi - dj
            pens.append(None if dj == 0 else jnp.where(j >= 0, 0.0, NEG_BIG).astype(F32))
            keys = pl.ds(pl.multiple_of(jnp.maximum(j, 0) * tk, tk), tk)
            kv.append(_KV(k_ref[:, keys], v_ref[:, keys], True))
        _band_core(q_heads, lo, kv, pens, lambda h, i: bias_ref[h, i], o_ref.at[rows, :], n_pairs)
        return carry

    lax.fori_loop(0, blocks_per_step, one_block, 0)


def _band_sample_kernel(q_ref, kn_ref, vn_ref, kc_ref, vc_ref, bias_ref, o_ref, kpad_ref, vpad_ref,
                        *, ts, tk, n_cache_blocks, n_pairs):
    q_heads, lo = _split_heads(q_ref, n_pairs, ts)
    kpad_ref[...] = jnp.zeros_like(kpad_ref)
    vpad_ref[...] = jnp.zeros_like(vpad_ref)
    kpad_ref[:ts, :] = kn_ref[...]
    vpad_ref[:ts, :] = vn_ref[...]
    kv = [_KV(kpad_ref[...], vpad_ref[...], False)]
    for dj in range(1, n_cache_blocks + 1):
        keys = slice((n_cache_blocks - dj) * tk, (n_cache_blocks - dj + 1) * tk)
        kv.append(_KV(kc_ref[:, keys].astype(BF16), vc_ref[:, keys].astype(BF16), True))
    is_new_key = lax.broadcasted_iota(jnp.int32, (ts, tk), 1) < ts

    def bias(h, i):
        return jnp.where(is_new_key, bias_ref[h, 0], NEG_BIG) if i == 0 else bias_ref[h, i]

    _band_core(q_heads, lo, kv, [None] * len(kv), bias, o_ref, n_pairs)


def _band_attention_prompt(q, k, v, bias):
    b, t, _ = q.shape
    tq = tk = ATT_BLOCK
    assert t % tk == 0
    n_pairs = BAND_PAIRS_PER_STEP
    w = n_pairs * LANES
    per_step = math.gcd(t // tq, ATT_BLOCKS_PER_STEP)
    kern = functools.partial(_band_prompt_kernel, tq=tq, tk=tk, n_pairs=n_pairs, blocks_per_step=per_step)
    kv_spec = pl.BlockSpec((None, w, t), lambda bi, hp, qi: (bi, hp, 0))
    q_spec = pl.BlockSpec((None, per_step * tq, w), lambda bi, hp, qi: (bi, qi, hp))
    bias_spec = pl.BlockSpec((2 * n_pairs, 3, tq, tk), lambda bi, hp, qi: (hp, 0, 0, 0))
    return pl.pallas_call(
        kern,
        grid=(b, C_WIDTH // w, t // (per_step * tq)),
        in_specs=[q_spec, kv_spec, kv_spec, bias_spec],
        out_specs=q_spec,
        out_shape=jax.ShapeDtypeStruct(q.shape, BF16),
        compiler_params=_params("parallel", "parallel", "parallel"),
        name="band_attention",
    )(q, k, v, bias)


def _band_attention_sample(q, k_new, v_new, k_cache, v_cache, layer, bias):
    b, ts, _ = q.shape
    wc = k_cache.shape[3]
    tk = ATT_BLOCK
    n_cache_blocks = min(wc // tk, 2)
    assert wc % (n_cache_blocks * tk) == 0 and ts <= tk
    n_pairs = BAND_PAIRS_PER_STEP
    w = n_pairs * LANES
    kern = functools.partial(_band_sample_kernel, ts=ts, tk=tk, n_cache_blocks=n_cache_blocks,
                             n_pairs=n_pairs)
    new_spec = pl.BlockSpec((None, ts, w), lambda bi, hp: (bi, 0, hp))
    cache_rows = n_cache_blocks * tk
    cache_spec = pl.BlockSpec((None, None, w, cache_rows),
                              lambda bi, hp: (layer, bi, hp, wc // cache_rows - 1))
    bias_spec = pl.BlockSpec((2 * n_pairs, 1 + n_cache_blocks, ts, tk), lambda bi, hp: (hp, 0, 0, 0))
    return pl.pallas_call(
        kern,
        grid=(b, C_WIDTH // w),
        in_specs=[new_spec, new_spec, new_spec, cache_spec, cache_spec, bias_spec],
        out_specs=new_spec,
        out_shape=jax.ShapeDtypeStruct(q.shape, BF16),
        scratch_shapes=[pltpu.VMEM((tk, w), BF16), pltpu.VMEM((tk, w), BF16)],
        compiler_params=_params("parallel", "parallel"),
        name="band_attention_sample",
    )(q, k_new, v_new, k_cache, v_cache, bias)


def _band_bias_kernel(g_ref, o_ref):
    rows = tk = ATT_BLOCK
    q_chunk = lax.broadcasted_iota(jnp.int32, (rows, tk), 0) // CHUNK
    k_chunk = lax.broadcasted_iota(jnp.int32, (rows, tk), 1) // CHUNK
    for dj in range(3):
        g = jnp.broadcast_to(g_ref[0, dj], (rows, 2 * tk))
        tile = pltpu.roll(g, 0, axis=1, stride=1, stride_axis=0)[:, :tk]
        diff = dj * (tk // CHUNK) + q_chunk - k_chunk
        seen = (diff >= 0) & (diff <= C_LEFT_CHUNKS)
        o_ref[0, dj] = jnp.where(seen, tile * LOG2E, NEG_BIG)


def _band_bias(rel_table):
    rows = tk = ATT_BLOCK
    c = jnp.arange(2 * tk, dtype=jnp.int32)
    u = jnp.where(c <= tk, -c, 2 * tk - c)
    idx = jnp.clip(jnp.arange(3, dtype=jnp.int32)[:, None] * tk + u[None, :], REL_MIN, REL_MAX) - REL_MIN
    g = rel_table[:, idx].astype(F32).reshape(C_HEADS, 3, 1, 2 * tk)
    return pl.pallas_call(
        _band_bias_kernel,
        grid=(C_HEADS,),
        in_specs=[pl.BlockSpec((1, 3, 1, 2 * tk), lambda h: (h, 0, 0, 0))],
        out_specs=pl.BlockSpec((1, 3, rows, tk), lambda h: (h, 0, 0, 0)),
        out_shape=jax.ShapeDtypeStruct((C_HEADS, 3, rows, tk), F32),
        compiler_params=_params("parallel"),
        name="band_bias",
    )(g)


def _gla_pair_tile(q, k, v, la, states, emit):
    L = GLA_CHUNK
    chunks = range(q.shape[0] // L)
    row = lax.broadcasted_iota(jnp.int32, (L, L), 0)
    colm = lax.broadcasted_iota(jnp.int32, (L, L), 1)
    tri = (colm <= row).astype(BF16)
    causal = colm <= row
    lane = lax.broadcasted_iota(jnp.int32, (L, LANES), 1)
    sub = lax.broadcasted_iota(jnp.int32, (LANES, B_DV), 0)
    mine = [(lane >= h * B_DK) & (lane < (h + 1) * B_DK) for h in range(2)]
    mine_rows = [(sub >= h * B_DK) & (sub < (h + 1) * B_DK) for h in range(2)]
    rows = [slice(c * L, (c + 1) * L) for c in chunks]

    b = []
    for r in rows:
        g_hi, g_lo = _split_bf16(la[r, :])
        b.append(_dot(tri, g_hi) + _dot(tri, g_lo))
    yield
    qg, qg_h, kg, kd_t, decay = [], [], [], [], []
    for c, r in zip(chunks, rows):
        qg_c = q[r, :] * (B_DK ** -0.5) * jnp.exp(b[c])
        qg.append(qg_c.astype(BF16))
        qg_h.append([jnp.where(mine[h], qg_c, 0.0).astype(BF16) for h in range(2)])
        kg.append((k[r, :] * jnp.exp(-b[c])).astype(BF16))
        b_t = b[c].T
        b_last = b_t[:, L - 1:L]
        kd_t.append((k[r, :].T * jnp.exp(b_last - b_t)).astype(BF16))
        decay.append(jnp.exp(b_last))
    yield
    att =[[jnp.where(causal, _dot_nt(qg_h[c][h], kg[c]), 0.0).astype(BF16) for h in range(2)]
           for c in chunks]
    yield
    o_intra, own = [], []
    for c, r in zip(chunks, rows):
        v_h = [v[r, h * B_DV:(h + 1) * B_DV].astype(BF16) for h in range(2)]
        o_intra.append([_dot(att[c][h], v_h[h]) for h in range(2)])
        own.append([jnp.where(mine_rows[h], _dot(kd_t[c], v_h[h]), 0.0) for h in range(2)])

    yield
    states = list(states)
    start = []
    for c in chunks:
        start.append([s.astype(BF16) for s in states])
        states = [decay[c] * states[h] + own[c][h] for h in range(2)]
    yield
    for c in chunks:
        for h in range(2):
            emit(c, h, o_intra[c][h] + _dot(qg[c], start[c][h]))
    return states


def _pair_states(s0_ref, p):
    zeros_state = jnp.zeros((B_DK, B_DV), F32)
    return [jnp.concatenate([s0_ref[2 * p], zeros_state], axis=0),
            jnp.concatenate([zeros_state, s0_ref[2 * p + 1]], axis=0)]


def _pair_v_cols(p):
    return slice(2 * p * B_DV, 2 * (p + 1) * B_DV)


def _gla_rows_emit(o_ref, p):
    def emit(c, h, o):
        o_ref[c * GLA_CHUNK:(c + 1) * GLA_CHUNK, (2 * p + h) * B_DV:(2 * p + h + 1) * B_DV] = o
    return emit


def _gla_kernel(q_ref, k_ref, v_ref, la_ref, s0_ref, o_ref, s_out_ref):
    pairs = range(B_HEADS // 2)
    states = _interleave([
        _gla_pair_tile(q_ref[:, _pair_cols(p)], k_ref[:, _pair_cols(p)], v_ref[:, _pair_v_cols(p)],
                       la_ref[:, _pair_cols(p)], _pair_states(s0_ref, p), _gla_rows_emit(o_ref, p))
        for p in pairs])
    for p in pairs:
        s_out_ref[2 * p] = states[p][0][:B_DK, :]
        s_out_ref[2 * p + 1] = states[p][1][B_DK:, :]


def _gla(q, k, v, la, s0):
    b, t, _ = q.shape
    assert t % GLA_CHUNK == 0 and t <= GLA_MAX_TILE
    qk_spec = pl.BlockSpec((None, t, B_KW), lambda bi: (bi, 0, 0))
    v_spec = pl.BlockSpec((None, t, B_VW), lambda bi: (bi, 0, 0))
    s_spec = pl.BlockSpec((None, B_HEADS, B_DK, B_DV), lambda bi: (bi, 0, 0, 0))
    return pl.pallas_call(
        _gla_kernel,
        grid=(b,),
        in_specs=[qk_spec, qk_spec, v_spec, qk_spec, s_spec],
        out_specs=[v_spec, s_spec],
        out_shape=[jax.ShapeDtypeStruct(v.shape, F32), jax.ShapeDtypeStruct(s0.shape, F32)],
        compiler_params=_params("parallel"),
        name="gla",
    )(q, k, v, la, s0)


def _pad_rows(x, n):
    return jnp.pad(x, ((0, 0), (0, n - x.shape[1]), (0, 0)))


def _heads_last(x, heads):
    n, b, _, s = x.shape
    return jnp.transpose(x.reshape(n, b, heads, HEAD_DIM, s), (0, 1, 4, 2, 3))


def _feature_major(cache):
    n, b, s, heads, hd = cache.shape
    return jnp.transpose(cache, (0, 1, 3, 4, 2)).reshape(n, b, heads * hd, s)


def _row_tile(m):
    for tm in (512, 256, 128, 64, 32, 16, 8):
        if m % tm == 0:
            return tm
    raise ValueError(f"token count {m} is not a multiple of 8")


def kernel(x_prompt, x_sample, cache_a_k, cache_a_v, state_b, cache_c_k, cache_c_v, norm_mix_g, norm_ffn_g, w_in_ab, w_gate_b, b_gate_b, norm_gla_g, w_out_ab, w_qkv_c, rel_bias_c, w_out_c, w_ffn_gate, w_ffn_up, w_ffn_down, norm_final_g):
    bp, tp, d = x_prompt.shape
    bs, ts, _ = x_sample.shape
    depth = norm_mix_g.shape[0]
    past = cache_a_k.shape[2]
    wc = cache_c_k.shape[2]
    assert tp % ATT_BLOCK == 0 and past % ATT_BLOCK == 0 and wc % ATT_BLOCK == 0
    assert ts <= GLA_CHUNK and ts % 8 == 0
    mp, ms = bp * tp, bs * ts
    tmp, tms = _row_tile(tp), _row_tile(ms)
    xp = x_prompt.reshape(mp, d)
    xs = x_sample.reshape(ms, d)
    row2 = lambda a: a.reshape(1, -1)

    a_ks, a_vs, b_sp, b_ss, c_ks, c_vs = [], [], [], [], [], []
    a_kv_prompt = c_kv_prompt = None
    n_ab, n_c = (depth + 1) // 2, depth // 2
    keep = min(C_LEFT_CHUNKS * CHUNK, tp)

    kv0, kv1 = A_WIDTH, 3 * A_WIDTH
    o = 3 * A_WIDTH + 2 * B_KW + B_VW
    w_main_all = jnp.concatenate([w_in_ab[:, :, :kv0], w_in_ab[:, :, kv1:o], w_in_ab[:, :, o + B_GATE_RANK:]],
                                 axis=2).astype(BF16)
    w_kv_ab_all = w_in_ab[:, :, kv0:kv1].astype(BF16)
    w_kv_ab_t_all = jnp.swapaxes(w_in_ab, 1, 2)[:, kv0:kv1].astype(BF16)
    w_lr_all = jnp.pad(w_in_ab[:, :, o:o + B_GATE_RANK],
                       ((0, 0), (0, 0), (0, LANES - B_GATE_RANK))).astype(BF16)
    w_gate_all = jnp.pad(w_gate_b, ((0, 0), (0, LANES - B_GATE_RANK), (0, 0))).astype(BF16)
    w_out_ab_all = w_out_ab.astype(BF16)
    w_q_all = w_qkv_c[:, :, :C_WIDTH].astype(BF16)
    w_kv_c_all = w_qkv_c[:, :, C_WIDTH:].astype(BF16)
    w_kv_c_t_all = jnp.swapaxes(w_qkv_c[:, :, C_WIDTH:], 1, 2).astype(BF16)
    w_out_c_all = w_out_c.astype(BF16)
    cache_a_k_fm, cache_a_v_fm, cache_c_k_fm, cache_c_v_fm = (
        _feature_major(c) for c in (cache_a_k, cache_a_v, cache_c_k, cache_c_v))

    for layer in range(depth):
        i = layer // 2
        g_mix = row2(norm_mix_g[layer])
        riders = _Riders((w_ffn_gate, w_ffn_up, w_ffn_down), layer, mp // tmp)
        ffn_ends = (row2(norm_ffn_g[layer]), row2(norm_final_g) if layer == depth - 1 else None)
        if layer % 2 == 0:
            w_main, w_kv, w_kv_t, w_lr, w_gate, w_out = (
                _Slab(w, i) for w in (w_main_all, w_kv_ab_all, w_kv_ab_t_all, w_lr_all, w_gate_all,
                                      w_out_ab_all))
            b_gate = row2(b_gate_b[i])
            g_gla = row2(norm_gla_g[i])

            qa, ka, va, kab, vab, qb, kb, vb, r, la, *ffn_w = _proj_ab(
                xp, g_mix, w_main, w_kv_t, w_lr, w_gate, b_gate, tmp, batch=bp,
                stack=_KVStack(i, n_ab, tp, a_kv_prompt), riders=riders)
            ffn = (ffn_ends[0], *ffn_w, ffn_ends[1])
            a_kv_prompt = (ka, va)
            sh = lambda a: a.reshape(bp, tp, -1)
            oa, ob, sbp = _mixer_ab_prompt(sh(qa), kab, vab, sh(qb), sh(kb), sh(vb), sh(la))
            mixed_p = (oa.reshape(mp, -1), ob.reshape(mp, -1), r)
            b_sp.append(sbp)

            qa, ka, va, kab, vab, qb, kb, vb, r, la = _proj_ab(
                xs, g_mix, w_main, w_kv, w_lr, w_gate, b_gate, tms)
            sh = lambda a: a.reshape(bs, ts, -1)
            oa = _sb_attention_sample(sh(qa), sh(kab), sh(vab), cache_a_k_fm, cache_a_v_fm, i)
            pad_t = lambda a: _pad_rows(sh(a), GLA_CHUNK)
            ob, sbs = _gla(pad_t(qb), pad_t(kb), pad_t(vb), pad_t(la), state_b[i])
            mixed_s = (oa.reshape(ms, -1), ob[:, :ts].reshape(ms, -1), r)
            xp, xs = _layer_tail(xp, mixed_p, xs, mixed_s, (g_gla, w_out), *ffn, tmp)
            a_ks.append(ka.reshape(bs, ts, A_HEADS, HEAD_DIM))
            a_vs.append(va.reshape(bs, ts, A_HEADS, HEAD_DIM))
            b_ss.append(sbs)
        else:
            w_q, w_kv, w_kv_t, w_out = (
                _Slab(w, i) for w in (w_q_all, w_kv_c_all, w_kv_c_t_all, w_out_c_all))

            q, k, v, kb16, vb16, *ffn_w = _proj_c(xp, g_mix, w_q, w_kv_t, tmp, batch=bp,
                                                  stack=_KVStack(i, n_c, keep, c_kv_prompt), riders=riders)
            ffn = (ffn_ends[0], *ffn_w, ffn_ends[1])
            c_kv_prompt = (k, v)
            bias = _band_bias(rel_bias_c[i])
            oc = _band_attention_prompt(q.reshape(bp, tp, -1), kb16, vb16, bias)
            mixed_p = (oc.reshape(mp, -1),)

            q, k, v, kb16, vb16 = _proj_c(xs, g_mix, w_q, w_kv, tms)
            sh = lambda a: a.reshape(bs, ts, -1)
            oc = _band_attention_sample(sh(q), sh(kb16), sh(vb16), cache_c_k_fm, cache_c_v_fm, i, bias)
            xp, xs = _layer_tail(xp, mixed_p, xs, (oc.reshape(ms, -1),), (w_out,), *ffn, tmp)
            c_ks.append(k.reshape(bs, ts, C_HEADS, HEAD_DIM))
            c_vs.append(v.reshape(bs, ts, C_HEADS, HEAD_DIM))

    y_prompt = xp.reshape(bp, tp, d)
    y_sample = xs.reshape(bs, ts, d)
    a_kp, a_vp = (_heads_last(a, A_HEADS) for a in a_kv_prompt)
    c_kp, c_vp = (_heads_last(a, C_HEADS) for a in c_kv_prompt)
    return (y_prompt, y_sample, a_kp, a_vp, jnp.stack(a_ks), jnp.stack(a_vs),
            jnp.stack(b_sp), jnp.stack(b_ss), c_kp, c_vp, jnp.stack(c_ks), jnp.stack(c_vs))
```

```python
import functools
import math

import jax
import jax.numpy as jnp
from jax import lax
from jax.experimental import pallas as pl
from jax.experimental.pallas import tpu as pltpu

F32 = jnp.float32
BF16 = jnp.bfloat16

EPS = 1e-6
HEAD_DIM = 64
LANES = 128
A_HEADS = 8
A_WIDTH = A_HEADS * HEAD_DIM
B_HEADS = 4
B_DK = 64
B_DV = 128
B_KW = B_HEADS * B_DK
B_VW = B_HEADS * B_DV
B_GATE_RANK = 16
B_GATE_TEMP = 16.0
GLA_CHUNK = 64
C_HEADS = 16
C_WIDTH = C_HEADS * HEAD_DIM
CHUNK = 64
C_LEFT_CHUNKS = 8
REL_MIN = -(CHUNK - 1)
REL_MAX = 128
ATT_BLOCK = 256
NEG_BIG = -1e30
LOG2E = 1.4426950408889634
SB_DEAD = 104.0
SB_PAIRS_PER_STEP = 4
BAND_PAIRS_PER_STEP = 4
MIXER_BLOCKS_PER_STEP = 4
BAND_BLOCKS_PER_STEP = 8
GLA_MAX_TILE = 512
VMEM_LIMIT = 56 * 1024 * 1024


def _params(*sem):
    return pltpu.CompilerParams(dimension_semantics=sem, vmem_limit_bytes=VMEM_LIMIT)


class _Slab:
    def __init__(self, stacked, index):
        self.stacked, self.index, self.shape = stacked, index, stacked.shape[1:]


def _operand(a):
    return a.stacked if isinstance(a, _Slab) else a


def _resident(a):
    if isinstance(a, _Slab):
        index = (a.index,) + (0,) * len(a.shape)
        return pl.BlockSpec((None, *a.shape), lambda *_: index, pipeline_mode=pl.Buffered(1))
    return pl.BlockSpec(a.shape, lambda *_: (0,) * a.ndim, pipeline_mode=pl.Buffered(1))


def _rms(x, g):
    return x * lax.rsqrt(jnp.mean(x * x, axis=-1, keepdims=True) + EPS) * g


def _log_sigmoid_pair(z):
    l = jnp.log1p(jnp.exp(-jnp.abs(z)))
    return jnp.minimum(z, 0.0) - l, jnp.minimum(-z, 0.0) - l


def _split_bf16(x):
    hi = x.astype(BF16)
    lo = (x - hi.astype(F32)).astype(BF16)
    return hi, lo


def _dot(a, b):
    return jnp.dot(a, b, preferred_element_type=F32)


def _dot_nt(a, b):
    return lax.dot_general(a, b, (((1,), (1,)), ((), ())), preferred_element_type=F32)


def _dot_tn(a, b):
    return lax.dot_general(a, b, (((0,), (0,)), ((), ())), preferred_element_type=F32)


def _pair_cols(p):
    return slice(p * LANES, (p + 1) * LANES)


class _KV:
    def __init__(self, k, v, feature_major):
        self.k, self.v, self.feature_major = k, v, feature_major
        self.n_keys = k.shape[1] if feature_major else k.shape[0]

    def scores(self, q_h, p):
        if self.feature_major:
            return _dot(q_h, self.k[_pair_cols(p), :])
        return _dot_nt(q_h, self.k[:, _pair_cols(p)])

    def values(self, p):
        return self.v[_pair_cols(p), :] if self.feature_major else self.v[:, _pair_cols(p)]

    def weighted(self, w, v_p):
        return _dot_nt(w, v_p) if self.feature_major else _dot(w, v_p)

    def head_lanes(self):
        shape = (LANES, self.n_keys) if self.feature_major else (self.n_keys, LANES)
        return lax.broadcasted_iota(jnp.int32, shape, 0 if self.feature_major else 1) < HEAD_DIM


def _split_heads(q_ref, n_pairs, tq):
    lo = lax.broadcasted_iota(jnp.int32, (tq, LANES), 1) < HEAD_DIM
    heads = []
    for p in range(n_pairs):
        q = q_ref[:, _pair_cols(p)]
        heads += [jnp.where(lo, q, jnp.zeros_like(q)), jnp.where(lo, jnp.zeros_like(q), q)]
    return heads, lo


def _emit_kv(y, wkv_ref, k_ref, v_ref, kb_ref, vb_ref, feature_major, kept_tail=None):
    if feature_major:
        kv = _dot_nt(wkv_ref[...], y)
        width = kv.shape[0] // 2
        k, v = kv[:width, :], kv[width:, :]
    else:
        kv = _dot_nt(y, wkv_ref[...])
        width = kv.shape[1] // 2
        k, v = kv[:, :width], kv[:, width:]
    kb_ref[...] = k.astype(BF16)
    vb_ref[...] = v.astype(BF16)
    if kept_tail is None:
        k_ref[...] = k
        v_ref[...] = v
    else:
        n_tiles, n_kept = kept_tail
        tile = pl.program_id(0) % n_tiles
        tm = k.shape[1]
        cols = pl.ds(pl.multiple_of(jnp.maximum(tile - (n_tiles - n_kept), 0) * tm, tm), tm)
        k_ref[:, cols] = k
        v_ref[:, cols] = v


class _Riders:
    def __init__(self, stacks, layer, steps):
        self.stacks, self.layer = list(stacks), layer
        self.chunks = next(c for c in (16, 8, 4, 2, 1)
                           if c <= steps and all(w.shape[1] % (16 * c) == 0 for w in self.stacks))

    def specs(self):
        layer, last = self.layer, self.chunks - 1
        ins = [pl.BlockSpec((None, w.shape[1] // self.chunks, w.shape[2]),
                            lambda i: (layer, jnp.minimum(i, last), 0)) for w in self.stacks]
        outs = [pl.BlockSpec((w.shape[1] // self.chunks, w.shape[2]),
                             lambda i: (jnp.minimum(i, last), 0)) for w in self.stacks]
        shapes = [jax.ShapeDtypeStruct(w.shape[1:], BF16) for w in self.stacks]
        return ins, outs, shapes


def _split_rider_refs(refs, n_aliased, n_riders):
    rider_in = refs[n_aliased:n_aliased + n_riders]
    outs = refs[n_aliased + n_riders:len(refs) - n_riders]
    return outs, list(zip(rider_in, refs[len(refs) - n_riders:]))


def _cast_riders(pairs):
    for src, dst in pairs:
        dst[...] = src[...].astype(BF16)


def _proj_ab_kernel(x_ref, g_ref, w_ref, wkv_ref, wlr_ref, wgate_ref, bgate_ref, *refs,
                    feature_major, n_aliased, n_riders):
    outs, riders = _split_rider_refs(refs, n_aliased, n_riders)
    qa_ref, ka_ref, va_ref, kab_ref, vab_ref, qb_ref, kb_ref, vb_ref, r_ref, la_ref = outs
    _cast_riders(riders)
    y = _rms(x_ref[...], g_ref[...]).astype(BF16)
    _emit_kv(y, wkv_ref, ka_ref, va_ref, kab_ref, vab_ref, feature_major)
    z = _dot_nt(y, w_ref[...])
    c = 0
    qa_ref[...] = (z[:, c:c + A_WIDTH] * (HEAD_DIM ** -0.5)).astype(BF16); c += A_WIDTH
    qb_ref[...] = z[:, c:c + B_KW]; c += B_KW
    kb_ref[...] = z[:, c:c + B_KW]; c += B_KW
    vb_ref[...] = z[:, c:c + B_VW]; c += B_VW
    r_ref[...] = z[:, c:c + B_VW]
    g_lr = _dot_nt(y, wlr_ref[...])
    gate = _dot(g_lr.astype(BF16), wgate_ref[...]) + bgate_ref[...]
    la_ref[...] = _log_sigmoid_pair(gate)[0] * (1.0 / B_GATE_TEMP)


class _KVStack:
    def __init__(self, layer, n_layers, keep, previous=None):
        self.layer, self.n_layers, self.keep, self.previous = layer, n_layers, keep, previous


def _kv_out(m, width, tm, batch, stack):
    if batch is None:
        spec = pl.BlockSpec((tm, width), lambda i: (i, 0))
        return [spec] * 4, [jax.ShapeDtypeStruct((m, width), dt) for dt in (F32, F32, BF16, BF16)], None
    t = m // batch
    assert t % tm == 0 and stack.keep % tm == 0 and stack.keep <= t
    n_tiles, layer = t // tm, stack.layer
    copy_spec = pl.BlockSpec((None, width, tm), lambda i: (i // n_tiles, 0, i % n_tiles))
    copy_shape = jax.ShapeDtypeStruct((batch, width, t), BF16)
    if stack.keep == t:
        f32_spec = pl.BlockSpec((None, None, width, tm), lambda i: (layer, i // n_tiles, 0, i % n_tiles))
        kept_tail = None
    else:
        f32_spec = pl.BlockSpec((None, None, width, stack.keep), lambda i: (layer, i // n_tiles, 0, 0))
        kept_tail = (n_tiles, stack.keep // tm)
    f32_shape = jax.ShapeDtypeStruct((stack.n_layers, batch, width, stack.keep), F32)
    return [f32_spec, f32_spec, copy_spec, copy_spec], [f32_shape, f32_shape, copy_shape, copy_shape], kept_tail


def _aliased_stack(stack, n_inputs, first_output):
    if stack is None or stack.previous is None:
        return [], [], {}
    prev = list(stack.previous)
    specs = [pl.BlockSpec(memory_space=pl.ANY)] * len(prev)
    return prev, specs, {n_inputs + j: first_output + j for j in range(len(prev))}


def _rider_plumbing(riders):
    if riders is None:
        return [], [], [], []
    return (riders.stacks, *riders.specs())


def _proj_ab(x, g, w_main, w_kv, w_lr, w_gate, b_gate, tm, batch=None, stack=None, riders=None):
    m, d = x.shape
    row = lambda n: pl.BlockSpec((tm, n), lambda i: (i, 0))
    params = (g, w_main, w_kv, w_lr, w_gate, b_gate)
    kv_specs, kv_shapes, kept_tail = _kv_out(m, A_WIDTH, tm, batch, stack)
    assert kept_tail is None
    prev, prev_specs, aliases = _aliased_stack(stack, 1 + len(params), 1)
    ride_ops, ride_in, ride_out, ride_shapes = _rider_plumbing(riders)
    rest = [(B_KW, F32), (B_KW, F32), (B_VW, F32), (B_VW, F32), (B_KW, F32)]
    return pl.pallas_call(
        functools.partial(_proj_ab_kernel, feature_major=batch is not None, n_aliased=len(prev),
                          n_riders=len(ride_ops)),
        grid=(m // tm,),
        in_specs=[row(d)] + [_resident(a) for a in params] + prev_specs + ride_in,
        out_specs=[row(A_WIDTH)] + kv_specs + [row(n) for n, _ in rest] + ride_out,
        out_shape=([jax.ShapeDtypeStruct((m, A_WIDTH), BF16)] + kv_shapes
                   + [jax.ShapeDtypeStruct((m, n), dt) for n, dt in rest] + ride_shapes),
        input_output_aliases=aliases,
        compiler_params=_params("arbitrary" if ride_ops else "parallel"),
        name="proj_ab",
    )(x, *[_operand(a) for a in params], *prev, *ride_ops)


def _proj_c_kernel(x_ref, g_ref, wq_ref, wkv_ref, *refs, feature_major, n_aliased, kept_tail,
                   n_riders):
    outs, riders = _split_rider_refs(refs, n_aliased, n_riders)
    q_ref, k_ref, v_ref, kb_ref, vb_ref = outs
    _cast_riders(riders)
    y = _rms(x_ref[...], g_ref[...]).astype(BF16)
    _emit_kv(y, wkv_ref, k_ref, v_ref, kb_ref, vb_ref, feature_major, kept_tail)
    q_ref[...] = (_dot(y, wq_ref[...]) * (HEAD_DIM ** -0.5 * LOG2E)).astype(BF16)


def _proj_c(x, g, w_q, w_kv, tm, batch=None, stack=None, riders=None):
    m, d = x.shape
    row = lambda n: pl.BlockSpec((tm, n), lambda i: (i, 0))
    params = (g, w_q, w_kv)
    kv_specs, kv_shapes, kept_tail = _kv_out(m, C_WIDTH, tm, batch, stack)
    prev, prev_specs, aliases = _aliased_stack(stack, 1 + len(params), 1)
    ride_ops, ride_in, ride_out, ride_shapes = _rider_plumbing(riders)
    return pl.pallas_call(
        functools.partial(_proj_c_kernel, feature_major=batch is not None, n_aliased=len(prev),
                          kept_tail=kept_tail, n_riders=len(ride_ops)),
        grid=(m // tm,),
        in_specs=[row(d)] + [_resident(a) for a in params] + prev_specs + ride_in,
        out_specs=[row(C_WIDTH)] + kv_specs + ride_out,
        out_shape=[jax.ShapeDtypeStruct((m, C_WIDTH), BF16)] + kv_shapes + ride_shapes,
        input_output_aliases=aliases,
        compiler_params=_params("arbitrary" if (kept_tail or ride_ops) else "parallel"),
        name="proj_c",
    )(x, *[_operand(a) for a in params], *prev, *ride_ops)


def _layer_tail_rows(x_ref, mixer_refs, param_refs, o_ref, gla_merge, final_norm):
    if gla_merge:
        oa_ref, ob_ref, r_ref = mixer_refs
        ggla_ref, wout_ref, *param_refs = param_refs
        ob = ob_ref[...]
        parts = []
        for h in range(B_HEADS):
            seg = ob[:, h * B_DV:(h + 1) * B_DV]
            parts.append(seg * lax.rsqrt(jnp.mean(seg * seg, axis=-1, keepdims=True) + EPS))
        r = r_ref[...]
        obn = jnp.concatenate(parts, axis=-1) * ggla_ref[...] * (r * jax.nn.sigmoid(r))
        mix = _dot(oa_ref[...], wout_ref[:A_WIDTH, :]) + _dot(obn.astype(BF16), wout_ref[A_WIDTH:, :])
    else:
        (oc_ref,) = mixer_refs
        wout_ref, *param_refs = param_refs
        mix = _dot(oc_ref[...], wout_ref[...])
    gffn_ref, wg_ref, wu_ref, wd_ref, *param_refs = param_refs
    x = x_ref[...] + mix
    y = _rms(x, gffn_ref[...]).astype(BF16)
    h = _dot(y, wg_ref[...])
    u = _dot(y, wu_ref[...])
    a = (h * jax.nn.sigmoid(h) * u).astype(BF16)
    x = x + _dot(a, wd_ref[...])
    o_ref[...] = _rms(x, param_refs[0][...]) if final_norm else x


def _layer_tail_kernel(*refs, n_mixer, n_params, gla_merge, final_norm, main_steps):
    main, extra = refs[:1 + n_mixer], refs[1 + n_mixer:2 + 2 * n_mixer]
    params = refs[2 + 2 * n_mixer:2 + 2 * n_mixer + n_params]
    o_ref, o_extra_ref = refs[-2:]

    @pl.when(pl.program_id(0) < main_steps)
    def _():
        _layer_tail_rows(main[0], main[1:], params, o_ref, gla_merge, final_norm)

    @pl.when(pl.program_id(0) == main_steps)
    def _():
        _layer_tail_rows(extra[0], extra[1:], params, o_extra_ref, gla_merge, final_norm)


def _layer_tail(x, mixer_out, x_extra, mixer_out_extra, mixer_params, g_ffn, wg, wu, wd, g_fin, tm):
    m, d = x.shape
    main_steps = m // tm
    row = lambda a: pl.BlockSpec((tm, a.shape[1]), lambda i: (jnp.minimum(i, main_steps - 1), 0))
    whole = lambda a: pl.BlockSpec(a.shape, lambda i: (0, 0))
    params = [*mixer_params, g_ffn, wg, wu, wd] + ([] if g_fin is None else [g_fin])
    return pl.pallas_call(
        functools.partial(_layer_tail_kernel, n_mixer=len(mixer_out), n_params=len(params),
                          gla_merge=len(mixer_out) == 3, final_norm=g_fin is not None, main_steps=main_steps),
        grid=(main_steps + 1,),
        in_specs=([row(x)] + [row(a) for a in mixer_out] + [whole(x_extra)]
                  + [whole(a) for a in mixer_out_extra] + [_resident(a) for a in params]),
        out_specs=[row(x), whole(x_extra)],
        out_shape=[jax.ShapeDtypeStruct((m, d), F32), jax.ShapeDtypeStruct(x_extra.shape, F32)],
        compiler_params=_params("arbitrary"),
        name="layer_tail",
    )(x, *mixer_out, x_extra, *mixer_out_extra, *[_operand(a) for a in params])


def _interleave(stage_generators):
    results = [None] * len(stage_generators)
    live = list(range(len(stage_generators)))
    while live:
        for i in list(live):
            try:
                next(stage_generators[i])
            except StopIteration as done:
                results[i] = done.value
                live.remove(i)
    return results


def _sb_core(q_heads, lo, first_kv, first_mask, earlier_kv, n_earlier, acc_ref, c_ref, o_ref, n_pairs,
             companions=(), on_companions=None):
    tk = first_kv.n_keys
    heads = range(2 * n_pairs)
    later = (lax.broadcasted_iota(jnp.int32, (tk, tk), 0)
             > lax.broadcasted_iota(jnp.int32, (tk, tk), 1)).astype(BF16)
    c_ref[...] = jnp.zeros_like(c_ref)

    def block(kv, mask):
        z = [kv.scores(q_heads[h], h // 2) for h in heads]
        yield
        log_beta, drop, after = [], [], []
        for h in heads:
            d = jnp.maximum(z[h], 0.0) + jnp.log(1.0 + jnp.exp(-jnp.abs(z[h])))
            log_beta.append(z[h] - d)
            drop.append(d if mask is None else jnp.where(mask, d, 0.0))
        yield
        for h in heads:
            hi, lo_part = _split_bf16(drop[h])
            after.append(_dot(hi, later) + _dot(lo_part, later))
        yield
        pv = []
        for h in heads:
            c = c_ref[h]
            w = jnp.exp(log_beta[h] - after[h] - c)
            if mask is not None:
                w = jnp.where(mask, w, 0.0)
            c_ref[h] = c + after[h][:, 0:1] + drop[h][:, 0:1]
            pv.append(kv.weighted(w.astype(BF16), kv.values(h // 2)))
        yield
        out = [jnp.where(lo, pv[2 * p], pv[2 * p + 1]) for p in range(n_pairs)]
        return out[0] if n_pairs == 1 else jnp.concatenate(out, axis=-1)

    def all_dead():
        return jnp.min(c_ref[...]) > SB_DEAD

    first, *companion_results = _interleave([block(first_kv, first_mask), *companions])
    acc_ref[...] = first
    if companions:
        on_companions(companion_results)

    def cond(carry):
        n, dead = carry
        return (n < n_earlier) & jnp.logical_not(dead)

    def body(carry):
        n, _ = carry
        acc_ref[...] += _interleave([block(earlier_kv(n), None)])[0]
        return n + 1, all_dead()

    lax.while_loop(cond, body, (jnp.int32(0), all_dead()))
    o_ref[...] = acc_ref[...].astype(o_ref.dtype)


def _mixer_ab_prompt_kernel(q_ref, k_ref, v_ref, qb_ref, kb_ref, vb_ref, la_ref, o_ref, ob_ref, s_out_ref,
                            acc_ref, c_ref, carry_ref, *, tb, n_pairs, blocks_per_step):
    pairs = range(B_HEADS // 2)
    strictly_earlier = (lax.broadcasted_iota(jnp.int32, (tb, tb), 1)
                        < lax.broadcasted_iota(jnp.int32, (tb, tb), 0))

    def kv_block(j):
        keys = pl.ds(pl.multiple_of(j * tb, tb), tb)
        return _KV(k_ref[:, keys], v_ref[:, keys], True)

    def keep_states(states):
        for p in pairs:
            carry_ref[2 * p], carry_ref[2 * p + 1] = states[p]
            s_out_ref[2 * p] = states[p][0][:B_DK, :]
            s_out_ref[2 * p + 1] = states[p][1][B_DK:, :]

    @pl.when(pl.program_id(1) == 0)
    def _():
        carry_ref[...] = jnp.zeros_like(carry_ref)

    def one_block(s, carry):
        qi = pl.program_id(1) * blocks_per_step + s
        rows = pl.ds(pl.multiple_of(s * tb, tb), tb)
        gla = [_gla_pair_tile(qb_ref[rows, _pair_cols(p)], kb_ref[rows, _pair_cols(p)],
                              vb_ref[rows, _pair_v_cols(p)], la_ref[rows, _pair_cols(p)],
                              [carry_ref[2 * p], carry_ref[2 * p + 1]],
                              _gla_rows_emit(ob_ref.at[rows, :], p)) for p in pairs]
        q_heads, lo = _split_heads(q_ref.at[rows, :], n_pairs, tb)
        _sb_core(q_heads, lo, kv_block(qi), strictly_earlier, lambda n: kv_block(qi - 1 - n), qi,
                 acc_ref, c_ref, o_ref.at[rows, :], n_pairs, companions=gla, on_companions=keep_states)
        return carry

    lax.fori_loop(0, blocks_per_step, one_block, 0)


def _sb_sample_kernel(q_ref, kn_ref, vn_ref, kc_ref, vc_ref, o_ref, acc_ref, c_ref, kpad_ref, vpad_ref,
                      *, ts, tk, n_cache_blocks, n_pairs):
    q_heads, lo = _split_heads(q_ref, n_pairs, ts)
    kpad_ref[...] = jnp.zeros_like(kpad_ref)
    vpad_ref[...] = jnp.zeros_like(vpad_ref)
    kpad_ref[:ts, :] = kn_ref[...]
    vpad_ref[:ts, :] = vn_ref[...]

    def cache_block(n):
        keys = pl.ds(pl.multiple_of((n_cache_blocks - 1 - n) * tk, tk), tk)
        return _KV(kc_ref[:, keys].astype(BF16), vc_ref[:, keys].astype(BF16), True)

    strictly_earlier = (lax.broadcasted_iota(jnp.int32, (ts, tk), 1)
                        < lax.broadcasted_iota(jnp.int32, (ts, tk), 0))
    _sb_core(q_heads, lo, _KV(kpad_ref[...], vpad_ref[...], False), strictly_earlier, cache_block,
             n_cache_blocks, acc_ref, c_ref, o_ref, n_pairs)


def _mixer_ab_prompt(q, k, v, qb, kb, vb, la):
    b, t, _ = q.shape
    tb = ATT_BLOCK
    assert t % tb == 0 and tb % GLA_CHUNK == 0 and tb <= GLA_MAX_TILE
    n_pairs = A_WIDTH // LANES
    per_step = math.gcd(t // tb, MIXER_BLOCKS_PER_STEP)
    rows = lambda width: pl.BlockSpec((None, per_step * tb, width), lambda bi, qi: (bi, qi, 0))
    kv_spec = pl.BlockSpec((None, A_WIDTH, t), lambda bi, qi: (bi, 0, 0))
    state = (B_HEADS, B_DK, B_DV)
    return pl.pallas_call(
        functools.partial(_mixer_ab_prompt_kernel, tb=tb, n_pairs=n_pairs, blocks_per_step=per_step),
        grid=(b, t // (per_step * tb)),
        in_specs=[rows(A_WIDTH), kv_spec, kv_spec, rows(B_KW), rows(B_KW), rows(B_VW), rows(B_KW)],
        out_specs=[rows(A_WIDTH), rows(B_VW), pl.BlockSpec((None, *state), lambda bi, qi: (bi, 0, 0, 0))],
        out_shape=[jax.ShapeDtypeStruct(q.shape, BF16), jax.ShapeDtypeStruct(vb.shape, F32),
                   jax.ShapeDtypeStruct((b, *state), F32)],
        scratch_shapes=[pltpu.VMEM((tb, A_WIDTH), F32), pltpu.VMEM((2 * n_pairs, tb, 1), F32),
                        pltpu.VMEM((B_HEADS, LANES, B_DV), F32)],
        compiler_params=_params("parallel", "arbitrary"),
        name="mixer_ab",
    )(q, k, v, qb, kb, vb, la)


def _sb_attention_sample(q, k_new, v_new, k_cache, v_cache, layer):
    b, ts, _ = q.shape
    past = k_cache.shape[3]
    tk = ATT_BLOCK
    assert past % tk == 0 and ts <= tk
    n_pairs = SB_PAIRS_PER_STEP
    w = n_pairs * LANES
    new_spec = pl.BlockSpec((None, ts, w), lambda bi, hp: (bi, 0, hp))
    cache_spec = pl.BlockSpec((None, None, w, past), lambda bi, hp: (layer, bi, hp, 0))
    return pl.pallas_call(
        functools.partial(_sb_sample_kernel, ts=ts, tk=tk, n_cache_blocks=past // tk, n_pairs=n_pairs),
        grid=(b, A_WIDTH // w),
        in_specs=[new_spec, new_spec, new_spec, cache_spec, cache_spec],
        out_specs=new_spec,
        out_shape=jax.ShapeDtypeStruct(q.shape, BF16),
        scratch_shapes=[pltpu.VMEM((ts, w), F32), pltpu.VMEM((2 * n_pairs, ts, 1), F32),
                        pltpu.VMEM((tk, w), BF16), pltpu.VMEM((tk, w), BF16)],
        compiler_params=_params("parallel", "parallel"),
        name="sb_attention_sample",
    )(q, k_new, v_new, k_cache, v_cache)


def _band_core(q_heads, lo, kv, pens, bias, o_ref, n_pairs):
    heads = range(2 * n_pairs)
    blocks = range(len(kv))
    z = [[kv[i].scores(q_heads[h], h // 2) + bias(h, i) for i in blocks] for h in heads]
    acc = []
    for h in heads:
        m = None
        for i in blocks:
            mi = jnp.max(z[h][i], axis=-1, keepdims=True)
            if pens[i] is not None:
                mi = mi + pens[i]
            m = mi if m is None else jnp.maximum(m, mi)
        a = None
        for i in blocks:
            shift = m if pens[i] is None else m - pens[i]
            p = jnp.exp2(z[h][i] - shift).astype(BF16)
            v = kv[i].values(h // 2)
            ones = jnp.ones_like(v)
            first = kv[i].head_lanes()
            v = jnp.where(first, v, ones) if h % 2 == 0 else jnp.where(first, ones, v)
            pv = kv[i].weighted(p, v)
            a = pv if a is None else a + pv
        acc.append(a)
    for p in range(n_pairs):
        a0, a1 = acc[2 * p], acc[2 * p + 1]
        o_ref[:, _pair_cols(p)] = jnp.where(lo, a0 / pltpu.roll(a0, HEAD_DIM, axis=1),
                                            a1 / pltpu.roll(a1, HEAD_DIM, axis=1)).astype(o_ref.dtype)


def _band_prompt_kernel(q_ref, k_ref, v_ref, bias_ref, o_ref, *, tq, tk, n_pairs, blocks_per_step):
    def one_block(s, carry):
        qi = pl.program_id(2) * blocks_per_step + s
        rows = pl.ds(pl.multiple_of(s * tq, tq), tq)
        q_heads, lo = _split_heads(q_ref.at[rows, :], n_pairs, tq)
        kv, pens = [], []
        for dj in range(3):
            j = qi - dj
            pens.append(None if dj == 0 else jnp.where(j >= 0, 0.0, NEG_BIG).astype(F32))
            keys = pl.ds(pl.multiple_of(jnp.maximum(j, 0) * tk, tk), tk)
            kv.append(_KV(k_ref[:, keys], v_ref[:, keys], True))
        _band_core(q_heads, lo, kv, pens, lambda h, i: bias_ref[h, i], o_ref.at[rows, :], n_pairs)
        return carry

    lax.fori_loop(0, blocks_per_step, one_block, 0)


def _band_sample_kernel(q_ref, kn_ref, vn_ref, kc_ref, vc_ref, bias_ref, o_ref, kpad_ref, vpad_ref,
                        *, ts, tk, n_cache_blocks, n_pairs):
    q_heads, lo = _split_heads(q_ref, n_pairs, ts)
    kpad_ref[...] = jnp.zeros_like(kpad_ref)
    vpad_ref[...] = jnp.zeros_like(vpad_ref)
    kpad_ref[:ts, :] = kn_ref[...]
    vpad_ref[:ts, :] = vn_ref[...]
    kv = [_KV(kpad_ref[...], vpad_ref[...], False)]
    for dj in range(1, n_cache_blocks + 1):
        keys = slice((n_cache_blocks - dj) * tk, (n_cache_blocks - dj + 1) * tk)
        kv.append(_KV(kc_ref[:, keys].astype(BF16), vc_ref[:, keys].astype(BF16), True))
    is_new_key = lax.broadcasted_iota(jnp.int32, (ts, tk), 1) < ts

    def bias(h, i):
        return jnp.where(is_new_key, bias_ref[h, 0], NEG_BIG) if i == 0 else bias_ref[h, i]

    _band_core(q_heads, lo, kv, [None] * len(kv), bias, o_ref, n_pairs)


def _band_attention_prompt(q, k, v, bias):
    b, t, _ = q.shape
    tq = tk = ATT_BLOCK
    assert t % tk == 0
    n_pairs = BAND_PAIRS_PER_STEP
    w = n_pairs * LANES
    per_step = math.gcd(t // tq, BAND_BLOCKS_PER_STEP)
    kern = functools.partial(_band_prompt_kernel, tq=tq, tk=tk, n_pairs=n_pairs, blocks_per_step=per_step)
    kv_spec = pl.BlockSpec((None, w, t), lambda bi, hp, qi: (bi, hp, 0))
    q_spec = pl.BlockSpec((None, per_step * tq, w), lambda bi, hp, qi: (bi, qi, hp))
    bias_spec = pl.BlockSpec((2 * n_pairs, 3, tq, tk), lambda bi, hp, qi: (hp, 0, 0, 0))
    return pl.pallas_call(
        kern,
        grid=(b, C_WIDTH // w, t // (per_step * tq)),
        in_specs=[q_spec, kv_spec, kv_spec, bias_spec],
        out_specs=q_spec,
        out_shape=jax.ShapeDtypeStruct(q.shape, BF16),
        compiler_params=_params("parallel", "parallel", "parallel"),
        name="band_attention",
    )(q, k, v, bias)


def _band_attention_sample(q, k_new, v_new, k_cache, v_cache, layer, bias):
    b, ts, _ = q.shape
    wc = k_cache.shape[3]
    tk = ATT_BLOCK
    n_cache_blocks = min(wc // tk, 2)
    assert wc % (n_cache_blocks * tk) == 0 and ts <= tk
    n_pairs = BAND_PAIRS_PER_STEP
    w = n_pairs * LANES
    kern = functools.partial(_band_sample_kernel, ts=ts, tk=tk, n_cache_blocks=n_cache_blocks,
                             n_pairs=n_pairs)
    new_spec = pl.BlockSpec((None, ts, w), lambda bi, hp: (bi, 0, hp))
    cache_rows = n_cache_blocks * tk
    cache_spec = pl.BlockSpec((None, None, w, cache_rows),
                              lambda bi, hp: (layer, bi, hp, wc // cache_rows - 1))
    bias_spec = pl.BlockSpec((2 * n_pairs, 1 + n_cache_blocks, ts, tk), lambda bi, hp: (hp, 0, 0, 0))
    return pl.pallas_call(
        kern,
        grid=(b, C_WIDTH // w),
        in_specs=[new_spec, new_spec, new_spec, cache_spec, cache_spec, bias_spec],
        out_specs=new_spec,
        out_shape=jax.ShapeDtypeStruct(q.shape, BF16),
        scratch_shapes=[pltpu.VMEM((tk, w), BF16), pltpu.VMEM((tk, w), BF16)],
        compiler_params=_params("parallel", "parallel"),
        name="band_attention_sample",
    )(q, k_new, v_new, k_cache, v_cache, bias)


def _band_bias_kernel(g_ref, o_ref):
    rows = tk = ATT_BLOCK
    q_chunk = lax.broadcasted_iota(jnp.int32, (rows, tk), 0) // CHUNK
    k_chunk = lax.broadcasted_iota(jnp.int32, (rows, tk), 1) // CHUNK
    for dj in range(3):
        g = jnp.broadcast_to(g_ref[0, dj], (rows, 2 * tk))
        tile = pltpu.roll(g, 0, axis=1, stride=1, stride_axis=0)[:, :tk]
        diff = dj * (tk // CHUNK) + q_chunk - k_chunk
        seen = (diff >= 0) & (diff <= C_LEFT_CHUNKS)
        o_ref[0, dj] = jnp.where(seen, tile * LOG2E, NEG_BIG)


def _band_bias(rel_table):
    rows = tk = ATT_BLOCK
    c = jnp.arange(2 * tk, dtype=jnp.int32)
    u = jnp.where(c <= tk, -c, 2 * tk - c)
    idx = jnp.clip(jnp.arange(3, dtype=jnp.int32)[:, None] * tk + u[None, :], REL_MIN, REL_MAX) - REL_MIN
    g = rel_table[:, idx].astype(F32).reshape(C_HEADS, 3, 1, 2 * tk)
    return pl.pallas_call(
        _band_bias_kernel,
        grid=(C_HEADS,),
        in_specs=[pl.BlockSpec((1, 3, 1, 2 * tk), lambda h: (h, 0, 0, 0))],
        out_specs=pl.BlockSpec((1, 3, rows, tk), lambda h: (h, 0, 0, 0)),
        out_shape=jax.ShapeDtypeStruct((C_HEADS, 3, rows, tk), F32),
        compiler_params=_params("parallel"),
        name="band_bias",
    )(g)


def _gla_pair_tile(q, k, v, la, states, emit):
    L = GLA_CHUNK
    chunks = range(q.shape[0] // L)
    row = lax.broadcasted_iota(jnp.int32, (L, L), 0)
    colm = lax.broadcasted_iota(jnp.int32, (L, L), 1)
    tri = (colm <= row).astype(BF16)
    causal = colm <= row
    lane = lax.broadcasted_iota(jnp.int32, (L, LANES), 1)
    sub = lax.broadcasted_iota(jnp.int32, (LANES, B_DV), 0)
    mine = [(lane >= h * B_DK) & (lane < (h + 1) * B_DK) for h in range(2)]
    mine_rows = [(sub >= h * B_DK) & (sub < (h + 1) * B_DK) for h in range(2)]
    rows = [slice(c * L, (c + 1) * L) for c in chunks]

    b = []
    for r in rows:
        g_hi, g_lo = _split_bf16(la[r, :])
        b.append(_dot(tri, g_hi) + _dot(tri, g_lo))
    yield
    qg, qg_h, kg, kd_t, decay = [], [], [], [], []
    for c, r in zip(chunks, rows):
        qg_c = q[r, :] * (B_DK ** -0.5) * jnp.exp(b[c])
        qg.append(qg_c.astype(BF16))
        qg_h.append([jnp.where(mine[h], qg_c, 0.0).astype(BF16) for h in range(2)])
        kg.append((k[r, :] * jnp.exp(-b[c])).astype(BF16))
        b_t = b[c].T
        b_last = b_t[:, L - 1:L]
        kd_t.append((k[r, :].T * jnp.exp(b_last - b_t)).astype(BF16))
        decay.append(jnp.exp(b_last))
    yield
    att =[[jnp.where(causal, _dot_nt(qg_h[c][h], kg[c]), 0.0).astype(BF16) for h in range(2)]
           for c in chunks]
    yield
    o_intra, own = [], []
    for c, r in zip(chunks, rows):
        v_h = [v[r, h * B_DV:(h + 1) * B_DV].astype(BF16) for h in range(2)]
        o_intra.append([_dot(att[c][h], v_h[h]) for h in range(2)])
        own.append([jnp.where(mine_rows[h], _dot(kd_t[c], v_h[h]), 0.0) for h in range(2)])

    yield
    states = list(states)
    start = []
    for c in chunks:
        start.append([s.astype(BF16) for s in states])
        states = [decay[c] * states[h] + own[c][h] for h in range(2)]
    yield
    for c in chunks:
        for h in range(2):
            emit(c, h, o_intra[c][h] + _dot(qg[c], start[c][h]))
    return states


def _pair_states(s0_ref, p):
    zeros_state = jnp.zeros((B_DK, B_DV), F32)
    return [jnp.concatenate([s0_ref[2 * p], zeros_state], axis=0),
            jnp.concatenate([zeros_state, s0_ref[2 * p + 1]], axis=0)]


def _pair_v_cols(p):
    return slice(2 * p * B_DV, 2 * (p + 1) * B_DV)


def _gla_rows_emit(o_ref, p):
    def emit(c, h, o):
        o_ref[c * GLA_CHUNK:(c + 1) * GLA_CHUNK, (2 * p + h) * B_DV:(2 * p + h + 1) * B_DV] = o
    return emit


def _gla_kernel(q_ref, k_ref, v_ref, la_ref, s0_ref, o_ref, s_out_ref):
    pairs = range(B_HEADS // 2)
    states = _interleave([
        _gla_pair_tile(q_ref[:, _pair_cols(p)], k_ref[:, _pair_cols(p)], v_ref[:, _pair_v_cols(p)],
                       la_ref[:, _pair_cols(p)], _pair_states(s0_ref, p), _gla_rows_emit(o_ref, p))
        for p in pairs])
    for p in pairs:
        s_out_ref[2 * p] = states[p][0][:B_DK, :]
        s_out_ref[2 * p + 1] = states[p][1][B_DK:, :]


def _gla(q, k, v, la, s0):
    b, t, _ = q.shape
    assert t % GLA_CHUNK == 0 and t <= GLA_MAX_TILE
    qk_spec = pl.BlockSpec((None, t, B_KW), lambda bi: (bi, 0, 0))
    v_spec = pl.BlockSpec((None, t, B_VW), lambda bi: (bi, 0, 0))
    s_spec = pl.BlockSpec((None, B_HEADS, B_DK, B_DV), lambda bi: (bi, 0, 0, 0))
    return pl.pallas_call(
        _gla_kernel,
        grid=(b,),
        in_specs=[qk_spec, qk_spec, v_spec, qk_spec, s_spec],
        out_specs=[v_spec, s_spec],
        out_shape=[jax.ShapeDtypeStruct(v.shape, F32), jax.ShapeDtypeStruct(s0.shape, F32)],
        compiler_params=_params("parallel"),
        name="gla",
    )(q, k, v, la, s0)


def _pad_rows(x, n):
    return jnp.pad(x, ((0, 0), (0, n - x.shape[1]), (0, 0)))


def _heads_last(x, heads):
    n, b, _, s = x.shape
    return jnp.transpose(x.reshape(n, b, heads, HEAD_DIM, s), (0, 1, 4, 2, 3))


def _feature_major(cache):
    n, b, s, heads, hd = cache.shape
    return jnp.transpose(cache, (0, 1, 3, 4, 2)).reshape(n, b, heads * hd, s)


def _row_tile(m):
    for tm in (512, 256, 128, 64, 32, 16, 8):
        if m % tm == 0:
            return tm
    raise ValueError(f"token count {m} is not a multiple of 8")


def kernel(x_prompt, x_sample, cache_a_k, cache_a_v, state_b, cache_c_k, cache_c_v, norm_mix_g, norm_ffn_g, w_in_ab, w_gate_b, b_gate_b, norm_gla_g, w_out_ab, w_qkv_c, rel_bias_c, w_out_c, w_ffn_gate, w_ffn_up, w_ffn_down, norm_final_g):
    bp, tp, d = x_prompt.shape
    bs, ts, _ = x_sample.shape
    depth = norm_mix_g.shape[0]
    past = cache_a_k.shape[2]
    wc = cache_c_k.shape[2]
    assert tp % ATT_BLOCK == 0 and past % ATT_BLOCK == 0 and wc % ATT_BLOCK == 0
    assert ts <= GLA_CHUNK and ts % 8 == 0
    mp, ms = bp * tp, bs * ts
    tmp, tms = _row_tile(tp), _row_tile(ms)
    xp = x_prompt.reshape(mp, d)
    xs = x_sample.reshape(ms, d)
    row2 = lambda a: a.reshape(1, -1)

    a_ks, a_vs, b_sp, b_ss, c_ks, c_vs = [], [], [], [], [], []
    a_kv_prompt = c_kv_prompt = None
    n_ab, n_c = (depth + 1) // 2, depth // 2
    keep = min(C_LEFT_CHUNKS * CHUNK, tp)

    kv0, kv1 = A_WIDTH, 3 * A_WIDTH
    o = 3 * A_WIDTH + 2 * B_KW + B_VW
    w_in_t = jnp.swapaxes(w_in_ab, 1, 2)
    w_main_all = jnp.concatenate([w_in_t[:, :kv0], w_in_t[:, kv1:o], w_in_t[:, o + B_GATE_RANK:]],
                                 axis=1).astype(BF16)
    w_kv_ab_t_all = w_in_t[:, kv0:kv1].astype(BF16)
    w_lr_all = jnp.pad(w_in_t[:, o:o + B_GATE_RANK],
                       ((0, 0), (0, LANES - B_GATE_RANK), (0, 0))).astype(BF16)
    w_gate_all = jnp.pad(w_gate_b, ((0, 0), (0, LANES - B_GATE_RANK), (0, 0))).astype(BF16)
    w_out_ab_all = w_out_ab.astype(BF16)
    w_q_all = w_qkv_c[:, :, :C_WIDTH].astype(BF16)
    w_kv_c_t_all = jnp.swapaxes(w_qkv_c[:, :, C_WIDTH:], 1, 2).astype(BF16)
    w_out_c_all = w_out_c.astype(BF16)
    cache_a_k_fm, cache_a_v_fm, cache_c_k_fm, cache_c_v_fm = (
        _feature_major(c) for c in (cache_a_k, cache_a_v, cache_c_k, cache_c_v))

    for layer in range(depth):
        i = layer // 2
        g_mix = row2(norm_mix_g[layer])
        riders = _Riders((w_ffn_gate, w_ffn_up, w_ffn_down), layer, mp // tmp)
        ffn_ends = (row2(norm_ffn_g[layer]), row2(norm_final_g) if layer == depth - 1 else None)
        if layer % 2 == 0:
            w_main, w_kv_t, w_lr, w_gate, w_out = (
                _Slab(w, i) for w in (w_main_all, w_kv_ab_t_all, w_lr_all, w_gate_all, w_out_ab_all))
            b_gate = row2(b_gate_b[i])
            g_gla = row2(norm_gla_g[i])

            qa, ka, va, kab, vab, qb, kb, vb, r, la, *ffn_w = _proj_ab(
                xp, g_mix, w_main, w_kv_t, w_lr, w_gate, b_gate, tmp, batch=bp,
                stack=_KVStack(i, n_ab, tp, a_kv_prompt), riders=riders)
            ffn = (ffn_ends[0], *ffn_w, ffn_ends[1])
            a_kv_prompt = (ka, va)
            sh = lambda a: a.reshape(bp, tp, -1)
            oa, ob, sbp = _mixer_ab_prompt(sh(qa), kab, vab, sh(qb), sh(kb), sh(vb), sh(la))
            mixed_p = (oa.reshape(mp, -1), ob.reshape(mp, -1), r)
            b_sp.append(sbp)

            qa, ka, va, kab, vab, qb, kb, vb, r, la = _proj_ab(
                xs, g_mix, w_main, w_kv_t, w_lr, w_gate, b_gate, tms)
            sh = lambda a: a.reshape(bs, ts, -1)
            oa = _sb_attention_sample(sh(qa), sh(kab), sh(vab), cache_a_k_fm, cache_a_v_fm, i)
            pad_t = lambda a: _pad_rows(sh(a), GLA_CHUNK)
            ob, sbs = _gla(pad_t(qb), pad_t(kb), pad_t(vb), pad_t(la), state_b[i])
            mixed_s = (oa.reshape(ms, -1), ob[:, :ts].reshape(ms, -1), r)
            xp, xs = _layer_tail(xp, mixed_p, xs, mixed_s, (g_gla, w_out), *ffn, tmp)
            a_ks.append(ka.reshape(bs, ts, A_HEADS, HEAD_DIM))
            a_vs.append(va.reshape(bs, ts, A_HEADS, HEAD_DIM))
            b_ss.append(sbs)
        else:
            w_q, w_kv_t, w_out = (_Slab(w, i) for w in (w_q_all, w_kv_c_t_all, w_out_c_all))

            q, k, v, kb16, vb16, *ffn_w = _proj_c(xp, g_mix, w_q, w_kv_t, tmp, batch=bp,
                                                  stack=_KVStack(i, n_c, keep, c_kv_prompt), riders=riders)
            ffn = (ffn_ends[0], *ffn_w, ffn_ends[1])
            c_kv_prompt = (k, v)
            bias = _band_bias(rel_bias_c[i])
            oc = _band_attention_prompt(q.reshape(bp, tp, -1), kb16, vb16, bias)
            mixed_p = (oc.reshape(mp, -1),)

            q, k, v, kb16, vb16 = _proj_c(xs, g_mix, w_q, w_kv_t, tms)
            sh = lambda a: a.reshape(bs, ts, -1)
            oc = _band_attention_sample(sh(q), sh(kb16), sh(vb16), cache_c_k_fm, cache_c_v_fm, i, bias)
            xp, xs = _layer_tail(xp, mixed_p, xs, (oc.reshape(ms, -1),), (w_out,), *ffn, tmp)
            c_ks.append(k.reshape(bs, ts, C_HEADS, HEAD_DIM))
            c_vs.append(v.reshape(bs, ts, C_HEADS, HEAD_DIM))

    y_prompt = xp.reshape(bp, tp, d)
    y_sample = xs.reshape(bs, ts, d)
    a_kp, a_vp = (_heads_last(a, A_HEADS) for a in a_kv_prompt)
    c_kp, c_vp = (_heads_last(a, C_HEADS) for a in c_kv_prompt)
    return (y_prompt, y_sample, a_kp, a_vp, jnp.stack(a_ks), jnp.stack(a_vs),
            jnp.stack(b_sp), jnp.stack(b_ss), c_kp, c_vp, jnp.stack(c_ks), jnp.stack(c_vs))
```

```python
import functools
import math

import jax
import jax.numpy as jnp
from jax import lax
from jax.experimental import pallas as pl
from jax.experimental.pallas import tpu as pltpu

F32 = jnp.float32
BF16 = jnp.bfloat16

EPS = 1e-6
HEAD_DIM = 64
LANES = 128
A_HEADS = 8
A_WIDTH = A_HEADS * HEAD_DIM
B_HEADS = 4
B_DK = 64
B_DV = 128
B_KW = B_HEADS * B_DK
B_VW = B_HEADS * B_DV
B_GATE_RANK = 16
B_GATE_TEMP = 16.0
GLA_CHUNK = 64
C_HEADS = 16
C_WIDTH = C_HEADS * HEAD_DIM
CHUNK = 64
C_LEFT_CHUNKS = 8
REL_MIN = -(CHUNK - 1)
REL_MAX = 128
ATT_BLOCK = 256
NEG_BIG = -1e30
LOG2E = 1.4426950408889634
SB_DEAD = 104.0
SB_SAMPLE_WINDOW = 512
SB_PAIRS_PER_STEP = 4
BAND_PAIRS_PER_STEP = 4
MIXER_BLOCKS_PER_STEP = 4
BAND_BLOCKS_PER_STEP = 8
GLA_MAX_TILE = 512
VMEM_LIMIT = 56 * 1024 * 1024


def _params(*sem):
    return pltpu.CompilerParams(dimension_semantics=sem, vmem_limit_bytes=VMEM_LIMIT)


class _Slab:
    def __init__(self, stacked, index):
        self.stacked, self.index, self.shape = stacked, index, stacked.shape[1:]


def _operand(a):
    return a.stacked if isinstance(a, _Slab) else a


def _resident(a):
    if isinstance(a, _Slab):
        index = (a.index,) + (0,) * len(a.shape)
        return pl.BlockSpec((None, *a.shape), lambda *_: index, pipeline_mode=pl.Buffered(1))
    return pl.BlockSpec(a.shape, lambda *_: (0,) * a.ndim, pipeline_mode=pl.Buffered(1))


def _rms(x, g):
    return x * lax.rsqrt(jnp.mean(x * x, axis=-1, keepdims=True) + EPS) * g


def _log_sigmoid_pair(z):
    l = jnp.log1p(jnp.exp(-jnp.abs(z)))
    return jnp.minimum(z, 0.0) - l, jnp.minimum(-z, 0.0) - l


def _split_bf16(x):
    hi = x.astype(BF16)
    lo = (x - hi.astype(F32)).astype(BF16)
    return hi, lo


def _dot(a, b):
    return jnp.dot(a, b, preferred_element_type=F32)


def _dot_nt(a, b):
    return lax.dot_general(a, b, (((1,), (1,)), ((), ())), preferred_element_type=F32)


def _dot_tn(a, b):
    return lax.dot_general(a, b, (((0,), (0,)), ((), ())), preferred_element_type=F32)


def _pair_cols(p):
    return slice(p * LANES, (p + 1) * LANES)


class _KV:
    def __init__(self, k, v, feature_major):
        self.k, self.v, self.feature_major = k, v, feature_major
        self.n_keys = k.shape[1] if feature_major else k.shape[0]

    def scores(self, q_h, p):
        if self.feature_major:
            return _dot(q_h, self.k[_pair_cols(p), :])
        return _dot_nt(q_h, self.k[:, _pair_cols(p)])

    def values(self, p):
        return self.v[_pair_cols(p), :] if self.feature_major else self.v[:, _pair_cols(p)]

    def weighted(self, w, v_p):
        return _dot_nt(w, v_p) if self.feature_major else _dot(w, v_p)

    def head_lanes(self):
        shape = (LANES, self.n_keys) if self.feature_major else (self.n_keys, LANES)
        return lax.broadcasted_iota(jnp.int32, shape, 0 if self.feature_major else 1) < HEAD_DIM


def _split_heads(q_ref, n_pairs, tq):
    lo = lax.broadcasted_iota(jnp.int32, (tq, LANES), 1) < HEAD_DIM
    heads = []
    for p in range(n_pairs):
        q = q_ref[:, _pair_cols(p)]
        heads += [jnp.where(lo, q, jnp.zeros_like(q)), jnp.where(lo, jnp.zeros_like(q), q)]
    return heads, lo


def _emit_kv(y, wkv_ref, k_ref, v_ref, kb_ref, vb_ref, feature_major, kept_tail=None):
    if feature_major:
        kv = _dot_nt(wkv_ref[...], y)
        width = kv.shape[0] // 2
        k, v = kv[:width, :], kv[width:, :]
    else:
        kv = _dot_nt(y, wkv_ref[...])
        width = kv.shape[1] // 2
        k, v = kv[:, :width], kv[:, width:]
    kb_ref[...] = k.astype(BF16)
    vb_ref[...] = v.astype(BF16)
    if kept_tail is None:
        k_ref[...] = k
        v_ref[...] = v
    else:
        n_tiles, n_kept = kept_tail
        tile = pl.program_id(0) % n_tiles
        tm = k.shape[1]
        cols = pl.ds(pl.multiple_of(jnp.maximum(tile - (n_tiles - n_kept), 0) * tm, tm), tm)
        k_ref[:, cols] = k
        v_ref[:, cols] = v


class _Riders:
    def __init__(self, stacks, layer, steps):
        self.stacks, self.layer = list(stacks), layer
        self.chunks = next(c for c in (16, 8, 4, 2, 1)
                           if c <= steps and all(w.shape[1] % (16 * c) == 0 for w in self.stacks))

    def specs(self):
        layer, last = self.layer, self.chunks - 1
        ins = [pl.BlockSpec((None, w.shape[1] // self.chunks, w.shape[2]),
                            lambda i: (layer, jnp.minimum(i, last), 0)) for w in self.stacks]
        outs = [pl.BlockSpec((w.shape[1] // self.chunks, w.shape[2]),
                             lambda i: (jnp.minimum(i, last), 0)) for w in self.stacks]
        shapes = [jax.ShapeDtypeStruct(w.shape[1:], BF16) for w in self.stacks]
        return ins, outs, shapes


def _split_rider_refs(refs, n_aliased, n_riders):
    rider_in = refs[n_aliased:n_aliased + n_riders]
    outs = refs[n_aliased + n_riders:len(refs) - n_riders]
    return outs, list(zip(rider_in, refs[len(refs) - n_riders:]))


def _cast_riders(pairs):
    for src, dst in pairs:
        dst[...] = src[...].astype(BF16)


def _proj_ab_kernel(x_ref, g_ref, w_ref, wkv_ref, wlr_ref, wgate_ref, bgate_ref, *refs,
                    feature_major, n_aliased, n_riders):
    outs, riders = _split_rider_refs(refs, n_aliased, n_riders)
    qa_ref, ka_ref, va_ref, kab_ref, vab_ref, qb_ref, kb_ref, vb_ref, r_ref, la_ref = outs
    _cast_riders(riders)
    y = _rms(x_ref[...], g_ref[...]).astype(BF16)
    _emit_kv(y, wkv_ref, ka_ref, va_ref, kab_ref, vab_ref, feature_major)
    z = _dot_nt(y, w_ref[...])
    c = 0
    qa_ref[...] = (z[:, c:c + A_WIDTH] * (HEAD_DIM ** -0.5)).astype(BF16); c += A_WIDTH
    qb_ref[...] = z[:, c:c + B_KW]; c += B_KW
    kb_ref[...] = z[:, c:c + B_KW]; c += B_KW
    vb_ref[...] = z[:, c:c + B_VW]; c += B_VW
    r_ref[...] = z[:, c:c + B_VW]
    g_lr = _dot_nt(y, wlr_ref[...])
    gate = _dot(g_lr.astype(BF16), wgate_ref[...]) + bgate_ref[...]
    la_ref[...] = _log_sigmoid_pair(gate)[0] * (1.0 / B_GATE_TEMP)


class _KVStack:
    def __init__(self, layer, n_layers, keep, previous=None):
        self.layer, self.n_layers, self.keep, self.previous = layer, n_layers, keep, previous


def _kv_out(m, width, tm, batch, stack):
    if batch is None:
        spec = pl.BlockSpec((tm, width), lambda i: (i, 0))
        return [spec] * 4, [jax.ShapeDtypeStruct((m, width), dt) for dt in (F32, F32, BF16, BF16)], None
    t = m // batch
    assert t % tm == 0 and stack.keep % tm == 0 and stack.keep <= t
    n_tiles, layer = t // tm, stack.layer
    copy_spec = pl.BlockSpec((None, width, tm), lambda i: (i // n_tiles, 0, i % n_tiles))
    copy_shape = jax.ShapeDtypeStruct((batch, width, t), BF16)
    if stack.keep == t:
        f32_spec = pl.BlockSpec((None, None, width, tm), lambda i: (layer, i // n_tiles, 0, i % n_tiles))
        kept_tail = None
    else:
        f32_spec = pl.BlockSpec((None, None, width, stack.keep), lambda i: (layer, i // n_tiles, 0, 0))
        kept_tail = (n_tiles, stack.keep // tm)
    f32_shape = jax.ShapeDtypeStruct((stack.n_layers, batch, width, stack.keep), F32)
    return [f32_spec, f32_spec, copy_spec, copy_spec], [f32_shape, f32_shape, copy_shape, copy_shape], kept_tail


def _aliased_stack(stack, n_inputs, first_output):
    if stack is None or stack.previous is None:
        return [], [], {}
    prev = list(stack.previous)
    specs = [pl.BlockSpec(memory_space=pl.ANY)] * len(prev)
    return prev, specs, {n_inputs + j: first_output + j for j in range(len(prev))}


def _rider_plumbing(riders):
    if riders is None:
        return [], [], [], []
    return (riders.stacks, *riders.specs())


def _proj_ab(x, g, w_main, w_kv, w_lr, w_gate, b_gate, tm, batch=None, stack=None, riders=None):
    m, d = x.shape
    row = lambda n: pl.BlockSpec((tm, n), lambda i: (i, 0))
    params = (g, w_main, w_kv, w_lr, w_gate, b_gate)
    kv_specs, kv_shapes, kept_tail = _kv_out(m, A_WIDTH, tm, batch, stack)
    assert kept_tail is None
    prev, prev_specs, aliases = _aliased_stack(stack, 1 + len(params), 1)
    ride_ops, ride_in, ride_out, ride_shapes = _rider_plumbing(riders)
    rest = [(B_KW, F32), (B_KW, F32), (B_VW, F32), (B_VW, F32), (B_KW, F32)]
    return pl.pallas_call(
        functools.partial(_proj_ab_kernel, feature_major=batch is not None, n_aliased=len(prev),
                          n_riders=len(ride_ops)),
        grid=(m // tm,),
        in_specs=[row(d)] + [_resident(a) for a in params] + prev_specs + ride_in,
        out_specs=[row(A_WIDTH)] + kv_specs + [row(n) for n, _ in rest] + ride_out,
        out_shape=([jax.ShapeDtypeStruct((m, A_WIDTH), BF16)] + kv_shapes
                   + [jax.ShapeDtypeStruct((m, n), dt) for n, dt in rest] + ride_shapes),
        input_output_aliases=aliases,
        compiler_params=_params("arbitrary" if ride_ops else "parallel"),
        name="proj_ab",
    )(x, *[_operand(a) for a in params], *prev, *ride_ops)


def _proj_c_kernel(x_ref, g_ref, wq_ref, wkv_ref, *refs, feature_major, n_aliased, kept_tail,
                   n_riders):
    outs, riders = _split_rider_refs(refs, n_aliased, n_riders)
    q_ref, k_ref, v_ref, kb_ref, vb_ref = outs
    _cast_riders(riders)
    y = _rms(x_ref[...], g_ref[...]).astype(BF16)
    _emit_kv(y, wkv_ref, k_ref, v_ref, kb_ref, vb_ref, feature_major, kept_tail)
    q_ref[...] = (_dot(y, wq_ref[...]) * (HEAD_DIM ** -0.5 * LOG2E)).astype(BF16)


def _proj_c(x, g, w_q, w_kv, tm, batch=None, stack=None, riders=None):
    m, d = x.shape
    row = lambda n: pl.BlockSpec((tm, n), lambda i: (i, 0))
    params = (g, w_q, w_kv)
    kv_specs, kv_shapes, kept_tail = _kv_out(m, C_WIDTH, tm, batch, stack)
    prev, prev_specs, aliases = _aliased_stack(stack, 1 + len(params), 1)
    ride_ops, ride_in, ride_out, ride_shapes = _rider_plumbing(riders)
    return pl.pallas_call(
        functools.partial(_proj_c_kernel, feature_major=batch is not None, n_aliased=len(prev),
                          kept_tail=kept_tail, n_riders=len(ride_ops)),
        grid=(m // tm,),
        in_specs=[row(d)] + [_resident(a) for a in params] + prev_specs + ride_in,
        out_specs=[row(C_WIDTH)] + kv_specs + ride_out,
        out_shape=[jax.ShapeDtypeStruct((m, C_WIDTH), BF16)] + kv_shapes + ride_shapes,
        input_output_aliases=aliases,
        compiler_params=_params("arbitrary" if (kept_tail or ride_ops) else "parallel"),
        name="proj_c",
    )(x, *[_operand(a) for a in params], *prev, *ride_ops)


def _layer_tail_rows(x_ref, mixer_refs, param_refs, o_ref, gla_merge, final_norm):
    if gla_merge:
        oa_ref, ob_ref, r_ref = mixer_refs
        ggla_ref, wout_ref, *param_refs = param_refs
        ob = ob_ref[...]
        parts = []
        for h in range(B_HEADS):
            seg = ob[:, h * B_DV:(h + 1) * B_DV]
            parts.append(seg * lax.rsqrt(jnp.mean(seg * seg, axis=-1, keepdims=True) + EPS))
        r = r_ref[...]
        obn = jnp.concatenate(parts, axis=-1) * ggla_ref[...] * (r * jax.nn.sigmoid(r))
        mix = _dot(oa_ref[...], wout_ref[:A_WIDTH, :]) + _dot(obn.astype(BF16), wout_ref[A_WIDTH:, :])
    else:
        (oc_ref,) = mixer_refs
        wout_ref, *param_refs = param_refs
        mix = _dot(oc_ref[...], wout_ref[...])
    gffn_ref, wg_ref, wu_ref, wd_ref, *param_refs = param_refs
    x = x_ref[...] + mix
    y = _rms(x, gffn_ref[...]).astype(BF16)
    h = _dot(y, wg_ref[...])
    u = _dot(y, wu_ref[...])
    a = (h * jax.nn.sigmoid(h) * u).astype(BF16)
    x = x + _dot(a, wd_ref[...])
    o_ref[...] = _rms(x, param_refs[0][...]) if final_norm else x


def _layer_tail_kernel(*refs, n_mixer, n_params, gla_merge, final_norm, main_steps):
    main, extra = refs[:1 + n_mixer], refs[1 + n_mixer:2 + 2 * n_mixer]
    params = refs[2 + 2 * n_mixer:2 + 2 * n_mixer + n_params]
    o_ref, o_extra_ref = refs[-2:]

    @pl.when(pl.program_id(0) < main_steps)
    def _():
        _layer_tail_rows(main[0], main[1:], params, o_ref, gla_merge, final_norm)

    @pl.when(pl.program_id(0) == main_steps)
    def _():
        _layer_tail_rows(extra[0], extra[1:], params, o_extra_ref, gla_merge, final_norm)


def _layer_tail(x, mixer_out, x_extra, mixer_out_extra, mixer_params, g_ffn, wg, wu, wd, g_fin, tm):
    m, d = x.shape
    main_steps = m // tm
    row = lambda a: pl.BlockSpec((tm, a.shape[1]), lambda i: (jnp.minimum(i, main_steps - 1), 0))
    whole = lambda a: pl.BlockSpec(a.shape, lambda i: (0, 0))
    params = [*mixer_params, g_ffn, wg, wu, wd] + ([] if g_fin is None else [g_fin])
    return pl.pallas_call(
        functools.partial(_layer_tail_kernel, n_mixer=len(mixer_out), n_params=len(params),
                          gla_merge=len(mixer_out) == 3, final_norm=g_fin is not None, main_steps=main_steps),
        grid=(main_steps + 1,),
        in_specs=([row(x)] + [row(a) for a in mixer_out] + [whole(x_extra)]
                  + [whole(a) for a in mixer_out_extra] + [_resident(a) for a in params]),
        out_specs=[row(x), whole(x_extra)],
        out_shape=[jax.ShapeDtypeStruct((m, d), F32), jax.ShapeDtypeStruct(x_extra.shape, F32)],
        compiler_params=_params("arbitrary"),
        name="layer_tail",
    )(x, *mixer_out, x_extra, *mixer_out_extra, *[_operand(a) for a in params])


def _interleave(stage_generators):
    results = [None] * len(stage_generators)
    live = list(range(len(stage_generators)))
    while live:
        for i in list(live):
            try:
                next(stage_generators[i])
            except StopIteration as done:
                results[i] = done.value
                live.remove(i)
    return results


def _sb_core(q_heads, lo, first_kv, first_mask, earlier_kv, n_earlier, acc_ref, c_ref, o_ref, n_pairs,
             companions=(), on_companions=None):
    tk = first_kv.n_keys
    heads = range(2 * n_pairs)
    later = (lax.broadcasted_iota(jnp.int32, (tk, tk), 0)
             > lax.broadcasted_iota(jnp.int32, (tk, tk), 1)).astype(BF16)
    c_ref[...] = jnp.zeros_like(c_ref)

    def block(kv, mask):
        z = [kv.scores(q_heads[h], h // 2) for h in heads]
        yield
        log_beta, drop, after = [], [], []
        for h in heads:
            d = jnp.maximum(z[h], 0.0) + jnp.log(1.0 + jnp.exp(-jnp.abs(z[h])))
            log_beta.append(z[h] - d)
            drop.append(d if mask is None else jnp.where(mask, d, 0.0))
        yield
        for h in heads:
            hi, lo_part = _split_bf16(drop[h])
            after.append(_dot(hi, later) + _dot(lo_part, later))
        yield
        pv = []
        for h in heads:
            c = c_ref[h]
            w = jnp.exp(log_beta[h] - after[h] - c)
            if mask is not None:
                w = jnp.where(mask, w, 0.0)
            c_ref[h] = c + after[h][:, 0:1] + drop[h][:, 0:1]
            pv.append(kv.weighted(w.astype(BF16), kv.values(h // 2)))
        yield
        out = [jnp.where(lo, pv[2 * p], pv[2 * p + 1]) for p in range(n_pairs)]
        return out[0] if n_pairs == 1 else jnp.concatenate(out, axis=-1)

    def all_dead():
        return jnp.min(c_ref[...]) > SB_DEAD

    first, *companion_results = _interleave([block(first_kv, first_mask), *companions])
    acc_ref[...] = first
    if companions:
        on_companions(companion_results)

    def cond(carry):
        n, dead = carry
        return (n < n_earlier) & jnp.logical_not(dead)

    def body(carry):
        n, _ = carry
        acc_ref[...] += _interleave([block(earlier_kv(n), None)])[0]
        return n + 1, all_dead()

    lax.while_loop(cond, body, (jnp.int32(0), all_dead()))
    o_ref[...] = acc_ref[...].astype(o_ref.dtype)


def _mixer_ab_prompt_kernel(q_ref, k_ref, v_ref, qb_ref, kb_ref, vb_ref, la_ref, o_ref, ob_ref, s_out_ref,
                            acc_ref, c_ref, carry_ref, *, tb, n_pairs, blocks_per_step):
    pairs = range(B_HEADS // 2)
    strictly_earlier = (lax.broadcasted_iota(jnp.int32, (tb, tb), 1)
                        < lax.broadcasted_iota(jnp.int32, (tb, tb), 0))

    def kv_block(j):
        keys = pl.ds(pl.multiple_of(j * tb, tb), tb)
        return _KV(k_ref[:, keys], v_ref[:, keys], True)

    def keep_states(states):
        for p in pairs:
            carry_ref[2 * p], carry_ref[2 * p + 1] = states[p]
            s_out_ref[2 * p] = states[p][0][:B_DK, :]
            s_out_ref[2 * p + 1] = states[p][1][B_DK:, :]

    @pl.when(pl.program_id(1) == 0)
    def _():
        carry_ref[...] = jnp.zeros_like(carry_ref)

    def one_block(s, carry):
        qi = pl.program_id(1) * blocks_per_step + s
        rows = pl.ds(pl.multiple_of(s * tb, tb), tb)
        gla = [_gla_pair_tile(qb_ref[rows, _pair_cols(p)], kb_ref[rows, _pair_cols(p)],
                              vb_ref[rows, _pair_v_cols(p)], la_ref[rows, _pair_cols(p)],
                              [carry_ref[2 * p], carry_ref[2 * p + 1]],
                              _gla_rows_emit(ob_ref.at[rows, :], p)) for p in pairs]
        q_heads, lo = _split_heads(q_ref.at[rows, :], n_pairs, tb)
        _sb_core(q_heads, lo, kv_block(qi), strictly_earlier, lambda n: kv_block(qi - 1 - n), qi,
                 acc_ref, c_ref, o_ref.at[rows, :], n_pairs, companions=gla, on_companions=keep_states)
        return carry

    lax.fori_loop(0, blocks_per_step, one_block, 0)


def _sb_sample_kernel(q_ref, kn_ref, vn_ref, kw_ref, vw_ref, kc_hbm, vc_hbm, o_ref,
                      acc_ref, c_ref, kpad_ref, vpad_ref, kbuf_ref, vbuf_ref, sems,
                      *, ts, tk, n_cache_blocks, n_window_blocks, n_pairs, layer):
    q_heads, lo = _split_heads(q_ref, n_pairs, ts)
    kpad_ref[...] = jnp.zeros_like(kpad_ref)
    vpad_ref[...] = jnp.zeros_like(vpad_ref)
    kpad_ref[:ts, :] = kn_ref[...]
    vpad_ref[:ts, :] = vn_ref[...]

    def from_window(n):
        keys = pl.ds(pl.multiple_of((n_window_blocks - 1 - n) * tk, tk), tk)
        return kw_ref[:, keys], vw_ref[:, keys]

    def from_hbm(n):
        width = kbuf_ref.shape[0]
        keys = pl.ds(pl.multiple_of((n_cache_blocks - 1 - n) * tk, tk), tk)
        feats = pl.ds(pl.multiple_of(pl.program_id(1) * width, width), width)
        copies = [pltpu.make_async_copy(src.at[layer, pl.program_id(0), feats, keys], dst, sems.at[i])
                  for i, (src, dst) in enumerate(((kc_hbm, kbuf_ref), (vc_hbm, vbuf_ref)))]
        for copy in copies:
            copy.start()
        for copy in copies:
            copy.wait()
        return kbuf_ref[...], vbuf_ref[...]

    def cache_block(n):
        if n_window_blocks == n_cache_blocks:
            k, v = from_window(n)
        else:
            k, v = lax.cond(n < n_window_blocks, from_window, from_hbm, n)
        return _KV(k.astype(BF16), v.astype(BF16), True)

    strictly_earlier = (lax.broadcasted_iota(jnp.int32, (ts, tk), 1)
                        < lax.broadcasted_iota(jnp.int32, (ts, tk), 0))
    _sb_core(q_heads, lo, _KV(kpad_ref[...], vpad_ref[...], False), strictly_earlier, cache_block,
             n_cache_blocks, acc_ref, c_ref, o_ref, n_pairs)


def _mixer_ab_prompt(q, k, v, qb, kb, vb, la):
    b, t, _ = q.shape
    tb = ATT_BLOCK
    assert t % tb == 0 and tb % GLA_CHUNK == 0 and tb <= GLA_MAX_TILE
    n_pairs = A_WIDTH // LANES
    per_step = math.gcd(t // tb, MIXER_BLOCKS_PER_STEP)
    rows = lambda width: pl.BlockSpec((None, per_step * tb, width), lambda bi, qi: (bi, qi, 0))
    kv_spec = pl.BlockSpec((None, A_WIDTH, t), lambda bi, qi: (bi, 0, 0))
    state = (B_HEADS, B_DK, B_DV)
    return pl.pallas_call(
        functools.partial(_mixer_ab_prompt_kernel, tb=tb, n_pairs=n_pairs, blocks_per_step=per_step),
        grid=(b, t // (per_step * tb)),
        in_specs=[rows(A_WIDTH), kv_spec, kv_spec, rows(B_KW), rows(B_KW), rows(B_VW), rows(B_KW)],
        out_specs=[rows(A_WIDTH), rows(B_VW), pl.BlockSpec((None, *state), lambda bi, qi: (bi, 0, 0, 0))],
        out_shape=[jax.ShapeDtypeStruct(q.shape, BF16), jax.ShapeDtypeStruct(vb.shape, F32),
                   jax.ShapeDtypeStruct((b, *state), F32)],
        scratch_shapes=[pltpu.VMEM((tb, A_WIDTH), F32), pltpu.VMEM((2 * n_pairs, tb, 1), F32),
                        pltpu.VMEM((B_HEADS, LANES, B_DV), F32)],
        compiler_params=_params("parallel", "arbitrary"),
        name="mixer_ab",
    )(q, k, v, qb, kb, vb, la)


def _sb_attention_sample(q, k_new, v_new, k_cache, v_cache, layer):
    b, ts, _ = q.shape
    past = k_cache.shape[3]
    tk = ATT_BLOCK
    assert past % tk == 0 and ts <= tk
    n_pairs = SB_PAIRS_PER_STEP
    w = n_pairs * LANES
    window = math.gcd(past, SB_SAMPLE_WINDOW)
    assert window % tk == 0
    new_spec = pl.BlockSpec((None, ts, w), lambda bi, hp: (bi, 0, hp))
    window_spec = pl.BlockSpec((None, None, w, window), lambda bi, hp: (layer, bi, hp, past // window - 1))
    in_hbm = pl.BlockSpec(memory_space=pl.ANY)
    return pl.pallas_call(
        functools.partial(_sb_sample_kernel, ts=ts, tk=tk, n_cache_blocks=past // tk,
                          n_window_blocks=window // tk, n_pairs=n_pairs, layer=layer),
        grid=(b, A_WIDTH // w),
        in_specs=[new_spec, new_spec, new_spec, window_spec, window_spec, in_hbm, in_hbm],
        out_specs=new_spec,
        out_shape=jax.ShapeDtypeStruct(q.shape, BF16),
        scratch_shapes=[pltpu.VMEM((ts, w), F32), pltpu.VMEM((2 * n_pairs, ts, 1), F32),
                        pltpu.VMEM((tk, w), BF16), pltpu.VMEM((tk, w), BF16),
                        pltpu.VMEM((w, tk), F32), pltpu.VMEM((w, tk), F32), pltpu.SemaphoreType.DMA((2,))],
        compiler_params=_params("parallel", "parallel"),
        name="sb_attention_sample",
    )(q, k_new, v_new, k_cache, v_cache, k_cache, v_cache)


def _band_core(q_heads, lo, kv, pens, bias, o_ref, n_pairs):
    heads = range(2 * n_pairs)
    blocks = range(len(kv))
    z = [[kv[i].scores(q_heads[h], h // 2) + bias(h, i) for i in blocks] for h in heads]
    acc = []
    for h in heads:
        m = None
        for i in blocks:
            mi = jnp.max(z[h][i], axis=-1, keepdims=True)
            if pens[i] is not None:
                mi = mi + pens[i]
            m = mi if m is None else jnp.maximum(m, mi)
        a = None
        for i in blocks:
            shift = m if pens[i] is None else m - pens[i]
            p = jnp.exp2(z[h][i] - shift).astype(BF16)
            v = kv[i].values(h // 2)
            ones = jnp.ones_like(v)
            first = kv[i].head_lanes()
            v = jnp.where(first, v, ones) if h % 2 == 0 else jnp.where(first, ones, v)
            pv = kv[i].weighted(p, v)
            a = pv if a is None else a + pv
        acc.append(a)
    for p in range(n_pairs):
        a0, a1 = acc[2 * p], acc[2 * p + 1]
        o_ref[:, _pair_cols(p)] = jnp.where(lo, a0 / pltpu.roll(a0, HEAD_DIM, axis=1),
                                            a1 / pltpu.roll(a1, HEAD_DIM, axis=1)).astype(o_ref.dtype)


def _band_prompt_kernel(q_ref, k_ref, v_ref, bias_ref, o_ref, *, tq, tk, n_pairs, blocks_per_step):
    def one_block(s, carry):
        qi = pl.program_id(2) * blocks_per_step + s
        rows = pl.ds(pl.multiple_of(s * tq, tq), tq)
        q_heads, lo = _split_heads(q_ref.at[rows, :], n_pairs, tq)
        kv, pens = [], []
        for dj in range(3):
            j = qi - dj
            pens.append(None if dj == 0 else jnp.where(j >= 0, 0.0, NEG_BIG).astype(F32))
            keys = pl.ds(pl.multiple_of(jnp.maximum(j, 0) * tk, tk), tk)
            kv.append(_KV(k_ref[:, keys], v_ref[:, keys], True))
        _band_core(q_heads, lo, kv, pens, lambda h, i: bias_ref[h, i], o_ref.at[rows, :], n_pairs)
        return carry

    lax.fori_loop(0, blocks_per_step, one_block, 0)


def _band_sample_kernel(q_ref, kn_ref, vn_ref, kc_ref, vc_ref, bias_ref, o_ref, kpad_ref, vpad_ref,
                        *, ts, tk, n_cache_blocks, n_pairs):
    q_heads, lo = _split_heads(q_ref, n_pairs, ts)
    kpad_ref[...] = jnp.zeros_like(kpad_ref)
    vpad_ref[...] = jnp.zeros_like(vpad_ref)
    kpad_ref[:ts, :] = kn_ref[...]
    vpad_ref[:ts, :] = vn_ref[...]
    kv = [_KV(kpad_ref[...], vpad_ref[...], False)]
    for dj in range(1, n_cache_blocks + 1):
        keys = slice((n_cache_blocks - dj) * tk, (n_cache_blocks - dj + 1) * tk)
        kv.append(_KV(kc_ref[:, keys].astype(BF16), vc_ref[:, keys].astype(BF16), True))
    is_new_key = lax.broadcasted_iota(jnp.int32, (ts, tk), 1) < ts

    def bias(h, i):
        return jnp.where(is_new_key, bias_ref[h, 0], NEG_BIG) if i == 0 else bias_ref[h, i]

    _band_core(q_heads, lo, kv, [None] * len(kv), bias, o_ref, n_pairs)


def _band_attention_prompt(q, k, v, bias):
    b, t, _ = q.shape
    tq = tk = ATT_BLOCK
    assert t % tk == 0
    n_pairs = BAND_PAIRS_PER_STEP
    w = n_pairs * LANES
    per_step = math.gcd(t // tq, BAND_BLOCKS_PER_STEP)
    kern = functools.partial(_band_prompt_kernel, tq=tq, tk=tk, n_pairs=n_pairs, blocks_per_step=per_step)
    kv_spec = pl.BlockSpec((None, w, t), lambda bi, hp, qi: (bi, hp, 0))
    q_spec = pl.BlockSpec((None, per_step * tq, w), lambda bi, hp, qi: (bi, qi, hp))
    bias_spec = pl.BlockSpec((2 * n_pairs, 3, tq, tk), lambda bi, hp, qi: (hp, 0, 0, 0))
    return pl.pallas_call(
        kern,
        grid=(b, C_WIDTH // w, t // (per_step * tq)),
        in_specs=[q_spec, kv_spec, kv_spec, bias_spec],
        out_specs=q_spec,
        out_shape=jax.ShapeDtypeStruct(q.shape, BF16),
        compiler_params=_params("parallel", "parallel", "parallel"),
        name="band_attention",
    )(q, k, v, bias)


def _band_attention_sample(q, k_new, v_new, k_cache, v_cache, layer, bias):
    b, ts, _ = q.shape
    wc = k_cache.shape[3]
    tk = ATT_BLOCK
    n_cache_blocks = min(wc // tk, 2)
    assert wc % (n_cache_blocks * tk) == 0 and ts <= tk
    n_pairs = BAND_PAIRS_PER_STEP
    w = n_pairs * LANES
    kern = functools.partial(_band_sample_kernel, ts=ts, tk=tk, n_cache_blocks=n_cache_blocks,
                             n_pairs=n_pairs)
    new_spec = pl.BlockSpec((None, ts, w), lambda bi, hp: (bi, 0, hp))
    cache_rows = n_cache_blocks * tk
    cache_spec = pl.BlockSpec((None, None, w, cache_rows),
                              lambda bi, hp: (layer, bi, hp, wc // cache_rows - 1))
    bias_spec = pl.BlockSpec((2 * n_pairs, 1 + n_cache_blocks, ts, tk), lambda bi, hp: (hp, 0, 0, 0))
    return pl.pallas_call(
        kern,
        grid=(b, C_WIDTH // w),
        in_specs=[new_spec, new_spec, new_spec, cache_spec, cache_spec, bias_spec],
        out_specs=new_spec,
        out_shape=jax.ShapeDtypeStruct(q.shape, BF16),
        scratch_shapes=[pltpu.VMEM((tk, w), BF16), pltpu.VMEM((tk, w), BF16)],
        compiler_params=_params("parallel", "parallel"),
        name="band_attention_sample",
    )(q, k_new, v_new, k_cache, v_cache, bias)


def _band_bias_kernel(g_ref, o_ref):
    rows = tk = ATT_BLOCK
    q_chunk = lax.broadcasted_iota(jnp.int32, (rows, tk), 0) // CHUNK
    k_chunk = lax.broadcasted_iota(jnp.int32, (rows, tk), 1) // CHUNK
    for dj in range(3):
        g = jnp.broadcast_to(g_ref[0, dj], (rows, 2 * tk))
        tile = pltpu.roll(g, 0, axis=1, stride=1, stride_axis=0)[:, :tk]
        diff = dj * (tk // CHUNK) + q_chunk - k_chunk
        seen = (diff >= 0) & (diff <= C_LEFT_CHUNKS)
        o_ref[0, dj] = jnp.where(seen, tile * LOG2E, NEG_BIG)


def _band_bias(rel_table):
    rows = tk = ATT_BLOCK
    c = jnp.arange(2 * tk, dtype=jnp.int32)
    u = jnp.where(c <= tk, -c, 2 * tk - c)
    idx = jnp.clip(jnp.arange(3, dtype=jnp.int32)[:, None] * tk + u[None, :], REL_MIN, REL_MAX) - REL_MIN
    g = rel_table[:, idx].astype(F32).reshape(C_HEADS, 3, 1, 2 * tk)
    return pl.pallas_call(
        _band_bias_kernel,
        grid=(C_HEADS,),
        in_specs=[pl.BlockSpec((1, 3, 1, 2 * tk), lambda h: (h, 0, 0, 0))],
        out_specs=pl.BlockSpec((1, 3, rows, tk), lambda h: (h, 0, 0, 0)),
        out_shape=jax.ShapeDtypeStruct((C_HEADS, 3, rows, tk), F32),
        compiler_params=_params("parallel"),
        name="band_bias",
    )(g)


def _gla_pair_tile(q, k, v, la, states, emit):
    L = GLA_CHUNK
    chunks = range(q.shape[0] // L)
    row = lax.broadcasted_iota(jnp.int32, (L, L), 0)
    colm = lax.broadcasted_iota(jnp.int32, (L, L), 1)
    tri = (colm <= row).astype(BF16)
    causal = colm <= row
    lane = lax.broadcasted_iota(jnp.int32, (L, LANES), 1)
    sub = lax.broadcasted_iota(jnp.int32, (LANES, B_DV), 0)
    mine = [(lane >= h * B_DK) & (lane < (h + 1) * B_DK) for h in range(2)]
    mine_rows = [(sub >= h * B_DK) & (sub < (h + 1) * B_DK) for h in range(2)]
    rows = [slice(c * L, (c + 1) * L) for c in chunks]

    b = []
    for r in rows:
        g_hi, g_lo = _split_bf16(la[r, :])
        b.append(_dot(tri, g_hi) + _dot(tri, g_lo))
    yield
    qg, qg_h, kg, kd_t, decay = [], [], [], [], []
    for c, r in zip(chunks, rows):
        qg_c = q[r, :] * (B_DK ** -0.5) * jnp.exp(b[c])
        qg.append(qg_c.astype(BF16))
        qg_h.append([jnp.where(mine[h], qg_c, 0.0).astype(BF16) for h in range(2)])
        kg.append((k[r, :] * jnp.exp(-b[c])).astype(BF16))
        b_t = b[c].T
        b_last = b_t[:, L - 1:L]
        kd_t.append((k[r, :].T * jnp.exp(b_last - b_t)).astype(BF16))
        decay.append(jnp.exp(b_last))
    yield
    att =[[jnp.where(causal, _dot_nt(qg_h[c][h], kg[c]), 0.0).astype(BF16) for h in range(2)]
           for c in chunks]
    yield
    o_intra, own = [], []
    for c, r in zip(chunks, rows):
        v_h = [v[r, h * B_DV:(h + 1) * B_DV].astype(BF16) for h in range(2)]
        o_intra.append([_dot(att[c][h], v_h[h]) for h in range(2)])
        own.append([jnp.where(mine_rows[h], _dot(kd_t[c], v_h[h]), 0.0) for h in range(2)])

    yield
    states = list(states)
    start = []
    for c in chunks:
        start.append([s.astype(BF16) for s in states])
        states = [decay[c] * states[h] + own[c][h] for h in range(2)]
    yield
    for c in chunks:
        for h in range(2):
            emit(c, h, o_intra[c][h] + _dot(qg[c], start[c][h]))
    return states


def _pair_states(s0_ref, p):
    zeros_state = jnp.zeros((B_DK, B_DV), F32)
    return [jnp.concatenate([s0_ref[2 * p], zeros_state], axis=0),
            jnp.concatenate([zeros_state, s0_ref[2 * p + 1]], axis=0)]


def _pair_v_cols(p):
    return slice(2 * p * B_DV, 2 * (p + 1) * B_DV)


def _gla_rows_emit(o_ref, p):
    def emit(c, h, o):
        o_ref[c * GLA_CHUNK:(c + 1) * GLA_CHUNK, (2 * p + h) * B_DV:(2 * p + h + 1) * B_DV] = o
    return emit


def _gla_kernel(q_ref, k_ref, v_ref, la_ref, s0_ref, o_ref, s_out_ref):
    pairs = range(B_HEADS // 2)
    states = _interleave([
        _gla_pair_tile(q_ref[:, _pair_cols(p)], k_ref[:, _pair_cols(p)], v_ref[:, _pair_v_cols(p)],
                       la_ref[:, _pair_cols(p)], _pair_states(s0_ref, p), _gla_rows_emit(o_ref, p))
        for p in pairs])
    for p in pairs:
        s_out_ref[2 * p] = states[p][0][:B_DK, :]
        s_out_ref[2 * p + 1] = states[p][1][B_DK:, :]


def _gla(q, k, v, la, s0):
    b, t, _ = q.shape
    assert t % GLA_CHUNK == 0 and t <= GLA_MAX_TILE
    qk_spec = pl.BlockSpec((None, t, B_KW), lambda bi: (bi, 0, 0))
    v_spec = pl.BlockSpec((None, t, B_VW), lambda bi: (bi, 0, 0))
    s_spec = pl.BlockSpec((None, B_HEADS, B_DK, B_DV), lambda bi: (bi, 0, 0, 0))
    return pl.pallas_call(
        _gla_kernel,
        grid=(b,),
        in_specs=[qk_spec, qk_spec, v_spec, qk_spec, s_spec],
        out_specs=[v_spec, s_spec],
        out_shape=[jax.ShapeDtypeStruct(v.shape, F32), jax.ShapeDtypeStruct(s0.shape, F32)],
        compiler_params=_params("parallel"),
        name="gla",
    )(q, k, v, la, s0)


def _pad_rows(x, n):
    return jnp.pad(x, ((0, 0), (0, n - x.shape[1]), (0, 0)))


def _heads_last(x, heads):
    n, b, _, s = x.shape
    return jnp.transpose(x.reshape(n, b, heads, HEAD_DIM, s), (0, 1, 4, 2, 3))


def _feature_major(cache):
    n, b, s, heads, hd = cache.shape
    return jnp.transpose(cache, (0, 1, 3, 4, 2)).reshape(n, b, heads * hd, s)


def _row_tile(m):
    for tm in (512, 256, 128, 64, 32, 16, 8):
        if m % tm == 0:
            return tm
    raise ValueError(f"token count {m} is not a multiple of 8")


def kernel(x_prompt, x_sample, cache_a_k, cache_a_v, state_b, cache_c_k, cache_c_v, norm_mix_g, norm_ffn_g, w_in_ab, w_gate_b, b_gate_b, norm_gla_g, w_out_ab, w_qkv_c, rel_bias_c, w_out_c, w_ffn_gate, w_ffn_up, w_ffn_down, norm_final_g):
    bp, tp, d = x_prompt.shape
    bs, ts, _ = x_sample.shape
    depth = norm_mix_g.shape[0]
    past = cache_a_k.shape[2]
    wc = cache_c_k.shape[2]
    assert tp % ATT_BLOCK == 0 and past % ATT_BLOCK == 0 and wc % ATT_BLOCK == 0
    assert ts <= GLA_CHUNK and ts % 8 == 0
    mp, ms = bp * tp, bs * ts
    tmp, tms = _row_tile(tp), _row_tile(ms)
    xp = x_prompt.reshape(mp, d)
    xs = x_sample.reshape(ms, d)
    row2 = lambda a: a.reshape(1, -1)

    a_ks, a_vs, b_sp, b_ss, c_ks, c_vs = [], [], [], [], [], []
    a_kv_prompt = c_kv_prompt = None
    n_ab, n_c = (depth + 1) // 2, depth // 2
    keep = min(C_LEFT_CHUNKS * CHUNK, tp)

    kv0, kv1 = A_WIDTH, 3 * A_WIDTH
    o = 3 * A_WIDTH + 2 * B_KW + B_VW
    w_in_t = jnp.swapaxes(w_in_ab, 1, 2)
    w_main_all = jnp.concatenate([w_in_t[:, :kv0], w_in_t[:, kv1:o], w_in_t[:, o + B_GATE_RANK:]],
                                 axis=1).astype(BF16)
    w_kv_ab_t_all = w_in_t[:, kv0:kv1].astype(BF16)
    w_lr_all = jnp.pad(w_in_t[:, o:o + B_GATE_RANK],
                       ((0, 0), (0, LANES - B_GATE_RANK), (0, 0))).astype(BF16)
    w_gate_all = jnp.pad(w_gate_b, ((0, 0), (0, LANES - B_GATE_RANK), (0, 0))).astype(BF16)
    w_out_ab_all = w_out_ab.astype(BF16)
    w_q_all = w_qkv_c[:, :, :C_WIDTH].astype(BF16)
    w_kv_c_t_all = jnp.swapaxes(w_qkv_c[:, :, C_WIDTH:], 1, 2).astype(BF16)
    w_out_c_all = w_out_c.astype(BF16)
    cache_a_k_fm, cache_a_v_fm, cache_c_k_fm, cache_c_v_fm = (
        _feature_major(c) for c in (cache_a_k, cache_a_v, cache_c_k, cache_c_v))

    for layer in range(depth):
        i = layer // 2
        g_mix = row2(norm_mix_g[layer])
        riders = _Riders((w_ffn_gate, w_ffn_up, w_ffn_down), layer, mp // tmp)
        ffn_ends = (row2(norm_ffn_g[layer]), row2(norm_final_g) if layer == depth - 1 else None)
        if layer % 2 == 0:
            w_main, w_kv_t, w_lr, w_gate, w_out = (
                _Slab(w, i) for w in (w_main_all, w_kv_ab_t_all, w_lr_all, w_gate_all, w_out_ab_all))
            b_gate = row2(b_gate_b[i])
            g_gla = row2(norm_gla_g[i])

            qa, ka, va, kab, vab, qb, kb, vb, r, la, *ffn_w = _proj_ab(
                xp, g_mix, w_main, w_kv_t, w_lr, w_gate, b_gate, tmp, batch=bp,
                stack=_KVStack(i, n_ab, tp, a_kv_prompt), riders=riders)
            ffn = (ffn_ends[0], *ffn_w, ffn_ends[1])
            a_kv_prompt = (ka, va)
            sh = lambda a: a.reshape(bp, tp, -1)
            oa, ob, sbp = _mixer_ab_prompt(sh(qa), kab, vab, sh(qb), sh(kb), sh(vb), sh(la))
            mixed_p = (oa.reshape(mp, -1), ob.reshape(mp, -1), r)
            b_sp.append(sbp)

            qa, ka, va, kab, vab, qb, kb, vb, r, la = _proj_ab(
                xs, g_mix, w_main, w_kv_t, w_lr, w_gate, b_gate, tms)
            sh = lambda a: a.reshape(bs, ts, -1)
            oa = _sb_attention_sample(sh(qa), sh(kab), sh(vab), cache_a_k_fm, cache_a_v_fm, i)
            pad_t = lambda a: _pad_rows(sh(a), GLA_CHUNK)
            ob, sbs = _gla(pad_t(qb), pad_t(kb), pad_t(vb), pad_t(la), state_b[i])
            mixed_s = (oa.reshape(ms, -1), ob[:, :ts].reshape(ms, -1), r)
            xp, xs = _layer_tail(xp, mixed_p, xs, mixed_s, (g_gla, w_out), *ffn, tmp)
            a_ks.append(ka.reshape(bs, ts, A_HEADS, HEAD_DIM))
            a_vs.append(va.reshape(bs, ts, A_HEADS, HEAD_DIM))
            b_ss.append(sbs)
        else:
            w_q, w_kv_t, w_out = (_Slab(w, i) for w in (w_q_all, w_kv_c_t_all, w_out_c_all))

            q, k, v, kb16, vb16, *ffn_w = _proj_c(xp, g_mix, w_q, w_kv_t, tmp, batch=bp,
                                                  stack=_KVStack(i, n_c, keep, c_kv_prompt), riders=riders)
            ffn = (ffn_ends[0], *ffn_w, ffn_ends[1])
            c_kv_prompt = (k, v)
            bias = _band_bias(rel_bias_c[i])
            oc = _band_attention_prompt(q.reshape(bp, tp, -1), kb16, vb16, bias)
            mixed_p = (oc.reshape(mp, -1),)

            q, k, v, kb16, vb16 = _proj_c(xs, g_mix, w_q, w_kv_t, tms)
            sh = lambda a: a.reshape(bs, ts, -1)
            oc = _band_attention_sample(sh(q), sh(kb16), sh(vb16), cache_c_k_fm, cache_c_v_fm, i, bias)
            xp, xs = _layer_tail(xp, mixed_p, xs, (oc.reshape(ms, -1),), (w_out,), *ffn, tmp)
            c_ks.append(k.reshape(bs, ts, C_HEADS, HEAD_DIM))
            c_vs.append(v.reshape(bs, ts, C_HEADS, HEAD_DIM))

    y_prompt = xp.reshape(bp, tp, d)
    y_sample = xs.reshape(bs, ts, d)
    a_kp, a_vp = (_heads_last(a, A_HEADS) for a in a_kv_prompt)
    c_kp, c_vp = (_heads_last(a, C_HEADS) for a in c_kv_prompt)
    return (y_prompt, y_sample, a_kp, a_vp, jnp.stack(a_ks), jnp.stack(a_vs),
            jnp.stack(b_sp), jnp.stack(b_ss), c_kp, c_vp, jnp.stack(c_ks), jnp.stack(c_vs))
```

```python
import functools
import math

import jax
import jax.numpy as jnp
from jax import lax
from jax.experimental import pallas as pl
from jax.experimental.pallas import tpu as pltpu

F32 = jnp.float32
BF16 = jnp.bfloat16

EPS = 1e-6
HEAD_DIM = 64
LANES = 128
A_HEADS = 8
A_WIDTH = A_HEADS * HEAD_DIM
B_HEADS = 4
B_DK = 64
B_DV = 128
B_KW = B_HEADS * B_DK
B_VW = B_HEADS * B_DV
B_GATE_RANK = 16
B_GATE_TEMP = 16.0
GLA_CHUNK = 64
C_HEADS = 16
C_WIDTH = C_HEADS * HEAD_DIM
CHUNK = 64
C_LEFT_CHUNKS = 8
REL_MIN = -(CHUNK - 1)
REL_MAX = 128
ATT_BLOCK = 256
NEG_BIG = -1e30
LOG2E = 1.4426950408889634
SB_DEAD = 104.0
SB_PAIRS_PER_STEP = 4
BAND_PAIRS_PER_STEP = 4
MIXER_BLOCKS_PER_STEP = 4
BAND_BLOCKS_PER_STEP = 8
GLA_MAX_TILE = 512
VMEM_LIMIT = 56 * 1024 * 1024


def _params(*sem):
    return pltpu.CompilerParams(dimension_semantics=sem, vmem_limit_bytes=VMEM_LIMIT)


class _Slab:
    def __init__(self, stacked, index):
        self.stacked, self.index, self.shape = stacked, index, stacked.shape[1:]


def _operand(a):
    return a.stacked if isinstance(a, _Slab) else a


def _resident(a):
    if isinstance(a, _Slab):
        index = (a.index,) + (0,) * len(a.shape)
        return pl.BlockSpec((None, *a.shape), lambda *_: index, pipeline_mode=pl.Buffered(1))
    return pl.BlockSpec(a.shape, lambda *_: (0,) * a.ndim, pipeline_mode=pl.Buffered(1))


def _rms(x, g):
    return x * lax.rsqrt(jnp.mean(x * x, axis=-1, keepdims=True) + EPS) * g


def _log_sigmoid_pair(z):
    l = jnp.log1p(jnp.exp(-jnp.abs(z)))
    return jnp.minimum(z, 0.0) - l, jnp.minimum(-z, 0.0) - l


def _split_bf16(x):
    hi = x.astype(BF16)
    lo = (x - hi.astype(F32)).astype(BF16)
    return hi, lo


def _dot(a, b):
    return jnp.dot(a, b, preferred_element_type=F32)


def _dot_nt(a, b):
    return lax.dot_general(a, b, (((1,), (1,)), ((), ())), preferred_element_type=F32)


def _dot_tn(a, b):
    return lax.dot_general(a, b, (((0,), (0,)), ((), ())), preferred_element_type=F32)


def _pair_cols(p):
    return slice(p * LANES, (p + 1) * LANES)


class _KV:
    def __init__(self, k, v, feature_major):
        self.k, self.v, self.feature_major = k, v, feature_major
        self.n_keys = k.shape[1] if feature_major else k.shape[0]

    def scores(self, q_h, p):
        if self.feature_major:
            return _dot(q_h, self.k[_pair_cols(p), :])
        return _dot_nt(q_h, self.k[:, _pair_cols(p)])

    def values(self, p):
        return self.v[_pair_cols(p), :] if self.feature_major else self.v[:, _pair_cols(p)]

    def weighted(self, w, v_p):
        return _dot_nt(w, v_p) if self.feature_major else _dot(w, v_p)

    def head_lanes(self):
        shape = (LANES, self.n_keys) if self.feature_major else (self.n_keys, LANES)
        return lax.broadcasted_iota(jnp.int32, shape, 0 if self.feature_major else 1) < HEAD_DIM


def _split_heads(q_ref, n_pairs, tq):
    lo = lax.broadcasted_iota(jnp.int32, (tq, LANES), 1) < HEAD_DIM
    heads = []
    for p in range(n_pairs):
        q = q_ref[:, _pair_cols(p)]
        heads += [jnp.where(lo, q, jnp.zeros_like(q)), jnp.where(lo, jnp.zeros_like(q), q)]
    return heads, lo


def _emit_kv(y, wkv_ref, k_ref, v_ref, kb_ref, vb_ref, feature_major, kept_tail=None, earlier=()):
    if not feature_major:
        kv = _dot_nt(y, wkv_ref[...])
        width = kv.shape[1] // 2
        k, v = kv[:, :width], kv[:, width:]
        k_ref[...], v_ref[...], kb_ref[...], vb_ref[...] = k, v, k.astype(BF16), v.astype(BF16)
        return
    kv = _dot_nt(wkv_ref[...], y)
    width = kv.shape[0] // 2
    k, v = kv[:width, :], kv[width:, :]
    kb_ref[...] = k.astype(BF16)
    vb_ref[...] = v.astype(BF16)
    slot = k_ref.shape[0] - 1
    for src, dst in zip(earlier, (k_ref, v_ref)):
        for s in range(slot):
            dst[s] = src[s]
    if kept_tail is None:
        k_ref[slot] = k
        v_ref[slot] = v
    else:
        n_tiles, n_kept = kept_tail
        tile = pl.program_id(0) % n_tiles
        tm = k.shape[1]
        cols = pl.ds(pl.multiple_of(jnp.maximum(tile - (n_tiles - n_kept), 0) * tm, tm), tm)
        k_ref[slot, :, cols] = k
        v_ref[slot, :, cols] = v


class _Riders:
    def __init__(self, stacks, layer, steps):
        self.stacks, self.layer = list(stacks), layer
        self.chunks = next(c for c in (16, 8, 4, 2, 1)
                           if c <= steps and all(w.shape[1] % (16 * c) == 0 for w in self.stacks))

    def specs(self):
        layer, last = self.layer, self.chunks - 1
        ins = [pl.BlockSpec((None, w.shape[1] // self.chunks, w.shape[2]),
                            lambda i: (layer, jnp.minimum(i, last), 0)) for w in self.stacks]
        outs = [pl.BlockSpec((w.shape[1] // self.chunks, w.shape[2]),
                             lambda i: (jnp.minimum(i, last), 0)) for w in self.stacks]
        shapes = [jax.ShapeDtypeStruct(w.shape[1:], BF16) for w in self.stacks]
        return ins, outs, shapes


def _split_extra_refs(refs, n_earlier, n_riders):
    rider_in = refs[n_earlier:n_earlier + n_riders]
    outs = refs[n_earlier + n_riders:len(refs) - n_riders]
    return refs[:n_earlier], outs, list(zip(rider_in, refs[len(refs) - n_riders:]))


def _cast_riders(pairs):
    for src, dst in pairs:
        dst[...] = src[...].astype(BF16)


def _proj_ab_kernel(x_ref, g_ref, w_ref, wkv_ref, wlr_ref, wgate_ref, bgate_ref, *refs,
                    feature_major, n_earlier, n_riders):
    earlier, outs, riders = _split_extra_refs(refs, n_earlier, n_riders)
    qa_ref, ka_ref, va_ref, kab_ref, vab_ref, qb_ref, kb_ref, vb_ref, r_ref, la_ref = outs
    _cast_riders(riders)
    y = _rms(x_ref[...], g_ref[...]).astype(BF16)
    _emit_kv(y, wkv_ref, ka_ref, va_ref, kab_ref, vab_ref, feature_major, earlier=earlier)
    z = _dot_nt(y, w_ref[...])
    c = 0
    qa_ref[...] = (z[:, c:c + A_WIDTH] * (HEAD_DIM ** -0.5)).astype(BF16); c += A_WIDTH
    qb_ref[...] = z[:, c:c + B_KW]; c += B_KW
    kb_ref[...] = z[:, c:c + B_KW]; c += B_KW
    vb_ref[...] = z[:, c:c + B_VW]; c += B_VW
    r_ref[...] = z[:, c:c + B_VW]
    g_lr = _dot_nt(y, wlr_ref[...])
    gate = _dot(g_lr.astype(BF16), wgate_ref[...]) + bgate_ref[...]
    la_ref[...] = _log_sigmoid_pair(gate)[0] * (1.0 / B_GATE_TEMP)


class _KVStack:
    def __init__(self, keep, earlier=None):
        self.keep, self.earlier = keep, earlier
        self.slots = 1 if earlier is None else earlier[0].shape[0] + 1


def _kv_out(m, width, tm, batch, stack):
    if batch is None:
        spec = pl.BlockSpec((tm, width), lambda i: (i, 0))
        shapes = [jax.ShapeDtypeStruct((m, width), dt) for dt in (F32, F32, BF16, BF16)]
        return [spec] * 4, shapes, None, [], []
    t = m // batch
    assert t % tm == 0 and stack.keep % tm == 0 and stack.keep <= t
    n_tiles = t // tm
    copy_spec = pl.BlockSpec((None, width, tm), lambda i: (i // n_tiles, 0, i % n_tiles))
    copy_shape = jax.ShapeDtypeStruct((batch, width, t), BF16)
    if stack.keep == t:
        cols, index, kept_tail = tm, (lambda i: (0, i // n_tiles, 0, i % n_tiles)), None
    else:
        cols, index, kept_tail = stack.keep, (lambda i: (0, i // n_tiles, 0, 0)), (n_tiles, stack.keep // tm)
    f32_spec = pl.BlockSpec((stack.slots, None, width, cols), index)
    f32_shape = jax.ShapeDtypeStruct((stack.slots, batch, width, stack.keep), F32)
    earlier = [] if stack.earlier is None else list(stack.earlier)
    earlier_specs = [pl.BlockSpec((stack.slots - 1, None, width, cols), index)] * len(earlier)
    return ([f32_spec, f32_spec, copy_spec, copy_spec], [f32_shape, f32_shape, copy_shape, copy_shape],
            kept_tail, earlier, earlier_specs)


def _rider_plumbing(riders):
    if riders is None:
        return [], [], [], []
    return (riders.stacks, *riders.specs())


def _proj_ab(x, g, w_main, w_kv, w_lr, w_gate, b_gate, tm, batch=None, stack=None, riders=None):
    m, d = x.shape
    row = lambda n: pl.BlockSpec((tm, n), lambda i: (i, 0))
    params = (g, w_main, w_kv, w_lr, w_gate, b_gate)
    kv_specs, kv_shapes, kept_tail, earlier, earlier_specs = _kv_out(m, A_WIDTH, tm, batch, stack)
    assert kept_tail is None
    ride_ops, ride_in, ride_out, ride_shapes = _rider_plumbing(riders)
    rest = [(B_KW, F32), (B_KW, F32), (B_VW, F32), (B_VW, F32), (B_KW, F32)]
    return pl.pallas_call(
        functools.partial(_proj_ab_kernel, feature_major=batch is not None, n_earlier=len(earlier),
                          n_riders=len(ride_ops)),
        grid=(m // tm,),
        in_specs=[row(d)] + [_resident(a) for a in params] + earlier_specs + ride_in,
        out_specs=[row(A_WIDTH)] + kv_specs + [row(n) for n, _ in rest] + ride_out,
        out_shape=([jax.ShapeDtypeStruct((m, A_WIDTH), BF16)] + kv_shapes
                   + [jax.ShapeDtypeStruct((m, n), dt) for n, dt in rest] + ride_shapes),
        compiler_params=_params("arbitrary" if ride_ops else "parallel"),
        name="proj_ab",
    )(x, *[_operand(a) for a in params], *earlier, *ride_ops)


def _proj_c_kernel(x_ref, g_ref, wq_ref, wkv_ref, *refs, feature_major, n_earlier, kept_tail,
                   n_riders):
    earlier, outs, riders = _split_extra_refs(refs, n_earlier, n_riders)
    q_ref, k_ref, v_ref, kb_ref, vb_ref = outs
    _cast_riders(riders)
    y = _rms(x_ref[...], g_ref[...]).astype(BF16)
    _emit_kv(y, wkv_ref, k_ref, v_ref, kb_ref, vb_ref, feature_major, kept_tail, earlier)
    q_ref[...] = (_dot(y, wq_ref[...]) * (HEAD_DIM ** -0.5 * LOG2E)).astype(BF16)


def _proj_c(x, g, w_q, w_kv, tm, batch=None, stack=None, riders=None):
    m, d = x.shape
    row = lambda n: pl.BlockSpec((tm, n), lambda i: (i, 0))
    params = (g, w_q, w_kv)
    kv_specs, kv_shapes, kept_tail, earlier, earlier_specs = _kv_out(m, C_WIDTH, tm, batch, stack)
    ride_ops, ride_in, ride_out, ride_shapes = _rider_plumbing(riders)
    return pl.pallas_call(
        functools.partial(_proj_c_kernel, feature_major=batch is not None, n_earlier=len(earlier),
                          kept_tail=kept_tail, n_riders=len(ride_ops)),
        grid=(m // tm,),
        in_specs=[row(d)] + [_resident(a) for a in params] + earlier_specs + ride_in,
        out_specs=[row(C_WIDTH)] + kv_specs + ride_out,
        out_shape=[jax.ShapeDtypeStruct((m, C_WIDTH), BF16)] + kv_shapes + ride_shapes,
        compiler_params=_params("arbitrary" if (kept_tail or ride_ops) else "parallel"),
        name="proj_c",
    )(x, *[_operand(a) for a in params], *earlier, *ride_ops)


def _layer_tail_rows(x_ref, mixer_refs, param_refs, o_ref, gla_merge, final_norm):
    if gla_merge:
        oa_ref, ob_ref, r_ref = mixer_refs
        ggla_ref, wout_ref, *param_refs = param_refs
        ob = ob_ref[...]
        parts = []
        for h in range(B_HEADS):
            seg = ob[:, h * B_DV:(h + 1) * B_DV]
            parts.append(seg * lax.rsqrt(jnp.mean(seg * seg, axis=-1, keepdims=True) + EPS))
        r = r_ref[...]
        obn = jnp.concatenate(parts, axis=-1) * ggla_ref[...] * (r * jax.nn.sigmoid(r))
        mix = _dot(oa_ref[...], wout_ref[:A_WIDTH, :]) + _dot(obn.astype(BF16), wout_ref[A_WIDTH:, :])
    else:
        (oc_ref,) = mixer_refs
        wout_ref, *param_refs = param_refs
        mix = _dot(oc_ref[...], wout_ref[...])
    gffn_ref, wg_ref, wu_ref, wd_ref, *param_refs = param_refs
    x = x_ref[...] + mix
    y = _rms(x, gffn_ref[...]).astype(BF16)
    h = _dot(y, wg_ref[...])
    u = _dot(y, wu_ref[...])
    a = (h * jax.nn.sigmoid(h) * u).astype(BF16)
    x = x + _dot(a, wd_ref[...])
    o_ref[...] = _rms(x, param_refs[0][...]) if final_norm else x


def _layer_tail_kernel(*refs, n_mixer, n_params, gla_merge, final_norm, main_steps):
    main, extra = refs[:1 + n_mixer], refs[1 + n_mixer:2 + 2 * n_mixer]
    params = refs[2 + 2 * n_mixer:2 + 2 * n_mixer + n_params]
    o_ref, o_extra_ref = refs[-2:]

    @pl.when(pl.program_id(0) < main_steps)
    def _():
        _layer_tail_rows(main[0], main[1:], params, o_ref, gla_merge, final_norm)

    @pl.when(pl.program_id(0) == main_steps)
    def _():
        _layer_tail_rows(extra[0], extra[1:], params, o_extra_ref, gla_merge, final_norm)


def _layer_tail(x, mixer_out, x_extra, mixer_out_extra, mixer_params, g_ffn, wg, wu, wd, g_fin, tm):
    m, d = x.shape
    main_steps = m // tm
    row = lambda a: pl.BlockSpec((tm, a.shape[1]), lambda i: (jnp.minimum(i, main_steps - 1), 0))
    whole = lambda a: pl.BlockSpec(a.shape, lambda i: (0, 0))
    params = [*mixer_params, g_ffn, wg, wu, wd] + ([] if g_fin is None else [g_fin])
    return pl.pallas_call(
        functools.partial(_layer_tail_kernel, n_mixer=len(mixer_out), n_params=len(params),
                          gla_merge=len(mixer_out) == 3, final_norm=g_fin is not None, main_steps=main_steps),
        grid=(main_steps + 1,),
        in_specs=([row(x)] + [row(a) for a in mixer_out] + [whole(x_extra)]
                  + [whole(a) for a in mixer_out_extra] + [_resident(a) for a in params]),
        out_specs=[row(x), whole(x_extra)],
        out_shape=[jax.ShapeDtypeStruct((m, d), F32), jax.ShapeDtypeStruct(x_extra.shape, F32)],
        compiler_params=_params("arbitrary"),
        name="layer_tail",
    )(x, *mixer_out, x_extra, *mixer_out_extra, *[_operand(a) for a in params])


def _interleave(stage_generators):
    results = [None] * len(stage_generators)
    live = list(range(len(stage_generators)))
    while live:
        for i in list(live):
            try:
                next(stage_generators[i])
            except StopIteration as done:
                results[i] = done.value
                live.remove(i)
    return results


def _sb_core(q_heads, lo, first_kv, first_mask, earlier_kv, n_earlier, acc_ref, c_ref, o_ref, n_pairs,
             companions=(), on_companions=None):
    tk = first_kv.n_keys
    heads = range(2 * n_pairs)
    later = (lax.broadcasted_iota(jnp.int32, (tk, tk), 0)
             > lax.broadcasted_iota(jnp.int32, (tk, tk), 1)).astype(BF16)
    c_ref[...] = jnp.zeros_like(c_ref)

    def block(kv, mask):
        z = [kv.scores(q_heads[h], h // 2) for h in heads]
        yield
        log_beta, drop, after = [], [], []
        for h in heads:
            d = jnp.maximum(z[h], 0.0) + jnp.log(1.0 + jnp.exp(-jnp.abs(z[h])))
            log_beta.append(z[h] - d)
            drop.append(d if mask is None else jnp.where(mask, d, 0.0))
        yield
        for h in heads:
            hi, lo_part = _split_bf16(drop[h])
            after.append(_dot(hi, later) + _dot(lo_part, later))
        yield
        pv = []
        for h in heads:
            c = c_ref[h]
            w = jnp.exp(log_beta[h] - after[h] - c)
            if mask is not None:
                w = jnp.where(mask, w, 0.0)
            c_ref[h] = c + after[h][:, 0:1] + drop[h][:, 0:1]
            pv.append(kv.weighted(w.astype(BF16), kv.values(h // 2)))
        yield
        out = [jnp.where(lo, pv[2 * p], pv[2 * p + 1]) for p in range(n_pairs)]
        return out[0] if n_pairs == 1 else jnp.concatenate(out, axis=-1)

    def all_dead():
        return jnp.min(c_ref[...]) > SB_DEAD

    first, *companion_results = _interleave([block(first_kv, first_mask), *companions])
    acc_ref[...] = first
    if companions:
        on_companions(companion_results)

    def cond(carry):
        n, dead = carry
        return (n < n_earlier) & jnp.logical_not(dead)

    def body(carry):
        n, _ = carry
        acc_ref[...] += _interleave([block(earlier_kv(n), None)])[0]
        return n + 1, all_dead()

    lax.while_loop(cond, body, (jnp.int32(0), all_dead()))
    o_ref[...] = acc_ref[...].astype(o_ref.dtype)


def _mixer_ab_prompt_kernel(q_ref, k_ref, v_ref, qb_ref, kb_ref, vb_ref, la_ref, o_ref, ob_ref, s_out_ref,
                            acc_ref, c_ref, carry_ref, *, tb, n_pairs, blocks_per_step):
    pairs = range(B_HEADS // 2)
    strictly_earlier = (lax.broadcasted_iota(jnp.int32, (tb, tb), 1)
                        < lax.broadcasted_iota(jnp.int32, (tb, tb), 0))

    def kv_block(j):
        keys = pl.ds(pl.multiple_of(j * tb, tb), tb)
        return _KV(k_ref[:, keys], v_ref[:, keys], True)

    def keep_states(states):
        for p in pairs:
            carry_ref[2 * p], carry_ref[2 * p + 1] = states[p]
            s_out_ref[2 * p] = states[p][0][:B_DK, :]
            s_out_ref[2 * p + 1] = states[p][1][B_DK:, :]

    @pl.when(pl.program_id(1) == 0)
    def _():
        carry_ref[...] = jnp.zeros_like(carry_ref)

    def one_block(s, carry):
        qi = pl.program_id(1) * blocks_per_step + s
        rows = pl.ds(pl.multiple_of(s * tb, tb), tb)
        gla = [_gla_pair_tile(qb_ref[rows, _pair_cols(p)], kb_ref[rows, _pair_cols(p)],
                              vb_ref[rows, _pair_v_cols(p)], la_ref[rows, _pair_cols(p)],
                              [carry_ref[2 * p], carry_ref[2 * p + 1]],
                              _gla_rows_emit(ob_ref.at[rows, :], p)) for p in pairs]
        q_heads, lo = _split_heads(q_ref.at[rows, :], n_pairs, tb)
        _sb_core(q_heads, lo, kv_block(qi), strictly_earlier, lambda n: kv_block(qi - 1 - n), qi,
                 acc_ref, c_ref, o_ref.at[rows, :], n_pairs, companions=gla, on_companions=keep_states)
        return carry

    lax.fori_loop(0, blocks_per_step, one_block, 0)


def _sb_sample_kernel(q_ref, kn_ref, vn_ref, kc_ref, vc_ref, o_ref, acc_ref, c_ref, kpad_ref, vpad_ref,
                      *, ts, tk, n_cache_blocks, n_pairs):
    q_heads, lo = _split_heads(q_ref, n_pairs, ts)
    kpad_ref[...] = jnp.zeros_like(kpad_ref)
    vpad_ref[...] = jnp.zeros_like(vpad_ref)
    kpad_ref[:ts, :] = kn_ref[...]
    vpad_ref[:ts, :] = vn_ref[...]

    def cache_block(n):
        keys = pl.ds(pl.multiple_of((n_cache_blocks - 1 - n) * tk, tk), tk)
        return _KV(kc_ref[:, keys].astype(BF16), vc_ref[:, keys].astype(BF16), True)

    strictly_earlier = (lax.broadcasted_iota(jnp.int32, (ts, tk), 1)
                        < lax.broadcasted_iota(jnp.int32, (ts, tk), 0))
    _sb_core(q_heads, lo, _KV(kpad_ref[...], vpad_ref[...], False), strictly_earlier, cache_block,
             n_cache_blocks, acc_ref, c_ref, o_ref, n_pairs)


def _mixer_ab_prompt(q, k, v, qb, kb, vb, la):
    b, t, _ = q.shape
    tb = ATT_BLOCK
    assert t % tb == 0 and tb % GLA_CHUNK == 0 and tb <= GLA_MAX_TILE
    n_pairs = A_WIDTH // LANES
    per_step = math.gcd(t // tb, MIXER_BLOCKS_PER_STEP)
    rows = lambda width: pl.BlockSpec((None, per_step * tb, width), lambda bi, qi: (bi, qi, 0))
    kv_spec = pl.BlockSpec((None, A_WIDTH, t), lambda bi, qi: (bi, 0, 0))
    state = (B_HEADS, B_DK, B_DV)
    return pl.pallas_call(
        functools.partial(_mixer_ab_prompt_kernel, tb=tb, n_pairs=n_pairs, blocks_per_step=per_step),
        grid=(b, t // (per_step * tb)),
        in_specs=[rows(A_WIDTH), kv_spec, kv_spec, rows(B_KW), rows(B_KW), rows(B_VW), rows(B_KW)],
        out_specs=[rows(A_WIDTH), rows(B_VW), pl.BlockSpec((None, *state), lambda bi, qi: (bi, 0, 0, 0))],
        out_shape=[jax.ShapeDtypeStruct(q.shape, BF16), jax.ShapeDtypeStruct(vb.shape, F32),
                   jax.ShapeDtypeStruct((b, *state), F32)],
        scratch_shapes=[pltpu.VMEM((tb, A_WIDTH), F32), pltpu.VMEM((2 * n_pairs, tb, 1), F32),
                        pltpu.VMEM((B_HEADS, LANES, B_DV), F32)],
        compiler_params=_params("parallel", "arbitrary"),
        name="mixer_ab",
    )(q, k, v, qb, kb, vb, la)


def _sb_attention_sample(q, k_new, v_new, k_cache, v_cache, layer):
    b, ts, _ = q.shape
    past = k_cache.shape[3]
    tk = ATT_BLOCK
    assert past % tk == 0 and ts <= tk
    n_pairs = SB_PAIRS_PER_STEP
    w = n_pairs * LANES
    new_spec = pl.BlockSpec((None, ts, w), lambda bi, hp: (bi, 0, hp))
    cache_spec = pl.BlockSpec((None, None, w, past), lambda bi, hp: (layer, bi, hp, 0))
    return pl.pallas_call(
        functools.partial(_sb_sample_kernel, ts=ts, tk=tk, n_cache_blocks=past // tk, n_pairs=n_pairs),
        grid=(b, A_WIDTH // w),
        in_specs=[new_spec, new_spec, new_spec, cache_spec, cache_spec],
        out_specs=new_spec,
        out_shape=jax.ShapeDtypeStruct(q.shape, BF16),
        scratch_shapes=[pltpu.VMEM((ts, w), F32), pltpu.VMEM((2 * n_pairs, ts, 1), F32),
                        pltpu.VMEM((tk, w), BF16), pltpu.VMEM((tk, w), BF16)],
        compiler_params=_params("parallel", "parallel"),
        name="sb_attention_sample",
    )(q, k_new, v_new, k_cache, v_cache)


def _band_core(q_heads, lo, kv, pens, bias, o_ref, n_pairs):
    heads = range(2 * n_pairs)
    blocks = range(len(kv))
    z = [[kv[i].scores(q_heads[h], h // 2) + bias(h, i) for i in blocks] for h in heads]
    acc = []
    for h in heads:
        m = None
        for i in blocks:
            mi = jnp.max(z[h][i], axis=-1, keepdims=True)
            if pens[i] is not None:
                mi = mi + pens[i]
            m = mi if m is None else jnp.maximum(m, mi)
        a = None
        for i in blocks:
            shift = m if pens[i] is None else m - pens[i]
            p = jnp.exp2(z[h][i] - shift).astype(BF16)
            v = kv[i].values(h // 2)
            ones = jnp.ones_like(v)
            first = kv[i].head_lanes()
            v = jnp.where(first, v, ones) if h % 2 == 0 else jnp.where(first, ones, v)
            pv = kv[i].weighted(p, v)
            a = pv if a is None else a + pv
        acc.append(a)
    for p in range(n_pairs):
        a0, a1 = acc[2 * p], acc[2 * p + 1]
        o_ref[:, _pair_cols(p)] = jnp.where(lo, a0 / pltpu.roll(a0, HEAD_DIM, axis=1),
                                            a1 / pltpu.roll(a1, HEAD_DIM, axis=1)).astype(o_ref.dtype)


def _band_prompt_kernel(q_ref, k_ref, v_ref, bias_ref, o_ref, *, tq, tk, n_pairs, blocks_per_step):
    def one_block(s, carry):
        qi = pl.program_id(2) * blocks_per_step + s
        rows = pl.ds(pl.multiple_of(s * tq, tq), tq)
        q_heads, lo = _split_heads(q_ref.at[rows, :], n_pairs, tq)
        kv, pens = [], []
        for dj in range(3):
            j = qi - dj
            pens.append(None if dj == 0 else jnp.where(j >= 0, 0.0, NEG_BIG).astype(F32))
            keys = pl.ds(pl.multiple_of(jnp.maximum(j, 0) * tk, tk), tk)
            kv.append(_KV(k_ref[:, keys], v_ref[:, keys], True))
        _band_core(q_heads, lo, kv, pens, lambda h, i: bias_ref[h, i], o_ref.at[rows, :], n_pairs)
        return carry

    lax.fori_loop(0, blocks_per_step, one_block, 0)


def _band_sample_kernel(q_ref, kn_ref, vn_ref, kc_ref, vc_ref, bias_ref, o_ref, kpad_ref, vpad_ref,
                        *, ts, tk, n_cache_blocks, n_pairs):
    q_heads, lo = _split_heads(q_ref, n_pairs, ts)
    kpad_ref[...] = jnp.zeros_like(kpad_ref)
    vpad_ref[...] = jnp.zeros_like(vpad_ref)
    kpad_ref[:ts, :] = kn_ref[...]
    vpad_ref[:ts, :] = vn_ref[...]
    kv = [_KV(kpad_ref[...], vpad_ref[...], False)]
    for dj in range(1, n_cache_blocks + 1):
        keys = slice((n_cache_blocks - dj) * tk, (n_cache_blocks - dj + 1) * tk)
        kv.append(_KV(kc_ref[:, keys].astype(BF16), vc_ref[:, keys].astype(BF16), True))
    is_new_key = lax.broadcasted_iota(jnp.int32, (ts, tk), 1) < ts

    def bias(h, i):
        return jnp.where(is_new_key, bias_ref[h, 0], NEG_BIG) if i == 0 else bias_ref[h, i]

    _band_core(q_heads, lo, kv, [None] * len(kv), bias, o_ref, n_pairs)


def _band_attention_prompt(q, k, v, bias):
    b, t, _ = q.shape
    tq = tk = ATT_BLOCK
    assert t % tk == 0
    n_pairs = BAND_PAIRS_PER_STEP
    w = n_pairs * LANES
    per_step = math.gcd(t // tq, BAND_BLOCKS_PER_STEP)
    kern = functools.partial(_band_prompt_kernel, tq=tq, tk=tk, n_pairs=n_pairs, blocks_per_step=per_step)
    kv_spec = pl.BlockSpec((None, w, t), lambda bi, hp, qi: (bi, hp, 0))
    q_spec = pl.BlockSpec((None, per_step * tq, w), lambda bi, hp, qi: (bi, qi, hp))
    bias_spec = pl.BlockSpec((2 * n_pairs, 3, tq, tk), lambda bi, hp, qi: (hp, 0, 0, 0))
    return pl.pallas_call(
        kern,
        grid=(b, C_WIDTH // w, t // (per_step * tq)),
        in_specs=[q_spec, kv_spec, kv_spec, bias_spec],
        out_specs=q_spec,
        out_shape=jax.ShapeDtypeStruct(q.shape, BF16),
        compiler_params=_params("parallel", "parallel", "parallel"),
        name="band_attention",
    )(q, k, v, bias)


def _band_attention_sample(q, k_new, v_new, k_cache, v_cache, layer, bias):
    b, ts, _ = q.shape
    wc = k_cache.shape[3]
    tk = ATT_BLOCK
    n_cache_blocks = min(wc // tk, 2)
    assert wc % (n_cache_blocks * tk) == 0 and ts <= tk
    n_pairs = BAND_PAIRS_PER_STEP
    w = n_pairs * LANES
    kern = functools.partial(_band_sample_kernel, ts=ts, tk=tk, n_cache_blocks=n_cache_blocks,
                             n_pairs=n_pairs)
    new_spec = pl.BlockSpec((None, ts, w), lambda bi, hp: (bi, 0, hp))
    cache_rows = n_cache_blocks * tk
    cache_spec = pl.BlockSpec((None, None, w, cache_rows),
                              lambda bi, hp: (layer, bi, hp, wc // cache_rows - 1))
    bias_spec = pl.BlockSpec((2 * n_pairs, 1 + n_cache_blocks, ts, tk), lambda bi, hp: (hp, 0, 0, 0))
    return pl.pallas_call(
        kern,
        grid=(b, C_WIDTH // w),
        in_specs=[new_spec, new_spec, new_spec, cache_spec, cache_spec, bias_spec],
        out_specs=new_spec,
        out_shape=jax.ShapeDtypeStruct(q.shape, BF16),
        scratch_shapes=[pltpu.VMEM((tk, w), BF16), pltpu.VMEM((tk, w), BF16)],
        compiler_params=_params("parallel", "parallel"),
        name="band_attention_sample",
    )(q, k_new, v_new, k_cache, v_cache, bias)


def _band_bias_kernel(g_ref, o_ref):
    rows = tk = ATT_BLOCK
    q_chunk = lax.broadcasted_iota(jnp.int32, (rows, tk), 0) // CHUNK
    k_chunk = lax.broadcasted_iota(jnp.int32, (rows, tk), 1) // CHUNK
    for dj in range(3):
        g = jnp.broadcast_to(g_ref[0, dj], (rows, 2 * tk))
        tile = pltpu.roll(g, 0, axis=1, stride=1, stride_axis=0)[:, :tk]
        diff = dj * (tk // CHUNK) + q_chunk - k_chunk
        seen = (diff >= 0) & (diff <= C_LEFT_CHUNKS)
        o_ref[0, dj] = jnp.where(seen, tile * LOG2E, NEG_BIG)


def _band_bias(rel_table):
    rows = tk = ATT_BLOCK
    c = jnp.arange(2 * tk, dtype=jnp.int32)
    u = jnp.where(c <= tk, -c, 2 * tk - c)
    idx = jnp.clip(jnp.arange(3, dtype=jnp.int32)[:, None] * tk + u[None, :], REL_MIN, REL_MAX) - REL_MIN
    g = rel_table[:, idx].astype(F32).reshape(C_HEADS, 3, 1, 2 * tk)
    return pl.pallas_call(
        _band_bias_kernel,
        grid=(C_HEADS,),
        in_specs=[pl.BlockSpec((1, 3, 1, 2 * tk), lambda h: (h, 0, 0, 0))],
        out_specs=pl.BlockSpec((1, 3, rows, tk), lambda h: (h, 0, 0, 0)),
        out_shape=jax.ShapeDtypeStruct((C_HEADS, 3, rows, tk), F32),
        compiler_params=_params("parallel"),
        name="band_bias",
    )(g)


def _gla_pair_tile(q, k, v, la, states, emit):
    L = GLA_CHUNK
    chunks = range(q.shape[0] // L)
    row = lax.broadcasted_iota(jnp.int32, (L, L), 0)
    colm = lax.broadcasted_iota(jnp.int32, (L, L), 1)
    tri = (colm <= row).astype(BF16)
    causal = colm <= row
    lane = lax.broadcasted_iota(jnp.int32, (L, LANES), 1)
    sub = lax.broadcasted_iota(jnp.int32, (LANES, B_DV), 0)
    mine = [(lane >= h * B_DK) & (lane < (h + 1) * B_DK) for h in range(2)]
    mine_rows = [(sub >= h * B_DK) & (sub < (h + 1) * B_DK) for h in range(2)]
    rows = [slice(c * L, (c + 1) * L) for c in chunks]

    b = []
    for r in rows:
        g_hi, g_lo = _split_bf16(la[r, :])
        b.append(_dot(tri, g_hi) + _dot(tri, g_lo))
    yield
    qg, qg_h, kg, kd_t, decay = [], [], [], [], []
    for c, r in zip(chunks, rows):
        qg_c = q[r, :] * (B_DK ** -0.5) * jnp.exp(b[c])
        qg.append(qg_c.astype(BF16))
        qg_h.append([jnp.where(mine[h], qg_c, 0.0).astype(BF16) for h in range(2)])
        kg.append((k[r, :] * jnp.exp(-b[c])).astype(BF16))
        b_t = b[c].T
        b_last = b_t[:, L - 1:L]
        kd_t.append((k[r, :].T * jnp.exp(b_last - b_t)).astype(BF16))
        decay.append(jnp.exp(b_last))
    yield
    att =[[jnp.where(causal, _dot_nt(qg_h[c][h], kg[c]), 0.0).astype(BF16) for h in range(2)]
           for c in chunks]
    yield
    o_intra, own = [], []
    for c, r in zip(chunks, rows):
        v_h = [v[r, h * B_DV:(h + 1) * B_DV].astype(BF16) for h in range(2)]
        o_intra.append([_dot(att[c][h], v_h[h]) for h in range(2)])
        own.append([jnp.where(mine_rows[h], _dot(kd_t[c], v_h[h]), 0.0) for h in range(2)])

    yield
    states = list(states)
    start = []
    for c in chunks:
        start.append([s.astype(BF16) for s in states])
        states = [decay[c] * states[h] + own[c][h] for h in range(2)]
    yield
    for c in chunks:
        for h in range(2):
            emit(c, h, o_intra[c][h] + _dot(qg[c], start[c][h]))
    return states


def _pair_states(s0_ref, p):
    zeros_state = jnp.zeros((B_DK, B_DV), F32)
    return [jnp.concatenate([s0_ref[2 * p], zeros_state], axis=0),
            jnp.concatenate([zeros_state, s0_ref[2 * p + 1]], axis=0)]


def _pair_v_cols(p):
    return slice(2 * p * B_DV, 2 * (p + 1) * B_DV)


def _gla_rows_emit(o_ref, p):
    def emit(c, h, o):
        o_ref[c * GLA_CHUNK:(c + 1) * GLA_CHUNK, (2 * p + h) * B_DV:(2 * p + h + 1) * B_DV] = o
    return emit


def _gla_kernel(q_ref, k_ref, v_ref, la_ref, s0_ref, o_ref, s_out_ref):
    pairs = range(B_HEADS // 2)
    states = _interleave([
        _gla_pair_tile(q_ref[:, _pair_cols(p)], k_ref[:, _pair_cols(p)], v_ref[:, _pair_v_cols(p)],
                       la_ref[:, _pair_cols(p)], _pair_states(s0_ref, p), _gla_rows_emit(o_ref, p))
        for p in pairs])
    for p in pairs:
        s_out_ref[2 * p] = states[p][0][:B_DK, :]
        s_out_ref[2 * p + 1] = states[p][1][B_DK:, :]


def _gla(q, k, v, la, s0):
    b, t, _ = q.shape
    assert t % GLA_CHUNK == 0 and t <= GLA_MAX_TILE
    qk_spec = pl.BlockSpec((None, t, B_KW), lambda bi: (bi, 0, 0))
    v_spec = pl.BlockSpec((None, t, B_VW), lambda bi: (bi, 0, 0))
    s_spec = pl.BlockSpec((None, B_HEADS, B_DK, B_DV), lambda bi: (bi, 0, 0, 0))
    return pl.pallas_call(
        _gla_kernel,
        grid=(b,),
        in_specs=[qk_spec, qk_spec, v_spec, qk_spec, s_spec],
        out_specs=[v_spec, s_spec],
        out_shape=[jax.ShapeDtypeStruct(v.shape, F32), jax.ShapeDtypeStruct(s0.shape, F32)],
        compiler_params=_params("parallel"),
        name="gla",
    )(q, k, v, la, s0)


def _pad_rows(x, n):
    return jnp.pad(x, ((0, 0), (0, n - x.shape[1]), (0, 0)))


def _heads_last(x, heads):
    n, b, _, s = x.shape
    return jnp.transpose(x.reshape(n, b, heads, HEAD_DIM, s), (0, 1, 4, 2, 3))


def _feature_major(cache):
    n, b, s, heads, hd = cache.shape
    return jnp.transpose(cache, (0, 1, 3, 4, 2)).reshape(n, b, heads * hd, s)


def _row_tile(m):
    for tm in (512, 256, 128, 64, 32, 16, 8):
        if m % tm == 0:
            return tm
    raise ValueError(f"token count {m} is not a multiple of 8")


def kernel(x_prompt, x_sample, cache_a_k, cache_a_v, state_b, cache_c_k, cache_c_v, norm_mix_g, norm_ffn_g, w_in_ab, w_gate_b, b_gate_b, norm_gla_g, w_out_ab, w_qkv_c, rel_bias_c, w_out_c, w_ffn_gate, w_ffn_up, w_ffn_down, norm_final_g):
    bp, tp, d = x_prompt.shape
    bs, ts, _ = x_sample.shape
    depth = norm_mix_g.shape[0]
    past = cache_a_k.shape[2]
    wc = cache_c_k.shape[2]
    assert tp % ATT_BLOCK == 0 and past % ATT_BLOCK == 0 and wc % ATT_BLOCK == 0
    assert ts <= GLA_CHUNK and ts % 8 == 0
    mp, ms = bp * tp, bs * ts
    tmp, tms = _row_tile(tp), _row_tile(ms)
    xp = x_prompt.reshape(mp, d)
    xs = x_sample.reshape(ms, d)
    row2 = lambda a: a.reshape(1, -1)

    a_ks, a_vs, b_sp, b_ss, c_ks, c_vs = [], [], [], [], [], []
    a_kv_prompt = c_kv_prompt = None
    keep = min(C_LEFT_CHUNKS * CHUNK, tp)

    kv0, kv1 = A_WIDTH, 3 * A_WIDTH
    o = 3 * A_WIDTH + 2 * B_KW + B_VW
    w_in_t = jnp.swapaxes(w_in_ab, 1, 2)
    w_main_all = jnp.concatenate([w_in_t[:, :kv0], w_in_t[:, kv1:o], w_in_t[:, o + B_GATE_RANK:]],
                                 axis=1).astype(BF16)
    w_kv_ab_t_all = w_in_t[:, kv0:kv1].astype(BF16)
    w_lr_all = jnp.pad(w_in_t[:, o:o + B_GATE_RANK],
                       ((0, 0), (0, LANES - B_GATE_RANK), (0, 0))).astype(BF16)
    w_gate_all = jnp.pad(w_gate_b, ((0, 0), (0, LANES - B_GATE_RANK), (0, 0))).astype(BF16)
    w_out_ab_all = w_out_ab.astype(BF16)
    w_q_all = w_qkv_c[:, :, :C_WIDTH].astype(BF16)
    w_kv_c_t_all = jnp.swapaxes(w_qkv_c[:, :, C_WIDTH:], 1, 2).astype(BF16)
    w_out_c_all = w_out_c.astype(BF16)
    cache_a_k_fm, cache_a_v_fm, cache_c_k_fm, cache_c_v_fm = (
        _feature_major(c) for c in (cache_a_k, cache_a_v, cache_c_k, cache_c_v))

    for layer in range(depth):
        i = layer // 2
        g_mix = row2(norm_mix_g[layer])
        riders = _Riders((w_ffn_gate, w_ffn_up, w_ffn_down), layer, mp // tmp)
        ffn_ends = (row2(norm_ffn_g[layer]), row2(norm_final_g) if layer == depth - 1 else None)
        if layer % 2 == 0:
            w_main, w_kv_t, w_lr, w_gate, w_out = (
                _Slab(w, i) for w in (w_main_all, w_kv_ab_t_all, w_lr_all, w_gate_all, w_out_ab_all))
            b_gate = row2(b_gate_b[i])
            g_gla = row2(norm_gla_g[i])

            qa, ka, va, kab, vab, qb, kb, vb, r, la, *ffn_w = _proj_ab(
                xp, g_mix, w_main, w_kv_t, w_lr, w_gate, b_gate, tmp, batch=bp,
                stack=_KVStack(tp, a_kv_prompt), riders=riders)
            ffn = (ffn_ends[0], *ffn_w, ffn_ends[1])
            a_kv_prompt = (ka, va)
            sh = lambda a: a.reshape(bp, tp, -1)
            oa, ob, sbp = _mixer_ab_prompt(sh(qa), kab, vab, sh(qb), sh(kb), sh(vb), sh(la))
            mixed_p = (oa.reshape(mp, -1), ob.reshape(mp, -1), r)
            b_sp.append(sbp)

            qa, ka, va, kab, vab, qb, kb, vb, r, la = _proj_ab(
                xs, g_mix, w_main, w_kv_t, w_lr, w_gate, b_gate, tms)
            sh = lambda a: a.reshape(bs, ts, -1)
            oa = _sb_attention_sample(sh(qa), sh(kab), sh(vab), cache_a_k_fm, cache_a_v_fm, i)
            pad_t = lambda a: _pad_rows(sh(a), GLA_CHUNK)
            ob, sbs = _gla(pad_t(qb), pad_t(kb), pad_t(vb), pad_t(la), state_b[i])
            mixed_s = (oa.reshape(ms, -1), ob[:, :ts].reshape(ms, -1), r)
            xp, xs = _layer_tail(xp, mixed_p, xs, mixed_s, (g_gla, w_out), *ffn, tmp)
            a_ks.append(ka.reshape(bs, ts, A_HEADS, HEAD_DIM))
            a_vs.append(va.reshape(bs, ts, A_HEADS, HEAD_DIM))
            b_ss.append(sbs)
        else:
            w_q, w_kv_t, w_out = (_Slab(w, i) for w in (w_q_all, w_kv_c_t_all, w_out_c_all))

            q, k, v, kb16, vb16, *ffn_w = _proj_c(xp, g_mix, w_q, w_kv_t, tmp, batch=bp,
                                                  stack=_KVStack(keep, c_kv_prompt), riders=riders)
            ffn = (ffn_ends[0], *ffn_w, ffn_ends[1])
            c_kv_prompt = (k, v)
            bias = _band_bias(rel_bias_c[i])
            oc = _band_attention_prompt(q.reshape(bp, tp, -1), kb16, vb16, bias)
            mixed_p = (oc.reshape(mp, -1),)

            q, k, v, kb16, vb16 = _proj_c(xs, g_mix, w_q, w_kv_t, tms)
            sh = lambda a: a.reshape(bs, ts, -1)
            oc = _band_attention_sample(sh(q), sh(kb16), sh(vb16), cache_c_k_fm, cache_c_v_fm, i, bias)
            xp, xs = _layer_tail(xp, mixed_p, xs, (oc.reshape(ms, -1),), (w_out,), *ffn, tmp)
            c_ks.append(k.reshape(bs, ts, C_HEADS, HEAD_DIM))
            c_vs.append(v.reshape(bs, ts, C_HEADS, HEAD_DIM))

    y_prompt = xp.reshape(bp, tp, d)
    y_sample = xs.reshape(bs, ts, d)
    a_kp, a_vp = (_heads_last(a, A_HEADS) for a in a_kv_prompt)
    c_kp, c_vp = (_heads_last(a, C_HEADS) for a in c_kv_prompt)
    return (y_prompt, y_sample, a_kp, a_vp, jnp.stack(a_ks), jnp.stack(a_vs),
            jnp.stack(b_sp), jnp.stack(b_ss), c_kp, c_vp, jnp.stack(c_ks), jnp.stack(c_vs))
```

```python
import functools
import math

import jax
import jax.numpy as jnp
from jax import lax
from jax.experimental import pallas as pl
from jax.experimental.pallas import tpu as pltpu

F32 = jnp.float32
BF16 = jnp.bfloat16

EPS = 1e-6
HEAD_DIM = 64
LANES = 128
A_HEADS = 8
A_WIDTH = A_HEADS * HEAD_DIM
B_HEADS = 4
B_DK = 64
B_DV = 128
B_KW = B_HEADS * B_DK
B_VW = B_HEADS * B_DV
B_GATE_RANK = 16
B_GATE_TEMP = 16.0
GLA_CHUNK = 64
C_HEADS = 16
C_WIDTH = C_HEADS * HEAD_DIM
CHUNK = 64
C_LEFT_CHUNKS = 8
REL_MIN = -(CHUNK - 1)
REL_MAX = 128
ATT_BLOCK = 256
NEG_BIG = -1e30
LOG2E = 1.4426950408889634
SB_DEAD = 104.0
SB_PAIRS_PER_STEP = 4
BAND_PAIRS_PER_STEP = 4
MIXER_BLOCKS_PER_STEP = 4
BAND_BLOCKS_PER_STEP = 8
GLA_MAX_TILE = 512
VMEM_LIMIT = 56 * 1024 * 1024


def _params(*sem):
    return pltpu.CompilerParams(dimension_semantics=sem, vmem_limit_bytes=VMEM_LIMIT)


class _Slab:
    def __init__(self, stacked, index):
        self.stacked, self.index, self.shape = stacked, index, stacked.shape[1:]


def _operand(a):
    return a.stacked if isinstance(a, _Slab) else a


def _resident(a):
    if isinstance(a, _Slab):
        index = (a.index,) + (0,) * len(a.shape)
        return pl.BlockSpec((None, *a.shape), lambda *_: index, pipeline_mode=pl.Buffered(1))
    return pl.BlockSpec(a.shape, lambda *_: (0,) * a.ndim, pipeline_mode=pl.Buffered(1))


def _rms(x, g):
    return x * lax.rsqrt(jnp.mean(x * x, axis=-1, keepdims=True) + EPS) * g


def _log_sigmoid_pair(z):
    l = jnp.log1p(jnp.exp(-jnp.abs(z)))
    return jnp.minimum(z, 0.0) - l, jnp.minimum(-z, 0.0) - l


def _split_bf16(x):
    hi = x.astype(BF16)
    lo = (x - hi.astype(F32)).astype(BF16)
    return hi, lo


def _dot(a, b):
    return jnp.dot(a, b, preferred_element_type=F32)


def _dot_nt(a, b):
    return lax.dot_general(a, b, (((1,), (1,)), ((), ())), preferred_element_type=F32)


def _dot_tn(a, b):
    return lax.dot_general(a, b, (((0,), (0,)), ((), ())), preferred_element_type=F32)


def _pair_cols(p):
    return slice(p * LANES, (p + 1) * LANES)


class _KV:
    def __init__(self, k, v, feature_major):
        self.k, self.v, self.feature_major = k, v, feature_major
        self.n_keys = k.shape[1] if feature_major else k.shape[0]

    def scores(self, q_h, p):
        if self.feature_major:
            return _dot(q_h, self.k[_pair_cols(p), :])
        return _dot_nt(q_h, self.k[:, _pair_cols(p)])

    def values(self, p):
        return self.v[_pair_cols(p), :] if self.feature_major else self.v[:, _pair_cols(p)]

    def weighted(self, w, v_p):
        return _dot_nt(w, v_p) if self.feature_major else _dot(w, v_p)

    def head_lanes(self):
        shape = (LANES, self.n_keys) if self.feature_major else (self.n_keys, LANES)
        return lax.broadcasted_iota(jnp.int32, shape, 0 if self.feature_major else 1) < HEAD_DIM


def _split_heads(q_ref, n_pairs, tq):
    lo = lax.broadcasted_iota(jnp.int32, (tq, LANES), 1) < HEAD_DIM
    heads = []
    for p in range(n_pairs):
        q = q_ref[:, _pair_cols(p)]
        heads += [jnp.where(lo, q, jnp.zeros_like(q)), jnp.where(lo, jnp.zeros_like(q), q)]
    return heads, lo


def _emit_kv(y, wkv_ref, k_ref, v_ref, kb_ref, vb_ref, feature_major, kept_tail=None, own_slot=0):
    if not feature_major:
        kv = _dot_nt(y, wkv_ref[...])
        width = kv.shape[1] // 2
        k, v = kv[:, :width], kv[:, width:]
        k_ref[...], v_ref[...], kb_ref[...], vb_ref[...] = k, v, k.astype(BF16), v.astype(BF16)
        return
    kv = _dot_nt(wkv_ref[...], y)
    width = kv.shape[0] // 2
    k, v = kv[:width, :], kv[width:, :]
    kb_ref[...] = k.astype(BF16)
    vb_ref[...] = v.astype(BF16)
    slot = own_slot
    for dst in (k_ref, v_ref):
        for s in range(dst.shape[0]):
            if s != slot:
                dst[s] = jnp.zeros(dst.shape[1:], dst.dtype)
    if kept_tail is None:
        k_ref[slot] = k
        v_ref[slot] = v
    else:
        n_tiles, n_kept = kept_tail
        tile = pl.program_id(0) % n_tiles
        tm = k.shape[1]
        cols = pl.ds(pl.multiple_of(jnp.maximum(tile - (n_tiles - n_kept), 0) * tm, tm), tm)
        k_ref[slot, :, cols] = k
        v_ref[slot, :, cols] = v


class _Riders:
    def __init__(self, stacks, layer, steps):
        self.stacks, self.layer = list(stacks), layer
        self.chunks = next(c for c in (16, 8, 4, 2, 1)
                           if c <= steps and all(w.shape[1] % (16 * c) == 0 for w in self.stacks))

    def specs(self):
        layer, last = self.layer, self.chunks - 1
        ins = [pl.BlockSpec((None, w.shape[1] // self.chunks, w.shape[2]),
                            lambda i: (layer, jnp.minimum(i, last), 0)) for w in self.stacks]
        outs = [pl.BlockSpec((w.shape[1] // self.chunks, w.shape[2]),
                             lambda i: (jnp.minimum(i, last), 0)) for w in self.stacks]
        shapes = [jax.ShapeDtypeStruct(w.shape[1:], BF16) for w in self.stacks]
        return ins, outs, shapes


def _split_extra_refs(refs, n_aliased, n_riders):
    rider_in = refs[n_aliased:n_aliased + n_riders]
    outs = refs[n_aliased + n_riders:len(refs) - n_riders]
    return outs, list(zip(rider_in, refs[len(refs) - n_riders:]))


def _cast_riders(pairs):
    for src, dst in pairs:
        dst[...] = src[...].astype(BF16)


def _proj_ab_kernel(x_ref, g_ref, w_ref, wkv_ref, wlr_ref, wgate_ref, bgate_ref, *refs,
                    feature_major, n_aliased, n_riders, own_slot):
    outs, riders = _split_extra_refs(refs, n_aliased, n_riders)
    qa_ref, ka_ref, va_ref, kab_ref, vab_ref, qb_ref, kb_ref, vb_ref, r_ref, la_ref = outs
    _cast_riders(riders)
    y = _rms(x_ref[...], g_ref[...]).astype(BF16)
    _emit_kv(y, wkv_ref, ka_ref, va_ref, kab_ref, vab_ref, feature_major, own_slot=own_slot)
    z = _dot_nt(y, w_ref[...])
    c = 0
    qa_ref[...] = (z[:, c:c + A_WIDTH] * (HEAD_DIM ** -0.5)).astype(BF16); c += A_WIDTH
    qb_ref[...] = z[:, c:c + B_KW]; c += B_KW
    kb_ref[...] = z[:, c:c + B_KW]; c += B_KW
    vb_ref[...] = z[:, c:c + B_VW]; c += B_VW
    r_ref[...] = z[:, c:c + B_VW]
    g_lr = _dot_nt(y, wlr_ref[...])
    gate = _dot(g_lr.astype(BF16), wgate_ref[...]) + bgate_ref[...]
    la_ref[...] = _log_sigmoid_pair(gate)[0] * (1.0 / B_GATE_TEMP)


class _KVStack:
    def __init__(self, layer, n_layers, keep, previous=None):
        self.layer, self.n_layers, self.keep, self.previous = layer, n_layers, keep, previous


def _kv_out(m, width, tm, batch, stack):
    if batch is None:
        spec = pl.BlockSpec((tm, width), lambda i: (i, 0))
        return [spec] * 4, [jax.ShapeDtypeStruct((m, width), dt) for dt in (F32, F32, BF16, BF16)], None, 0
    t = m // batch
    assert t % tm == 0 and stack.keep % tm == 0 and stack.keep <= t
    n_tiles = t // tm
    copy_spec = pl.BlockSpec((None, width, tm), lambda i: (i // n_tiles, 0, i % n_tiles))
    copy_shape = jax.ShapeDtypeStruct((batch, width, t), BF16)
    slots, first_slot, own_slot = ((stack.n_layers, 0, stack.layer) if stack.previous is None
                                   else (1, stack.layer, 0))
    if stack.keep == t:
        f32_spec = pl.BlockSpec((slots, None, width, tm), lambda i: (first_slot, i // n_tiles, 0, i % n_tiles))
        kept_tail = None
    else:
        f32_spec = pl.BlockSpec((slots, None, width, stack.keep), lambda i: (first_slot, i // n_tiles, 0, 0))
        kept_tail = (n_tiles, stack.keep // tm)
    f32_shape = jax.ShapeDtypeStruct((stack.n_layers, batch, width, stack.keep), F32)
    return ([f32_spec, f32_spec, copy_spec, copy_spec], [f32_shape, f32_shape, copy_shape, copy_shape],
            kept_tail, own_slot)


def _aliased_stack(stack, n_inputs, first_output):
    if stack is None or stack.previous is None:
        return [], [], {}
    prev = list(stack.previous)
    specs = [pl.BlockSpec(memory_space=pl.ANY)] * len(prev)
    return prev, specs, {n_inputs + j: first_output + j for j in range(len(prev))}


def _rider_plumbing(riders):
    if riders is None:
        return [], [], [], []
    return (riders.stacks, *riders.specs())


def _proj_ab(x, g, w_main, w_kv, w_lr, w_gate, b_gate, tm, batch=None, stack=None, riders=None):
    m, d = x.shape
    row = lambda n: pl.BlockSpec((tm, n), lambda i: (i, 0))
    params = (g, w_main, w_kv, w_lr, w_gate, b_gate)
    kv_specs, kv_shapes, kept_tail, own_slot = _kv_out(m, A_WIDTH, tm, batch, stack)
    assert kept_tail is None
    prev, prev_specs, aliases = _aliased_stack(stack, 1 + len(params), 1)
    ride_ops, ride_in, ride_out, ride_shapes = _rider_plumbing(riders)
    rest = [(B_KW, F32), (B_KW, F32), (B_VW, F32), (B_VW, F32), (B_KW, F32)]
    return pl.pallas_call(
        functools.partial(_proj_ab_kernel, feature_major=batch is not None, n_aliased=len(prev),
                          n_riders=len(ride_ops), own_slot=own_slot),
        grid=(m // tm,),
        in_specs=[row(d)] + [_resident(a) for a in params] + prev_specs + ride_in,
        out_specs=[row(A_WIDTH)] + kv_specs + [row(n) for n, _ in rest] + ride_out,
        out_shape=([jax.ShapeDtypeStruct((m, A_WIDTH), BF16)] + kv_shapes
                   + [jax.ShapeDtypeStruct((m, n), dt) for n, dt in rest] + ride_shapes),
        input_output_aliases=aliases,
        compiler_params=_params("arbitrary" if ride_ops else "parallel"),
        name="proj_ab",
    )(x, *[_operand(a) for a in params], *prev, *ride_ops)


def _proj_c_kernel(x_ref, g_ref, wq_ref, wkv_ref, *refs, feature_major, n_aliased, kept_tail,
                   n_riders, own_slot):
    outs, riders = _split_extra_refs(refs, n_aliased, n_riders)
    q_ref, k_ref, v_ref, kb_ref, vb_ref = outs
    _cast_riders(riders)
    y = _rms(x_ref[...], g_ref[...]).astype(BF16)
    _emit_kv(y, wkv_ref, k_ref, v_ref, kb_ref, vb_ref, feature_major, kept_tail, own_slot)
    q_ref[...] = (_dot(y, wq_ref[...]) * (HEAD_DIM ** -0.5 * LOG2E)).astype(BF16)


def _proj_c(x, g, w_q, w_kv, tm, batch=None, stack=None, riders=None):
    m, d = x.shape
    row = lambda n: pl.BlockSpec((tm, n), lambda i: (i, 0))
    params = (g, w_q, w_kv)
    kv_specs, kv_shapes, kept_tail, own_slot = _kv_out(m, C_WIDTH, tm, batch, stack)
    prev, prev_specs, aliases = _aliased_stack(stack, 1 + len(params), 1)
    ride_ops, ride_in, ride_out, ride_shapes = _rider_plumbing(riders)
    return pl.pallas_call(
        functools.partial(_proj_c_kernel, feature_major=batch is not None, n_aliased=len(prev),
                          kept_tail=kept_tail, n_riders=len(ride_ops), own_slot=own_slot),
        grid=(m // tm,),
        in_specs=[row(d)] + [_resident(a) for a in params] + prev_specs + ride_in,
        out_specs=[row(C_WIDTH)] + kv_specs + ride_out,
        out_shape=[jax.ShapeDtypeStruct((m, C_WIDTH), BF16)] + kv_shapes + ride_shapes,
        input_output_aliases=aliases,
        compiler_params=_params("arbitrary" if (kept_tail or ride_ops) else "parallel"),
        name="proj_c",
    )(x, *[_operand(a) for a in params], *prev, *ride_ops)


def _layer_tail_rows(x_ref, mixer_refs, param_refs, o_ref, gla_merge, final_norm):
    if gla_merge:
        oa_ref, ob_ref, r_ref = mixer_refs
        ggla_ref, wout_ref, *param_refs = param_refs
        ob = ob_ref[...]
        parts = []
        for h in range(B_HEADS):
            seg = ob[:, h * B_DV:(h + 1) * B_DV]
            parts.append(seg * lax.rsqrt(jnp.mean(seg * seg, axis=-1, keepdims=True) + EPS))
        r = r_ref[...]
        obn = jnp.concatenate(parts, axis=-1) * ggla_ref[...] * (r * jax.nn.sigmoid(r))
        mix = _dot(oa_ref[...], wout_ref[:A_WIDTH, :]) + _dot(obn.astype(BF16), wout_ref[A_WIDTH:, :])
    else:
        (oc_ref,) = mixer_refs
        wout_ref, *param_refs = param_refs
        mix = _dot(oc_ref[...], wout_ref[...])
    gffn_ref, wg_ref, wu_ref, wd_ref, *param_refs = param_refs
    x = x_ref[...] + mix
    y = _rms(x, gffn_ref[...]).astype(BF16)
    h = _dot(y, wg_ref[...])
    u = _dot(y, wu_ref[...])
    a = (h * jax.nn.sigmoid(h) * u).astype(BF16)
    x = x + _dot(a, wd_ref[...])
    o_ref[...] = _rms(x, param_refs[0][...]) if final_norm else x


def _layer_tail_kernel(*refs, n_mixer, n_params, gla_merge, final_norm, main_steps):
    main, extra = refs[:1 + n_mixer], refs[1 + n_mixer:2 + 2 * n_mixer]
    params = refs[2 + 2 * n_mixer:2 + 2 * n_mixer + n_params]
    o_ref, o_extra_ref = refs[-2:]

    @pl.when(pl.program_id(0) < main_steps)
    def _():
        _layer_tail_rows(main[0], main[1:], params, o_ref, gla_merge, final_norm)

    @pl.when(pl.program_id(0) == main_steps)
    def _():
        _layer_tail_rows(extra[0], extra[1:], params, o_extra_ref, gla_merge, final_norm)


def _layer_tail(x, mixer_out, x_extra, mixer_out_extra, mixer_params, g_ffn, wg, wu, wd, g_fin, tm):
    m, d = x.shape
    main_steps = m // tm
    row = lambda a: pl.BlockSpec((tm, a.shape[1]), lambda i: (jnp.minimum(i, main_steps - 1), 0))
    whole = lambda a: pl.BlockSpec(a.shape, lambda i: (0, 0))
    params = [*mixer_params, g_ffn, wg, wu, wd] + ([] if g_fin is None else [g_fin])
    return pl.pallas_call(
        functools.partial(_layer_tail_kernel, n_mixer=len(mixer_out), n_params=len(params),
                          gla_merge=len(mixer_out) == 3, final_norm=g_fin is not None, main_steps=main_steps),
        grid=(main_steps + 1,),
        in_specs=([row(x)] + [row(a) for a in mixer_out] + [whole(x_extra)]
                  + [whole(a) for a in mixer_out_extra] + [_resident(a) for a in params]),
        out_specs=[row(x), whole(x_extra)],
        out_shape=[jax.ShapeDtypeStruct((m, d), F32), jax.ShapeDtypeStruct(x_extra.shape, F32)],
        compiler_params=_params("arbitrary"),
        name="layer_tail",
    )(x, *mixer_out, x_extra, *mixer_out_extra, *[_operand(a) for a in params])


def _interleave(stage_generators):
    results = [None] * len(stage_generators)
    live = list(range(len(stage_generators)))
    while live:
        for i in list(live):
            try:
                next(stage_generators[i])
            except StopIteration as done:
                results[i] = done.value
                live.remove(i)
    return results


def _sb_core(q_heads, lo, first_kv, first_mask, earlier_kv, n_earlier, acc_ref, c_ref, o_ref, n_pairs,
             companions=(), on_companions=None):
    tk = first_kv.n_keys
    heads = range(2 * n_pairs)
    later = (lax.broadcasted_iota(jnp.int32, (tk, tk), 0)
             > lax.broadcasted_iota(jnp.int32, (tk, tk), 1)).astype(BF16)
    c_ref[...] = jnp.zeros_like(c_ref)

    def block(kv, mask):
        z = [kv.scores(q_heads[h], h // 2) for h in heads]
        yield
        log_beta, drop, after = [], [], []
        for h in heads:
            d = jnp.maximum(z[h], 0.0) + jnp.log(1.0 + jnp.exp(-jnp.abs(z[h])))
            log_beta.append(z[h] - d)
            drop.append(d if mask is None else jnp.where(mask, d, 0.0))
        yield
        for h in heads:
            hi, lo_part = _split_bf16(drop[h])
            after.append(_dot(hi, later) + _dot(lo_part, later))
        yield
        pv = []
        for h in heads:
            c = c_ref[h]
            w = jnp.exp(log_beta[h] - after[h] - c)
            if mask is not None:
                w = jnp.where(mask, w, 0.0)
            c_ref[h] = c + after[h][:, 0:1] + drop[h][:, 0:1]
            pv.append(kv.weighted(w.astype(BF16), kv.values(h // 2)))
        yield
        out = [jnp.where(lo, pv[2 * p], pv[2 * p + 1]) for p in range(n_pairs)]
        return out[0] if n_pairs == 1 else jnp.concatenate(out, axis=-1)

    def all_dead():
        return jnp.min(c_ref[...]) > SB_DEAD

    first, *companion_results = _interleave([block(first_kv, first_mask), *companions])
    acc_ref[...] = first
    if companions:
        on_companions(companion_results)

    def cond(carry):
        n, dead = carry
        return (n < n_earlier) & jnp.logical_not(dead)

    def body(carry):
        n, _ = carry
        acc_ref[...] += _interleave([block(earlier_kv(n), None)])[0]
        return n + 1, all_dead()

    lax.while_loop(cond, body, (jnp.int32(0), all_dead()))
    o_ref[...] = acc_ref[...].astype(o_ref.dtype)


def _mixer_ab_prompt_kernel(q_ref, k_ref, v_ref, qb_ref, kb_ref, vb_ref, la_ref, o_ref, ob_ref, s_out_ref,
                            acc_ref, c_ref, carry_ref, *, tb, n_pairs, blocks_per_step):
    pairs = range(B_HEADS // 2)
    strictly_earlier = (lax.broadcasted_iota(jnp.int32, (tb, tb), 1)
                        < lax.broadcasted_iota(jnp.int32, (tb, tb), 0))

    def kv_block(j):
        keys = pl.ds(pl.multiple_of(j * tb, tb), tb)
        return _KV(k_ref[:, keys], v_ref[:, keys], True)

    def keep_states(states):
        for p in pairs:
            carry_ref[2 * p], carry_ref[2 * p + 1] = states[p]
            s_out_ref[2 * p] = states[p][0][:B_DK, :]
            s_out_ref[2 * p + 1] = states[p][1][B_DK:, :]

    @pl.when(pl.program_id(1) == 0)
    def _():
        carry_ref[...] = jnp.zeros_like(carry_ref)

    def one_block(s, carry):
        qi = pl.program_id(1) * blocks_per_step + s
        rows = pl.ds(pl.multiple_of(s * tb, tb), tb)
        gla = [_gla_pair_tile(qb_ref[rows, _pair_cols(p)], kb_ref[rows, _pair_cols(p)],
                              vb_ref[rows, _pair_v_cols(p)], la_ref[rows, _pair_cols(p)],
                              [carry_ref[2 * p], carry_ref[2 * p + 1]],
                              _gla_rows_emit(ob_ref.at[rows, :], p)) for p in pairs]
        q_heads, lo = _split_heads(q_ref.at[rows, :], n_pairs, tb)
        _sb_core(q_heads, lo, kv_block(qi), strictly_earlier, lambda n: kv_block(qi - 1 - n), qi,
                 acc_ref, c_ref, o_ref.at[rows, :], n_pairs, companions=gla, on_companions=keep_states)
        return carry

    lax.fori_loop(0, blocks_per_step, one_block, 0)


def _sb_sample_kernel(q_ref, kn_ref, vn_ref, kc_ref, vc_ref, o_ref, acc_ref, c_ref, kpad_ref, vpad_ref,
                      *, ts, tk, n_cache_blocks, n_pairs):
    q_heads, lo = _split_heads(q_ref, n_pairs, ts)
    kpad_ref[...] = jnp.zeros_like(kpad_ref)
    vpad_ref[...] = jnp.zeros_like(vpad_ref)
    kpad_ref[:ts, :] = kn_ref[...]
    vpad_ref[:ts, :] = vn_ref[...]

    def cache_block(n):
        keys = pl.ds(pl.multiple_of((n_cache_blocks - 1 - n) * tk, tk), tk)
        return _KV(kc_ref[:, keys].astype(BF16), vc_ref[:, keys].astype(BF16), True)

    strictly_earlier = (lax.broadcasted_iota(jnp.int32, (ts, tk), 1)
                        < lax.broadcasted_iota(jnp.int32, (ts, tk), 0))
    _sb_core(q_heads, lo, _KV(kpad_ref[...], vpad_ref[...], False), strictly_earlier, cache_block,
             n_cache_blocks, acc_ref, c_ref, o_ref, n_pairs)


def _mixer_ab_prompt(q, k, v, qb, kb, vb, la):
    b, t, _ = q.shape
    tb = ATT_BLOCK
    assert t % tb == 0 and tb % GLA_CHUNK == 0 and tb <= GLA_MAX_TILE
    n_pairs = A_WIDTH // LANES
    per_step = math.gcd(t // tb, MIXER_BLOCKS_PER_STEP)
    rows = lambda width: pl.BlockSpec((None, per_step * tb, width), lambda bi, qi: (bi, qi, 0))
    kv_spec = pl.BlockSpec((None, A_WIDTH, t), lambda bi, qi: (bi, 0, 0))
    state = (B_HEADS, B_DK, B_DV)
    return pl.pallas_call(
        functools.partial(_mixer_ab_prompt_kernel, tb=tb, n_pairs=n_pairs, blocks_per_step=per_step),
        grid=(b, t // (per_step * tb)),
        in_specs=[rows(A_WIDTH), kv_spec, kv_spec, rows(B_KW), rows(B_KW), rows(B_VW), rows(B_KW)],
        out_specs=[rows(A_WIDTH), rows(B_VW), pl.BlockSpec((None, *state), lambda bi, qi: (bi, 0, 0, 0))],
        out_shape=[jax.ShapeDtypeStruct(q.shape, BF16), jax.ShapeDtypeStruct(vb.shape, F32),
                   jax.ShapeDtypeStruct((b, *state), F32)],
        scratch_shapes=[pltpu.VMEM((tb, A_WIDTH), F32), pltpu.VMEM((2 * n_pairs, tb, 1), F32),
                        pltpu.VMEM((B_HEADS, LANES, B_DV), F32)],
        compiler_params=_params("parallel", "arbitrary"),
        name="mixer_ab",
    )(q, k, v, qb, kb, vb, la)


def _sb_attention_sample(q, k_new, v_new, k_cache, v_cache, layer):
    b, ts, _ = q.shape
    past = k_cache.shape[3]
    tk = ATT_BLOCK
    assert past % tk == 0 and ts <= tk
    n_pairs = SB_PAIRS_PER_STEP
    w = n_pairs * LANES
    new_spec = pl.BlockSpec((None, ts, w), lambda bi, hp: (bi, 0, hp))
    cache_spec = pl.BlockSpec((None, None, w, past), lambda bi, hp: (layer, bi, hp, 0))
    return pl.pallas_call(
        functools.partial(_sb_sample_kernel, ts=ts, tk=tk, n_cache_blocks=past // tk, n_pairs=n_pairs),
        grid=(b, A_WIDTH // w),
        in_specs=[new_spec, new_spec, new_spec, cache_spec, cache_spec],
        out_specs=new_spec,
        out_shape=jax.ShapeDtypeStruct(q.shape, BF16),
        scratch_shapes=[pltpu.VMEM((ts, w), F32), pltpu.VMEM((2 * n_pairs, ts, 1), F32),
                        pltpu.VMEM((tk, w), BF16), pltpu.VMEM((tk, w), BF16)],
        compiler_params=_params("parallel", "parallel"),
        name="sb_attention_sample",
    )(q, k_new, v_new, k_cache, v_cache)


def _band_core(q_heads, lo, kv, pens, bias, o_ref, n_pairs):
    heads = range(2 * n_pairs)
    blocks = range(len(kv))
    z = [[kv[i].scores(q_heads[h], h // 2) + bias(h, i) for i in blocks] for h in heads]
    acc = []
    for h in heads:
        m = None
        for i in blocks:
            mi = jnp.max(z[h][i], axis=-1, keepdims=True)
            if pens[i] is not None:
                mi = mi + pens[i]
            m = mi if m is None else jnp.maximum(m, mi)
        a = None
        for i in blocks:
            shift = m if pens[i] is None else m - pens[i]
            p = jnp.exp2(z[h][i] - shift).astype(BF16)
            v = kv[i].values(h // 2)
            ones = jnp.ones_like(v)
            first = kv[i].head_lanes()
            v = jnp.where(first, v, ones) if h % 2 == 0 else jnp.where(first, ones, v)
            pv = kv[i].weighted(p, v)
            a = pv if a is None else a + pv
        acc.append(a)
    for p in range(n_pairs):
        a0, a1 = acc[2 * p], acc[2 * p + 1]
        o_ref[:, _pair_cols(p)] = jnp.where(lo, a0 / pltpu.roll(a0, HEAD_DIM, axis=1),
                                            a1 / pltpu.roll(a1, HEAD_DIM, axis=1)).astype(o_ref.dtype)


def _band_prompt_kernel(q_ref, k_ref, v_ref, bias_ref, o_ref, *, tq, tk, n_pairs, blocks_per_step):
    def one_block(s, carry):
        qi = pl.program_id(2) * blocks_per_step + s
        rows = pl.ds(pl.multiple_of(s * tq, tq), tq)
        q_heads, lo = _split_heads(q_ref.at[rows, :], n_pairs, tq)
        kv, pens = [], []
        for dj in range(3):
            j = qi - dj
            pens.append(None if dj == 0 else jnp.where(j >= 0, 0.0, NEG_BIG).astype(F32))
            keys = pl.ds(pl.multiple_of(jnp.maximum(j, 0) * tk, tk), tk)
            kv.append(_KV(k_ref[:, keys], v_ref[:, keys], True))
        _band_core(q_heads, lo, kv, pens, lambda h, i: bias_ref[h, i], o_ref.at[rows, :], n_pairs)
        return carry

    lax.fori_loop(0, blocks_per_step, one_block, 0)


def _band_sample_kernel(q_ref, kn_ref, vn_ref, kc_ref, vc_ref, bias_ref, o_ref, kpad_ref, vpad_ref,
                        *, ts, tk, n_cache_blocks, n_pairs):
    q_heads, lo = _split_heads(q_ref, n_pairs, ts)
    kpad_ref[...] = jnp.zeros_like(kpad_ref)
    vpad_ref[...] = jnp.zeros_like(vpad_ref)
    kpad_ref[:ts, :] = kn_ref[...]
    vpad_ref[:ts, :] = vn_ref[...]
    kv = [_KV(kpad_ref[...], vpad_ref[...], False)]
    for dj in range(1, n_cache_blocks + 1):
        keys = slice((n_cache_blocks - dj) * tk, (n_cache_blocks - dj + 1) * tk)
        kv.append(_KV(kc_ref[:, keys].astype(BF16), vc_ref[:, keys].astype(BF16), True))
    is_new_key = lax.broadcasted_iota(jnp.int32, (ts, tk), 1) < ts

    def bias(h, i):
        return jnp.where(is_new_key, bias_ref[h, 0], NEG_BIG) if i == 0 else bias_ref[h, i]

    _band_core(q_heads, lo, kv, [None] * len(kv), bias, o_ref, n_pairs)


def _band_attention_prompt(q, k, v, bias):
    b, t, _ = q.shape
    tq = tk = ATT_BLOCK
    assert t % tk == 0
    n_pairs = BAND_PAIRS_PER_STEP
    w = n_pairs * LANES
    per_step = math.gcd(t // tq, BAND_BLOCKS_PER_STEP)
    kern = functools.partial(_band_prompt_kernel, tq=tq, tk=tk, n_pairs=n_pairs, blocks_per_step=per_step)
    kv_spec = pl.BlockSpec((None, w, t), lambda bi, hp, qi: (bi, hp, 0))
    q_spec = pl.BlockSpec((None, per_step * tq, w), lambda bi, hp, qi: (bi, qi, hp))
    bias_spec = pl.BlockSpec((2 * n_pairs, 3, tq, tk), lambda bi, hp, qi: (hp, 0, 0, 0))
    return pl.pallas_call(
        kern,
        grid=(b, C_WIDTH // w, t // (per_step * tq)),
        in_specs=[q_spec, kv_spec, kv_spec, bias_spec],
        out_specs=q_spec,
        out_shape=jax.ShapeDtypeStruct(q.shape, BF16),
        compiler_params=_params("parallel", "parallel", "parallel"),
        name="band_attention",
    )(q, k, v, bias)


def _band_attention_sample(q, k_new, v_new, k_cache, v_cache, layer, bias):
    b, ts, _ = q.shape
    wc = k_cache.shape[3]
    tk = ATT_BLOCK
    n_cache_blocks = min(wc // tk, 2)
    assert wc % (n_cache_blocks * tk) == 0 and ts <= tk
    n_pairs = BAND_PAIRS_PER_STEP
    w = n_pairs * LANES
    kern = functools.partial(_band_sample_kernel, ts=ts, tk=tk, n_cache_blocks=n_cache_blocks,
                             n_pairs=n_pairs)
    new_spec = pl.BlockSpec((None, ts, w), lambda bi, hp: (bi, 0, hp))
    cache_rows = n_cache_blocks * tk
    cache_spec = pl.BlockSpec((None, None, w, cache_rows),
                              lambda bi, hp: (layer, bi, hp, wc // cache_rows - 1))
    bias_spec = pl.BlockSpec((2 * n_pairs, 1 + n_cache_blocks, ts, tk), lambda bi, hp: (hp, 0, 0, 0))
    return pl.pallas_call(
        kern,
        grid=(b, C_WIDTH // w),
        in_specs=[new_spec, new_spec, new_spec, cache_spec, cache_spec, bias_spec],
        out_specs=new_spec,
        out_shape=jax.ShapeDtypeStruct(q.shape, BF16),
        scratch_shapes=[pltpu.VMEM((tk, w), BF16), pltpu.VMEM((tk, w), BF16)],
        compiler_params=_params("parallel", "parallel"),
        name="band_attention_sample",
    )(q, k_new, v_new, k_cache, v_cache, bias)


def _band_bias_kernel(g_ref, o_ref):
    rows = tk = ATT_BLOCK
    q_chunk = lax.broadcasted_iota(jnp.int32, (rows, tk), 0) // CHUNK
    k_chunk = lax.broadcasted_iota(jnp.int32, (rows, tk), 1) // CHUNK
    for dj in range(3):
        g = jnp.broadcast_to(g_ref[0, dj], (rows, 2 * tk))
        tile = pltpu.roll(g, 0, axis=1, stride=1, stride_axis=0)[:, :tk]
        diff = dj * (tk // CHUNK) + q_chunk - k_chunk
        seen = (diff >= 0) & (diff <= C_LEFT_CHUNKS)
        o_ref[0, dj] = jnp.where(seen, tile * LOG2E, NEG_BIG)


def _band_bias(rel_table):
    rows = tk = ATT_BLOCK
    c = jnp.arange(2 * tk, dtype=jnp.int32)
    u = jnp.where(c <= tk, -c, 2 * tk - c)
    idx = jnp.clip(jnp.arange(3, dtype=jnp.int32)[:, None] * tk + u[None, :], REL_MIN, REL_MAX) - REL_MIN
    g = rel_table[:, idx].astype(F32).reshape(C_HEADS, 3, 1, 2 * tk)
    return pl.pallas_call(
        _band_bias_kernel,
        grid=(C_HEADS,),
        in_specs=[pl.BlockSpec((1, 3, 1, 2 * tk), lambda h: (h, 0, 0, 0))],
        out_specs=pl.BlockSpec((1, 3, rows, tk), lambda h: (h, 0, 0, 0)),
        out_shape=jax.ShapeDtypeStruct((C_HEADS, 3, rows, tk), F32),
        compiler_params=_params("parallel"),
        name="band_bias",
    )(g)


def _gla_pair_tile(q, k, v, la, states, emit):
    L = GLA_CHUNK
    chunks = range(q.shape[0] // L)
    row = lax.broadcasted_iota(jnp.int32, (L, L), 0)
    colm = lax.broadcasted_iota(jnp.int32, (L, L), 1)
    tri = (colm <= row).astype(BF16)
    causal = colm <= row
    lane = lax.broadcasted_iota(jnp.int32, (L, LANES), 1)
    sub = lax.broadcasted_iota(jnp.int32, (LANES, B_DV), 0)
    mine = [(lane >= h * B_DK) & (lane < (h + 1) * B_DK) for h in range(2)]
    mine_rows = [(sub >= h * B_DK) & (sub < (h + 1) * B_DK) for h in range(2)]
    rows = [slice(c * L, (c + 1) * L) for c in chunks]

    b = []
    for r in rows:
        g_hi, g_lo = _split_bf16(la[r, :])
        b.append(_dot(tri, g_hi) + _dot(tri, g_lo))
    yield
    qg, qg_h, kg, kd_t, decay = [], [], [], [], []
    for c, r in zip(chunks, rows):
        qg_c = q[r, :] * (B_DK ** -0.5) * jnp.exp(b[c])
        qg.append(qg_c.astype(BF16))
        qg_h.append([jnp.where(mine[h], qg_c, 0.0).astype(BF16) for h in range(2)])
        kg.append((k[r, :] * jnp.exp(-b[c])).astype(BF16))
        b_t = b[c].T
        b_last = b_t[:, L - 1:L]
        kd_t.append((k[r, :].T * jnp.exp(b_last - b_t)).astype(BF16))
        decay.append(jnp.exp(b_last))
    yield
    att =[[jnp.where(causal, _dot_nt(qg_h[c][h], kg[c]), 0.0).astype(BF16) for h in range(2)]
           for c in chunks]
    yield
    o_intra, own = [], []
    for c, r in zip(chunks, rows):
        v_h = [v[r, h * B_DV:(h + 1) * B_DV].astype(BF16) for h in range(2)]
        o_intra.append([_dot(att[c][h], v_h[h]) for h in range(2)])
        own.append([jnp.where(mine_rows[h], _dot(kd_t[c], v_h[h]), 0.0) for h in range(2)])

    yield
    states = list(states)
    start = []
    for c in chunks:
        start.append([s.astype(BF16) for s in states])
        states = [decay[c] * states[h] + own[c][h] for h in range(2)]
    yield
    for c in chunks:
        for h in range(2):
            emit(c, h, o_intra[c][h] + _dot(qg[c], start[c][h]))
    return states


def _pair_states(s0_ref, p):
    zeros_state = jnp.zeros((B_DK, B_DV), F32)
    return [jnp.concatenate([s0_ref[2 * p], zeros_state], axis=0),
            jnp.concatenate([zeros_state, s0_ref[2 * p + 1]], axis=0)]


def _pair_v_cols(p):
    return slice(2 * p * B_DV, 2 * (p + 1) * B_DV)


def _gla_rows_emit(o_ref, p):
    def emit(c, h, o):
        o_ref[c * GLA_CHUNK:(c + 1) * GLA_CHUNK, (2 * p + h) * B_DV:(2 * p + h + 1) * B_DV] = o
    return emit


def _gla_kernel(q_ref, k_ref, v_ref, la_ref, s0_ref, o_ref, s_out_ref):
    pairs = range(B_HEADS // 2)
    states = _interleave([
        _gla_pair_tile(q_ref[:, _pair_cols(p)], k_ref[:, _pair_cols(p)], v_ref[:, _pair_v_cols(p)],
                       la_ref[:, _pair_cols(p)], _pair_states(s0_ref, p), _gla_rows_emit(o_ref, p))
        for p in pairs])
    for p in pairs:
        s_out_ref[2 * p] = states[p][0][:B_DK, :]
        s_out_ref[2 * p + 1] = states[p][1][B_DK:, :]


def _gla(q, k, v, la, s0):
    b, t, _ = q.shape
    assert t % GLA_CHUNK == 0 and t <= GLA_MAX_TILE
    qk_spec = pl.BlockSpec((None, t, B_KW), lambda bi: (bi, 0, 0))
    v_spec = pl.BlockSpec((None, t, B_VW), lambda bi: (bi, 0, 0))
    s_spec = pl.BlockSpec((None, B_HEADS, B_DK, B_DV), lambda bi: (bi, 0, 0, 0))
    return pl.pallas_call(
        _gla_kernel,
        grid=(b,),
        in_specs=[qk_spec, qk_spec, v_spec, qk_spec, s_spec],
        out_specs=[v_spec, s_spec],
        out_shape=[jax.ShapeDtypeStruct(v.shape, F32), jax.ShapeDtypeStruct(s0.shape, F32)],
        compiler_params=_params("parallel"),
        name="gla",
    )(q, k, v, la, s0)


def _pad_rows(x, n):
    return jnp.pad(x, ((0, 0), (0, n - x.shape[1]), (0, 0)))


def _heads_last(x, heads):
    n, b, _, s = x.shape
    return jnp.transpose(x.reshape(n, b, heads, HEAD_DIM, s), (0, 1, 4, 2, 3))


def _feature_major(cache):
    n, b, s, heads, hd = cache.shape
    return jnp.transpose(cache, (0, 1, 3, 4, 2)).reshape(n, b, heads * hd, s)


def _row_tile(m):
    for tm in (512, 256, 128, 64, 32, 16, 8):
        if m % tm == 0:
            return tm
    raise ValueError(f"token count {m} is not a multiple of 8")


def kernel(x_prompt, x_sample, cache_a_k, cache_a_v, state_b, cache_c_k, cache_c_v, norm_mix_g, norm_ffn_g, w_in_ab, w_gate_b, b_gate_b, norm_gla_g, w_out_ab, w_qkv_c, rel_bias_c, w_out_c, w_ffn_gate, w_ffn_up, w_ffn_down, norm_final_g):
    bp, tp, d = x_prompt.shape
    bs, ts, _ = x_sample.shape
    depth = norm_mix_g.shape[0]
    past = cache_a_k.shape[2]
    wc = cache_c_k.shape[2]
    assert tp % ATT_BLOCK == 0 and past % ATT_BLOCK == 0 and wc % ATT_BLOCK == 0
    assert ts <= GLA_CHUNK and ts % 8 == 0
    mp, ms = bp * tp, bs * ts
    tmp, tms = _row_tile(tp), _row_tile(ms)
    xp = x_prompt.reshape(mp, d)
    xs = x_sample.reshape(ms, d)
    row2 = lambda a: a.reshape(1, -1)

    a_ks, a_vs, b_sp, b_ss, c_ks, c_vs = [], [], [], [], [], []
    a_kv_prompt = c_kv_prompt = None
    n_ab, n_c = (depth + 1) // 2, depth // 2
    keep = min(C_LEFT_CHUNKS * CHUNK, tp)

    kv0, kv1 = A_WIDTH, 3 * A_WIDTH
    o = 3 * A_WIDTH + 2 * B_KW + B_VW
    w_in_t = jnp.swapaxes(w_in_ab, 1, 2)
    w_main_all = jnp.concatenate([w_in_t[:, :kv0], w_in_t[:, kv1:o], w_in_t[:, o + B_GATE_RANK:]],
                                 axis=1).astype(BF16)
    w_kv_ab_t_all = w_in_t[:, kv0:kv1].astype(BF16)
    w_lr_all = jnp.pad(w_in_t[:, o:o + B_GATE_RANK],
                       ((0, 0), (0, LANES - B_GATE_RANK), (0, 0))).astype(BF16)
    w_gate_all = jnp.pad(w_gate_b, ((0, 0), (0, LANES - B_GATE_RANK), (0, 0))).astype(BF16)
    w_out_ab_all = w_out_ab.astype(BF16)
    w_q_all = w_qkv_c[:, :, :C_WIDTH].astype(BF16)
    w_kv_c_t_all = jnp.swapaxes(w_qkv_c[:, :, C_WIDTH:], 1, 2).astype(BF16)
    w_out_c_all = w_out_c.astype(BF16)
    cache_a_k_fm, cache_a_v_fm, cache_c_k_fm, cache_c_v_fm = (
        _feature_major(c) for c in (cache_a_k, cache_a_v, cache_c_k, cache_c_v))

    for layer in range(depth):
        i = layer // 2
        g_mix = row2(norm_mix_g[layer])
        riders = _Riders((w_ffn_gate, w_ffn_up, w_ffn_down), layer, mp // tmp)
        ffn_ends = (row2(norm_ffn_g[layer]), row2(norm_final_g) if layer == depth - 1 else None)
        if layer % 2 == 0:
            w_main, w_kv_t, w_lr, w_gate, w_out = (
                _Slab(w, i) for w in (w_main_all, w_kv_ab_t_all, w_lr_all, w_gate_all, w_out_ab_all))
            b_gate = row2(b_gate_b[i])
            g_gla = row2(norm_gla_g[i])

            qa, ka, va, kab, vab, qb, kb, vb, r, la, *ffn_w = _proj_ab(
                xp, g_mix, w_main, w_kv_t, w_lr, w_gate, b_gate, tmp, batch=bp,
                stack=_KVStack(i, n_ab, tp, a_kv_prompt), riders=riders)
            ffn = (ffn_ends[0], *ffn_w, ffn_ends[1])
            a_kv_prompt = (ka, va)
            sh = lambda a: a.reshape(bp, tp, -1)
            oa, ob, sbp = _mixer_ab_prompt(sh(qa), kab, vab, sh(qb), sh(kb), sh(vb), sh(la))
            mixed_p = (oa.reshape(mp, -1), ob.reshape(mp, -1), r)
            b_sp.append(sbp)

            qa, ka, va, kab, vab, qb, kb, vb, r, la = _proj_ab(
                xs, g_mix, w_main, w_kv_t, w_lr, w_gate, b_gate, tms)
            sh = lambda a: a.reshape(bs, ts, -1)
            oa = _sb_attention_sample(sh(qa), sh(kab), sh(vab), cache_a_k_fm, cache_a_v_fm, i)
            pad_t = lambda a: _pad_rows(sh(a), GLA_CHUNK)
            ob, sbs = _gla(pad_t(qb), pad_t(kb), pad_t(vb), pad_t(la), state_b[i])
            mixed_s = (oa.reshape(ms, -1), ob[:, :ts].reshape(ms, -1), r)
            xp, xs = _layer_tail(xp, mixed_p, xs, mixed_s, (g_gla, w_out), *ffn, tmp)
            a_ks.append(ka.reshape(bs, ts, A_HEADS, HEAD_DIM))
            a_vs.append(va.reshape(bs, ts, A_HEADS, HEAD_DIM))
            b_ss.append(sbs)
        else:
            w_q, w_kv_t, w_out = (_Slab(w, i) for w in (w_q_all, w_kv_c_t_all, w_out_c_all))

            q, k, v, kb16, vb16, *ffn_w = _proj_c(xp, g_mix, w_q, w_kv_t, tmp, batch=bp,
                                                  stack=_KVStack(i, n_c, keep, c_kv_prompt), riders=riders)
            ffn = (ffn_ends[0], *ffn_w, ffn_ends[1])
            c_kv_prompt = (k, v)
            bias = _band_bias(rel_bias_c[i])
            oc = _band_attention_prompt(q.reshape(bp, tp, -1), kb16, vb16, bias)
            mixed_p = (oc.reshape(mp, -1),)

            q, k, v, kb16, vb16 = _proj_c(xs, g_mix, w_q, w_kv_t, tms)
            sh = lambda a: a.reshape(bs, ts, -1)
            oc = _band_attention_sample(sh(q), sh(kb16), sh(vb16), cache_c_k_fm, cache_c_v_fm, i, bias)
            xp, xs = _layer_tail(xp, mixed_p, xs, (oc.reshape(ms, -1),), (w_out,), *ffn, tmp)
            c_ks.append(k.reshape(bs, ts, C_HEADS, HEAD_DIM))
            c_vs.append(v.reshape(bs, ts, C_HEADS, HEAD_DIM))

    y_prompt = xp.reshape(bp, tp, d)
    y_sample = xs.reshape(bs, ts, d)
    a_kp, a_vp = (_heads_last(a, A_HEADS) for a in a_kv_prompt)
    c_kp, c_vp = (_heads_last(a, C_HEADS) for a in c_kv_prompt)
    return (y_prompt, y_sample, a_kp, a_vp, jnp.stack(a_ks), jnp.stack(a_vs),
            jnp.stack(b_sp), jnp.stack(b_ss), c_kp, c_vp, jnp.stack(c_ks), jnp.stack(c_vs))
```

```python
import functools
import math

import jax
import jax.numpy as jnp
from jax import lax
from jax.experimental import pallas as pl
from jax.experimental.pallas import tpu as pltpu

F32 = jnp.float32
BF16 = jnp.bfloat16

EPS = 1e-6
HEAD_DIM = 64
LANES = 128
A_HEADS = 8
A_WIDTH = A_HEADS * HEAD_DIM
B_HEADS = 4
B_DK = 64
B_DV = 128
B_KW = B_HEADS * B_DK
B_VW = B_HEADS * B_DV
B_GATE_RANK = 16
B_GATE_TEMP = 16.0
GLA_CHUNK = 64
C_HEADS = 16
C_WIDTH = C_HEADS * HEAD_DIM
CHUNK = 64
C_LEFT_CHUNKS = 8
REL_MIN = -(CHUNK - 1)
REL_MAX = 128
ATT_BLOCK = 256
NEG_BIG = -1e30
LOG2E = 1.4426950408889634
SB_DEAD = 104.0
SB_PAIRS_PER_STEP = 4
BAND_PAIRS_PER_STEP = 4
MIXER_BLOCKS_PER_STEP = 4
BAND_BLOCKS_PER_STEP = 8
GLA_MAX_TILE = 512
VMEM_LIMIT = 56 * 1024 * 1024


def _params(*sem):
    return pltpu.CompilerParams(dimension_semantics=sem, vmem_limit_bytes=VMEM_LIMIT)


class _Slab:
    def __init__(self, stacked, index):
        self.stacked, self.index, self.shape = stacked, index, stacked.shape[1:]


def _operand(a):
    return a.stacked if isinstance(a, _Slab) else a


def _resident(a):
    if isinstance(a, _Slab):
        index = (a.index,) + (0,) * len(a.shape)
        return pl.BlockSpec((None, *a.shape), lambda *_: index, pipeline_mode=pl.Buffered(1))
    return pl.BlockSpec(a.shape, lambda *_: (0,) * a.ndim, pipeline_mode=pl.Buffered(1))


def _rms(x, g):
    return x * lax.rsqrt(jnp.mean(x * x, axis=-1, keepdims=True) + EPS) * g


def _log_sigmoid_pair(z):
    l = jnp.log1p(jnp.exp(-jnp.abs(z)))
    return jnp.minimum(z, 0.0) - l, jnp.minimum(-z, 0.0) - l


def _split_bf16(x):
    hi = x.astype(BF16)
    lo = (x - hi.astype(F32)).astype(BF16)
    return hi, lo


def _dot(a, b):
    return jnp.dot(a, b, preferred_element_type=F32)


def _dot_nt(a, b):
    return lax.dot_general(a, b, (((1,), (1,)), ((), ())), preferred_element_type=F32)


def _dot_tn(a, b):
    return lax.dot_general(a, b, (((0,), (0,)), ((), ())), preferred_element_type=F32)


def _pair_cols(p):
    return slice(p * LANES, (p + 1) * LANES)


class _KV:
    def __init__(self, k, v, feature_major):
        self.k, self.v, self.feature_major = k, v, feature_major
        self.n_keys = k.shape[1] if feature_major else k.shape[0]

    def scores(self, q_h, p):
        if self.feature_major:
            return _dot(q_h, self.k[_pair_cols(p), :])
        return _dot_nt(q_h, self.k[:, _pair_cols(p)])

    def values(self, p):
        return self.v[_pair_cols(p), :] if self.feature_major else self.v[:, _pair_cols(p)]

    def weighted(self, w, v_p):
        return _dot_nt(w, v_p) if self.feature_major else _dot(w, v_p)

    def head_lanes(self):
        shape = (LANES, self.n_keys) if self.feature_major else (self.n_keys, LANES)
        return lax.broadcasted_iota(jnp.int32, shape, 0 if self.feature_major else 1) < HEAD_DIM


def _split_heads(q_ref, n_pairs, tq):
    lo = lax.broadcasted_iota(jnp.int32, (tq, LANES), 1) < HEAD_DIM
    heads = []
    for p in range(n_pairs):
        q = q_ref[:, _pair_cols(p)]
        heads += [jnp.where(lo, q, jnp.zeros_like(q)), jnp.where(lo, jnp.zeros_like(q), q)]
    return heads, lo


def _emit_kv(y, wkv_ref, k_ref, v_ref, kb_ref, vb_ref, feature_major, kept_tail=None, own_slot=0):
    if not feature_major:
        kv = _dot_nt(y, wkv_ref[...])
        width = kv.shape[1] // 2
        k, v = kv[:, :width], kv[:, width:]
        k_ref[...], v_ref[...], kb_ref[...], vb_ref[...] = k, v, k.astype(BF16), v.astype(BF16)
        return
    kv = _dot_nt(wkv_ref[...], y)
    width = kv.shape[0] // 2
    k, v = kv[:width, :], kv[width:, :]
    kb_ref[...] = k.astype(BF16)
    vb_ref[...] = v.astype(BF16)
    slot = own_slot
    other_slots = [s for s in range(k_ref.shape[0]) if s != slot]
    if kept_tail is None:
        fill, fill_cols = slice(None), k_ref.shape[2]
        k_ref[slot] = k
        v_ref[slot] = v
    else:
        n_tiles, n_kept = kept_tail
        tile = pl.program_id(0) % n_tiles
        tm = k.shape[1]
        cols = pl.ds(pl.multiple_of(jnp.maximum(tile - (n_tiles - n_kept), 0) * tm, tm), tm)
        k_ref[slot, :, cols] = k
        v_ref[slot, :, cols] = v
        share = k_ref.shape[2] // n_tiles
        if share % LANES == 0 and share * n_tiles == k_ref.shape[2]:
            fill, fill_cols = pl.ds(pl.multiple_of(tile * share, share), share), share
        else:
            fill, fill_cols = slice(None), k_ref.shape[2]
    for dst in (k_ref, v_ref):
        for s in other_slots:
            dst[s, :, fill] = jnp.zeros((dst.shape[1], fill_cols), dst.dtype)


class _Riders:
    def __init__(self, stacks, layer, steps):
        self.stacks, self.layer = list(stacks), layer
        self.chunks = next(c for c in (16, 8, 4, 2, 1)
                           if c <= steps and all(w.shape[1] % (16 * c) == 0 for w in self.stacks))

    def specs(self):
        layer, last = self.layer, self.chunks - 1
        ins = [pl.BlockSpec((None, w.shape[1] // self.chunks, w.shape[2]),
                            lambda i: (layer, jnp.minimum(i, last), 0)) for w in self.stacks]
        outs = [pl.BlockSpec((w.shape[1] // self.chunks, w.shape[2]),
                             lambda i: (jnp.minimum(i, last), 0)) for w in self.stacks]
        shapes = [jax.ShapeDtypeStruct(w.shape[1:], BF16) for w in self.stacks]
        return ins, outs, shapes


def _split_extra_refs(refs, n_aliased, n_riders):
    rider_in = refs[n_aliased:n_aliased + n_riders]
    outs = refs[n_aliased + n_riders:len(refs) - n_riders]
    return outs, list(zip(rider_in, refs[len(refs) - n_riders:]))


def _cast_riders(pairs):
    for src, dst in pairs:
        dst[...] = src[...].astype(BF16)


def _proj_ab_kernel(x_ref, g_ref, w_ref, wkv_ref, wlr_ref, wgate_ref, bgate_ref, *refs,
                    feature_major, n_aliased, n_riders, own_slot):
    outs, riders = _split_extra_refs(refs, n_aliased, n_riders)
    qa_ref, ka_ref, va_ref, kab_ref, vab_ref, qb_ref, kb_ref, vb_ref, r_ref, la_ref = outs
    _cast_riders(riders)
    y = _rms(x_ref[...], g_ref[...]).astype(BF16)
    _emit_kv(y, wkv_ref, ka_ref, va_ref, kab_ref, vab_ref, feature_major, own_slot=own_slot)
    z = _dot_nt(y, w_ref[...])
    c = 0
    qa_ref[...] = (z[:, c:c + A_WIDTH] * (HEAD_DIM ** -0.5)).astype(BF16); c += A_WIDTH
    qb_ref[...] = z[:, c:c + B_KW]; c += B_KW
    kb_ref[...] = z[:, c:c + B_KW]; c += B_KW
    vb_ref[...] = z[:, c:c + B_VW]; c += B_VW
    r_ref[...] = z[:, c:c + B_VW]
    g_lr = _dot_nt(y, wlr_ref[...])
    gate = _dot(g_lr.astype(BF16), wgate_ref[...]) + bgate_ref[...]
    la_ref[...] = _log_sigmoid_pair(gate)[0] * (1.0 / B_GATE_TEMP)


class _KVStack:
    def __init__(self, layer, n_layers, keep, previous=None):
        self.layer, self.n_layers, self.keep, self.previous = layer, n_layers, keep, previous


def _kv_out(m, width, tm, batch, stack):
    if batch is None:
        spec = pl.BlockSpec((tm, width), lambda i: (i, 0))
        return [spec] * 4, [jax.ShapeDtypeStruct((m, width), dt) for dt in (F32, F32, BF16, BF16)], None, 0
    t = m // batch
    assert t % tm == 0 and stack.keep % tm == 0 and stack.keep <= t
    n_tiles = t // tm
    copy_spec = pl.BlockSpec((None, width, tm), lambda i: (i // n_tiles, 0, i % n_tiles))
    copy_shape = jax.ShapeDtypeStruct((batch, width, t), BF16)
    slots, first_slot, own_slot = ((stack.n_layers, 0, stack.layer) if stack.previous is None
                                   else (1, stack.layer, 0))
    if stack.keep == t:
        f32_spec = pl.BlockSpec((slots, None, width, tm), lambda i: (first_slot, i // n_tiles, 0, i % n_tiles))
        kept_tail = None
    else:
        f32_spec = pl.BlockSpec((slots, None, width, stack.keep), lambda i: (first_slot, i // n_tiles, 0, 0))
        kept_tail = (n_tiles, stack.keep // tm)
    f32_shape = jax.ShapeDtypeStruct((stack.n_layers, batch, width, stack.keep), F32)
    return ([f32_spec, f32_spec, copy_spec, copy_spec], [f32_shape, f32_shape, copy_shape, copy_shape],
            kept_tail, own_slot)


def _aliased_stack(stack, n_inputs, first_output):
    if stack is None or stack.previous is None:
        return [], [], {}
    prev = list(stack.previous)
    specs = [pl.BlockSpec(memory_space=pl.ANY)] * len(prev)
    return prev, specs, {n_inputs + j: first_output + j for j in range(len(prev))}


def _rider_plumbing(riders):
    if riders is None:
        return [], [], [], []
    return (riders.stacks, *riders.specs())


def _proj_ab(x, g, w_main, w_kv, w_lr, w_gate, b_gate, tm, batch=None, stack=None, riders=None):
    m, d = x.shape
    row = lambda n: pl.BlockSpec((tm, n), lambda i: (i, 0))
    params = (g, w_main, w_kv, w_lr, w_gate, b_gate)
    kv_specs, kv_shapes, kept_tail, own_slot = _kv_out(m, A_WIDTH, tm, batch, stack)
    assert kept_tail is None
    prev, prev_specs, aliases = _aliased_stack(stack, 1 + len(params), 1)
    ride_ops, ride_in, ride_out, ride_shapes = _rider_plumbing(riders)
    rest = [(B_KW, F32), (B_KW, F32), (B_VW, F32), (B_VW, F32), (B_KW, F32)]
    return pl.pallas_call(
        functools.partial(_proj_ab_kernel, feature_major=batch is not None, n_aliased=len(prev),
                          n_riders=len(ride_ops), own_slot=own_slot),
        grid=(m // tm,),
        in_specs=[row(d)] + [_resident(a) for a in params] + prev_specs + ride_in,
        out_specs=[row(A_WIDTH)] + kv_specs + [row(n) for n, _ in rest] + ride_out,
        out_shape=([jax.ShapeDtypeStruct((m, A_WIDTH), BF16)] + kv_shapes
                   + [jax.ShapeDtypeStruct((m, n), dt) for n, dt in rest] + ride_shapes),
        input_output_aliases=aliases,
        compiler_params=_params("arbitrary" if ride_ops else "parallel"),
        name="proj_ab",
    )(x, *[_operand(a) for a in params], *prev, *ride_ops)


def _proj_c_kernel(x_ref, g_ref, wq_ref, wkv_ref, *refs, feature_major, n_aliased, kept_tail,
                   n_riders, own_slot):
    outs, riders = _split_extra_refs(refs, n_aliased, n_riders)
    q_ref, k_ref, v_ref, kb_ref, vb_ref = outs
    _cast_riders(riders)
    y = _rms(x_ref[...], g_ref[...]).astype(BF16)
    _emit_kv(y, wkv_ref, k_ref, v_ref, kb_ref, vb_ref, feature_major, kept_tail, own_slot)
    q_ref[...] = (_dot(y, wq_ref[...]) * (HEAD_DIM ** -0.5 * LOG2E)).astype(BF16)


def _proj_c(x, g, w_q, w_kv, tm, batch=None, stack=None, riders=None):
    m, d = x.shape
    row = lambda n: pl.BlockSpec((tm, n), lambda i: (i, 0))
    params = (g, w_q, w_kv)
    kv_specs, kv_shapes, kept_tail, own_slot = _kv_out(m, C_WIDTH, tm, batch, stack)
    prev, prev_specs, aliases = _aliased_stack(stack, 1 + len(params), 1)
    ride_ops, ride_in, ride_out, ride_shapes = _rider_plumbing(riders)
    return pl.pallas_call(
        functools.partial(_proj_c_kernel, feature_major=batch is not None, n_aliased=len(prev),
                          kept_tail=kept_tail, n_riders=len(ride_ops), own_slot=own_slot),
        grid=(m // tm,),
        in_specs=[row(d)] + [_resident(a) for a in params] + prev_specs + ride_in,
        out_specs=[row(C_WIDTH)] + kv_specs + ride_out,
        out_shape=[jax.ShapeDtypeStruct((m, C_WIDTH), BF16)] + kv_shapes + ride_shapes,
        input_output_aliases=aliases,
        compiler_params=_params("arbitrary" if (kept_tail or ride_ops) else "parallel"),
        name="proj_c",
    )(x, *[_operand(a) for a in params], *prev, *ride_ops)


def _layer_tail_rows(x_ref, mixer_refs, param_refs, o_ref, gla_merge, final_norm):
    if gla_merge:
        oa_ref, ob_ref, r_ref = mixer_refs
        ggla_ref, wout_ref, *param_refs = param_refs
        ob = ob_ref[...]
        parts = []
        for h in range(B_HEADS):
            seg = ob[:, h * B_DV:(h + 1) * B_DV]
            parts.append(seg * lax.rsqrt(jnp.mean(seg * seg, axis=-1, keepdims=True) + EPS))
        r = r_ref[...]
        obn = jnp.concatenate(parts, axis=-1) * ggla_ref[...] * (r * jax.nn.sigmoid(r))
        mix = _dot(oa_ref[...], wout_ref[:A_WIDTH, :]) + _dot(obn.astype(BF16), wout_ref[A_WIDTH:, :])
    else:
        (oc_ref,) = mixer_refs
        wout_ref, *param_refs = param_refs
        mix = _dot(oc_ref[...], wout_ref[...])
    gffn_ref, wg_ref, wu_ref, wd_ref, *param_refs = param_refs
    x = x_ref[...] + mix
    y = _rms(x, gffn_ref[...]).astype(BF16)
    h = _dot(y, wg_ref[...])
    u = _dot(y, wu_ref[...])
    a = (h * jax.nn.sigmoid(h) * u).astype(BF16)
    x = x + _dot(a, wd_ref[...])
    o_ref[...] = _rms(x, param_refs[0][...]) if final_norm else x


def _layer_tail_kernel(*refs, n_mixer, n_params, gla_merge, final_norm, main_steps):
    main, extra = refs[:1 + n_mixer], refs[1 + n_mixer:2 + 2 * n_mixer]
    params = refs[2 + 2 * n_mixer:2 + 2 * n_mixer + n_params]
    o_ref, o_extra_ref = refs[-2:]

    @pl.when(pl.program_id(0) < main_steps)
    def _():
        _layer_tail_rows(main[0], main[1:], params, o_ref, gla_merge, final_norm)

    @pl.when(pl.program_id(0) == main_steps)
    def _():
        _layer_tail_rows(extra[0], extra[1:], params, o_extra_ref, gla_merge, final_norm)


def _layer_tail(x, mixer_out, x_extra, mixer_out_extra, mixer_params, g_ffn, wg, wu, wd, g_fin, tm):
    m, d = x.shape
    main_steps = m // tm
    row = lambda a: pl.BlockSpec((tm, a.shape[1]), lambda i: (jnp.minimum(i, main_steps - 1), 0))
    whole = lambda a: pl.BlockSpec(a.shape, lambda i: (0, 0))
    params = [*mixer_params, g_ffn, wg, wu, wd] + ([] if g_fin is None else [g_fin])
    return pl.pallas_call(
        functools.partial(_layer_tail_kernel, n_mixer=len(mixer_out), n_params=len(params),
                          gla_merge=len(mixer_out) == 3, final_norm=g_fin is not None, main_steps=main_steps),
        grid=(main_steps + 1,),
        in_specs=([row(x)] + [row(a) for a in mixer_out] + [whole(x_extra)]
                  + [whole(a) for a in mixer_out_extra] + [_resident(a) for a in params]),
        out_specs=[row(x), whole(x_extra)],
        out_shape=[jax.ShapeDtypeStruct((m, d), F32), jax.ShapeDtypeStruct(x_extra.shape, F32)],
        compiler_params=_params("arbitrary"),
        name="layer_tail",
    )(x, *mixer_out, x_extra, *mixer_out_extra, *[_operand(a) for a in params])


def _interleave(stage_generators):
    results = [None] * len(stage_generators)
    live = list(range(len(stage_generators)))
    while live:
        for i in list(live):
            try:
                next(stage_generators[i])
            except StopIteration as done:
                results[i] = done.value
                live.remove(i)
    return results


def _sb_core(q_heads, lo, first_kv, first_mask, earlier_kv, n_earlier, acc_ref, c_ref, o_ref, n_pairs,
             companions=(), on_companions=None):
    tk = first_kv.n_keys
    heads = range(2 * n_pairs)
    later = (lax.broadcasted_iota(jnp.int32, (tk, tk), 0)
             > lax.broadcasted_iota(jnp.int32, (tk, tk), 1)).astype(BF16)
    c_ref[...] = jnp.zeros_like(c_ref)

    def block(kv, mask):
        z = [kv.scores(q_heads[h], h // 2) for h in heads]
        yield
        log_beta, drop, after = [], [], []
        for h in heads:
            d = jnp.maximum(z[h], 0.0) + jnp.log(1.0 + jnp.exp(-jnp.abs(z[h])))
            log_beta.append(z[h] - d)
            drop.append(d if mask is None else jnp.where(mask, d, 0.0))
        yield
        for h in heads:
            hi, lo_part = _split_bf16(drop[h])
            after.append(_dot(hi, later) + _dot(lo_part, later))
        yield
        pv = []
        for h in heads:
            c = c_ref[h]
            w = jnp.exp(log_beta[h] - after[h] - c)
            if mask is not None:
                w = jnp.where(mask, w, 0.0)
            c_ref[h] = c + after[h][:, 0:1] + drop[h][:, 0:1]
            pv.append(kv.weighted(w.astype(BF16), kv.values(h // 2)))
        yield
        out = [jnp.where(lo, pv[2 * p], pv[2 * p + 1]) for p in range(n_pairs)]
        return out[0] if n_pairs == 1 else jnp.concatenate(out, axis=-1)

    def all_dead():
        return jnp.min(c_ref[...]) > SB_DEAD

    first, *companion_results = _interleave([block(first_kv, first_mask), *companions])
    acc_ref[...] = first
    if companions:
        on_companions(companion_results)

    def cond(carry):
        n, dead = carry
        return (n < n_earlier) & jnp.logical_not(dead)

    def body(carry):
        n, _ = carry
        acc_ref[...] += _interleave([block(earlier_kv(n), None)])[0]
        return n + 1, all_dead()

    lax.while_loop(cond, body, (jnp.int32(0), all_dead()))
    o_ref[...] = acc_ref[...].astype(o_ref.dtype)


def _mixer_ab_prompt_kernel(q_ref, k_ref, v_ref, qb_ref, kb_ref, vb_ref, la_ref, o_ref, ob_ref, s_out_ref,
                            acc_ref, c_ref, carry_ref, *, tb, n_pairs, blocks_per_step):
    pairs = range(B_HEADS // 2)
    strictly_earlier = (lax.broadcasted_iota(jnp.int32, (tb, tb), 1)
                        < lax.broadcasted_iota(jnp.int32, (tb, tb), 0))

    def kv_block(j):
        keys = pl.ds(pl.multiple_of(j * tb, tb), tb)
        return _KV(k_ref[:, keys], v_ref[:, keys], True)

    def keep_states(states):
        for p in pairs:
            carry_ref[2 * p], carry_ref[2 * p + 1] = states[p]
            s_out_ref[2 * p] = states[p][0][:B_DK, :]
            s_out_ref[2 * p + 1] = states[p][1][B_DK:, :]

    @pl.when(pl.program_id(1) == 0)
    def _():
        carry_ref[...] = jnp.zeros_like(carry_ref)

    def one_block(s, carry):
        qi = pl.program_id(1) * blocks_per_step + s
        rows = pl.ds(pl.multiple_of(s * tb, tb), tb)
        gla = [_gla_pair_tile(qb_ref[rows, _pair_cols(p)], kb_ref[rows, _pair_cols(p)],
                              vb_ref[rows, _pair_v_cols(p)], la_ref[rows, _pair_cols(p)],
                              [carry_ref[2 * p], carry_ref[2 * p + 1]],
                              _gla_rows_emit(ob_ref.at[rows, :], p)) for p in pairs]
        q_heads, lo = _split_heads(q_ref.at[rows, :], n_pairs, tb)
        _sb_core(q_heads, lo, kv_block(qi), strictly_earlier, lambda n: kv_block(qi - 1 - n), qi,
                 acc_ref, c_ref, o_ref.at[rows, :], n_pairs, companions=gla, on_companions=keep_states)
        return carry

    lax.fori_loop(0, blocks_per_step, one_block, 0)


def _sb_sample_kernel(q_ref, kn_ref, vn_ref, kc_ref, vc_ref, o_ref, acc_ref, c_ref, kpad_ref, vpad_ref,
                      *, ts, tk, n_cache_blocks, n_pairs):
    q_heads, lo = _split_heads(q_ref, n_pairs, ts)
    kpad_ref[...] = jnp.zeros_like(kpad_ref)
    vpad_ref[...] = jnp.zeros_like(vpad_ref)
    kpad_ref[:ts, :] = kn_ref[...]
    vpad_ref[:ts, :] = vn_ref[...]

    def cache_block(n):
        keys = pl.ds(pl.multiple_of((n_cache_blocks - 1 - n) * tk, tk), tk)
        return _KV(kc_ref[:, keys].astype(BF16), vc_ref[:, keys].astype(BF16), True)

    strictly_earlier = (lax.broadcasted_iota(jnp.int32, (ts, tk), 1)
                        < lax.broadcasted_iota(jnp.int32, (ts, tk), 0))
    _sb_core(q_heads, lo, _KV(kpad_ref[...], vpad_ref[...], False), strictly_earlier, cache_block,
             n_cache_blocks, acc_ref, c_ref, o_ref, n_pairs)


def _mixer_ab_prompt(q, k, v, qb, kb, vb, la):
    b, t, _ = q.shape
    tb = ATT_BLOCK
    assert t % tb == 0 and tb % GLA_CHUNK == 0 and tb <= GLA_MAX_TILE
    n_pairs = A_WIDTH // LANES
    per_step = math.gcd(t // tb, MIXER_BLOCKS_PER_STEP)
    rows = lambda width: pl.BlockSpec((None, per_step * tb, width), lambda bi, qi: (bi, qi, 0))
    kv_spec = pl.BlockSpec((None, A_WIDTH, t), lambda bi, qi: (bi, 0, 0))
    state = (B_HEADS, B_DK, B_DV)
    return pl.pallas_call(
        functools.partial(_mixer_ab_prompt_kernel, tb=tb, n_pairs=n_pairs, blocks_per_step=per_step),
        grid=(b, t // (per_step * tb)),
        in_specs=[rows(A_WIDTH), kv_spec, kv_spec, rows(B_KW), rows(B_KW), rows(B_VW), rows(B_KW)],
        out_specs=[rows(A_WIDTH), rows(B_VW), pl.BlockSpec((None, *state), lambda bi, qi: (bi, 0, 0, 0))],
        out_shape=[jax.ShapeDtypeStruct(q.shape, BF16), jax.ShapeDtypeStruct(vb.shape, F32),
                   jax.ShapeDtypeStruct((b, *state), F32)],
        scratch_shapes=[pltpu.VMEM((tb, A_WIDTH), F32), pltpu.VMEM((2 * n_pairs, tb, 1), F32),
                        pltpu.VMEM((B_HEADS, LANES, B_DV), F32)],
        compiler_params=_params("parallel", "arbitrary"),
        name="mixer_ab",
    )(q, k, v, qb, kb, vb, la)


def _sb_attention_sample(q, k_new, v_new, k_cache, v_cache, layer):
    b, ts, _ = q.shape
    past = k_cache.shape[3]
    tk = ATT_BLOCK
    assert past % tk == 0 and ts <= tk
    n_pairs = SB_PAIRS_PER_STEP
    w = n_pairs * LANES
    new_spec = pl.BlockSpec((None, ts, w), lambda bi, hp: (bi, 0, hp))
    cache_spec = pl.BlockSpec((None, None, w, past), lambda bi, hp: (layer, bi, hp, 0))
    return pl.pallas_call(
        functools.partial(_sb_sample_kernel, ts=ts, tk=tk, n_cache_blocks=past // tk, n_pairs=n_pairs),
        grid=(b, A_WIDTH // w),
        in_specs=[new_spec, new_spec, new_spec, cache_spec, cache_spec],
        out_specs=new_spec,
        out_shape=jax.ShapeDtypeStruct(q.shape, BF16),
        scratch_shapes=[pltpu.VMEM((ts, w), F32), pltpu.VMEM((2 * n_pairs, ts, 1), F32),
                        pltpu.VMEM((tk, w), BF16), pltpu.VMEM((tk, w), BF16)],
        compiler_params=_params("parallel", "parallel"),
        name="sb_attention_sample",
    )(q, k_new, v_new, k_cache, v_cache)


def _band_core(q_heads, lo, kv, pens, bias, o_ref, n_pairs):
    heads = range(2 * n_pairs)
    blocks = range(len(kv))
    z = [[kv[i].scores(q_heads[h], h // 2) + bias(h, i) for i in blocks] for h in heads]
    acc = []
    for h in heads:
        m = None
        for i in blocks:
            mi = jnp.max(z[h][i], axis=-1, keepdims=True)
            if pens[i] is not None:
                mi = mi + pens[i]
            m = mi if m is None else jnp.maximum(m, mi)
        a = None
        for i in blocks:
            shift = m if pens[i] is None else m - pens[i]
            p = jnp.exp2(z[h][i] - shift).astype(BF16)
            v = kv[i].values(h // 2)
            ones = jnp.ones_like(v)
            first = kv[i].head_lanes()
            v = jnp.where(first, v, ones) if h % 2 == 0 else jnp.where(first, ones, v)
            pv = kv[i].weighted(p, v)
            a = pv if a is None else a + pv
        acc.append(a)
    for p in range(n_pairs):
        a0, a1 = acc[2 * p], acc[2 * p + 1]
        o_ref[:, _pair_cols(p)] = jnp.where(lo, a0 / pltpu.roll(a0, HEAD_DIM, axis=1),
                                            a1 / pltpu.roll(a1, HEAD_DIM, axis=1)).astype(o_ref.dtype)


def _band_prompt_kernel(q_ref, k_ref, v_ref, bias_ref, o_ref, *, tq, tk, n_pairs, blocks_per_step):
    def one_block(s, carry):
        qi = pl.program_id(2) * blocks_per_step + s
        rows = pl.ds(pl.multiple_of(s * tq, tq), tq)
        q_heads, lo = _split_heads(q_ref.at[rows, :], n_pairs, tq)
        kv, pens = [], []
        for dj in range(3):
            j = qi - dj
            pens.append(None if dj == 0 else jnp.where(j >= 0, 0.0, NEG_BIG).astype(F32))
            keys = pl.ds(pl.multiple_of(jnp.maximum(j, 0) * tk, tk), tk)
            kv.append(_KV(k_ref[:, keys], v_ref[:, keys], True))
        _band_core(q_heads, lo, kv, pens, lambda h, i: bias_ref[h, i], o_ref.at[rows, :], n_pairs)
        return carry

    lax.fori_loop(0, blocks_per_step, one_block, 0)


def _band_sample_kernel(q_ref, kn_ref, vn_ref, kc_ref, vc_ref, bias_ref, o_ref, kpad_ref, vpad_ref,
                        *, ts, tk, n_cache_blocks, n_pairs):
    q_heads, lo = _split_heads(q_ref, n_pairs, ts)
    kpad_ref[...] = jnp.zeros_like(kpad_ref)
    vpad_ref[...] = jnp.zeros_like(vpad_ref)
    kpad_ref[:ts, :] = kn_ref[...]
    vpad_ref[:ts, :] = vn_ref[...]
    kv = [_KV(kpad_ref[...], vpad_ref[...], False)]
    for dj in range(1, n_cache_blocks + 1):
        keys = slice((n_cache_blocks - dj) * tk, (n_cache_blocks - dj + 1) * tk)
        kv.append(_KV(kc_ref[:, keys].astype(BF16), vc_ref[:, keys].astype(BF16), True))
    is_new_key = lax.broadcasted_iota(jnp.int32, (ts, tk), 1) < ts

    def bias(h, i):
        return jnp.where(is_new_key, bias_ref[h, 0], NEG_BIG) if i == 0 else bias_ref[h, i]

    _band_core(q_heads, lo, kv, [None] * len(kv), bias, o_ref, n_pairs)


def _band_attention_prompt(q, k, v, bias):
    b, t, _ = q.shape
    tq = tk = ATT_BLOCK
    assert t % tk == 0
    n_pairs = BAND_PAIRS_PER_STEP
    w = n_pairs * LANES
    per_step = math.gcd(t // tq, BAND_BLOCKS_PER_STEP)
    kern = functools.partial(_band_prompt_kernel, tq=tq, tk=tk, n_pairs=n_pairs, blocks_per_step=per_step)
    kv_spec = pl.BlockSpec((None, w, t), lambda bi, hp, qi: (bi, hp, 0))
    q_spec = pl.BlockSpec((None, per_step * tq, w), lambda bi, hp, qi: (bi, qi, hp))
    bias_spec = pl.BlockSpec((2 * n_pairs, 3, tq, tk), lambda bi, hp, qi: (hp, 0, 0, 0))
    return pl.pallas_call(
        kern,
        grid=(b, C_WIDTH // w, t // (per_step * tq)),
        in_specs=[q_spec, kv_spec, kv_spec, bias_spec],
        out_specs=q_spec,
        out_shape=jax.ShapeDtypeStruct(q.shape, BF16),
        compiler_params=_params("parallel", "parallel", "parallel"),
        name="band_attention",
    )(q, k, v, bias)


def _band_attention_sample(q, k_new, v_new, k_cache, v_cache, layer, bias):
    b, ts, _ = q.shape
    wc = k_cache.shape[3]
    tk = ATT_BLOCK
    n_cache_blocks = min(wc // tk, 2)
    assert wc % (n_cache_blocks * tk) == 0 and ts <= tk
    n_pairs = BAND_PAIRS_PER_STEP
    w = n_pairs * LANES
    kern = functools.partial(_band_sample_kernel, ts=ts, tk=tk, n_cache_blocks=n_cache_blocks,
                             n_pairs=n_pairs)
    new_spec = pl.BlockSpec((None, ts, w), lambda bi, hp: (bi, 0, hp))
    cache_rows = n_cache_blocks * tk
    cache_spec = pl.BlockSpec((None, None, w, cache_rows),
                              lambda bi, hp: (layer, bi, hp, wc // cache_rows - 1))
    bias_spec = pl.BlockSpec((2 * n_pairs, 1 + n_cache_blocks, ts, tk), lambda bi, hp: (hp, 0, 0, 0))
    return pl.pallas_call(
        kern,
        grid=(b, C_WIDTH // w),
        in_specs=[new_spec, new_spec, new_spec, cache_spec, cache_spec, bias_spec],
        out_specs=new_spec,
        out_shape=jax.ShapeDtypeStruct(q.shape, BF16),
        scratch_shapes=[pltpu.VMEM((tk, w), BF16), pltpu.VMEM((tk, w), BF16)],
        compiler_params=_params("parallel", "parallel"),
        name="band_attention_sample",
    )(q, k_new, v_new, k_cache, v_cache, bias)


def _band_bias_kernel(g_ref, o_ref):
    rows = tk = ATT_BLOCK
    q_chunk = lax.broadcasted_iota(jnp.int32, (rows, tk), 0) // CHUNK
    k_chunk = lax.broadcasted_iota(jnp.int32, (rows, tk), 1) // CHUNK
    for dj in range(3):
        g = jnp.broadcast_to(g_ref[0, dj], (rows, 2 * tk))
        tile = pltpu.roll(g, 0, axis=1, stride=1, stride_axis=0)[:, :tk]
        diff = dj * (tk // CHUNK) + q_chunk - k_chunk
        seen = (diff >= 0) & (diff <= C_LEFT_CHUNKS)
        o_ref[0, dj] = jnp.where(seen, tile * LOG2E, NEG_BIG)


def _band_bias(rel_table):
    rows = tk = ATT_BLOCK
    c = jnp.arange(2 * tk, dtype=jnp.int32)
    u = jnp.where(c <= tk, -c, 2 * tk - c)
    idx = jnp.clip(jnp.arange(3, dtype=jnp.int32)[:, None] * tk + u[None, :], REL_MIN, REL_MAX) - REL_MIN
    g = rel_table[:, idx].astype(F32).reshape(C_HEADS, 3, 1, 2 * tk)
    return pl.pallas_call(
        _band_bias_kernel,
        grid=(C_HEADS,),
        in_specs=[pl.BlockSpec((1, 3, 1, 2 * tk), lambda h: (h, 0, 0, 0))],
        out_specs=pl.BlockSpec((1, 3, rows, tk), lambda h: (h, 0, 0, 0)),
        out_shape=jax.ShapeDtypeStruct((C_HEADS, 3, rows, tk), F32),
        compiler_params=_params("parallel"),
        name="band_bias",
    )(g)


def _gla_pair_tile(q, k, v, la, states, emit):
    L = GLA_CHUNK
    chunks = range(q.shape[0] // L)
    row = lax.broadcasted_iota(jnp.int32, (L, L), 0)
    colm = lax.broadcasted_iota(jnp.int32, (L, L), 1)
    tri = (colm <= row).astype(BF16)
    causal = colm <= row
    lane = lax.broadcasted_iota(jnp.int32, (L, LANES), 1)
    sub = lax.broadcasted_iota(jnp.int32, (LANES, B_DV), 0)
    mine = [(lane >= h * B_DK) & (lane < (h + 1) * B_DK) for h in range(2)]
    mine_rows = [(sub >= h * B_DK) & (sub < (h + 1) * B_DK) for h in range(2)]
    rows = [slice(c * L, (c + 1) * L) for c in chunks]

    b = []
    for r in rows:
        g_hi, g_lo = _split_bf16(la[r, :])
        b.append(_dot(tri, g_hi) + _dot(tri, g_lo))
    yield
    qg, qg_h, kg, kd_t, decay = [], [], [], [], []
    for c, r in zip(chunks, rows):
        qg_c = q[r, :] * (B_DK ** -0.5) * jnp.exp(b[c])
        qg.append(qg_c.astype(BF16))
        qg_h.append([jnp.where(mine[h], qg_c, 0.0).astype(BF16) for h in range(2)])
        kg.append((k[r, :] * jnp.exp(-b[c])).astype(BF16))
        b_t = b[c].T
        b_last = b_t[:, L - 1:L]
        kd_t.append((k[r, :].T * jnp.exp(b_last - b_t)).astype(BF16))
        decay.append(jnp.exp(b_last))
    yield
    att =[[jnp.where(causal, _dot_nt(qg_h[c][h], kg[c]), 0.0).astype(BF16) for h in range(2)]
           for c in chunks]
    yield
    o_intra, own = [], []
    for c, r in zip(chunks, rows):
        v_h = [v[r, h * B_DV:(h + 1) * B_DV].astype(BF16) for h in range(2)]
        o_intra.append([_dot(att[c][h], v_h[h]) for h in range(2)])
        own.append([jnp.where(mine_rows[h], _dot(kd_t[c], v_h[h]), 0.0) for h in range(2)])

    yield
    states = list(states)
    start = []
    for c in chunks:
        start.append([s.astype(BF16) for s in states])
        states = [decay[c] * states[h] + own[c][h] for h in range(2)]
    yield
    for c in chunks:
        for h in range(2):
            emit(c, h, o_intra[c][h] + _dot(qg[c], start[c][h]))
    return states


def _pair_states(s0_ref, p):
    zeros_state = jnp.zeros((B_DK, B_DV), F32)
    return [jnp.concatenate([s0_ref[2 * p], zeros_state], axis=0),
            jnp.concatenate([zeros_state, s0_ref[2 * p + 1]], axis=0)]


def _pair_v_cols(p):
    return slice(2 * p * B_DV, 2 * (p + 1) * B_DV)


def _gla_rows_emit(o_ref, p):
    def emit(c, h, o):
        o_ref[c * GLA_CHUNK:(c + 1) * GLA_CHUNK, (2 * p + h) * B_DV:(2 * p + h + 1) * B_DV] = o
    return emit


def _gla_kernel(q_ref, k_ref, v_ref, la_ref, s0_ref, o_ref, s_out_ref):
    pairs = range(B_HEADS // 2)
    states = _interleave([
        _gla_pair_tile(q_ref[:, _pair_cols(p)], k_ref[:, _pair_cols(p)], v_ref[:, _pair_v_cols(p)],
                       la_ref[:, _pair_cols(p)], _pair_states(s0_ref, p), _gla_rows_emit(o_ref, p))
        for p in pairs])
    for p in pairs:
        s_out_ref[2 * p] = states[p][0][:B_DK, :]
        s_out_ref[2 * p + 1] = states[p][1][B_DK:, :]


def _gla(q, k, v, la, s0):
    b, t, _ = q.shape
    assert t % GLA_CHUNK == 0 and t <= GLA_MAX_TILE
    qk_spec = pl.BlockSpec((None, t, B_KW), lambda bi: (bi, 0, 0))
    v_spec = pl.BlockSpec((None, t, B_VW), lambda bi: (bi, 0, 0))
    s_spec = pl.BlockSpec((None, B_HEADS, B_DK, B_DV), lambda bi: (bi, 0, 0, 0))
    return pl.pallas_call(
        _gla_kernel,
        grid=(b,),
        in_specs=[qk_spec, qk_spec, v_spec, qk_spec, s_spec],
        out_specs=[v_spec, s_spec],
        out_shape=[jax.ShapeDtypeStruct(v.shape, F32), jax.ShapeDtypeStruct(s0.shape, F32)],
        compiler_params=_params("parallel"),
        name="gla",
    )(q, k, v, la, s0)


def _pad_rows(x, n):
    return jnp.pad(x, ((0, 0), (0, n - x.shape[1]), (0, 0)))


def _heads_last(x, heads):
    n, b, _, s = x.shape
    return jnp.transpose(x.reshape(n, b, heads, HEAD_DIM, s), (0, 1, 4, 2, 3))


def _feature_major(cache):
    n, b, s, heads, hd = cache.shape
    return jnp.transpose(cache, (0, 1, 3, 4, 2)).reshape(n, b, heads * hd, s)


def _row_tile(m):
    for tm in (512, 256, 128, 64, 32, 16, 8):
        if m % tm == 0:
            return tm
    raise ValueError(f"token count {m} is not a multiple of 8")


def kernel(x_prompt, x_sample, cache_a_k, cache_a_v, state_b, cache_c_k, cache_c_v, norm_mix_g, norm_ffn_g, w_in_ab, w_gate_b, b_gate_b, norm_gla_g, w_out_ab, w_qkv_c, rel_bias_c, w_out_c, w_ffn_gate, w_ffn_up, w_ffn_down, norm_final_g):
    bp, tp, d = x_prompt.shape
    bs, ts, _ = x_sample.shape
    depth = norm_mix_g.shape[0]
    past = cache_a_k.shape[2]
    wc = cache_c_k.shape[2]
    assert tp % ATT_BLOCK == 0 and past % ATT_BLOCK == 0 and wc % ATT_BLOCK == 0
    assert ts <= GLA_CHUNK and ts % 8 == 0
    mp, ms = bp * tp, bs * ts
    tmp, tms = _row_tile(tp), _row_tile(ms)
    xp = x_prompt.reshape(mp, d)
    xs = x_sample.reshape(ms, d)
    row2 = lambda a: a.reshape(1, -1)

    a_ks, a_vs, b_sp, b_ss, c_ks, c_vs = [], [], [], [], [], []
    a_kv_prompt = c_kv_prompt = None
    n_ab, n_c = (depth + 1) // 2, depth // 2
    keep = min(C_LEFT_CHUNKS * CHUNK, tp)

    kv0, kv1 = A_WIDTH, 3 * A_WIDTH
    o = 3 * A_WIDTH + 2 * B_KW + B_VW
    w_in_t = jnp.swapaxes(w_in_ab, 1, 2)
    w_main_all = jnp.concatenate([w_in_t[:, :kv0], w_in_t[:, kv1:o], w_in_t[:, o + B_GATE_RANK:]],
                                 axis=1).astype(BF16)
    w_kv_ab_t_all = w_in_t[:, kv0:kv1].astype(BF16)
    w_lr_all = jnp.pad(w_in_t[:, o:o + B_GATE_RANK],
                       ((0, 0), (0, LANES - B_GATE_RANK), (0, 0))).astype(BF16)
    w_gate_all = jnp.pad(w_gate_b, ((0, 0), (0, LANES - B_GATE_RANK), (0, 0))).astype(BF16)
    w_out_ab_all = w_out_ab.astype(BF16)
    w_q_all = w_qkv_c[:, :, :C_WIDTH].astype(BF16)
    w_kv_c_t_all = jnp.swapaxes(w_qkv_c[:, :, C_WIDTH:], 1, 2).astype(BF16)
    w_out_c_all = w_out_c.astype(BF16)
    cache_a_k_fm, cache_a_v_fm, cache_c_k_fm, cache_c_v_fm = (
        _feature_major(c) for c in (cache_a_k, cache_a_v, cache_c_k, cache_c_v))

    for layer in range(depth):
        i = layer // 2
        g_mix = row2(norm_mix_g[layer])
        riders = _Riders((w_ffn_gate, w_ffn_up, w_ffn_down), layer, mp // tmp)
        ffn_ends = (row2(norm_ffn_g[layer]), row2(norm_final_g) if layer == depth - 1 else None)
        if layer % 2 == 0:
            w_main, w_kv_t, w_lr, w_gate, w_out = (
                _Slab(w, i) for w in (w_main_all, w_kv_ab_t_all, w_lr_all, w_gate_all, w_out_ab_all))
            b_gate = row2(b_gate_b[i])
            g_gla = row2(norm_gla_g[i])

            qa, ka, va, kab, vab, qb, kb, vb, r, la, *ffn_w = _proj_ab(
                xp, g_mix, w_main, w_kv_t, w_lr, w_gate, b_gate, tmp, batch=bp,
                stack=_KVStack(i, n_ab, tp, a_kv_prompt), riders=riders)
            ffn = (ffn_ends[0], *ffn_w, ffn_ends[1])
            a_kv_prompt = (ka, va)
            sh = lambda a: a.reshape(bp, tp, -1)
            oa, ob, sbp = _mixer_ab_prompt(sh(qa), kab, vab, sh(qb), sh(kb), sh(vb), sh(la))
            mixed_p = (oa.reshape(mp, -1), ob.reshape(mp, -1), r)
            b_sp.append(sbp)

            qa, ka, va, kab, vab, qb, kb, vb, r, la = _proj_ab(
                xs, g_mix, w_main, w_kv_t, w_lr, w_gate, b_gate, tms)
            sh = lambda a: a.reshape(bs, ts, -1)
            oa = _sb_attention_sample(sh(qa), sh(kab), sh(vab), cache_a_k_fm, cache_a_v_fm, i)
            pad_t = lambda a: _pad_rows(sh(a), GLA_CHUNK)
            ob, sbs = _gla(pad_t(qb), pad_t(kb), pad_t(vb), pad_t(la), state_b[i])
            mixed_s = (oa.reshape(ms, -1), ob[:, :ts].reshape(ms, -1), r)
            xp, xs = _layer_tail(xp, mixed_p, xs, mixed_s, (g_gla, w_out), *ffn, tmp)
            a_ks.append(ka.reshape(bs, ts, A_HEADS, HEAD_DIM))
            a_vs.append(va.reshape(bs, ts, A_HEADS, HEAD_DIM))
            b_ss.append(sbs)
        else:
            w_q, w_kv_t, w_out = (_Slab(w, i) for w in (w_q_all, w_kv_c_t_all, w_out_c_all))

            q, k, v, kb16, vb16, *ffn_w = _proj_c(xp, g_mix, w_q, w_kv_t, tmp, batch=bp,
                                                  stack=_KVStack(i, n_c, keep, c_kv_prompt), riders=riders)
            ffn = (ffn_ends[0], *ffn_w, ffn_ends[1])
            c_kv_prompt = (k, v)
            bias = _band_bias(rel_bias_c[i])
            oc = _band_attention_prompt(q.reshape(bp, tp, -1), kb16, vb16, bias)
            mixed_p = (oc.reshape(mp, -1),)

            q, k, v, kb16, vb16 = _proj_c(xs, g_mix, w_q, w_kv_t, tms)
            sh = lambda a: a.reshape(bs, ts, -1)
            oc = _band_attention_sample(sh(q), sh(kb16), sh(vb16), cache_c_k_fm, cache_c_v_fm, i, bias)
            xp, xs = _layer_tail(xp, mixed_p, xs, (oc.reshape(ms, -1),), (w_out,), *ffn, tmp)
            c_ks.append(k.reshape(bs, ts, C_HEADS, HEAD_DIM))
            c_vs.append(v.reshape(bs, ts, C_HEADS, HEAD_DIM))

    y_prompt = xp.reshape(bp, tp, d)
    y_sample = xs.reshape(bs, ts, d)
    a_kp, a_vp = (_heads_last(a, A_HEADS) for a in a_kv_prompt)
    c_kp, c_vp = (_heads_last(a, C_HEADS) for a in c_kv_prompt)
    return (y_prompt, y_sample, a_kp, a_vp, jnp.stack(a_ks), jnp.stack(a_vs),
            jnp.stack(b_sp), jnp.stack(b_ss), c_kp, c_vp, jnp.stack(c_ks), jnp.stack(c_vs))
```

```python
import functools
import math

import jax
import jax.numpy as jnp
from jax import lax
from jax.experimental import pallas as pl
from jax.experimental.pallas import tpu as pltpu

F32 = jnp.float32
BF16 = jnp.bfloat16

EPS = 1e-6
HEAD_DIM = 64
LANES = 128
A_HEADS = 8
A_WIDTH = A_HEADS * HEAD_DIM
B_HEADS = 4
B_DK = 64
B_DV = 128
B_KW = B_HEADS * B_DK
B_VW = B_HEADS * B_DV
B_GATE_RANK = 16
B_GATE_TEMP = 16.0
GLA_CHUNK = 64
C_HEADS = 16
C_WIDTH = C_HEADS * HEAD_DIM
CHUNK = 64
C_LEFT_CHUNKS = 8
REL_MIN = -(CHUNK - 1)
REL_MAX = 128
ATT_BLOCK = 256
NEG_BIG = -1e30
LOG2E = 1.4426950408889634
SB_DEAD = 104.0
SB_PAIRS_PER_STEP = 4
BAND_PAIRS_PER_STEP = 4
MIXER_BLOCKS_PER_STEP = 4
BAND_BLOCKS_PER_STEP = 8
GLA_MAX_TILE = 512
VMEM_LIMIT = 56 * 1024 * 1024


def _params(*sem):
    return pltpu.CompilerParams(dimension_semantics=sem, vmem_limit_bytes=VMEM_LIMIT)


class _Slab:
    def __init__(self, stacked, index):
        self.stacked, self.index, self.shape = stacked, index, stacked.shape[1:]


def _operand(a):
    return a.stacked if isinstance(a, _Slab) else a


def _resident(a):
    if isinstance(a, _Slab):
        index = (a.index,) + (0,) * len(a.shape)
        return pl.BlockSpec((None, *a.shape), lambda *_: index, pipeline_mode=pl.Buffered(1))
    return pl.BlockSpec(a.shape, lambda *_: (0,) * a.ndim, pipeline_mode=pl.Buffered(1))


def _rms(x, g):
    return x * lax.rsqrt(jnp.mean(x * x, axis=-1, keepdims=True) + EPS) * g


def _log_sigmoid_pair(z):
    l = jnp.log1p(jnp.exp(-jnp.abs(z)))
    return jnp.minimum(z, 0.0) - l, jnp.minimum(-z, 0.0) - l


def _split_bf16(x):
    hi = x.astype(BF16)
    lo = (x - hi.astype(F32)).astype(BF16)
    return hi, lo


def _dot(a, b):
    return jnp.dot(a, b, preferred_element_type=F32)


def _dot_nt(a, b):
    return lax.dot_general(a, b, (((1,), (1,)), ((), ())), preferred_element_type=F32)


def _dot_tn(a, b):
    return lax.dot_general(a, b, (((0,), (0,)), ((), ())), preferred_element_type=F32)


def _pair_cols(p):
    return slice(p * LANES, (p + 1) * LANES)


class _KV:
    def __init__(self, k, v, feature_major):
        self.k, self.v, self.feature_major = k, v, feature_major
        self.n_keys = k.shape[1] if feature_major else k.shape[0]

    def scores(self, q_h, p):
        if self.feature_major:
            return _dot(q_h, self.k[_pair_cols(p), :])
        return _dot_nt(q_h, self.k[:, _pair_cols(p)])

    def first(self, n):
        if self.feature_major:
            return _KV(self.k[:, :n], self.v[:, :n], True)
        return _KV(self.k[:n], self.v[:n], False)

    def values(self, p):
        return self.v[_pair_cols(p), :] if self.feature_major else self.v[:, _pair_cols(p)]

    def weighted(self, w, v_p):
        return _dot_nt(w, v_p) if self.feature_major else _dot(w, v_p)

    def head_lanes(self):
        shape = (LANES, self.n_keys) if self.feature_major else (self.n_keys, LANES)
        return lax.broadcasted_iota(jnp.int32, shape, 0 if self.feature_major else 1) < HEAD_DIM


def _split_heads(q_ref, n_pairs, tq):
    lo = lax.broadcasted_iota(jnp.int32, (tq, LANES), 1) < HEAD_DIM
    heads = []
    for p in range(n_pairs):
        q = q_ref[:, _pair_cols(p)]
        heads += [jnp.where(lo, q, jnp.zeros_like(q)), jnp.where(lo, jnp.zeros_like(q), q)]
    return heads, lo


def _emit_kv(y, wkv_ref, k_ref, v_ref, kb_ref, vb_ref, feature_major, kept_tail=None, own_slot=0):
    if not feature_major:
        kv = _dot_nt(y, wkv_ref[...])
        width = kv.shape[1] // 2
        k, v = kv[:, :width], kv[:, width:]
        k_ref[...], v_ref[...], kb_ref[...], vb_ref[...] = k, v, k.astype(BF16), v.astype(BF16)
        return
    kv = _dot_nt(wkv_ref[...], y)
    width = kv.shape[0] // 2
    k, v = kv[:width, :], kv[width:, :]
    kb_ref[...] = k.astype(BF16)
    vb_ref[...] = v.astype(BF16)
    slot = own_slot
    other_slots = [s for s in range(k_ref.shape[0]) if s != slot]
    if kept_tail is None:
        fill, fill_cols = slice(None), k_ref.shape[2]
        k_ref[slot] = k
        v_ref[slot] = v
    else:
        n_tiles, n_kept = kept_tail
        tile = pl.program_id(0) % n_tiles
        tm = k.shape[1]
        cols = pl.ds(pl.multiple_of(jnp.maximum(tile - (n_tiles - n_kept), 0) * tm, tm), tm)
        k_ref[slot, :, cols] = k
        v_ref[slot, :, cols] = v
        share = k_ref.shape[2] // n_tiles
        if share % LANES == 0 and share * n_tiles == k_ref.shape[2]:
            fill, fill_cols = pl.ds(pl.multiple_of(tile * share, share), share), share
        else:
            fill, fill_cols = slice(None), k_ref.shape[2]
    for dst in (k_ref, v_ref):
        for s in other_slots:
            dst[s, :, fill] = jnp.zeros((dst.shape[1], fill_cols), dst.dtype)


class _Riders:
    def __init__(self, stacks, layer, steps):
        self.stacks, self.layer = list(stacks), layer
        self.chunks = next(c for c in (16, 8, 4, 2, 1)
                           if c <= steps and all(w.shape[1] % (16 * c) == 0 for w in self.stacks))

    def specs(self):
        layer, last = self.layer, self.chunks - 1
        ins = [pl.BlockSpec((None, w.shape[1] // self.chunks, w.shape[2]),
                            lambda i: (layer, jnp.minimum(i, last), 0)) for w in self.stacks]
        outs = [pl.BlockSpec((w.shape[1] // self.chunks, w.shape[2]),
                             lambda i: (jnp.minimum(i, last), 0)) for w in self.stacks]
        shapes = [jax.ShapeDtypeStruct(w.shape[1:], BF16) for w in self.stacks]
        return ins, outs, shapes


def _split_extra_refs(refs, n_aliased, n_riders):
    rider_in = refs[n_aliased:n_aliased + n_riders]
    outs = refs[n_aliased + n_riders:len(refs) - n_riders]
    return outs, list(zip(rider_in, refs[len(refs) - n_riders:]))


def _cast_riders(pairs):
    for src, dst in pairs:
        dst[...] = src[...].astype(BF16)


def _proj_ab_kernel(x_ref, g_ref, w_ref, wkv_ref, wlr_ref, wgate_ref, bgate_ref, *refs,
                    feature_major, n_aliased, n_riders, own_slot):
    outs, riders = _split_extra_refs(refs, n_aliased, n_riders)
    qa_ref, ka_ref, va_ref, kab_ref, vab_ref, qb_ref, kb_ref, vb_ref, r_ref, la_ref = outs
    _cast_riders(riders)
    y = _rms(x_ref[...], g_ref[...]).astype(BF16)
    _emit_kv(y, wkv_ref, ka_ref, va_ref, kab_ref, vab_ref, feature_major, own_slot=own_slot)
    z = _dot_nt(y, w_ref[...])
    c = 0
    qa_ref[...] = (z[:, c:c + A_WIDTH] * (HEAD_DIM ** -0.5)).astype(BF16); c += A_WIDTH
    qb_ref[...] = z[:, c:c + B_KW]; c += B_KW
    kb_ref[...] = z[:, c:c + B_KW]; c += B_KW
    vb_ref[...] = z[:, c:c + B_VW]; c += B_VW
    r_ref[...] = z[:, c:c + B_VW]
    g_lr = _dot_nt(y, wlr_ref[...])
    gate = _dot(g_lr.astype(BF16), wgate_ref[...]) + bgate_ref[...]
    la_ref[...] = _log_sigmoid_pair(gate)[0] * (1.0 / B_GATE_TEMP)


class _KVStack:
    def __init__(self, layer, n_layers, keep, previous=None):
        self.layer, self.n_layers, self.keep, self.previous = layer, n_layers, keep, previous


def _kv_out(m, width, tm, batch, stack):
    if batch is None:
        spec = pl.BlockSpec((tm, width), lambda i: (i, 0))
        return [spec] * 4, [jax.ShapeDtypeStruct((m, width), dt) for dt in (F32, F32, BF16, BF16)], None, 0
    t = m // batch
    assert t % tm == 0 and stack.keep % tm == 0 and stack.keep <= t
    n_tiles = t // tm
    copy_spec = pl.BlockSpec((None, width, tm), lambda i: (i // n_tiles, 0, i % n_tiles))
    copy_shape = jax.ShapeDtypeStruct((batch, width, t), BF16)
    slots, first_slot, own_slot = ((stack.n_layers, 0, stack.layer) if stack.previous is None
                                   else (1, stack.layer, 0))
    if stack.keep == t:
        f32_spec = pl.BlockSpec((slots, None, width, tm), lambda i: (first_slot, i // n_tiles, 0, i % n_tiles))
        kept_tail = None
    else:
        f32_spec = pl.BlockSpec((slots, None, width, stack.keep), lambda i: (first_slot, i // n_tiles, 0, 0))
        kept_tail = (n_tiles, stack.keep // tm)
    f32_shape = jax.ShapeDtypeStruct((stack.n_layers, batch, width, stack.keep), F32)
    return ([f32_spec, f32_spec, copy_spec, copy_spec], [f32_shape, f32_shape, copy_shape, copy_shape],
            kept_tail, own_slot)


def _aliased_stack(stack, n_inputs, first_output):
    if stack is None or stack.previous is None:
        return [], [], {}
    prev = list(stack.previous)
    specs = [pl.BlockSpec(memory_space=pl.ANY)] * len(prev)
    return prev, specs, {n_inputs + j: first_output + j for j in range(len(prev))}


def _rider_plumbing(riders):
    if riders is None:
        return [], [], [], []
    return (riders.stacks, *riders.specs())


def _proj_ab(x, g, w_main, w_kv, w_lr, w_gate, b_gate, tm, batch=None, stack=None, riders=None):
    m, d = x.shape
    row = lambda n: pl.BlockSpec((tm, n), lambda i: (i, 0))
    params = (g, w_main, w_kv, w_lr, w_gate, b_gate)
    kv_specs, kv_shapes, kept_tail, own_slot = _kv_out(m, A_WIDTH, tm, batch, stack)
    assert kept_tail is None
    prev, prev_specs, aliases = _aliased_stack(stack, 1 + len(params), 1)
    ride_ops, ride_in, ride_out, ride_shapes = _rider_plumbing(riders)
    rest = [(B_KW, F32), (B_KW, F32), (B_VW, F32), (B_VW, F32), (B_KW, F32)]
    return pl.pallas_call(
        functools.partial(_proj_ab_kernel, feature_major=batch is not None, n_aliased=len(prev),
                          n_riders=len(ride_ops), own_slot=own_slot),
        grid=(m // tm,),
        in_specs=[row(d)] + [_resident(a) for a in params] + prev_specs + ride_in,
        out_specs=[row(A_WIDTH)] + kv_specs + [row(n) for n, _ in rest] + ride_out,
        out_shape=([jax.ShapeDtypeStruct((m, A_WIDTH), BF16)] + kv_shapes
                   + [jax.ShapeDtypeStruct((m, n), dt) for n, dt in rest] + ride_shapes),
        input_output_aliases=aliases,
        compiler_params=_params("arbitrary" if ride_ops else "parallel"),
        name="proj_ab",
    )(x, *[_operand(a) for a in params], *prev, *ride_ops)


def _proj_c_kernel(x_ref, g_ref, wq_ref, wkv_ref, *refs, feature_major, n_aliased, kept_tail,
                   n_riders, own_slot):
    outs, riders = _split_extra_refs(refs, n_aliased, n_riders)
    q_ref, k_ref, v_ref, kb_ref, vb_ref = outs
    _cast_riders(riders)
    y = _rms(x_ref[...], g_ref[...]).astype(BF16)
    _emit_kv(y, wkv_ref, k_ref, v_ref, kb_ref, vb_ref, feature_major, kept_tail, own_slot)
    q_ref[...] = (_dot(y, wq_ref[...]) * (HEAD_DIM ** -0.5 * LOG2E)).astype(BF16)


def _proj_c(x, g, w_q, w_kv, tm, batch=None, stack=None, riders=None):
    m, d = x.shape
    row = lambda n: pl.BlockSpec((tm, n), lambda i: (i, 0))
    params = (g, w_q, w_kv)
    kv_specs, kv_shapes, kept_tail, own_slot = _kv_out(m, C_WIDTH, tm, batch, stack)
    prev, prev_specs, aliases = _aliased_stack(stack, 1 + len(params), 1)
    ride_ops, ride_in, ride_out, ride_shapes = _rider_plumbing(riders)
    return pl.pallas_call(
        functools.partial(_proj_c_kernel, feature_major=batch is not None, n_aliased=len(prev),
                          kept_tail=kept_tail, n_riders=len(ride_ops), own_slot=own_slot),
        grid=(m // tm,),
        in_specs=[row(d)] + [_resident(a) for a in params] + prev_specs + ride_in,
        out_specs=[row(C_WIDTH)] + kv_specs + ride_out,
        out_shape=[jax.ShapeDtypeStruct((m, C_WIDTH), BF16)] + kv_shapes + ride_shapes,
        input_output_aliases=aliases,
        compiler_params=_params("arbitrary" if (kept_tail or ride_ops) else "parallel"),
        name="proj_c",
    )(x, *[_operand(a) for a in params], *prev, *ride_ops)


def _layer_tail_rows(x_ref, mixer_refs, param_refs, o_ref, gla_merge, final_norm):
    if gla_merge:
        oa_ref, ob_ref, r_ref = mixer_refs
        ggla_ref, wout_ref, *param_refs = param_refs
        ob = ob_ref[...]
        parts = []
        for h in range(B_HEADS):
            seg = ob[:, h * B_DV:(h + 1) * B_DV]
            parts.append(seg * lax.rsqrt(jnp.mean(seg * seg, axis=-1, keepdims=True) + EPS))
        r = r_ref[...]
        obn = jnp.concatenate(parts, axis=-1) * ggla_ref[...] * (r * jax.nn.sigmoid(r))
        mix = _dot(oa_ref[...], wout_ref[:A_WIDTH, :]) + _dot(obn.astype(BF16), wout_ref[A_WIDTH:, :])
    else:
        (oc_ref,) = mixer_refs
        wout_ref, *param_refs = param_refs
        mix = _dot(oc_ref[...], wout_ref[...])
    gffn_ref, wg_ref, wu_ref, wd_ref, *param_refs = param_refs
    x = x_ref[...] + mix
    y = _rms(x, gffn_ref[...]).astype(BF16)
    h = _dot(y, wg_ref[...])
    u = _dot(y, wu_ref[...])
    a = (h * jax.nn.sigmoid(h) * u).astype(BF16)
    x = x + _dot(a, wd_ref[...])
    o_ref[...] = _rms(x, param_refs[0][...]) if final_norm else x


def _layer_tail_kernel(*refs, n_mixer, n_params, gla_merge, final_norm, main_steps):
    main, extra = refs[:1 + n_mixer], refs[1 + n_mixer:2 + 2 * n_mixer]
    params = refs[2 + 2 * n_mixer:2 + 2 * n_mixer + n_params]
    o_ref, o_extra_ref = refs[-2:]

    @pl.when(pl.program_id(0) < main_steps)
    def _():
        _layer_tail_rows(main[0], main[1:], params, o_ref, gla_merge, final_norm)

    @pl.when(pl.program_id(0) == main_steps)
    def _():
        _layer_tail_rows(extra[0], extra[1:], params, o_extra_ref, gla_merge, final_norm)


def _layer_tail(x, mixer_out, x_extra, mixer_out_extra, mixer_params, g_ffn, wg, wu, wd, g_fin, tm):
    m, d = x.shape
    main_steps = m // tm
    row = lambda a: pl.BlockSpec((tm, a.shape[1]), lambda i: (jnp.minimum(i, main_steps - 1), 0))
    whole = lambda a: pl.BlockSpec(a.shape, lambda i: (0, 0))
    params = [*mixer_params, g_ffn, wg, wu, wd] + ([] if g_fin is None else [g_fin])
    return pl.pallas_call(
        functools.partial(_layer_tail_kernel, n_mixer=len(mixer_out), n_params=len(params),
                          gla_merge=len(mixer_out) == 3, final_norm=g_fin is not None, main_steps=main_steps),
        grid=(main_steps + 1,),
        in_specs=([row(x)] + [row(a) for a in mixer_out] + [whole(x_extra)]
                  + [whole(a) for a in mixer_out_extra] + [_resident(a) for a in params]),
        out_specs=[row(x), whole(x_extra)],
        out_shape=[jax.ShapeDtypeStruct((m, d), F32), jax.ShapeDtypeStruct(x_extra.shape, F32)],
        compiler_params=_params("arbitrary"),
        name="layer_tail",
    )(x, *mixer_out, x_extra, *mixer_out_extra, *[_operand(a) for a in params])


def _interleave(stage_generators):
    results = [None] * len(stage_generators)
    live = list(range(len(stage_generators)))
    while live:
        for i in list(live):
            try:
                next(stage_generators[i])
            except StopIteration as done:
                results[i] = done.value
                live.remove(i)
    return results


def _sb_core(q_heads, lo, first_kv, first_mask, earlier_kv, n_earlier, acc_ref, c_ref, o_ref, n_pairs,
             companions=(), on_companions=None):
    tk = first_kv.n_keys
    heads = range(2 * n_pairs)
    later = (lax.broadcasted_iota(jnp.int32, (tk, tk), 0)
             > lax.broadcasted_iota(jnp.int32, (tk, tk), 1)).astype(BF16)
    c_ref[...] = jnp.zeros_like(c_ref)

    tq = q_heads[0].shape[0]

    def row_groups(kv, mask):
        if mask is None or tq != tk or tq % (2 * LANES):
            return [(slice(None), kv, mask, later)]
        half = tq // 2
        return [(slice(0, half), kv.first(half), mask[:half, :half], later[:half, :half]),
                (slice(half, tq), kv, mask[half:, :], later)]

    def block(kv, mask):
        groups = row_groups(kv, mask)
        chains = [(h, g) for h in heads for g in range(len(groups))]
        z = [groups[g][1].scores(q_heads[h][groups[g][0], :], h // 2) for h, g in chains]
        yield
        log_beta, drop, after = [], [], []
        for i, (h, g) in enumerate(chains):
            d = jnp.maximum(z[i], 0.0) + jnp.log(1.0 + jnp.exp(-jnp.abs(z[i])))
            log_beta.append(z[i] - d)
            drop.append(d if mask is None else jnp.where(groups[g][2], d, 0.0))
        yield
        for i, (h, g) in enumerate(chains):
            hi, lo_part = _split_bf16(drop[i])
            tri = groups[g][3]
            after.append(_dot(hi, tri) + _dot(lo_part, tri))
        yield
        pv = {}
        for i, (h, g) in enumerate(chains):
            rows, kv_g = groups[g][0], groups[g][1]
            c = c_ref[h, rows, :]
            w = jnp.exp(log_beta[i] - after[i] - c)
            if mask is not None:
                w = jnp.where(groups[g][2], w, 0.0)
            c_ref[h, rows, :] = c + after[i][:, 0:1] + drop[i][:, 0:1]
            pv[h, g] = kv_g.weighted(w.astype(BF16), kv_g.values(h // 2))
        yield
        per_head = [pv[h, 0] if len(groups) == 1 else jnp.concatenate([pv[h, g] for g in range(len(groups))], 0)
                    for h in heads]
        out = [jnp.where(lo, per_head[2 * p], per_head[2 * p + 1]) for p in range(n_pairs)]
        return out[0] if n_pairs == 1 else jnp.concatenate(out, axis=-1)

    def all_dead():
        return jnp.min(c_ref[...]) > SB_DEAD

    first, *companion_results = _interleave([block(first_kv, first_mask), *companions])
    acc_ref[...] = first
    if companions:
        on_companions(companion_results)

    def cond(carry):
        n, dead = carry
        return (n < n_earlier) & jnp.logical_not(dead)

    def body(carry):
        n, _ = carry
        acc_ref[...] += _interleave([block(earlier_kv(n), None)])[0]
        return n + 1, all_dead()

    lax.while_loop(cond, body, (jnp.int32(0), jnp.asarray(False)))
    o_ref[...] = acc_ref[...].astype(o_ref.dtype)


def _mixer_ab_prompt_kernel(q_ref, k_ref, v_ref, qb_ref, kb_ref, vb_ref, la_ref, o_ref, ob_ref, s_out_ref,
                            acc_ref, c_ref, carry_ref, *, tb, n_pairs, blocks_per_step):
    pairs = range(B_HEADS // 2)
    strictly_earlier = (lax.broadcasted_iota(jnp.int32, (tb, tb), 1)
                        < lax.broadcasted_iota(jnp.int32, (tb, tb), 0))

    def kv_block(j):
        keys = pl.ds(pl.multiple_of(j * tb, tb), tb)
        return _KV(k_ref[:, keys], v_ref[:, keys], True)

    def keep_states(states):
        for p in pairs:
            carry_ref[2 * p], carry_ref[2 * p + 1] = states[p]
            s_out_ref[2 * p] = states[p][0][:B_DK, :]
            s_out_ref[2 * p + 1] = states[p][1][B_DK:, :]

    @pl.when(pl.program_id(1) == 0)
    def _():
        carry_ref[...] = jnp.zeros_like(carry_ref)

    def one_block(s, carry):
        qi = pl.program_id(1) * blocks_per_step + s
        rows = pl.ds(pl.multiple_of(s * tb, tb), tb)
        gla = [_gla_pair_tile(qb_ref[rows, _pair_cols(p)], kb_ref[rows, _pair_cols(p)],
                              vb_ref[rows, _pair_v_cols(p)], la_ref[rows, _pair_cols(p)],
                              [carry_ref[2 * p], carry_ref[2 * p + 1]],
                              _gla_rows_emit(ob_ref.at[rows, :], p)) for p in pairs]
        q_heads, lo = _split_heads(q_ref.at[rows, :], n_pairs, tb)
        _sb_core(q_heads, lo, kv_block(qi), strictly_earlier, lambda n: kv_block(qi - 1 - n), qi,
                 acc_ref, c_ref, o_ref.at[rows, :], n_pairs, companions=gla, on_companions=keep_states)
        return carry

    lax.fori_loop(0, blocks_per_step, one_block, 0)


def _sb_sample_kernel(q_ref, kn_ref, vn_ref, kc_ref, vc_ref, o_ref, acc_ref, c_ref, kpad_ref, vpad_ref,
                      *, ts, tk, n_cache_blocks, n_pairs):
    q_heads, lo = _split_heads(q_ref, n_pairs, ts)
    kpad_ref[...] = jnp.zeros_like(kpad_ref)
    vpad_ref[...] = jnp.zeros_like(vpad_ref)
    kpad_ref[:ts, :] = kn_ref[...]
    vpad_ref[:ts, :] = vn_ref[...]

    def cache_block(n):
        keys = pl.ds(pl.multiple_of((n_cache_blocks - 1 - n) * tk, tk), tk)
        return _KV(kc_ref[:, keys].astype(BF16), vc_ref[:, keys].astype(BF16), True)

    strictly_earlier = (lax.broadcasted_iota(jnp.int32, (ts, tk), 1)
                        < lax.broadcasted_iota(jnp.int32, (ts, tk), 0))
    _sb_core(q_heads, lo, _KV(kpad_ref[...], vpad_ref[...], False), strictly_earlier, cache_block,
             n_cache_blocks, acc_ref, c_ref, o_ref, n_pairs)


def _mixer_ab_prompt(q, k, v, qb, kb, vb, la):
    b, t, _ = q.shape
    tb = ATT_BLOCK
    assert t % tb == 0 and tb % GLA_CHUNK == 0 and tb <= GLA_MAX_TILE
    n_pairs = A_WIDTH // LANES
    per_step = math.gcd(t // tb, MIXER_BLOCKS_PER_STEP)
    rows = lambda width: pl.BlockSpec((None, per_step * tb, width), lambda bi, qi: (bi, qi, 0))
    kv_spec = pl.BlockSpec((None, A_WIDTH, t), lambda bi, qi: (bi, 0, 0))
    state = (B_HEADS, B_DK, B_DV)
    return pl.pallas_call(
        functools.partial(_mixer_ab_prompt_kernel, tb=tb, n_pairs=n_pairs, blocks_per_step=per_step),
        grid=(b, t // (per_step * tb)),
        in_specs=[rows(A_WIDTH), kv_spec, kv_spec, rows(B_KW), rows(B_KW), rows(B_VW), rows(B_KW)],
        out_specs=[rows(A_WIDTH), rows(B_VW), pl.BlockSpec((None, *state), lambda bi, qi: (bi, 0, 0, 0))],
        out_shape=[jax.ShapeDtypeStruct(q.shape, BF16), jax.ShapeDtypeStruct(vb.shape, F32),
                   jax.ShapeDtypeStruct((b, *state), F32)],
        scratch_shapes=[pltpu.VMEM((tb, A_WIDTH), F32), pltpu.VMEM((2 * n_pairs, tb, 1), F32),
                        pltpu.VMEM((B_HEADS, LANES, B_DV), F32)],
        compiler_params=_params("parallel", "arbitrary"),
        name="mixer_ab",
    )(q, k, v, qb, kb, vb, la)


def _sb_attention_sample(q, k_new, v_new, k_cache, v_cache, layer):
    b, ts, _ = q.shape
    past = k_cache.shape[3]
    tk = ATT_BLOCK
    assert past % tk == 0 and ts <= tk
    n_pairs = SB_PAIRS_PER_STEP
    w = n_pairs * LANES
    new_spec = pl.BlockSpec((None, ts, w), lambda bi, hp: (bi, 0, hp))
    cache_spec = pl.BlockSpec((None, None, w, past), lambda bi, hp: (layer, bi, hp, 0))
    return pl.pallas_call(
        functools.partial(_sb_sample_kernel, ts=ts, tk=tk, n_cache_blocks=past // tk, n_pairs=n_pairs),
        grid=(b, A_WIDTH // w),
        in_specs=[new_spec, new_spec, new_spec, cache_spec, cache_spec],
        out_specs=new_spec,
        out_shape=jax.ShapeDtypeStruct(q.shape, BF16),
        scratch_shapes=[pltpu.VMEM((ts, w), F32), pltpu.VMEM((2 * n_pairs, ts, 1), F32),
                        pltpu.VMEM((tk, w), BF16), pltpu.VMEM((tk, w), BF16)],
        compiler_params=_params("parallel", "parallel"),
        name="sb_attention_sample",
    )(q, k_new, v_new, k_cache, v_cache)


def _band_core(q_heads, lo, kv, pens, bias, o_ref, n_pairs):
    heads = range(2 * n_pairs)
    blocks = range(len(kv))
    z = [[kv[i].scores(q_heads[h], h // 2) + bias(h, i) for i in blocks] for h in heads]
    acc = []
    for h in heads:
        m = None
        for i in blocks:
            mi = jnp.max(z[h][i], axis=-1, keepdims=True)
            if pens[i] is not None:
                mi = mi + pens[i]
            m = mi if m is None else jnp.maximum(m, mi)
        a = None
        for i in blocks:
            shift = m if pens[i] is None else m - pens[i]
            p = jnp.exp2(z[h][i] - shift).astype(BF16)
            v = kv[i].values(h // 2)
            ones = jnp.ones_like(v)
            first = kv[i].head_lanes()
            v = jnp.where(first, v, ones) if h % 2 == 0 else jnp.where(first, ones, v)
            pv = kv[i].weighted(p, v)
            a = pv if a is None else a + pv
        acc.append(a)
    for p in range(n_pairs):
        a0, a1 = acc[2 * p], acc[2 * p + 1]
        o_ref[:, _pair_cols(p)] = jnp.where(lo, a0 / pltpu.roll(a0, HEAD_DIM, axis=1),
                                            a1 / pltpu.roll(a1, HEAD_DIM, axis=1)).astype(o_ref.dtype)


def _band_prompt_kernel(q_ref, k_ref, v_ref, bias_ref, o_ref, *, tq, tk, n_pairs, blocks_per_step):
    def one_block(s, carry):
        qi = pl.program_id(2) * blocks_per_step + s
        rows = pl.ds(pl.multiple_of(s * tq, tq), tq)
        q_heads, lo = _split_heads(q_ref.at[rows, :], n_pairs, tq)
        kv, pens = [], []
        for dj in range(3):
            j = qi - dj
            pens.append(None if dj == 0 else jnp.where(j >= 0, 0.0, NEG_BIG).astype(F32))
            keys = pl.ds(pl.multiple_of(jnp.maximum(j, 0) * tk, tk), tk)
            kv.append(_KV(k_ref[:, keys], v_ref[:, keys], True))
        _band_core(q_heads, lo, kv, pens, lambda h, i: bias_ref[h, i], o_ref.at[rows, :], n_pairs)
        return carry

    lax.fori_loop(0, blocks_per_step, one_block, 0)


def _band_sample_kernel(q_ref, kn_ref, vn_ref, kc_ref, vc_ref, bias_ref, o_ref, kpad_ref, vpad_ref,
                        *, ts, tk, n_cache_blocks, n_pairs):
    q_heads, lo = _split_heads(q_ref, n_pairs, ts)
    kpad_ref[...] = jnp.zeros_like(kpad_ref)
    vpad_ref[...] = jnp.zeros_like(vpad_ref)
    kpad_ref[:ts, :] = kn_ref[...]
    vpad_ref[:ts, :] = vn_ref[...]
    kv = [_KV(kpad_ref[...], vpad_ref[...], False)]
    for dj in range(1, n_cache_blocks + 1):
        keys = slice((n_cache_blocks - dj) * tk, (n_cache_blocks - dj + 1) * tk)
        kv.append(_KV(kc_ref[:, keys].astype(BF16), vc_ref[:, keys].astype(BF16), True))
    is_new_key = lax.broadcasted_iota(jnp.int32, (ts, tk), 1) < ts

    def bias(h, i):
        return jnp.where(is_new_key, bias_ref[h, 0], NEG_BIG) if i == 0 else bias_ref[h, i]

    _band_core(q_heads, lo, kv, [None] * len(kv), bias, o_ref, n_pairs)


def _band_attention_prompt(q, k, v, bias):
    b, t, _ = q.shape
    tq = tk = ATT_BLOCK
    assert t % tk == 0
    n_pairs = BAND_PAIRS_PER_STEP
    w = n_pairs * LANES
    per_step = math.gcd(t // tq, BAND_BLOCKS_PER_STEP)
    kern = functools.partial(_band_prompt_kernel, tq=tq, tk=tk, n_pairs=n_pairs, blocks_per_step=per_step)
    kv_spec = pl.BlockSpec((None, w, t), lambda bi, hp, qi: (bi, hp, 0))
    q_spec = pl.BlockSpec((None, per_step * tq, w), lambda bi, hp, qi: (bi, qi, hp))
    bias_spec = pl.BlockSpec((2 * n_pairs, 3, tq, tk), lambda bi, hp, qi: (hp, 0, 0, 0))
    return pl.pallas_call(
        kern,
        grid=(b, C_WIDTH // w, t // (per_step * tq)),
        in_specs=[q_spec, kv_spec, kv_spec, bias_spec],
        out_specs=q_spec,
        out_shape=jax.ShapeDtypeStruct(q.shape, BF16),
        compiler_params=_params("parallel", "parallel", "parallel"),
        name="band_attention",
    )(q, k, v, bias)


def _band_attention_sample(q, k_new, v_new, k_cache, v_cache, layer, bias):
    b, ts, _ = q.shape
    wc = k_cache.shape[3]
    tk = ATT_BLOCK
    n_cache_blocks = min(wc // tk, 2)
    assert wc % (n_cache_blocks * tk) == 0 and ts <= tk
    n_pairs = BAND_PAIRS_PER_STEP
    w = n_pairs * LANES
    kern = functools.partial(_band_sample_kernel, ts=ts, tk=tk, n_cache_blocks=n_cache_blocks,
                             n_pairs=n_pairs)
    new_spec = pl.BlockSpec((None, ts, w), lambda bi, hp: (bi, 0, hp))
    cache_rows = n_cache_blocks * tk
    cache_spec = pl.BlockSpec((None, None, w, cache_rows),
                              lambda bi, hp: (layer, bi, hp, wc // cache_rows - 1))
    bias_spec = pl.BlockSpec((2 * n_pairs, 1 + n_cache_blocks, ts, tk), lambda bi, hp: (hp, 0, 0, 0))
    return pl.pallas_call(
        kern,
        grid=(b, C_WIDTH // w),
        in_specs=[new_spec, new_spec, new_spec, cache_spec, cache_spec, bias_spec],
        out_specs=new_spec,
        out_shape=jax.ShapeDtypeStruct(q.shape, BF16),
        scratch_shapes=[pltpu.VMEM((tk, w), BF16), pltpu.VMEM((tk, w), BF16)],
        compiler_params=_params("parallel", "parallel"),
        name="band_attention_sample",
    )(q, k_new, v_new, k_cache, v_cache, bias)


def _band_bias_kernel(g_ref, o_ref):
    rows = tk = ATT_BLOCK
    q_chunk = lax.broadcasted_iota(jnp.int32, (rows, tk), 0) // CHUNK
    k_chunk = lax.broadcasted_iota(jnp.int32, (rows, tk), 1) // CHUNK
    for dj in range(3):
        g = jnp.broadcast_to(g_ref[0, dj], (rows, 2 * tk))
        tile = pltpu.roll(g, 0, axis=1, stride=1, stride_axis=0)[:, :tk]
        diff = dj * (tk // CHUNK) + q_chunk - k_chunk
        seen = (diff >= 0) & (diff <= C_LEFT_CHUNKS)
        o_ref[0, dj] = jnp.where(seen, tile * LOG2E, NEG_BIG)


def _band_bias(rel_table):
    rows = tk = ATT_BLOCK
    c = jnp.arange(2 * tk, dtype=jnp.int32)
    u = jnp.where(c <= tk, -c, 2 * tk - c)
    idx = jnp.clip(jnp.arange(3, dtype=jnp.int32)[:, None] * tk + u[None, :], REL_MIN, REL_MAX) - REL_MIN
    g = rel_table[:, idx].astype(F32).reshape(C_HEADS, 3, 1, 2 * tk)
    return pl.pallas_call(
        _band_bias_kernel,
        grid=(C_HEADS,),
        in_specs=[pl.BlockSpec((1, 3, 1, 2 * tk), lambda h: (h, 0, 0, 0))],
        out_specs=pl.BlockSpec((1, 3, rows, tk), lambda h: (h, 0, 0, 0)),
        out_shape=jax.ShapeDtypeStruct((C_HEADS, 3, rows, tk), F32),
        compiler_params=_params("parallel"),
        name="band_bias",
    )(g)


def _gla_pair_tile(q, k, v, la, states, emit):
    L = GLA_CHUNK
    chunks = range(q.shape[0] // L)
    row = lax.broadcasted_iota(jnp.int32, (L, L), 0)
    colm = lax.broadcasted_iota(jnp.int32, (L, L), 1)
    tri = (colm <= row).astype(BF16)
    causal = colm <= row
    lane = lax.broadcasted_iota(jnp.int32, (L, LANES), 1)
    sub = lax.broadcasted_iota(jnp.int32, (LANES, B_DV), 0)
    mine = [(lane >= h * B_DK) & (lane < (h + 1) * B_DK) for h in range(2)]
    mine_rows = [(sub >= h * B_DK) & (sub < (h + 1) * B_DK) for h in range(2)]
    rows = [slice(c * L, (c + 1) * L) for c in chunks]

    b = []
    for r in rows:
        g_hi, g_lo = _split_bf16(la[r, :])
        b.append(_dot(tri, g_hi) + _dot(tri, g_lo))
    yield
    qg, qg_h, kg, kd_t, decay = [], [], [], [], []
    for c, r in zip(chunks, rows):
        qg_c = q[r, :] * (B_DK ** -0.5) * jnp.exp(b[c])
        qg.append(qg_c.astype(BF16))
        qg_h.append([jnp.where(mine[h], qg_c, 0.0).astype(BF16) for h in range(2)])
        kg.append((k[r, :] * jnp.exp(-b[c])).astype(BF16))
        b_t = b[c].T
        b_last = b_t[:, L - 1:L]
        kd_t.append((k[r, :].T * jnp.exp(b_last - b_t)).astype(BF16))
        decay.append(jnp.exp(b_last))
    yield
    att =[[jnp.where(causal, _dot_nt(qg_h[c][h], kg[c]), 0.0).astype(BF16) for h in range(2)]
           for c in chunks]
    yield
    o_intra, own = [], []
    for c, r in zip(chunks, rows):
        v_h = [v[r, h * B_DV:(h + 1) * B_DV].astype(BF16) for h in range(2)]
        o_intra.append([_dot(att[c][h], v_h[h]) for h in range(2)])
        own.append([jnp.where(mine_rows[h], _dot(kd_t[c], v_h[h]), 0.0) for h in range(2)])

    yield
    states = list(states)
    start = []
    for c in chunks:
        start.append([s.astype(BF16) for s in states])
        states = [decay[c] * states[h] + own[c][h] for h in range(2)]
    yield
    for c in chunks:
        for h in range(2):
            emit(c, h, o_intra[c][h] + _dot(qg[c], start[c][h]))
    return states


def _pair_states(s0_ref, p):
    zeros_state = jnp.zeros((B_DK, B_DV), F32)
    return [jnp.concatenate([s0_ref[2 * p], zeros_state], axis=0),
            jnp.concatenate([zeros_state, s0_ref[2 * p + 1]], axis=0)]


def _pair_v_cols(p):
    return slice(2 * p * B_DV, 2 * (p + 1) * B_DV)


def _gla_rows_emit(o_ref, p):
    def emit(c, h, o):
        o_ref[c * GLA_CHUNK:(c + 1) * GLA_CHUNK, (2 * p + h) * B_DV:(2 * p + h + 1) * B_DV] = o
    return emit


def _gla_kernel(q_ref, k_ref, v_ref, la_ref, s0_ref, o_ref, s_out_ref):
    pairs = range(B_HEADS // 2)
    states = _interleave([
        _gla_pair_tile(q_ref[:, _pair_cols(p)], k_ref[:, _pair_cols(p)], v_ref[:, _pair_v_cols(p)],
                       la_ref[:, _pair_cols(p)], _pair_states(s0_ref, p), _gla_rows_emit(o_ref, p))
        for p in pairs])
    for p in pairs:
        s_out_ref[2 * p] = states[p][0][:B_DK, :]
        s_out_ref[2 * p + 1] = states[p][1][B_DK:, :]


def _gla(q, k, v, la, s0):
    b, t, _ = q.shape
    assert t % GLA_CHUNK == 0 and t <= GLA_MAX_TILE
    qk_spec = pl.BlockSpec((None, t, B_KW), lambda bi: (bi, 0, 0))
    v_spec = pl.BlockSpec((None, t, B_VW), lambda bi: (bi, 0, 0))
    s_spec = pl.BlockSpec((None, B_HEADS, B_DK, B_DV), lambda bi: (bi, 0, 0, 0))
    return pl.pallas_call(
        _gla_kernel,
        grid=(b,),
        in_specs=[qk_spec, qk_spec, v_spec, qk_spec, s_spec],
        out_specs=[v_spec, s_spec],
        out_shape=[jax.ShapeDtypeStruct(v.shape, F32), jax.ShapeDtypeStruct(s0.shape, F32)],
        compiler_params=_params("parallel"),
        name="gla",
    )(q, k, v, la, s0)


def _pad_rows(x, n):
    return jnp.pad(x, ((0, 0), (0, n - x.shape[1]), (0, 0)))


def _heads_last(x, heads):
    n, b, _, s = x.shape
    return jnp.transpose(x.reshape(n, b, heads, HEAD_DIM, s), (0, 1, 4, 2, 3))


def _feature_major(cache):
    n, b, s, heads, hd = cache.shape
    return jnp.transpose(cache, (0, 1, 3, 4, 2)).reshape(n, b, heads * hd, s)


def _row_tile(m):
    for tm in (512, 256, 128, 64, 32, 16, 8):
        if m % tm == 0:
            return tm
    raise ValueError(f"token count {m} is not a multiple of 8")


def kernel(x_prompt, x_sample, cache_a_k, cache_a_v, state_b, cache_c_k, cache_c_v, norm_mix_g, norm_ffn_g, w_in_ab, w_gate_b, b_gate_b, norm_gla_g, w_out_ab, w_qkv_c, rel_bias_c, w_out_c, w_ffn_gate, w_ffn_up, w_ffn_down, norm_final_g):
    bp, tp, d = x_prompt.shape
    bs, ts, _ = x_sample.shape
    depth = norm_mix_g.shape[0]
    past = cache_a_k.shape[2]
    wc = cache_c_k.shape[2]
    assert tp % ATT_BLOCK == 0 and past % ATT_BLOCK == 0 and wc % ATT_BLOCK == 0
    assert ts <= GLA_CHUNK and ts % 8 == 0
    mp, ms = bp * tp, bs * ts
    tmp, tms = _row_tile(tp), _row_tile(ms)
    xp = x_prompt.reshape(mp, d)
    xs = x_sample.reshape(ms, d)
    row2 = lambda a: a.reshape(1, -1)

    a_ks, a_vs, b_sp, b_ss, c_ks, c_vs = [], [], [], [], [], []
    a_kv_prompt = c_kv_prompt = None
    n_ab, n_c = (depth + 1) // 2, depth // 2
    keep = min(C_LEFT_CHUNKS * CHUNK, tp)

    kv0, kv1 = A_WIDTH, 3 * A_WIDTH
    o = 3 * A_WIDTH + 2 * B_KW + B_VW
    w_in_t = jnp.swapaxes(w_in_ab, 1, 2)
    w_main_all = jnp.concatenate([w_in_t[:, :kv0], w_in_t[:, kv1:o], w_in_t[:, o + B_GATE_RANK:]],
                                 axis=1).astype(BF16)
    w_kv_ab_t_all = w_in_t[:, kv0:kv1].astype(BF16)
    w_lr_all = jnp.pad(w_in_t[:, o:o + B_GATE_RANK],
                       ((0, 0), (0, LANES - B_GATE_RANK), (0, 0))).astype(BF16)
    w_gate_all = jnp.pad(w_gate_b, ((0, 0), (0, LANES - B_GATE_RANK), (0, 0))).astype(BF16)
    w_out_ab_all = w_out_ab.astype(BF16)
    w_q_all = w_qkv_c[:, :, :C_WIDTH].astype(BF16)
    w_kv_c_t_all = jnp.swapaxes(w_qkv_c[:, :, C_WIDTH:], 1, 2).astype(BF16)
    w_out_c_all = w_out_c.astype(BF16)
    cache_a_k_fm, cache_a_v_fm, cache_c_k_fm, cache_c_v_fm = (
        _feature_major(c) for c in (cache_a_k, cache_a_v, cache_c_k, cache_c_v))

    for layer in range(depth):
        i = layer // 2
        g_mix = row2(norm_mix_g[layer])
        riders = _Riders((w_ffn_gate, w_ffn_up, w_ffn_down), layer, mp // tmp)
        ffn_ends = (row2(norm_ffn_g[layer]), row2(norm_final_g) if layer == depth - 1 else None)
        if layer % 2 == 0:
            w_main, w_kv_t, w_lr, w_gate, w_out = (
                _Slab(w, i) for w in (w_main_all, w_kv_ab_t_all, w_lr_all, w_gate_all, w_out_ab_all))
            b_gate = row2(b_gate_b[i])
            g_gla = row2(norm_gla_g[i])

            qa, ka, va, kab, vab, qb, kb, vb, r, la, *ffn_w = _proj_ab(
                xp, g_mix, w_main, w_kv_t, w_lr, w_gate, b_gate, tmp, batch=bp,
                stack=_KVStack(i, n_ab, tp, a_kv_prompt), riders=riders)
            ffn = (ffn_ends[0], *ffn_w, ffn_ends[1])
            a_kv_prompt = (ka, va)
            sh = lambda a: a.reshape(bp, tp, -1)
            oa, ob, sbp = _mixer_ab_prompt(sh(qa), kab, vab, sh(qb), sh(kb), sh(vb), sh(la))
            mixed_p = (oa.reshape(mp, -1), ob.reshape(mp, -1), r)
            b_sp.append(sbp)

            qa, ka, va, kab, vab, qb, kb, vb, r, la = _proj_ab(
                xs, g_mix, w_main, w_kv_t, w_lr, w_gate, b_gate, tms)
            sh = lambda a: a.reshape(bs, ts, -1)
            oa = _sb_attention_sample(sh(qa), sh(kab), sh(vab), cache_a_k_fm, cache_a_v_fm, i)
            pad_t = lambda a: _pad_rows(sh(a), GLA_CHUNK)
            ob, sbs = _gla(pad_t(qb), pad_t(kb), pad_t(vb), pad_t(la), state_b[i])
            mixed_s = (oa.reshape(ms, -1), ob[:, :ts].reshape(ms, -1), r)
            xp, xs = _layer_tail(xp, mixed_p, xs, mixed_s, (g_gla, w_out), *ffn, tmp)
            a_ks.append(ka.reshape(bs, ts, A_HEADS, HEAD_DIM))
            a_vs.append(va.reshape(bs, ts, A_HEADS, HEAD_DIM))
            b_ss.append(sbs)
        else:
            w_q, w_kv_t, w_out = (_Slab(w, i) for w in (w_q_all, w_kv_c_t_all, w_out_c_all))

            q, k, v, kb16, vb16, *ffn_w = _proj_c(xp, g_mix, w_q, w_kv_t, tmp, batch=bp,
                                                  stack=_KVStack(i, n_c, keep, c_kv_prompt), riders=riders)
            ffn = (ffn_ends[0], *ffn_w, ffn_ends[1])
            c_kv_prompt = (k, v)
            bias = _band_bias(rel_bias_c[i])
            oc = _band_attention_prompt(q.reshape(bp, tp, -1), kb16, vb16, bias)
            mixed_p = (oc.reshape(mp, -1),)

            q, k, v, kb16, vb16 = _proj_c(xs, g_mix, w_q, w_kv_t, tms)
            sh = lambda a: a.reshape(bs, ts, -1)
            oc = _band_attention_sample(sh(q), sh(kb16), sh(vb16), cache_c_k_fm, cache_c_v_fm, i, bias)
            xp, xs = _layer_tail(xp, mixed_p, xs, (oc.reshape(ms, -1),), (w_out,), *ffn, tmp)
            c_ks.append(k.reshape(bs, ts, C_HEADS, HEAD_DIM))
            c_vs.append(v.reshape(bs, ts, C_HEADS, HEAD_DIM))

    y_prompt = xp.reshape(bp, tp, d)
    y_sample = xs.reshape(bs, ts, d)
    a_kp, a_vp = (_heads_last(a, A_HEADS) for a in a_kv_prompt)
    c_kp, c_vp = (_heads_last(a, C_HEADS) for a in c_kv_prompt)
    return (y_prompt, y_sample, a_kp, a_vp, jnp.stack(a_ks), jnp.stack(a_vs),
            jnp.stack(b_sp), jnp.stack(b_ss), c_kp, c_vp, jnp.stack(c_ks), jnp.stack(c_vs))
```

```python
import functools
import math

import jax
import jax.numpy as jnp
from jax import lax
from jax.experimental import pallas as pl
from jax.experimental.pallas import tpu as pltpu

F32 = jnp.float32
BF16 = jnp.bfloat16

EPS = 1e-6
HEAD_DIM = 64
LANES = 128
A_HEADS = 8
A_WIDTH = A_HEADS * HEAD_DIM
B_HEADS = 4
B_DK = 64
B_DV = 128
B_KW = B_HEADS * B_DK
B_VW = B_HEADS * B_DV
B_GATE_RANK = 16
B_GATE_TEMP = 16.0
GLA_CHUNK = 64
C_HEADS = 16
C_WIDTH = C_HEADS * HEAD_DIM
CHUNK = 64
C_LEFT_CHUNKS = 8
REL_MIN = -(CHUNK - 1)
REL_MAX = 128
ATT_BLOCK = 256
NEG_BIG = -1e30
LOG2E = 1.4426950408889634
SB_DEAD = 104.0
SB_PAIRS_PER_STEP = 4
BAND_PAIRS_PER_STEP = 4
MIXER_BLOCKS_PER_STEP = 4
BAND_BLOCKS_PER_STEP = 8
GLA_MAX_TILE = 512
VMEM_LIMIT = 56 * 1024 * 1024


def _params(*sem):
    return pltpu.CompilerParams(dimension_semantics=sem, vmem_limit_bytes=VMEM_LIMIT)


class _Slab:
    def __init__(self, stacked, index):
        self.stacked, self.index, self.shape = stacked, index, stacked.shape[1:]


def _operand(a):
    return a.stacked if isinstance(a, _Slab) else a


def _resident(a):
    if isinstance(a, _Slab):
        index = (a.index,) + (0,) * len(a.shape)
        return pl.BlockSpec((None, *a.shape), lambda *_: index, pipeline_mode=pl.Buffered(1))
    return pl.BlockSpec(a.shape, lambda *_: (0,) * a.ndim, pipeline_mode=pl.Buffered(1))


def _rms(x, g):
    return x * lax.rsqrt(jnp.mean(x * x, axis=-1, keepdims=True) + EPS) * g


def _log_sigmoid_pair(z):
    l = jnp.log1p(jnp.exp(-jnp.abs(z)))
    return jnp.minimum(z, 0.0) - l, jnp.minimum(-z, 0.0) - l


def _split_bf16(x):
    hi = x.astype(BF16)
    lo = (x - hi.astype(F32)).astype(BF16)
    return hi, lo


def _dot(a, b):
    return jnp.dot(a, b, preferred_element_type=F32)


def _dot_nt(a, b):
    return lax.dot_general(a, b, (((1,), (1,)), ((), ())), preferred_element_type=F32)


def _dot_tn(a, b):
    return lax.dot_general(a, b, (((0,), (0,)), ((), ())), preferred_element_type=F32)


def _pair_cols(p):
    return slice(p * LANES, (p + 1) * LANES)


class _KV:
    def __init__(self, k, v, feature_major):
        self.k, self.v, self.feature_major = k, v, feature_major
        self.n_keys = k.shape[1] if feature_major else k.shape[0]

    def scores(self, q_h, p):
        if self.feature_major:
            return _dot(q_h, self.k[_pair_cols(p), :])
        return _dot_nt(q_h, self.k[:, _pair_cols(p)])

    def first(self, n):
        if self.feature_major:
            return _KV(self.k[:, :n], self.v[:, :n], True)
        return _KV(self.k[:n], self.v[:n], False)

    def last(self, n):
        if self.feature_major:
            return _KV(self.k[:, -n:], self.v[:, -n:], True)
        return _KV(self.k[-n:], self.v[-n:], False)

    def values(self, p):
        return self.v[_pair_cols(p), :] if self.feature_major else self.v[:, _pair_cols(p)]

    def weighted(self, w, v_p):
        return _dot_nt(w, v_p) if self.feature_major else _dot(w, v_p)

    def head_lanes(self):
        shape = (LANES, self.n_keys) if self.feature_major else (self.n_keys, LANES)
        return lax.broadcasted_iota(jnp.int32, shape, 0 if self.feature_major else 1) < HEAD_DIM


def _split_heads(q_ref, n_pairs, tq):
    lo = lax.broadcasted_iota(jnp.int32, (tq, LANES), 1) < HEAD_DIM
    heads = []
    for p in range(n_pairs):
        q = q_ref[:, _pair_cols(p)]
        heads += [jnp.where(lo, q, jnp.zeros_like(q)), jnp.where(lo, jnp.zeros_like(q), q)]
    return heads, lo


def _emit_kv(y, wkv_ref, k_ref, v_ref, kb_ref, vb_ref, feature_major, kept_tail=None, own_slot=0):
    if not feature_major:
        kv = _dot_nt(y, wkv_ref[...])
        width = kv.shape[1] // 2
        k, v = kv[:, :width], kv[:, width:]
        k_ref[...], v_ref[...], kb_ref[...], vb_ref[...] = k, v, k.astype(BF16), v.astype(BF16)
        return
    kv = _dot_nt(wkv_ref[...], y)
    width = kv.shape[0] // 2
    k, v = kv[:width, :], kv[width:, :]
    kb_ref[...] = k.astype(BF16)
    vb_ref[...] = v.astype(BF16)
    slot = own_slot
    other_slots = [s for s in range(k_ref.shape[0]) if s != slot]
    if kept_tail is None:
        fill, fill_cols = slice(None), k_ref.shape[2]
        k_ref[slot] = k
        v_ref[slot] = v
    else:
        n_tiles, n_kept = kept_tail
        tile = pl.program_id(0) % n_tiles
        tm = k.shape[1]
        cols = pl.ds(pl.multiple_of(jnp.maximum(tile - (n_tiles - n_kept), 0) * tm, tm), tm)
        k_ref[slot, :, cols] = k
        v_ref[slot, :, cols] = v
        share = k_ref.shape[2] // n_tiles
        if share % LANES == 0 and share * n_tiles == k_ref.shape[2]:
            fill, fill_cols = pl.ds(pl.multiple_of(tile * share, share), share), share
        else:
            fill, fill_cols = slice(None), k_ref.shape[2]
    for dst in (k_ref, v_ref):
        for s in other_slots:
            dst[s, :, fill] = jnp.zeros((dst.shape[1], fill_cols), dst.dtype)


class _Riders:
    def __init__(self, stacks, layer, steps):
        self.stacks, self.layer = list(stacks), layer
        self.chunks = next(c for c in (16, 8, 4, 2, 1)
                           if c <= steps and all(w.shape[1] % (16 * c) == 0 for w in self.stacks))

    def specs(self):
        layer, last = self.layer, self.chunks - 1
        ins = [pl.BlockSpec((None, w.shape[1] // self.chunks, w.shape[2]),
                            lambda i: (layer, jnp.minimum(i, last), 0)) for w in self.stacks]
        outs = [pl.BlockSpec((w.shape[1] // self.chunks, w.shape[2]),
                             lambda i: (jnp.minimum(i, last), 0)) for w in self.stacks]
        shapes = [jax.ShapeDtypeStruct(w.shape[1:], BF16) for w in self.stacks]
        return ins, outs, shapes


def _split_extra_refs(refs, n_aliased, n_riders):
    rider_in = refs[n_aliased:n_aliased + n_riders]
    outs = refs[n_aliased + n_riders:len(refs) - n_riders]
    return outs, list(zip(rider_in, refs[len(refs) - n_riders:]))


def _cast_riders(pairs):
    for src, dst in pairs:
        dst[...] = src[...].astype(BF16)


def _proj_ab_kernel(x_ref, g_ref, w_ref, wkv_ref, wlr_ref, wgate_ref, bgate_ref, *refs,
                    feature_major, n_aliased, n_riders, own_slot):
    outs, riders = _split_extra_refs(refs, n_aliased, n_riders)
    qa_ref, ka_ref, va_ref, kab_ref, vab_ref, qb_ref, kb_ref, vb_ref, r_ref, la_ref = outs
    _cast_riders(riders)
    y = _rms(x_ref[...], g_ref[...]).astype(BF16)
    _emit_kv(y, wkv_ref, ka_ref, va_ref, kab_ref, vab_ref, feature_major, own_slot=own_slot)
    z = _dot_nt(y, w_ref[...])
    c = 0
    qa_ref[...] = (z[:, c:c + A_WIDTH] * (HEAD_DIM ** -0.5)).astype(BF16); c += A_WIDTH
    qb_ref[...] = z[:, c:c + B_KW]; c += B_KW
    kb_ref[...] = z[:, c:c + B_KW]; c += B_KW
    vb_ref[...] = z[:, c:c + B_VW]; c += B_VW
    r_ref[...] = z[:, c:c + B_VW]
    g_lr = _dot_nt(y, wlr_ref[...])
    gate = _dot(g_lr.astype(BF16), wgate_ref[...]) + bgate_ref[...]
    la_ref[...] = _log_sigmoid_pair(gate)[0] * (1.0 / B_GATE_TEMP)


class _KVStack:
    def __init__(self, layer, n_layers, keep, previous=None):
        self.layer, self.n_layers, self.keep, self.previous = layer, n_layers, keep, previous


def _kv_out(m, width, tm, batch, stack):
    if batch is None:
        spec = pl.BlockSpec((tm, width), lambda i: (i, 0))
        return [spec] * 4, [jax.ShapeDtypeStruct((m, width), dt) for dt in (F32, F32, BF16, BF16)], None, 0
    t = m // batch
    assert t % tm == 0 and stack.keep % tm == 0 and stack.keep <= t
    n_tiles = t // tm
    copy_spec = pl.BlockSpec((None, width, tm), lambda i: (i // n_tiles, 0, i % n_tiles))
    copy_shape = jax.ShapeDtypeStruct((batch, width, t), BF16)
    slots, first_slot, own_slot = ((stack.n_layers, 0, stack.layer) if stack.previous is None
                                   else (1, stack.layer, 0))
    if stack.keep == t:
        f32_spec = pl.BlockSpec((slots, None, width, tm), lambda i: (first_slot, i // n_tiles, 0, i % n_tiles))
        kept_tail = None
    else:
        f32_spec = pl.BlockSpec((slots, None, width, stack.keep), lambda i: (first_slot, i // n_tiles, 0, 0))
        kept_tail = (n_tiles, stack.keep // tm)
    f32_shape = jax.ShapeDtypeStruct((stack.n_layers, batch, width, stack.keep), F32)
    return ([f32_spec, f32_spec, copy_spec, copy_spec], [f32_shape, f32_shape, copy_shape, copy_shape],
            kept_tail, own_slot)


def _aliased_stack(stack, n_inputs, first_output):
    if stack is None or stack.previous is None:
        return [], [], {}
    prev = list(stack.previous)
    specs = [pl.BlockSpec(memory_space=pl.ANY)] * len(prev)
    return prev, specs, {n_inputs + j: first_output + j for j in range(len(prev))}


def _rider_plumbing(riders):
    if riders is None:
        return [], [], [], []
    return (riders.stacks, *riders.specs())


def _proj_ab(x, g, w_main, w_kv, w_lr, w_gate, b_gate, tm, batch=None, stack=None, riders=None):
    m, d = x.shape
    row = lambda n: pl.BlockSpec((tm, n), lambda i: (i, 0))
    params = (g, w_main, w_kv, w_lr, w_gate, b_gate)
    kv_specs, kv_shapes, kept_tail, own_slot = _kv_out(m, A_WIDTH, tm, batch, stack)
    assert kept_tail is None
    prev, prev_specs, aliases = _aliased_stack(stack, 1 + len(params), 1)
    ride_ops, ride_in, ride_out, ride_shapes = _rider_plumbing(riders)
    rest = [(B_KW, F32), (B_KW, F32), (B_VW, F32), (B_VW, F32), (B_KW, F32)]
    return pl.pallas_call(
        functools.partial(_proj_ab_kernel, feature_major=batch is not None, n_aliased=len(prev),
                          n_riders=len(ride_ops), own_slot=own_slot),
        grid=(m // tm,),
        in_specs=[row(d)] + [_resident(a) for a in params] + prev_specs + ride_in,
        out_specs=[row(A_WIDTH)] + kv_specs + [row(n) for n, _ in rest] + ride_out,
        out_shape=([jax.ShapeDtypeStruct((m, A_WIDTH), BF16)] + kv_shapes
                   + [jax.ShapeDtypeStruct((m, n), dt) for n, dt in rest] + ride_shapes),
        input_output_aliases=aliases,
        compiler_params=_params("arbitrary" if ride_ops else "parallel"),
        name="proj_ab",
    )(x, *[_operand(a) for a in params], *prev, *ride_ops)


def _proj_c_kernel(x_ref, g_ref, wq_ref, wkv_ref, *refs, feature_major, n_aliased, kept_tail,
                   n_riders, own_slot):
    outs, riders = _split_extra_refs(refs, n_aliased, n_riders)
    q_ref, k_ref, v_ref, kb_ref, vb_ref = outs
    _cast_riders(riders)
    y = _rms(x_ref[...], g_ref[...]).astype(BF16)
    _emit_kv(y, wkv_ref, k_ref, v_ref, kb_ref, vb_ref, feature_major, kept_tail, own_slot)
    q_ref[...] = (_dot(y, wq_ref[...]) * (HEAD_DIM ** -0.5 * LOG2E)).astype(BF16)


def _proj_c(x, g, w_q, w_kv, tm, batch=None, stack=None, riders=None):
    m, d = x.shape
    row = lambda n: pl.BlockSpec((tm, n), lambda i: (i, 0))
    params = (g, w_q, w_kv)
    kv_specs, kv_shapes, kept_tail, own_slot = _kv_out(m, C_WIDTH, tm, batch, stack)
    prev, prev_specs, aliases = _aliased_stack(stack, 1 + len(params), 1)
    ride_ops, ride_in, ride_out, ride_shapes = _rider_plumbing(riders)
    return pl.pallas_call(
        functools.partial(_proj_c_kernel, feature_major=batch is not None, n_aliased=len(prev),
                          kept_tail=kept_tail, n_riders=len(ride_ops), own_slot=own_slot),
        grid=(m // tm,),
        in_specs=[row(d)] + [_resident(a) for a in params] + prev_specs + ride_in,
        out_specs=[row(C_WIDTH)] + kv_specs + ride_out,
        out_shape=[jax.ShapeDtypeStruct((m, C_WIDTH), BF16)] + kv_shapes + ride_shapes,
        input_output_aliases=aliases,
        compiler_params=_params("arbitrary" if (kept_tail or ride_ops) else "parallel"),
        name="proj_c",
    )(x, *[_operand(a) for a in params], *prev, *ride_ops)


def _layer_tail_rows(x_ref, mixer_refs, param_refs, o_ref, gla_merge, final_norm):
    if gla_merge:
        oa_ref, ob_ref, r_ref = mixer_refs
        ggla_ref, wout_ref, *param_refs = param_refs
        ob = ob_ref[...]
        parts = []
        for h in range(B_HEADS):
            seg = ob[:, h * B_DV:(h + 1) * B_DV]
            parts.append(seg * lax.rsqrt(jnp.mean(seg * seg, axis=-1, keepdims=True) + EPS))
        r = r_ref[...]
        obn = jnp.concatenate(parts, axis=-1) * ggla_ref[...] * (r * jax.nn.sigmoid(r))
        mix = _dot(oa_ref[...], wout_ref[:A_WIDTH, :]) + _dot(obn.astype(BF16), wout_ref[A_WIDTH:, :])
    else:
        (oc_ref,) = mixer_refs
        wout_ref, *param_refs = param_refs
        mix = _dot(oc_ref[...], wout_ref[...])
    gffn_ref, wg_ref, wu_ref, wd_ref, *param_refs = param_refs
    x = x_ref[...] + mix
    y = _rms(x, gffn_ref[...]).astype(BF16)
    h = _dot(y, wg_ref[...])
    u = _dot(y, wu_ref[...])
    a = (h * jax.nn.sigmoid(h) * u).astype(BF16)
    x = x + _dot(a, wd_ref[...])
    o_ref[...] = _rms(x, param_refs[0][...]) if final_norm else x


def _layer_tail_kernel(*refs, n_mixer, n_params, gla_merge, final_norm, main_steps):
    main, extra = refs[:1 + n_mixer], refs[1 + n_mixer:2 + 2 * n_mixer]
    params = refs[2 + 2 * n_mixer:2 + 2 * n_mixer + n_params]
    o_ref, o_extra_ref = refs[-2:]

    @pl.when(pl.program_id(0) < main_steps)
    def _():
        _layer_tail_rows(main[0], main[1:], params, o_ref, gla_merge, final_norm)

    @pl.when(pl.program_id(0) == main_steps)
    def _():
        _layer_tail_rows(extra[0], extra[1:], params, o_extra_ref, gla_merge, final_norm)


def _layer_tail(x, mixer_out, x_extra, mixer_out_extra, mixer_params, g_ffn, wg, wu, wd, g_fin, tm):
    m, d = x.shape
    main_steps = m // tm
    row = lambda a: pl.BlockSpec((tm, a.shape[1]), lambda i: (jnp.minimum(i, main_steps - 1), 0))
    whole = lambda a: pl.BlockSpec(a.shape, lambda i: (0, 0))
    params = [*mixer_params, g_ffn, wg, wu, wd] + ([] if g_fin is None else [g_fin])
    return pl.pallas_call(
        functools.partial(_layer_tail_kernel, n_mixer=len(mixer_out), n_params=len(params),
                          gla_merge=len(mixer_out) == 3, final_norm=g_fin is not None, main_steps=main_steps),
        grid=(main_steps + 1,),
        in_specs=([row(x)] + [row(a) for a in mixer_out] + [whole(x_extra)]
                  + [whole(a) for a in mixer_out_extra] + [_resident(a) for a in params]),
        out_specs=[row(x), whole(x_extra)],
        out_shape=[jax.ShapeDtypeStruct((m, d), F32), jax.ShapeDtypeStruct(x_extra.shape, F32)],
        compiler_params=_params("arbitrary"),
        name="layer_tail",
    )(x, *mixer_out, x_extra, *mixer_out_extra, *[_operand(a) for a in params])


def _interleave(stage_generators):
    results = [None] * len(stage_generators)
    live = list(range(len(stage_generators)))
    while live:
        for i in list(live):
            try:
                next(stage_generators[i])
            except StopIteration as done:
                results[i] = done.value
                live.remove(i)
    return results


def _sb_core(q_heads, lo, first_kv, first_mask, earlier_kv, n_earlier, acc_ref, c_ref, o_ref, n_pairs,
             companions=(), on_companions=None):
    tk = first_kv.n_keys
    heads = range(2 * n_pairs)
    later = (lax.broadcasted_iota(jnp.int32, (tk, tk), 0)
             > lax.broadcasted_iota(jnp.int32, (tk, tk), 1)).astype(BF16)
    c_ref[...] = jnp.zeros_like(c_ref)

    tq = q_heads[0].shape[0]

    def row_groups(kv, mask):
        if mask is None or tq != tk or tq % (2 * LANES):
            return [(slice(None), kv, mask, later)]
        half = tq // 2
        return [(slice(0, half), kv.first(half), mask[:half, :half], later[:half, :half]),
                (slice(half, tq), kv, mask[half:, :], later)]

    def block(kv, mask):
        groups = row_groups(kv, mask)
        chains = [(h, g) for h in heads for g in range(len(groups))]
        z = [groups[g][1].scores(q_heads[h][groups[g][0], :], h // 2) for h, g in chains]
        yield
        log_beta, drop, after = [], [], []
        for i, (h, g) in enumerate(chains):
            d = jnp.maximum(z[i], 0.0) + jnp.log(1.0 + jnp.exp(-jnp.abs(z[i])))
            log_beta.append(z[i] - d)
            drop.append(d if mask is None else jnp.where(groups[g][2], d, 0.0))
        yield
        for i, (h, g) in enumerate(chains):
            hi, lo_part = _split_bf16(drop[i])
            tri = groups[g][3]
            after.append(_dot(hi, tri) + _dot(lo_part, tri))
        yield
        pv = {}
        for i, (h, g) in enumerate(chains):
            rows, kv_g = groups[g][0], groups[g][1]
            c = c_ref[h, rows, :]
            w = jnp.exp(log_beta[i] - after[i] - c)
            if mask is not None:
                w = jnp.where(groups[g][2], w, 0.0)
            c_ref[h, rows, :] = c + after[i][:, 0:1] + drop[i][:, 0:1]
            pv[h, g] = kv_g.weighted(w.astype(BF16), kv_g.values(h // 2))
        yield
        per_head = [pv[h, 0] if len(groups) == 1 else jnp.concatenate([pv[h, g] for g in range(len(groups))], 0)
                    for h in heads]
        out = [jnp.where(lo, per_head[2 * p], per_head[2 * p + 1]) for p in range(n_pairs)]
        return out[0] if n_pairs == 1 else jnp.concatenate(out, axis=-1)

    def all_dead():
        return jnp.min(c_ref[...]) > SB_DEAD

    first, *companion_results = _interleave([block(first_kv, first_mask), *companions])
    acc_ref[...] = first
    if companions:
        on_companions(companion_results)

    def cond(carry):
        n, dead = carry
        return (n < n_earlier) & jnp.logical_not(dead)

    def body(carry):
        n, _ = carry
        acc_ref[...] += _interleave([block(earlier_kv(n), None)])[0]
        return n + 1, all_dead()

    lax.while_loop(cond, body, (jnp.int32(0), jnp.asarray(False)))
    o_ref[...] = acc_ref[...].astype(o_ref.dtype)


def _mixer_ab_prompt_kernel(q_ref, k_ref, v_ref, qb_ref, kb_ref, vb_ref, la_ref, o_ref, ob_ref, s_out_ref,
                            acc_ref, c_ref, carry_ref, *, tb, n_pairs, blocks_per_step):
    pairs = range(B_HEADS // 2)
    strictly_earlier = (lax.broadcasted_iota(jnp.int32, (tb, tb), 1)
                        < lax.broadcasted_iota(jnp.int32, (tb, tb), 0))

    def kv_block(j):
        keys = pl.ds(pl.multiple_of(j * tb, tb), tb)
        return _KV(k_ref[:, keys], v_ref[:, keys], True)

    def keep_states(states):
        for p in pairs:
            carry_ref[2 * p], carry_ref[2 * p + 1] = states[p]
            s_out_ref[2 * p] = states[p][0][:B_DK, :]
            s_out_ref[2 * p + 1] = states[p][1][B_DK:, :]

    @pl.when(pl.program_id(1) == 0)
    def _():
        carry_ref[...] = jnp.zeros_like(carry_ref)

    def one_block(s, carry):
        qi = pl.program_id(1) * blocks_per_step + s
        rows = pl.ds(pl.multiple_of(s * tb, tb), tb)
        gla = [_gla_pair_tile(qb_ref[rows, _pair_cols(p)], kb_ref[rows, _pair_cols(p)],
                              vb_ref[rows, _pair_v_cols(p)], la_ref[rows, _pair_cols(p)],
                              [carry_ref[2 * p], carry_ref[2 * p + 1]],
                              _gla_rows_emit(ob_ref.at[rows, :], p)) for p in pairs]
        q_heads, lo = _split_heads(q_ref.at[rows, :], n_pairs, tb)
        _sb_core(q_heads, lo, kv_block(qi), strictly_earlier, lambda n: kv_block(qi - 1 - n), qi,
                 acc_ref, c_ref, o_ref.at[rows, :], n_pairs, companions=gla, on_companions=keep_states)
        return carry

    lax.fori_loop(0, blocks_per_step, one_block, 0)


def _sb_sample_kernel(q_ref, kn_ref, vn_ref, kc_ref, vc_ref, o_ref, acc_ref, c_ref, kpad_ref, vpad_ref,
                      *, ts, tk, n_cache_blocks, n_pairs):
    q_heads, lo = _split_heads(q_ref, n_pairs, ts)
    kpad_ref[...] = jnp.zeros_like(kpad_ref)
    vpad_ref[...] = jnp.zeros_like(vpad_ref)
    kpad_ref[:ts, :] = kn_ref[...]
    vpad_ref[:ts, :] = vn_ref[...]

    def cache_block(n):
        keys = pl.ds(pl.multiple_of((n_cache_blocks - 1 - n) * tk, tk), tk)
        return _KV(kc_ref[:, keys].astype(BF16), vc_ref[:, keys].astype(BF16), True)

    strictly_earlier = (lax.broadcasted_iota(jnp.int32, (ts, tk), 1)
                        < lax.broadcasted_iota(jnp.int32, (ts, tk), 0))
    _sb_core(q_heads, lo, _KV(kpad_ref[...], vpad_ref[...], False), strictly_earlier, cache_block,
             n_cache_blocks, acc_ref, c_ref, o_ref, n_pairs)


def _mixer_ab_prompt(q, k, v, qb, kb, vb, la):
    b, t, _ = q.shape
    tb = ATT_BLOCK
    assert t % tb == 0 and tb % GLA_CHUNK == 0 and tb <= GLA_MAX_TILE
    n_pairs = A_WIDTH // LANES
    per_step = math.gcd(t // tb, MIXER_BLOCKS_PER_STEP)
    rows = lambda width: pl.BlockSpec((None, per_step * tb, width), lambda bi, qi: (bi, qi, 0))
    kv_spec = pl.BlockSpec((None, A_WIDTH, t), lambda bi, qi: (bi, 0, 0))
    state = (B_HEADS, B_DK, B_DV)
    return pl.pallas_call(
        functools.partial(_mixer_ab_prompt_kernel, tb=tb, n_pairs=n_pairs, blocks_per_step=per_step),
        grid=(b, t // (per_step * tb)),
        in_specs=[rows(A_WIDTH), kv_spec, kv_spec, rows(B_KW), rows(B_KW), rows(B_VW), rows(B_KW)],
        out_specs=[rows(A_WIDTH), rows(B_VW), pl.BlockSpec((None, *state), lambda bi, qi: (bi, 0, 0, 0))],
        out_shape=[jax.ShapeDtypeStruct(q.shape, BF16), jax.ShapeDtypeStruct(vb.shape, F32),
                   jax.ShapeDtypeStruct((b, *state), F32)],
        scratch_shapes=[pltpu.VMEM((tb, A_WIDTH), F32), pltpu.VMEM((2 * n_pairs, tb, 1), F32),
                        pltpu.VMEM((B_HEADS, LANES, B_DV), F32)],
        compiler_params=_params("parallel", "arbitrary"),
        name="mixer_ab",
    )(q, k, v, qb, kb, vb, la)


def _sb_attention_sample(q, k_new, v_new, k_cache, v_cache, layer):
    b, ts, _ = q.shape
    past = k_cache.shape[3]
    tk = ATT_BLOCK
    assert past % tk == 0 and ts <= tk
    n_pairs = SB_PAIRS_PER_STEP
    w = n_pairs * LANES
    new_spec = pl.BlockSpec((None, ts, w), lambda bi, hp: (bi, 0, hp))
    cache_spec = pl.BlockSpec((None, None, w, past), lambda bi, hp: (layer, bi, hp, 0))
    return pl.pallas_call(
        functools.partial(_sb_sample_kernel, ts=ts, tk=tk, n_cache_blocks=past // tk, n_pairs=n_pairs),
        grid=(b, A_WIDTH // w),
        in_specs=[new_spec, new_spec, new_spec, cache_spec, cache_spec],
        out_specs=new_spec,
        out_shape=jax.ShapeDtypeStruct(q.shape, BF16),
        scratch_shapes=[pltpu.VMEM((ts, w), F32), pltpu.VMEM((2 * n_pairs, ts, 1), F32),
                        pltpu.VMEM((tk, w), BF16), pltpu.VMEM((tk, w), BF16)],
        compiler_params=_params("parallel", "parallel"),
        name="sb_attention_sample",
    )(q, k_new, v_new, k_cache, v_cache)


def _band_core(q_heads, lo, kv, pens, bias, o_ref, n_pairs, skip_hidden_quarters=False):
    heads = range(2 * n_pairs)
    tq = q_heads[0].shape[0]
    everything = [(slice(None), [(kv[i], i, slice(None)) for i in range(len(kv))])]
    groups = _band_quarters(kv, tq) if skip_hidden_quarters else everything
    chains = [(h, g) for h in heads for g in range(len(groups))]
    z = [[piece.scores(q_heads[h][groups[g][0], :], h // 2) + bias(h, i)[groups[g][0], cols]
          for piece, i, cols in groups[g][1]] for h, g in chains]
    acc = {}
    for c, (h, g) in enumerate(chains):
        pieces = groups[g][1]
        m = None
        for j, (_, i, _) in enumerate(pieces):
            mi = jnp.max(z[c][j], axis=-1, keepdims=True)
            if pens[i] is not None:
                mi = mi + pens[i]
            m = mi if m is None else jnp.maximum(m, mi)
        a = None
        for j, (piece, i, _) in enumerate(pieces):
            shift = m if pens[i] is None else m - pens[i]
            p = jnp.exp2(z[c][j] - shift).astype(BF16)
            v = piece.values(h // 2)
            ones = jnp.ones_like(v)
            first = piece.head_lanes()
            v = jnp.where(first, v, ones) if h % 2 == 0 else jnp.where(first, ones, v)
            pv = piece.weighted(p, v)
            a = pv if a is None else a + pv
        acc[h, g] = a
    per_head = [acc[h, 0] if len(groups) == 1 else jnp.concatenate([acc[h, g] for g in range(len(groups))], 0)
                for h in heads]
    for p in range(n_pairs):
        a0, a1 = per_head[2 * p], per_head[2 * p + 1]
        o_ref[:, _pair_cols(p)] = jnp.where(lo, a0 / pltpu.roll(a0, HEAD_DIM, axis=1),
                                            a1 / pltpu.roll(a1, HEAD_DIM, axis=1)).astype(o_ref.dtype)


def _band_quarters(kv, tq):
    own, back1, back2 = kv
    half = tq // 2
    assert own.n_keys == tq and half % CHUNK == 0 and 2 * (tq // CHUNK) >= C_LEFT_CHUNKS
    whole = slice(None)
    return [(slice(0, half), [(own.first(half), 0, slice(0, half)), (back1, 1, whole), (back2, 2, whole)]),
            (slice(half, tq), [(own, 0, whole), (back1, 1, whole), (back2.last(half), 2, slice(half, tq))])]


def _band_prompt_kernel(q_ref, k_ref, v_ref, bias_ref, o_ref, *, tq, tk, n_pairs, blocks_per_step):
    def one_block(s, carry):
        qi = pl.program_id(2) * blocks_per_step + s
        rows = pl.ds(pl.multiple_of(s * tq, tq), tq)
        q_heads, lo = _split_heads(q_ref.at[rows, :], n_pairs, tq)
        kv, pens = [], []
        for dj in range(3):
            j = qi - dj
            pens.append(None if dj == 0 else jnp.where(j >= 0, 0.0, NEG_BIG).astype(F32))
            keys = pl.ds(pl.multiple_of(jnp.maximum(j, 0) * tk, tk), tk)
            kv.append(_KV(k_ref[:, keys], v_ref[:, keys], True))
        _band_core(q_heads, lo, kv, pens, lambda h, i: bias_ref.at[h, i], o_ref.at[rows, :], n_pairs,
                   skip_hidden_quarters=True)
        return carry

    lax.fori_loop(0, blocks_per_step, one_block, 0)


def _band_sample_kernel(q_ref, kn_ref, vn_ref, kc_ref, vc_ref, bias_ref, o_ref, kpad_ref, vpad_ref,
                        *, ts, tk, n_cache_blocks, n_pairs):
    q_heads, lo = _split_heads(q_ref, n_pairs, ts)
    kpad_ref[...] = jnp.zeros_like(kpad_ref)
    vpad_ref[...] = jnp.zeros_like(vpad_ref)
    kpad_ref[:ts, :] = kn_ref[...]
    vpad_ref[:ts, :] = vn_ref[...]
    kv = [_KV(kpad_ref[...], vpad_ref[...], False)]
    for dj in range(1, n_cache_blocks + 1):
        keys = slice((n_cache_blocks - dj) * tk, (n_cache_blocks - dj + 1) * tk)
        kv.append(_KV(kc_ref[:, keys].astype(BF16), vc_ref[:, keys].astype(BF16), True))
    is_new_key = lax.broadcasted_iota(jnp.int32, (ts, tk), 1) < ts

    def bias(h, i):
        return jnp.where(is_new_key, bias_ref[h, 0], NEG_BIG) if i == 0 else bias_ref[h, i]

    _band_core(q_heads, lo, kv, [None] * len(kv), bias, o_ref, n_pairs)


def _band_attention_prompt(q, k, v, bias):
    b, t, _ = q.shape
    tq = tk = ATT_BLOCK
    assert t % tk == 0
    n_pairs = BAND_PAIRS_PER_STEP
    w = n_pairs * LANES
    per_step = math.gcd(t // tq, BAND_BLOCKS_PER_STEP)
    kern = functools.partial(_band_prompt_kernel, tq=tq, tk=tk, n_pairs=n_pairs, blocks_per_step=per_step)
    kv_spec = pl.BlockSpec((None, w, t), lambda bi, hp, qi: (bi, hp, 0))
    q_spec = pl.BlockSpec((None, per_step * tq, w), lambda bi, hp, qi: (bi, qi, hp))
    bias_spec = pl.BlockSpec((2 * n_pairs, 3, tq, tk), lambda bi, hp, qi: (hp, 0, 0, 0))
    return pl.pallas_call(
        kern,
        grid=(b, C_WIDTH // w, t // (per_step * tq)),
        in_specs=[q_spec, kv_spec, kv_spec, bias_spec],
        out_specs=q_spec,
        out_shape=jax.ShapeDtypeStruct(q.shape, BF16),
        compiler_params=_params("parallel", "parallel", "parallel"),
        name="band_attention",
    )(q, k, v, bias)


def _band_attention_sample(q, k_new, v_new, k_cache, v_cache, layer, bias):
    b, ts, _ = q.shape
    wc = k_cache.shape[3]
    tk = ATT_BLOCK
    n_cache_blocks = min(wc // tk, 2)
    assert wc % (n_cache_blocks * tk) == 0 and ts <= tk
    n_pairs = BAND_PAIRS_PER_STEP
    w = n_pairs * LANES
    kern = functools.partial(_band_sample_kernel, ts=ts, tk=tk, n_cache_blocks=n_cache_blocks,
                             n_pairs=n_pairs)
    new_spec = pl.BlockSpec((None, ts, w), lambda bi, hp: (bi, 0, hp))
    cache_rows = n_cache_blocks * tk
    cache_spec = pl.BlockSpec((None, None, w, cache_rows),
                              lambda bi, hp: (layer, bi, hp, wc // cache_rows - 1))
    bias_spec = pl.BlockSpec((2 * n_pairs, 1 + n_cache_blocks, ts, tk), lambda bi, hp: (hp, 0, 0, 0))
    return pl.pallas_call(
        kern,
        grid=(b, C_WIDTH // w),
        in_specs=[new_spec, new_spec, new_spec, cache_spec, cache_spec, bias_spec],
        out_specs=new_spec,
        out_shape=jax.ShapeDtypeStruct(q.shape, BF16),
        scratch_shapes=[pltpu.VMEM((tk, w), BF16), pltpu.VMEM((tk, w), BF16)],
        compiler_params=_params("parallel", "parallel"),
        name="band_attention_sample",
    )(q, k_new, v_new, k_cache, v_cache, bias)


def _band_bias_kernel(g_ref, o_ref):
    rows = tk = ATT_BLOCK
    q_chunk = lax.broadcasted_iota(jnp.int32, (rows, tk), 0) // CHUNK
    k_chunk = lax.broadcasted_iota(jnp.int32, (rows, tk), 1) // CHUNK
    for dj in range(3):
        g = jnp.broadcast_to(g_ref[0, dj], (rows, 2 * tk))
        tile = pltpu.roll(g, 0, axis=1, stride=1, stride_axis=0)[:, :tk]
        diff = dj * (tk // CHUNK) + q_chunk - k_chunk
        seen = (diff >= 0) & (diff <= C_LEFT_CHUNKS)
        o_ref[0, dj] = jnp.where(seen, tile * LOG2E, NEG_BIG)


def _band_bias(rel_table):
    rows = tk = ATT_BLOCK
    c = jnp.arange(2 * tk, dtype=jnp.int32)
    u = jnp.where(c <= tk, -c, 2 * tk - c)
    idx = jnp.clip(jnp.arange(3, dtype=jnp.int32)[:, None] * tk + u[None, :], REL_MIN, REL_MAX) - REL_MIN
    g = rel_table[:, idx].astype(F32).reshape(C_HEADS, 3, 1, 2 * tk)
    return pl.pallas_call(
        _band_bias_kernel,
        grid=(C_HEADS,),
        in_specs=[pl.BlockSpec((1, 3, 1, 2 * tk), lambda h: (h, 0, 0, 0))],
        out_specs=pl.BlockSpec((1, 3, rows, tk), lambda h: (h, 0, 0, 0)),
        out_shape=jax.ShapeDtypeStruct((C_HEADS, 3, rows, tk), F32),
        compiler_params=_params("parallel"),
        name="band_bias",
    )(g)


def _gla_pair_tile(q, k, v, la, states, emit):
    L = GLA_CHUNK
    chunks = range(q.shape[0] // L)
    row = lax.broadcasted_iota(jnp.int32, (L, L), 0)
    colm = lax.broadcasted_iota(jnp.int32, (L, L), 1)
    tri = (colm <= row).astype(BF16)
    causal = colm <= row
    lane = lax.broadcasted_iota(jnp.int32, (L, LANES), 1)
    sub = lax.broadcasted_iota(jnp.int32, (LANES, B_DV), 0)
    mine = [(lane >= h * B_DK) & (lane < (h + 1) * B_DK) for h in range(2)]
    mine_rows = [(sub >= h * B_DK) & (sub < (h + 1) * B_DK) for h in range(2)]
    rows = [slice(c * L, (c + 1) * L) for c in chunks]

    b = []
    for r in rows:
        g_hi, g_lo = _split_bf16(la[r, :])
        b.append(_dot(tri, g_hi) + _dot(tri, g_lo))
    yield
    qg, qg_h, kg, kd_t, decay = [], [], [], [], []
    for c, r in zip(chunks, rows):
        qg_c = q[r, :] * (B_DK ** -0.5) * jnp.exp(b[c])
        qg.append(qg_c.astype(BF16))
        qg_h.append([jnp.where(mine[h], qg_c, 0.0).astype(BF16) for h in range(2)])
        kg.append((k[r, :] * jnp.exp(-b[c])).astype(BF16))
        b_t = b[c].T
        b_last = b_t[:, L - 1:L]
        kd_t.append((k[r, :].T * jnp.exp(b_last - b_t)).astype(BF16))
        decay.append(jnp.exp(b_last))
    yield
    att =[[jnp.where(causal, _dot_nt(qg_h[c][h], kg[c]), 0.0).astype(BF16) for h in range(2)]
           for c in chunks]
    yield
    o_intra, own = [], []
    for c, r in zip(chunks, rows):
        v_h = [v[r, h * B_DV:(h + 1) * B_DV].astype(BF16) for h in range(2)]
        o_intra.append([_dot(att[c][h], v_h[h]) for h in range(2)])
        own.append([jnp.where(mine_rows[h], _dot(kd_t[c], v_h[h]), 0.0) for h in range(2)])

    yield
    states = list(states)
    start = []
    for c in chunks:
        start.append([s.astype(BF16) for s in states])
        states = [decay[c] * states[h] + own[c][h] for h in range(2)]
    yield
    for c in chunks:
        for h in range(2):
            emit(c, h, o_intra[c][h] + _dot(qg[c], start[c][h]))
    return states


def _pair_states(s0_ref, p):
    zeros_state = jnp.zeros((B_DK, B_DV), F32)
    return [jnp.concatenate([s0_ref[2 * p], zeros_state], axis=0),
            jnp.concatenate([zeros_state, s0_ref[2 * p + 1]], axis=0)]


def _pair_v_cols(p):
    return slice(2 * p * B_DV, 2 * (p + 1) * B_DV)


def _gla_rows_emit(o_ref, p):
    def emit(c, h, o):
        o_ref[c * GLA_CHUNK:(c + 1) * GLA_CHUNK, (2 * p + h) * B_DV:(2 * p + h + 1) * B_DV] = o
    return emit


def _gla_kernel(q_ref, k_ref, v_ref, la_ref, s0_ref, o_ref, s_out_ref):
    pairs = range(B_HEADS // 2)
    states = _interleave([
        _gla_pair_tile(q_ref[:, _pair_cols(p)], k_ref[:, _pair_cols(p)], v_ref[:, _pair_v_cols(p)],
                       la_ref[:, _pair_cols(p)], _pair_states(s0_ref, p), _gla_rows_emit(o_ref, p))
        for p in pairs])
    for p in pairs:
        s_out_ref[2 * p] = states[p][0][:B_DK, :]
        s_out_ref[2 * p + 1] = states[p][1][B_DK:, :]


def _gla(q, k, v, la, s0):
    b, t, _ = q.shape
    assert t % GLA_CHUNK == 0 and t <= GLA_MAX_TILE
    qk_spec = pl.BlockSpec((None, t, B_KW), lambda bi: (bi, 0, 0))
    v_spec = pl.BlockSpec((None, t, B_VW), lambda bi: (bi, 0, 0))
    s_spec = pl.BlockSpec((None, B_HEADS, B_DK, B_DV), lambda bi: (bi, 0, 0, 0))
    return pl.pallas_call(
        _gla_kernel,
        grid=(b,),
        in_specs=[qk_spec, qk_spec, v_spec, qk_spec, s_spec],
        out_specs=[v_spec, s_spec],
        out_shape=[jax.ShapeDtypeStruct(v.shape, F32), jax.ShapeDtypeStruct(s0.shape, F32)],
        compiler_params=_params("parallel"),
        name="gla",
    )(q, k, v, la, s0)


def _pad_rows(x, n):
    return jnp.pad(x, ((0, 0), (0, n - x.shape[1]), (0, 0)))


def _heads_last(x, heads):
    n, b, _, s = x.shape
    return jnp.transpose(x.reshape(n, b, heads, HEAD_DIM, s), (0, 1, 4, 2, 3))


def _feature_major(cache):
    n, b, s, heads, hd = cache.shape
    return jnp.transpose(cache, (0, 1, 3, 4, 2)).reshape(n, b, heads * hd, s)


def _row_tile(m):
    for tm in (512, 256, 128, 64, 32, 16, 8):
        if m % tm == 0:
            return tm
    raise ValueError(f"token count {m} is not a multiple of 8")


def kernel(x_prompt, x_sample, cache_a_k, cache_a_v, state_b, cache_c_k, cache_c_v, norm_mix_g, norm_ffn_g, w_in_ab, w_gate_b, b_gate_b, norm_gla_g, w_out_ab, w_qkv_c, rel_bias_c, w_out_c, w_ffn_gate, w_ffn_up, w_ffn_down, norm_final_g):
    bp, tp, d = x_prompt.shape
    bs, ts, _ = x_sample.shape
    depth = norm_mix_g.shape[0]
    past = cache_a_k.shape[2]
    wc = cache_c_k.shape[2]
    assert tp % ATT_BLOCK == 0 and past % ATT_BLOCK == 0 and wc % ATT_BLOCK == 0
    assert ts <= GLA_CHUNK and ts % 8 == 0
    mp, ms = bp * tp, bs * ts
    tmp, tms = _row_tile(tp), _row_tile(ms)
    xp = x_prompt.reshape(mp, d)
    xs = x_sample.reshape(ms, d)
    row2 = lambda a: a.reshape(1, -1)

    a_ks, a_vs, b_sp, b_ss, c_ks, c_vs = [], [], [], [], [], []
    a_kv_prompt = c_kv_prompt = None
    n_ab, n_c = (depth + 1) // 2, depth // 2
    keep = min(C_LEFT_CHUNKS * CHUNK, tp)

    kv0, kv1 = A_WIDTH, 3 * A_WIDTH
    o = 3 * A_WIDTH + 2 * B_KW + B_VW
    w_in_t = jnp.swapaxes(w_in_ab, 1, 2)
    w_main_all = jnp.concatenate([w_in_t[:, :kv0], w_in_t[:, kv1:o], w_in_t[:, o + B_GATE_RANK:]],
                                 axis=1).astype(BF16)
    w_kv_ab_t_all = w_in_t[:, kv0:kv1].astype(BF16)
    w_lr_all = jnp.pad(w_in_t[:, o:o + B_GATE_RANK],
                       ((0, 0), (0, LANES - B_GATE_RANK), (0, 0))).astype(BF16)
    w_gate_all = jnp.pad(w_gate_b, ((0, 0), (0, LANES - B_GATE_RANK), (0, 0))).astype(BF16)
    w_out_ab_all = w_out_ab.astype(BF16)
    w_q_all = w_qkv_c[:, :, :C_WIDTH].astype(BF16)
    w_kv_c_t_all = jnp.swapaxes(w_qkv_c[:, :, C_WIDTH:], 1, 2).astype(BF16)
    w_out_c_all = w_out_c.astype(BF16)
    cache_a_k_fm, cache_a_v_fm, cache_c_k_fm, cache_c_v_fm = (
        _feature_major(c) for c in (cache_a_k, cache_a_v, cache_c_k, cache_c_v))

    for layer in range(depth):
        i = layer // 2
        g_mix = row2(norm_mix_g[layer])
        riders = _Riders((w_ffn_gate, w_ffn_up, w_ffn_down), layer, mp // tmp)
        ffn_ends = (row2(norm_ffn_g[layer]), row2(norm_final_g) if layer == depth - 1 else None)
        if layer % 2 == 0:
            w_main, w_kv_t, w_lr, w_gate, w_out = (
                _Slab(w, i) for w in (w_main_all, w_kv_ab_t_all, w_lr_all, w_gate_all, w_out_ab_all))
            b_gate = row2(b_gate_b[i])
            g_gla = row2(norm_gla_g[i])

            qa, ka, va, kab, vab, qb, kb, vb, r, la, *ffn_w = _proj_ab(
                xp, g_mix, w_main, w_kv_t, w_lr, w_gate, b_gate, tmp, batch=bp,
                stack=_KVStack(i, n_ab, tp, a_kv_prompt), riders=riders)
            ffn = (ffn_ends[0], *ffn_w, ffn_ends[1])
            a_kv_prompt = (ka, va)
            sh = lambda a: a.reshape(bp, tp, -1)
            oa, ob, sbp = _mixer_ab_prompt(sh(qa), kab, vab, sh(qb), sh(kb), sh(vb), sh(la))
            mixed_p = (oa.reshape(mp, -1), ob.reshape(mp, -1), r)
            b_sp.append(sbp)

            qa, ka, va, kab, vab, qb, kb, vb, r, la = _proj_ab(
                xs, g_mix, w_main, w_kv_t, w_lr, w_gate, b_gate, tms)
            sh = lambda a: a.reshape(bs, ts, -1)
            oa = _sb_attention_sample(sh(qa), sh(kab), sh(vab), cache_a_k_fm, cache_a_v_fm, i)
            pad_t = lambda a: _pad_rows(sh(a), GLA_CHUNK)
            ob, sbs = _gla(pad_t(qb), pad_t(kb), pad_t(vb), pad_t(la), state_b[i])
            mixed_s = (oa.reshape(ms, -1), ob[:, :ts].reshape(ms, -1), r)
            xp, xs = _layer_tail(xp, mixed_p, xs, mixed_s, (g_gla, w_out), *ffn, tmp)
            a_ks.append(ka.reshape(bs, ts, A_HEADS, HEAD_DIM))
            a_vs.append(va.reshape(bs, ts, A_HEADS, HEAD_DIM))
            b_ss.append(sbs)
        else:
            w_q, w_kv_t, w_out = (_Slab(w, i) for w in (w_q_all, w_kv_c_t_all, w_out_c_all))

            q, k, v, kb16, vb16, *ffn_w = _proj_c(xp, g_mix, w_q, w_kv_t, tmp, batch=bp,
                                                  stack=_KVStack(i, n_c, keep, c_kv_prompt), riders=riders)
            ffn = (ffn_ends[0], *ffn_w, ffn_ends[1])
            c_kv_prompt = (k, v)
            bias = _band_bias(rel_bias_c[i])
            oc = _band_attention_prompt(q.reshape(bp, tp, -1), kb16, vb16, bias)
            mixed_p = (oc.reshape(mp, -1),)

            q, k, v, kb16, vb16 = _proj_c(xs, g_mix, w_q, w_kv_t, tms)
            sh = lambda a: a.reshape(bs, ts, -1)
            oc = _band_attention_sample(sh(q), sh(kb16), sh(vb16), cache_c_k_fm, cache_c_v_fm, i, bias)
            xp, xs = _layer_tail(xp, mixed_p, xs, (oc.reshape(ms, -1),), (w_out,), *ffn, tmp)
            c_ks.append(k.reshape(bs, ts, C_HEADS, HEAD_DIM))
            c_vs.append(v.reshape(bs, ts, C_HEADS, HEAD_DIM))

    y_prompt = xp.reshape(bp, tp, d)
    y_sample = xs.reshape(bs, ts, d)
    a_kp, a_vp = (_heads_last(a, A_HEADS) for a in a_kv_prompt)
    c_kp, c_vp = (_heads_last(a, C_HEADS) for a in c_kv_prompt)
    return (y_prompt, y_sample, a_kp, a_vp, jnp.stack(a_ks), jnp.stack(a_vs),
            jnp.stack(b_sp), jnp.stack(b_ss), c_kp, c_vp, jnp.stack(c_ks), jnp.stack(c_vs))
```

```python
import functools
import math

import jax
import jax.numpy as jnp
from jax import lax
from jax.experimental import pallas as pl
from jax.experimental.pallas import tpu as pltpu

F32 = jnp.float32
BF16 = jnp.bfloat16

EPS = 1e-6
HEAD_DIM = 64
LANES = 128
A_HEADS = 8
A_WIDTH = A_HEADS * HEAD_DIM
B_HEADS = 4
B_DK = 64
B_DV = 128
B_KW = B_HEADS * B_DK
B_VW = B_HEADS * B_DV
B_GATE_RANK = 16
B_GATE_TEMP = 16.0
GLA_CHUNK = 64
C_HEADS = 16
C_WIDTH = C_HEADS * HEAD_DIM
CHUNK = 64
C_LEFT_CHUNKS = 8
REL_MIN = -(CHUNK - 1)
REL_MAX = 128
ATT_BLOCK = 256
NEG_BIG = -1e30
LOG2E = 1.4426950408889634
SB_DEAD = 104.0
SB_PAIRS_PER_STEP = 4
BAND_PAIRS_PER_STEP = 4
MIXER_BLOCKS_PER_STEP = 4
BAND_BLOCKS_PER_STEP = 8
GLA_MAX_TILE = 512
VMEM_LIMIT = 56 * 1024 * 1024


def _params(*sem):
    return pltpu.CompilerParams(dimension_semantics=sem, vmem_limit_bytes=VMEM_LIMIT)


class _Slab:
    def __init__(self, stacked, index):
        self.stacked, self.index, self.shape = stacked, index, stacked.shape[1:]


def _operand(a):
    return a.stacked if isinstance(a, _Slab) else a


def _resident(a):
    if isinstance(a, _Slab):
        index = (a.index,) + (0,) * len(a.shape)
        return pl.BlockSpec((None, *a.shape), lambda *_: index, pipeline_mode=pl.Buffered(1))
    return pl.BlockSpec(a.shape, lambda *_: (0,) * a.ndim, pipeline_mode=pl.Buffered(1))


def _rms(x, g):
    return x * lax.rsqrt(jnp.mean(x * x, axis=-1, keepdims=True) + EPS) * g


def _log_sigmoid_pair(z):
    l = jnp.log1p(jnp.exp(-jnp.abs(z)))
    return jnp.minimum(z, 0.0) - l, jnp.minimum(-z, 0.0) - l


def _split_bf16(x):
    hi = x.astype(BF16)
    lo = (x - hi.astype(F32)).astype(BF16)
    return hi, lo


def _dot(a, b):
    return jnp.dot(a, b, preferred_element_type=F32)


def _dot_nt(a, b):
    return lax.dot_general(a, b, (((1,), (1,)), ((), ())), preferred_element_type=F32)


def _dot_tn(a, b):
    return lax.dot_general(a, b, (((0,), (0,)), ((), ())), preferred_element_type=F32)


def _pair_cols(p):
    return slice(p * LANES, (p + 1) * LANES)


class _KV:
    def __init__(self, k, v, feature_major):
        self.k, self.v, self.feature_major = k, v, feature_major
        self.n_keys = k.shape[1] if feature_major else k.shape[0]

    def scores(self, q_h, p):
        if self.feature_major:
            return _dot(q_h, self.k[_pair_cols(p), :])
        return _dot_nt(q_h, self.k[:, _pair_cols(p)])

    def first(self, n):
        if self.feature_major:
            return _KV(self.k[:, :n], self.v[:, :n], True)
        return _KV(self.k[:n], self.v[:n], False)

    def last(self, n):
        if self.feature_major:
            return _KV(self.k[:, -n:], self.v[:, -n:], True)
        return _KV(self.k[-n:], self.v[-n:], False)

    def values(self, p):
        return self.v[_pair_cols(p), :] if self.feature_major else self.v[:, _pair_cols(p)]

    def weighted(self, w, v_p):
        return _dot_nt(w, v_p) if self.feature_major else _dot(w, v_p)

    def head_lanes(self):
        shape = (LANES, self.n_keys) if self.feature_major else (self.n_keys, LANES)
        return lax.broadcasted_iota(jnp.int32, shape, 0 if self.feature_major else 1) < HEAD_DIM


def _split_heads(q_ref, n_pairs, tq):
    lo = lax.broadcasted_iota(jnp.int32, (tq, LANES), 1) < HEAD_DIM
    heads = []
    for p in range(n_pairs):
        q = q_ref[:, _pair_cols(p)]
        heads += [jnp.where(lo, q, jnp.zeros_like(q)), jnp.where(lo, jnp.zeros_like(q), q)]
    return heads, lo


def _emit_kv(y, wkv_ref, k_ref, v_ref, kb_ref, vb_ref, feature_major, kept_tail=None, own_slot=0):
    if not feature_major:
        kv = _dot_nt(y, wkv_ref[...])
        width = kv.shape[1] // 2
        k, v = kv[:, :width], kv[:, width:]
        k_ref[...], v_ref[...], kb_ref[...], vb_ref[...] = k, v, k.astype(BF16), v.astype(BF16)
        return
    kv = _dot_nt(wkv_ref[...], y)
    width = kv.shape[0] // 2
    k, v = kv[:width, :], kv[width:, :]
    kb_ref[...] = k.astype(BF16)
    vb_ref[...] = v.astype(BF16)
    slot = own_slot
    other_slots = [s for s in range(k_ref.shape[0]) if s != slot]
    if kept_tail is None:
        fill, fill_cols = slice(None), k_ref.shape[2]
        k_ref[slot] = k
        v_ref[slot] = v
    else:
        n_tiles, n_kept = kept_tail
        tile = pl.program_id(0) % n_tiles
        tm = k.shape[1]
        cols = pl.ds(pl.multiple_of(jnp.maximum(tile - (n_tiles - n_kept), 0) * tm, tm), tm)
        k_ref[slot, :, cols] = k
        v_ref[slot, :, cols] = v
        share = k_ref.shape[2] // n_tiles
        if share % LANES == 0 and share * n_tiles == k_ref.shape[2]:
            fill, fill_cols = pl.ds(pl.multiple_of(tile * share, share), share), share
        else:
            fill, fill_cols = slice(None), k_ref.shape[2]
    for dst in (k_ref, v_ref):
        for s in other_slots:
            dst[s, :, fill] = jnp.zeros((dst.shape[1], fill_cols), dst.dtype)


class _Riders:
    def __init__(self, stacks, layer, steps):
        self.stacks, self.layer = list(stacks), layer
        self.chunks = next(c for c in (16, 8, 4, 2, 1)
                           if c <= steps and all(w.shape[1] % (16 * c) == 0 for w in self.stacks))

    def specs(self):
        layer, last = self.layer, self.chunks - 1
        ins = [pl.BlockSpec((None, w.shape[1] // self.chunks, w.shape[2]),
                            lambda i: (layer, jnp.minimum(i, last), 0)) for w in self.stacks]
        outs = [pl.BlockSpec((w.shape[1] // self.chunks, w.shape[2]),
                             lambda i: (jnp.minimum(i, last), 0)) for w in self.stacks]
        shapes = [jax.ShapeDtypeStruct(w.shape[1:], BF16) for w in self.stacks]
        return ins, outs, shapes


def _split_extra_refs(refs, n_aliased, n_riders):
    rider_in = refs[n_aliased:n_aliased + n_riders]
    outs = refs[n_aliased + n_riders:len(refs) - n_riders]
    return outs, list(zip(rider_in, refs[len(refs) - n_riders:]))


def _cast_riders(pairs):
    for src, dst in pairs:
        dst[...] = src[...].astype(BF16)


def _proj_ab_kernel(x_ref, g_ref, w_ref, wkv_ref, wlr_ref, wgate_ref, bgate_ref, *refs,
                    feature_major, n_aliased, n_riders, own_slot):
    outs, riders = _split_extra_refs(refs, n_aliased, n_riders)
    qa_ref, ka_ref, va_ref, kab_ref, vab_ref, qb_ref, kb_ref, vb_ref, r_ref, la_ref = outs
    _cast_riders(riders)
    y = _rms(x_ref[...], g_ref[...]).astype(BF16)
    _emit_kv(y, wkv_ref, ka_ref, va_ref, kab_ref, vab_ref, feature_major, own_slot=own_slot)
    z = _dot_nt(y, w_ref[...])
    c = 0
    qa_ref[...] = (z[:, c:c + A_WIDTH] * (HEAD_DIM ** -0.5)).astype(BF16); c += A_WIDTH
    qb_ref[...] = z[:, c:c + B_KW]; c += B_KW
    kb_ref[...] = z[:, c:c + B_KW]; c += B_KW
    vb_ref[...] = z[:, c:c + B_VW]; c += B_VW
    r_ref[...] = z[:, c:c + B_VW]
    g_lr = _dot_nt(y, wlr_ref[...])
    gate = _dot(g_lr.astype(BF16), wgate_ref[...]) + bgate_ref[...]
    la_ref[...] = _log_sigmoid_pair(gate)[0] * (1.0 / B_GATE_TEMP)


class _KVStack:
    def __init__(self, layer, n_layers, keep, previous=None):
        self.layer, self.n_layers, self.keep, self.previous = layer, n_layers, keep, previous


def _kv_out(m, width, tm, batch, stack):
    if batch is None:
        spec = pl.BlockSpec((tm, width), lambda i: (i, 0))
        return [spec] * 4, [jax.ShapeDtypeStruct((m, width), dt) for dt in (F32, F32, BF16, BF16)], None, 0
    t = m // batch
    assert t % tm == 0 and stack.keep % tm == 0 and stack.keep <= t
    n_tiles = t // tm
    copy_spec = pl.BlockSpec((None, width, tm), lambda i: (i // n_tiles, 0, i % n_tiles))
    copy_shape = jax.ShapeDtypeStruct((batch, width, t), BF16)
    slots, first_slot, own_slot = ((stack.n_layers, 0, stack.layer) if stack.previous is None
                                   else (1, stack.layer, 0))
    if stack.keep == t:
        f32_spec = pl.BlockSpec((slots, None, width, tm), lambda i: (first_slot, i // n_tiles, 0, i % n_tiles))
        kept_tail = None
    else:
        f32_spec = pl.BlockSpec((slots, None, width, stack.keep), lambda i: (first_slot, i // n_tiles, 0, 0))
        kept_tail = (n_tiles, stack.keep // tm)
    f32_shape = jax.ShapeDtypeStruct((stack.n_layers, batch, width, stack.keep), F32)
    return ([f32_spec, f32_spec, copy_spec, copy_spec], [f32_shape, f32_shape, copy_shape, copy_shape],
            kept_tail, own_slot)


def _aliased_stack(stack, n_inputs, first_output):
    if stack is None or stack.previous is None:
        return [], [], {}
    prev = list(stack.previous)
    specs = [pl.BlockSpec(memory_space=pl.ANY)] * len(prev)
    return prev, specs, {n_inputs + j: first_output + j for j in range(len(prev))}


def _rider_plumbing(riders):
    if riders is None:
        return [], [], [], []
    return (riders.stacks, *riders.specs())


def _proj_ab(x, g, w_main, w_kv, w_lr, w_gate, b_gate, tm, batch=None, stack=None, riders=None):
    m, d = x.shape
    row = lambda n: pl.BlockSpec((tm, n), lambda i: (i, 0))
    params = (g, w_main, w_kv, w_lr, w_gate, b_gate)
    kv_specs, kv_shapes, kept_tail, own_slot = _kv_out(m, A_WIDTH, tm, batch, stack)
    assert kept_tail is None
    prev, prev_specs, aliases = _aliased_stack(stack, 1 + len(params), 1)
    ride_ops, ride_in, ride_out, ride_shapes = _rider_plumbing(riders)
    rest = [(B_KW, F32), (B_KW, F32), (B_VW, F32), (B_VW, F32), (B_KW, F32)]
    return pl.pallas_call(
        functools.partial(_proj_ab_kernel, feature_major=batch is not None, n_aliased=len(prev),
                          n_riders=len(ride_ops), own_slot=own_slot),
        grid=(m // tm,),
        in_specs=[row(d)] + [_resident(a) for a in params] + prev_specs + ride_in,
        out_specs=[row(A_WIDTH)] + kv_specs + [row(n) for n, _ in rest] + ride_out,
        out_shape=([jax.ShapeDtypeStruct((m, A_WIDTH), BF16)] + kv_shapes
                   + [jax.ShapeDtypeStruct((m, n), dt) for n, dt in rest] + ride_shapes),
        input_output_aliases=aliases,
        compiler_params=_params("arbitrary" if ride_ops else "parallel"),
        name="proj_ab",
    )(x, *[_operand(a) for a in params], *prev, *ride_ops)


def _proj_c_kernel(x_ref, g_ref, wq_ref, wkv_ref, *refs, feature_major, n_aliased, kept_tail,
                   n_riders, own_slot):
    outs, riders = _split_extra_refs(refs, n_aliased, n_riders)
    q_ref, k_ref, v_ref, kb_ref, vb_ref = outs
    _cast_riders(riders)
    y = _rms(x_ref[...], g_ref[...]).astype(BF16)
    _emit_kv(y, wkv_ref, k_ref, v_ref, kb_ref, vb_ref, feature_major, kept_tail, own_slot)
    q_ref[...] = (_dot(y, wq_ref[...]) * (HEAD_DIM ** -0.5 * LOG2E)).astype(BF16)


def _proj_c(x, g, w_q, w_kv, tm, batch=None, stack=None, riders=None):
    m, d = x.shape
    row = lambda n: pl.BlockSpec((tm, n), lambda i: (i, 0))
    params = (g, w_q, w_kv)
    kv_specs, kv_shapes, kept_tail, own_slot = _kv_out(m, C_WIDTH, tm, batch, stack)
    prev, prev_specs, aliases = _aliased_stack(stack, 1 + len(params), 1)
    ride_ops, ride_in, ride_out, ride_shapes = _rider_plumbing(riders)
    return pl.pallas_call(
        functools.partial(_proj_c_kernel, feature_major=batch is not None, n_aliased=len(prev),
                          kept_tail=kept_tail, n_riders=len(ride_ops), own_slot=own_slot),
        grid=(m // tm,),
        in_specs=[row(d)] + [_resident(a) for a in params] + prev_specs + ride_in,
        out_specs=[row(C_WIDTH)] + kv_specs + ride_out,
        out_shape=[jax.ShapeDtypeStruct((m, C_WIDTH), BF16)] + kv_shapes + ride_shapes,
        input_output_aliases=aliases,
        compiler_params=_params("arbitrary" if (kept_tail or ride_ops) else "parallel"),
        name="proj_c",
    )(x, *[_operand(a) for a in params], *prev, *ride_ops)


def _layer_tail_rows(x_ref, mixer_refs, param_refs, o_ref, gla_merge, final_norm):
    if gla_merge:
        oa_ref, ob_ref, r_ref = mixer_refs
        ggla_ref, wout_ref, *param_refs = param_refs
        ob = ob_ref[...]
        parts = []
        for h in range(B_HEADS):
            seg = ob[:, h * B_DV:(h + 1) * B_DV]
            parts.append(seg * lax.rsqrt(jnp.mean(seg * seg, axis=-1, keepdims=True) + EPS))
        r = r_ref[...]
        obn = jnp.concatenate(parts, axis=-1) * ggla_ref[...] * (r * jax.nn.sigmoid(r))
        mix = _dot(oa_ref[...], wout_ref[:A_WIDTH, :]) + _dot(obn.astype(BF16), wout_ref[A_WIDTH:, :])
    else:
        (oc_ref,) = mixer_refs
        wout_ref, *param_refs = param_refs
        mix = _dot(oc_ref[...], wout_ref[...])
    gffn_ref, wg_ref, wu_ref, wd_ref, *param_refs = param_refs
    x = x_ref[...] + mix
    y = _rms(x, gffn_ref[...]).astype(BF16)
    h = _dot(y, wg_ref[...])
    u = _dot(y, wu_ref[...])
    a = (h * jax.nn.sigmoid(h) * u).astype(BF16)
    x = x + _dot(a, wd_ref[...])
    o_ref[...] = _rms(x, param_refs[0][...]) if final_norm else x


def _layer_tail_kernel(*refs, n_mixer, n_params, gla_merge, final_norm, main_steps):
    main, extra = refs[:1 + n_mixer], refs[1 + n_mixer:2 + 2 * n_mixer]
    params = refs[2 + 2 * n_mixer:2 + 2 * n_mixer + n_params]
    o_ref, o_extra_ref = refs[-2:]

    @pl.when(pl.program_id(0) < main_steps)
    def _():
        _layer_tail_rows(main[0], main[1:], params, o_ref, gla_merge, final_norm)

    @pl.when(pl.program_id(0) == main_steps)
    def _():
        _layer_tail_rows(extra[0], extra[1:], params, o_extra_ref, gla_merge, final_norm)


def _layer_tail(x, mixer_out, x_extra, mixer_out_extra, mixer_params, g_ffn, wg, wu, wd, g_fin, tm):
    m, d = x.shape
    main_steps = m // tm
    row = lambda a: pl.BlockSpec((tm, a.shape[1]), lambda i: (jnp.minimum(i, main_steps - 1), 0))
    whole = lambda a: pl.BlockSpec(a.shape, lambda i: (0, 0))
    params = [*mixer_params, g_ffn, wg, wu, wd] + ([] if g_fin is None else [g_fin])
    return pl.pallas_call(
        functools.partial(_layer_tail_kernel, n_mixer=len(mixer_out), n_params=len(params),
                          gla_merge=len(mixer_out) == 3, final_norm=g_fin is not None, main_steps=main_steps),
        grid=(main_steps + 1,),
        in_specs=([row(x)] + [row(a) for a in mixer_out] + [whole(x_extra)]
                  + [whole(a) for a in mixer_out_extra] + [_resident(a) for a in params]),
        out_specs=[row(x), whole(x_extra)],
        out_shape=[jax.ShapeDtypeStruct((m, d), F32), jax.ShapeDtypeStruct(x_extra.shape, F32)],
        compiler_params=_params("arbitrary"),
        name="layer_tail",
    )(x, *mixer_out, x_extra, *mixer_out_extra, *[_operand(a) for a in params])


def _interleave(stage_generators):
    results = [None] * len(stage_generators)
    live = list(range(len(stage_generators)))
    while live:
        for i in list(live):
            try:
                next(stage_generators[i])
            except StopIteration as done:
                results[i] = done.value
                live.remove(i)
    return results


def _sb_core(q_heads, lo, first_kv, first_mask, earlier_kv, n_earlier, acc_ref, c_ref, o_ref, n_pairs,
             companions=(), on_companions=None):
    tk = first_kv.n_keys
    heads = range(2 * n_pairs)
    later = (lax.broadcasted_iota(jnp.int32, (tk, tk), 0)
             > lax.broadcasted_iota(jnp.int32, (tk, tk), 1)).astype(BF16)
    c_ref[...] = jnp.zeros_like(c_ref)

    tq = q_heads[0].shape[0]

    def row_groups(kv, mask):
        if mask is None or tq != tk or tq % (2 * LANES):
            return [(slice(None), kv, mask, later)]
        half = tq // 2
        return [(slice(0, half), kv.first(half), mask[:half, :half], later[:half, :half]),
                (slice(half, tq), kv, mask[half:, :], later)]

    def block(kv, mask):
        groups = row_groups(kv, mask)
        chains = [(h, g) for h in heads for g in range(len(groups))]
        z = [groups[g][1].scores(q_heads[h][groups[g][0], :], h // 2) for h, g in chains]
        yield
        log_beta, drop, after = [], [], []
        for i, (h, g) in enumerate(chains):
            d = jnp.maximum(z[i], 0.0) + jnp.log(1.0 + jnp.exp(-jnp.abs(z[i])))
            log_beta.append(z[i] - d)
            drop.append(d if mask is None else jnp.where(groups[g][2], d, 0.0))
        yield
        for i, (h, g) in enumerate(chains):
            hi, lo_part = _split_bf16(drop[i])
            tri = groups[g][3]
            after.append(_dot(hi, tri) + _dot(lo_part, tri))
        yield
        pv = {}
        for i, (h, g) in enumerate(chains):
            rows, kv_g = groups[g][0], groups[g][1]
            c = c_ref[h, rows, :]
            w = jnp.exp(log_beta[i] - after[i] - c)
            if mask is not None:
                w = jnp.where(groups[g][2], w, 0.0)
            c_ref[h, rows, :] = c + after[i][:, 0:1] + drop[i][:, 0:1]
            pv[h, g] = kv_g.weighted(w.astype(BF16), kv_g.values(h // 2))
        yield
        per_head = [pv[h, 0] if len(groups) == 1 else jnp.concatenate([pv[h, g] for g in range(len(groups))], 0)
                    for h in heads]
        out = [jnp.where(lo, per_head[2 * p], per_head[2 * p + 1]) for p in range(n_pairs)]
        return out[0] if n_pairs == 1 else jnp.concatenate(out, axis=-1)

    def all_dead():
        return jnp.min(c_ref[...]) > SB_DEAD

    first, *companion_results = _interleave([block(first_kv, first_mask), *companions])
    acc_ref[...] = first
    if companions:
        on_companions(companion_results)

    def cond(carry):
        n, dead = carry
        return (n < n_earlier) & jnp.logical_not(dead)

    def body(carry):
        n, _ = carry
        acc_ref[...] += _interleave([block(earlier_kv(n), None)])[0]
        return n + 1, all_dead()

    lax.while_loop(cond, body, (jnp.int32(0), jnp.asarray(False)))
    o_ref[...] = acc_ref[...].astype(o_ref.dtype)


def _mixer_ab_prompt_kernel(q_ref, k_ref, v_ref, qb_ref, kb_ref, vb_ref, la_ref, o_ref, ob_ref, s_out_ref,
                            acc_ref, c_ref, carry_ref, *, tb, n_pairs, blocks_per_step):
    pairs = range(B_HEADS // 2)
    strictly_earlier = (lax.broadcasted_iota(jnp.int32, (tb, tb), 1)
                        < lax.broadcasted_iota(jnp.int32, (tb, tb), 0))

    def kv_block(j):
        keys = pl.ds(pl.multiple_of(j * tb, tb), tb)
        return _KV(k_ref[:, keys], v_ref[:, keys], True)

    def keep_states(states):
        for p in pairs:
            carry_ref[2 * p], carry_ref[2 * p + 1] = states[p]
            s_out_ref[2 * p] = states[p][0][:B_DK, :]
            s_out_ref[2 * p + 1] = states[p][1][B_DK:, :]

    @pl.when(pl.program_id(1) == 0)
    def _():
        carry_ref[...] = jnp.zeros_like(carry_ref)

    def one_block(s, carry):
        qi = pl.program_id(1) * blocks_per_step + s
        rows = pl.ds(pl.multiple_of(s * tb, tb), tb)
        gla = [_gla_pair_tile(qb_ref[rows, _pair_cols(p)], kb_ref[rows, _pair_cols(p)],
                              vb_ref[rows, _pair_v_cols(p)], la_ref[rows, _pair_cols(p)],
                              [carry_ref[2 * p], carry_ref[2 * p + 1]],
                              _gla_rows_emit(ob_ref.at[rows, :], p)) for p in pairs]
        q_heads, lo = _split_heads(q_ref.at[rows, :], n_pairs, tb)
        _sb_core(q_heads, lo, kv_block(qi), strictly_earlier, lambda n: kv_block(qi - 1 - n), qi,
                 acc_ref, c_ref, o_ref.at[rows, :], n_pairs, companions=gla, on_companions=keep_states)
        return carry

    lax.fori_loop(0, blocks_per_step, one_block, 0)


def _sb_sample_kernel(q_ref, kn_ref, vn_ref, kc_ref, vc_ref, o_ref, acc_ref, c_ref, kpad_ref, vpad_ref,
                      *, ts, tk, n_cache_blocks, n_pairs):
    q_heads, lo = _split_heads(q_ref, n_pairs, ts)
    kpad_ref[...] = jnp.zeros_like(kpad_ref)
    vpad_ref[...] = jnp.zeros_like(vpad_ref)
    kpad_ref[:ts, :] = kn_ref[...]
    vpad_ref[:ts, :] = vn_ref[...]

    def cache_block(n):
        keys = pl.ds(pl.multiple_of((n_cache_blocks - 1 - n) * tk, tk), tk)
        return _KV(kc_ref[:, keys].astype(BF16), vc_ref[:, keys].astype(BF16), True)

    strictly_earlier = (lax.broadcasted_iota(jnp.int32, (ts, tk), 1)
                        < lax.broadcasted_iota(jnp.int32, (ts, tk), 0))
    _sb_core(q_heads, lo, _KV(kpad_ref[...], vpad_ref[...], False), strictly_earlier, cache_block,
             n_cache_blocks, acc_ref, c_ref, o_ref, n_pairs)


def _mixer_ab_prompt(q, k, v, qb, kb, vb, la):
    b, t, _ = q.shape
    tb = ATT_BLOCK
    assert t % tb == 0 and tb % GLA_CHUNK == 0 and tb <= GLA_MAX_TILE
    n_pairs = A_WIDTH // LANES
    per_step = math.gcd(t // tb, MIXER_BLOCKS_PER_STEP)
    rows = lambda width: pl.BlockSpec((None, per_step * tb, width), lambda bi, qi: (bi, qi, 0))
    kv_spec = pl.BlockSpec((None, A_WIDTH, t), lambda bi, qi: (bi, 0, 0))
    state = (B_HEADS, B_DK, B_DV)
    return pl.pallas_call(
        functools.partial(_mixer_ab_prompt_kernel, tb=tb, n_pairs=n_pairs, blocks_per_step=per_step),
        grid=(b, t // (per_step * tb)),
        in_specs=[rows(A_WIDTH), kv_spec, kv_spec, rows(B_KW), rows(B_KW), rows(B_VW), rows(B_KW)],
        out_specs=[rows(A_WIDTH), rows(B_VW), pl.BlockSpec((None, *state), lambda bi, qi: (bi, 0, 0, 0))],
        out_shape=[jax.ShapeDtypeStruct(q.shape, BF16), jax.ShapeDtypeStruct(vb.shape, F32),
                   jax.ShapeDtypeStruct((b, *state), F32)],
        scratch_shapes=[pltpu.VMEM((tb, A_WIDTH), F32), pltpu.VMEM((2 * n_pairs, tb, 1), F32),
                        pltpu.VMEM((B_HEADS, LANES, B_DV), F32)],
        compiler_params=_params("parallel", "arbitrary"),
        name="mixer_ab",
    )(q, k, v, qb, kb, vb, la)


def _sb_attention_sample(q, k_new, v_new, k_cache, v_cache, layer):
    b, ts, _ = q.shape
    past = k_cache.shape[3]
    tk = ATT_BLOCK
    assert past % tk == 0 and ts <= tk
    n_pairs = SB_PAIRS_PER_STEP
    w = n_pairs * LANES
    new_spec = pl.BlockSpec((None, ts, w), lambda bi, hp: (bi, 0, hp))
    cache_spec = pl.BlockSpec((None, None, w, past), lambda bi, hp: (layer, bi, hp, 0))
    return pl.pallas_call(
        functools.partial(_sb_sample_kernel, ts=ts, tk=tk, n_cache_blocks=past // tk, n_pairs=n_pairs),
        grid=(b, A_WIDTH // w),
        in_specs=[new_spec, new_spec, new_spec, cache_spec, cache_spec],
        out_specs=new_spec,
        out_shape=jax.ShapeDtypeStruct(q.shape, BF16),
        scratch_shapes=[pltpu.VMEM((ts, w), F32), pltpu.VMEM((2 * n_pairs, ts, 1), F32),
                        pltpu.VMEM((tk, w), BF16), pltpu.VMEM((tk, w), BF16)],
        compiler_params=_params("parallel", "parallel"),
        name="sb_attention_sample",
    )(q, k_new, v_new, k_cache, v_cache)


def _band_core(q_heads, lo, kv, pens, bias, o_ref, n_pairs, skip_hidden_quarters=False):
    heads = range(2 * n_pairs)
    tq = q_heads[0].shape[0]
    everything = [(slice(None), [(kv[i], i, slice(None)) for i in range(len(kv))])]
    groups = _band_quarters(kv, tq) if skip_hidden_quarters else everything
    chains = [(h, g) for h in heads for g in range(len(groups))]
    z = [[piece.scores(q_heads[h][groups[g][0], :], h // 2) + bias(h, i)[groups[g][0], cols]
          for piece, i, cols in groups[g][1]] for h, g in chains]
    acc = {}
    for c, (h, g) in enumerate(chains):
        pieces = groups[g][1]
        m = None
        for j, (_, i, _) in enumerate(pieces):
            mi = jnp.max(z[c][j], axis=-1, keepdims=True)
            if pens[i] is not None:
                mi = mi + pens[i]
            m = mi if m is None else jnp.maximum(m, mi)
        a = None
        for j, (piece, i, _) in enumerate(pieces):
            shift = m if pens[i] is None else m - pens[i]
            p = jnp.exp2(z[c][j] - shift).astype(BF16)
            v = piece.values(h // 2)
            ones = jnp.ones_like(v)
            first = piece.head_lanes()
            v = jnp.where(first, v, ones) if h % 2 == 0 else jnp.where(first, ones, v)
            pv = piece.weighted(p, v)
            a = pv if a is None else a + pv
        acc[h, g] = a
    per_head = [acc[h, 0] if len(groups) == 1 else jnp.concatenate([acc[h, g] for g in range(len(groups))], 0)
                for h in heads]
    for p in range(n_pairs):
        a0, a1 = per_head[2 * p], per_head[2 * p + 1]
        o_ref[:, _pair_cols(p)] = jnp.where(lo, a0 / pltpu.roll(a0, HEAD_DIM, axis=1),
                                            a1 / pltpu.roll(a1, HEAD_DIM, axis=1)).astype(o_ref.dtype)


def _band_quarters(kv, tq):
    own, back1, back2 = kv
    half = tq // 2
    assert own.n_keys == tq and half % CHUNK == 0 and 2 * (tq // CHUNK) >= C_LEFT_CHUNKS
    whole = slice(None)
    return [(slice(0, half), [(own.first(half), 0, slice(0, half)), (back1, 1, whole), (back2, 2, whole)]),
            (slice(half, tq), [(own, 0, whole), (back1, 1, whole), (back2.last(half), 2, slice(half, tq))])]


def _band_prompt_kernel(q_ref, k_ref, v_ref, bias_ref, o_ref, *, tq, tk, n_pairs, blocks_per_step):
    def one_block(s, carry):
        qi = pl.program_id(2) * blocks_per_step + s
        rows = pl.ds(pl.multiple_of(s * tq, tq), tq)
        q_heads, lo = _split_heads(q_ref.at[rows, :], n_pairs, tq)
        kv, pens = [], []
        for dj in range(3):
            j = qi - dj
            pens.append(None if dj == 0 else jnp.where(j >= 0, 0.0, NEG_BIG).astype(F32))
            keys = pl.ds(pl.multiple_of(jnp.maximum(j, 0) * tk, tk), tk)
            kv.append(_KV(k_ref[:, keys], v_ref[:, keys], True))
        _band_core(q_heads, lo, kv, pens, lambda h, i: bias_ref.at[h, i], o_ref.at[rows, :], n_pairs,
                   skip_hidden_quarters=True)
        return carry

    lax.fori_loop(0, blocks_per_step, one_block, 0)


def _band_sample_kernel(q_ref, kn_ref, vn_ref, kc_ref, vc_ref, bias_ref, o_ref, kpad_ref, vpad_ref,
                        *, ts, tk, n_cache_blocks, n_pairs):
    q_heads, lo = _split_heads(q_ref, n_pairs, ts)
    kpad_ref[...] = jnp.zeros_like(kpad_ref)
    vpad_ref[...] = jnp.zeros_like(vpad_ref)
    kpad_ref[:ts, :] = kn_ref[...]
    vpad_ref[:ts, :] = vn_ref[...]
    kv = [_KV(kpad_ref[...], vpad_ref[...], False)]
    for dj in range(1, n_cache_blocks + 1):
        keys = slice((n_cache_blocks - dj) * tk, (n_cache_blocks - dj + 1) * tk)
        kv.append(_KV(kc_ref[:, keys].astype(BF16), vc_ref[:, keys].astype(BF16), True))
    is_new_key = lax.broadcasted_iota(jnp.int32, (ts, tk), 1) < ts

    def bias(h, i):
        return jnp.where(is_new_key, bias_ref[h, 0], NEG_BIG) if i == 0 else bias_ref[h, i]

    _band_core(q_heads, lo, kv, [None] * len(kv), bias, o_ref, n_pairs)


def _band_attention_prompt(q, k, v, bias):
    b, t, _ = q.shape
    tq = tk = ATT_BLOCK
    assert t % tk == 0
    n_pairs = BAND_PAIRS_PER_STEP
    w = n_pairs * LANES
    per_step = math.gcd(t // tq, BAND_BLOCKS_PER_STEP)
    kern = functools.partial(_band_prompt_kernel, tq=tq, tk=tk, n_pairs=n_pairs, blocks_per_step=per_step)
    kv_spec = pl.BlockSpec((None, w, t), lambda bi, hp, qi: (bi, hp, 0))
    q_spec = pl.BlockSpec((None, per_step * tq, w), lambda bi, hp, qi: (bi, qi, hp))
    bias_spec = pl.BlockSpec((2 * n_pairs, 3, tq, tk), lambda bi, hp, qi: (hp, 0, 0, 0))
    return pl.pallas_call(
        kern,
        grid=(b, C_WIDTH // w, t // (per_step * tq)),
        in_specs=[q_spec, kv_spec, kv_spec, bias_spec],
        out_specs=q_spec,
        out_shape=jax.ShapeDtypeStruct(q.shape, BF16),
        compiler_params=_params("parallel", "parallel", "parallel"),
        name="band_attention",
    )(q, k, v, bias)


def _band_attention_sample(q, k_new, v_new, k_cache, v_cache, layer, bias):
    b, ts, _ = q.shape
    wc = k_cache.shape[3]
    tk = ATT_BLOCK
    n_cache_blocks = min(wc // tk, 2)
    assert wc % (n_cache_blocks * tk) == 0 and ts <= tk
    n_pairs = C_WIDTH // LANES
    w = n_pairs * LANES
    kern = functools.partial(_band_sample_kernel, ts=ts, tk=tk, n_cache_blocks=n_cache_blocks,
                             n_pairs=n_pairs)
    new_spec = pl.BlockSpec((None, ts, w), lambda bi, hp: (bi, 0, hp))
    cache_rows = n_cache_blocks * tk
    cache_spec = pl.BlockSpec((None, None, w, cache_rows),
                              lambda bi, hp: (layer, bi, hp, wc // cache_rows - 1))
    bias_spec = pl.BlockSpec((2 * n_pairs, 1 + n_cache_blocks, ts, tk), lambda bi, hp: (hp, 0, 0, 0))
    return pl.pallas_call(
        kern,
        grid=(b, C_WIDTH // w),
        in_specs=[new_spec, new_spec, new_spec, cache_spec, cache_spec, bias_spec],
        out_specs=new_spec,
        out_shape=jax.ShapeDtypeStruct(q.shape, BF16),
        scratch_shapes=[pltpu.VMEM((tk, w), BF16), pltpu.VMEM((tk, w), BF16)],
        compiler_params=_params("parallel", "parallel"),
        name="band_attention_sample",
    )(q, k_new, v_new, k_cache, v_cache, bias)


def _band_bias_kernel(g_ref, o_ref):
    rows = tk = ATT_BLOCK
    q_chunk = lax.broadcasted_iota(jnp.int32, (rows, tk), 0) // CHUNK
    k_chunk = lax.broadcasted_iota(jnp.int32, (rows, tk), 1) // CHUNK
    for dj in range(3):
        g = jnp.broadcast_to(g_ref[0, dj], (rows, 2 * tk))
        tile = pltpu.roll(g, 0, axis=1, stride=1, stride_axis=0)[:, :tk]
        diff = dj * (tk // CHUNK) + q_chunk - k_chunk
        seen = (diff >= 0) & (diff <= C_LEFT_CHUNKS)
        o_ref[0, dj] = jnp.where(seen, tile * LOG2E, NEG_BIG)


def _band_bias(rel_table):
    rows = tk = ATT_BLOCK
    c = jnp.arange(2 * tk, dtype=jnp.int32)
    u = jnp.where(c <= tk, -c, 2 * tk - c)
    idx = jnp.clip(jnp.arange(3, dtype=jnp.int32)[:, None] * tk + u[None, :], REL_MIN, REL_MAX) - REL_MIN
    g = rel_table[:, idx].astype(F32).reshape(C_HEADS, 3, 1, 2 * tk)
    return pl.pallas_call(
        _band_bias_kernel,
        grid=(C_HEADS,),
        in_specs=[pl.BlockSpec((1, 3, 1, 2 * tk), lambda h: (h, 0, 0, 0))],
        out_specs=pl.BlockSpec((1, 3, rows, tk), lambda h: (h, 0, 0, 0)),
        out_shape=jax.ShapeDtypeStruct((C_HEADS, 3, rows, tk), F32),
        compiler_params=_params("parallel"),
        name="band_bias",
    )(g)


def _gla_pair_tile(q, k, v, la, states, emit):
    L = GLA_CHUNK
    chunks = range(q.shape[0] // L)
    row = lax.broadcasted_iota(jnp.int32, (L, L), 0)
    colm = lax.broadcasted_iota(jnp.int32, (L, L), 1)
    tri = (colm <= row).astype(BF16)
    causal = colm <= row
    lane = lax.broadcasted_iota(jnp.int32, (L, LANES), 1)
    sub = lax.broadcasted_iota(jnp.int32, (LANES, B_DV), 0)
    mine = [(lane >= h * B_DK) & (lane < (h + 1) * B_DK) for h in range(2)]
    mine_rows = [(sub >= h * B_DK) & (sub < (h + 1) * B_DK) for h in range(2)]
    rows = [slice(c * L, (c + 1) * L) for c in chunks]

    b = []
    for r in rows:
        g_hi, g_lo = _split_bf16(la[r, :])
        b.append(_dot(tri, g_hi) + _dot(tri, g_lo))
    yield
    qg, qg_h, kg, kd_t, decay = [], [], [], [], []
    for c, r in zip(chunks, rows):
        qg_c = q[r, :] * (B_DK ** -0.5) * jnp.exp(b[c])
        qg.append(qg_c.astype(BF16))
        qg_h.append([jnp.where(mine[h], qg_c, 0.0).astype(BF16) for h in range(2)])
        kg.append((k[r, :] * jnp.exp(-b[c])).astype(BF16))
        b_t = b[c].T
        b_last = b_t[:, L - 1:L]
        kd_t.append((k[r, :].T * jnp.exp(b_last - b_t)).astype(BF16))
        decay.append(jnp.exp(b_last))
    yield
    att =[[jnp.where(causal, _dot_nt(qg_h[c][h], kg[c]), 0.0).astype(BF16) for h in range(2)]
           for c in chunks]
    yield
    o_intra, own = [], []
    for c, r in zip(chunks, rows):
        v_h = [v[r, h * B_DV:(h + 1) * B_DV].astype(BF16) for h in range(2)]
        o_intra.append([_dot(att[c][h], v_h[h]) for h in range(2)])
        own.append([jnp.where(mine_rows[h], _dot(kd_t[c], v_h[h]), 0.0) for h in range(2)])

    yield
    states = list(states)
    start = []
    for c in chunks:
        start.append([s.astype(BF16) for s in states])
        states = [decay[c] * states[h] + own[c][h] for h in range(2)]
    yield
    for c in chunks:
        for h in range(2):
            emit(c, h, o_intra[c][h] + _dot(qg[c], start[c][h]))
    return states


def _pair_states(s0_ref, p):
    zeros_state = jnp.zeros((B_DK, B_DV), F32)
    return [jnp.concatenate([s0_ref[2 * p], zeros_state], axis=0),
            jnp.concatenate([zeros_state, s0_ref[2 * p + 1]], axis=0)]


def _pair_v_cols(p):
    return slice(2 * p * B_DV, 2 * (p + 1) * B_DV)


def _gla_rows_emit(o_ref, p):
    def emit(c, h, o):
        o_ref[c * GLA_CHUNK:(c + 1) * GLA_CHUNK, (2 * p + h) * B_DV:(2 * p + h + 1) * B_DV] = o
    return emit


def _gla_kernel(q_ref, k_ref, v_ref, la_ref, s0_ref, o_ref, s_out_ref):
    pairs = range(B_HEADS // 2)
    states = _interleave([
        _gla_pair_tile(q_ref[:, _pair_cols(p)], k_ref[:, _pair_cols(p)], v_ref[:, _pair_v_cols(p)],
                       la_ref[:, _pair_cols(p)], _pair_states(s0_ref, p), _gla_rows_emit(o_ref, p))
        for p in pairs])
    for p in pairs:
        s_out_ref[2 * p] = states[p][0][:B_DK, :]
        s_out_ref[2 * p + 1] = states[p][1][B_DK:, :]


def _gla(q, k, v, la, s0):
    b, t, _ = q.shape
    assert t % GLA_CHUNK == 0 and t <= GLA_MAX_TILE
    qk_spec = pl.BlockSpec((None, t, B_KW), lambda bi: (bi, 0, 0))
    v_spec = pl.BlockSpec((None, t, B_VW), lambda bi: (bi, 0, 0))
    s_spec = pl.BlockSpec((None, B_HEADS, B_DK, B_DV), lambda bi: (bi, 0, 0, 0))
    return pl.pallas_call(
        _gla_kernel,
        grid=(b,),
        in_specs=[qk_spec, qk_spec, v_spec, qk_spec, s_spec],
        out_specs=[v_spec, s_spec],
        out_shape=[jax.ShapeDtypeStruct(v.shape, F32), jax.ShapeDtypeStruct(s0.shape, F32)],
        compiler_params=_params("parallel"),
        name="gla",
    )(q, k, v, la, s0)


def _pad_rows(x, n):
    return jnp.pad(x, ((0, 0), (0, n - x.shape[1]), (0, 0)))


def _heads_last(x, heads):
    n, b, _, s = x.shape
    return jnp.transpose(x.reshape(n, b, heads, HEAD_DIM, s), (0, 1, 4, 2, 3))


def _feature_major(cache):
    n, b, s, heads, hd = cache.shape
    return jnp.transpose(cache, (0, 1, 3, 4, 2)).reshape(n, b, heads * hd, s)


def _row_tile(m):
    for tm in (512, 256, 128, 64, 32, 16, 8):
        if m % tm == 0:
            return tm
    raise ValueError(f"token count {m} is not a multiple of 8")


def kernel(x_prompt, x_sample, cache_a_k, cache_a_v, state_b, cache_c_k, cache_c_v, norm_mix_g, norm_ffn_g, w_in_ab, w_gate_b, b_gate_b, norm_gla_g, w_out_ab, w_qkv_c, rel_bias_c, w_out_c, w_ffn_gate, w_ffn_up, w_ffn_down, norm_final_g):
    bp, tp, d = x_prompt.shape
    bs, ts, _ = x_sample.shape
    depth = norm_mix_g.shape[0]
    past = cache_a_k.shape[2]
    wc = cache_c_k.shape[2]
    assert tp % ATT_BLOCK == 0 and past % ATT_BLOCK == 0 and wc % ATT_BLOCK == 0
    assert ts <= GLA_CHUNK and ts % 8 == 0
    mp, ms = bp * tp, bs * ts
    tmp, tms = _row_tile(tp), _row_tile(ms)
    xp = x_prompt.reshape(mp, d)
    xs = x_sample.reshape(ms, d)
    row2 = lambda a: a.reshape(1, -1)

    a_ks, a_vs, b_sp, b_ss, c_ks, c_vs = [], [], [], [], [], []
    a_kv_prompt = c_kv_prompt = None
    n_ab, n_c = (depth + 1) // 2, depth // 2
    keep = min(C_LEFT_CHUNKS * CHUNK, tp)

    kv0, kv1 = A_WIDTH, 3 * A_WIDTH
    o = 3 * A_WIDTH + 2 * B_KW + B_VW
    w_in_t = jnp.swapaxes(w_in_ab, 1, 2)
    w_main_all = jnp.concatenate([w_in_t[:, :kv0], w_in_t[:, kv1:o], w_in_t[:, o + B_GATE_RANK:]],
                                 axis=1).astype(BF16)
    w_kv_ab_t_all = w_in_t[:, kv0:kv1].astype(BF16)
    w_lr_all = jnp.pad(w_in_t[:, o:o + B_GATE_RANK],
                       ((0, 0), (0, LANES - B_GATE_RANK), (0, 0))).astype(BF16)
    w_gate_all = jnp.pad(w_gate_b, ((0, 0), (0, LANES - B_GATE_RANK), (0, 0))).astype(BF16)
    w_out_ab_all = w_out_ab.astype(BF16)
    w_q_all = w_qkv_c[:, :, :C_WIDTH].astype(BF16)
    w_kv_c_t_all = jnp.swapaxes(w_qkv_c[:, :, C_WIDTH:], 1, 2).astype(BF16)
    w_out_c_all = w_out_c.astype(BF16)
    cache_a_k_fm, cache_a_v_fm, cache_c_k_fm, cache_c_v_fm = (
        _feature_major(c) for c in (cache_a_k, cache_a_v, cache_c_k, cache_c_v))

    for layer in range(depth):
        i = layer // 2
        g_mix = row2(norm_mix_g[layer])
        riders = _Riders((w_ffn_gate, w_ffn_up, w_ffn_down), layer, mp // tmp)
        ffn_ends = (row2(norm_ffn_g[layer]), row2(norm_final_g) if layer == depth - 1 else None)
        if layer % 2 == 0:
            w_main, w_kv_t, w_lr, w_gate, w_out = (
                _Slab(w, i) for w in (w_main_all, w_kv_ab_t_all, w_lr_all, w_gate_all, w_out_ab_all))
            b_gate = row2(b_gate_b[i])
            g_gla = row2(norm_gla_g[i])

            qa, ka, va, kab, vab, qb, kb, vb, r, la, *ffn_w = _proj_ab(
                xp, g_mix, w_main, w_kv_t, w_lr, w_gate, b_gate, tmp, batch=bp,
                stack=_KVStack(i, n_ab, tp, a_kv_prompt), riders=riders)
            ffn = (ffn_ends[0], *ffn_w, ffn_ends[1])
            a_kv_prompt = (ka, va)
            sh = lambda a: a.reshape(bp, tp, -1)
            oa, ob, sbp = _mixer_ab_prompt(sh(qa), kab, vab, sh(qb), sh(kb), sh(vb), sh(la))
            mixed_p = (oa.reshape(mp, -1), ob.reshape(mp, -1), r)
            b_sp.append(sbp)

            qa, ka, va, kab, vab, qb, kb, vb, r, la = _proj_ab(
                xs, g_mix, w_main, w_kv_t, w_lr, w_gate, b_gate, tms)
            sh = lambda a: a.reshape(bs, ts, -1)
            oa = _sb_attention_sample(sh(qa), sh(kab), sh(vab), cache_a_k_fm, cache_a_v_fm, i)
            pad_t = lambda a: _pad_rows(sh(a), GLA_CHUNK)
            ob, sbs = _gla(pad_t(qb), pad_t(kb), pad_t(vb), pad_t(la), state_b[i])
            mixed_s = (oa.reshape(ms, -1), ob[:, :ts].reshape(ms, -1), r)
            xp, xs = _layer_tail(xp, mixed_p, xs, mixed_s, (g_gla, w_out), *ffn, tmp)
            a_ks.append(ka.reshape(bs, ts, A_HEADS, HEAD_DIM))
            a_vs.append(va.reshape(bs, ts, A_HEADS, HEAD_DIM))
            b_ss.append(sbs)
        else:
            w_q, w_kv_t, w_out = (_Slab(w, i) for w in (w_q_all, w_kv_c_t_all, w_out_c_all))

            q, k, v, kb16, vb16, *ffn_w = _proj_c(xp, g_mix, w_q, w_kv_t, tmp, batch=bp,
                                                  stack=_KVStack(i, n_c, keep, c_kv_prompt), riders=riders)
            ffn = (ffn_ends[0], *ffn_w, ffn_ends[1])
            c_kv_prompt = (k, v)
            bias = _band_bias(rel_bias_c[i])
            oc = _band_attention_prompt(q.reshape(bp, tp, -1), kb16, vb16, bias)
            mixed_p = (oc.reshape(mp, -1),)

            q, k, v, kb16, vb16 = _proj_c(xs, g_mix, w_q, w_kv_t, tms)
            sh = lambda a: a.reshape(bs, ts, -1)
            oc = _band_attention_sample(sh(q), sh(kb16), sh(vb16), cache_c_k_fm, cache_c_v_fm, i, bias)
            xp, xs = _layer_tail(xp, mixed_p, xs, (oc.reshape(ms, -1),), (w_out,), *ffn, tmp)
            c_ks.append(k.reshape(bs, ts, C_HEADS, HEAD_DIM))
            c_vs.append(v.reshape(bs, ts, C_HEADS, HEAD_DIM))

    y_prompt = xp.reshape(bp, tp, d)
    y_sample = xs.reshape(bs, ts, d)
    a_kp, a_vp = (_heads_last(a, A_HEADS) for a in a_kv_prompt)
    c_kp, c_vp = (_heads_last(a, C_HEADS) for a in c_kv_prompt)
    return (y_prompt, y_sample, a_kp, a_vp, jnp.stack(a_ks), jnp.stack(a_vs),
            jnp.stack(b_sp), jnp.stack(b_ss), c_kp, c_vp, jnp.stack(c_ks), jnp.stack(c_vs))
```

```python
import functools
import math

import jax
import jax.numpy as jnp
from jax import lax
from jax.experimental import pallas as pl
from jax.experimental.pallas import tpu as pltpu

F32 = jnp.float32
BF16 = jnp.bfloat16

EPS = 1e-6
HEAD_DIM = 64
LANES = 128
A_HEADS = 8
A_WIDTH = A_HEADS * HEAD_DIM
B_HEADS = 4
B_DK = 64
B_DV = 128
B_KW = B_HEADS * B_DK
B_VW = B_HEADS * B_DV
B_GATE_RANK = 16
B_GATE_TEMP = 16.0
GLA_CHUNK = 64
C_HEADS = 16
C_WIDTH = C_HEADS * HEAD_DIM
CHUNK = 64
C_LEFT_CHUNKS = 8
REL_MIN = -(CHUNK - 1)
REL_MAX = 128
ATT_BLOCK = 256
NEG_BIG = -1e30
LOG2E = 1.4426950408889634
SB_DEAD = 104.0
SB_PAIRS_PER_STEP = 4
BAND_PAIRS_PER_STEP = 4
BIAS_HEADS_PER_STEP = 4
MIXER_BLOCKS_PER_STEP = 4
BAND_BLOCKS_PER_STEP = 8
GLA_MAX_TILE = 512
VMEM_LIMIT = 56 * 1024 * 1024


def _params(*sem):
    return pltpu.CompilerParams(dimension_semantics=sem, vmem_limit_bytes=VMEM_LIMIT)


class _Slab:
    def __init__(self, stacked, index):
        self.stacked, self.index, self.shape = stacked, index, stacked.shape[1:]


def _operand(a):
    return a.stacked if isinstance(a, _Slab) else a


def _resident(a):
    if isinstance(a, _Slab):
        index = (a.index,) + (0,) * len(a.shape)
        return pl.BlockSpec((None, *a.shape), lambda *_: index, pipeline_mode=pl.Buffered(1))
    return pl.BlockSpec(a.shape, lambda *_: (0,) * a.ndim, pipeline_mode=pl.Buffered(1))


def _rms(x, g):
    return x * lax.rsqrt(jnp.mean(x * x, axis=-1, keepdims=True) + EPS) * g


def _log_sigmoid_pair(z):
    l = jnp.log1p(jnp.exp(-jnp.abs(z)))
    return jnp.minimum(z, 0.0) - l, jnp.minimum(-z, 0.0) - l


def _split_bf16(x):
    hi = x.astype(BF16)
    lo = (x - hi.astype(F32)).astype(BF16)
    return hi, lo


def _dot(a, b):
    return jnp.dot(a, b, preferred_element_type=F32)


def _dot_nt(a, b):
    return lax.dot_general(a, b, (((1,), (1,)), ((), ())), preferred_element_type=F32)


def _dot_tn(a, b):
    return lax.dot_general(a, b, (((0,), (0,)), ((), ())), preferred_element_type=F32)


def _pair_cols(p):
    return slice(p * LANES, (p + 1) * LANES)


class _KV:
    def __init__(self, k, v, feature_major):
        self.k, self.v, self.feature_major = k, v, feature_major
        self.n_keys = k.shape[1] if feature_major else k.shape[0]

    def scores(self, q_h, p):
        if self.feature_major:
            return _dot(q_h, self.k[_pair_cols(p), :])
        return _dot_nt(q_h, self.k[:, _pair_cols(p)])

    def first(self, n):
        if self.feature_major:
            return _KV(self.k[:, :n], self.v[:, :n], True)
        return _KV(self.k[:n], self.v[:n], False)

    def last(self, n):
        if self.feature_major:
            return _KV(self.k[:, -n:], self.v[:, -n:], True)
        return _KV(self.k[-n:], self.v[-n:], False)

    def values(self, p):
        return self.v[_pair_cols(p), :] if self.feature_major else self.v[:, _pair_cols(p)]

    def weighted(self, w, v_p):
        return _dot_nt(w, v_p) if self.feature_major else _dot(w, v_p)

    def head_lanes(self):
        shape = (LANES, self.n_keys) if self.feature_major else (self.n_keys, LANES)
        return lax.broadcasted_iota(jnp.int32, shape, 0 if self.feature_major else 1) < HEAD_DIM


def _split_heads(q_ref, n_pairs, tq):
    lo = lax.broadcasted_iota(jnp.int32, (tq, LANES), 1) < HEAD_DIM
    heads = []
    for p in range(n_pairs):
        q = q_ref[:, _pair_cols(p)]
        heads += [jnp.where(lo, q, jnp.zeros_like(q)), jnp.where(lo, jnp.zeros_like(q), q)]
    return heads, lo


def _emit_kv(y, wkv_ref, k_ref, v_ref, kb_ref, vb_ref, feature_major, kept_tail=None, own_slot=0):
    if not feature_major:
        kv = _dot_nt(y, wkv_ref[...])
        width = kv.shape[1] // 2
        k, v = kv[:, :width], kv[:, width:]
        k_ref[...], v_ref[...], kb_ref[...], vb_ref[...] = k, v, k.astype(BF16), v.astype(BF16)
        return
    kv = _dot_nt(wkv_ref[...], y)
    width = kv.shape[0] // 2
    k, v = kv[:width, :], kv[width:, :]
    kb_ref[...] = k.astype(BF16)
    vb_ref[...] = v.astype(BF16)
    slot = own_slot
    other_slots = [s for s in range(k_ref.shape[0]) if s != slot]
    if kept_tail is None:
        fill, fill_cols = slice(None), k_ref.shape[2]
        k_ref[slot] = k
        v_ref[slot] = v
    else:
        n_tiles, n_kept = kept_tail
        tile = pl.program_id(0) % n_tiles
        tm = k.shape[1]
        cols = pl.ds(pl.multiple_of(jnp.maximum(tile - (n_tiles - n_kept), 0) * tm, tm), tm)
        k_ref[slot, :, cols] = k
        v_ref[slot, :, cols] = v
        share = k_ref.shape[2] // n_tiles
        if share % LANES == 0 and share * n_tiles == k_ref.shape[2]:
            fill, fill_cols = pl.ds(pl.multiple_of(tile * share, share), share), share
        else:
            fill, fill_cols = slice(None), k_ref.shape[2]
    for dst in (k_ref, v_ref):
        for s in other_slots:
            dst[s, :, fill] = jnp.zeros((dst.shape[1], fill_cols), dst.dtype)


class _Riders:
    def __init__(self, stacks, layer, steps):
        self.stacks, self.layer = list(stacks), layer
        self.chunks = next(c for c in (16, 8, 4, 2, 1)
                           if c <= steps and all(w.shape[1] % (16 * c) == 0 for w in self.stacks))

    def specs(self):
        layer, last = self.layer, self.chunks - 1
        ins = [pl.BlockSpec((None, w.shape[1] // self.chunks, w.shape[2]),
                            lambda i: (layer, jnp.minimum(i, last), 0)) for w in self.stacks]
        outs = [pl.BlockSpec((w.shape[1] // self.chunks, w.shape[2]),
                             lambda i: (jnp.minimum(i, last), 0)) for w in self.stacks]
        shapes = [jax.ShapeDtypeStruct(w.shape[1:], BF16) for w in self.stacks]
        return ins, outs, shapes


def _split_extra_refs(refs, n_aliased, n_riders):
    rider_in = refs[n_aliased:n_aliased + n_riders]
    outs = refs[n_aliased + n_riders:len(refs) - n_riders]
    return outs, list(zip(rider_in, refs[len(refs) - n_riders:]))


def _cast_riders(pairs):
    for src, dst in pairs:
        dst[...] = src[...].astype(BF16)


def _proj_ab_kernel(x_ref, g_ref, w_ref, wkv_ref, wlr_ref, wgate_ref, bgate_ref, *refs,
                    feature_major, n_aliased, n_riders, own_slot):
    outs, riders = _split_extra_refs(refs, n_aliased, n_riders)
    qa_ref, ka_ref, va_ref, kab_ref, vab_ref, qb_ref, kb_ref, vb_ref, r_ref, la_ref = outs
    _cast_riders(riders)
    y = _rms(x_ref[...], g_ref[...]).astype(BF16)
    _emit_kv(y, wkv_ref, ka_ref, va_ref, kab_ref, vab_ref, feature_major, own_slot=own_slot)
    z = _dot_nt(y, w_ref[...])
    c = 0
    qa_ref[...] = (z[:, c:c + A_WIDTH] * (HEAD_DIM ** -0.5)).astype(BF16); c += A_WIDTH
    qb_ref[...] = z[:, c:c + B_KW]; c += B_KW
    kb_ref[...] = z[:, c:c + B_KW]; c += B_KW
    vb_ref[...] = z[:, c:c + B_VW]; c += B_VW
    r_ref[...] = z[:, c:c + B_VW]
    g_lr = _dot_nt(y, wlr_ref[...])
    gate = _dot(g_lr.astype(BF16), wgate_ref[...]) + bgate_ref[...]
    la_ref[...] = _log_sigmoid_pair(gate)[0] * (1.0 / B_GATE_TEMP)


class _KVStack:
    def __init__(self, layer, n_layers, keep, previous=None):
        self.layer, self.n_layers, self.keep, self.previous = layer, n_layers, keep, previous


def _kv_out(m, width, tm, batch, stack):
    if batch is None:
        spec = pl.BlockSpec((tm, width), lambda i: (i, 0))
        return [spec] * 4, [jax.ShapeDtypeStruct((m, width), dt) for dt in (F32, F32, BF16, BF16)], None, 0
    t = m // batch
    assert t % tm == 0 and stack.keep % tm == 0 and stack.keep <= t
    n_tiles = t // tm
    copy_spec = pl.BlockSpec((None, width, tm), lambda i: (i // n_tiles, 0, i % n_tiles))
    copy_shape = jax.ShapeDtypeStruct((batch, width, t), BF16)
    slots, first_slot, own_slot = ((stack.n_layers, 0, stack.layer) if stack.previous is None
                                   else (1, stack.layer, 0))
    if stack.keep == t:
        f32_spec = pl.BlockSpec((slots, None, width, tm), lambda i: (first_slot, i // n_tiles, 0, i % n_tiles))
        kept_tail = None
    else:
        f32_spec = pl.BlockSpec((slots, None, width, stack.keep), lambda i: (first_slot, i // n_tiles, 0, 0))
        kept_tail = (n_tiles, stack.keep // tm)
    f32_shape = jax.ShapeDtypeStruct((stack.n_layers, batch, width, stack.keep), F32)
    return ([f32_spec, f32_spec, copy_spec, copy_spec], [f32_shape, f32_shape, copy_shape, copy_shape],
            kept_tail, own_slot)


def _aliased_stack(stack, n_inputs, first_output):
    if stack is None or stack.previous is None:
        return [], [], {}
    prev = list(stack.previous)
    specs = [pl.BlockSpec(memory_space=pl.ANY)] * len(prev)
    return prev, specs, {n_inputs + j: first_output + j for j in range(len(prev))}


def _rider_plumbing(riders):
    if riders is None:
        return [], [], [], []
    return (riders.stacks, *riders.specs())


def _proj_ab(x, g, w_main, w_kv, w_lr, w_gate, b_gate, tm, batch=None, stack=None, riders=None):
    m, d = x.shape
    row = lambda n: pl.BlockSpec((tm, n), lambda i: (i, 0))
    params = (g, w_main, w_kv, w_lr, w_gate, b_gate)
    kv_specs, kv_shapes, kept_tail, own_slot = _kv_out(m, A_WIDTH, tm, batch, stack)
    assert kept_tail is None
    prev, prev_specs, aliases = _aliased_stack(stack, 1 + len(params), 1)
    ride_ops, ride_in, ride_out, ride_shapes = _rider_plumbing(riders)
    rest = [(B_KW, F32), (B_KW, F32), (B_VW, F32), (B_VW, F32), (B_KW, F32)]
    return pl.pallas_call(
        functools.partial(_proj_ab_kernel, feature_major=batch is not None, n_aliased=len(prev),
                          n_riders=len(ride_ops), own_slot=own_slot),
        grid=(m // tm,),
        in_specs=[row(d)] + [_resident(a) for a in params] + prev_specs + ride_in,
        out_specs=[row(A_WIDTH)] + kv_specs + [row(n) for n, _ in rest] + ride_out,
        out_shape=([jax.ShapeDtypeStruct((m, A_WIDTH), BF16)] + kv_shapes
                   + [jax.ShapeDtypeStruct((m, n), dt) for n, dt in rest] + ride_shapes),
        input_output_aliases=aliases,
        compiler_params=_params("arbitrary" if ride_ops else "parallel"),
        name="proj_ab",
    )(x, *[_operand(a) for a in params], *prev, *ride_ops)


def _proj_c_kernel(x_ref, g_ref, wq_ref, wkv_ref, *refs, feature_major, n_aliased, kept_tail,
                   n_riders, own_slot):
    outs, riders = _split_extra_refs(refs, n_aliased, n_riders)
    q_ref, k_ref, v_ref, kb_ref, vb_ref = outs
    _cast_riders(riders)
    y = _rms(x_ref[...], g_ref[...]).astype(BF16)
    _emit_kv(y, wkv_ref, k_ref, v_ref, kb_ref, vb_ref, feature_major, kept_tail, own_slot)
    q_ref[...] = (_dot(y, wq_ref[...]) * (HEAD_DIM ** -0.5 * LOG2E)).astype(BF16)


def _proj_c(x, g, w_q, w_kv, tm, batch=None, stack=None, riders=None):
    m, d = x.shape
    row = lambda n: pl.BlockSpec((tm, n), lambda i: (i, 0))
    params = (g, w_q, w_kv)
    kv_specs, kv_shapes, kept_tail, own_slot = _kv_out(m, C_WIDTH, tm, batch, stack)
    prev, prev_specs, aliases = _aliased_stack(stack, 1 + len(params), 1)
    ride_ops, ride_in, ride_out, ride_shapes = _rider_plumbing(riders)
    return pl.pallas_call(
        functools.partial(_proj_c_kernel, feature_major=batch is not None, n_aliased=len(prev),
                          kept_tail=kept_tail, n_riders=len(ride_ops), own_slot=own_slot),
        grid=(m // tm,),
        in_specs=[row(d)] + [_resident(a) for a in params] + prev_specs + ride_in,
        out_specs=[row(C_WIDTH)] + kv_specs + ride_out,
        out_shape=[jax.ShapeDtypeStruct((m, C_WIDTH), BF16)] + kv_shapes + ride_shapes,
        input_output_aliases=aliases,
        compiler_params=_params("arbitrary" if (kept_tail or ride_ops) else "parallel"),
        name="proj_c",
    )(x, *[_operand(a) for a in params], *prev, *ride_ops)


def _layer_tail_rows(x_ref, mixer_refs, param_refs, o_ref, gla_merge, final_norm):
    if gla_merge:
        oa_ref, ob_ref, r_ref = mixer_refs
        ggla_ref, wout_ref, *param_refs = param_refs
        ob = ob_ref[...]
        parts = []
        for h in range(B_HEADS):
            seg = ob[:, h * B_DV:(h + 1) * B_DV]
            parts.append(seg * lax.rsqrt(jnp.mean(seg * seg, axis=-1, keepdims=True) + EPS))
        r = r_ref[...]
        obn = jnp.concatenate(parts, axis=-1) * ggla_ref[...] * (r * jax.nn.sigmoid(r))
        mix = _dot(oa_ref[...], wout_ref[:A_WIDTH, :]) + _dot(obn.astype(BF16), wout_ref[A_WIDTH:, :])
    else:
        (oc_ref,) = mixer_refs
        wout_ref, *param_refs = param_refs
        mix = _dot(oc_ref[...], wout_ref[...])
    gffn_ref, wg_ref, wu_ref, wd_ref, *param_refs = param_refs
    x = x_ref[...] + mix
    y = _rms(x, gffn_ref[...]).astype(BF16)
    h = _dot(y, wg_ref[...])
    u = _dot(y, wu_ref[...])
    a = (h * jax.nn.sigmoid(h) * u).astype(BF16)
    x = x + _dot(a, wd_ref[...])
    o_ref[...] = _rms(x, param_refs[0][...]) if final_norm else x


def _layer_tail_kernel(*refs, n_mixer, n_params, gla_merge, final_norm, main_steps):
    main, extra = refs[:1 + n_mixer], refs[1 + n_mixer:2 + 2 * n_mixer]
    params = refs[2 + 2 * n_mixer:2 + 2 * n_mixer + n_params]
    o_ref, o_extra_ref = refs[-2:]

    @pl.when(pl.program_id(0) < main_steps)
    def _():
        _layer_tail_rows(main[0], main[1:], params, o_ref, gla_merge, final_norm)

    @pl.when(pl.program_id(0) == main_steps)
    def _():
        _layer_tail_rows(extra[0], extra[1:], params, o_extra_ref, gla_merge, final_norm)


def _layer_tail(x, mixer_out, x_extra, mixer_out_extra, mixer_params, g_ffn, wg, wu, wd, g_fin, tm):
    m, d = x.shape
    main_steps = m // tm
    row = lambda a: pl.BlockSpec((tm, a.shape[1]), lambda i: (jnp.minimum(i, main_steps - 1), 0))
    whole = lambda a: pl.BlockSpec(a.shape, lambda i: (0, 0))
    params = [*mixer_params, g_ffn, wg, wu, wd] + ([] if g_fin is None else [g_fin])
    return pl.pallas_call(
        functools.partial(_layer_tail_kernel, n_mixer=len(mixer_out), n_params=len(params),
                          gla_merge=len(mixer_out) == 3, final_norm=g_fin is not None, main_steps=main_steps),
        grid=(main_steps + 1,),
        in_specs=([row(x)] + [row(a) for a in mixer_out] + [whole(x_extra)]
                  + [whole(a) for a in mixer_out_extra] + [_resident(a) for a in params]),
        out_specs=[row(x), whole(x_extra)],
        out_shape=[jax.ShapeDtypeStruct((m, d), F32), jax.ShapeDtypeStruct(x_extra.shape, F32)],
        compiler_params=_params("arbitrary"),
        name="layer_tail",
    )(x, *mixer_out, x_extra, *mixer_out_extra, *[_operand(a) for a in params])


def _interleave(stage_generators):
    results = [None] * len(stage_generators)
    live = list(range(len(stage_generators)))
    while live:
        for i in list(live):
            try:
                next(stage_generators[i])
            except StopIteration as done:
                results[i] = done.value
                live.remove(i)
    return results


def _sb_core(q_heads, lo, first_kv, first_mask, earlier_kv, n_earlier, acc_ref, c_ref, o_ref, n_pairs,
             companions=(), on_companions=None):
    tk = first_kv.n_keys
    heads = range(2 * n_pairs)
    later = (lax.broadcasted_iota(jnp.int32, (tk, tk), 0)
             > lax.broadcasted_iota(jnp.int32, (tk, tk), 1)).astype(BF16)
    c_ref[...] = jnp.zeros_like(c_ref)

    tq = q_heads[0].shape[0]

    def row_groups(kv, mask):
        if mask is None or tq != tk or tq % (2 * LANES):
            return [(slice(None), kv, mask, later)]
        half = tq // 2
        return [(slice(0, half), kv.first(half), mask[:half, :half], later[:half, :half]),
                (slice(half, tq), kv, mask[half:, :], later)]

    def block(kv, mask):
        groups = row_groups(kv, mask)
        chains = [(h, g) for h in heads for g in range(len(groups))]
        z = [groups[g][1].scores(q_heads[h][groups[g][0], :], h // 2) for h, g in chains]
        yield
        log_beta, drop, after = [], [], []
        for i, (h, g) in enumerate(chains):
            d = jnp.maximum(z[i], 0.0) + jnp.log(1.0 + jnp.exp(-jnp.abs(z[i])))
            log_beta.append(z[i] - d)
            drop.append(d if mask is None else jnp.where(groups[g][2], d, 0.0))
        yield
        for i, (h, g) in enumerate(chains):
            hi, lo_part = _split_bf16(drop[i])
            tri = groups[g][3]
            after.append(_dot(hi, tri) + _dot(lo_part, tri))
        yield
        pv = {}
        for i, (h, g) in enumerate(chains):
            rows, kv_g = groups[g][0], groups[g][1]
            c = c_ref[h, rows, :]
            w = jnp.exp(log_beta[i] - after[i] - c)
            if mask is not None:
                w = jnp.where(groups[g][2], w, 0.0)
            c_ref[h, rows, :] = c + after[i][:, 0:1] + drop[i][:, 0:1]
            pv[h, g] = kv_g.weighted(w.astype(BF16), kv_g.values(h // 2))
        yield
        per_head = [pv[h, 0] if len(groups) == 1 else jnp.concatenate([pv[h, g] for g in range(len(groups))], 0)
                    for h in heads]
        out = [jnp.where(lo, per_head[2 * p], per_head[2 * p + 1]) for p in range(n_pairs)]
        return out[0] if n_pairs == 1 else jnp.concatenate(out, axis=-1)

    def all_dead():
        return jnp.min(c_ref[...]) > SB_DEAD

    first, *companion_results = _interleave([block(first_kv, first_mask), *companions])
    acc_ref[...] = first
    if companions:
        on_companions(companion_results)

    def cond(carry):
        n, dead = carry
        return (n < n_earlier) & jnp.logical_not(dead)

    def body(carry):
        n, _ = carry
        acc_ref[...] += _interleave([block(earlier_kv(n), None)])[0]
        return n + 1, all_dead()

    lax.while_loop(cond, body, (jnp.int32(0), jnp.asarray(False)))
    o_ref[...] = acc_ref[...].astype(o_ref.dtype)


def _mixer_ab_prompt_kernel(q_ref, k_ref, v_ref, qb_ref, kb_ref, vb_ref, la_ref, o_ref, ob_ref, s_out_ref,
                            acc_ref, c_ref, carry_ref, *, tb, n_pairs, blocks_per_step):
    pairs = range(B_HEADS // 2)
    strictly_earlier = (lax.broadcasted_iota(jnp.int32, (tb, tb), 1)
                        < lax.broadcasted_iota(jnp.int32, (tb, tb), 0))

    def kv_block(j):
        keys = pl.ds(pl.multiple_of(j * tb, tb), tb)
        return _KV(k_ref[:, keys], v_ref[:, keys], True)

    def keep_states(states):
        for p in pairs:
            carry_ref[2 * p], carry_ref[2 * p + 1] = states[p]
            s_out_ref[2 * p] = states[p][0][:B_DK, :]
            s_out_ref[2 * p + 1] = states[p][1][B_DK:, :]

    @pl.when(pl.program_id(1) == 0)
    def _():
        carry_ref[...] = jnp.zeros_like(carry_ref)

    def one_block(s, carry):
        qi = pl.program_id(1) * blocks_per_step + s
        rows = pl.ds(pl.multiple_of(s * tb, tb), tb)
        gla = [_gla_pair_tile(qb_ref[rows, _pair_cols(p)], kb_ref[rows, _pair_cols(p)],
                              vb_ref[rows, _pair_v_cols(p)], la_ref[rows, _pair_cols(p)],
                              [carry_ref[2 * p], carry_ref[2 * p + 1]],
                              _gla_rows_emit(ob_ref.at[rows, :], p)) for p in pairs]
        q_heads, lo = _split_heads(q_ref.at[rows, :], n_pairs, tb)
        _sb_core(q_heads, lo, kv_block(qi), strictly_earlier, lambda n: kv_block(qi - 1 - n), qi,
                 acc_ref, c_ref, o_ref.at[rows, :], n_pairs, companions=gla, on_companions=keep_states)
        return carry

    lax.fori_loop(0, blocks_per_step, one_block, 0)


def _sb_sample_kernel(q_ref, kn_ref, vn_ref, kc_ref, vc_ref, o_ref, acc_ref, c_ref, kpad_ref, vpad_ref,
                      *, ts, tk, n_cache_blocks, n_pairs):
    q_heads, lo = _split_heads(q_ref, n_pairs, ts)
    kpad_ref[...] = jnp.zeros_like(kpad_ref)
    vpad_ref[...] = jnp.zeros_like(vpad_ref)
    kpad_ref[:ts, :] = kn_ref[...]
    vpad_ref[:ts, :] = vn_ref[...]

    def cache_block(n):
        keys = pl.ds(pl.multiple_of((n_cache_blocks - 1 - n) * tk, tk), tk)
        return _KV(kc_ref[:, keys].astype(BF16), vc_ref[:, keys].astype(BF16), True)

    strictly_earlier = (lax.broadcasted_iota(jnp.int32, (ts, tk), 1)
                        < lax.broadcasted_iota(jnp.int32, (ts, tk), 0))
    _sb_core(q_heads, lo, _KV(kpad_ref[...], vpad_ref[...], False), strictly_earlier, cache_block,
             n_cache_blocks, acc_ref, c_ref, o_ref, n_pairs)


def _mixer_ab_prompt(q, k, v, qb, kb, vb, la):
    b, t, _ = q.shape
    tb = ATT_BLOCK
    assert t % tb == 0 and tb % GLA_CHUNK == 0 and tb <= GLA_MAX_TILE
    n_pairs = A_WIDTH // LANES
    per_step = math.gcd(t // tb, MIXER_BLOCKS_PER_STEP)
    rows = lambda width: pl.BlockSpec((None, per_step * tb, width), lambda bi, qi: (bi, qi, 0))
    kv_spec = pl.BlockSpec((None, A_WIDTH, t), lambda bi, qi: (bi, 0, 0))
    state = (B_HEADS, B_DK, B_DV)
    return pl.pallas_call(
        functools.partial(_mixer_ab_prompt_kernel, tb=tb, n_pairs=n_pairs, blocks_per_step=per_step),
        grid=(b, t // (per_step * tb)),
        in_specs=[rows(A_WIDTH), kv_spec, kv_spec, rows(B_KW), rows(B_KW), rows(B_VW), rows(B_KW)],
        out_specs=[rows(A_WIDTH), rows(B_VW), pl.BlockSpec((None, *state), lambda bi, qi: (bi, 0, 0, 0))],
        out_shape=[jax.ShapeDtypeStruct(q.shape, BF16), jax.ShapeDtypeStruct(vb.shape, F32),
                   jax.ShapeDtypeStruct((b, *state), F32)],
        scratch_shapes=[pltpu.VMEM((tb, A_WIDTH), F32), pltpu.VMEM((2 * n_pairs, tb, 1), F32),
                        pltpu.VMEM((B_HEADS, LANES, B_DV), F32)],
        compiler_params=_params("parallel", "arbitrary"),
        name="mixer_ab",
    )(q, k, v, qb, kb, vb, la)


def _sb_attention_sample(q, k_new, v_new, k_cache, v_cache, layer):
    b, ts, _ = q.shape
    past = k_cache.shape[3]
    tk = ATT_BLOCK
    assert past % tk == 0 and ts <= tk
    n_pairs = SB_PAIRS_PER_STEP
    w = n_pairs * LANES
    new_spec = pl.BlockSpec((None, ts, w), lambda bi, hp: (bi, 0, hp))
    cache_spec = pl.BlockSpec((None, None, w, past), lambda bi, hp: (layer, bi, hp, 0))
    return pl.pallas_call(
        functools.partial(_sb_sample_kernel, ts=ts, tk=tk, n_cache_blocks=past // tk, n_pairs=n_pairs),
        grid=(b, A_WIDTH // w),
        in_specs=[new_spec, new_spec, new_spec, cache_spec, cache_spec],
        out_specs=new_spec,
        out_shape=jax.ShapeDtypeStruct(q.shape, BF16),
        scratch_shapes=[pltpu.VMEM((ts, w), F32), pltpu.VMEM((2 * n_pairs, ts, 1), F32),
                        pltpu.VMEM((tk, w), BF16), pltpu.VMEM((tk, w), BF16)],
        compiler_params=_params("parallel", "parallel"),
        name="sb_attention_sample",
    )(q, k_new, v_new, k_cache, v_cache)


def _band_core(q_heads, lo, kv, pens, bias, o_ref, n_pairs, skip_hidden_quarters=False):
    heads = range(2 * n_pairs)
    tq = q_heads[0].shape[0]
    everything = [(slice(None), [(kv[i], i, slice(None)) for i in range(len(kv))])]
    groups = _band_quarters(kv, tq) if skip_hidden_quarters else everything
    chains = [(h, g) for h in heads for g in range(len(groups))]
    z = [[piece.scores(q_heads[h][groups[g][0], :], h // 2) + bias(h, i)[groups[g][0], cols]
          for piece, i, cols in groups[g][1]] for h, g in chains]
    acc = {}
    for c, (h, g) in enumerate(chains):
        pieces = groups[g][1]
        m = None
        for j, (_, i, _) in enumerate(pieces):
            mi = jnp.max(z[c][j], axis=-1, keepdims=True)
            if pens[i] is not None:
                mi = mi + pens[i]
            m = mi if m is None else jnp.maximum(m, mi)
        a = None
        for j, (piece, i, _) in enumerate(pieces):
            shift = m if pens[i] is None else m - pens[i]
            p = jnp.exp2(z[c][j] - shift).astype(BF16)
            v = piece.values(h // 2)
            ones = jnp.ones_like(v)
            first = piece.head_lanes()
            v = jnp.where(first, v, ones) if h % 2 == 0 else jnp.where(first, ones, v)
            pv = piece.weighted(p, v)
            a = pv if a is None else a + pv
        acc[h, g] = a
    per_head = [acc[h, 0] if len(groups) == 1 else jnp.concatenate([acc[h, g] for g in range(len(groups))], 0)
                for h in heads]
    for p in range(n_pairs):
        a0, a1 = per_head[2 * p], per_head[2 * p + 1]
        o_ref[:, _pair_cols(p)] = jnp.where(lo, a0 / pltpu.roll(a0, HEAD_DIM, axis=1),
                                            a1 / pltpu.roll(a1, HEAD_DIM, axis=1)).astype(o_ref.dtype)


def _band_quarters(kv, tq):
    own, back1, back2 = kv
    half = tq // 2
    assert own.n_keys == tq and half % CHUNK == 0 and 2 * (tq // CHUNK) >= C_LEFT_CHUNKS
    whole = slice(None)
    return [(slice(0, half), [(own.first(half), 0, slice(0, half)), (back1, 1, whole), (back2, 2, whole)]),
            (slice(half, tq), [(own, 0, whole), (back1, 1, whole), (back2.last(half), 2, slice(half, tq))])]


def _band_prompt_kernel(q_ref, k_ref, v_ref, bias_ref, o_ref, *, tq, tk, n_pairs, blocks_per_step):
    def one_block(s, carry):
        qi = pl.program_id(2) * blocks_per_step + s
        rows = pl.ds(pl.multiple_of(s * tq, tq), tq)
        q_heads, lo = _split_heads(q_ref.at[rows, :], n_pairs, tq)
        kv, pens = [], []
        for dj in range(3):
            j = qi - dj
            pens.append(None if dj == 0 else jnp.where(j >= 0, 0.0, NEG_BIG).astype(F32))
            keys = pl.ds(pl.multiple_of(jnp.maximum(j, 0) * tk, tk), tk)
            kv.append(_KV(k_ref[:, keys], v_ref[:, keys], True))
        _band_core(q_heads, lo, kv, pens, lambda h, i: bias_ref.at[h, i], o_ref.at[rows, :], n_pairs,
                   skip_hidden_quarters=True)
        return carry

    lax.fori_loop(0, blocks_per_step, one_block, 0)


def _band_sample_kernel(q_ref, kn_ref, vn_ref, kc_ref, vc_ref, bias_ref, o_ref, kpad_ref, vpad_ref,
                        *, ts, tk, n_cache_blocks, n_pairs):
    q_heads, lo = _split_heads(q_ref, n_pairs, ts)
    kpad_ref[...] = jnp.zeros_like(kpad_ref)
    vpad_ref[...] = jnp.zeros_like(vpad_ref)
    kpad_ref[:ts, :] = kn_ref[...]
    vpad_ref[:ts, :] = vn_ref[...]
    kv = [_KV(kpad_ref[...], vpad_ref[...], False)]
    for dj in range(1, n_cache_blocks + 1):
        keys = slice((n_cache_blocks - dj) * tk, (n_cache_blocks - dj + 1) * tk)
        kv.append(_KV(kc_ref[:, keys].astype(BF16), vc_ref[:, keys].astype(BF16), True))
    is_new_key = lax.broadcasted_iota(jnp.int32, (ts, tk), 1) < ts

    def bias(h, i):
        return jnp.where(is_new_key, bias_ref[h, 0], NEG_BIG) if i == 0 else bias_ref[h, i]

    _band_core(q_heads, lo, kv, [None] * len(kv), bias, o_ref, n_pairs)


def _band_attention_prompt(q, k, v, bias):
    b, t, _ = q.shape
    tq = tk = ATT_BLOCK
    assert t % tk == 0
    n_pairs = BAND_PAIRS_PER_STEP
    w = n_pairs * LANES
    per_step = math.gcd(t // tq, BAND_BLOCKS_PER_STEP)
    kern = functools.partial(_band_prompt_kernel, tq=tq, tk=tk, n_pairs=n_pairs, blocks_per_step=per_step)
    kv_spec = pl.BlockSpec((None, w, t), lambda bi, hp, qi: (bi, hp, 0))
    q_spec = pl.BlockSpec((None, per_step * tq, w), lambda bi, hp, qi: (bi, qi, hp))
    bias_spec = pl.BlockSpec((2 * n_pairs, 3, tq, tk), lambda bi, hp, qi: (hp, 0, 0, 0))
    return pl.pallas_call(
        kern,
        grid=(b, C_WIDTH // w, t // (per_step * tq)),
        in_specs=[q_spec, kv_spec, kv_spec, bias_spec],
        out_specs=q_spec,
        out_shape=jax.ShapeDtypeStruct(q.shape, BF16),
        compiler_params=_params("parallel", "parallel", "parallel"),
        name="band_attention",
    )(q, k, v, bias)


def _band_attention_sample(q, k_new, v_new, k_cache, v_cache, layer, bias):
    b, ts, _ = q.shape
    wc = k_cache.shape[3]
    tk = ATT_BLOCK
    n_cache_blocks = min(wc // tk, 2)
    assert wc % (n_cache_blocks * tk) == 0 and ts <= tk
    n_pairs = C_WIDTH // LANES
    w = n_pairs * LANES
    kern = functools.partial(_band_sample_kernel, ts=ts, tk=tk, n_cache_blocks=n_cache_blocks,
                             n_pairs=n_pairs)
    new_spec = pl.BlockSpec((None, ts, w), lambda bi, hp: (bi, 0, hp))
    cache_rows = n_cache_blocks * tk
    cache_spec = pl.BlockSpec((None, None, w, cache_rows),
                              lambda bi, hp: (layer, bi, hp, wc // cache_rows - 1))
    bias_spec = pl.BlockSpec((2 * n_pairs, 1 + n_cache_blocks, ts, tk), lambda bi, hp: (hp, 0, 0, 0))
    return pl.pallas_call(
        kern,
        grid=(b, C_WIDTH // w),
        in_specs=[new_spec, new_spec, new_spec, cache_spec, cache_spec, bias_spec],
        out_specs=new_spec,
        out_shape=jax.ShapeDtypeStruct(q.shape, BF16),
        scratch_shapes=[pltpu.VMEM((tk, w), BF16), pltpu.VMEM((tk, w), BF16)],
        compiler_params=_params("parallel", "parallel"),
        name="band_attention_sample",
    )(q, k_new, v_new, k_cache, v_cache, bias)


def _band_bias_kernel(g_ref, o_ref):
    rows = tk = ATT_BLOCK
    q_chunk = lax.broadcasted_iota(jnp.int32, (rows, tk), 0) // CHUNK
    k_chunk = lax.broadcasted_iota(jnp.int32, (rows, tk), 1) // CHUNK
    for dj in range(3):
        diff = dj * (tk // CHUNK) + q_chunk - k_chunk
        seen = (diff >= 0) & (diff <= C_LEFT_CHUNKS)
        for h in range(g_ref.shape[0]):
            g = jnp.broadcast_to(g_ref[h, dj], (rows, 2 * tk))
            tile = pltpu.roll(g, 0, axis=1, stride=1, stride_axis=0)[:, :tk]
            o_ref[h, dj] = jnp.where(seen, tile * LOG2E, NEG_BIG)


def _band_bias(rel_table):
    rows = tk = ATT_BLOCK
    c = jnp.arange(2 * tk, dtype=jnp.int32)
    u = jnp.where(c <= tk, -c, 2 * tk - c)
    idx = jnp.clip(jnp.arange(3, dtype=jnp.int32)[:, None] * tk + u[None, :], REL_MIN, REL_MAX) - REL_MIN
    g = rel_table[:, idx].astype(F32).reshape(C_HEADS, 3, 1, 2 * tk)
    return pl.pallas_call(
        _band_bias_kernel,
        grid=(C_HEADS // BIAS_HEADS_PER_STEP,),
        in_specs=[pl.BlockSpec((BIAS_HEADS_PER_STEP, 3, 1, 2 * tk), lambda h: (h, 0, 0, 0))],
        out_specs=pl.BlockSpec((BIAS_HEADS_PER_STEP, 3, rows, tk), lambda h: (h, 0, 0, 0)),
        out_shape=jax.ShapeDtypeStruct((C_HEADS, 3, rows, tk), F32),
        compiler_params=_params("parallel"),
        name="band_bias",
    )(g)


def _gla_pair_tile(q, k, v, la, states, emit):
    L = GLA_CHUNK
    chunks = range(q.shape[0] // L)
    row = lax.broadcasted_iota(jnp.int32, (L, L), 0)
    colm = lax.broadcasted_iota(jnp.int32, (L, L), 1)
    tri = (colm <= row).astype(BF16)
    causal = colm <= row
    lane = lax.broadcasted_iota(jnp.int32, (L, LANES), 1)
    sub = lax.broadcasted_iota(jnp.int32, (LANES, B_DV), 0)
    mine = [(lane >= h * B_DK) & (lane < (h + 1) * B_DK) for h in range(2)]
    mine_rows = [(sub >= h * B_DK) & (sub < (h + 1) * B_DK) for h in range(2)]
    rows = [slice(c * L, (c + 1) * L) for c in chunks]

    b = []
    for r in rows:
        g_hi, g_lo = _split_bf16(la[r, :])
        b.append(_dot(tri, g_hi) + _dot(tri, g_lo))
    yield
    qg, qg_h, kg, kd_t, decay = [], [], [], [], []
    for c, r in zip(chunks, rows):
        qg_c = q[r, :] * (B_DK ** -0.5) * jnp.exp(b[c])
        qg.append(qg_c.astype(BF16))
        qg_h.append([jnp.where(mine[h], qg_c, 0.0).astype(BF16) for h in range(2)])
        kg.append((k[r, :] * jnp.exp(-b[c])).astype(BF16))
        b_t = b[c].T
        b_last = b_t[:, L - 1:L]
        kd_t.append((k[r, :].T * jnp.exp(b_last - b_t)).astype(BF16))
        decay.append(jnp.exp(b_last))
    yield
    att =[[jnp.where(causal, _dot_nt(qg_h[c][h], kg[c]), 0.0).astype(BF16) for h in range(2)]
           for c in chunks]
    yield
    o_intra, own = [], []
    for c, r in zip(chunks, rows):
        v_h = [v[r, h * B_DV:(h + 1) * B_DV].astype(BF16) for h in range(2)]
        o_intra.append([_dot(att[c][h], v_h[h]) for h in range(2)])
        own.append([jnp.where(mine_rows[h], _dot(kd_t[c], v_h[h]), 0.0) for h in range(2)])

    yield
    states = list(states)
    start = []
    for c in chunks:
        start.append([s.astype(BF16) for s in states])
        states = [decay[c] * states[h] + own[c][h] for h in range(2)]
    yield
    for c in chunks:
        for h in range(2):
            emit(c, h, o_intra[c][h] + _dot(qg[c], start[c][h]))
    return states


def _pair_states(s0_ref, p):
    zeros_state = jnp.zeros((B_DK, B_DV), F32)
    return [jnp.concatenate([s0_ref[2 * p], zeros_state], axis=0),
            jnp.concatenate([zeros_state, s0_ref[2 * p + 1]], axis=0)]


def _pair_v_cols(p):
    return slice(2 * p * B_DV, 2 * (p + 1) * B_DV)


def _gla_rows_emit(o_ref, p):
    def emit(c, h, o):
        o_ref[c * GLA_CHUNK:(c + 1) * GLA_CHUNK, (2 * p + h) * B_DV:(2 * p + h + 1) * B_DV] = o
    return emit


def _gla_kernel(q_ref, k_ref, v_ref, la_ref, s0_ref, o_ref, s_out_ref):
    pairs = range(B_HEADS // 2)
    states = _interleave([
        _gla_pair_tile(q_ref[:, _pair_cols(p)], k_ref[:, _pair_cols(p)], v_ref[:, _pair_v_cols(p)],
                       la_ref[:, _pair_cols(p)], _pair_states(s0_ref, p), _gla_rows_emit(o_ref, p))
        for p in pairs])
    for p in pairs:
        s_out_ref[2 * p] = states[p][0][:B_DK, :]
        s_out_ref[2 * p + 1] = states[p][1][B_DK:, :]


def _gla(q, k, v, la, s0):
    b, t, _ = q.shape
    assert t % GLA_CHUNK == 0 and t <= GLA_MAX_TILE
    qk_spec = pl.BlockSpec((None, t, B_KW), lambda bi: (bi, 0, 0))
    v_spec = pl.BlockSpec((None, t, B_VW), lambda bi: (bi, 0, 0))
    s_spec = pl.BlockSpec((None, B_HEADS, B_DK, B_DV), lambda bi: (bi, 0, 0, 0))
    return pl.pallas_call(
        _gla_kernel,
        grid=(b,),
        in_specs=[qk_spec, qk_spec, v_spec, qk_spec, s_spec],
        out_specs=[v_spec, s_spec],
        out_shape=[jax.ShapeDtypeStruct(v.shape, F32), jax.ShapeDtypeStruct(s0.shape, F32)],
        compiler_params=_params("parallel"),
        name="gla",
    )(q, k, v, la, s0)


def _pad_rows(x, n):
    return jnp.pad(x, ((0, 0), (0, n - x.shape[1]), (0, 0)))


def _heads_last(x, heads):
    n, b, _, s = x.shape
    return jnp.transpose(x.reshape(n, b, heads, HEAD_DIM, s), (0, 1, 4, 2, 3))


def _feature_major(cache):
    n, b, s, heads, hd = cache.shape
    return jnp.transpose(cache, (0, 1, 3, 4, 2)).reshape(n, b, heads * hd, s)


def _row_tile(m):
    for tm in (512, 256, 128, 64, 32, 16, 8):
        if m % tm == 0:
            return tm
    raise ValueError(f"token count {m} is not a multiple of 8")


def kernel(x_prompt, x_sample, cache_a_k, cache_a_v, state_b, cache_c_k, cache_c_v, norm_mix_g, norm_ffn_g, w_in_ab, w_gate_b, b_gate_b, norm_gla_g, w_out_ab, w_qkv_c, rel_bias_c, w_out_c, w_ffn_gate, w_ffn_up, w_ffn_down, norm_final_g):
    bp, tp, d = x_prompt.shape
    bs, ts, _ = x_sample.shape
    depth = norm_mix_g.shape[0]
    past = cache_a_k.shape[2]
    wc = cache_c_k.shape[2]
    assert tp % ATT_BLOCK == 0 and past % ATT_BLOCK == 0 and wc % ATT_BLOCK == 0
    assert ts <= GLA_CHUNK and ts % 8 == 0
    mp, ms = bp * tp, bs * ts
    tmp, tms = _row_tile(tp), _row_tile(ms)
    xp = x_prompt.reshape(mp, d)
    xs = x_sample.reshape(ms, d)
    row2 = lambda a: a.reshape(1, -1)

    a_ks, a_vs, b_sp, b_ss, c_ks, c_vs = [], [], [], [], [], []
    a_kv_prompt = c_kv_prompt = None
    n_ab, n_c = (depth + 1) // 2, depth // 2
    keep = min(C_LEFT_CHUNKS * CHUNK, tp)

    kv0, kv1 = A_WIDTH, 3 * A_WIDTH
    o = 3 * A_WIDTH + 2 * B_KW + B_VW
    w_in_t = jnp.swapaxes(w_in_ab, 1, 2)
    w_main_all = jnp.concatenate([w_in_t[:, :kv0], w_in_t[:, kv1:o], w_in_t[:, o + B_GATE_RANK:]],
                                 axis=1).astype(BF16)
    w_kv_ab_t_all = w_in_t[:, kv0:kv1].astype(BF16)
    w_lr_all = jnp.pad(w_in_t[:, o:o + B_GATE_RANK],
                       ((0, 0), (0, LANES - B_GATE_RANK), (0, 0))).astype(BF16)
    w_gate_all = jnp.pad(w_gate_b, ((0, 0), (0, LANES - B_GATE_RANK), (0, 0))).astype(BF16)
    w_out_ab_all = w_out_ab.astype(BF16)
    w_q_all = w_qkv_c[:, :, :C_WIDTH].astype(BF16)
    w_kv_c_t_all = jnp.swapaxes(w_qkv_c[:, :, C_WIDTH:], 1, 2).astype(BF16)
    w_out_c_all = w_out_c.astype(BF16)
    cache_a_k_fm, cache_a_v_fm, cache_c_k_fm, cache_c_v_fm = (
        _feature_major(c) for c in (cache_a_k, cache_a_v, cache_c_k, cache_c_v))

    for layer in range(depth):
        i = layer // 2
        g_mix = row2(norm_mix_g[layer])
        riders = _Riders((w_ffn_gate, w_ffn_up, w_ffn_down), layer, mp // tmp)
        ffn_ends = (row2(norm_ffn_g[layer]), row2(norm_final_g) if layer == depth - 1 else None)
        if layer % 2 == 0:
            w_main, w_kv_t, w_lr, w_gate, w_out = (
                _Slab(w, i) for w in (w_main_all, w_kv_ab_t_all, w_lr_all, w_gate_all, w_out_ab_all))
            b_gate = row2(b_gate_b[i])
            g_gla = row2(norm_gla_g[i])

            qa, ka, va, kab, vab, qb, kb, vb, r, la, *ffn_w = _proj_ab(
                xp, g_mix, w_main, w_kv_t, w_lr, w_gate, b_gate, tmp, batch=bp,
                stack=_KVStack(i, n_ab, tp, a_kv_prompt), riders=riders)
            ffn = (ffn_ends[0], *ffn_w, ffn_ends[1])
            a_kv_prompt = (ka, va)
            sh = lambda a: a.reshape(bp, tp, -1)
            oa, ob, sbp = _mixer_ab_prompt(sh(qa), kab, vab, sh(qb), sh(kb), sh(vb), sh(la))
            mixed_p = (oa.reshape(mp, -1), ob.reshape(mp, -1), r)
            b_sp.append(sbp)

            qa, ka, va, kab, vab, qb, kb, vb, r, la = _proj_ab(
                xs, g_mix, w_main, w_kv_t, w_lr, w_gate, b_gate, tms)
            sh = lambda a: a.reshape(bs, ts, -1)
            oa = _sb_attention_sample(sh(qa), sh(kab), sh(vab), cache_a_k_fm, cache_a_v_fm, i)
            pad_t = lambda a: _pad_rows(sh(a), GLA_CHUNK)
            ob, sbs = _gla(pad_t(qb), pad_t(kb), pad_t(vb), pad_t(la), state_b[i])
            mixed_s = (oa.reshape(ms, -1), ob[:, :ts].reshape(ms, -1), r)
            xp, xs = _layer_tail(xp, mixed_p, xs, mixed_s, (g_gla, w_out), *ffn, tmp)
            a_ks.append(ka.reshape(bs, ts, A_HEADS, HEAD_DIM))
            a_vs.append(va.reshape(bs, ts, A_HEADS, HEAD_DIM))
            b_ss.append(sbs)
        else:
            w_q, w_kv_t, w_out = (_Slab(w, i) for w in (w_q_all, w_kv_c_t_all, w_out_c_all))

            q, k, v, kb16, vb16, *ffn_w = _proj_c(xp, g_mix, w_q, w_kv_t, tmp, batch=bp,
                                                  stack=_KVStack(i, n_c, keep, c_kv_prompt), riders=riders)
            ffn = (ffn_ends[0], *ffn_w, ffn_ends[1])
            c_kv_prompt = (k, v)
            bias = _band_bias(rel_bias_c[i])
            oc = _band_attention_prompt(q.reshape(bp, tp, -1), kb16, vb16, bias)
            mixed_p = (oc.reshape(mp, -1),)

            q, k, v, kb16, vb16 = _proj_c(xs, g_mix, w_q, w_kv_t, tms)
            sh = lambda a: a.reshape(bs, ts, -1)
            oc = _band_attention_sample(sh(q), sh(kb16), sh(vb16), cache_c_k_fm, cache_c_v_fm, i, bias)
            xp, xs = _layer_tail(xp, mixed_p, xs, (oc.reshape(ms, -1),), (w_out,), *ffn, tmp)
            c_ks.append(k.reshape(bs, ts, C_HEADS, HEAD_DIM))
            c_vs.append(v.reshape(bs, ts, C_HEADS, HEAD_DIM))

    y_prompt = xp.reshape(bp, tp, d)
    y_sample = xs.reshape(bs, ts, d)
    a_kp, a_vp = (_heads_last(a, A_HEADS) for a in a_kv_prompt)
    c_kp, c_vp = (_heads_last(a, C_HEADS) for a in c_kv_prompt)
    return (y_prompt, y_sample, a_kp, a_vp, jnp.stack(a_ks), jnp.stack(a_vs),
            jnp.stack(b_sp), jnp.stack(b_ss), c_kp, c_vp, jnp.stack(c_ks), jnp.stack(c_vs))
```

```python
import functools
import math

import jax
import jax.numpy as jnp
from jax import lax
from jax.experimental import pallas as pl
from jax.experimental.pallas import tpu as pltpu

F32 = jnp.float32
BF16 = jnp.bfloat16

EPS = 1e-6
HEAD_DIM = 64
LANES = 128
A_HEADS = 8
A_WIDTH = A_HEADS * HEAD_DIM
B_HEADS = 4
B_DK = 64
B_DV = 128
B_KW = B_HEADS * B_DK
B_VW = B_HEADS * B_DV
B_GATE_RANK = 16
B_GATE_TEMP = 16.0
GLA_CHUNK = 64
C_HEADS = 16
C_WIDTH = C_HEADS * HEAD_DIM
CHUNK = 64
C_LEFT_CHUNKS = 8
REL_MIN = -(CHUNK - 1)
REL_MAX = 128
ATT_BLOCK = 256
NEG_BIG = -1e30
LOG2E = 1.4426950408889634
SB_DEAD = 104.0
SB_PAIRS_PER_STEP = 4
BAND_PAIRS_PER_STEP = 4
BIAS_HEADS_PER_STEP = 4
MIXER_BLOCKS_PER_STEP = 4
BAND_BLOCKS_PER_STEP = 8
GLA_MAX_TILE = 512
VMEM_LIMIT = 56 * 1024 * 1024


def _params(*sem):
    return pltpu.CompilerParams(dimension_semantics=sem, vmem_limit_bytes=VMEM_LIMIT)


class _Slab:
    def __init__(self, stacked, index):
        self.stacked, self.index, self.shape = stacked, index, stacked.shape[1:]


def _operand(a):
    return a.stacked if isinstance(a, _Slab) else a


def _resident(a):
    if isinstance(a, _Slab):
        index = (a.index,) + (0,) * len(a.shape)
        return pl.BlockSpec((None, *a.shape), lambda *_: index, pipeline_mode=pl.Buffered(1))
    return pl.BlockSpec(a.shape, lambda *_: (0,) * a.ndim, pipeline_mode=pl.Buffered(1))


def _rms(x, g):
    return x * lax.rsqrt(jnp.mean(x * x, axis=-1, keepdims=True) + EPS) * g


def _log_sigmoid_pair(z):
    l = jnp.log1p(jnp.exp(-jnp.abs(z)))
    return jnp.minimum(z, 0.0) - l, jnp.minimum(-z, 0.0) - l


def _split_bf16(x):
    hi = x.astype(BF16)
    lo = (x - hi.astype(F32)).astype(BF16)
    return hi, lo


def _dot(a, b):
    return jnp.dot(a, b, preferred_element_type=F32)


def _dot_nt(a, b):
    return lax.dot_general(a, b, (((1,), (1,)), ((), ())), preferred_element_type=F32)


def _dot_tn(a, b):
    return lax.dot_general(a, b, (((0,), (0,)), ((), ())), preferred_element_type=F32)


def _pair_cols(p):
    return slice(p * LANES, (p + 1) * LANES)


class _KV:
    def __init__(self, k, v, feature_major):
        self.k, self.v, self.feature_major = k, v, feature_major
        self.n_keys = k.shape[1] if feature_major else k.shape[0]

    def scores(self, q_h, p):
        if self.feature_major:
            return _dot(q_h, self.k[_pair_cols(p), :])
        return _dot_nt(q_h, self.k[:, _pair_cols(p)])

    def first(self, n):
        if self.feature_major:
            return _KV(self.k[:, :n], self.v[:, :n], True)
        return _KV(self.k[:n], self.v[:n], False)

    def last(self, n):
        if self.feature_major:
            return _KV(self.k[:, -n:], self.v[:, -n:], True)
        return _KV(self.k[-n:], self.v[-n:], False)

    def values(self, p):
        return self.v[_pair_cols(p), :] if self.feature_major else self.v[:, _pair_cols(p)]

    def weighted(self, w, v_p):
        return _dot_nt(w, v_p) if self.feature_major else _dot(w, v_p)

    def head_lanes(self):
        shape = (LANES, self.n_keys) if self.feature_major else (self.n_keys, LANES)
        return lax.broadcasted_iota(jnp.int32, shape, 0 if self.feature_major else 1) < HEAD_DIM


def _split_heads(q_ref, n_pairs, tq):
    lo = lax.broadcasted_iota(jnp.int32, (tq, LANES), 1) < HEAD_DIM
    heads = []
    for p in range(n_pairs):
        q = q_ref[:, _pair_cols(p)]
        heads += [jnp.where(lo, q, jnp.zeros_like(q)), jnp.where(lo, jnp.zeros_like(q), q)]
    return heads, lo


def _emit_kv(y, wkv_ref, k_ref, v_ref, kb_ref, vb_ref, feature_major, kept_tail=None, own_slot=0):
    if not feature_major:
        kv = _dot_nt(y, wkv_ref[...])
        width = kv.shape[1] // 2
        k, v = kv[:, :width], kv[:, width:]
        k_ref[...], v_ref[...], kb_ref[...], vb_ref[...] = k, v, k.astype(BF16), v.astype(BF16)
        return
    kv = _dot_nt(wkv_ref[...], y)
    width = kv.shape[0] // 2
    k, v = kv[:width, :], kv[width:, :]
    kb_ref[...] = k.astype(BF16)
    vb_ref[...] = v.astype(BF16)
    slot = own_slot
    other_slots = [s for s in range(k_ref.shape[0]) if s != slot]
    if kept_tail is None:
        fill, fill_cols = slice(None), k_ref.shape[2]
        k_ref[slot] = k
        v_ref[slot] = v
    else:
        n_tiles, n_kept = kept_tail
        tile = pl.program_id(0) % n_tiles
        tm = k.shape[1]
        cols = pl.ds(pl.multiple_of(jnp.maximum(tile - (n_tiles - n_kept), 0) * tm, tm), tm)
        k_ref[slot, :, cols] = k
        v_ref[slot, :, cols] = v
        share = k_ref.shape[2] // n_tiles
        if share % LANES == 0 and share * n_tiles == k_ref.shape[2]:
            fill, fill_cols = pl.ds(pl.multiple_of(tile * share, share), share), share
        else:
            fill, fill_cols = slice(None), k_ref.shape[2]
    for dst in (k_ref, v_ref):
        for s in other_slots:
            dst[s, :, fill] = jnp.zeros((dst.shape[1], fill_cols), dst.dtype)


class _Riders:
    def __init__(self, stacks, layer, grid):
        self.stacks, self.layer, self.grid = list(stacks), layer, tuple(grid)
        self.chunks = next(c for c in (16, 8, 4, 2, 1)
                           if c <= math.prod(grid) and all(w.shape[1] % (16 * c) == 0 for w in self.stacks))

    def _chunk(self, *idx):
        step = 0
        for i, n in zip(idx, self.grid):
            step = step * n + i
        return jnp.minimum(step, self.chunks - 1)

    def specs(self):
        layer = self.layer
        ins = [pl.BlockSpec((None, w.shape[1] // self.chunks, w.shape[2]),
                            lambda *idx: (layer, self._chunk(*idx), 0)) for w in self.stacks]
        outs = [pl.BlockSpec((w.shape[1] // self.chunks, w.shape[2]),
                             lambda *idx: (self._chunk(*idx), 0)) for w in self.stacks]
        shapes = [jax.ShapeDtypeStruct(w.shape[1:], BF16) for w in self.stacks]
        return ins, outs, shapes


def _cast_riders(pairs):
    for src, dst in pairs:
        dst[...] = src[...].astype(BF16)


def _proj_ab_kernel(x_ref, g_ref, w_ref, wkv_ref, wlr_ref, wgate_ref, bgate_ref, *refs,
                    feature_major, n_aliased, own_slot):
    qa_ref, ka_ref, va_ref, kab_ref, vab_ref, qb_ref, kb_ref, vb_ref, r_ref, la_ref = refs[n_aliased:]
    y = _rms(x_ref[...], g_ref[...]).astype(BF16)
    _emit_kv(y, wkv_ref, ka_ref, va_ref, kab_ref, vab_ref, feature_major, own_slot=own_slot)
    z = _dot_nt(y, w_ref[...])
    c = 0
    qa_ref[...] = (z[:, c:c + A_WIDTH] * (HEAD_DIM ** -0.5)).astype(BF16); c += A_WIDTH
    qb_ref[...] = z[:, c:c + B_KW]; c += B_KW
    kb_ref[...] = z[:, c:c + B_KW]; c += B_KW
    vb_ref[...] = z[:, c:c + B_VW]; c += B_VW
    r_ref[...] = z[:, c:c + B_VW]
    g_lr = _dot_nt(y, wlr_ref[...])
    gate = _dot(g_lr.astype(BF16), wgate_ref[...]) + bgate_ref[...]
    la_ref[...] = _log_sigmoid_pair(gate)[0] * (1.0 / B_GATE_TEMP)


class _KVStack:
    def __init__(self, layer, n_layers, keep, previous=None):
        self.layer, self.n_layers, self.keep, self.previous = layer, n_layers, keep, previous


def _kv_out(m, width, tm, batch, stack):
    if batch is None:
        spec = pl.BlockSpec((tm, width), lambda i: (i, 0))
        return [spec] * 4, [jax.ShapeDtypeStruct((m, width), dt) for dt in (F32, F32, BF16, BF16)], None, 0
    t = m // batch
    assert t % tm == 0 and stack.keep % tm == 0 and stack.keep <= t
    n_tiles = t // tm
    copy_spec = pl.BlockSpec((None, width, tm), lambda i: (i // n_tiles, 0, i % n_tiles))
    copy_shape = jax.ShapeDtypeStruct((batch, width, t), BF16)
    slots, first_slot, own_slot = ((stack.n_layers, 0, stack.layer) if stack.previous is None
                                   else (1, stack.layer, 0))
    if stack.keep == t:
        f32_spec = pl.BlockSpec((slots, None, width, tm), lambda i: (first_slot, i // n_tiles, 0, i % n_tiles))
        kept_tail = None
    else:
        f32_spec = pl.BlockSpec((slots, None, width, stack.keep), lambda i: (first_slot, i // n_tiles, 0, 0))
        kept_tail = (n_tiles, stack.keep // tm)
    f32_shape = jax.ShapeDtypeStruct((stack.n_layers, batch, width, stack.keep), F32)
    return ([f32_spec, f32_spec, copy_spec, copy_spec], [f32_shape, f32_shape, copy_shape, copy_shape],
            kept_tail, own_slot)


def _aliased_stack(stack, n_inputs, first_output):
    if stack is None or stack.previous is None:
        return [], [], {}
    prev = list(stack.previous)
    specs = [pl.BlockSpec(memory_space=pl.ANY)] * len(prev)
    return prev, specs, {n_inputs + j: first_output + j for j in range(len(prev))}


def _proj_ab(x, g, w_main, w_kv, w_lr, w_gate, b_gate, tm, batch=None, stack=None):
    m, d = x.shape
    row = lambda n: pl.BlockSpec((tm, n), lambda i: (i, 0))
    params = (g, w_main, w_kv, w_lr, w_gate, b_gate)
    kv_specs, kv_shapes, kept_tail, own_slot = _kv_out(m, A_WIDTH, tm, batch, stack)
    assert kept_tail is None
    prev, prev_specs, aliases = _aliased_stack(stack, 1 + len(params), 1)
    rest = [(B_KW, F32), (B_KW, F32), (B_VW, F32), (B_VW, F32), (B_KW, F32)]
    return pl.pallas_call(
        functools.partial(_proj_ab_kernel, feature_major=batch is not None, n_aliased=len(prev),
                          own_slot=own_slot),
        grid=(m // tm,),
        in_specs=[row(d)] + [_resident(a) for a in params] + prev_specs,
        out_specs=[row(A_WIDTH)] + kv_specs + [row(n) for n, _ in rest],
        out_shape=([jax.ShapeDtypeStruct((m, A_WIDTH), BF16)] + kv_shapes
                   + [jax.ShapeDtypeStruct((m, n), dt) for n, dt in rest]),
        input_output_aliases=aliases,
        compiler_params=_params("parallel"),
        name="proj_ab",
    )(x, *[_operand(a) for a in params], *prev)


def _proj_c_kernel(x_ref, g_ref, wq_ref, wkv_ref, *refs, feature_major, n_aliased, kept_tail, own_slot):
    q_ref, k_ref, v_ref, kb_ref, vb_ref = refs[n_aliased:]
    y = _rms(x_ref[...], g_ref[...]).astype(BF16)
    _emit_kv(y, wkv_ref, k_ref, v_ref, kb_ref, vb_ref, feature_major, kept_tail, own_slot)
    q_ref[...] = (_dot(y, wq_ref[...]) * (HEAD_DIM ** -0.5 * LOG2E)).astype(BF16)


def _proj_c(x, g, w_q, w_kv, tm, batch=None, stack=None):
    m, d = x.shape
    row = lambda n: pl.BlockSpec((tm, n), lambda i: (i, 0))
    params = (g, w_q, w_kv)
    kv_specs, kv_shapes, kept_tail, own_slot = _kv_out(m, C_WIDTH, tm, batch, stack)
    prev, prev_specs, aliases = _aliased_stack(stack, 1 + len(params), 1)
    return pl.pallas_call(
        functools.partial(_proj_c_kernel, feature_major=batch is not None, n_aliased=len(prev),
                          kept_tail=kept_tail, own_slot=own_slot),
        grid=(m // tm,),
        in_specs=[row(d)] + [_resident(a) for a in params] + prev_specs,
        out_specs=[row(C_WIDTH)] + kv_specs,
        out_shape=[jax.ShapeDtypeStruct((m, C_WIDTH), BF16)] + kv_shapes,
        input_output_aliases=aliases,
        compiler_params=_params("arbitrary" if kept_tail else "parallel"),
        name="proj_c",
    )(x, *[_operand(a) for a in params], *prev)


def _layer_tail_rows(x_ref, mixer_refs, param_refs, o_ref, gla_merge, final_norm):
    if gla_merge:
        oa_ref, ob_ref, r_ref = mixer_refs
        ggla_ref, wout_ref, *param_refs = param_refs
        ob = ob_ref[...]
        parts = []
        for h in range(B_HEADS):
            seg = ob[:, h * B_DV:(h + 1) * B_DV]
            parts.append(seg * lax.rsqrt(jnp.mean(seg * seg, axis=-1, keepdims=True) + EPS))
        r = r_ref[...]
        obn = jnp.concatenate(parts, axis=-1) * ggla_ref[...] * (r * jax.nn.sigmoid(r))
        mix = _dot(oa_ref[...], wout_ref[:A_WIDTH, :]) + _dot(obn.astype(BF16), wout_ref[A_WIDTH:, :])
    else:
        (oc_ref,) = mixer_refs
        wout_ref, *param_refs = param_refs
        mix = _dot(oc_ref[...], wout_ref[...])
    gffn_ref, wg_ref, wu_ref, wd_ref, *param_refs = param_refs
    x = x_ref[...] + mix
    y = _rms(x, gffn_ref[...]).astype(BF16)
    h = _dot(y, wg_ref[...])
    u = _dot(y, wu_ref[...])
    a = (h * jax.nn.sigmoid(h) * u).astype(BF16)
    x = x + _dot(a, wd_ref[...])
    o_ref[...] = _rms(x, param_refs[0][...]) if final_norm else x


def _layer_tail_kernel(*refs, n_mixer, n_params, gla_merge, final_norm, main_steps):
    main, extra = refs[:1 + n_mixer], refs[1 + n_mixer:2 + 2 * n_mixer]
    params = refs[2 + 2 * n_mixer:2 + 2 * n_mixer + n_params]
    o_ref, o_extra_ref = refs[-2:]

    @pl.when(pl.program_id(0) < main_steps)
    def _():
        _layer_tail_rows(main[0], main[1:], params, o_ref, gla_merge, final_norm)

    @pl.when(pl.program_id(0) == main_steps)
    def _():
        _layer_tail_rows(extra[0], extra[1:], params, o_extra_ref, gla_merge, final_norm)


def _layer_tail(x, mixer_out, x_extra, mixer_out_extra, mixer_params, g_ffn, wg, wu, wd, g_fin, tm):
    m, d = x.shape
    main_steps = m // tm
    row = lambda a: pl.BlockSpec((tm, a.shape[1]), lambda i: (jnp.minimum(i, main_steps - 1), 0))
    whole = lambda a: pl.BlockSpec(a.shape, lambda i: (0, 0))
    params = [*mixer_params, g_ffn, wg, wu, wd] + ([] if g_fin is None else [g_fin])
    return pl.pallas_call(
        functools.partial(_layer_tail_kernel, n_mixer=len(mixer_out), n_params=len(params),
                          gla_merge=len(mixer_out) == 3, final_norm=g_fin is not None, main_steps=main_steps),
        grid=(main_steps + 1,),
        in_specs=([row(x)] + [row(a) for a in mixer_out] + [whole(x_extra)]
                  + [whole(a) for a in mixer_out_extra] + [_resident(a) for a in params]),
        out_specs=[row(x), whole(x_extra)],
        out_shape=[jax.ShapeDtypeStruct((m, d), F32), jax.ShapeDtypeStruct(x_extra.shape, F32)],
        compiler_params=_params("arbitrary"),
        name="layer_tail",
    )(x, *mixer_out, x_extra, *mixer_out_extra, *[_operand(a) for a in params])


def _interleave(stage_generators):
    results = [None] * len(stage_generators)
    live = list(range(len(stage_generators)))
    while live:
        for i in list(live):
            try:
                next(stage_generators[i])
            except StopIteration as done:
                results[i] = done.value
                live.remove(i)
    return results


def _sb_core(q_heads, lo, first_kv, first_mask, earlier_kv, n_earlier, acc_ref, c_ref, o_ref, n_pairs,
             companions=(), on_companions=None):
    tk = first_kv.n_keys
    heads = range(2 * n_pairs)
    later = (lax.broadcasted_iota(jnp.int32, (tk, tk), 0)
             > lax.broadcasted_iota(jnp.int32, (tk, tk), 1)).astype(BF16)
    c_ref[...] = jnp.zeros_like(c_ref)

    tq = q_heads[0].shape[0]

    def row_groups(kv, mask):
        if mask is None or tq != tk or tq % (2 * LANES):
            return [(slice(None), kv, mask, later)]
        half = tq // 2
        return [(slice(0, half), kv.first(half), mask[:half, :half], later[:half, :half]),
                (slice(half, tq), kv, mask[half:, :], later)]

    def block(kv, mask):
        groups = row_groups(kv, mask)
        chains = [(h, g) for h in heads for g in range(len(groups))]
        z = [groups[g][1].scores(q_heads[h][groups[g][0], :], h // 2) for h, g in chains]
        yield
        log_beta, drop, after = [], [], []
        for i, (h, g) in enumerate(chains):
            d = jnp.maximum(z[i], 0.0) + jnp.log(1.0 + jnp.exp(-jnp.abs(z[i])))
            log_beta.append(z[i] - d)
            drop.append(d if mask is None else jnp.where(groups[g][2], d, 0.0))
        yield
        for i, (h, g) in enumerate(chains):
            hi, lo_part = _split_bf16(drop[i])
            tri = groups[g][3]
            after.append(_dot(hi, tri) + _dot(lo_part, tri))
        yield
        pv = {}
        for i, (h, g) in enumerate(chains):
            rows, kv_g = groups[g][0], groups[g][1]
            c = c_ref[h, rows, :]
            w = jnp.exp(log_beta[i] - after[i] - c)
            if mask is not None:
                w = jnp.where(groups[g][2], w, 0.0)
            c_ref[h, rows, :] = c + after[i][:, 0:1] + drop[i][:, 0:1]
            pv[h, g] = kv_g.weighted(w.astype(BF16), kv_g.values(h // 2))
        yield
        per_head = [pv[h, 0] if len(groups) == 1 else jnp.concatenate([pv[h, g] for g in range(len(groups))], 0)
                    for h in heads]
        out = [jnp.where(lo, per_head[2 * p], per_head[2 * p + 1]) for p in range(n_pairs)]
        return out[0] if n_pairs == 1 else jnp.concatenate(out, axis=-1)

    def all_dead():
        return jnp.min(c_ref[...]) > SB_DEAD

    first, *companion_results = _interleave([block(first_kv, first_mask), *companions])
    acc_ref[...] = first
    if companions:
        on_companions(companion_results)

    def cond(carry):
        n, dead = carry
        return (n < n_earlier) & jnp.logical_not(dead)

    def body(carry):
        n, _ = carry
        acc_ref[...] += _interleave([block(earlier_kv(n), None)])[0]
        return n + 1, all_dead()

    lax.while_loop(cond, body, (jnp.int32(0), jnp.asarray(False)))
    o_ref[...] = acc_ref[...].astype(o_ref.dtype)


def _mixer_ab_prompt_kernel(q_ref, k_ref, v_ref, qb_ref, kb_ref, vb_ref, la_ref, *refs,
                            tb, n_pairs, blocks_per_step, n_riders):
    o_ref, ob_ref, s_out_ref = refs[n_riders:n_riders + 3]
    acc_ref, c_ref, carry_ref = refs[3 + 2 * n_riders:]
    _cast_riders(list(zip(refs[:n_riders], refs[n_riders + 3:3 + 2 * n_riders])))
    pairs = range(B_HEADS // 2)
    strictly_earlier = (lax.broadcasted_iota(jnp.int32, (tb, tb), 1)
                        < lax.broadcasted_iota(jnp.int32, (tb, tb), 0))

    def kv_block(j):
        keys = pl.ds(pl.multiple_of(j * tb, tb), tb)
        return _KV(k_ref[:, keys], v_ref[:, keys], True)

    def keep_states(states):
        for p in pairs:
            carry_ref[2 * p], carry_ref[2 * p + 1] = states[p]
            s_out_ref[2 * p] = states[p][0][:B_DK, :]
            s_out_ref[2 * p + 1] = states[p][1][B_DK:, :]

    @pl.when(pl.program_id(1) == 0)
    def _():
        carry_ref[...] = jnp.zeros_like(carry_ref)

    def one_block(s, carry):
        qi = pl.program_id(1) * blocks_per_step + s
        rows = pl.ds(pl.multiple_of(s * tb, tb), tb)
        gla = [_gla_pair_tile(qb_ref[rows, _pair_cols(p)], kb_ref[rows, _pair_cols(p)],
                              vb_ref[rows, _pair_v_cols(p)], la_ref[rows, _pair_cols(p)],
                              [carry_ref[2 * p], carry_ref[2 * p + 1]],
                              _gla_rows_emit(ob_ref.at[rows, :], p)) for p in pairs]
        q_heads, lo = _split_heads(q_ref.at[rows, :], n_pairs, tb)
        _sb_core(q_heads, lo, kv_block(qi), strictly_earlier, lambda n: kv_block(qi - 1 - n), qi,
                 acc_ref, c_ref, o_ref.at[rows, :], n_pairs, companions=gla, on_companions=keep_states)
        return carry

    lax.fori_loop(0, blocks_per_step, one_block, 0)


def _sb_sample_kernel(q_ref, kn_ref, vn_ref, kc_ref, vc_ref, o_ref, acc_ref, c_ref, kpad_ref, vpad_ref,
                      *, ts, tk, n_cache_blocks, n_pairs):
    q_heads, lo = _split_heads(q_ref, n_pairs, ts)
    kpad_ref[...] = jnp.zeros_like(kpad_ref)
    vpad_ref[...] = jnp.zeros_like(vpad_ref)
    kpad_ref[:ts, :] = kn_ref[...]
    vpad_ref[:ts, :] = vn_ref[...]

    def cache_block(n):
        keys = pl.ds(pl.multiple_of((n_cache_blocks - 1 - n) * tk, tk), tk)
        return _KV(kc_ref[:, keys].astype(BF16), vc_ref[:, keys].astype(BF16), True)

    strictly_earlier = (lax.broadcasted_iota(jnp.int32, (ts, tk), 1)
                        < lax.broadcasted_iota(jnp.int32, (ts, tk), 0))
    _sb_core(q_heads, lo, _KV(kpad_ref[...], vpad_ref[...], False), strictly_earlier, cache_block,
             n_cache_blocks, acc_ref, c_ref, o_ref, n_pairs)


def _mixer_ab_prompt(q, k, v, qb, kb, vb, la, ride_stacks, layer):
    b, t, _ = q.shape
    tb = ATT_BLOCK
    assert t % tb == 0 and tb % GLA_CHUNK == 0 and tb <= GLA_MAX_TILE
    n_pairs = A_WIDTH // LANES
    per_step = math.gcd(t // tb, MIXER_BLOCKS_PER_STEP)
    rows = lambda width: pl.BlockSpec((None, per_step * tb, width), lambda bi, qi: (bi, qi, 0))
    kv_spec = pl.BlockSpec((None, A_WIDTH, t), lambda bi, qi: (bi, 0, 0))
    state = (B_HEADS, B_DK, B_DV)
    grid = (b, t // (per_step * tb))
    ride_in, ride_out, ride_shapes = _Riders(ride_stacks, layer, grid).specs()
    return pl.pallas_call(
        functools.partial(_mixer_ab_prompt_kernel, tb=tb, n_pairs=n_pairs, blocks_per_step=per_step,
                          n_riders=len(ride_stacks)),
        grid=grid,
        in_specs=[rows(A_WIDTH), kv_spec, kv_spec, rows(B_KW), rows(B_KW), rows(B_VW), rows(B_KW)] + ride_in,
        out_specs=[rows(A_WIDTH), rows(B_VW),
                   pl.BlockSpec((None, *state), lambda bi, qi: (bi, 0, 0, 0))] + ride_out,
        out_shape=[jax.ShapeDtypeStruct(q.shape, BF16), jax.ShapeDtypeStruct(vb.shape, F32),
                   jax.ShapeDtypeStruct((b, *state), F32)] + ride_shapes,
        scratch_shapes=[pltpu.VMEM((tb, A_WIDTH), F32), pltpu.VMEM((2 * n_pairs, tb, 1), F32),
                        pltpu.VMEM((B_HEADS, LANES, B_DV), F32)],
        compiler_params=_params("arbitrary", "arbitrary"),
        name="mixer_ab",
    )(q, k, v, qb, kb, vb, la, *ride_stacks)


def _sb_attention_sample(q, k_new, v_new, k_cache, v_cache, layer):
    b, ts, _ = q.shape
    past = k_cache.shape[3]
    tk = ATT_BLOCK
    assert past % tk == 0 and ts <= tk
    n_pairs = SB_PAIRS_PER_STEP
    w = n_pairs * LANES
    new_spec = pl.BlockSpec((None, ts, w), lambda bi, hp: (bi, 0, hp))
    cache_spec = pl.BlockSpec((None, None, w, past), lambda bi, hp: (layer, bi, hp, 0))
    return pl.pallas_call(
        functools.partial(_sb_sample_kernel, ts=ts, tk=tk, n_cache_blocks=past // tk, n_pairs=n_pairs),
        grid=(b, A_WIDTH // w),
        in_specs=[new_spec, new_spec, new_spec, cache_spec, cache_spec],
        out_specs=new_spec,
        out_shape=jax.ShapeDtypeStruct(q.shape, BF16),
        scratch_shapes=[pltpu.VMEM((ts, w), F32), pltpu.VMEM((2 * n_pairs, ts, 1), F32),
                        pltpu.VMEM((tk, w), BF16), pltpu.VMEM((tk, w), BF16)],
        compiler_params=_params("parallel", "parallel"),
        name="sb_attention_sample",
    )(q, k_new, v_new, k_cache, v_cache)


def _band_core(q_heads, lo, kv, pens, bias, o_ref, n_pairs, skip_hidden_quarters=False):
    heads = range(2 * n_pairs)
    tq = q_heads[0].shape[0]
    everything = [(slice(None), [(kv[i], i, slice(None)) for i in range(len(kv))])]
    groups = _band_quarters(kv, tq) if skip_hidden_quarters else everything
    chains = [(h, g) for h in heads for g in range(len(groups))]
    z = [[piece.scores(q_heads[h][groups[g][0], :], h // 2) + bias(h, i)[groups[g][0], cols]
          for piece, i, cols in groups[g][1]] for h, g in chains]
    acc = {}
    for c, (h, g) in enumerate(chains):
        pieces = groups[g][1]
        m = None
        for j, (_, i, _) in enumerate(pieces):
            mi = jnp.max(z[c][j], axis=-1, keepdims=True)
            if pens[i] is not None:
                mi = mi + pens[i]
            m = mi if m is None else jnp.maximum(m, mi)
        a = None
        for j, (piece, i, _) in enumerate(pieces):
            shift = m if pens[i] is None else m - pens[i]
            p = jnp.exp2(z[c][j] - shift).astype(BF16)
            v = piece.values(h // 2)
            ones = jnp.ones_like(v)
            first = piece.head_lanes()
            v = jnp.where(first, v, ones) if h % 2 == 0 else jnp.where(first, ones, v)
            pv = piece.weighted(p, v)
            a = pv if a is None else a + pv
        acc[h, g] = a
    per_head = [acc[h, 0] if len(groups) == 1 else jnp.concatenate([acc[h, g] for g in range(len(groups))], 0)
                for h in heads]
    for p in range(n_pairs):
        a0, a1 = per_head[2 * p], per_head[2 * p + 1]
        o_ref[:, _pair_cols(p)] = jnp.where(lo, a0 / pltpu.roll(a0, HEAD_DIM, axis=1),
                                            a1 / pltpu.roll(a1, HEAD_DIM, axis=1)).astype(o_ref.dtype)


def _band_quarters(kv, tq):
    own, back1, back2 = kv
    half = tq // 2
    assert own.n_keys == tq and half % CHUNK == 0 and 2 * (tq // CHUNK) >= C_LEFT_CHUNKS
    whole = slice(None)
    return [(slice(0, half), [(own.first(half), 0, slice(0, half)), (back1, 1, whole), (back2, 2, whole)]),
            (slice(half, tq), [(own, 0, whole), (back1, 1, whole), (back2.last(half), 2, slice(half, tq))])]


def _band_prompt_kernel(q_ref, k_ref, v_ref, bias_ref, *refs, tq, tk, n_pairs, blocks_per_step):
    n_riders = (len(refs) - 1) // 2
    o_ref = refs[n_riders]
    _cast_riders(list(zip(refs[:n_riders], refs[n_riders + 1:])))

    def one_block(s, carry):
        qi = pl.program_id(2) * blocks_per_step + s
        rows = pl.ds(pl.multiple_of(s * tq, tq), tq)
        q_heads, lo = _split_heads(q_ref.at[rows, :], n_pairs, tq)
        kv, pens = [], []
        for dj in range(3):
            j = qi - dj
            pens.append(None if dj == 0 else jnp.where(j >= 0, 0.0, NEG_BIG).astype(F32))
            keys = pl.ds(pl.multiple_of(jnp.maximum(j, 0) * tk, tk), tk)
            kv.append(_KV(k_ref[:, keys], v_ref[:, keys], True))
        _band_core(q_heads, lo, kv, pens, lambda h, i: bias_ref.at[h, i], o_ref.at[rows, :], n_pairs,
                   skip_hidden_quarters=True)
        return carry

    lax.fori_loop(0, blocks_per_step, one_block, 0)


def _band_sample_kernel(q_ref, kn_ref, vn_ref, kc_ref, vc_ref, bias_ref, o_ref, kpad_ref, vpad_ref,
                        *, ts, tk, n_cache_blocks, n_pairs):
    q_heads, lo = _split_heads(q_ref, n_pairs, ts)
    kpad_ref[...] = jnp.zeros_like(kpad_ref)
    vpad_ref[...] = jnp.zeros_like(vpad_ref)
    kpad_ref[:ts, :] = kn_ref[...]
    vpad_ref[:ts, :] = vn_ref[...]
    kv = [_KV(kpad_ref[...], vpad_ref[...], False)]
    for dj in range(1, n_cache_blocks + 1):
        keys = slice((n_cache_blocks - dj) * tk, (n_cache_blocks - dj + 1) * tk)
        kv.append(_KV(kc_ref[:, keys].astype(BF16), vc_ref[:, keys].astype(BF16), True))
    is_new_key = lax.broadcasted_iota(jnp.int32, (ts, tk), 1) < ts

    def bias(h, i):
        return jnp.where(is_new_key, bias_ref[h, 0], NEG_BIG) if i == 0 else bias_ref[h, i]

    _band_core(q_heads, lo, kv, [None] * len(kv), bias, o_ref, n_pairs)


def _band_attention_prompt(q, k, v, bias, ride_stacks, layer):
    b, t, _ = q.shape
    tq = tk = ATT_BLOCK
    assert t % tk == 0
    n_pairs = BAND_PAIRS_PER_STEP
    w = n_pairs * LANES
    per_step = math.gcd(t // tq, BAND_BLOCKS_PER_STEP)
    kern = functools.partial(_band_prompt_kernel, tq=tq, tk=tk, n_pairs=n_pairs, blocks_per_step=per_step)
    kv_spec = pl.BlockSpec((None, w, t), lambda bi, hp, qi: (bi, hp, 0))
    q_spec = pl.BlockSpec((None, per_step * tq, w), lambda bi, hp, qi: (bi, qi, hp))
    bias_spec = pl.BlockSpec((2 * n_pairs, 3, tq, tk), lambda bi, hp, qi: (hp, 0, 0, 0))
    grid = (b, C_WIDTH // w, t // (per_step * tq))
    ride_in, ride_out, ride_shapes = _Riders(ride_stacks, layer, grid).specs()
    return pl.pallas_call(
        kern,
        grid=grid,
        in_specs=[q_spec, kv_spec, kv_spec, bias_spec] + ride_in,
        out_specs=[q_spec] + ride_out,
        out_shape=[jax.ShapeDtypeStruct(q.shape, BF16)] + ride_shapes,
        compiler_params=_params("arbitrary", "arbitrary", "arbitrary"),
        name="band_attention",
    )(q, k, v, bias, *ride_stacks)


def _band_attention_sample(q, k_new, v_new, k_cache, v_cache, layer, bias):
    b, ts, _ = q.shape
    wc = k_cache.shape[3]
    tk = ATT_BLOCK
    n_cache_blocks = min(wc // tk, 2)
    assert wc % (n_cache_blocks * tk) == 0 and ts <= tk
    n_pairs = C_WIDTH // LANES
    w = n_pairs * LANES
    kern = functools.partial(_band_sample_kernel, ts=ts, tk=tk, n_cache_blocks=n_cache_blocks,
                             n_pairs=n_pairs)
    new_spec = pl.BlockSpec((None, ts, w), lambda bi, hp: (bi, 0, hp))
    cache_rows = n_cache_blocks * tk
    cache_spec = pl.BlockSpec((None, None, w, cache_rows),
                              lambda bi, hp: (layer, bi, hp, wc // cache_rows - 1))
    bias_spec = pl.BlockSpec((2 * n_pairs, 1 + n_cache_blocks, ts, tk), lambda bi, hp: (hp, 0, 0, 0))
    return pl.pallas_call(
        kern,
        grid=(b, C_WIDTH // w),
        in_specs=[new_spec, new_spec, new_spec, cache_spec, cache_spec, bias_spec],
        out_specs=new_spec,
        out_shape=jax.ShapeDtypeStruct(q.shape, BF16),
        scratch_shapes=[pltpu.VMEM((tk, w), BF16), pltpu.VMEM((tk, w), BF16)],
        compiler_params=_params("parallel", "parallel"),
        name="band_attention_sample",
    )(q, k_new, v_new, k_cache, v_cache, bias)


def _band_bias_kernel(g_ref, o_ref):
    rows = tk = ATT_BLOCK
    q_chunk = lax.broadcasted_iota(jnp.int32, (rows, tk), 0) // CHUNK
    k_chunk = lax.broadcasted_iota(jnp.int32, (rows, tk), 1) // CHUNK
    for dj in range(3):
        diff = dj * (tk // CHUNK) + q_chunk - k_chunk
        seen = (diff >= 0) & (diff <= C_LEFT_CHUNKS)
        for h in range(g_ref.shape[0]):
            g = jnp.broadcast_to(g_ref[h, dj], (rows, 2 * tk))
            tile = pltpu.roll(g, 0, axis=1, stride=1, stride_axis=0)[:, :tk]
            o_ref[h, dj] = jnp.where(seen, tile * LOG2E, NEG_BIG)


def _band_bias(rel_table):
    rows = tk = ATT_BLOCK
    c = jnp.arange(2 * tk, dtype=jnp.int32)
    u = jnp.where(c <= tk, -c, 2 * tk - c)
    idx = jnp.clip(jnp.arange(3, dtype=jnp.int32)[:, None] * tk + u[None, :], REL_MIN, REL_MAX) - REL_MIN
    g = rel_table[:, idx].astype(F32).reshape(C_HEADS, 3, 1, 2 * tk)
    return pl.pallas_call(
        _band_bias_kernel,
        grid=(C_HEADS // BIAS_HEADS_PER_STEP,),
        in_specs=[pl.BlockSpec((BIAS_HEADS_PER_STEP, 3, 1, 2 * tk), lambda h: (h, 0, 0, 0))],
        out_specs=pl.BlockSpec((BIAS_HEADS_PER_STEP, 3, rows, tk), lambda h: (h, 0, 0, 0)),
        out_shape=jax.ShapeDtypeStruct((C_HEADS, 3, rows, tk), F32),
        compiler_params=_params("parallel"),
        name="band_bias",
    )(g)


def _gla_pair_tile(q, k, v, la, states, emit):
    L = GLA_CHUNK
    chunks = range(q.shape[0] // L)
    row = lax.broadcasted_iota(jnp.int32, (L, L), 0)
    colm = lax.broadcasted_iota(jnp.int32, (L, L), 1)
    tri = (colm <= row).astype(BF16)
    causal = colm <= row
    lane = lax.broadcasted_iota(jnp.int32, (L, LANES), 1)
    sub = lax.broadcasted_iota(jnp.int32, (LANES, B_DV), 0)
    mine = [(lane >= h * B_DK) & (lane < (h + 1) * B_DK) for h in range(2)]
    mine_rows = [(sub >= h * B_DK) & (sub < (h + 1) * B_DK) for h in range(2)]
    rows = [slice(c * L, (c + 1) * L) for c in chunks]

    b = []
    for r in rows:
        g_hi, g_lo = _split_bf16(la[r, :])
        b.append(_dot(tri, g_hi) + _dot(tri, g_lo))
    yield
    qg, qg_h, kg, kd_t, decay = [], [], [], [], []
    for c, r in zip(chunks, rows):
        qg_c = q[r, :] * (B_DK ** -0.5) * jnp.exp(b[c])
        qg.append(qg_c.astype(BF16))
        qg_h.append([jnp.where(mine[h], qg_c, 0.0).astype(BF16) for h in range(2)])
        kg.append((k[r, :] * jnp.exp(-b[c])).astype(BF16))
        b_t = b[c].T
        b_last = b_t[:, L - 1:L]
        kd_t.append((k[r, :].T * jnp.exp(b_last - b_t)).astype(BF16))
        decay.append(jnp.exp(b_last))
    yield
    att =[[jnp.where(causal, _dot_nt(qg_h[c][h], kg[c]), 0.0).astype(BF16) for h in range(2)]
           for c in chunks]
    yield
    o_intra, own = [], []
    for c, r in zip(chunks, rows):
        v_h = [v[r, h * B_DV:(h + 1) * B_DV].astype(BF16) for h in range(2)]
        o_intra.append([_dot(att[c][h], v_h[h]) for h in range(2)])
        own.append([jnp.where(mine_rows[h], _dot(kd_t[c], v_h[h]), 0.0) for h in range(2)])

    yield
    states = list(states)
    start = []
    for c in chunks:
        start.append([s.astype(BF16) for s in states])
        states = [decay[c] * states[h] + own[c][h] for h in range(2)]
    yield
    for c in chunks:
        for h in range(2):
            emit(c, h, o_intra[c][h] + _dot(qg[c], start[c][h]))
    return states


def _pair_states(s0_ref, p):
    zeros_state = jnp.zeros((B_DK, B_DV), F32)
    return [jnp.concatenate([s0_ref[2 * p], zeros_state], axis=0),
            jnp.concatenate([zeros_state, s0_ref[2 * p + 1]], axis=0)]


def _pair_v_cols(p):
    return slice(2 * p * B_DV, 2 * (p + 1) * B_DV)


def _gla_rows_emit(o_ref, p):
    def emit(c, h, o):
        o_ref[c * GLA_CHUNK:(c + 1) * GLA_CHUNK, (2 * p + h) * B_DV:(2 * p + h + 1) * B_DV] = o
    return emit


def _gla_kernel(q_ref, k_ref, v_ref, la_ref, s0_ref, o_ref, s_out_ref):
    pairs = range(B_HEADS // 2)
    states = _interleave([
        _gla_pair_tile(q_ref[:, _pair_cols(p)], k_ref[:, _pair_cols(p)], v_ref[:, _pair_v_cols(p)],
                       la_ref[:, _pair_cols(p)], _pair_states(s0_ref, p), _gla_rows_emit(o_ref, p))
        for p in pairs])
    for p in pairs:
        s_out_ref[2 * p] = states[p][0][:B_DK, :]
        s_out_ref[2 * p + 1] = states[p][1][B_DK:, :]


def _gla(q, k, v, la, s0):
    b, t, _ = q.shape
    assert t % GLA_CHUNK == 0 and t <= GLA_MAX_TILE
    qk_spec = pl.BlockSpec((None, t, B_KW), lambda bi: (bi, 0, 0))
    v_spec = pl.BlockSpec((None, t, B_VW), lambda bi: (bi, 0, 0))
    s_spec = pl.BlockSpec((None, B_HEADS, B_DK, B_DV), lambda bi: (bi, 0, 0, 0))
    return pl.pallas_call(
        _gla_kernel,
        grid=(b,),
        in_specs=[qk_spec, qk_spec, v_spec, qk_spec, s_spec],
        out_specs=[v_spec, s_spec],
        out_shape=[jax.ShapeDtypeStruct(v.shape, F32), jax.ShapeDtypeStruct(s0.shape, F32)],
        compiler_params=_params("parallel"),
        name="gla",
    )(q, k, v, la, s0)


def _pad_rows(x, n):
    return jnp.pad(x, ((0, 0), (0, n - x.shape[1]), (0, 0)))


def _heads_last(x, heads):
    n, b, _, s = x.shape
    return jnp.transpose(x.reshape(n, b, heads, HEAD_DIM, s), (0, 1, 4, 2, 3))


def _feature_major(cache):
    n, b, s, heads, hd = cache.shape
    return jnp.transpose(cache, (0, 1, 3, 4, 2)).reshape(n, b, heads * hd, s)


def _row_tile(m):
    for tm in (512, 256, 128, 64, 32, 16, 8):
        if m % tm == 0:
            return tm
    raise ValueError(f"token count {m} is not a multiple of 8")


def kernel(x_prompt, x_sample, cache_a_k, cache_a_v, state_b, cache_c_k, cache_c_v, norm_mix_g, norm_ffn_g, w_in_ab, w_gate_b, b_gate_b, norm_gla_g, w_out_ab, w_qkv_c, rel_bias_c, w_out_c, w_ffn_gate, w_ffn_up, w_ffn_down, norm_final_g):
    bp, tp, d = x_prompt.shape
    bs, ts, _ = x_sample.shape
    depth = norm_mix_g.shape[0]
    past = cache_a_k.shape[2]
    wc = cache_c_k.shape[2]
    assert tp % ATT_BLOCK == 0 and past % ATT_BLOCK == 0 and wc % ATT_BLOCK == 0
    assert ts <= GLA_CHUNK and ts % 8 == 0
    mp, ms = bp * tp, bs * ts
    tmp, tms = _row_tile(tp), _row_tile(ms)
    xp = x_prompt.reshape(mp, d)
    xs = x_sample.reshape(ms, d)
    row2 = lambda a: a.reshape(1, -1)

    a_ks, a_vs, b_sp, b_ss, c_ks, c_vs = [], [], [], [], [], []
    a_kv_prompt = c_kv_prompt = None
    n_ab, n_c = (depth + 1) // 2, depth // 2
    keep = min(C_LEFT_CHUNKS * CHUNK, tp)

    kv0, kv1 = A_WIDTH, 3 * A_WIDTH
    o = 3 * A_WIDTH + 2 * B_KW + B_VW
    w_in_t = jnp.swapaxes(w_in_ab, 1, 2)
    w_main_all = jnp.concatenate([w_in_t[:, :kv0], w_in_t[:, kv1:o], w_in_t[:, o + B_GATE_RANK:]],
                                 axis=1).astype(BF16)
    w_kv_ab_t_all = w_in_t[:, kv0:kv1].astype(BF16)
    w_lr_all = jnp.pad(w_in_t[:, o:o + B_GATE_RANK],
                       ((0, 0), (0, LANES - B_GATE_RANK), (0, 0))).astype(BF16)
    w_gate_all = jnp.pad(w_gate_b, ((0, 0), (0, LANES - B_GATE_RANK), (0, 0))).astype(BF16)
    w_out_ab_all = w_out_ab.astype(BF16)
    w_q_all = w_qkv_c[:, :, :C_WIDTH].astype(BF16)
    w_kv_c_t_all = jnp.swapaxes(w_qkv_c[:, :, C_WIDTH:], 1, 2).astype(BF16)
    w_out_c_all = w_out_c.astype(BF16)
    cache_a_k_fm, cache_a_v_fm, cache_c_k_fm, cache_c_v_fm = (
        _feature_major(c) for c in (cache_a_k, cache_a_v, cache_c_k, cache_c_v))

    for layer in range(depth):
        i = layer // 2
        g_mix = row2(norm_mix_g[layer])
        ffn_f32 = (w_ffn_gate, w_ffn_up, w_ffn_down)
        ffn_ends = (row2(norm_ffn_g[layer]), row2(norm_final_g) if layer == depth - 1 else None)
        if layer % 2 == 0:
            w_main, w_kv_t, w_lr, w_gate, w_out = (
                _Slab(w, i) for w in (w_main_all, w_kv_ab_t_all, w_lr_all, w_gate_all, w_out_ab_all))
            b_gate = row2(b_gate_b[i])
            g_gla = row2(norm_gla_g[i])

            qa, ka, va, kab, vab, qb, kb, vb, r, la = _proj_ab(
                xp, g_mix, w_main, w_kv_t, w_lr, w_gate, b_gate, tmp, batch=bp,
                stack=_KVStack(i, n_ab, tp, a_kv_prompt))
            a_kv_prompt = (ka, va)
            sh = lambda a: a.reshape(bp, tp, -1)
            oa, ob, sbp, *ffn_w = _mixer_ab_prompt(sh(qa), kab, vab, sh(qb), sh(kb), sh(vb), sh(la),
                                                   ffn_f32, layer)
            ffn = (ffn_ends[0], *ffn_w, ffn_ends[1])
            mixed_p = (oa.reshape(mp, -1), ob.reshape(mp, -1), r)
            b_sp.append(sbp)

            qa, ka, va, kab, vab, qb, kb, vb, r, la = _proj_ab(
                xs, g_mix, w_main, w_kv_t, w_lr, w_gate, b_gate, tms)
            sh = lambda a: a.reshape(bs, ts, -1)
            oa = _sb_attention_sample(sh(qa), sh(kab), sh(vab), cache_a_k_fm, cache_a_v_fm, i)
            pad_t = lambda a: _pad_rows(sh(a), GLA_CHUNK)
            ob, sbs = _gla(pad_t(qb), pad_t(kb), pad_t(vb), pad_t(la), state_b[i])
            mixed_s = (oa.reshape(ms, -1), ob[:, :ts].reshape(ms, -1), r)
            xp, xs = _layer_tail(xp, mixed_p, xs, mixed_s, (g_gla, w_out), *ffn, tmp)
            a_ks.append(ka.reshape(bs, ts, A_HEADS, HEAD_DIM))
            a_vs.append(va.reshape(bs, ts, A_HEADS, HEAD_DIM))
            b_ss.append(sbs)
        else:
            w_q, w_kv_t, w_out = (_Slab(w, i) for w in (w_q_all, w_kv_c_t_all, w_out_c_all))

            q, k, v, kb16, vb16 = _proj_c(xp, g_mix, w_q, w_kv_t, tmp, batch=bp,
                                          stack=_KVStack(i, n_c, keep, c_kv_prompt))
            c_kv_prompt = (k, v)
            bias = _band_bias(rel_bias_c[i])
            oc, *ffn_w = _band_attention_prompt(q.reshape(bp, tp, -1), kb16, vb16, bias, ffn_f32, layer)
            ffn = (ffn_ends[0], *ffn_w, ffn_ends[1])
            mixed_p = (oc.reshape(mp, -1),)

            q, k, v, kb16, vb16 = _proj_c(xs, g_mix, w_q, w_kv_t, tms)
            sh = lambda a: a.reshape(bs, ts, -1)
            oc = _band_attention_sample(sh(q), sh(kb16), sh(vb16), cache_c_k_fm, cache_c_v_fm, i, bias)
            xp, xs = _layer_tail(xp, mixed_p, xs, (oc.reshape(ms, -1),), (w_out,), *ffn, tmp)
            c_ks.append(k.reshape(bs, ts, C_HEADS, HEAD_DIM))
            c_vs.append(v.reshape(bs, ts, C_HEADS, HEAD_DIM))

    y_prompt = xp.reshape(bp, tp, d)
    y_sample = xs.reshape(bs, ts, d)
    a_kp, a_vp = (_heads_last(a, A_HEADS) for a in a_kv_prompt)
    c_kp, c_vp = (_heads_last(a, C_HEADS) for a in c_kv_prompt)
    return (y_prompt, y_sample, a_kp, a_vp, jnp.stack(a_ks), jnp.stack(a_vs),
            jnp.stack(b_sp), jnp.stack(b_ss), c_kp, c_vp, jnp.stack(c_ks), jnp.stack(c_vs))
```

```python
import functools
import math

import jax
import jax.numpy as jnp
from jax import lax
from jax.experimental import pallas as pl
from jax.experimental.pallas import tpu as pltpu

F32 = jnp.float32
BF16 = jnp.bfloat16

EPS = 1e-6
HEAD_DIM = 64
LANES = 128
A_HEADS = 8
A_WIDTH = A_HEADS * HEAD_DIM
B_HEADS = 4
B_DK = 64
B_DV = 128
B_KW = B_HEADS * B_DK
B_VW = B_HEADS * B_DV
B_GATE_RANK = 16
B_GATE_TEMP = 16.0
GLA_CHUNK = 64
C_HEADS = 16
C_WIDTH = C_HEADS * HEAD_DIM
CHUNK = 64
C_LEFT_CHUNKS = 8
REL_MIN = -(CHUNK - 1)
REL_MAX = 128
ATT_BLOCK = 256
NEG_BIG = -1e30
LOG2E = 1.4426950408889634
SB_DEAD = 104.0
SB_PAIRS_PER_STEP = 4
BAND_PAIRS_PER_STEP = 4
BIAS_HEADS_PER_STEP = 8
MIXER_BLOCKS_PER_STEP = 4
BAND_BLOCKS_PER_STEP = 8
GLA_MAX_TILE = 512
VMEM_LIMIT = 56 * 1024 * 1024


def _params(*sem):
    return pltpu.CompilerParams(dimension_semantics=sem, vmem_limit_bytes=VMEM_LIMIT)


class _Slab:
    def __init__(self, stacked, index):
        self.stacked, self.index, self.shape = stacked, index, stacked.shape[1:]


def _operand(a):
    return a.stacked if isinstance(a, _Slab) else a


def _resident(a):
    if isinstance(a, _Slab):
        index = (a.index,) + (0,) * len(a.shape)
        return pl.BlockSpec((None, *a.shape), lambda *_: index, pipeline_mode=pl.Buffered(1))
    return pl.BlockSpec(a.shape, lambda *_: (0,) * a.ndim, pipeline_mode=pl.Buffered(1))


def _rms(x, g):
    return x * lax.rsqrt(jnp.mean(x * x, axis=-1, keepdims=True) + EPS) * g


def _log_sigmoid(z):
    return jnp.minimum(z, 0.0) - jnp.log1p(jnp.exp(-jnp.abs(z)))


def _split_bf16(x):
    hi = x.astype(BF16)
    lo = (x - hi.astype(F32)).astype(BF16)
    return hi, lo


def _dot(a, b):
    return jnp.dot(a, b, preferred_element_type=F32)


def _dot_nt(a, b):
    return lax.dot_general(a, b, (((1,), (1,)), ((), ())), preferred_element_type=F32)


def _pair_cols(p):
    return slice(p * LANES, (p + 1) * LANES)


class _KV:
    def __init__(self, k, v, feature_major):
        self.k, self.v, self.feature_major = k, v, feature_major
        self.n_keys = k.shape[1] if feature_major else k.shape[0]

    def scores(self, q_h, p):
        if self.feature_major:
            return _dot(q_h, self.k[_pair_cols(p), :])
        return _dot_nt(q_h, self.k[:, _pair_cols(p)])

    def first(self, n):
        if self.feature_major:
            return _KV(self.k[:, :n], self.v[:, :n], True)
        return _KV(self.k[:n], self.v[:n], False)

    def last(self, n):
        if self.feature_major:
            return _KV(self.k[:, -n:], self.v[:, -n:], True)
        return _KV(self.k[-n:], self.v[-n:], False)

    def values(self, p):
        return self.v[_pair_cols(p), :] if self.feature_major else self.v[:, _pair_cols(p)]

    def weighted(self, w, v_p):
        return _dot_nt(w, v_p) if self.feature_major else _dot(w, v_p)

    def head_lanes(self):
        shape = (LANES, self.n_keys) if self.feature_major else (self.n_keys, LANES)
        return lax.broadcasted_iota(jnp.int32, shape, 0 if self.feature_major else 1) < HEAD_DIM


def _split_heads(q_ref, n_pairs, tq):
    lo = lax.broadcasted_iota(jnp.int32, (tq, LANES), 1) < HEAD_DIM
    heads = []
    for p in range(n_pairs):
        q = q_ref[:, _pair_cols(p)]
        heads += [jnp.where(lo, q, jnp.zeros_like(q)), jnp.where(lo, jnp.zeros_like(q), q)]
    return heads, lo


def _emit_kv(y, wkv_ref, k_ref, v_ref, kb_ref, vb_ref, feature_major, kept_tail=None, own_slot=0):
    if not feature_major:
        kv = _dot_nt(y, wkv_ref[...])
        width = kv.shape[1] // 2
        k, v = kv[:, :width], kv[:, width:]
        k_ref[...], v_ref[...], kb_ref[...], vb_ref[...] = k, v, k.astype(BF16), v.astype(BF16)
        return
    kv = _dot_nt(wkv_ref[...], y)
    width = kv.shape[0] // 2
    k, v = kv[:width, :], kv[width:, :]
    kb_ref[...] = k.astype(BF16)
    vb_ref[...] = v.astype(BF16)
    slot = own_slot
    other_slots = [s for s in range(k_ref.shape[0]) if s != slot]
    if kept_tail is None:
        fill, fill_cols = slice(None), k_ref.shape[2]
        k_ref[slot] = k
        v_ref[slot] = v
    else:
        n_tiles, n_kept = kept_tail
        tile = pl.program_id(0) % n_tiles
        tm = k.shape[1]
        cols = pl.ds(pl.multiple_of(jnp.maximum(tile - (n_tiles - n_kept), 0) * tm, tm), tm)
        k_ref[slot, :, cols] = k
        v_ref[slot, :, cols] = v
        share = k_ref.shape[2] // n_tiles
        if share % LANES == 0 and share * n_tiles == k_ref.shape[2]:
            fill, fill_cols = pl.ds(pl.multiple_of(tile * share, share), share), share
        else:
            fill, fill_cols = slice(None), k_ref.shape[2]
    for dst in (k_ref, v_ref):
        for s in other_slots:
            dst[s, :, fill] = jnp.zeros((dst.shape[1], fill_cols), dst.dtype)


class _Riders:
    def __init__(self, stacks, layer, grid):
        self.stacks, self.layer, self.grid = list(stacks), layer, tuple(grid)
        self.chunks = next(c for c in (16, 8, 4, 2, 1)
                           if c <= math.prod(grid) and all(w.shape[1] % (16 * c) == 0 for w in self.stacks))

    def _chunk(self, *idx):
        step = 0
        for i, n in zip(idx, self.grid):
            step = step * n + i
        return jnp.minimum(step, self.chunks - 1)

    def specs(self):
        layer = self.layer
        ins = [pl.BlockSpec((None, w.shape[1] // self.chunks, w.shape[2]),
                            lambda *idx: (layer, self._chunk(*idx), 0)) for w in self.stacks]
        outs = [pl.BlockSpec((w.shape[1] // self.chunks, w.shape[2]),
                             lambda *idx: (self._chunk(*idx), 0)) for w in self.stacks]
        shapes = [jax.ShapeDtypeStruct(w.shape[1:], BF16) for w in self.stacks]
        return ins, outs, shapes


def _cast_riders(pairs):
    for src, dst in pairs:
        dst[...] = src[...].astype(BF16)


def _proj_ab_kernel(x_ref, g_ref, w_ref, wkv_ref, wlr_ref, wgate_ref, bgate_ref, *refs,
                    feature_major, n_aliased, own_slot):
    qa_ref, ka_ref, va_ref, kab_ref, vab_ref, qb_ref, kb_ref, vb_ref, r_ref, la_ref = refs[n_aliased:]
    y = _rms(x_ref[...], g_ref[...]).astype(BF16)
    _emit_kv(y, wkv_ref, ka_ref, va_ref, kab_ref, vab_ref, feature_major, own_slot=own_slot)
    z = _dot_nt(y, w_ref[...])
    c = 0
    qa_ref[...] = (z[:, c:c + A_WIDTH] * (HEAD_DIM ** -0.5)).astype(BF16); c += A_WIDTH
    qb_ref[...] = z[:, c:c + B_KW]; c += B_KW
    kb_ref[...] = z[:, c:c + B_KW]; c += B_KW
    vb_ref[...] = z[:, c:c + B_VW]; c += B_VW
    r_ref[...] = z[:, c:c + B_VW]
    g_lr = _dot_nt(y, wlr_ref[...])
    gate = _dot(g_lr.astype(BF16), wgate_ref[...]) + bgate_ref[...]
    la_ref[...] = _log_sigmoid(gate) * (1.0 / B_GATE_TEMP)


class _KVStack:
    def __init__(self, layer, n_layers, keep, previous=None):
        self.layer, self.n_layers, self.keep, self.previous = layer, n_layers, keep, previous


def _kv_out(m, width, tm, batch, stack):
    if batch is None:
        spec = pl.BlockSpec((tm, width), lambda i: (i, 0))
        return [spec] * 4, [jax.ShapeDtypeStruct((m, width), dt) for dt in (F32, F32, BF16, BF16)], None, 0
    t = m // batch
    assert t % tm == 0 and stack.keep % tm == 0 and stack.keep <= t
    n_tiles = t // tm
    copy_spec = pl.BlockSpec((None, width, tm), lambda i: (i // n_tiles, 0, i % n_tiles))
    copy_shape = jax.ShapeDtypeStruct((batch, width, t), BF16)
    slots, first_slot, own_slot = ((stack.n_layers, 0, stack.layer) if stack.previous is None
                                   else (1, stack.layer, 0))
    if stack.keep == t:
        f32_spec = pl.BlockSpec((slots, None, width, tm), lambda i: (first_slot, i // n_tiles, 0, i % n_tiles))
        kept_tail = None
    else:
        f32_spec = pl.BlockSpec((slots, None, width, stack.keep), lambda i: (first_slot, i // n_tiles, 0, 0))
        kept_tail = (n_tiles, stack.keep // tm)
    f32_shape = jax.ShapeDtypeStruct((stack.n_layers, batch, width, stack.keep), F32)
    return ([f32_spec, f32_spec, copy_spec, copy_spec], [f32_shape, f32_shape, copy_shape, copy_shape],
            kept_tail, own_slot)


def _aliased_stack(stack, n_inputs, first_output):
    if stack is None or stack.previous is None:
        return [], [], {}
    prev = list(stack.previous)
    specs = [pl.BlockSpec(memory_space=pl.ANY)] * len(prev)
    return prev, specs, {n_inputs + j: first_output + j for j in range(len(prev))}


def _proj_ab(x, g, w_main, w_kv, w_lr, w_gate, b_gate, tm, batch=None, stack=None):
    m, d = x.shape
    row = lambda n: pl.BlockSpec((tm, n), lambda i: (i, 0))
    params = (g, w_main, w_kv, w_lr, w_gate, b_gate)
    kv_specs, kv_shapes, kept_tail, own_slot = _kv_out(m, A_WIDTH, tm, batch, stack)
    assert kept_tail is None
    prev, prev_specs, aliases = _aliased_stack(stack, 1 + len(params), 1)
    rest = [(B_KW, F32), (B_KW, F32), (B_VW, F32), (B_VW, F32), (B_KW, F32)]
    return pl.pallas_call(
        functools.partial(_proj_ab_kernel, feature_major=batch is not None, n_aliased=len(prev),
                          own_slot=own_slot),
        grid=(m // tm,),
        in_specs=[row(d)] + [_resident(a) for a in params] + prev_specs,
        out_specs=[row(A_WIDTH)] + kv_specs + [row(n) for n, _ in rest],
        out_shape=([jax.ShapeDtypeStruct((m, A_WIDTH), BF16)] + kv_shapes
                   + [jax.ShapeDtypeStruct((m, n), dt) for n, dt in rest]),
        input_output_aliases=aliases,
        compiler_params=_params("parallel"),
        name="proj_ab",
    )(x, *[_operand(a) for a in params], *prev)


def _proj_c_kernel(x_ref, g_ref, wq_ref, wkv_ref, *refs, feature_major, n_aliased, kept_tail, own_slot):
    q_ref, k_ref, v_ref, kb_ref, vb_ref = refs[n_aliased:]
    y = _rms(x_ref[...], g_ref[...]).astype(BF16)
    _emit_kv(y, wkv_ref, k_ref, v_ref, kb_ref, vb_ref, feature_major, kept_tail, own_slot)
    q_ref[...] = (_dot(y, wq_ref[...]) * (HEAD_DIM ** -0.5 * LOG2E)).astype(BF16)


def _proj_c(x, g, w_q, w_kv, tm, batch=None, stack=None):
    m, d = x.shape
    row = lambda n: pl.BlockSpec((tm, n), lambda i: (i, 0))
    params = (g, w_q, w_kv)
    kv_specs, kv_shapes, kept_tail, own_slot = _kv_out(m, C_WIDTH, tm, batch, stack)
    prev, prev_specs, aliases = _aliased_stack(stack, 1 + len(params), 1)
    return pl.pallas_call(
        functools.partial(_proj_c_kernel, feature_major=batch is not None, n_aliased=len(prev),
                          kept_tail=kept_tail, own_slot=own_slot),
        grid=(m // tm,),
        in_specs=[row(d)] + [_resident(a) for a in params] + prev_specs,
        out_specs=[row(C_WIDTH)] + kv_specs,
        out_shape=[jax.ShapeDtypeStruct((m, C_WIDTH), BF16)] + kv_shapes,
        input_output_aliases=aliases,
        compiler_params=_params("arbitrary" if kept_tail else "parallel"),
        name="proj_c",
    )(x, *[_operand(a) for a in params], *prev)


def _layer_tail_rows(x_ref, mixer_refs, param_refs, o_ref, gla_merge, final_norm):
    if gla_merge:
        oa_ref, ob_ref, r_ref = mixer_refs
        ggla_ref, wout_ref, *param_refs = param_refs
        ob = ob_ref[...]
        parts = []
        for h in range(B_HEADS):
            seg = ob[:, h * B_DV:(h + 1) * B_DV]
            parts.append(seg * lax.rsqrt(jnp.mean(seg * seg, axis=-1, keepdims=True) + EPS))
        r = r_ref[...]
        obn = jnp.concatenate(parts, axis=-1) * ggla_ref[...] * (r * jax.nn.sigmoid(r))
        mix = _dot(oa_ref[...], wout_ref[:A_WIDTH, :]) + _dot(obn.astype(BF16), wout_ref[A_WIDTH:, :])
    else:
        (oc_ref,) = mixer_refs
        wout_ref, *param_refs = param_refs
        mix = _dot(oc_ref[...], wout_ref[...])
    gffn_ref, wg_ref, wu_ref, wd_ref, *param_refs = param_refs
    x = x_ref[...] + mix
    y = _rms(x, gffn_ref[...]).astype(BF16)
    h = _dot(y, wg_ref[...])
    u = _dot(y, wu_ref[...])
    a = (h * jax.nn.sigmoid(h) * u).astype(BF16)
    x = x + _dot(a, wd_ref[...])
    o_ref[...] = _rms(x, param_refs[0][...]) if final_norm else x


def _layer_tail_kernel(*refs, n_mixer, n_params, gla_merge, final_norm, main_steps):
    main, extra = refs[:1 + n_mixer], refs[1 + n_mixer:2 + 2 * n_mixer]
    params = refs[2 + 2 * n_mixer:2 + 2 * n_mixer + n_params]
    o_ref, o_extra_ref = refs[-2:]

    @pl.when(pl.program_id(0) < main_steps)
    def _():
        _layer_tail_rows(main[0], main[1:], params, o_ref, gla_merge, final_norm)

    @pl.when(pl.program_id(0) == main_steps)
    def _():
        _layer_tail_rows(extra[0], extra[1:], params, o_extra_ref, gla_merge, final_norm)


def _layer_tail(x, mixer_out, x_extra, mixer_out_extra, mixer_params, g_ffn, wg, wu, wd, g_fin, tm):
    m, d = x.shape
    main_steps = m // tm
    row = lambda a: pl.BlockSpec((tm, a.shape[1]), lambda i: (jnp.minimum(i, main_steps - 1), 0))
    whole = lambda a: pl.BlockSpec(a.shape, lambda i: (0, 0))
    params = [*mixer_params, g_ffn, wg, wu, wd] + ([] if g_fin is None else [g_fin])
    return pl.pallas_call(
        functools.partial(_layer_tail_kernel, n_mixer=len(mixer_out), n_params=len(params),
                          gla_merge=len(mixer_out) == 3, final_norm=g_fin is not None, main_steps=main_steps),
        grid=(main_steps + 1,),
        in_specs=([row(x)] + [row(a) for a in mixer_out] + [whole(x_extra)]
                  + [whole(a) for a in mixer_out_extra] + [_resident(a) for a in params]),
        out_specs=[row(x), whole(x_extra)],
        out_shape=[jax.ShapeDtypeStruct((m, d), F32), jax.ShapeDtypeStruct(x_extra.shape, F32)],
        compiler_params=_params("arbitrary"),
        name="layer_tail",
    )(x, *mixer_out, x_extra, *mixer_out_extra, *[_operand(a) for a in params])


def _interleave(stage_generators):
    results = [None] * len(stage_generators)
    live = list(range(len(stage_generators)))
    while live:
        for i in list(live):
            try:
                next(stage_generators[i])
            except StopIteration as done:
                results[i] = done.value
                live.remove(i)
    return results


def _sb_core(q_heads, lo, first_kv, first_mask, earlier_kv, n_earlier, acc_ref, c_ref, o_ref, n_pairs,
             companions=(), on_companions=None):
    tk = first_kv.n_keys
    heads = range(2 * n_pairs)
    later = (lax.broadcasted_iota(jnp.int32, (tk, tk), 0)
             > lax.broadcasted_iota(jnp.int32, (tk, tk), 1)).astype(BF16)
    c_ref[...] = jnp.zeros_like(c_ref)

    tq = q_heads[0].shape[0]

    def row_groups(kv, mask):
        if mask is None or tq != tk or tq % (2 * LANES):
            return [(slice(None), kv, mask, later)]
        half = tq // 2
        return [(slice(0, half), kv.first(half), mask[:half, :half], later[:half, :half]),
                (slice(half, tq), kv, mask[half:, :], later)]

    def block(kv, mask):
        groups = row_groups(kv, mask)
        chains = [(h, g) for h in heads for g in range(len(groups))]
        z = [groups[g][1].scores(q_heads[h][groups[g][0], :], h // 2) for h, g in chains]
        yield
        log_beta, drop, after = [], [], []
        for i, (h, g) in enumerate(chains):
            d = jnp.maximum(z[i], 0.0) + jnp.log(1.0 + jnp.exp(-jnp.abs(z[i])))
            log_beta.append(z[i] - d)
            drop.append(d if mask is None else jnp.where(groups[g][2], d, 0.0))
        yield
        for i, (h, g) in enumerate(chains):
            hi, lo_part = _split_bf16(drop[i])
            tri = groups[g][3]
            after.append(_dot(hi, tri) + _dot(lo_part, tri))
        yield
        pv = {}
        for i, (h, g) in enumerate(chains):
            rows, kv_g = groups[g][0], groups[g][1]
            c = c_ref[h, rows, :]
            w = jnp.exp(log_beta[i] - after[i] - c)
            if mask is not None:
                w = jnp.where(groups[g][2], w, 0.0)
            c_ref[h, rows, :] = c + after[i][:, 0:1] + drop[i][:, 0:1]
            pv[h, g] = kv_g.weighted(w.astype(BF16), kv_g.values(h // 2))
        yield
        per_head = [pv[h, 0] if len(groups) == 1 else jnp.concatenate([pv[h, g] for g in range(len(groups))], 0)
                    for h in heads]
        out = [jnp.where(lo, per_head[2 * p], per_head[2 * p + 1]) for p in range(n_pairs)]
        return out[0] if n_pairs == 1 else jnp.concatenate(out, axis=-1)

    def all_dead():
        return jnp.min(c_ref[...]) > SB_DEAD

    first, *companion_results = _interleave([block(first_kv, first_mask), *companions])
    acc_ref[...] = first
    if companions:
        on_companions(companion_results)

    def cond(carry):
        n, dead = carry
        return (n < n_earlier) & jnp.logical_not(dead)

    def body(carry):
        n, _ = carry
        acc_ref[...] += _interleave([block(earlier_kv(n), None)])[0]
        return n + 1, all_dead()

    lax.while_loop(cond, body, (jnp.int32(0), jnp.asarray(False)))
    o_ref[...] = acc_ref[...].astype(o_ref.dtype)


def _mixer_ab_prompt_kernel(q_ref, k_ref, v_ref, qb_ref, kb_ref, vb_ref, la_ref, *refs,
                            tb, n_pairs, blocks_per_step, n_riders):
    o_ref, ob_ref, s_out_ref = refs[n_riders:n_riders + 3]
    acc_ref, c_ref, carry_ref = refs[3 + 2 * n_riders:]
    _cast_riders(list(zip(refs[:n_riders], refs[n_riders + 3:3 + 2 * n_riders])))
    pairs = range(B_HEADS // 2)
    strictly_earlier = (lax.broadcasted_iota(jnp.int32, (tb, tb), 1)
                        < lax.broadcasted_iota(jnp.int32, (tb, tb), 0))

    def kv_block(j):
        keys = pl.ds(pl.multiple_of(j * tb, tb), tb)
        return _KV(k_ref[:, keys], v_ref[:, keys], True)

    def keep_states(states):
        for p in pairs:
            carry_ref[2 * p], carry_ref[2 * p + 1] = states[p]
            s_out_ref[2 * p] = states[p][0][:B_DK, :]
            s_out_ref[2 * p + 1] = states[p][1][B_DK:, :]

    @pl.when(pl.program_id(1) == 0)
    def _():
        carry_ref[...] = jnp.zeros_like(carry_ref)

    def one_block(s, carry):
        qi = pl.program_id(1) * blocks_per_step + s
        rows = pl.ds(pl.multiple_of(s * tb, tb), tb)
        gla = [_gla_pair_tile(qb_ref[rows, _pair_cols(p)], kb_ref[rows, _pair_cols(p)],
                              vb_ref[rows, _pair_v_cols(p)], la_ref[rows, _pair_cols(p)],
                              [carry_ref[2 * p], carry_ref[2 * p + 1]],
                              _gla_rows_emit(ob_ref.at[rows, :], p)) for p in pairs]
        q_heads, lo = _split_heads(q_ref.at[rows, :], n_pairs, tb)
        _sb_core(q_heads, lo, kv_block(qi), strictly_earlier, lambda n: kv_block(qi - 1 - n), qi,
                 acc_ref, c_ref, o_ref.at[rows, :], n_pairs, companions=gla, on_companions=keep_states)
        return carry

    lax.fori_loop(0, blocks_per_step, one_block, 0)


def _sb_sample_kernel(q_ref, kn_ref, vn_ref, kc_ref, vc_ref, o_ref, acc_ref, c_ref, kpad_ref, vpad_ref,
                      *, ts, tk, n_cache_blocks, n_pairs):
    q_heads, lo = _split_heads(q_ref, n_pairs, ts)
    kpad_ref[...] = jnp.zeros_like(kpad_ref)
    vpad_ref[...] = jnp.zeros_like(vpad_ref)
    kpad_ref[:ts, :] = kn_ref[...]
    vpad_ref[:ts, :] = vn_ref[...]

    def cache_block(n):
        keys = pl.ds(pl.multiple_of((n_cache_blocks - 1 - n) * tk, tk), tk)
        return _KV(kc_ref[:, keys].astype(BF16), vc_ref[:, keys].astype(BF16), True)

    strictly_earlier = (lax.broadcasted_iota(jnp.int32, (ts, tk), 1)
                        < lax.broadcasted_iota(jnp.int32, (ts, tk), 0))
    _sb_core(q_heads, lo, _KV(kpad_ref[...], vpad_ref[...], False), strictly_earlier, cache_block,
             n_cache_blocks, acc_ref, c_ref, o_ref, n_pairs)


def _mixer_ab_prompt(q, k, v, qb, kb, vb, la, ride_stacks, layer):
    b, t, _ = q.shape
    tb = ATT_BLOCK
    assert t % tb == 0 and tb % GLA_CHUNK == 0 and tb <= GLA_MAX_TILE
    n_pairs = A_WIDTH // LANES
    per_step = math.gcd(t // tb, MIXER_BLOCKS_PER_STEP)
    rows = lambda width: pl.BlockSpec((None, per_step * tb, width), lambda bi, qi: (bi, qi, 0))
    kv_spec = pl.BlockSpec((None, A_WIDTH, t), lambda bi, qi: (bi, 0, 0))
    state = (B_HEADS, B_DK, B_DV)
    grid = (b, t // (per_step * tb))
    ride_in, ride_out, ride_shapes = _Riders(ride_stacks, layer, grid).specs()
    return pl.pallas_call(
        functools.partial(_mixer_ab_prompt_kernel, tb=tb, n_pairs=n_pairs, blocks_per_step=per_step,
                          n_riders=len(ride_stacks)),
        grid=grid,
        in_specs=[rows(A_WIDTH), kv_spec, kv_spec, rows(B_KW), rows(B_KW), rows(B_VW), rows(B_KW)] + ride_in,
        out_specs=[rows(A_WIDTH), rows(B_VW),
                   pl.BlockSpec((None, *state), lambda bi, qi: (bi, 0, 0, 0))] + ride_out,
        out_shape=[jax.ShapeDtypeStruct(q.shape, BF16), jax.ShapeDtypeStruct(vb.shape, F32),
                   jax.ShapeDtypeStruct((b, *state), F32)] + ride_shapes,
        scratch_shapes=[pltpu.VMEM((tb, A_WIDTH), F32), pltpu.VMEM((2 * n_pairs, tb, 1), F32),
                        pltpu.VMEM((B_HEADS, LANES, B_DV), F32)],
        compiler_params=_params("arbitrary", "arbitrary"),
        name="mixer_ab",
    )(q, k, v, qb, kb, vb, la, *ride_stacks)


def _sb_attention_sample(q, k_new, v_new, k_cache, v_cache, layer):
    b, ts, _ = q.shape
    past = k_cache.shape[3]
    tk = ATT_BLOCK
    assert past % tk == 0 and ts <= tk
    n_pairs = SB_PAIRS_PER_STEP
    w = n_pairs * LANES
    new_spec = pl.BlockSpec((None, ts, w), lambda bi, hp: (bi, 0, hp))
    cache_spec = pl.BlockSpec((None, None, w, past), lambda bi, hp: (layer, bi, hp, 0))
    return pl.pallas_call(
        functools.partial(_sb_sample_kernel, ts=ts, tk=tk, n_cache_blocks=past // tk, n_pairs=n_pairs),
        grid=(b, A_WIDTH // w),
        in_specs=[new_spec, new_spec, new_spec, cache_spec, cache_spec],
        out_specs=new_spec,
        out_shape=jax.ShapeDtypeStruct(q.shape, BF16),
        scratch_shapes=[pltpu.VMEM((ts, w), F32), pltpu.VMEM((2 * n_pairs, ts, 1), F32),
                        pltpu.VMEM((tk, w), BF16), pltpu.VMEM((tk, w), BF16)],
        compiler_params=_params("parallel", "parallel"),
        name="sb_attention_sample",
    )(q, k_new, v_new, k_cache, v_cache)


def _band_core(q_heads, lo, kv, pens, bias, o_ref, n_pairs, skip_hidden_quarters=False):
    heads = range(2 * n_pairs)
    tq = q_heads[0].shape[0]
    everything = [(slice(None), [(kv[i], i, slice(None)) for i in range(len(kv))])]
    groups = _band_quarters(kv, tq) if skip_hidden_quarters else everything
    chains = [(h, g) for h in heads for g in range(len(groups))]
    z = [[piece.scores(q_heads[h][groups[g][0], :], h // 2) + bias(h, i)[groups[g][0], cols]
          for piece, i, cols in groups[g][1]] for h, g in chains]
    acc = {}
    for c, (h, g) in enumerate(chains):
        pieces = groups[g][1]
        m = None
        for j, (_, i, _) in enumerate(pieces):
            mi = jnp.max(z[c][j], axis=-1, keepdims=True)
            if pens[i] is not None:
                mi = mi + pens[i]
            m = mi if m is None else jnp.maximum(m, mi)
        a = None
        for j, (piece, i, _) in enumerate(pieces):
            shift = m if pens[i] is None else m - pens[i]
            p = jnp.exp2(z[c][j] - shift).astype(BF16)
            v = piece.values(h // 2)
            ones = jnp.ones_like(v)
            first = piece.head_lanes()
            v = jnp.where(first, v, ones) if h % 2 == 0 else jnp.where(first, ones, v)
            pv = piece.weighted(p, v)
            a = pv if a is None else a + pv
        acc[h, g] = a
    per_head = [acc[h, 0] if len(groups) == 1 else jnp.concatenate([acc[h, g] for g in range(len(groups))], 0)
                for h in heads]
    for p in range(n_pairs):
        a0, a1 = per_head[2 * p], per_head[2 * p + 1]
        o_ref[:, _pair_cols(p)] = jnp.where(lo, a0 / pltpu.roll(a0, HEAD_DIM, axis=1),
                                            a1 / pltpu.roll(a1, HEAD_DIM, axis=1)).astype(o_ref.dtype)


def _band_quarters(kv, tq):
    own, back1, back2 = kv
    half = tq // 2
    assert own.n_keys == tq and half % CHUNK == 0 and 2 * (tq // CHUNK) >= C_LEFT_CHUNKS
    whole = slice(None)
    return [(slice(0, half), [(own.first(half), 0, slice(0, half)), (back1, 1, whole), (back2, 2, whole)]),
            (slice(half, tq), [(own, 0, whole), (back1, 1, whole), (back2.last(half), 2, slice(half, tq))])]


def _band_prompt_kernel(q_ref, k_ref, v_ref, bias_ref, *refs, tq, tk, n_pairs, blocks_per_step):
    n_riders = (len(refs) - 1) // 2
    o_ref = refs[n_riders]
    _cast_riders(list(zip(refs[:n_riders], refs[n_riders + 1:])))

    def one_block(s, carry):
        qi = pl.program_id(2) * blocks_per_step + s
        rows = pl.ds(pl.multiple_of(s * tq, tq), tq)
        q_heads, lo = _split_heads(q_ref.at[rows, :], n_pairs, tq)
        kv, pens = [], []
        for dj in range(3):
            j = qi - dj
            pens.append(None if dj == 0 else jnp.where(j >= 0, 0.0, NEG_BIG).astype(F32))
            keys = pl.ds(pl.multiple_of(jnp.maximum(j, 0) * tk, tk), tk)
            kv.append(_KV(k_ref[:, keys], v_ref[:, keys], True))
        _band_core(q_heads, lo, kv, pens, lambda h, i: bias_ref.at[h, i], o_ref.at[rows, :], n_pairs,
                   skip_hidden_quarters=True)
        return carry

    lax.fori_loop(0, blocks_per_step, one_block, 0)


def _band_sample_kernel(q_ref, kn_ref, vn_ref, kc_ref, vc_ref, bias_ref, o_ref, kpad_ref, vpad_ref,
                        *, ts, tk, n_cache_blocks, n_pairs):
    q_heads, lo = _split_heads(q_ref, n_pairs, ts)
    kpad_ref[...] = jnp.zeros_like(kpad_ref)
    vpad_ref[...] = jnp.zeros_like(vpad_ref)
    kpad_ref[:ts, :] = kn_ref[...]
    vpad_ref[:ts, :] = vn_ref[...]
    kv = [_KV(kpad_ref[...], vpad_ref[...], False)]
    for dj in range(1, n_cache_blocks + 1):
        keys = slice((n_cache_blocks - dj) * tk, (n_cache_blocks - dj + 1) * tk)
        kv.append(_KV(kc_ref[:, keys].astype(BF16), vc_ref[:, keys].astype(BF16), True))
    is_new_key = lax.broadcasted_iota(jnp.int32, (ts, tk), 1) < ts

    def bias(h, i):
        return jnp.where(is_new_key, bias_ref[h, 0], NEG_BIG) if i == 0 else bias_ref[h, i]

    _band_core(q_heads, lo, kv, [None] * len(kv), bias, o_ref, n_pairs)


def _band_attention_prompt(q, k, v, bias, ride_stacks, layer):
    b, t, _ = q.shape
    tq = tk = ATT_BLOCK
    assert t % tk == 0
    n_pairs = BAND_PAIRS_PER_STEP
    w = n_pairs * LANES
    per_step = math.gcd(t // tq, BAND_BLOCKS_PER_STEP)
    kern = functools.partial(_band_prompt_kernel, tq=tq, tk=tk, n_pairs=n_pairs, blocks_per_step=per_step)
    kv_spec = pl.BlockSpec((None, w, t), lambda bi, hp, qi: (bi, hp, 0))
    q_spec = pl.BlockSpec((None, per_step * tq, w), lambda bi, hp, qi: (bi, qi, hp))
    bias_spec = pl.BlockSpec((2 * n_pairs, 3, tq, tk), lambda bi, hp, qi: (hp, 0, 0, 0))
    grid = (b, C_WIDTH // w, t // (per_step * tq))
    ride_in, ride_out, ride_shapes = _Riders(ride_stacks, layer, grid).specs()
    return pl.pallas_call(
        kern,
        grid=grid,
        in_specs=[q_spec, kv_spec, kv_spec, bias_spec] + ride_in,
        out_specs=[q_spec] + ride_out,
        out_shape=[jax.ShapeDtypeStruct(q.shape, BF16)] + ride_shapes,
        compiler_params=_params("arbitrary", "arbitrary", "arbitrary"),
        name="band_attention",
    )(q, k, v, bias, *ride_stacks)


def _band_attention_sample(q, k_new, v_new, k_cache, v_cache, layer, bias):
    b, ts, _ = q.shape
    wc = k_cache.shape[3]
    tk = ATT_BLOCK
    n_cache_blocks = min(wc // tk, 2)
    assert wc % (n_cache_blocks * tk) == 0 and ts <= tk
    n_pairs = C_WIDTH // LANES
    w = n_pairs * LANES
    kern = functools.partial(_band_sample_kernel, ts=ts, tk=tk, n_cache_blocks=n_cache_blocks,
                             n_pairs=n_pairs)
    new_spec = pl.BlockSpec((None, ts, w), lambda bi, hp: (bi, 0, hp))
    cache_rows = n_cache_blocks * tk
    cache_spec = pl.BlockSpec((None, None, w, cache_rows),
                              lambda bi, hp: (layer, bi, hp, wc // cache_rows - 1))
    bias_spec = pl.BlockSpec((2 * n_pairs, 1 + n_cache_blocks, ts, tk), lambda bi, hp: (hp, 0, 0, 0))
    return pl.pallas_call(
        kern,
        grid=(b, C_WIDTH // w),
        in_specs=[new_spec, new_spec, new_spec, cache_spec, cache_spec, bias_spec],
        out_specs=new_spec,
        out_shape=jax.ShapeDtypeStruct(q.shape, BF16),
        scratch_shapes=[pltpu.VMEM((tk, w), BF16), pltpu.VMEM((tk, w), BF16)],
        compiler_params=_params("parallel", "parallel"),
        name="band_attention_sample",
    )(q, k_new, v_new, k_cache, v_cache, bias)


def _band_bias_kernel(g_ref, o_ref):
    rows = tk = ATT_BLOCK
    q_chunk = lax.broadcasted_iota(jnp.int32, (rows, tk), 0) // CHUNK
    k_chunk = lax.broadcasted_iota(jnp.int32, (rows, tk), 1) // CHUNK
    for dj in range(3):
        diff = dj * (tk // CHUNK) + q_chunk - k_chunk
        seen = (diff >= 0) & (diff <= C_LEFT_CHUNKS)
        for h in range(g_ref.shape[0]):
            g = jnp.broadcast_to(g_ref[h, dj], (rows, 2 * tk))
            tile = pltpu.roll(g, 0, axis=1, stride=1, stride_axis=0)[:, :tk]
            o_ref[h, dj] = jnp.where(seen, tile * LOG2E, NEG_BIG)


def _band_bias(rel_table):
    rows = tk = ATT_BLOCK
    c = jnp.arange(2 * tk, dtype=jnp.int32)
    u = jnp.where(c <= tk, -c, 2 * tk - c)
    idx = jnp.clip(jnp.arange(3, dtype=jnp.int32)[:, None] * tk + u[None, :], REL_MIN, REL_MAX) - REL_MIN
    g = rel_table[:, idx].astype(F32).reshape(C_HEADS, 3, 1, 2 * tk)
    return pl.pallas_call(
        _band_bias_kernel,
        grid=(C_HEADS // BIAS_HEADS_PER_STEP,),
        in_specs=[pl.BlockSpec((BIAS_HEADS_PER_STEP, 3, 1, 2 * tk), lambda h: (h, 0, 0, 0))],
        out_specs=pl.BlockSpec((BIAS_HEADS_PER_STEP, 3, rows, tk), lambda h: (h, 0, 0, 0)),
        out_shape=jax.ShapeDtypeStruct((C_HEADS, 3, rows, tk), F32),
        compiler_params=_params("parallel"),
        name="band_bias",
    )(g)


def _gla_pair_tile(q, k, v, la, states, emit):
    L = GLA_CHUNK
    chunks = range(q.shape[0] // L)
    row = lax.broadcasted_iota(jnp.int32, (L, L), 0)
    colm = lax.broadcasted_iota(jnp.int32, (L, L), 1)
    tri = (colm <= row).astype(BF16)
    causal = colm <= row
    lane = lax.broadcasted_iota(jnp.int32, (L, LANES), 1)
    sub = lax.broadcasted_iota(jnp.int32, (LANES, B_DV), 0)
    mine = [(lane >= h * B_DK) & (lane < (h + 1) * B_DK) for h in range(2)]
    mine_rows = [(sub >= h * B_DK) & (sub < (h + 1) * B_DK) for h in range(2)]
    rows = [slice(c * L, (c + 1) * L) for c in chunks]

    b = []
    for r in rows:
        g_hi, g_lo = _split_bf16(la[r, :])
        b.append(_dot(tri, g_hi) + _dot(tri, g_lo))
    yield
    qg, qg_h, kg, kd_t, decay = [], [], [], [], []
    for c, r in zip(chunks, rows):
        qg_c = q[r, :] * (B_DK ** -0.5) * jnp.exp(b[c])
        qg.append(qg_c.astype(BF16))
        qg_h.append([jnp.where(mine[h], qg_c, 0.0).astype(BF16) for h in range(2)])
        kg.append((k[r, :] * jnp.exp(-b[c])).astype(BF16))
        b_t = b[c].T
        b_last = b_t[:, L - 1:L]
        kd_t.append((k[r, :].T * jnp.exp(b_last - b_t)).astype(BF16))
        decay.append(jnp.exp(b_last))
    yield
    att =[[jnp.where(causal, _dot_nt(qg_h[c][h], kg[c]), 0.0).astype(BF16) for h in range(2)]
           for c in chunks]
    yield
    o_intra, own = [], []
    for c, r in zip(chunks, rows):
        v_h = [v[r, h * B_DV:(h + 1) * B_DV].astype(BF16) for h in range(2)]
        o_intra.append([_dot(att[c][h], v_h[h]) for h in range(2)])
        own.append([jnp.where(mine_rows[h], _dot(kd_t[c], v_h[h]), 0.0) for h in range(2)])

    yield
    states = list(states)
    start = []
    for c in chunks:
        start.append([s.astype(BF16) for s in states])
        states = [decay[c] * states[h] + own[c][h] for h in range(2)]
    yield
    for c in chunks:
        for h in range(2):
            emit(c, h, o_intra[c][h] + _dot(qg[c], start[c][h]))
    return states


def _pair_states(s0_ref, p):
    zeros_state = jnp.zeros((B_DK, B_DV), F32)
    return [jnp.concatenate([s0_ref[2 * p], zeros_state], axis=0),
            jnp.concatenate([zeros_state, s0_ref[2 * p + 1]], axis=0)]


def _pair_v_cols(p):
    return slice(2 * p * B_DV, 2 * (p + 1) * B_DV)


def _gla_rows_emit(o_ref, p):
    def emit(c, h, o):
        o_ref[c * GLA_CHUNK:(c + 1) * GLA_CHUNK, (2 * p + h) * B_DV:(2 * p + h + 1) * B_DV] = o
    return emit


def _gla_kernel(q_ref, k_ref, v_ref, la_ref, s0_ref, o_ref, s_out_ref):
    pairs = range(B_HEADS // 2)
    states = _interleave([
        _gla_pair_tile(q_ref[:, _pair_cols(p)], k_ref[:, _pair_cols(p)], v_ref[:, _pair_v_cols(p)],
                       la_ref[:, _pair_cols(p)], _pair_states(s0_ref, p), _gla_rows_emit(o_ref, p))
        for p in pairs])
    for p in pairs:
        s_out_ref[2 * p] = states[p][0][:B_DK, :]
        s_out_ref[2 * p + 1] = states[p][1][B_DK:, :]


def _gla(q, k, v, la, s0):
    b, t, _ = q.shape
    assert t % GLA_CHUNK == 0 and t <= GLA_MAX_TILE
    qk_spec = pl.BlockSpec((None, t, B_KW), lambda bi: (bi, 0, 0))
    v_spec = pl.BlockSpec((None, t, B_VW), lambda bi: (bi, 0, 0))
    s_spec = pl.BlockSpec((None, B_HEADS, B_DK, B_DV), lambda bi: (bi, 0, 0, 0))
    return pl.pallas_call(
        _gla_kernel,
        grid=(b,),
        in_specs=[qk_spec, qk_spec, v_spec, qk_spec, s_spec],
        out_specs=[v_spec, s_spec],
        out_shape=[jax.ShapeDtypeStruct(v.shape, F32), jax.ShapeDtypeStruct(s0.shape, F32)],
        compiler_params=_params("parallel"),
        name="gla",
    )(q, k, v, la, s0)


def _pad_rows(x, n):
    return jnp.pad(x, ((0, 0), (0, n - x.shape[1]), (0, 0)))


def _heads_last(x, heads):
    n, b, _, s = x.shape
    return jnp.transpose(x.reshape(n, b, heads, HEAD_DIM, s), (0, 1, 4, 2, 3))


def _feature_major(cache):
    n, b, s, heads, hd = cache.shape
    return jnp.transpose(cache, (0, 1, 3, 4, 2)).reshape(n, b, heads * hd, s)


def _row_tile(m):
    for tm in (512, 256, 128, 64, 32, 16, 8):
        if m % tm == 0:
            return tm
    raise ValueError(f"token count {m} is not a multiple of 8")


def kernel(x_prompt, x_sample, cache_a_k, cache_a_v, state_b, cache_c_k, cache_c_v, norm_mix_g, norm_ffn_g, w_in_ab, w_gate_b, b_gate_b, norm_gla_g, w_out_ab, w_qkv_c, rel_bias_c, w_out_c, w_ffn_gate, w_ffn_up, w_ffn_down, norm_final_g):
    bp, tp, d = x_prompt.shape
    bs, ts, _ = x_sample.shape
    depth = norm_mix_g.shape[0]
    past = cache_a_k.shape[2]
    wc = cache_c_k.shape[2]
    assert tp % ATT_BLOCK == 0 and past % ATT_BLOCK == 0 and wc % ATT_BLOCK == 0
    assert ts <= GLA_CHUNK and ts % 8 == 0
    mp, ms = bp * tp, bs * ts
    tmp, tms = _row_tile(tp), _row_tile(ms)
    xp = x_prompt.reshape(mp, d)
    xs = x_sample.reshape(ms, d)
    row2 = lambda a: a.reshape(1, -1)

    a_ks, a_vs, b_sp, b_ss, c_ks, c_vs = [], [], [], [], [], []
    a_kv_prompt = c_kv_prompt = None
    n_ab, n_c = (depth + 1) // 2, depth // 2
    keep = min(C_LEFT_CHUNKS * CHUNK, tp)

    kv0, kv1 = A_WIDTH, 3 * A_WIDTH
    o = 3 * A_WIDTH + 2 * B_KW + B_VW
    w_in_t = jnp.swapaxes(w_in_ab, 1, 2)
    w_main_all = jnp.concatenate([w_in_t[:, :kv0], w_in_t[:, kv1:o], w_in_t[:, o + B_GATE_RANK:]],
                                 axis=1).astype(BF16)
    w_kv_ab_t_all = w_in_t[:, kv0:kv1].astype(BF16)
    w_lr_all = jnp.pad(w_in_t[:, o:o + B_GATE_RANK],
                       ((0, 0), (0, LANES - B_GATE_RANK), (0, 0))).astype(BF16)
    w_gate_all = jnp.pad(w_gate_b, ((0, 0), (0, LANES - B_GATE_RANK), (0, 0))).astype(BF16)
    w_out_ab_all = w_out_ab.astype(BF16)
    w_q_all = w_qkv_c[:, :, :C_WIDTH].astype(BF16)
    w_kv_c_t_all = jnp.swapaxes(w_qkv_c[:, :, C_WIDTH:], 1, 2).astype(BF16)
    w_out_c_all = w_out_c.astype(BF16)
    cache_a_k_fm, cache_a_v_fm, cache_c_k_fm, cache_c_v_fm = (
        _feature_major(c) for c in (cache_a_k, cache_a_v, cache_c_k, cache_c_v))

    for layer in range(depth):
        i = layer // 2
        g_mix = row2(norm_mix_g[layer])
        ffn_f32 = (w_ffn_gate, w_ffn_up, w_ffn_down)
        ffn_ends = (row2(norm_ffn_g[layer]), row2(norm_final_g) if layer == depth - 1 else None)
        if layer % 2 == 0:
            w_main, w_kv_t, w_lr, w_gate, w_out = (
                _Slab(w, i) for w in (w_main_all, w_kv_ab_t_all, w_lr_all, w_gate_all, w_out_ab_all))
            b_gate = row2(b_gate_b[i])
            g_gla = row2(norm_gla_g[i])

            qa, ka, va, kab, vab, qb, kb, vb, r, la = _proj_ab(
                xp, g_mix, w_main, w_kv_t, w_lr, w_gate, b_gate, tmp, batch=bp,
                stack=_KVStack(i, n_ab, tp, a_kv_prompt))
            a_kv_prompt = (ka, va)
            sh = lambda a: a.reshape(bp, tp, -1)
            oa, ob, sbp, *ffn_w = _mixer_ab_prompt(sh(qa), kab, vab, sh(qb), sh(kb), sh(vb), sh(la),
                                                   ffn_f32, layer)
            ffn = (ffn_ends[0], *ffn_w, ffn_ends[1])
            mixed_p = (oa.reshape(mp, -1), ob.reshape(mp, -1), r)
            b_sp.append(sbp)

            qa, ka, va, kab, vab, qb, kb, vb, r, la = _proj_ab(
                xs, g_mix, w_main, w_kv_t, w_lr, w_gate, b_gate, tms)
            sh = lambda a: a.reshape(bs, ts, -1)
            oa = _sb_attention_sample(sh(qa), sh(kab), sh(vab), cache_a_k_fm, cache_a_v_fm, i)
            pad_t = lambda a: _pad_rows(sh(a), GLA_CHUNK)
            ob, sbs = _gla(pad_t(qb), pad_t(kb), pad_t(vb), pad_t(la), state_b[i])
            mixed_s = (oa.reshape(ms, -1), ob[:, :ts].reshape(ms, -1), r)
            xp, xs = _layer_tail(xp, mixed_p, xs, mixed_s, (g_gla, w_out), *ffn, tmp)
            a_ks.append(ka.reshape(bs, ts, A_HEADS, HEAD_DIM))
            a_vs.append(va.reshape(bs, ts, A_HEADS, HEAD_DIM))
            b_ss.append(sbs)
        else:
            w_q, w_kv_t, w_out = (_Slab(w, i) for w in (w_q_all, w_kv_c_t_all, w_out_c_all))

            q, k, v, kb16, vb16 = _proj_c(xp, g_mix, w_q, w_kv_t, tmp, batch=bp,
                                          stack=_KVStack(i, n_c, keep, c_kv_prompt))
            c_kv_prompt = (k, v)
            bias = _band_bias(rel_bias_c[i])
            oc, *ffn_w = _band_attention_prompt(q.reshape(bp, tp, -1), kb16, vb16, bias, ffn_f32, layer)
            ffn = (ffn_ends[0], *ffn_w, ffn_ends[1])
            mixed_p = (oc.reshape(mp, -1),)

            q, k, v, kb16, vb16 = _proj_c(xs, g_mix, w_q, w_kv_t, tms)
            sh = lambda a: a.reshape(bs, ts, -1)
            oc = _band_attention_sample(sh(q), sh(kb16), sh(vb16), cache_c_k_fm, cache_c_v_fm, i, bias)
            xp, xs = _layer_tail(xp, mixed_p, xs, (oc.reshape(ms, -1),), (w_out,), *ffn, tmp)
            c_ks.append(k.reshape(bs, ts, C_HEADS, HEAD_DIM))
            c_vs.append(v.reshape(bs, ts, C_HEADS, HEAD_DIM))

    y_prompt = xp.reshape(bp, tp, d)
    y_sample = xs.reshape(bs, ts, d)
    a_kp, a_vp = (_heads_last(a, A_HEADS) for a in a_kv_prompt)
    c_kp, c_vp = (_heads_last(a, C_HEADS) for a in c_kv_prompt)
    return (y_prompt, y_sample, a_kp, a_vp, jnp.stack(a_ks), jnp.stack(a_vs),
            jnp.stack(b_sp), jnp.stack(b_ss), c_kp, c_vp, jnp.stack(c_ks), jnp.stack(c_vs))
```

```python
import functools
import math

import jax
import jax.numpy as jnp
from jax import lax
from jax.experimental import pallas as pl
from jax.experimental.pallas import tpu as pltpu

F32 = jnp.float32
BF16 = jnp.bfloat16

EPS = 1e-6
HEAD_DIM = 64
LANES = 128
A_HEADS = 8
A_WIDTH = A_HEADS * HEAD_DIM
B_HEADS = 4
B_DK = 64
B_DV = 128
B_KW = B_HEADS * B_DK
B_VW = B_HEADS * B_DV
B_GATE_RANK = 16
B_GATE_TEMP = 16.0
GLA_CHUNK = 64
C_HEADS = 16
C_WIDTH = C_HEADS * HEAD_DIM
CHUNK = 64
C_LEFT_CHUNKS = 8
REL_MIN = -(CHUNK - 1)
REL_MAX = 128
ATT_BLOCK = 256
NEG_BIG = -1e30
LOG2E = 1.4426950408889634
SB_DEAD = 104.0
SB_PAIRS_PER_STEP = 4
BAND_PAIRS_PER_STEP = 4
BIAS_HEADS_PER_STEP = 4
MIXER_BLOCKS_PER_STEP = 4
BAND_BLOCKS_PER_STEP = 8
GLA_MAX_TILE = 512
VMEM_LIMIT = 56 * 1024 * 1024


def _params(*sem):
    return pltpu.CompilerParams(dimension_semantics=sem, vmem_limit_bytes=VMEM_LIMIT)


class _Slab:
    def __init__(self, stacked, index):
        self.stacked, self.index, self.shape = stacked, index, stacked.shape[1:]


def _operand(a):
    return a.stacked if isinstance(a, _Slab) else a


def _resident(a):
    if isinstance(a, _Slab):
        index = (a.index,) + (0,) * len(a.shape)
        return pl.BlockSpec((None, *a.shape), lambda *_: index, pipeline_mode=pl.Buffered(1))
    return pl.BlockSpec(a.shape, lambda *_: (0,) * a.ndim, pipeline_mode=pl.Buffered(1))


def _rms(x, g):
    return x * lax.rsqrt(jnp.mean(x * x, axis=-1, keepdims=True) + EPS) * g


def _log_sigmoid(z):
    return jnp.minimum(z, 0.0) - jnp.log1p(jnp.exp(-jnp.abs(z)))


def _split_bf16(x):
    hi = x.astype(BF16)
    lo = (x - hi.astype(F32)).astype(BF16)
    return hi, lo


def _dot(a, b):
    return jnp.dot(a, b, preferred_element_type=F32)


def _dot_nt(a, b):
    return lax.dot_general(a, b, (((1,), (1,)), ((), ())), preferred_element_type=F32)


def _pair_cols(p):
    return slice(p * LANES, (p + 1) * LANES)


class _KV:
    def __init__(self, k, v, feature_major):
        self.k, self.v, self.feature_major = k, v, feature_major
        self.n_keys = k.shape[1] if feature_major else k.shape[0]

    def scores(self, q_h, p):
        if self.feature_major:
            return _dot(q_h, self.k[_pair_cols(p), :])
        return _dot_nt(q_h, self.k[:, _pair_cols(p)])

    def first(self, n):
        if self.feature_major:
            return _KV(self.k[:, :n], self.v[:, :n], True)
        return _KV(self.k[:n], self.v[:n], False)

    def last(self, n):
        if self.feature_major:
            return _KV(self.k[:, -n:], self.v[:, -n:], True)
        return _KV(self.k[-n:], self.v[-n:], False)

    def values(self, p):
        return self.v[_pair_cols(p), :] if self.feature_major else self.v[:, _pair_cols(p)]

    def weighted(self, w, v_p):
        return _dot_nt(w, v_p) if self.feature_major else _dot(w, v_p)

    def head_lanes(self):
        shape = (LANES, self.n_keys) if self.feature_major else (self.n_keys, LANES)
        return lax.broadcasted_iota(jnp.int32, shape, 0 if self.feature_major else 1) < HEAD_DIM


def _split_heads(q_ref, n_pairs, tq):
    lo = lax.broadcasted_iota(jnp.int32, (tq, LANES), 1) < HEAD_DIM
    heads = []
    for p in range(n_pairs):
        q = q_ref[:, _pair_cols(p)]
        heads += [jnp.where(lo, q, jnp.zeros_like(q)), jnp.where(lo, jnp.zeros_like(q), q)]
    return heads, lo


def _emit_kv(y, wkv_ref, k_ref, v_ref, kb_ref, vb_ref, feature_major, kept_tail=None, own_slot=0):
    if not feature_major:
        kv = _dot_nt(y, wkv_ref[...])
        width = kv.shape[1] // 2
        k, v = kv[:, :width], kv[:, width:]
        k_ref[...], v_ref[...], kb_ref[...], vb_ref[...] = k, v, k.astype(BF16), v.astype(BF16)
        return
    kv = _dot_nt(wkv_ref[...], y)
    width = kv.shape[0] // 2
    k, v = kv[:width, :], kv[width:, :]
    kb_ref[...] = k.astype(BF16)
    vb_ref[...] = v.astype(BF16)
    slot = own_slot
    other_slots = [s for s in range(k_ref.shape[0]) if s != slot]
    if kept_tail is None:
        fill, fill_cols = slice(None), k_ref.shape[2]
        k_ref[slot] = k
        v_ref[slot] = v
    else:
        n_tiles, n_kept = kept_tail
        tile = pl.program_id(0) % n_tiles
        tm = k.shape[1]
        cols = pl.ds(pl.multiple_of(jnp.maximum(tile - (n_tiles - n_kept), 0) * tm, tm), tm)
        k_ref[slot, :, cols] = k
        v_ref[slot, :, cols] = v
        share = k_ref.shape[2] // n_tiles
        if share % LANES == 0 and share * n_tiles == k_ref.shape[2]:
            fill, fill_cols = pl.ds(pl.multiple_of(tile * share, share), share), share
        else:
            fill, fill_cols = slice(None), k_ref.shape[2]
    for dst in (k_ref, v_ref):
        for s in other_slots:
            dst[s, :, fill] = jnp.zeros((dst.shape[1], fill_cols), dst.dtype)


class _Riders:
    def __init__(self, stacks, layer, grid):
        self.stacks, self.layer, self.grid = list(stacks), layer, tuple(grid)
        self.chunks = next(c for c in (16, 8, 4, 2, 1)
                           if c <= math.prod(grid) and all(w.shape[1] % (16 * c) == 0 for w in self.stacks))

    def _chunk(self, *idx):
        step = 0
        for i, n in zip(idx, self.grid):
            step = step * n + i
        return jnp.minimum(step, self.chunks - 1)

    def specs(self):
        layer = self.layer
        ins = [pl.BlockSpec((None, w.shape[1] // self.chunks, w.shape[2]),
                            lambda *idx: (layer, self._chunk(*idx), 0)) for w in self.stacks]
        outs = [pl.BlockSpec((w.shape[1] // self.chunks, w.shape[2]),
                             lambda *idx: (self._chunk(*idx), 0)) for w in self.stacks]
        shapes = [jax.ShapeDtypeStruct(w.shape[1:], BF16) for w in self.stacks]
        return ins, outs, shapes


def _cast_riders(pairs):
    for src, dst in pairs:
        dst[...] = src[...].astype(BF16)


def _proj_ab_kernel(x_ref, g_ref, w_ref, wkv_ref, wlr_ref, wgate_ref, bgate_ref, *refs,
                    feature_major, n_aliased, own_slot):
    qa_ref, ka_ref, va_ref, kab_ref, vab_ref, qb_ref, kb_ref, vb_ref, r_ref, la_ref = refs[n_aliased:]
    y = _rms(x_ref[...], g_ref[...]).astype(BF16)
    _emit_kv(y, wkv_ref, ka_ref, va_ref, kab_ref, vab_ref, feature_major, own_slot=own_slot)
    z = _dot_nt(y, w_ref[...])
    c = 0
    qa_ref[...] = (z[:, c:c + A_WIDTH] * (HEAD_DIM ** -0.5)).astype(BF16); c += A_WIDTH
    qb_ref[...] = z[:, c:c + B_KW].astype(BF16); c += B_KW
    kb_ref[...] = z[:, c:c + B_KW].astype(BF16); c += B_KW
    vb_ref[...] = z[:, c:c + B_VW].astype(BF16); c += B_VW
    r_ref[...] = z[:, c:c + B_VW].astype(BF16)
    g_lr = _dot_nt(y, wlr_ref[...])
    gate = _dot(g_lr.astype(BF16), wgate_ref[...]) + bgate_ref[...]
    la_ref[...] = _log_sigmoid(gate) * (1.0 / B_GATE_TEMP)


class _KVStack:
    def __init__(self, layer, n_layers, keep, previous=None):
        self.layer, self.n_layers, self.keep, self.previous = layer, n_layers, keep, previous


def _kv_out(m, width, tm, batch, stack):
    if batch is None:
        spec = pl.BlockSpec((tm, width), lambda i: (i, 0))
        return [spec] * 4, [jax.ShapeDtypeStruct((m, width), dt) for dt in (F32, F32, BF16, BF16)], None, 0
    t = m // batch
    assert t % tm == 0 and stack.keep % tm == 0 and stack.keep <= t
    n_tiles = t // tm
    copy_spec = pl.BlockSpec((None, width, tm), lambda i: (i // n_tiles, 0, i % n_tiles))
    copy_shape = jax.ShapeDtypeStruct((batch, width, t), BF16)
    slots, first_slot, own_slot = ((stack.n_layers, 0, stack.layer) if stack.previous is None
                                   else (1, stack.layer, 0))
    if stack.keep == t:
        f32_spec = pl.BlockSpec((slots, None, width, tm), lambda i: (first_slot, i // n_tiles, 0, i % n_tiles))
        kept_tail = None
    else:
        f32_spec = pl.BlockSpec((slots, None, width, stack.keep), lambda i: (first_slot, i // n_tiles, 0, 0))
        kept_tail = (n_tiles, stack.keep // tm)
    f32_shape = jax.ShapeDtypeStruct((stack.n_layers, batch, width, stack.keep), F32)
    return ([f32_spec, f32_spec, copy_spec, copy_spec], [f32_shape, f32_shape, copy_shape, copy_shape],
            kept_tail, own_slot)


def _aliased_stack(stack, n_inputs, first_output):
    if stack is None or stack.previous is None:
        return [], [], {}
    prev = list(stack.previous)
    specs = [pl.BlockSpec(memory_space=pl.ANY)] * len(prev)
    return prev, specs, {n_inputs + j: first_output + j for j in range(len(prev))}


def _proj_ab(x, g, w_main, w_kv, w_lr, w_gate, b_gate, tm, batch=None, stack=None):
    m, d = x.shape
    row = lambda n: pl.BlockSpec((tm, n), lambda i: (i, 0))
    params = (g, w_main, w_kv, w_lr, w_gate, b_gate)
    kv_specs, kv_shapes, kept_tail, own_slot = _kv_out(m, A_WIDTH, tm, batch, stack)
    assert kept_tail is None
    prev, prev_specs, aliases = _aliased_stack(stack, 1 + len(params), 1)
    rest = [(B_KW, BF16), (B_KW, BF16), (B_VW, BF16), (B_VW, BF16), (B_KW, F32)]
    return pl.pallas_call(
        functools.partial(_proj_ab_kernel, feature_major=batch is not None, n_aliased=len(prev),
                          own_slot=own_slot),
        grid=(m // tm,),
        in_specs=[row(d)] + [_resident(a) for a in params] + prev_specs,
        out_specs=[row(A_WIDTH)] + kv_specs + [row(n) for n, _ in rest],
        out_shape=([jax.ShapeDtypeStruct((m, A_WIDTH), BF16)] + kv_shapes
                   + [jax.ShapeDtypeStruct((m, n), dt) for n, dt in rest]),
        input_output_aliases=aliases,
        compiler_params=_params("parallel"),
        name="proj_ab",
    )(x, *[_operand(a) for a in params], *prev)


def _proj_c_kernel(x_ref, g_ref, wq_ref, wkv_ref, *refs, feature_major, n_aliased, kept_tail, own_slot):
    q_ref, k_ref, v_ref, kb_ref, vb_ref = refs[n_aliased:]
    y = _rms(x_ref[...], g_ref[...]).astype(BF16)
    _emit_kv(y, wkv_ref, k_ref, v_ref, kb_ref, vb_ref, feature_major, kept_tail, own_slot)
    q_ref[...] = (_dot(y, wq_ref[...]) * (HEAD_DIM ** -0.5 * LOG2E)).astype(BF16)


def _proj_c(x, g, w_q, w_kv, tm, batch=None, stack=None):
    m, d = x.shape
    row = lambda n: pl.BlockSpec((tm, n), lambda i: (i, 0))
    params = (g, w_q, w_kv)
    kv_specs, kv_shapes, kept_tail, own_slot = _kv_out(m, C_WIDTH, tm, batch, stack)
    prev, prev_specs, aliases = _aliased_stack(stack, 1 + len(params), 1)
    return pl.pallas_call(
        functools.partial(_proj_c_kernel, feature_major=batch is not None, n_aliased=len(prev),
                          kept_tail=kept_tail, own_slot=own_slot),
        grid=(m // tm,),
        in_specs=[row(d)] + [_resident(a) for a in params] + prev_specs,
        out_specs=[row(C_WIDTH)] + kv_specs,
        out_shape=[jax.ShapeDtypeStruct((m, C_WIDTH), BF16)] + kv_shapes,
        input_output_aliases=aliases,
        compiler_params=_params("arbitrary" if kept_tail else "parallel"),
        name="proj_c",
    )(x, *[_operand(a) for a in params], *prev)


def _layer_tail_rows(x_ref, mixer_refs, param_refs, o_ref, gla_merge, final_norm):
    if gla_merge:
        oa_ref, ob_ref, r_ref = mixer_refs
        ggla_ref, wout_ref, *param_refs = param_refs
        ob = ob_ref[...]
        parts = []
        for h in range(B_HEADS):
            seg = ob[:, h * B_DV:(h + 1) * B_DV]
            parts.append(seg * lax.rsqrt(jnp.mean(seg * seg, axis=-1, keepdims=True) + EPS))
        r = r_ref[...].astype(F32)
        obn = jnp.concatenate(parts, axis=-1) * ggla_ref[...] * (r * jax.nn.sigmoid(r))
        mix = _dot(oa_ref[...], wout_ref[:A_WIDTH, :]) + _dot(obn.astype(BF16), wout_ref[A_WIDTH:, :])
    else:
        (oc_ref,) = mixer_refs
        wout_ref, *param_refs = param_refs
        mix = _dot(oc_ref[...], wout_ref[...])
    gffn_ref, wg_ref, wu_ref, wd_ref, *param_refs = param_refs
    x = x_ref[...] + mix
    y = _rms(x, gffn_ref[...]).astype(BF16)
    h = _dot(y, wg_ref[...])
    u = _dot(y, wu_ref[...])
    a = (h * jax.nn.sigmoid(h) * u).astype(BF16)
    x = x + _dot(a, wd_ref[...])
    o_ref[...] = _rms(x, param_refs[0][...]) if final_norm else x


def _layer_tail_kernel(*refs, n_mixer, n_params, gla_merge, final_norm, main_steps):
    main, extra = refs[:1 + n_mixer], refs[1 + n_mixer:2 + 2 * n_mixer]
    params = refs[2 + 2 * n_mixer:2 + 2 * n_mixer + n_params]
    o_ref, o_extra_ref = refs[-2:]

    @pl.when(pl.program_id(0) < main_steps)
    def _():
        _layer_tail_rows(main[0], main[1:], params, o_ref, gla_merge, final_norm)

    @pl.when(pl.program_id(0) == main_steps)
    def _():
        _layer_tail_rows(extra[0], extra[1:], params, o_extra_ref, gla_merge, final_norm)


def _layer_tail(x, mixer_out, x_extra, mixer_out_extra, mixer_params, g_ffn, wg, wu, wd, g_fin, tm):
    m, d = x.shape
    main_steps = m // tm
    row = lambda a: pl.BlockSpec((tm, a.shape[1]), lambda i: (jnp.minimum(i, main_steps - 1), 0))
    whole = lambda a: pl.BlockSpec(a.shape, lambda i: (0, 0))
    params = [*mixer_params, g_ffn, wg, wu, wd] + ([] if g_fin is None else [g_fin])
    return pl.pallas_call(
        functools.partial(_layer_tail_kernel, n_mixer=len(mixer_out), n_params=len(params),
                          gla_merge=len(mixer_out) == 3, final_norm=g_fin is not None, main_steps=main_steps),
        grid=(main_steps + 1,),
        in_specs=([row(x)] + [row(a) for a in mixer_out] + [whole(x_extra)]
                  + [whole(a) for a in mixer_out_extra] + [_resident(a) for a in params]),
        out_specs=[row(x), whole(x_extra)],
        out_shape=[jax.ShapeDtypeStruct((m, d), F32), jax.ShapeDtypeStruct(x_extra.shape, F32)],
        compiler_params=_params("arbitrary"),
        name="layer_tail",
    )(x, *mixer_out, x_extra, *mixer_out_extra, *[_operand(a) for a in params])


def _interleave(stage_generators):
    results = [None] * len(stage_generators)
    live = list(range(len(stage_generators)))
    while live:
        for i in list(live):
            try:
                next(stage_generators[i])
            except StopIteration as done:
                results[i] = done.value
                live.remove(i)
    return results


def _sb_core(q_heads, lo, first_kv, first_mask, earlier_kv, n_earlier, acc_ref, c_ref, o_ref, n_pairs,
             companions=(), on_companions=None):
    tk = first_kv.n_keys
    heads = range(2 * n_pairs)
    later = (lax.broadcasted_iota(jnp.int32, (tk, tk), 0)
             > lax.broadcasted_iota(jnp.int32, (tk, tk), 1)).astype(BF16)
    c_ref[...] = jnp.zeros_like(c_ref)

    tq = q_heads[0].shape[0]

    def row_groups(kv, mask):
        if mask is None or tq != tk or tq % (2 * LANES):
            return [(slice(None), kv, mask, later)]
        half = tq // 2
        return [(slice(0, half), kv.first(half), mask[:half, :half], later[:half, :half]),
                (slice(half, tq), kv, mask[half:, :], later)]

    def block(kv, mask):
        groups = row_groups(kv, mask)
        chains = [(h, g) for h in heads for g in range(len(groups))]
        z = [groups[g][1].scores(q_heads[h][groups[g][0], :], h // 2) for h, g in chains]
        yield
        log_beta, drop, after = [], [], []
        for i, (h, g) in enumerate(chains):
            d = jnp.maximum(z[i], 0.0) + jnp.log(1.0 + jnp.exp(-jnp.abs(z[i])))
            log_beta.append(z[i] - d)
            drop.append(d if mask is None else jnp.where(groups[g][2], d, 0.0))
        yield
        for i, (h, g) in enumerate(chains):
            hi, lo_part = _split_bf16(drop[i])
            tri = groups[g][3]
            after.append(_dot(hi, tri) + _dot(lo_part, tri))
        yield
        pv = {}
        for i, (h, g) in enumerate(chains):
            rows, kv_g = groups[g][0], groups[g][1]
            c = c_ref[h, rows, :]
            w = jnp.exp(log_beta[i] - after[i] - c)
            if mask is not None:
                w = jnp.where(groups[g][2], w, 0.0)
            c_ref[h, rows, :] = c + after[i][:, 0:1] + drop[i][:, 0:1]
            pv[h, g] = kv_g.weighted(w.astype(BF16), kv_g.values(h // 2))
        yield
        per_head = [pv[h, 0] if len(groups) == 1 else jnp.concatenate([pv[h, g] for g in range(len(groups))], 0)
                    for h in heads]
        out = [jnp.where(lo, per_head[2 * p], per_head[2 * p + 1]) for p in range(n_pairs)]
        return out[0] if n_pairs == 1 else jnp.concatenate(out, axis=-1)

    def all_dead():
        return jnp.min(c_ref[...]) > SB_DEAD

    first, *companion_results = _interleave([block(first_kv, first_mask), *companions])
    acc_ref[...] = first
    if companions:
        on_companions(companion_results)

    def cond(carry):
        n, dead = carry
        return (n < n_earlier) & jnp.logical_not(dead)

    def body(carry):
        n, _ = carry
        acc_ref[...] += _interleave([block(earlier_kv(n), None)])[0]
        return n + 1, all_dead()

    lax.while_loop(cond, body, (jnp.int32(0), jnp.asarray(False)))
    o_ref[...] = acc_ref[...].astype(o_ref.dtype)


def _mixer_ab_prompt_kernel(q_ref, k_ref, v_ref, qb_ref, kb_ref, vb_ref, la_ref, *refs,
                            tb, n_pairs, blocks_per_step, n_riders):
    o_ref, ob_ref, s_out_ref = refs[n_riders:n_riders + 3]
    acc_ref, c_ref, carry_ref = refs[3 + 2 * n_riders:]
    _cast_riders(list(zip(refs[:n_riders], refs[n_riders + 3:3 + 2 * n_riders])))
    pairs = range(B_HEADS // 2)
    strictly_earlier = (lax.broadcasted_iota(jnp.int32, (tb, tb), 1)
                        < lax.broadcasted_iota(jnp.int32, (tb, tb), 0))

    def kv_block(j):
        keys = pl.ds(pl.multiple_of(j * tb, tb), tb)
        return _KV(k_ref[:, keys], v_ref[:, keys], True)

    def keep_states(states):
        for p in pairs:
            carry_ref[2 * p], carry_ref[2 * p + 1] = states[p]
            s_out_ref[2 * p] = states[p][0][:B_DK, :]
            s_out_ref[2 * p + 1] = states[p][1][B_DK:, :]

    @pl.when(pl.program_id(1) == 0)
    def _():
        carry_ref[...] = jnp.zeros_like(carry_ref)

    def one_block(s, carry):
        qi = pl.program_id(1) * blocks_per_step + s
        rows = pl.ds(pl.multiple_of(s * tb, tb), tb)
        gla = [_gla_pair_tile(qb_ref[rows, _pair_cols(p)], kb_ref[rows, _pair_cols(p)],
                              vb_ref[rows, _pair_v_cols(p)], la_ref[rows, _pair_cols(p)],
                              [carry_ref[2 * p], carry_ref[2 * p + 1]],
                              _gla_rows_emit(ob_ref.at[rows, :], p)) for p in pairs]
        q_heads, lo = _split_heads(q_ref.at[rows, :], n_pairs, tb)
        _sb_core(q_heads, lo, kv_block(qi), strictly_earlier, lambda n: kv_block(qi - 1 - n), qi,
                 acc_ref, c_ref, o_ref.at[rows, :], n_pairs, companions=gla, on_companions=keep_states)
        return carry

    lax.fori_loop(0, blocks_per_step, one_block, 0)


def _sb_sample_kernel(q_ref, kn_ref, vn_ref, kc_ref, vc_ref, o_ref, acc_ref, c_ref, kpad_ref, vpad_ref,
                      *, ts, tk, n_cache_blocks, n_pairs):
    q_heads, lo = _split_heads(q_ref, n_pairs, ts)
    kpad_ref[...] = jnp.zeros_like(kpad_ref)
    vpad_ref[...] = jnp.zeros_like(vpad_ref)
    kpad_ref[:ts, :] = kn_ref[...]
    vpad_ref[:ts, :] = vn_ref[...]

    def cache_block(n):
        keys = pl.ds(pl.multiple_of((n_cache_blocks - 1 - n) * tk, tk), tk)
        return _KV(kc_ref[:, keys].astype(BF16), vc_ref[:, keys].astype(BF16), True)

    strictly_earlier = (lax.broadcasted_iota(jnp.int32, (ts, tk), 1)
                        < lax.broadcasted_iota(jnp.int32, (ts, tk), 0))
    _sb_core(q_heads, lo, _KV(kpad_ref[...], vpad_ref[...], False), strictly_earlier, cache_block,
             n_cache_blocks, acc_ref, c_ref, o_ref, n_pairs)


def _mixer_ab_prompt(q, k, v, qb, kb, vb, la, ride_stacks, layer):
    b, t, _ = q.shape
    tb = ATT_BLOCK
    assert t % tb == 0 and tb % GLA_CHUNK == 0 and tb <= GLA_MAX_TILE
    n_pairs = A_WIDTH // LANES
    per_step = math.gcd(t // tb, MIXER_BLOCKS_PER_STEP)
    rows = lambda width: pl.BlockSpec((None, per_step * tb, width), lambda bi, qi: (bi, qi, 0))
    kv_spec = pl.BlockSpec((None, A_WIDTH, t), lambda bi, qi: (bi, 0, 0))
    state = (B_HEADS, B_DK, B_DV)
    grid = (b, t // (per_step * tb))
    ride_in, ride_out, ride_shapes = _Riders(ride_stacks, layer, grid).specs()
    return pl.pallas_call(
        functools.partial(_mixer_ab_prompt_kernel, tb=tb, n_pairs=n_pairs, blocks_per_step=per_step,
                          n_riders=len(ride_stacks)),
        grid=grid,
        in_specs=[rows(A_WIDTH), kv_spec, kv_spec, rows(B_KW), rows(B_KW), rows(B_VW), rows(B_KW)] + ride_in,
        out_specs=[rows(A_WIDTH), rows(B_VW),
                   pl.BlockSpec((None, *state), lambda bi, qi: (bi, 0, 0, 0))] + ride_out,
        out_shape=[jax.ShapeDtypeStruct(q.shape, BF16), jax.ShapeDtypeStruct(vb.shape, F32),
                   jax.ShapeDtypeStruct((b, *state), F32)] + ride_shapes,
        scratch_shapes=[pltpu.VMEM((tb, A_WIDTH), F32), pltpu.VMEM((2 * n_pairs, tb, 1), F32),
                        pltpu.VMEM((B_HEADS, LANES, B_DV), F32)],
        compiler_params=_params("arbitrary", "arbitrary"),
        name="mixer_ab",
    )(q, k, v, qb, kb, vb, la, *ride_stacks)


def _sb_attention_sample(q, k_new, v_new, k_cache, v_cache, layer):
    b, ts, _ = q.shape
    past = k_cache.shape[3]
    tk = ATT_BLOCK
    assert past % tk == 0 and ts <= tk
    n_pairs = SB_PAIRS_PER_STEP
    w = n_pairs * LANES
    new_spec = pl.BlockSpec((None, ts, w), lambda bi, hp: (bi, 0, hp))
    cache_spec = pl.BlockSpec((None, None, w, past), lambda bi, hp: (layer, bi, hp, 0))
    return pl.pallas_call(
        functools.partial(_sb_sample_kernel, ts=ts, tk=tk, n_cache_blocks=past // tk, n_pairs=n_pairs),
        grid=(b, A_WIDTH // w),
        in_specs=[new_spec, new_spec, new_spec, cache_spec, cache_spec],
        out_specs=new_spec,
        out_shape=jax.ShapeDtypeStruct(q.shape, BF16),
        scratch_shapes=[pltpu.VMEM((ts, w), F32), pltpu.VMEM((2 * n_pairs, ts, 1), F32),
                        pltpu.VMEM((tk, w), BF16), pltpu.VMEM((tk, w), BF16)],
        compiler_params=_params("parallel", "parallel"),
        name="sb_attention_sample",
    )(q, k_new, v_new, k_cache, v_cache)


def _band_core(q_heads, lo, kv, pens, bias, o_ref, n_pairs, skip_hidden_quarters=False):
    heads = range(2 * n_pairs)
    tq = q_heads[0].shape[0]
    everything = [(slice(None), [(kv[i], i, slice(None)) for i in range(len(kv))])]
    groups = _band_quarters(kv, tq) if skip_hidden_quarters else everything
    chains = [(h, g) for h in heads for g in range(len(groups))]
    z = [[piece.scores(q_heads[h][groups[g][0], :], h // 2) + bias(h, i)[groups[g][0], cols]
          for piece, i, cols in groups[g][1]] for h, g in chains]
    acc = {}
    for c, (h, g) in enumerate(chains):
        pieces = groups[g][1]
        m = None
        for j, (_, i, _) in enumerate(pieces):
            mi = jnp.max(z[c][j], axis=-1, keepdims=True)
            if pens[i] is not None:
                mi = mi + pens[i]
            m = mi if m is None else jnp.maximum(m, mi)
        a = None
        for j, (piece, i, _) in enumerate(pieces):
            shift = m if pens[i] is None else m - pens[i]
            p = jnp.exp2(z[c][j] - shift).astype(BF16)
            v = piece.values(h // 2)
            ones = jnp.ones_like(v)
            first = piece.head_lanes()
            v = jnp.where(first, v, ones) if h % 2 == 0 else jnp.where(first, ones, v)
            pv = piece.weighted(p, v)
            a = pv if a is None else a + pv
        acc[h, g] = a
    per_head = [acc[h, 0] if len(groups) == 1 else jnp.concatenate([acc[h, g] for g in range(len(groups))], 0)
                for h in heads]
    for p in range(n_pairs):
        a0, a1 = per_head[2 * p], per_head[2 * p + 1]
        o_ref[:, _pair_cols(p)] = jnp.where(lo, a0 / pltpu.roll(a0, HEAD_DIM, axis=1),
                                            a1 / pltpu.roll(a1, HEAD_DIM, axis=1)).astype(o_ref.dtype)


def _band_quarters(kv, tq):
    own, back1, back2 = kv
    half = tq // 2
    assert own.n_keys == tq and half % CHUNK == 0 and 2 * (tq // CHUNK) >= C_LEFT_CHUNKS
    whole = slice(None)
    return [(slice(0, half), [(own.first(half), 0, slice(0, half)), (back1, 1, whole), (back2, 2, whole)]),
            (slice(half, tq), [(own, 0, whole), (back1, 1, whole), (back2.last(half), 2, slice(half, tq))])]


def _band_prompt_kernel(q_ref, k_ref, v_ref, bias_ref, *refs, tq, tk, n_pairs, blocks_per_step):
    n_riders = (len(refs) - 1) // 2
    o_ref = refs[n_riders]
    _cast_riders(list(zip(refs[:n_riders], refs[n_riders + 1:])))

    def one_block(s, carry):
        qi = pl.program_id(2) * blocks_per_step + s
        rows = pl.ds(pl.multiple_of(s * tq, tq), tq)
        q_heads, lo = _split_heads(q_ref.at[rows, :], n_pairs, tq)
        kv, pens = [], []
        for dj in range(3):
            j = qi - dj
            pens.append(None if dj == 0 else jnp.where(j >= 0, 0.0, NEG_BIG).astype(F32))
            keys = pl.ds(pl.multiple_of(jnp.maximum(j, 0) * tk, tk), tk)
            kv.append(_KV(k_ref[:, keys], v_ref[:, keys], True))
        _band_core(q_heads, lo, kv, pens, lambda h, i: bias_ref.at[h, i], o_ref.at[rows, :], n_pairs,
                   skip_hidden_quarters=True)
        return carry

    lax.fori_loop(0, blocks_per_step, one_block, 0)


def _band_sample_kernel(q_ref, kn_ref, vn_ref, kc_ref, vc_ref, bias_ref, o_ref, kpad_ref, vpad_ref,
                        *, ts, tk, n_cache_blocks, n_pairs):
    q_heads, lo = _split_heads(q_ref, n_pairs, ts)
    kpad_ref[...] = jnp.zeros_like(kpad_ref)
    vpad_ref[...] = jnp.zeros_like(vpad_ref)
    kpad_ref[:ts, :] = kn_ref[...]
    vpad_ref[:ts, :] = vn_ref[...]
    kv = [_KV(kpad_ref[...], vpad_ref[...], False)]
    for dj in range(1, n_cache_blocks + 1):
        keys = slice((n_cache_blocks - dj) * tk, (n_cache_blocks - dj + 1) * tk)
        kv.append(_KV(kc_ref[:, keys].astype(BF16), vc_ref[:, keys].astype(BF16), True))
    is_new_key = lax.broadcasted_iota(jnp.int32, (ts, tk), 1) < ts

    def bias(h, i):
        return jnp.where(is_new_key, bias_ref[h, 0], NEG_BIG) if i == 0 else bias_ref[h, i]

    _band_core(q_heads, lo, kv, [None] * len(kv), bias, o_ref, n_pairs)


def _band_attention_prompt(q, k, v, bias, ride_stacks, layer):
    b, t, _ = q.shape
    tq = tk = ATT_BLOCK
    assert t % tk == 0
    n_pairs = BAND_PAIRS_PER_STEP
    w = n_pairs * LANES
    per_step = math.gcd(t // tq, BAND_BLOCKS_PER_STEP)
    kern = functools.partial(_band_prompt_kernel, tq=tq, tk=tk, n_pairs=n_pairs, blocks_per_step=per_step)
    kv_spec = pl.BlockSpec((None, w, t), lambda bi, hp, qi: (bi, hp, 0))
    q_spec = pl.BlockSpec((None, per_step * tq, w), lambda bi, hp, qi: (bi, qi, hp))
    bias_spec = pl.BlockSpec((2 * n_pairs, 3, tq, tk), lambda bi, hp, qi: (hp, 0, 0, 0))
    grid = (b, C_WIDTH // w, t // (per_step * tq))
    ride_in, ride_out, ride_shapes = _Riders(ride_stacks, layer, grid).specs()
    return pl.pallas_call(
        kern,
        grid=grid,
        in_specs=[q_spec, kv_spec, kv_spec, bias_spec] + ride_in,
        out_specs=[q_spec] + ride_out,
        out_shape=[jax.ShapeDtypeStruct(q.shape, BF16)] + ride_shapes,
        compiler_params=_params("arbitrary", "arbitrary", "arbitrary"),
        name="band_attention",
    )(q, k, v, bias, *ride_stacks)


def _band_attention_sample(q, k_new, v_new, k_cache, v_cache, layer, bias):
    b, ts, _ = q.shape
    wc = k_cache.shape[3]
    tk = ATT_BLOCK
    n_cache_blocks = min(wc // tk, 2)
    assert wc % (n_cache_blocks * tk) == 0 and ts <= tk
    n_pairs = C_WIDTH // LANES
    w = n_pairs * LANES
    kern = functools.partial(_band_sample_kernel, ts=ts, tk=tk, n_cache_blocks=n_cache_blocks,
                             n_pairs=n_pairs)
    new_spec = pl.BlockSpec((None, ts, w), lambda bi, hp: (bi, 0, hp))
    cache_rows = n_cache_blocks * tk
    cache_spec = pl.BlockSpec((None, None, w, cache_rows),
                              lambda bi, hp: (layer, bi, hp, wc // cache_rows - 1))
    bias_spec = pl.BlockSpec((2 * n_pairs, 1 + n_cache_blocks, ts, tk), lambda bi, hp: (hp, 0, 0, 0))
    return pl.pallas_call(
        kern,
        grid=(b, C_WIDTH // w),
        in_specs=[new_spec, new_spec, new_spec, cache_spec, cache_spec, bias_spec],
        out_specs=new_spec,
        out_shape=jax.ShapeDtypeStruct(q.shape, BF16),
        scratch_shapes=[pltpu.VMEM((tk, w), BF16), pltpu.VMEM((tk, w), BF16)],
        compiler_params=_params("parallel", "parallel"),
        name="band_attention_sample",
    )(q, k_new, v_new, k_cache, v_cache, bias)


def _band_bias_kernel(g_ref, o_ref):
    rows = tk = ATT_BLOCK
    q_chunk = lax.broadcasted_iota(jnp.int32, (rows, tk), 0) // CHUNK
    k_chunk = lax.broadcasted_iota(jnp.int32, (rows, tk), 1) // CHUNK
    for dj in range(3):
        diff = dj * (tk // CHUNK) + q_chunk - k_chunk
        seen = (diff >= 0) & (diff <= C_LEFT_CHUNKS)
        for h in range(g_ref.shape[0]):
            g = jnp.broadcast_to(g_ref[h, dj], (rows, 2 * tk))
            tile = pltpu.roll(g, 0, axis=1, stride=1, stride_axis=0)[:, :tk]
            o_ref[h, dj] = jnp.where(seen, tile * LOG2E, NEG_BIG)


def _band_bias(rel_table):
    rows = tk = ATT_BLOCK
    c = jnp.arange(2 * tk, dtype=jnp.int32)
    u = jnp.where(c <= tk, -c, 2 * tk - c)
    idx = jnp.clip(jnp.arange(3, dtype=jnp.int32)[:, None] * tk + u[None, :], REL_MIN, REL_MAX) - REL_MIN
    g = rel_table[:, idx].astype(F32).reshape(C_HEADS, 3, 1, 2 * tk)
    return pl.pallas_call(
        _band_bias_kernel,
        grid=(C_HEADS // BIAS_HEADS_PER_STEP,),
        in_specs=[pl.BlockSpec((BIAS_HEADS_PER_STEP, 3, 1, 2 * tk), lambda h: (h, 0, 0, 0))],
        out_specs=pl.BlockSpec((BIAS_HEADS_PER_STEP, 3, rows, tk), lambda h: (h, 0, 0, 0)),
        out_shape=jax.ShapeDtypeStruct((C_HEADS, 3, rows, tk), F32),
        compiler_params=_params("parallel"),
        name="band_bias",
    )(g)


def _gla_pair_tile(q, k, v, la, states, emit):
    L = GLA_CHUNK
    chunks = range(q.shape[0] // L)
    row = lax.broadcasted_iota(jnp.int32, (L, L), 0)
    colm = lax.broadcasted_iota(jnp.int32, (L, L), 1)
    tri = (colm <= row).astype(BF16)
    causal = colm <= row
    lane = lax.broadcasted_iota(jnp.int32, (L, LANES), 1)
    sub = lax.broadcasted_iota(jnp.int32, (LANES, B_DV), 0)
    mine = [(lane >= h * B_DK) & (lane < (h + 1) * B_DK) for h in range(2)]
    mine_rows = [(sub >= h * B_DK) & (sub < (h + 1) * B_DK) for h in range(2)]
    rows = [slice(c * L, (c + 1) * L) for c in chunks]

    b = []
    for r in rows:
        g_hi, g_lo = _split_bf16(la[r, :])
        b.append(_dot(tri, g_hi) + _dot(tri, g_lo))
    yield
    qg, qg_h, kg, kd_t, decay = [], [], [], [], []
    for c, r in zip(chunks, rows):
        q_c, k_c = q[r, :].astype(F32), k[r, :].astype(F32)
        qg_c = q_c * (B_DK ** -0.5) * jnp.exp(b[c])
        qg.append(qg_c.astype(BF16))
        qg_h.append([jnp.where(mine[h], qg_c, 0.0).astype(BF16) for h in range(2)])
        kg.append((k_c * jnp.exp(-b[c])).astype(BF16))
        b_t = b[c].T
        b_last = b_t[:, L - 1:L]
        kd_t.append((k_c.T * jnp.exp(b_last - b_t)).astype(BF16))
        decay.append(jnp.exp(b_last))
    yield
    att =[[jnp.where(causal, _dot_nt(qg_h[c][h], kg[c]), 0.0).astype(BF16) for h in range(2)]
           for c in chunks]
    yield
    o_intra, own = [], []
    for c, r in zip(chunks, rows):
        v_h = [v[r, h * B_DV:(h + 1) * B_DV].astype(BF16) for h in range(2)]
        o_intra.append([_dot(att[c][h], v_h[h]) for h in range(2)])
        own.append([jnp.where(mine_rows[h], _dot(kd_t[c], v_h[h]), 0.0) for h in range(2)])

    yield
    states = list(states)
    start = []
    for c in chunks:
        start.append([s.astype(BF16) for s in states])
        states = [decay[c] * states[h] + own[c][h] for h in range(2)]
    yield
    for c in chunks:
        for h in range(2):
            emit(c, h, o_intra[c][h] + _dot(qg[c], start[c][h]))
    return states


def _pair_states(s0_ref, p):
    zeros_state = jnp.zeros((B_DK, B_DV), F32)
    return [jnp.concatenate([s0_ref[2 * p], zeros_state], axis=0),
            jnp.concatenate([zeros_state, s0_ref[2 * p + 1]], axis=0)]


def _pair_v_cols(p):
    return slice(2 * p * B_DV, 2 * (p + 1) * B_DV)


def _gla_rows_emit(o_ref, p):
    def emit(c, h, o):
        o_ref[c * GLA_CHUNK:(c + 1) * GLA_CHUNK, (2 * p + h) * B_DV:(2 * p + h + 1) * B_DV] = o
    return emit


def _gla_kernel(q_ref, k_ref, v_ref, la_ref, s0_ref, o_ref, s_out_ref):
    pairs = range(B_HEADS // 2)
    states = _interleave([
        _gla_pair_tile(q_ref[:, _pair_cols(p)], k_ref[:, _pair_cols(p)], v_ref[:, _pair_v_cols(p)],
                       la_ref[:, _pair_cols(p)], _pair_states(s0_ref, p), _gla_rows_emit(o_ref, p))
        for p in pairs])
    for p in pairs:
        s_out_ref[2 * p] = states[p][0][:B_DK, :]
        s_out_ref[2 * p + 1] = states[p][1][B_DK:, :]


def _gla(q, k, v, la, s0):
    b, t, _ = q.shape
    assert t % GLA_CHUNK == 0 and t <= GLA_MAX_TILE
    qk_spec = pl.BlockSpec((None, t, B_KW), lambda bi: (bi, 0, 0))
    v_spec = pl.BlockSpec((None, t, B_VW), lambda bi: (bi, 0, 0))
    s_spec = pl.BlockSpec((None, B_HEADS, B_DK, B_DV), lambda bi: (bi, 0, 0, 0))
    return pl.pallas_call(
        _gla_kernel,
        grid=(b,),
        in_specs=[qk_spec, qk_spec, v_spec, qk_spec, s_spec],
        out_specs=[v_spec, s_spec],
        out_shape=[jax.ShapeDtypeStruct(v.shape, F32), jax.ShapeDtypeStruct(s0.shape, F32)],
        compiler_params=_params("parallel"),
        name="gla",
    )(q, k, v, la, s0)


def _pad_rows(x, n):
    return jnp.pad(x, ((0, 0), (0, n - x.shape[1]), (0, 0)))


def _heads_last(x, heads):
    n, b, _, s = x.shape
    return jnp.transpose(x.reshape(n, b, heads, HEAD_DIM, s), (0, 1, 4, 2, 3))


def _feature_major(cache):
    n, b, s, heads, hd = cache.shape
    return jnp.transpose(cache, (0, 1, 3, 4, 2)).reshape(n, b, heads * hd, s)


def _row_tile(m):
    for tm in (512, 256, 128, 64, 32, 16, 8):
        if m % tm == 0:
            return tm
    raise ValueError(f"token count {m} is not a multiple of 8")


def kernel(x_prompt, x_sample, cache_a_k, cache_a_v, state_b, cache_c_k, cache_c_v, norm_mix_g, norm_ffn_g, w_in_ab, w_gate_b, b_gate_b, norm_gla_g, w_out_ab, w_qkv_c, rel_bias_c, w_out_c, w_ffn_gate, w_ffn_up, w_ffn_down, norm_final_g):
    bp, tp, d = x_prompt.shape
    bs, ts, _ = x_sample.shape
    depth = norm_mix_g.shape[0]
    past = cache_a_k.shape[2]
    wc = cache_c_k.shape[2]
    assert tp % ATT_BLOCK == 0 and past % ATT_BLOCK == 0 and wc % ATT_BLOCK == 0
    assert ts <= GLA_CHUNK and ts % 8 == 0
    mp, ms = bp * tp, bs * ts
    tmp, tms = _row_tile(tp), _row_tile(ms)
    xp = x_prompt.reshape(mp, d)
    xs = x_sample.reshape(ms, d)
    row2 = lambda a: a.reshape(1, -1)

    a_ks, a_vs, b_sp, b_ss, c_ks, c_vs = [], [], [], [], [], []
    a_kv_prompt = c_kv_prompt = None
    n_ab, n_c = (depth + 1) // 2, depth // 2
    keep = min(C_LEFT_CHUNKS * CHUNK, tp)

    kv0, kv1 = A_WIDTH, 3 * A_WIDTH
    o = 3 * A_WIDTH + 2 * B_KW + B_VW
    w_in_t = jnp.swapaxes(w_in_ab, 1, 2)
    w_main_all = jnp.concatenate([w_in_t[:, :kv0], w_in_t[:, kv1:o], w_in_t[:, o + B_GATE_RANK:]],
                                 axis=1).astype(BF16)
    w_kv_ab_t_all = w_in_t[:, kv0:kv1].astype(BF16)
    w_lr_all = jnp.pad(w_in_t[:, o:o + B_GATE_RANK],
                       ((0, 0), (0, LANES - B_GATE_RANK), (0, 0))).astype(BF16)
    w_gate_all = jnp.pad(w_gate_b, ((0, 0), (0, LANES - B_GATE_RANK), (0, 0))).astype(BF16)
    w_out_ab_all = w_out_ab.astype(BF16)
    w_q_all = w_qkv_c[:, :, :C_WIDTH].astype(BF16)
    w_kv_c_t_all = jnp.swapaxes(w_qkv_c[:, :, C_WIDTH:], 1, 2).astype(BF16)
    w_out_c_all = w_out_c.astype(BF16)
    cache_a_k_fm, cache_a_v_fm, cache_c_k_fm, cache_c_v_fm = (
        _feature_major(c) for c in (cache_a_k, cache_a_v, cache_c_k, cache_c_v))

    for layer in range(depth):
        i = layer // 2
        g_mix = row2(norm_mix_g[layer])
        ffn_f32 = (w_ffn_gate, w_ffn_up, w_ffn_down)
        ffn_ends = (row2(norm_ffn_g[layer]), row2(norm_final_g) if layer == depth - 1 else None)
        if layer % 2 == 0:
            w_main, w_kv_t, w_lr, w_gate, w_out = (
                _Slab(w, i) for w in (w_main_all, w_kv_ab_t_all, w_lr_all, w_gate_all, w_out_ab_all))
            b_gate = row2(b_gate_b[i])
            g_gla = row2(norm_gla_g[i])

            qa, ka, va, kab, vab, qb, kb, vb, r, la = _proj_ab(
                xp, g_mix, w_main, w_kv_t, w_lr, w_gate, b_gate, tmp, batch=bp,
                stack=_KVStack(i, n_ab, tp, a_kv_prompt))
            a_kv_prompt = (ka, va)
            sh = lambda a: a.reshape(bp, tp, -1)
            oa, ob, sbp, *ffn_w = _mixer_ab_prompt(sh(qa), kab, vab, sh(qb), sh(kb), sh(vb), sh(la),
                                                   ffn_f32, layer)
            ffn = (ffn_ends[0], *ffn_w, ffn_ends[1])
            mixed_p = (oa.reshape(mp, -1), ob.reshape(mp, -1), r)
            b_sp.append(sbp)

            qa, ka, va, kab, vab, qb, kb, vb, r, la = _proj_ab(
                xs, g_mix, w_main, w_kv_t, w_lr, w_gate, b_gate, tms)
            sh = lambda a: a.reshape(bs, ts, -1)
            oa = _sb_attention_sample(sh(qa), sh(kab), sh(vab), cache_a_k_fm, cache_a_v_fm, i)
            pad_t = lambda a: _pad_rows(sh(a), GLA_CHUNK)
            ob, sbs = _gla(pad_t(qb), pad_t(kb), pad_t(vb), pad_t(la), state_b[i])
            mixed_s = (oa.reshape(ms, -1), ob[:, :ts].reshape(ms, -1), r)
            xp, xs = _layer_tail(xp, mixed_p, xs, mixed_s, (g_gla, w_out), *ffn, tmp)
            a_ks.append(ka.reshape(bs, ts, A_HEADS, HEAD_DIM))
            a_vs.append(va.reshape(bs, ts, A_HEADS, HEAD_DIM))
            b_ss.append(sbs)
        else:
            w_q, w_kv_t, w_out = (_Slab(w, i) for w in (w_q_all, w_kv_c_t_all, w_out_c_all))

            q, k, v, kb16, vb16 = _proj_c(xp, g_mix, w_q, w_kv_t, tmp, batch=bp,
                                          stack=_KVStack(i, n_c, keep, c_kv_prompt))
            c_kv_prompt = (k, v)
            bias = _band_bias(rel_bias_c[i])
            oc, *ffn_w = _band_attention_prompt(q.reshape(bp, tp, -1), kb16, vb16, bias, ffn_f32, layer)
            ffn = (ffn_ends[0], *ffn_w, ffn_ends[1])
            mixed_p = (oc.reshape(mp, -1),)

            q, k, v, kb16, vb16 = _proj_c(xs, g_mix, w_q, w_kv_t, tms)
            sh = lambda a: a.reshape(bs, ts, -1)
            oc = _band_attention_sample(sh(q), sh(kb16), sh(vb16), cache_c_k_fm, cache_c_v_fm, i, bias)
            xp, xs = _layer_tail(xp, mixed_p, xs, (oc.reshape(ms, -1),), (w_out,), *ffn, tmp)
            c_ks.append(k.reshape(bs, ts, C_HEADS, HEAD_DIM))
            c_vs.append(v.reshape(bs, ts, C_HEADS, HEAD_DIM))

    y_prompt = xp.reshape(bp, tp, d)
    y_sample = xs.reshape(bs, ts, d)
    a_kp, a_vp = (_heads_last(a, A_HEADS) for a in a_kv_prompt)
    c_kp, c_vp = (_heads_last(a, C_HEADS) for a in c_kv_prompt)
    return (y_prompt, y_sample, a_kp, a_vp, jnp.stack(a_ks), jnp.stack(a_vs),
            jnp.stack(b_sp), jnp.stack(b_ss), c_kp, c_vp, jnp.stack(c_ks), jnp.stack(c_vs))
```

```python
import functools
import math

import jax
import jax.numpy as jnp
from jax import lax
from jax.experimental import pallas as pl
from jax.experimental.pallas import tpu as pltpu

F32 = jnp.float32
BF16 = jnp.bfloat16

EPS = 1e-6
HEAD_DIM = 64
LANES = 128
A_HEADS = 8
A_WIDTH = A_HEADS * HEAD_DIM
B_HEADS = 4
B_DK = 64
B_DV = 128
B_KW = B_HEADS * B_DK
B_VW = B_HEADS * B_DV
B_GATE_RANK = 16
B_GATE_TEMP = 16.0
GLA_CHUNK = 64
C_HEADS = 16
C_WIDTH = C_HEADS * HEAD_DIM
CHUNK = 64
C_LEFT_CHUNKS = 8
REL_MIN = -(CHUNK - 1)
REL_MAX = 128
ATT_BLOCK = 256
NEG_BIG = -1e30
LOG2E = 1.4426950408889634
SB_DEAD = 104.0
SB_PAIRS_PER_STEP = 4
BAND_PAIRS_PER_STEP = 4
BIAS_HEADS_PER_STEP = 4
MIXER_BLOCKS_PER_STEP = 4
BAND_BLOCKS_PER_STEP = 8
GLA_MAX_TILE = 512
VMEM_LIMIT = 56 * 1024 * 1024


def _params(*sem):
    return pltpu.CompilerParams(dimension_semantics=sem, vmem_limit_bytes=VMEM_LIMIT)


class _Slab:
    def __init__(self, stacked, index):
        self.stacked, self.index, self.shape = stacked, index, stacked.shape[1:]


def _operand(a):
    return a.stacked if isinstance(a, _Slab) else a


def _resident(a):
    if isinstance(a, _Slab):
        index = (a.index,) + (0,) * len(a.shape)
        return pl.BlockSpec((None, *a.shape), lambda *_: index, pipeline_mode=pl.Buffered(1))
    return pl.BlockSpec(a.shape, lambda *_: (0,) * a.ndim, pipeline_mode=pl.Buffered(1))


def _rms(x, g):
    return x * lax.rsqrt(jnp.mean(x * x, axis=-1, keepdims=True) + EPS) * g


def _log_sigmoid(z):
    return jnp.minimum(z, 0.0) - jnp.log1p(jnp.exp(-jnp.abs(z)))


def _split_bf16(x):
    hi = x.astype(BF16)
    lo = (x - hi.astype(F32)).astype(BF16)
    return hi, lo


def _dot(a, b):
    return jnp.dot(a, b, preferred_element_type=F32)


def _dot_nt(a, b):
    return lax.dot_general(a, b, (((1,), (1,)), ((), ())), preferred_element_type=F32)


def _pair_cols(p):
    return slice(p * LANES, (p + 1) * LANES)


class _KV:
    def __init__(self, k, v, feature_major):
        self.k, self.v, self.feature_major = k, v, feature_major
        self.n_keys = k.shape[1] if feature_major else k.shape[0]

    def scores(self, q_h, p):
        if self.feature_major:
            return _dot(q_h, self.k[_pair_cols(p), :])
        return _dot_nt(q_h, self.k[:, _pair_cols(p)])

    def first(self, n):
        if self.feature_major:
            return _KV(self.k[:, :n], self.v[:, :n], True)
        return _KV(self.k[:n], self.v[:n], False)

    def last(self, n):
        if self.feature_major:
            return _KV(self.k[:, -n:], self.v[:, -n:], True)
        return _KV(self.k[-n:], self.v[-n:], False)

    def values(self, p):
        return self.v[_pair_cols(p), :] if self.feature_major else self.v[:, _pair_cols(p)]

    def weighted(self, w, v_p):
        return _dot_nt(w, v_p) if self.feature_major else _dot(w, v_p)

    def head_lanes(self):
        shape = (LANES, self.n_keys) if self.feature_major else (self.n_keys, LANES)
        return lax.broadcasted_iota(jnp.int32, shape, 0 if self.feature_major else 1) < HEAD_DIM


def _split_heads(q_ref, n_pairs, tq):
    lo = lax.broadcasted_iota(jnp.int32, (tq, LANES), 1) < HEAD_DIM
    heads = []
    for p in range(n_pairs):
        q = q_ref[:, _pair_cols(p)]
        heads += [jnp.where(lo, q, jnp.zeros_like(q)), jnp.where(lo, jnp.zeros_like(q), q)]
    return heads, lo


def _emit_kv(y, wkv_ref, k_ref, v_ref, kb_ref, vb_ref, feature_major, kept_tail=None, own_slot=0):
    if not feature_major:
        kv = _dot_nt(y, wkv_ref[...])
        width = kv.shape[1] // 2
        k, v = kv[:, :width], kv[:, width:]
        k_ref[...], v_ref[...], kb_ref[...], vb_ref[...] = k, v, k.astype(BF16), v.astype(BF16)
        return
    kv = _dot_nt(wkv_ref[...], y)
    width = kv.shape[0] // 2
    k, v = kv[:width, :], kv[width:, :]
    kb_ref[...] = k.astype(BF16)
    vb_ref[...] = v.astype(BF16)
    slot = own_slot
    other_slots = [s for s in range(k_ref.shape[0]) if s != slot]
    if kept_tail is None:
        fill, fill_cols = slice(None), k_ref.shape[2]
        k_ref[slot] = k
        v_ref[slot] = v
    else:
        n_tiles, n_kept = kept_tail
        tile = pl.program_id(0) % n_tiles
        tm = k.shape[1]
        cols = pl.ds(pl.multiple_of(jnp.maximum(tile - (n_tiles - n_kept), 0) * tm, tm), tm)
        k_ref[slot, :, cols] = k
        v_ref[slot, :, cols] = v
        share = k_ref.shape[2] // n_tiles
        if share % LANES == 0 and share * n_tiles == k_ref.shape[2]:
            fill, fill_cols = pl.ds(pl.multiple_of(tile * share, share), share), share
        else:
            fill, fill_cols = slice(None), k_ref.shape[2]
    for dst in (k_ref, v_ref):
        for s in other_slots:
            dst[s, :, fill] = jnp.zeros((dst.shape[1], fill_cols), dst.dtype)


class _Riders:
    def __init__(self, stacks, layer, grid):
        self.stacks, self.layer, self.grid = list(stacks), layer, tuple(grid)
        self.chunks = next(c for c in (16, 8, 4, 2, 1)
                           if c <= math.prod(grid) and all(w.shape[1] % (16 * c) == 0 for w in self.stacks))

    def _chunk(self, *idx):
        step = 0
        for i, n in zip(idx, self.grid):
            step = step * n + i
        return jnp.minimum(step, self.chunks - 1)

    def specs(self):
        layer = self.layer
        ins = [pl.BlockSpec((None, w.shape[1] // self.chunks, w.shape[2]),
                            lambda *idx: (layer, self._chunk(*idx), 0)) for w in self.stacks]
        outs = [pl.BlockSpec((w.shape[1] // self.chunks, w.shape[2]),
                             lambda *idx: (self._chunk(*idx), 0)) for w in self.stacks]
        shapes = [jax.ShapeDtypeStruct(w.shape[1:], BF16) for w in self.stacks]
        return ins, outs, shapes


def _cast_riders(pairs):
    for src, dst in pairs:
        dst[...] = src[...].astype(BF16)


def _proj_ab_kernel(x_ref, g_ref, w_ref, wkv_ref, wlr_ref, wgate_ref, bgate_ref, *refs,
                    feature_major, n_aliased, own_slot):
    qa_ref, ka_ref, va_ref, kab_ref, vab_ref, qb_ref, kb_ref, vb_ref, r_ref, la_ref = refs[n_aliased:]
    y = _rms(x_ref[...], g_ref[...]).astype(BF16)
    _emit_kv(y, wkv_ref, ka_ref, va_ref, kab_ref, vab_ref, feature_major, own_slot=own_slot)
    z = _dot_nt(y, w_ref[...])
    c = 0
    qa_ref[...] = (z[:, c:c + A_WIDTH] * (HEAD_DIM ** -0.5)).astype(BF16); c += A_WIDTH
    qb_ref[...] = z[:, c:c + B_KW].astype(BF16); c += B_KW
    kb_ref[...] = z[:, c:c + B_KW].astype(BF16); c += B_KW
    vb_ref[...] = z[:, c:c + B_VW].astype(BF16); c += B_VW
    r_ref[...] = z[:, c:c + B_VW].astype(BF16)
    g_lr = _dot_nt(y, wlr_ref[...])
    gate = _dot(g_lr.astype(BF16), wgate_ref[...]) + bgate_ref[...]
    la_ref[...] = _log_sigmoid(gate) * (1.0 / B_GATE_TEMP)


class _KVStack:
    def __init__(self, layer, n_layers, keep, previous=None):
        self.layer, self.n_layers, self.keep, self.previous = layer, n_layers, keep, previous


def _kv_out(m, width, tm, batch, stack):
    if batch is None:
        spec = pl.BlockSpec((tm, width), lambda i: (i, 0))
        return [spec] * 4, [jax.ShapeDtypeStruct((m, width), dt) for dt in (F32, F32, BF16, BF16)], None, 0
    t = m // batch
    assert t % tm == 0 and stack.keep % tm == 0 and stack.keep <= t
    n_tiles = t // tm
    copy_spec = pl.BlockSpec((None, width, tm), lambda i: (i // n_tiles, 0, i % n_tiles))
    copy_shape = jax.ShapeDtypeStruct((batch, width, t), BF16)
    slots, first_slot, own_slot = ((stack.n_layers, 0, stack.layer) if stack.previous is None
                                   else (1, stack.layer, 0))
    if stack.keep == t:
        f32_spec = pl.BlockSpec((slots, None, width, tm), lambda i: (first_slot, i // n_tiles, 0, i % n_tiles))
        kept_tail = None
    else:
        f32_spec = pl.BlockSpec((slots, None, width, stack.keep), lambda i: (first_slot, i // n_tiles, 0, 0))
        kept_tail = (n_tiles, stack.keep // tm)
    f32_shape = jax.ShapeDtypeStruct((stack.n_layers, batch, width, stack.keep), F32)
    return ([f32_spec, f32_spec, copy_spec, copy_spec], [f32_shape, f32_shape, copy_shape, copy_shape],
            kept_tail, own_slot)


def _aliased_stack(stack, n_inputs, first_output):
    if stack is None or stack.previous is None:
        return [], [], {}
    prev = list(stack.previous)
    specs = [pl.BlockSpec(memory_space=pl.ANY)] * len(prev)
    return prev, specs, {n_inputs + j: first_output + j for j in range(len(prev))}


def _proj_ab(x, g, w_main, w_kv, w_lr, w_gate, b_gate, tm, batch=None, stack=None):
    m, d = x.shape
    row = lambda n: pl.BlockSpec((tm, n), lambda i: (i, 0))
    params = (g, w_main, w_kv, w_lr, w_gate, b_gate)
    kv_specs, kv_shapes, kept_tail, own_slot = _kv_out(m, A_WIDTH, tm, batch, stack)
    assert kept_tail is None
    prev, prev_specs, aliases = _aliased_stack(stack, 1 + len(params), 1)
    rest = [(B_KW, BF16), (B_KW, BF16), (B_VW, BF16), (B_VW, BF16), (B_KW, F32)]
    return pl.pallas_call(
        functools.partial(_proj_ab_kernel, feature_major=batch is not None, n_aliased=len(prev),
                          own_slot=own_slot),
        grid=(m // tm,),
        in_specs=[row(d)] + [_resident(a) for a in params] + prev_specs,
        out_specs=[row(A_WIDTH)] + kv_specs + [row(n) for n, _ in rest],
        out_shape=([jax.ShapeDtypeStruct((m, A_WIDTH), BF16)] + kv_shapes
                   + [jax.ShapeDtypeStruct((m, n), dt) for n, dt in rest]),
        input_output_aliases=aliases,
        compiler_params=_params("parallel"),
        name="proj_ab",
    )(x, *[_operand(a) for a in params], *prev)


def _proj_c_kernel(x_ref, g_ref, wq_ref, wkv_ref, *refs, feature_major, n_aliased, kept_tail, own_slot):
    q_ref, k_ref, v_ref, kb_ref, vb_ref = refs[n_aliased:]
    y = _rms(x_ref[...], g_ref[...]).astype(BF16)
    _emit_kv(y, wkv_ref, k_ref, v_ref, kb_ref, vb_ref, feature_major, kept_tail, own_slot)
    q_ref[...] = (_dot(y, wq_ref[...]) * (HEAD_DIM ** -0.5 * LOG2E)).astype(BF16)


def _proj_c(x, g, w_q, w_kv, tm, batch=None, stack=None):
    m, d = x.shape
    row = lambda n: pl.BlockSpec((tm, n), lambda i: (i, 0))
    params = (g, w_q, w_kv)
    kv_specs, kv_shapes, kept_tail, own_slot = _kv_out(m, C_WIDTH, tm, batch, stack)
    prev, prev_specs, aliases = _aliased_stack(stack, 1 + len(params), 1)
    return pl.pallas_call(
        functools.partial(_proj_c_kernel, feature_major=batch is not None, n_aliased=len(prev),
                          kept_tail=kept_tail, own_slot=own_slot),
        grid=(m // tm,),
        in_specs=[row(d)] + [_resident(a) for a in params] + prev_specs,
        out_specs=[row(C_WIDTH)] + kv_specs,
        out_shape=[jax.ShapeDtypeStruct((m, C_WIDTH), BF16)] + kv_shapes,
        input_output_aliases=aliases,
        compiler_params=_params("arbitrary" if kept_tail else "parallel"),
        name="proj_c",
    )(x, *[_operand(a) for a in params], *prev)


def _layer_tail_rows(x_ref, mixer_refs, param_refs, o_ref, gla_merge, final_norm):
    if gla_merge:
        oa_ref, ob_ref, r_ref = mixer_refs
        ggla_ref, wout_ref, *param_refs = param_refs
        ob = ob_ref[...].astype(F32)
        parts = []
        for h in range(B_HEADS):
            seg = ob[:, h * B_DV:(h + 1) * B_DV]
            parts.append(seg * lax.rsqrt(jnp.mean(seg * seg, axis=-1, keepdims=True) + EPS))
        r = r_ref[...].astype(F32)
        obn = jnp.concatenate(parts, axis=-1) * ggla_ref[...] * (r * jax.nn.sigmoid(r))
        mix = _dot(oa_ref[...], wout_ref[:A_WIDTH, :]) + _dot(obn.astype(BF16), wout_ref[A_WIDTH:, :])
    else:
        (oc_ref,) = mixer_refs
        wout_ref, *param_refs = param_refs
        mix = _dot(oc_ref[...], wout_ref[...])
    gffn_ref, wg_ref, wu_ref, wd_ref, *param_refs = param_refs
    x = x_ref[...] + mix
    y = _rms(x, gffn_ref[...]).astype(BF16)
    h = _dot(y, wg_ref[...])
    u = _dot(y, wu_ref[...])
    a = (h * jax.nn.sigmoid(h) * u).astype(BF16)
    x = x + _dot(a, wd_ref[...])
    o_ref[...] = _rms(x, param_refs[0][...]) if final_norm else x


def _layer_tail_kernel(*refs, n_mixer, n_params, gla_merge, final_norm, main_steps):
    main, extra = refs[:1 + n_mixer], refs[1 + n_mixer:2 + 2 * n_mixer]
    params = refs[2 + 2 * n_mixer:2 + 2 * n_mixer + n_params]
    o_ref, o_extra_ref = refs[-2:]

    @pl.when(pl.program_id(0) < main_steps)
    def _():
        _layer_tail_rows(main[0], main[1:], params, o_ref, gla_merge, final_norm)

    @pl.when(pl.program_id(0) == main_steps)
    def _():
        _layer_tail_rows(extra[0], extra[1:], params, o_extra_ref, gla_merge, final_norm)


def _layer_tail(x, mixer_out, x_extra, mixer_out_extra, mixer_params, g_ffn, wg, wu, wd, g_fin, tm):
    m, d = x.shape
    main_steps = m // tm
    row = lambda a: pl.BlockSpec((tm, a.shape[1]), lambda i: (jnp.minimum(i, main_steps - 1), 0))
    whole = lambda a: pl.BlockSpec(a.shape, lambda i: (0, 0))
    params = [*mixer_params, g_ffn, wg, wu, wd] + ([] if g_fin is None else [g_fin])
    return pl.pallas_call(
        functools.partial(_layer_tail_kernel, n_mixer=len(mixer_out), n_params=len(params),
                          gla_merge=len(mixer_out) == 3, final_norm=g_fin is not None, main_steps=main_steps),
        grid=(main_steps + 1,),
        in_specs=([row(x)] + [row(a) for a in mixer_out] + [whole(x_extra)]
                  + [whole(a) for a in mixer_out_extra] + [_resident(a) for a in params]),
        out_specs=[row(x), whole(x_extra)],
        out_shape=[jax.ShapeDtypeStruct((m, d), F32), jax.ShapeDtypeStruct(x_extra.shape, F32)],
        compiler_params=_params("arbitrary"),
        name="layer_tail",
    )(x, *mixer_out, x_extra, *mixer_out_extra, *[_operand(a) for a in params])


def _interleave(stage_generators):
    results = [None] * len(stage_generators)
    live = list(range(len(stage_generators)))
    while live:
        for i in list(live):
            try:
                next(stage_generators[i])
            except StopIteration as done:
                results[i] = done.value
                live.remove(i)
    return results


def _sb_core(q_heads, lo, first_kv, first_mask, earlier_kv, n_earlier, acc_ref, c_ref, o_ref, n_pairs,
             companions=(), on_companions=None):
    tk = first_kv.n_keys
    heads = range(2 * n_pairs)
    later = (lax.broadcasted_iota(jnp.int32, (tk, tk), 0)
             > lax.broadcasted_iota(jnp.int32, (tk, tk), 1)).astype(BF16)
    c_ref[...] = jnp.zeros_like(c_ref)

    tq = q_heads[0].shape[0]

    def row_groups(kv, mask):
        if mask is None or tq != tk or tq % (2 * LANES):
            return [(slice(None), kv, mask, later)]
        half = tq // 2
        return [(slice(0, half), kv.first(half), mask[:half, :half], later[:half, :half]),
                (slice(half, tq), kv, mask[half:, :], later)]

    def block(kv, mask):
        groups = row_groups(kv, mask)
        chains = [(h, g) for h in heads for g in range(len(groups))]
        z = [groups[g][1].scores(q_heads[h][groups[g][0], :], h // 2) for h, g in chains]
        yield
        log_beta, drop, after = [], [], []
        for i, (h, g) in enumerate(chains):
            d = jnp.maximum(z[i], 0.0) + jnp.log(1.0 + jnp.exp(-jnp.abs(z[i])))
            log_beta.append(z[i] - d)
            drop.append(d if mask is None else jnp.where(groups[g][2], d, 0.0))
        yield
        for i, (h, g) in enumerate(chains):
            hi, lo_part = _split_bf16(drop[i])
            tri = groups[g][3]
            after.append(_dot(hi, tri) + _dot(lo_part, tri))
        yield
        pv = {}
        for i, (h, g) in enumerate(chains):
            rows, kv_g = groups[g][0], groups[g][1]
            c = c_ref[h, rows, :]
            w = jnp.exp(log_beta[i] - after[i] - c)
            if mask is not None:
                w = jnp.where(groups[g][2], w, 0.0)
            c_ref[h, rows, :] = c + after[i][:, 0:1] + drop[i][:, 0:1]
            pv[h, g] = kv_g.weighted(w.astype(BF16), kv_g.values(h // 2))
        yield
        per_head = [pv[h, 0] if len(groups) == 1 else jnp.concatenate([pv[h, g] for g in range(len(groups))], 0)
                    for h in heads]
        out = [jnp.where(lo, per_head[2 * p], per_head[2 * p + 1]) for p in range(n_pairs)]
        return out[0] if n_pairs == 1 else jnp.concatenate(out, axis=-1)

    def all_dead():
        return jnp.min(c_ref[...]) > SB_DEAD

    first, *companion_results = _interleave([block(first_kv, first_mask), *companions])
    acc_ref[...] = first
    if companions:
        on_companions(companion_results)

    def cond(carry):
        n, dead = carry
        return (n < n_earlier) & jnp.logical_not(dead)

    def body(carry):
        n, _ = carry
        acc_ref[...] += _interleave([block(earlier_kv(n), None)])[0]
        return n + 1, all_dead()

    lax.while_loop(cond, body, (jnp.int32(0), jnp.asarray(False)))
    o_ref[...] = acc_ref[...].astype(o_ref.dtype)


def _mixer_ab_prompt_kernel(q_ref, k_ref, v_ref, qb_ref, kb_ref, vb_ref, la_ref, *refs,
                            tb, n_pairs, blocks_per_step, n_riders):
    o_ref, ob_ref, s_out_ref = refs[n_riders:n_riders + 3]
    acc_ref, c_ref, carry_ref = refs[3 + 2 * n_riders:]
    _cast_riders(list(zip(refs[:n_riders], refs[n_riders + 3:3 + 2 * n_riders])))
    pairs = range(B_HEADS // 2)
    strictly_earlier = (lax.broadcasted_iota(jnp.int32, (tb, tb), 1)
                        < lax.broadcasted_iota(jnp.int32, (tb, tb), 0))

    def kv_block(j):
        keys = pl.ds(pl.multiple_of(j * tb, tb), tb)
        return _KV(k_ref[:, keys], v_ref[:, keys], True)

    def keep_states(states):
        for p in pairs:
            carry_ref[2 * p], carry_ref[2 * p + 1] = states[p]
            s_out_ref[2 * p] = states[p][0][:B_DK, :]
            s_out_ref[2 * p + 1] = states[p][1][B_DK:, :]

    @pl.when(pl.program_id(1) == 0)
    def _():
        carry_ref[...] = jnp.zeros_like(carry_ref)

    def one_block(s, carry):
        qi = pl.program_id(1) * blocks_per_step + s
        rows = pl.ds(pl.multiple_of(s * tb, tb), tb)
        gla = [_gla_pair_tile(qb_ref[rows, _pair_cols(p)], kb_ref[rows, _pair_cols(p)],
                              vb_ref[rows, _pair_v_cols(p)], la_ref[rows, _pair_cols(p)],
                              [carry_ref[2 * p], carry_ref[2 * p + 1]],
                              _gla_rows_emit(ob_ref.at[rows, :], p)) for p in pairs]
        q_heads, lo = _split_heads(q_ref.at[rows, :], n_pairs, tb)
        _sb_core(q_heads, lo, kv_block(qi), strictly_earlier, lambda n: kv_block(qi - 1 - n), qi,
                 acc_ref, c_ref, o_ref.at[rows, :], n_pairs, companions=gla, on_companions=keep_states)
        return carry

    lax.fori_loop(0, blocks_per_step, one_block, 0)


def _sb_sample_kernel(q_ref, kn_ref, vn_ref, kc_ref, vc_ref, o_ref, acc_ref, c_ref, kpad_ref, vpad_ref,
                      *, ts, tk, n_cache_blocks, n_pairs):
    q_heads, lo = _split_heads(q_ref, n_pairs, ts)
    kpad_ref[...] = jnp.zeros_like(kpad_ref)
    vpad_ref[...] = jnp.zeros_like(vpad_ref)
    kpad_ref[:ts, :] = kn_ref[...]
    vpad_ref[:ts, :] = vn_ref[...]

    def cache_block(n):
        keys = pl.ds(pl.multiple_of((n_cache_blocks - 1 - n) * tk, tk), tk)
        return _KV(kc_ref[:, keys].astype(BF16), vc_ref[:, keys].astype(BF16), True)

    strictly_earlier = (lax.broadcasted_iota(jnp.int32, (ts, tk), 1)
                        < lax.broadcasted_iota(jnp.int32, (ts, tk), 0))
    _sb_core(q_heads, lo, _KV(kpad_ref[...], vpad_ref[...], False), strictly_earlier, cache_block,
             n_cache_blocks, acc_ref, c_ref, o_ref, n_pairs)


def _mixer_ab_prompt(q, k, v, qb, kb, vb, la, ride_stacks, layer):
    b, t, _ = q.shape
    tb = ATT_BLOCK
    assert t % tb == 0 and tb % GLA_CHUNK == 0 and tb <= GLA_MAX_TILE
    n_pairs = A_WIDTH // LANES
    per_step = math.gcd(t // tb, MIXER_BLOCKS_PER_STEP)
    rows = lambda width: pl.BlockSpec((None, per_step * tb, width), lambda bi, qi: (bi, qi, 0))
    kv_spec = pl.BlockSpec((None, A_WIDTH, t), lambda bi, qi: (bi, 0, 0))
    state = (B_HEADS, B_DK, B_DV)
    grid = (b, t // (per_step * tb))
    ride_in, ride_out, ride_shapes = _Riders(ride_stacks, layer, grid).specs()
    return pl.pallas_call(
        functools.partial(_mixer_ab_prompt_kernel, tb=tb, n_pairs=n_pairs, blocks_per_step=per_step,
                          n_riders=len(ride_stacks)),
        grid=grid,
        in_specs=[rows(A_WIDTH), kv_spec, kv_spec, rows(B_KW), rows(B_KW), rows(B_VW), rows(B_KW)] + ride_in,
        out_specs=[rows(A_WIDTH), rows(B_VW),
                   pl.BlockSpec((None, *state), lambda bi, qi: (bi, 0, 0, 0))] + ride_out,
        out_shape=[jax.ShapeDtypeStruct(q.shape, BF16), jax.ShapeDtypeStruct(vb.shape, BF16),
                   jax.ShapeDtypeStruct((b, *state), F32)] + ride_shapes,
        scratch_shapes=[pltpu.VMEM((tb, A_WIDTH), F32), pltpu.VMEM((2 * n_pairs, tb, 1), F32),
                        pltpu.VMEM((B_HEADS, LANES, B_DV), F32)],
        compiler_params=_params("arbitrary", "arbitrary"),
        name="mixer_ab",
    )(q, k, v, qb, kb, vb, la, *ride_stacks)


def _sb_attention_sample(q, k_new, v_new, k_cache, v_cache, layer):
    b, ts, _ = q.shape
    past = k_cache.shape[3]
    tk = ATT_BLOCK
    assert past % tk == 0 and ts <= tk
    n_pairs = SB_PAIRS_PER_STEP
    w = n_pairs * LANES
    new_spec = pl.BlockSpec((None, ts, w), lambda bi, hp: (bi, 0, hp))
    cache_spec = pl.BlockSpec((None, None, w, past), lambda bi, hp: (layer, bi, hp, 0))
    return pl.pallas_call(
        functools.partial(_sb_sample_kernel, ts=ts, tk=tk, n_cache_blocks=past // tk, n_pairs=n_pairs),
        grid=(b, A_WIDTH // w),
        in_specs=[new_spec, new_spec, new_spec, cache_spec, cache_spec],
        out_specs=new_spec,
        out_shape=jax.ShapeDtypeStruct(q.shape, BF16),
        scratch_shapes=[pltpu.VMEM((ts, w), F32), pltpu.VMEM((2 * n_pairs, ts, 1), F32),
                        pltpu.VMEM((tk, w), BF16), pltpu.VMEM((tk, w), BF16)],
        compiler_params=_params("parallel", "parallel"),
        name="sb_attention_sample",
    )(q, k_new, v_new, k_cache, v_cache)


def _band_core(q_heads, lo, kv, pens, bias, o_ref, n_pairs, skip_hidden_quarters=False):
    heads = range(2 * n_pairs)
    tq = q_heads[0].shape[0]
    everything = [(slice(None), [(kv[i], i, slice(None)) for i in range(len(kv))])]
    groups = _band_quarters(kv, tq) if skip_hidden_quarters else everything
    chains = [(h, g) for h in heads for g in range(len(groups))]
    z = [[piece.scores(q_heads[h][groups[g][0], :], h // 2) + bias(h, i)[groups[g][0], cols]
          for piece, i, cols in groups[g][1]] for h, g in chains]
    acc = {}
    for c, (h, g) in enumerate(chains):
        pieces = groups[g][1]
        m = None
        for j, (_, i, _) in enumerate(pieces):
            mi = jnp.max(z[c][j], axis=-1, keepdims=True)
            if pens[i] is not None:
                mi = mi + pens[i]
            m = mi if m is None else jnp.maximum(m, mi)
        a = None
        for j, (piece, i, _) in enumerate(pieces):
            shift = m if pens[i] is None else m - pens[i]
            p = jnp.exp2(z[c][j] - shift).astype(BF16)
            v = piece.values(h // 2)
            ones = jnp.ones_like(v)
            first = piece.head_lanes()
            v = jnp.where(first, v, ones) if h % 2 == 0 else jnp.where(first, ones, v)
            pv = piece.weighted(p, v)
            a = pv if a is None else a + pv
        acc[h, g] = a
    per_head = [acc[h, 0] if len(groups) == 1 else jnp.concatenate([acc[h, g] for g in range(len(groups))], 0)
                for h in heads]
    for p in range(n_pairs):
        a0, a1 = per_head[2 * p], per_head[2 * p + 1]
        o_ref[:, _pair_cols(p)] = jnp.where(lo, a0 / pltpu.roll(a0, HEAD_DIM, axis=1),
                                            a1 / pltpu.roll(a1, HEAD_DIM, axis=1)).astype(o_ref.dtype)


def _band_quarters(kv, tq):
    own, back1, back2 = kv
    half = tq // 2
    assert own.n_keys == tq and half % CHUNK == 0 and 2 * (tq // CHUNK) >= C_LEFT_CHUNKS
    whole = slice(None)
    return [(slice(0, half), [(own.first(half), 0, slice(0, half)), (back1, 1, whole), (back2, 2, whole)]),
            (slice(half, tq), [(own, 0, whole), (back1, 1, whole), (back2.last(half), 2, slice(half, tq))])]


def _band_prompt_kernel(q_ref, k_ref, v_ref, bias_ref, *refs, tq, tk, n_pairs, blocks_per_step):
    n_riders = (len(refs) - 1) // 2
    o_ref = refs[n_riders]
    _cast_riders(list(zip(refs[:n_riders], refs[n_riders + 1:])))

    def one_block(s, carry):
        qi = pl.program_id(2) * blocks_per_step + s
        rows = pl.ds(pl.multiple_of(s * tq, tq), tq)
        q_heads, lo = _split_heads(q_ref.at[rows, :], n_pairs, tq)
        kv, pens = [], []
        for dj in range(3):
            j = qi - dj
            pens.append(None if dj == 0 else jnp.where(j >= 0, 0.0, NEG_BIG).astype(F32))
            keys = pl.ds(pl.multiple_of(jnp.maximum(j, 0) * tk, tk), tk)
            kv.append(_KV(k_ref[:, keys], v_ref[:, keys], True))
        _band_core(q_heads, lo, kv, pens, lambda h, i: bias_ref.at[h, i], o_ref.at[rows, :], n_pairs,
                   skip_hidden_quarters=True)
        return carry

    lax.fori_loop(0, blocks_per_step, one_block, 0)


def _band_sample_kernel(q_ref, kn_ref, vn_ref, kc_ref, vc_ref, bias_ref, o_ref, kpad_ref, vpad_ref,
                        *, ts, tk, n_cache_blocks, n_pairs):
    q_heads, lo = _split_heads(q_ref, n_pairs, ts)
    kpad_ref[...] = jnp.zeros_like(kpad_ref)
    vpad_ref[...] = jnp.zeros_like(vpad_ref)
    kpad_ref[:ts, :] = kn_ref[...]
    vpad_ref[:ts, :] = vn_ref[...]
    kv = [_KV(kpad_ref[...], vpad_ref[...], False)]
    for dj in range(1, n_cache_blocks + 1):
        keys = slice((n_cache_blocks - dj) * tk, (n_cache_blocks - dj + 1) * tk)
        kv.append(_KV(kc_ref[:, keys].astype(BF16), vc_ref[:, keys].astype(BF16), True))
    is_new_key = lax.broadcasted_iota(jnp.int32, (ts, tk), 1) < ts

    def bias(h, i):
        return jnp.where(is_new_key, bias_ref[h, 0], NEG_BIG) if i == 0 else bias_ref[h, i]

    _band_core(q_heads, lo, kv, [None] * len(kv), bias, o_ref, n_pairs)


def _band_attention_prompt(q, k, v, bias, ride_stacks, layer):
    b, t, _ = q.shape
    tq = tk = ATT_BLOCK
    assert t % tk == 0
    n_pairs = BAND_PAIRS_PER_STEP
    w = n_pairs * LANES
    per_step = math.gcd(t // tq, BAND_BLOCKS_PER_STEP)
    kern = functools.partial(_band_prompt_kernel, tq=tq, tk=tk, n_pairs=n_pairs, blocks_per_step=per_step)
    kv_spec = pl.BlockSpec((None, w, t), lambda bi, hp, qi: (bi, hp, 0))
    q_spec = pl.BlockSpec((None, per_step * tq, w), lambda bi, hp, qi: (bi, qi, hp))
    bias_spec = pl.BlockSpec((2 * n_pairs, 3, tq, tk), lambda bi, hp, qi: (hp, 0, 0, 0))
    grid = (b, C_WIDTH // w, t // (per_step * tq))
    ride_in, ride_out, ride_shapes = _Riders(ride_stacks, layer, grid).specs()
    return pl.pallas_call(
        kern,
        grid=grid,
        in_specs=[q_spec, kv_spec, kv_spec, bias_spec] + ride_in,
        out_specs=[q_spec] + ride_out,
        out_shape=[jax.ShapeDtypeStruct(q.shape, BF16)] + ride_shapes,
        compiler_params=_params("arbitrary", "arbitrary", "arbitrary"),
        name="band_attention",
    )(q, k, v, bias, *ride_stacks)


def _band_attention_sample(q, k_new, v_new, k_cache, v_cache, layer, bias):
    b, ts, _ = q.shape
    wc = k_cache.shape[3]
    tk = ATT_BLOCK
    n_cache_blocks = min(wc // tk, 2)
    assert wc % (n_cache_blocks * tk) == 0 and ts <= tk
    n_pairs = C_WIDTH // LANES
    w = n_pairs * LANES
    kern = functools.partial(_band_sample_kernel, ts=ts, tk=tk, n_cache_blocks=n_cache_blocks,
                             n_pairs=n_pairs)
    new_spec = pl.BlockSpec((None, ts, w), lambda bi, hp: (bi, 0, hp))
    cache_rows = n_cache_blocks * tk
    cache_spec = pl.BlockSpec((None, None, w, cache_rows),
                              lambda bi, hp: (layer, bi, hp, wc // cache_rows - 1))
    bias_spec = pl.BlockSpec((2 * n_pairs, 1 + n_cache_blocks, ts, tk), lambda bi, hp: (hp, 0, 0, 0))
    return pl.pallas_call(
        kern,
        grid=(b, C_WIDTH // w),
        in_specs=[new_spec, new_spec, new_spec, cache_spec, cache_spec, bias_spec],
        out_specs=new_spec,
        out_shape=jax.ShapeDtypeStruct(q.shape, BF16),
        scratch_shapes=[pltpu.VMEM((tk, w), BF16), pltpu.VMEM((tk, w), BF16)],
        compiler_params=_params("parallel", "parallel"),
        name="band_attention_sample",
    )(q, k_new, v_new, k_cache, v_cache, bias)


def _band_bias_kernel(g_ref, o_ref):
    rows = tk = ATT_BLOCK
    q_chunk = lax.broadcasted_iota(jnp.int32, (rows, tk), 0) // CHUNK
    k_chunk = lax.broadcasted_iota(jnp.int32, (rows, tk), 1) // CHUNK
    for dj in range(3):
        diff = dj * (tk // CHUNK) + q_chunk - k_chunk
        seen = (diff >= 0) & (diff <= C_LEFT_CHUNKS)
        for h in range(g_ref.shape[0]):
            g = jnp.broadcast_to(g_ref[h, dj], (rows, 2 * tk))
            tile = pltpu.roll(g, 0, axis=1, stride=1, stride_axis=0)[:, :tk]
            o_ref[h, dj] = jnp.where(seen, tile * LOG2E, NEG_BIG)


def _band_bias(rel_table):
    rows = tk = ATT_BLOCK
    c = jnp.arange(2 * tk, dtype=jnp.int32)
    u = jnp.where(c <= tk, -c, 2 * tk - c)
    idx = jnp.clip(jnp.arange(3, dtype=jnp.int32)[:, None] * tk + u[None, :], REL_MIN, REL_MAX) - REL_MIN
    g = rel_table[:, idx].astype(F32).reshape(C_HEADS, 3, 1, 2 * tk)
    return pl.pallas_call(
        _band_bias_kernel,
        grid=(C_HEADS // BIAS_HEADS_PER_STEP,),
        in_specs=[pl.BlockSpec((BIAS_HEADS_PER_STEP, 3, 1, 2 * tk), lambda h: (h, 0, 0, 0))],
        out_specs=pl.BlockSpec((BIAS_HEADS_PER_STEP, 3, rows, tk), lambda h: (h, 0, 0, 0)),
        out_shape=jax.ShapeDtypeStruct((C_HEADS, 3, rows, tk), F32),
        compiler_params=_params("parallel"),
        name="band_bias",
    )(g)


def _gla_pair_tile(q, k, v, la, states, emit):
    L = GLA_CHUNK
    chunks = range(q.shape[0] // L)
    row = lax.broadcasted_iota(jnp.int32, (L, L), 0)
    colm = lax.broadcasted_iota(jnp.int32, (L, L), 1)
    tri = (colm <= row).astype(BF16)
    causal = colm <= row
    lane = lax.broadcasted_iota(jnp.int32, (L, LANES), 1)
    sub = lax.broadcasted_iota(jnp.int32, (LANES, B_DV), 0)
    mine = [(lane >= h * B_DK) & (lane < (h + 1) * B_DK) for h in range(2)]
    mine_rows = [(sub >= h * B_DK) & (sub < (h + 1) * B_DK) for h in range(2)]
    rows = [slice(c * L, (c + 1) * L) for c in chunks]

    b = []
    for r in rows:
        g_hi, g_lo = _split_bf16(la[r, :])
        b.append(_dot(tri, g_hi) + _dot(tri, g_lo))
    yield
    qg, qg_h, kg, kd_t, decay = [], [], [], [], []
    for c, r in zip(chunks, rows):
        q_c, k_c = q[r, :].astype(F32), k[r, :].astype(F32)
        qg_c = q_c * (B_DK ** -0.5) * jnp.exp(b[c])
        qg.append(qg_c.astype(BF16))
        qg_h.append([jnp.where(mine[h], qg_c, 0.0).astype(BF16) for h in range(2)])
        kg.append((k_c * jnp.exp(-b[c])).astype(BF16))
        b_t = b[c].T
        b_last = b_t[:, L - 1:L]
        kd_t.append((k_c.T * jnp.exp(b_last - b_t)).astype(BF16))
        decay.append(jnp.exp(b_last))
    yield
    att =[[jnp.where(causal, _dot_nt(qg_h[c][h], kg[c]), 0.0).astype(BF16) for h in range(2)]
           for c in chunks]
    yield
    o_intra, own = [], []
    for c, r in zip(chunks, rows):
        v_h = [v[r, h * B_DV:(h + 1) * B_DV].astype(BF16) for h in range(2)]
        o_intra.append([_dot(att[c][h], v_h[h]) for h in range(2)])
        own.append([jnp.where(mine_rows[h], _dot(kd_t[c], v_h[h]), 0.0) for h in range(2)])

    yield
    states = list(states)
    start = []
    for c in chunks:
        start.append([s.astype(BF16) for s in states])
        states = [decay[c] * states[h] + own[c][h] for h in range(2)]
    yield
    for c in chunks:
        for h in range(2):
            emit(c, h, o_intra[c][h] + _dot(qg[c], start[c][h]))
    return states


def _pair_states(s0_ref, p):
    zeros_state = jnp.zeros((B_DK, B_DV), F32)
    return [jnp.concatenate([s0_ref[2 * p], zeros_state], axis=0),
            jnp.concatenate([zeros_state, s0_ref[2 * p + 1]], axis=0)]


def _pair_v_cols(p):
    return slice(2 * p * B_DV, 2 * (p + 1) * B_DV)


def _gla_rows_emit(o_ref, p):
    def emit(c, h, o):
        o_ref[c * GLA_CHUNK:(c + 1) * GLA_CHUNK, (2 * p + h) * B_DV:(2 * p + h + 1) * B_DV] = o.astype(o_ref.dtype)
    return emit


def _gla_kernel(q_ref, k_ref, v_ref, la_ref, s0_ref, o_ref, s_out_ref):
    pairs = range(B_HEADS // 2)
    states = _interleave([
        _gla_pair_tile(q_ref[:, _pair_cols(p)], k_ref[:, _pair_cols(p)], v_ref[:, _pair_v_cols(p)],
                       la_ref[:, _pair_cols(p)], _pair_states(s0_ref, p), _gla_rows_emit(o_ref, p))
        for p in pairs])
    for p in pairs:
        s_out_ref[2 * p] = states[p][0][:B_DK, :]
        s_out_ref[2 * p + 1] = states[p][1][B_DK:, :]


def _gla(q, k, v, la, s0):
    b, t, _ = q.shape
    assert t % GLA_CHUNK == 0 and t <= GLA_MAX_TILE
    qk_spec = pl.BlockSpec((None, t, B_KW), lambda bi: (bi, 0, 0))
    v_spec = pl.BlockSpec((None, t, B_VW), lambda bi: (bi, 0, 0))
    s_spec = pl.BlockSpec((None, B_HEADS, B_DK, B_DV), lambda bi: (bi, 0, 0, 0))
    return pl.pallas_call(
        _gla_kernel,
        grid=(b,),
        in_specs=[qk_spec, qk_spec, v_spec, qk_spec, s_spec],
        out_specs=[v_spec, s_spec],
        out_shape=[jax.ShapeDtypeStruct(v.shape, BF16), jax.ShapeDtypeStruct(s0.shape, F32)],
        compiler_params=_params("parallel"),
        name="gla",
    )(q, k, v, la, s0)


def _pad_rows(x, n):
    return jnp.pad(x, ((0, 0), (0, n - x.shape[1]), (0, 0)))


def _heads_last(x, heads):
    n, b, _, s = x.shape
    return jnp.transpose(x.reshape(n, b, heads, HEAD_DIM, s), (0, 1, 4, 2, 3))


def _feature_major(cache):
    n, b, s, heads, hd = cache.shape
    return jnp.transpose(cache, (0, 1, 3, 4, 2)).reshape(n, b, heads * hd, s)


def _row_tile(m):
    for tm in (512, 256, 128, 64, 32, 16, 8):
        if m % tm == 0:
            return tm
    raise ValueError(f"token count {m} is not a multiple of 8")


def kernel(x_prompt, x_sample, cache_a_k, cache_a_v, state_b, cache_c_k, cache_c_v, norm_mix_g, norm_ffn_g, w_in_ab, w_gate_b, b_gate_b, norm_gla_g, w_out_ab, w_qkv_c, rel_bias_c, w_out_c, w_ffn_gate, w_ffn_up, w_ffn_down, norm_final_g):
    bp, tp, d = x_prompt.shape
    bs, ts, _ = x_sample.shape
    depth = norm_mix_g.shape[0]
    past = cache_a_k.shape[2]
    wc = cache_c_k.shape[2]
    assert tp % ATT_BLOCK == 0 and past % ATT_BLOCK == 0 and wc % ATT_BLOCK == 0
    assert ts <= GLA_CHUNK and ts % 8 == 0
    mp, ms = bp * tp, bs * ts
    tmp, tms = _row_tile(tp), _row_tile(ms)
    xp = x_prompt.reshape(mp, d)
    xs = x_sample.reshape(ms, d)
    row2 = lambda a: a.reshape(1, -1)

    a_ks, a_vs, b_sp, b_ss, c_ks, c_vs = [], [], [], [], [], []
    a_kv_prompt = c_kv_prompt = None
    n_ab, n_c = (depth + 1) // 2, depth // 2
    keep = min(C_LEFT_CHUNKS * CHUNK, tp)

    kv0, kv1 = A_WIDTH, 3 * A_WIDTH
    o = 3 * A_WIDTH + 2 * B_KW + B_VW
    w_in_t = jnp.swapaxes(w_in_ab, 1, 2)
    w_main_all = jnp.concatenate([w_in_t[:, :kv0], w_in_t[:, kv1:o], w_in_t[:, o + B_GATE_RANK:]],
                                 axis=1).astype(BF16)
    w_kv_ab_t_all = w_in_t[:, kv0:kv1].astype(BF16)
    w_lr_all = jnp.pad(w_in_t[:, o:o + B_GATE_RANK],
                       ((0, 0), (0, LANES - B_GATE_RANK), (0, 0))).astype(BF16)
    w_gate_all = jnp.pad(w_gate_b, ((0, 0), (0, LANES - B_GATE_RANK), (0, 0))).astype(BF16)
    w_out_ab_all = w_out_ab.astype(BF16)
    w_q_all = w_qkv_c[:, :, :C_WIDTH].astype(BF16)
    w_kv_c_t_all = jnp.swapaxes(w_qkv_c[:, :, C_WIDTH:], 1, 2).astype(BF16)
    w_out_c_all = w_out_c.astype(BF16)
    cache_a_k_fm, cache_a_v_fm, cache_c_k_fm, cache_c_v_fm = (
        _feature_major(c) for c in (cache_a_k, cache_a_v, cache_c_k, cache_c_v))

    for layer in range(depth):
        i = layer // 2
        g_mix = row2(norm_mix_g[layer])
        ffn_f32 = (w_ffn_gate, w_ffn_up, w_ffn_down)
        ffn_ends = (row2(norm_ffn_g[layer]), row2(norm_final_g) if layer == depth - 1 else None)
        if layer % 2 == 0:
            w_main, w_kv_t, w_lr, w_gate, w_out = (
                _Slab(w, i) for w in (w_main_all, w_kv_ab_t_all, w_lr_all, w_gate_all, w_out_ab_all))
            b_gate = row2(b_gate_b[i])
            g_gla = row2(norm_gla_g[i])

            qa, ka, va, kab, vab, qb, kb, vb, r, la = _proj_ab(
                xp, g_mix, w_main, w_kv_t, w_lr, w_gate, b_gate, tmp, batch=bp,
                stack=_KVStack(i, n_ab, tp, a_kv_prompt))
            a_kv_prompt = (ka, va)
            sh = lambda a: a.reshape(bp, tp, -1)
            oa, ob, sbp, *ffn_w = _mixer_ab_prompt(sh(qa), kab, vab, sh(qb), sh(kb), sh(vb), sh(la),
                                                   ffn_f32, layer)
            ffn = (ffn_ends[0], *ffn_w, ffn_ends[1])
            mixed_p = (oa.reshape(mp, -1), ob.reshape(mp, -1), r)
            b_sp.append(sbp)

            qa, ka, va, kab, vab, qb, kb, vb, r, la = _proj_ab(
                xs, g_mix, w_main, w_kv_t, w_lr, w_gate, b_gate, tms)
            sh = lambda a: a.reshape(bs, ts, -1)
            oa = _sb_attention_sample(sh(qa), sh(kab), sh(vab), cache_a_k_fm, cache_a_v_fm, i)
            pad_t = lambda a: _pad_rows(sh(a), GLA_CHUNK)
            ob, sbs = _gla(pad_t(qb), pad_t(kb), pad_t(vb), pad_t(la), state_b[i])
            mixed_s = (oa.reshape(ms, -1), ob[:, :ts].reshape(ms, -1), r)
            xp, xs = _layer_tail(xp, mixed_p, xs, mixed_s, (g_gla, w_out), *ffn, tmp)
            a_ks.append(ka.reshape(bs, ts, A_HEADS, HEAD_DIM))
            a_vs.append(va.reshape(bs, ts, A_HEADS, HEAD_DIM))
            b_ss.append(sbs)
        else:
            w_q, w_kv_t, w_out = (_Slab(w, i) for w in (w_q_all, w_kv_c_t_all, w_out_c_all))

            q, k, v, kb16, vb16 = _proj_c(xp, g_mix, w_q, w_kv_t, tmp, batch=bp,
                                          stack=_KVStack(i, n_c, keep, c_kv_prompt))
            c_kv_prompt = (k, v)
            bias = _band_bias(rel_bias_c[i])
            oc, *ffn_w = _band_attention_prompt(q.reshape(bp, tp, -1), kb16, vb16, bias, ffn_f32, layer)
            ffn = (ffn_ends[0], *ffn_w, ffn_ends[1])
            mixed_p = (oc.reshape(mp, -1),)

            q, k, v, kb16, vb16 = _proj_c(xs, g_mix, w_q, w_kv_t, tms)
            sh = lambda a: a.reshape(bs, ts, -1)
            oc = _band_attention_sample(sh(q), sh(kb16), sh(vb16), cache_c_k_fm, cache_c_v_fm, i, bias)
            xp, xs = _layer_tail(xp, mixed_p, xs, (oc.reshape(ms, -1),), (w_out,), *ffn, tmp)
            c_ks.append(k.reshape(bs, ts, C_HEADS, HEAD_DIM))
            c_vs.append(v.reshape(bs, ts, C_HEADS, HEAD_DIM))

    y_prompt = xp.reshape(bp, tp, d)
    y_sample = xs.reshape(bs, ts, d)
    a_kp, a_vp = (_heads_last(a, A_HEADS) for a in a_kv_prompt)
    c_kp, c_vp = (_heads_last(a, C_HEADS) for a in c_kv_prompt)
    return (y_prompt, y_sample, a_kp, a_vp, jnp.stack(a_ks), jnp.stack(a_vs),
            jnp.stack(b_sp), jnp.stack(b_ss), c_kp, c_vp, jnp.stack(c_ks), jnp.stack(c_vs))
```
